```python
import math
import jax
import jax.numpy as jnp
from jax import lax
import numpy as np

D_MODEL = 1024
BATCH = 8
SEQ = 2048
DEPTH = 1

EPS = 1e-6
GDN_HEADS = 4
GDN_DK = 128
GDN_DV = 128
GDN_CONV = 4
GDN_CHUNK = 64
RET_HEADS = 4
RET_DK = 128
RET_DV = 256
RET_CHUNK = 128
ROPE_BASE = 10000.0
N_GROUPS = 4
EXPERTS_PER_GROUP = 8
N_EXPERTS = N_GROUPS * EXPERTS_PER_GROUP
TOP_K = 2
D_EXPERT = 512

GDN_QK = GDN_HEADS * GDN_DK
GDN_V = GDN_HEADS * GDN_DV
GDN_CONV_DIM = 2 * GDN_QK + GDN_V
RET_QK = RET_HEADS * RET_DK
RET_V = RET_HEADS * RET_DV
SPLITS = (GDN_QK, GDN_QK, GDN_V, GDN_HEADS, GDN_HEADS, GDN_V, RET_QK, RET_QK, RET_V, RET_V, D_MODEL, D_MODEL)
D_IN = sum(SPLITS)

kernel_name = "hybrid_gdn_retention_hmoe"


def rms_norm(x, w=None):
    xf = x.astype(jnp.float32)
    y = xf * lax.rsqrt(jnp.mean(xf * xf, axis=-1, keepdims=True) + EPS)
    if w is not None:
        y = y * w.astype(jnp.float32)
    return y.astype(x.dtype)


def l2_norm(x):
    return x * lax.rsqrt(jnp.sum(x * x, axis=-1, keepdims=True) + EPS)


def to_chunks(a, chunk):
    b, t, h = a.shape[:3]
    a = a.reshape((b, t // chunk, chunk, h) + a.shape[3:])
    return jnp.moveaxis(a, (1, 3), (0, 2))


def from_chunks(a):
    n, b, h, c, d = a.shape
    return jnp.moveaxis(a, (0, 2), (1, 3)).reshape(b, n * c, h, d)


def causal_depthwise_conv(x, w):
    k, c = w.shape
    return lax.conv_general_dilated(
        x, w[:, None, :].astype(x.dtype), window_strides=(1,), padding=[(k - 1, 0)],
        dimension_numbers=("NWC", "WIO", "NWC"), feature_group_count=c)


def gated_delta_rule(q, k, v, g, beta):
    bsz, t, h, dk = q.shape
    dv = v.shape[-1]
    c = GDN_CHUNK
    q = to_chunks(q, c) * (dk ** -0.5)
    k = to_chunks(k, c)
    v = to_chunks(v, c)
    g = to_chunks(g, c)
    beta = to_chunks(beta, c)
    gc = jnp.cumsum(g, axis=-1)
    causal = jnp.tril(jnp.ones((c, c), dtype=bool))
    strict = jnp.tril(jnp.ones((c, c), dtype=bool), -1)
    diff = gc[..., :, None] - gc[..., None, :]
    decay = jnp.where(causal, jnp.exp(jnp.where(causal, diff, 0.0)), 0.0)
    kb = k * beta[..., None]
    lower = jnp.where(strict, jnp.einsum('nbhid,nbhjd->nbhij', kb, k) * decay, 0.0)
    rhs = jnp.concatenate([v * beta[..., None], kb * jnp.exp(gc)[..., None]], axis=-1)
    sol = lax.linalg.triangular_solve(lower, rhs, left_side=True, lower=True, unit_diagonal=True)
    u, w = sol[..., :dv], sol[..., dv:]
    attn = jnp.where(causal, jnp.einsum('nbhid,nbhjd->nbhij', q, k) * decay, 0.0)
    qg = q * jnp.exp(gc)[..., None]
    kd = k * jnp.exp(gc[..., -1:] - gc)[..., None]
    glast = jnp.exp(gc[..., -1])

    def step(state, inp):
        qg_i, w_i, u_i, attn_i, kd_i, gl_i = inp
        v_new = u_i - jnp.einsum('bhck,bhkv->bhcv', w_i, state)
        o = jnp.einsum('bhck,bhkv->bhcv', qg_i, state) + jnp.einsum('bhij,bhjv->bhiv', attn_i, v_new)
        state = state * gl_i[..., None, None] + jnp.einsum('bhck,bhcv->bhkv', kd_i, v_new)
        return state, o

    s0 = jnp.zeros((bsz, h, dk, dv), jnp.float32)
    _, o = lax.scan(step, s0, (qg, w, u, attn, kd, glast))
    return from_chunks(o)


def gdn_branch(q, k, v, a, b, z, conv_w, A_log, dt_bias, norm_w):
    bsz, t, _ = q.shape
    f32 = jnp.float32
    qkv = jax.nn.silu(causal_depthwise_conv(jnp.concatenate([q, k, v], axis=-1), conv_w)).astype(f32)
    q, k, v = jnp.split(qkv, [GDN_QK, 2 * GDN_QK], axis=-1)
    q = l2_norm(q.reshape(bsz, t, GDN_HEADS, GDN_DK))
    k = l2_norm(k.reshape(bsz, t, GDN_HEADS, GDN_DK))
    v = v.reshape(bsz, t, GDN_HEADS, GDN_DV)
    beta = jax.nn.sigmoid(b.astype(f32))
    g = -jnp.exp(A_log.astype(f32)) * jax.nn.softplus(a.astype(f32) + dt_bias.astype(f32))
    o = gated_delta_rule(q, k, v, g, beta)
    o = rms_norm(o, norm_w) * jax.nn.silu(z.astype(f32).reshape(bsz, t, GDN_HEADS, GDN_DV))
    return o.reshape(bsz, t, GDN_V)


def xpos_rotate(x):
    t, d = x.shape[1], x.shape[-1]
    inv_freq = 1.0 / (ROPE_BASE ** jnp.linspace(0.0, 1.0, d // 2, dtype=jnp.float32))
    ang = jnp.arange(t, dtype=jnp.float32)[:, None] * inv_freq[None, :]
    sin = jnp.repeat(jnp.sin(ang), 2, axis=-1)[None, :, None, :]
    cos = jnp.repeat(jnp.cos(ang), 2, axis=-1)[None, :, None, :]
    rot = jnp.stack([-x[..., 1::2], x[..., 0::2]], axis=-1).reshape(x.shape)
    return x * cos + rot * sin


def retention_chunkwise(q, k, v):
    bsz, t, h, dk = q.shape
    dv = v.shape[-1]
    c = RET_CHUNK
    log_gamma = jnp.log(1.0 - 2.0 ** (-5.0 - jnp.arange(h, dtype=jnp.float32)))
    q, k, v = to_chunks(q, c), to_chunks(k, c), to_chunks(v, c)
    idx = jnp.arange(c, dtype=jnp.float32)
    causal = jnp.tril(jnp.ones((c, c), dtype=bool))
    rel = jnp.where(causal, idx[:, None] - idx[None, :], 0.0)
    inner_decay = jnp.where(causal, jnp.exp(rel[None] * log_gamma[:, None, None]), 0.0)
    scores = jnp.einsum('nbhid,nbhjd->nbhij', q, k) * inner_decay
    o_intra = jnp.einsum('nbhij,nbhjv->nbhiv', scores, v)
    k_decay = jnp.exp(log_gamma[:, None] * (c - 1.0 - idx)[None, :])
    q_decay = jnp.exp(log_gamma[:, None] * (idx + 1.0)[None, :])
    kv = jnp.einsum('nbhck,hc,nbhcv->nbhkv', k, k_decay, v)
    chunk_decay = jnp.exp(log_gamma * c)

    def step(state, kv_n):
        return state * chunk_decay[:, None, None] + kv_n, state

    _, state_prev = lax.scan(step, jnp.zeros((bsz, h, dk, dv), jnp.float32), kv)
    o_inter = jnp.einsum('nbhck,hc,nbhkv->nbhcv', q, q_decay, state_prev)
    return from_chunks(o_intra + o_inter)


def retention_branch(q, k, v, gate):
    bsz, t, _ = q.shape
    f32 = jnp.float32
    q = xpos_rotate(q.astype(f32).reshape(bsz, t, RET_HEADS, RET_DK))
    k = xpos_rotate(k.astype(f32).reshape(bsz, t, RET_HEADS, RET_DK)) * (RET_DK ** -0.5)
    v = v.astype(f32).reshape(bsz, t, RET_HEADS, RET_DV)
    o = rms_norm(retention_chunkwise(q, k, v))
    o = o * jax.nn.silu(gate.astype(f32).reshape(bsz, t, RET_HEADS, RET_DV))
    return o.reshape(bsz, t, RET_V)


def hier_moe(x, w_group, b_group, w_expert, b_expert, w_gate, w_up, w_down):
    bsz, t, d = x.shape
    f32 = jnp.float32
    xt = x.reshape(-1, d)
    g_logits = (xt @ w_group).astype(f32) + b_group.astype(f32)
    g_w, g_idx = lax.top_k(jax.nn.softmax(g_logits, axis=-1), 1)
    e_logits = ((xt @ w_expert).astype(f32) + b_expert.astype(f32)).reshape(-1, N_GROUPS, EXPERTS_PER_GROUP)
    e_logits = jnp.take_along_axis(e_logits, g_idx[:, :, None], axis=1)[:, 0]
    e_val, e_idx = lax.top_k(e_logits, TOP_K)
    e_w = jax.nn.softmax(e_val, axis=-1) * g_w
    e_global = g_idx * EXPERTS_PER_GROUP + e_idx
    combine = jnp.sum(jax.nn.one_hot(e_global, N_EXPERTS, dtype=f32) * e_w[..., None], axis=1)
    y = jnp.zeros(xt.shape, f32)
    for e in range(N_EXPERTS):
        hid = jax.nn.silu(xt @ w_gate[e]) * (xt @ w_up[e])
        y = y + combine[:, e:e + 1] * (hid @ w_down[e]).astype(f32)
    return y.astype(x.dtype).reshape(bsz, t, d)


def setup_inputs(seed: int = 0) -> dict:
    key = jax.random.key(seed)
    ks = jax.random.split(key, 20)
    f32 = jnp.float32
    L = DEPTH

    def nrm(k, shape, scale):
        return jax.random.normal(k, shape, f32) * scale

    x = jax.random.normal(ks[0], (BATCH, SEQ, D_MODEL), f32)
    norm_mix_w = 1.0 + nrm(ks[1], (L, D_MODEL), 0.02)
    w_in = nrm(ks[2], (L, D_MODEL, D_IN), D_MODEL ** -0.5)
    conv_w = nrm(ks[3], (L, GDN_CONV, GDN_CONV_DIM), GDN_CONV ** -0.5)
    A_log = jnp.log(jax.random.uniform(ks[4], (L, GDN_HEADS), f32, 1.0, 16.0))
    dt = jnp.exp(jax.random.uniform(ks[5], (L, GDN_HEADS), f32, math.log(1e-3), math.log(1e-1)))
    dt_bias = dt + jnp.log(-jnp.expm1(-dt))
    gdn_norm_w = 1.0 + nrm(ks[6], (L, GDN_DV), 0.02)
    w_up_gdn = nrm(ks[7], (L, GDN_V, D_MODEL), GDN_V ** -0.5)
    w_up_ret = nrm(ks[8], (L, RET_V, D_MODEL), RET_V ** -0.5)
    w_out = nrm(ks[9], (L, D_MODEL, D_MODEL), D_MODEL ** -0.5)
    norm_ffn_w = 1.0 + nrm(ks[10], (L, D_MODEL), 0.02)
    w_group = nrm(ks[11], (L, D_MODEL, N_GROUPS), D_MODEL ** -0.5)
    b_group = nrm(ks[12], (L, N_GROUPS), 0.01)
    w_expert = nrm(ks[13], (L, D_MODEL, N_EXPERTS), D_MODEL ** -0.5)
    b_expert = nrm(ks[14], (L, N_EXPERTS), 0.01)
    w_gate = nrm(ks[15], (L, N_EXPERTS, D_MODEL, D_EXPERT), D_MODEL ** -0.5)
    w_up = nrm(ks[16], (L, N_EXPERTS, D_MODEL, D_EXPERT), D_MODEL ** -0.5)
    w_down = nrm(ks[17], (L, N_EXPERTS, D_EXPERT, D_MODEL), D_EXPERT ** -0.5)
    norm_final_w = 1.0 + nrm(ks[18], (D_MODEL,), 0.02)
    return {"x": x, "norm_mix_w": norm_mix_w, "w_in": w_in, "conv_w": conv_w, "A_log": A_log,
            "dt_bias": dt_bias, "gdn_norm_w": gdn_norm_w, "w_up_gdn": w_up_gdn, "w_up_ret": w_up_ret,
            "w_out": w_out, "norm_ffn_w": norm_ffn_w, "w_group": w_group, "b_group": b_group,
            "w_expert": w_expert, "b_expert": b_expert, "w_gate": w_gate, "w_up": w_up,
            "w_down": w_down, "norm_final_w": norm_final_w}


def reference(x, norm_mix_w, w_in, conv_w, A_log, dt_bias, gdn_norm_w, w_up_gdn, w_up_ret, w_out,
              norm_ffn_w, w_group, b_group, w_expert, b_expert, w_gate, w_up, w_down, norm_final_w):
    dt = x.dtype
    split_points = np.cumsum(np.array(SPLITS))[:-1].tolist()
    h = x
    for l in range(DEPTH):
        u = rms_norm(h, norm_mix_w[l])
        proj = jnp.einsum('btd,de->bte', u, w_in[l])
        (gq, gk, gv, ga, gb, gz, rq, rk, rv, rg, m_a, m_b) = jnp.split(proj, split_points, axis=-1)
        y_a = gdn_branch(gq, gk, gv, ga, gb, gz, conv_w[l], A_log[l], dt_bias[l], gdn_norm_w[l]).astype(dt)
        y_b = retention_branch(rq, rk, rv, rg).astype(dt)
        merged = (jax.nn.sigmoid(m_a) * jnp.einsum('bte,ed->btd', y_a, w_up_gdn[l])
                  + jax.nn.sigmoid(m_b) * jnp.einsum('bte,ed->btd', y_b, w_up_ret[l]))
        h = h + jnp.einsum('btd,de->bte', merged, w_out[l])
        h = h + hier_moe(rms_norm(h, norm_ffn_w[l]), w_group[l], b_group[l], w_expert[l], b_expert[l],
                         w_gate[l], w_up[l], w_down[l])
    return rms_norm(h, norm_final_w)
```

```python
import functools
import math

import jax
import jax.numpy as jnp
from jax import lax
from jax.experimental import pallas as pl
from jax.experimental.pallas import tpu as pltpu

F32 = jnp.float32
BF16 = jnp.bfloat16

D_MODEL = 1024
EPS = 1e-6
GDN_HEADS = 4
GDN_DK = 128
GDN_DV = 128
GDN_CONV = 4
RET_HEADS = 4
RET_DK = 128
RET_DV = 256
ROPE_BASE = 10000.0
N_GROUPS = 4
EXPERTS_PER_GROUP = 8
N_EXPERTS = N_GROUPS * EXPERTS_PER_GROUP
D_EXPERT = 512

GDN_QK = GDN_HEADS * GDN_DK
GDN_V = GDN_HEADS * GDN_DV
RET_QK = RET_HEADS * RET_DK
RET_V = RET_HEADS * RET_DV

LANES = 128
CHUNK = 128
INV_BLOCK = 16
VMEM_LIMIT = 56 * 1024 * 1024

PROJ_COLS = 3 * GDN_QK + GDN_V + 2 * RET_QK + 2 * RET_V + 2 * D_MODEL


def _silu(x):
    return x / (1.0 + jnp.exp(-x))


def _sigmoid(x):
    return 1.0 / (1.0 + jnp.exp(-x))


def _dot(a, b):
    return jnp.dot(a, b, preferred_element_type=F32)


def _dot_nt(a, b):
    return lax.dot_general(a, b, (((1,), (1,)), ((), ())), preferred_element_type=F32)


def _proj_kernel(x_ref, nw_ref, w_ref, wab_ref, proj_ref, ab_ref, u_ref):
    j = pl.program_id(1)

    @pl.when(j == 0)
    def _():
        x = x_ref[...]
        u = x * lax.rsqrt(jnp.mean(x * x, axis=-1, keepdims=True) + EPS) * nw_ref[...]
        ub = u.astype(BF16)
        u_ref[...] = ub
        ab_ref[...] = _dot(ub, wab_ref[...])

    proj_ref[...] = _dot(u_ref[...], w_ref[...]).astype(BF16)


def _proj_call(x2, norm_w, w_main, w_ab, tm=1024, tn=1024):
    m = x2.shape[0]
    return pl.pallas_call(
        _proj_kernel,
        grid=(m // tm, PROJ_COLS // tn),
        in_specs=[
            pl.BlockSpec((tm, D_MODEL), lambda i, j: (i, 0)),
            pl.BlockSpec((1, D_MODEL), lambda i, j: (0, 0)),
            pl.BlockSpec((D_MODEL, tn), lambda i, j: (0, j)),
            pl.BlockSpec((D_MODEL, LANES), lambda i, j: (0, 0)),
        ],
        out_specs=[
            pl.BlockSpec((tm, tn), lambda i, j: (i, j)),
            pl.BlockSpec((tm, LANES), lambda i, j: (i, 0)),
        ],
        out_shape=[
            jax.ShapeDtypeStruct((m, PROJ_COLS), BF16),
            jax.ShapeDtypeStruct((m, LANES), F32),
        ],
        scratch_shapes=[pltpu.VMEM((tm, D_MODEL), BF16)],
        compiler_params=pltpu.CompilerParams(
            dimension_semantics=("arbitrary", "arbitrary"), vmem_limit_bytes=VMEM_LIMIT),
        name="proj",
    )(x2, norm_w, w_main, w_ab)


def _unit_lower_inverse(low, ii, jj):
    eye = jnp.where(ii == jj, 1.0, 0.0).astype(F32)
    n = jnp.where((ii // INV_BLOCK) == (jj // INV_BLOCK), -low, 0.0)
    t = eye + n
    p = n
    span = 2
    while span < INV_BLOCK:
        p = _dot(p, p)
        t = t + _dot(t, p)
        span *= 2
    s = INV_BLOCK
    while s < CHUNK:
        c = jnp.where(((ii // (2 * s)) == (jj // (2 * s))) & ((ii // s) != (jj // s)), low, 0.0)
        t = t - _dot(t, _dot(c, t))
        s *= 2
    return t


def _gdn_kernel(qkv_ref, z_ref, ab_ref, convw_ref, alog_ref, dtb_ref, normw_ref, o_ref,
                u_s, w_s, qg_s, kdt_s, attn_s, gl_s, state_s, *, seq):
    nchunk = seq // CHUNK
    ii = lax.broadcasted_iota(jnp.int32, (CHUNK, CHUNK), 0)
    jj = lax.broadcasted_iota(jnp.int32, (CHUNK, CHUNK), 1)
    causal = ii >= jj
    strict = ii > jj
    tri = jnp.where(causal, 1.0, 0.0).astype(F32)
    neg_a = -jnp.exp(alog_ref[...])
    dtb = dtb_ref[...]

    def conv_cols(c, r0, lo):
        x = qkv_ref[pl.ds(r0, CHUNK), lo:lo + LANES].astype(F32)
        prev0 = pl.multiple_of(jnp.maximum(r0 - 16, 0), 16)
        prev = qkv_ref[pl.ds(prev0, 16), lo:lo + LANES].astype(F32)
        prev = prev * jnp.where(c > 0, 1.0, 0.0)
        zf = jnp.concatenate([prev[8:16], x], axis=0)
        w = convw_ref[:, lo:lo + LANES]
        y = (w[3:4] * x + w[2:3] * zf[7:7 + CHUNK] + w[1:2] * zf[6:6 + CHUNK]
             + w[0:1] * zf[5:5 + CHUNK])
        return _silu(y)

    def l2n(x):
        return x * lax.rsqrt(jnp.sum(x * x, axis=-1, keepdims=True) + EPS)

    def prep(c, carry):
        r0 = pl.multiple_of(c * CHUNK, CHUNK)
        ab = ab_ref[pl.ds(r0, CHUNK), :]
        xg = ab + dtb
        softplus = jnp.maximum(xg, 0.0) + jnp.log(1.0 + jnp.exp(-jnp.abs(xg)))
        g_all = neg_a * softplus
        beta_all = _sigmoid(ab)
        gc_all = jnp.dot(tri, g_all, preferred_element_type=F32, precision=lax.Precision.HIGHEST)
        gc_t = gc_all.T
        gl_s[c] = jnp.exp(gc_all[CHUNK - 1:CHUNK, :])
        for h in range(GDN_HEADS):
            q = l2n(conv_cols(c, r0, h * GDN_DK)) * (GDN_DK ** -0.5)
            k = l2n(conv_cols(c, r0, GDN_QK + h * GDN_DK))
            v = conv_cols(c, r0, 2 * GDN_QK + h * GDN_DV)
            gcol = gc_all[:, h:h + 1]
            grow = gc_t[h:h + 1, :]
            beta = beta_all[:, GDN_HEADS + h:GDN_HEADS + h + 1]
            decay = jnp.where(causal, jnp.exp(jnp.where(causal, gcol - grow, 0.0)), 0.0)
            eg = jnp.exp(gcol)
            kb = k * beta
            low = jnp.where(strict, _dot_nt(kb, k) * decay, 0.0)
            attn_s[c, h] = jnp.where(causal, _dot_nt(q, k) * decay, 0.0)
            t = _unit_lower_inverse(low, ii, jj)
            sol = _dot(t, jnp.concatenate([v * beta, kb * eg], axis=1))
            u_s[c, h] = sol[:, :GDN_DV]
            w_s[c, h] = sol[:, GDN_DV:]
            qg_s[c, h] = q * eg
            kd = k * jnp.exp(gc_all[CHUNK - 1:CHUNK, h:h + 1] - gcol)
            kdt_s[c, h] = kd.T
        return carry

    lax.fori_loop(0, nchunk, prep, 0)

    state_s[...] = jnp.zeros_like(state_s)
    normw = normw_ref[...]

    def scan(c, carry):
        r0 = pl.multiple_of(c * CHUNK, CHUNK)
        gl = gl_s[c]
        for h in range(GDN_HEADS):
            s = state_s[h]
            v_new = u_s[c, h] - _dot(w_s[c, h], s)
            o = _dot(qg_s[c, h], s) + _dot(attn_s[c, h], v_new)
            state_s[h] = s * gl[:, h:h + 1] + _dot(kdt_s[c, h], v_new)
            z = z_ref[pl.ds(r0, CHUNK), h * GDN_DV:(h + 1) * GDN_DV].astype(F32)
            on = o * lax.rsqrt(jnp.mean(o * o, axis=-1, keepdims=True) + EPS) * normw
            o_ref[pl.ds(r0, CHUNK), h * GDN_DV:(h + 1) * GDN_DV] = (on * _silu(z)).astype(BF16)
        return carry

    lax.fori_loop(0, nchunk, scan, 0)


def _gdn_call(proj, ab, conv_w8, alog_p, dtb_p, normw, batch, seq):
    nchunk = seq // CHUNK
    hs = (nchunk, GDN_HEADS, CHUNK, CHUNK)
    return pl.pallas_call(
        functools.partial(_gdn_kernel, seq=seq),
        grid=(batch,),
        in_specs=[
            pl.BlockSpec((seq, 3 * GDN_QK), lambda b: (b, 0)),
            pl.BlockSpec((seq, GDN_V), lambda b: (b, 3)),
            pl.BlockSpec((seq, LANES), lambda b: (b, 0)),
            pl.BlockSpec((8, 3 * GDN_QK), lambda b: (0, 0)),
            pl.BlockSpec((1, LANES), lambda b: (0, 0)),
            pl.BlockSpec((1, LANES), lambda b: (0, 0)),
            pl.BlockSpec((1, GDN_DV), lambda b: (0, 0)),
        ],
        out_specs=pl.BlockSpec((seq, GDN_V), lambda b: (b, 0)),
        out_shape=jax.ShapeDtypeStruct((batch * seq, GDN_V), BF16),
        scratch_shapes=[
            pltpu.VMEM(hs, F32), pltpu.VMEM(hs, F32), pltpu.VMEM(hs, F32), pltpu.VMEM(hs, F32),
            pltpu.VMEM(hs, F32),
            pltpu.VMEM((nchunk, 1, LANES), F32),
            pltpu.VMEM((GDN_HEADS, GDN_DK, GDN_DV), F32),
        ],
        compiler_params=pltpu.CompilerParams(
            dimension_semantics=("arbitrary",), vmem_limit_bytes=VMEM_LIMIT),
        name="gdn",
    )(proj, proj, ab, conv_w8, alog_p, dtb_p, normw)


def _ret_kernel(qk_ref, v_ref, g_ref, sin_ref, cos_ref, inner_ref, kdec_ref, qdec_ref, cdec_ref,
                o_ref, state_s, *, seq):
    nchunk = seq // CHUNK
    lane = lax.broadcasted_iota(jnp.int32, (CHUNK, RET_DK), 1)
    even = (lane % 2) == 0

    def rotate(x, sin, cos):
        nxt = pltpu.roll(x, RET_DK - 1, axis=1)
        prv = pltpu.roll(x, 1, axis=1)
        return x * cos + jnp.where(even, -nxt, prv) * sin

    state_s[...] = jnp.zeros_like(state_s)
    kdec = kdec_ref[...]
    qdec = qdec_ref[...]
    cdec = cdec_ref[...]

    def body(c, carry):
        r0 = pl.multiple_of(c * CHUNK, CHUNK)
        sin = sin_ref[pl.ds(r0, CHUNK), :]
        cos = cos_ref[pl.ds(r0, CHUNK), :]
        for h in range(RET_HEADS):
            q = rotate(qk_ref[pl.ds(r0, CHUNK), h * RET_DK:(h + 1) * RET_DK].astype(F32), sin, cos)
            k = rotate(qk_ref[pl.ds(r0, CHUNK), RET_QK + h * RET_DK:RET_QK + (h + 1) * RET_DK]
                       .astype(F32), sin, cos) * (RET_DK ** -0.5)
            v = v_ref[pl.ds(r0, CHUNK), h * RET_DV:(h + 1) * RET_DV].astype(F32)
            s = state_s[h]
            scores = _dot_nt(q, k) * inner_ref[h]
            o = _dot(scores, v) + _dot(q * qdec[:, h:h + 1], s)
            state_s[h] = s * cdec[:, h:h + 1] + _dot((k * kdec[:, h:h + 1]).T, v)
            gate = g_ref[pl.ds(r0, CHUNK), h * RET_DV:(h + 1) * RET_DV].astype(F32)
            on = o * lax.rsqrt(jnp.mean(o * o, axis=-1, keepdims=True) + EPS)
            o_ref[pl.ds(r0, CHUNK), h * RET_DV:(h + 1) * RET_DV] = (on * _silu(gate)).astype(BF16)
        return carry

    lax.fori_loop(0, nchunk, body, 0)


def _ret_call(proj, sin, cos, inner, kdec, qdec, cdec, batch, seq):
    return pl.pallas_call(
        functools.partial(_ret_kernel, seq=seq),
        grid=(batch,),
        in_specs=[
            pl.BlockSpec((seq, 2 * RET_QK), lambda b: (b, 2)),
            pl.BlockSpec((seq, RET_V), lambda b: (b, 3)),
            pl.BlockSpec((seq, RET_V), lambda b: (b, 4)),
            pl.BlockSpec((seq, RET_DK), lambda b: (0, 0)),
            pl.BlockSpec((seq, RET_DK), lambda b: (0, 0)),
            pl.BlockSpec((RET_HEADS, CHUNK, CHUNK), lambda b: (0, 0, 0)),
            pl.BlockSpec((CHUNK, LANES), lambda b: (0, 0)),
            pl.BlockSpec((CHUNK, LANES), lambda b: (0, 0)),
            pl.BlockSpec((1, LANES), lambda b: (0, 0)),
        ],
        out_specs=pl.BlockSpec((seq, RET_V), lambda b: (b, 0)),
        out_shape=jax.ShapeDtypeStruct((batch * seq, RET_V), BF16),
        scratch_shapes=[pltpu.VMEM((RET_HEADS, RET_DK, RET_DV), F32)],
        compiler_params=pltpu.CompilerParams(
            dimension_semantics=("arbitrary",), vmem_limit_bytes=VMEM_LIMIT),
        name="retention",
    )(proj, proj, proj, sin, cos, inner, kdec, qdec, cdec)


def _retention_tables(seq):
    inv_freq = 1.0 / (ROPE_BASE ** jnp.linspace(0.0, 1.0, RET_DK // 2, dtype=F32))
    ang = jnp.arange(seq, dtype=F32)[:, None] * inv_freq[None, :]
    sin = jnp.repeat(jnp.sin(ang), 2, axis=-1)
    cos = jnp.repeat(jnp.cos(ang), 2, axis=-1)
    log_gamma = jnp.log(1.0 - 2.0 ** (-5.0 - jnp.arange(RET_HEADS, dtype=F32)))
    idx = jnp.arange(CHUNK, dtype=F32)
    causal = jnp.tril(jnp.ones((CHUNK, CHUNK), dtype=bool))
    rel = jnp.where(causal, idx[:, None] - idx[None, :], 0.0)
    inner = jnp.where(causal, jnp.exp(rel[None] * log_gamma[:, None, None]), 0.0)
    k_decay = jnp.exp(log_gamma[:, None] * (CHUNK - 1.0 - idx)[None, :])
    q_decay = jnp.exp(log_gamma[:, None] * (idx + 1.0)[None, :])
    chunk_decay = jnp.exp(log_gamma * CHUNK)
    pad = LANES - RET_HEADS
    kdec = jnp.pad(k_decay.T, ((0, 0), (0, pad)))
    qdec = jnp.pad(q_decay.T, ((0, 0), (0, pad)))
    cdec = jnp.pad(chunk_decay[None, :], ((0, 0), (0, pad)))
    return sin, cos, inner, kdec, qdec, cdec


def _merge_kernel(x_ref, ya_ref, yb_ref, ma_ref, mb_ref, wa_ref, wr_ref, wo_ref, nw_ref,
                  wg_ref, bg_ref, we_ref, be_ref, h_ref, xn_ref, comb_ref):
    a = _dot(ya_ref[...], wa_ref[...])
    r = _dot(yb_ref[...], wr_ref[...])
    merged = _sigmoid(ma_ref[...].astype(F32)) * a + _sigmoid(mb_ref[...].astype(F32)) * r
    h = x_ref[...] + _dot(merged.astype(BF16), wo_ref[...])
    h_ref[...] = h
    xn = h * lax.rsqrt(jnp.mean(h * h, axis=-1, keepdims=True) + EPS) * nw_ref[...]
    xn_ref[...] = xn.astype(BF16)

    hi = lax.Precision.HIGHEST
    lane = lax.broadcasted_iota(jnp.int32, (xn.shape[0], LANES), 1)
    neg = -jnp.inf
    gl = jnp.dot(xn, wg_ref[...], preferred_element_type=F32, precision=hi) + bg_ref[...]
    gl = jnp.where(lane < N_GROUPS, gl, neg)
    gmax = jnp.max(gl, axis=-1, keepdims=True)
    gidx = jnp.min(jnp.where(gl == gmax, lane, LANES), axis=-1, keepdims=True)
    g_w = 1.0 / jnp.sum(jnp.exp(gl - gmax), axis=-1, keepdims=True)
    el = jnp.dot(xn, we_ref[...], preferred_element_type=F32, precision=hi) + be_ref[...]
    el = jnp.where((lane // EXPERTS_PER_GROUP == gidx) & (lane < N_EXPERTS), el, neg)
    m1 = jnp.max(el, axis=-1, keepdims=True)
    i1 = jnp.min(jnp.where(el == m1, lane, LANES), axis=-1, keepdims=True)
    el2 = jnp.where(lane == i1, neg, el)
    m2 = jnp.max(el2, axis=-1, keepdims=True)
    i2 = jnp.min(jnp.where(el2 == m2, lane, LANES), axis=-1, keepdims=True)
    e2 = jnp.exp(m2 - m1)
    p1 = g_w / (1.0 + e2)
    p2 = g_w * e2 / (1.0 + e2)
    comb_ref[...] = jnp.where(lane == i1, p1, 0.0) + jnp.where(lane == i2, p2, 0.0)


def _merge_call(x2, ya, yb, proj, wa, wr, wo, nw, wg, bg, we, be, tm=512):
    m = x2.shape[0]
    full = lambda shape: pl.BlockSpec(shape, lambda i: (0, 0))
    return pl.pallas_call(
        _merge_kernel,
        grid=(m // tm,),
        in_specs=[
            pl.BlockSpec((tm, D_MODEL), lambda i: (i, 0)),
            pl.BlockSpec((tm, GDN_V), lambda i: (i, 0)),
            pl.BlockSpec((tm, RET_V), lambda i: (i, 0)),
            pl.BlockSpec((tm, D_MODEL), lambda i: (i, 5)),
            pl.BlockSpec((tm, D_MODEL), lambda i: (i, 6)),
            full((GDN_V, D_MODEL)), full((RET_V, D_MODEL)), full((D_MODEL, D_MODEL)),
            full((1, D_MODEL)),
            full((D_MODEL, LANES)), full((1, LANES)), full((D_MODEL, LANES)), full((1, LANES)),
        ],
        out_specs=[
            pl.BlockSpec((tm, D_MODEL), lambda i: (i, 0)),
            pl.BlockSpec((tm, D_MODEL), lambda i: (i, 0)),
            pl.BlockSpec((tm, LANES), lambda i: (i, 0)),
        ],
        out_shape=[
            jax.ShapeDtypeStruct((m, D_MODEL), F32),
            jax.ShapeDtypeStruct((m, D_MODEL), BF16),
            jax.ShapeDtypeStruct((m, LANES), F32),
        ],
        compiler_params=pltpu.CompilerParams(
            dimension_semantics=("arbitrary",), vmem_limit_bytes=VMEM_LIMIT),
        name="merge_router",
    )(x2, ya, yb, proj, proj, wa, wr, wo, nw, wg, bg, we, be)


def _moe_kernel(xn_ref, h_ref, comb_ref, wg_ref, wu_ref, wd_ref, nw_ref, o_ref, acc_ref):
    e = pl.program_id(1)

    @pl.when(e == 0)
    def _():
        acc_ref[...] = jnp.zeros_like(acc_ref)

    x = xn_ref[...]
    lane = lax.broadcasted_iota(jnp.int32, comb_ref.shape, 1)
    cw = jnp.sum(jnp.where(lane == e, comb_ref[...], 0.0), axis=-1, keepdims=True)
    hid = _silu(_dot(x, wg_ref[0])) * _dot(x, wu_ref[0]) * cw
    acc_ref[...] += _dot(hid.astype(BF16), wd_ref[0])

    @pl.when(e == pl.num_programs(1) - 1)
    def _():
        h = h_ref[...] + acc_ref[...]
        o_ref[...] = h * lax.rsqrt(jnp.mean(h * h, axis=-1, keepdims=True) + EPS) * nw_ref[...]


def _moe_call(xn, h1, comb, wg, wu, wd, nw, tm=1024):
    m = xn.shape[0]
    return pl.pallas_call(
        _moe_kernel,
        grid=(m // tm, N_EXPERTS),
        in_specs=[
            pl.BlockSpec((tm, D_MODEL), lambda i, e: (i, 0)),
            pl.BlockSpec((tm, D_MODEL), lambda i, e: (i, 0)),
            pl.BlockSpec((tm, LANES), lambda i, e: (i, 0)),
            pl.BlockSpec((1, D_MODEL, D_EXPERT), lambda i, e: (e, 0, 0)),
            pl.BlockSpec((1, D_MODEL, D_EXPERT), lambda i, e: (e, 0, 0)),
            pl.BlockSpec((1, D_EXPERT, D_MODEL), lambda i, e: (e, 0, 0)),
            pl.BlockSpec((1, D_MODEL), lambda i, e: (0, 0)),
        ],
        out_specs=pl.BlockSpec((tm, D_MODEL), lambda i, e: (i, 0)),
        out_shape=jax.ShapeDtypeStruct((m, D_MODEL), F32),
        scratch_shapes=[pltpu.VMEM((tm, D_MODEL), F32)],
        compiler_params=pltpu.CompilerParams(
            dimension_semantics=("arbitrary", "arbitrary"), vmem_limit_bytes=VMEM_LIMIT),
        name="moe_dense",
    )(xn, h1, comb, wg, wu, wd, nw)


def _pad_lanes(a):
    return jnp.pad(a, ((0, 0), (0, LANES - a.shape[1])))


def kernel(x, norm_mix_w, w_in, conv_w, A_log, dt_bias, gdn_norm_w, w_up_gdn, w_up_ret, w_out,
           norm_ffn_w, w_group, b_group, w_expert, b_expert, w_gate, w_up, w_down, norm_final_w):
    batch, seq, d = x.shape
    m = batch * seq
    h = x.reshape(m, d)
    depth = w_in.shape[0]
    sin, cos, inner, kdec, qdec, cdec = _retention_tables(seq)
    for l in range(depth):
        o_ab = 3 * GDN_QK
        o_z = o_ab + 2 * GDN_HEADS
        wl = w_in[l]
        w_main = jnp.concatenate([wl[:, :o_ab], wl[:, o_z:]], axis=1).astype(BF16)
        w_ab = _pad_lanes(wl[:, o_ab:o_z]).astype(BF16)
        proj, ab = _proj_call(h, norm_mix_w[l][None, :], w_main, w_ab)

        conv8 = jnp.pad(conv_w[l], ((0, 8 - GDN_CONV), (0, 0)))
        ya = _gdn_call(proj, ab, conv8, _pad_lanes(A_log[l][None, :]), _pad_lanes(dt_bias[l][None, :]),
                       gdn_norm_w[l][None, :], batch, seq)
        yb = _ret_call(proj, sin, cos, inner, kdec, qdec, cdec, batch, seq)

        h1, xn, comb = _merge_call(
            h, ya, yb, proj, w_up_gdn[l].astype(BF16), w_up_ret[l].astype(BF16), w_out[l].astype(BF16),
            norm_ffn_w[l][None, :], _pad_lanes(w_group[l]), _pad_lanes(b_group[l][None, :]),
            _pad_lanes(w_expert[l]), _pad_lanes(b_expert[l][None, :]))

        assert depth == 1
        h = _moe_call(xn, h1, comb, w_gate[l].astype(BF16), w_up[l].astype(BF16),
                      w_down[l].astype(BF16), norm_final_w[None, :])
    return h.reshape(batch, seq, d)
```

```python
import functools
import math

import jax
import jax.numpy as jnp
from jax import lax
from jax.experimental import pallas as pl
from jax.experimental.pallas import tpu as pltpu
from jax.experimental.pallas import tpu_sc as plsc

F32 = jnp.float32
BF16 = jnp.bfloat16
I32 = jnp.int32
U32 = jnp.uint32

D_MODEL = 1024
EPS = 1e-6
GDN_HEADS = 4
GDN_DK = 128
GDN_DV = 128
GDN_CONV = 4
RET_HEADS = 4
RET_DK = 128
RET_DV = 256
ROPE_BASE = 10000.0
N_GROUPS = 4
EXPERTS_PER_GROUP = 8
N_EXPERTS = N_GROUPS * EXPERTS_PER_GROUP
D_EXPERT = 512

GDN_QK = GDN_HEADS * GDN_DK
GDN_V = GDN_HEADS * GDN_DV
RET_QK = RET_HEADS * RET_DK
RET_V = RET_HEADS * RET_DV

LANES = 128
CHUNK = 128
INV_BLOCK = 16
VMEM_LIMIT = 56 * 1024 * 1024

TOP_K = 2
SLOT_TILE = 256
PLAN_TILE = 256
HALF = D_MODEL // 2
SC_WORKERS = 32
SC_ROWS = 64

PROJ_COLS = 3 * GDN_QK + GDN_V + 2 * RET_QK + 2 * RET_V + 2 * D_MODEL


def _silu(x):
    return x / (1.0 + jnp.exp(-x))


def _sigmoid(x):
    return 1.0 / (1.0 + jnp.exp(-x))


def _dot(a, b):
    return jnp.dot(a, b, preferred_element_type=F32)


def _dot_nt(a, b):
    return lax.dot_general(a, b, (((1,), (1,)), ((), ())), preferred_element_type=F32)


def _pack_bf16_pairs(x):
    bits = lax.bitcast_convert_type(x.astype(BF16).astype(F32), U32)
    return (bits[:, :HALF] >> 16) | (bits[:, HALF:] & jnp.uint32(0xFFFF0000))


def _unpack_bf16_pairs(p):
    lo = lax.bitcast_convert_type(p << 16, F32)
    hi = lax.bitcast_convert_type(p & jnp.uint32(0xFFFF0000), F32)
    return jnp.concatenate([lo, hi], axis=1)


def _proj_kernel(x_ref, nw_ref, w_ref, wab_ref, proj_ref, ab_ref, u_ref):
    j = pl.program_id(1)

    @pl.when(j == 0)
    def _():
        x = x_ref[...]
        u = x * lax.rsqrt(jnp.mean(x * x, axis=-1, keepdims=True) + EPS) * nw_ref[...]
        ub = u.astype(BF16)
        u_ref[...] = ub
        ab_ref[...] = _dot(ub, wab_ref[...])

    proj_ref[...] = _dot(u_ref[...], w_ref[...]).astype(BF16)


def _proj_call(x2, norm_w, w_main, w_ab, tm=1024, tn=1024):
    m = x2.shape[0]
    return pl.pallas_call(
        _proj_kernel,
        grid=(m // tm, PROJ_COLS // tn),
        in_specs=[
            pl.BlockSpec((tm, D_MODEL), lambda i, j: (i, 0)),
            pl.BlockSpec((1, D_MODEL), lambda i, j: (0, 0)),
            pl.BlockSpec((D_MODEL, tn), lambda i, j: (0, j)),
            pl.BlockSpec((D_MODEL, LANES), lambda i, j: (0, 0)),
        ],
        out_specs=[
            pl.BlockSpec((tm, tn), lambda i, j: (i, j)),
            pl.BlockSpec((tm, LANES), lambda i, j: (i, 0)),
        ],
        out_shape=[
            jax.ShapeDtypeStruct((m, PROJ_COLS), BF16),
            jax.ShapeDtypeStruct((m, LANES), F32),
        ],
        scratch_shapes=[pltpu.VMEM((tm, D_MODEL), BF16)],
        compiler_params=pltpu.CompilerParams(
            dimension_semantics=("arbitrary", "arbitrary"), vmem_limit_bytes=VMEM_LIMIT),
        name="proj",
    )(x2, norm_w, w_main, w_ab)


def _unit_lower_inverse(low, ii, jj):
    eye = jnp.where(ii == jj, 1.0, 0.0).astype(F32)
    n = jnp.where((ii // INV_BLOCK) == (jj // INV_BLOCK), -low, 0.0)
    t = eye + n
    p = n
    span = 2
    while span < INV_BLOCK:
        p = _dot(p, p)
        t = t + _dot(t, p)
        span *= 2
    s = INV_BLOCK
    while s < CHUNK:
        c = jnp.where(((ii // (2 * s)) == (jj // (2 * s))) & ((ii // s) != (jj // s)), low, 0.0)
        t = t - _dot(t, _dot(c, t))
        s *= 2
    return t


def _gdn_kernel(qkv_ref, z_ref, ab_ref, convw_ref, alog_ref, dtb_ref, normw_ref, o_ref,
                u_s, w_s, qg_s, kdt_s, attn_s, gl_s, state_s, *, seq):
    nchunk = seq // CHUNK
    ii = lax.broadcasted_iota(jnp.int32, (CHUNK, CHUNK), 0)
    jj = lax.broadcasted_iota(jnp.int32, (CHUNK, CHUNK), 1)
    causal = ii >= jj
    strict = ii > jj
    tri = jnp.where(causal, 1.0, 0.0).astype(F32)
    neg_a = -jnp.exp(alog_ref[...])
    dtb = dtb_ref[...]

    def conv_cols(c, r0, lo):
        x = qkv_ref[pl.ds(r0, CHUNK), lo:lo + LANES].astype(F32)
        prev0 = pl.multiple_of(jnp.maximum(r0 - 16, 0), 16)
        prev = qkv_ref[pl.ds(prev0, 16), lo:lo + LANES].astype(F32)
        prev = prev * jnp.where(c > 0, 1.0, 0.0)
        zf = jnp.concatenate([prev[8:16], x], axis=0)
        w = convw_ref[:, lo:lo + LANES]
        y = (w[3:4] * x + w[2:3] * zf[7:7 + CHUNK] + w[1:2] * zf[6:6 + CHUNK]
             + w[0:1] * zf[5:5 + CHUNK])
        return _silu(y)

    def l2n(x):
        return x * lax.rsqrt(jnp.sum(x * x, axis=-1, keepdims=True) + EPS)

    def prep(c, carry):
        r0 = pl.multiple_of(c * CHUNK, CHUNK)
        ab = ab_ref[pl.ds(r0, CHUNK), :]
        xg = ab + dtb
        softplus = jnp.maximum(xg, 0.0) + jnp.log(1.0 + jnp.exp(-jnp.abs(xg)))
        g_all = neg_a * softplus
        beta_all = _sigmoid(ab)
        gc_all = jnp.dot(tri, g_all, preferred_element_type=F32, precision=lax.Precision.HIGHEST)
        gc_t = gc_all.T
        gl_s[c] = jnp.exp(gc_all[CHUNK - 1:CHUNK, :])
        for h in range(GDN_HEADS):
            q = l2n(conv_cols(c, r0, h * GDN_DK)) * (GDN_DK ** -0.5)
            k = l2n(conv_cols(c, r0, GDN_QK + h * GDN_DK))
            v = conv_cols(c, r0, 2 * GDN_QK + h * GDN_DV)
            gcol = gc_all[:, h:h + 1]
            grow = gc_t[h:h + 1, :]
            beta = beta_all[:, GDN_HEADS + h:GDN_HEADS + h + 1]
            decay = jnp.where(causal, jnp.exp(jnp.where(causal, gcol - grow, 0.0)), 0.0)
            eg = jnp.exp(gcol)
            kb = k * beta
            low = jnp.where(strict, _dot_nt(kb, k) * decay, 0.0)
            attn_s[c, h] = jnp.where(causal, _dot_nt(q, k) * decay, 0.0)
            t = _unit_lower_inverse(low, ii, jj)
            sol = _dot(t, jnp.concatenate([v * beta, kb * eg], axis=1))
            u_s[c, h] = sol[:, :GDN_DV]
            w_s[c, h] = sol[:, GDN_DV:]
            qg_s[c, h] = q * eg
            kd = k * jnp.exp(gc_all[CHUNK - 1:CHUNK, h:h + 1] - gcol)
            kdt_s[c, h] = kd.T
        return carry

    lax.fori_loop(0, nchunk, prep, 0)

    state_s[...] = jnp.zeros_like(state_s)
    normw = normw_ref[...]

    def scan(c, carry):
        r0 = pl.multiple_of(c * CHUNK, CHUNK)
        gl = gl_s[c]
        for h in range(GDN_HEADS):
            s = state_s[h]
            v_new = u_s[c, h] - _dot(w_s[c, h], s)
            o = _dot(qg_s[c, h], s) + _dot(attn_s[c, h], v_new)
            state_s[h] = s * gl[:, h:h + 1] + _dot(kdt_s[c, h], v_new)
            z = z_ref[pl.ds(r0, CHUNK), h * GDN_DV:(h + 1) * GDN_DV].astype(F32)
            on = o * lax.rsqrt(jnp.mean(o * o, axis=-1, keepdims=True) + EPS) * normw
            o_ref[pl.ds(r0, CHUNK), h * GDN_DV:(h + 1) * GDN_DV] = (on * _silu(z)).astype(BF16)
        return carry

    lax.fori_loop(0, nchunk, scan, 0)


def _gdn_call(proj, ab, conv_w8, alog_p, dtb_p, normw, batch, seq):
    nchunk = seq // CHUNK
    hs = (nchunk, GDN_HEADS, CHUNK, CHUNK)
    return pl.pallas_call(
        functools.partial(_gdn_kernel, seq=seq),
        grid=(batch,),
        in_specs=[
            pl.BlockSpec((seq, 3 * GDN_QK), lambda b: (b, 0)),
            pl.BlockSpec((seq, GDN_V), lambda b: (b, 3)),
            pl.BlockSpec((seq, LANES), lambda b: (b, 0)),
            pl.BlockSpec((8, 3 * GDN_QK), lambda b: (0, 0)),
            pl.BlockSpec((1, LANES), lambda b: (0, 0)),
            pl.BlockSpec((1, LANES), lambda b: (0, 0)),
            pl.BlockSpec((1, GDN_DV), lambda b: (0, 0)),
        ],
        out_specs=pl.BlockSpec((seq, GDN_V), lambda b: (b, 0)),
        out_shape=jax.ShapeDtypeStruct((batch * seq, GDN_V), BF16),
        scratch_shapes=[
            pltpu.VMEM(hs, F32), pltpu.VMEM(hs, F32), pltpu.VMEM(hs, F32), pltpu.VMEM(hs, F32),
            pltpu.VMEM(hs, F32),
            pltpu.VMEM((nchunk, 1, LANES), F32),
            pltpu.VMEM((GDN_HEADS, GDN_DK, GDN_DV), F32),
        ],
        compiler_params=pltpu.CompilerParams(
            dimension_semantics=("arbitrary",), vmem_limit_bytes=VMEM_LIMIT),
        name="gdn",
    )(proj, proj, ab, conv_w8, alog_p, dtb_p, normw)


def _ret_kernel(qk_ref, v_ref, g_ref, sin_ref, cos_ref, inner_ref, kdec_ref, qdec_ref, cdec_ref,
                o_ref, state_s, *, seq):
    nchunk = seq // CHUNK
    lane = lax.broadcasted_iota(jnp.int32, (CHUNK, RET_DK), 1)
    even = (lane % 2) == 0

    def rotate(x, sin, cos):
        nxt = pltpu.roll(x, RET_DK - 1, axis=1)
        prv = pltpu.roll(x, 1, axis=1)
        return x * cos + jnp.where(even, -nxt, prv) * sin

    state_s[...] = jnp.zeros_like(state_s)
    kdec = kdec_ref[...]
    qdec = qdec_ref[...]
    cdec = cdec_ref[...]

    def body(c, carry):
        r0 = pl.multiple_of(c * CHUNK, CHUNK)
        sin = sin_ref[pl.ds(r0, CHUNK), :]
        cos = cos_ref[pl.ds(r0, CHUNK), :]
        for h in range(RET_HEADS):
            q = rotate(qk_ref[pl.ds(r0, CHUNK), h * RET_DK:(h + 1) * RET_DK].astype(F32), sin, cos)
            k = rotate(qk_ref[pl.ds(r0, CHUNK), RET_QK + h * RET_DK:RET_QK + (h + 1) * RET_DK]
                       .astype(F32), sin, cos) * (RET_DK ** -0.5)
            v = v_ref[pl.ds(r0, CHUNK), h * RET_DV:(h + 1) * RET_DV].astype(F32)
            s = state_s[h]
            scores = _dot_nt(q, k) * inner_ref[h]
            o = _dot(scores, v) + _dot(q * qdec[:, h:h + 1], s)
            state_s[h] = s * cdec[:, h:h + 1] + _dot((k * kdec[:, h:h + 1]).T, v)
            gate = g_ref[pl.ds(r0, CHUNK), h * RET_DV:(h + 1) * RET_DV].astype(F32)
            on = o * lax.rsqrt(jnp.mean(o * o, axis=-1, keepdims=True) + EPS)
            o_ref[pl.ds(r0, CHUNK), h * RET_DV:(h + 1) * RET_DV] = (on * _silu(gate)).astype(BF16)
        return carry

    lax.fori_loop(0, nchunk, body, 0)


def _ret_call(proj, sin, cos, inner, kdec, qdec, cdec, batch, seq):
    return pl.pallas_call(
        functools.partial(_ret_kernel, seq=seq),
        grid=(batch,),
        in_specs=[
            pl.BlockSpec((seq, 2 * RET_QK), lambda b: (b, 2)),
            pl.BlockSpec((seq, RET_V), lambda b: (b, 3)),
            pl.BlockSpec((seq, RET_V), lambda b: (b, 4)),
            pl.BlockSpec((seq, RET_DK), lambda b: (0, 0)),
            pl.BlockSpec((seq, RET_DK), lambda b: (0, 0)),
            pl.BlockSpec((RET_HEADS, CHUNK, CHUNK), lambda b: (0, 0, 0)),
            pl.BlockSpec((CHUNK, LANES), lambda b: (0, 0)),
            pl.BlockSpec((CHUNK, LANES), lambda b: (0, 0)),
            pl.BlockSpec((1, LANES), lambda b: (0, 0)),
        ],
        out_specs=pl.BlockSpec((seq, RET_V), lambda b: (b, 0)),
        out_shape=jax.ShapeDtypeStruct((batch * seq, RET_V), BF16),
        scratch_shapes=[pltpu.VMEM((RET_HEADS, RET_DK, RET_DV), F32)],
        compiler_params=pltpu.CompilerParams(
            dimension_semantics=("arbitrary",), vmem_limit_bytes=VMEM_LIMIT),
        name="retention",
    )(proj, proj, proj, sin, cos, inner, kdec, qdec, cdec)


def _retention_tables(seq):
    inv_freq = 1.0 / (ROPE_BASE ** jnp.linspace(0.0, 1.0, RET_DK // 2, dtype=F32))
    ang = jnp.arange(seq, dtype=F32)[:, None] * inv_freq[None, :]
    sin = jnp.repeat(jnp.sin(ang), 2, axis=-1)
    cos = jnp.repeat(jnp.cos(ang), 2, axis=-1)
    log_gamma = jnp.log(1.0 - 2.0 ** (-5.0 - jnp.arange(RET_HEADS, dtype=F32)))
    idx = jnp.arange(CHUNK, dtype=F32)
    causal = jnp.tril(jnp.ones((CHUNK, CHUNK), dtype=bool))
    rel = jnp.where(causal, idx[:, None] - idx[None, :], 0.0)
    inner = jnp.where(causal, jnp.exp(rel[None] * log_gamma[:, None, None]), 0.0)
    k_decay = jnp.exp(log_gamma[:, None] * (CHUNK - 1.0 - idx)[None, :])
    q_decay = jnp.exp(log_gamma[:, None] * (idx + 1.0)[None, :])
    chunk_decay = jnp.exp(log_gamma * CHUNK)
    pad = LANES - RET_HEADS
    kdec = jnp.pad(k_decay.T, ((0, 0), (0, pad)))
    qdec = jnp.pad(q_decay.T, ((0, 0), (0, pad)))
    cdec = jnp.pad(chunk_decay[None, :], ((0, 0), (0, pad)))
    return sin, cos, inner, kdec, qdec, cdec


def _merge_kernel(x_ref, ya_ref, yb_ref, ma_ref, mb_ref, wa_ref, wr_ref, wo_ref, nw_ref,
                  wg_ref, bg_ref, we_ref, be_ref, h_ref, xn_ref, ridx_ref, rw_ref):
    a = _dot(ya_ref[...], wa_ref[...])
    r = _dot(yb_ref[...], wr_ref[...])
    merged = _sigmoid(ma_ref[...].astype(F32)) * a + _sigmoid(mb_ref[...].astype(F32)) * r
    h = x_ref[...] + _dot(merged.astype(BF16), wo_ref[...])
    h_ref[...] = h
    xn = h * lax.rsqrt(jnp.mean(h * h, axis=-1, keepdims=True) + EPS) * nw_ref[...]
    xn_ref[...] = _pack_bf16_pairs(xn)

    hi = lax.Precision.HIGHEST
    lane = lax.broadcasted_iota(jnp.int32, (xn.shape[0], LANES), 1)
    neg = -jnp.inf
    gl = jnp.dot(xn, wg_ref[...], preferred_element_type=F32, precision=hi) + bg_ref[...]
    gl = jnp.where(lane < N_GROUPS, gl, neg)
    gmax = jnp.max(gl, axis=-1, keepdims=True)
    gidx = jnp.min(jnp.where(gl == gmax, lane, LANES), axis=-1, keepdims=True)
    g_w = 1.0 / jnp.sum(jnp.exp(gl - gmax), axis=-1, keepdims=True)
    el = jnp.dot(xn, we_ref[...], preferred_element_type=F32, precision=hi) + be_ref[...]
    el = jnp.where((lane // EXPERTS_PER_GROUP == gidx) & (lane < N_EXPERTS), el, neg)
    m1 = jnp.max(el, axis=-1, keepdims=True)
    i1 = jnp.min(jnp.where(el == m1, lane, LANES), axis=-1, keepdims=True)
    el2 = jnp.where(lane == i1, neg, el)
    m2 = jnp.max(el2, axis=-1, keepdims=True)
    i2 = jnp.min(jnp.where(el2 == m2, lane, LANES), axis=-1, keepdims=True)
    e2 = jnp.exp(m2 - m1)
    p1 = g_w / (1.0 + e2)
    p2 = g_w * e2 / (1.0 + e2)
    ridx_ref[...] = jnp.where(lane == 0, i1, jnp.where(lane == 1, i2, 0))
    rw_ref[...] = jnp.where(lane == 0, p1, jnp.where(lane == 1, p2, 0.0))


def _merge_call(x2, ya, yb, proj, wa, wr, wo, nw, wg, bg, we, be, tm=512):
    m = x2.shape[0]
    full = lambda shape: pl.BlockSpec(shape, lambda i: (0, 0))
    return pl.pallas_call(
        _merge_kernel,
        grid=(m // tm,),
        in_specs=[
            pl.BlockSpec((tm, D_MODEL), lambda i: (i, 0)),
            pl.BlockSpec((tm, GDN_V), lambda i: (i, 0)),
            pl.BlockSpec((tm, RET_V), lambda i: (i, 0)),
            pl.BlockSpec((tm, D_MODEL), lambda i: (i, 5)),
            pl.BlockSpec((tm, D_MODEL), lambda i: (i, 6)),
            full((GDN_V, D_MODEL)), full((RET_V, D_MODEL)), full((D_MODEL, D_MODEL)),
            full((1, D_MODEL)),
            full((D_MODEL, LANES)), full((1, LANES)), full((D_MODEL, LANES)), full((1, LANES)),
        ],
        out_specs=[
            pl.BlockSpec((tm, D_MODEL), lambda i: (i, 0)),
            pl.BlockSpec((tm, HALF), lambda i: (i, 0)),
            pl.BlockSpec((tm, LANES), lambda i: (i, 0)),
            pl.BlockSpec((tm, LANES), lambda i: (i, 0)),
        ],
        out_shape=[
            jax.ShapeDtypeStruct((m, D_MODEL), F32),
            jax.ShapeDtypeStruct((m, HALF), U32),
            jax.ShapeDtypeStruct((m, LANES), I32),
            jax.ShapeDtypeStruct((m, LANES), F32),
        ],
        compiler_params=pltpu.CompilerParams(
            dimension_semantics=("arbitrary",), vmem_limit_bytes=VMEM_LIMIT),
        name="merge_router",
    )(x2, ya, yb, proj, proj, wa, wr, wo, nw, wg, bg, we, be)


def _slot_counts(m):
    n_pad = N_EXPERTS * SLOT_TILE
    n_slots = TOP_K * m + n_pad
    return n_slots, n_slots // SLOT_TILE, n_pad


def _lane_prefix_sum(x, lane):
    s = 1
    while s < LANES:
        x = x + jnp.where(lane >= s, pltpu.roll(x, s, axis=1), 0.0)
        s *= 2
    return x


def _plan_kernel(ridx_ref, slots_ref, tile_ref, pad_ref, cnt_s, carry_s, off_s, *, n_slots):
    p = pl.program_id(0)
    i = pl.program_id(1)
    lane = lax.broadcasted_iota(I32, (PLAN_TILE, LANES), 1)
    row = lax.broadcasted_iota(I32, (PLAN_TILE, LANES), 0)
    lane1 = lane[0:1]
    e1 = ridx_ref[:, 0:1]
    e2 = ridx_ref[:, 1:2]
    onehot = jnp.where((lane == e1) | (lane == e2), 1.0, 0.0)
    colsum = jnp.sum(onehot, axis=0, keepdims=True)

    @pl.when((p == 0) & (i == 0))
    def _():
        cnt_s[...] = jnp.zeros_like(cnt_s)

    @pl.when(p == 0)
    def _():
        cnt_s[...] += colsum

    @pl.when((p == 1) & (i == 0))
    def _():
        cnt = cnt_s[...]
        tile = float(SLOT_TILE)
        padded = jnp.floor((cnt + (tile - 1.0)) / tile) * tile
        incl = _lane_prefix_sum(padded, lane1)
        off = incl - padded
        off_s[...] = off
        carry_s[...] = jnp.zeros_like(carry_s)
        first = (row * SLOT_TILE).astype(F32)
        ended = jnp.where((lane < N_EXPERTS) & (incl <= first), 1.0, 0.0)
        tile_ref[...] = jnp.broadcast_to(
            jnp.sum(ended, axis=-1, keepdims=True), (PLAN_TILE, LANES)).astype(I32)
        rowf = row.astype(F32)
        spill = (n_slots + lane * SLOT_TILE + row).astype(F32)
        pad_ref[...] = jnp.where(rowf < padded - cnt, off + cnt + rowf, spill).astype(I32)

    @pl.when(p == 1)
    def _():
        strict = jnp.where(row[:, 0:1] > lax.broadcasted_iota(I32, (PLAN_TILE, PLAN_TILE), 1),
                           1.0, 0.0).astype(BF16)
        rank = _dot(strict, onehot.astype(BF16)) + carry_s[...]
        pos = rank + off_s[...]
        s1 = jnp.sum(jnp.where(lane == e1, pos, 0.0), axis=-1, keepdims=True)
        s2 = jnp.sum(jnp.where(lane == e2, pos, 0.0), axis=-1, keepdims=True)
        slots_ref[...] = jnp.where(lane == 0, s1, jnp.where(lane == 1, s2, 0.0)).astype(I32)
        carry_s[...] += colsum


def _plan_call(ridx):
    m = ridx.shape[0]
    n_slots, n_tiles, _ = _slot_counts(m)
    assert n_tiles <= PLAN_TILE and SLOT_TILE == PLAN_TILE
    return pl.pallas_call(
        functools.partial(_plan_kernel, n_slots=n_slots),
        grid=(2, m // PLAN_TILE),
        in_specs=[pl.BlockSpec((PLAN_TILE, LANES), lambda p, i: (i, 0))],
        out_specs=[
            pl.BlockSpec((PLAN_TILE, LANES), lambda p, i: (i * p, 0)),
            pl.BlockSpec((PLAN_TILE, LANES), lambda p, i: (0, 0)),
            pl.BlockSpec((PLAN_TILE, LANES), lambda p, i: (0, 0)),
        ],
        out_shape=[
            jax.ShapeDtypeStruct((m, LANES), I32),
            jax.ShapeDtypeStruct((PLAN_TILE, LANES), I32),
            jax.ShapeDtypeStruct((PLAN_TILE, LANES), I32),
        ],
        scratch_shapes=[pltpu.VMEM((1, LANES), F32), pltpu.VMEM((1, LANES), F32),
                        pltpu.VMEM((1, LANES), F32)],
        compiler_params=pltpu.CompilerParams(dimension_semantics=("arbitrary", "arbitrary")),
        name="dispatch_plan",
    )(ridx)


def _sc_mesh():
    return plsc.VectorSubcoreMesh(core_axis_name="c", subcore_axis_name="s")


def _sc_worker():
    return lax.axis_index("s") * 2 + lax.axis_index("c")


def _sc_dispatch(xn, slot1, slot2, pad_slots, zeros):
    m = xn.shape[0]
    n_slots, _, n_pad = _slot_counts(m)
    per = m // SC_WORKERS
    pad_per = n_pad // SC_WORKERS

    @functools.partial(
        pl.kernel, mesh=_sc_mesh(),
        out_type=jax.ShapeDtypeStruct((n_slots + n_pad, HALF), I32),
        scratch_types=[pltpu.VMEM((SC_ROWS,), I32), pltpu.VMEM((SC_ROWS,), I32),
                       pltpu.VMEM((SC_ROWS, HALF), I32), pltpu.SemaphoreType.DMA],
        name="sc_dispatch")
    def k(x_hbm, s1_hbm, s2_hbm, p_hbm, z_hbm, o_hbm, i1_v, i2_v, rows_v, sem):
        wid = _sc_worker()

        @pl.loop(0, per // SC_ROWS)
        def _(ci):
            t0 = wid * per + ci * SC_ROWS
            pltpu.sync_copy(x_hbm.at[pl.ds(t0, SC_ROWS)], rows_v)
            pltpu.sync_copy(s1_hbm.at[pl.ds(t0, SC_ROWS)], i1_v)
            pltpu.sync_copy(s2_hbm.at[pl.ds(t0, SC_ROWS)], i2_v)
            c1 = pltpu.async_copy(rows_v, o_hbm.at[i1_v], sem)
            c2 = pltpu.async_copy(rows_v, o_hbm.at[i2_v], sem)
            c1.wait()
            c2.wait()

        pltpu.sync_copy(z_hbm, rows_v)

        @pl.loop(0, pad_per // SC_ROWS)
        def _(ci):
            t0 = wid * pad_per + ci * SC_ROWS
            pltpu.sync_copy(p_hbm.at[pl.ds(t0, SC_ROWS)], i1_v)
            pltpu.async_copy(rows_v, o_hbm.at[i1_v], sem).wait()

    return k(xn, slot1, slot2, pad_slots, zeros)


def _sc_collect(ys, slot1, slot2):
    m = slot1.shape[0]
    per = m // SC_WORKERS
    row = jax.ShapeDtypeStruct((m, HALF), I32)

    @functools.partial(
        pl.kernel, mesh=_sc_mesh(), out_type=[row, row],
        scratch_types=[pltpu.VMEM((SC_ROWS,), I32), pltpu.VMEM((SC_ROWS, HALF), I32),
                       pltpu.SemaphoreType.DMA],
        name="sc_collect")
    def k(y_hbm, s1_hbm, s2_hbm, g1_hbm, g2_hbm, i_v, rows_v, sem):
        wid = _sc_worker()

        @pl.loop(0, per // SC_ROWS)
        def _(ci):
            t0 = wid * per + ci * SC_ROWS
            for s_hbm, g_hbm in ((s1_hbm, g1_hbm), (s2_hbm, g2_hbm)):
                pltpu.sync_copy(s_hbm.at[pl.ds(t0, SC_ROWS)], i_v)
                pltpu.async_copy(y_hbm.at[i_v], rows_v, sem).wait()
                pltpu.sync_copy(rows_v, g_hbm.at[pl.ds(t0, SC_ROWS)])

    return k(ys, slot1, slot2)


def _expert_kernel(te_ref, xs_ref, wg_ref, wu_ref, wd_ref, ys_ref, wg_b, wu_b, wd_b):
    j = pl.program_id(0)
    e = te_ref[j]
    prev = te_ref[jnp.maximum(j - 1, 0)]

    @pl.when((j == 0) | (e != prev))
    def _():
        wg_b[...] = wg_ref[0].astype(BF16)
        wu_b[...] = wu_ref[0].astype(BF16)
        wd_b[...] = wd_ref[0].astype(BF16)

    @pl.when(e < N_EXPERTS)
    def _():
        x = _unpack_bf16_pairs(xs_ref[...]).astype(BF16)
        hid = _silu(_dot(x, wg_b[...])) * _dot(x, wu_b[...])
        ys_ref[...] = _pack_bf16_pairs(_dot(hid.astype(BF16), wd_b[...]))

    @pl.when(e >= N_EXPERTS)
    def _():
        ys_ref[...] = jnp.zeros_like(ys_ref)


def _expert_call(tile_expert, xs, wg, wu, wd, n_tiles):
    wmap = lambda j, te: (jnp.minimum(te[j], N_EXPERTS - 1), 0, 0)
    return pl.pallas_call(
        _expert_kernel,
        grid_spec=pltpu.PrefetchScalarGridSpec(
            num_scalar_prefetch=1,
            grid=(n_tiles,),
            in_specs=[
                pl.BlockSpec((SLOT_TILE, HALF), lambda j, te: (j, 0)),
                pl.BlockSpec((1, D_MODEL, D_EXPERT), wmap),
                pl.BlockSpec((1, D_MODEL, D_EXPERT), wmap),
                pl.BlockSpec((1, D_EXPERT, D_MODEL), wmap),
            ],
            out_specs=pl.BlockSpec((SLOT_TILE, HALF), lambda j, te: (j, 0)),
            scratch_shapes=[pltpu.VMEM((D_MODEL, D_EXPERT), BF16), pltpu.VMEM((D_MODEL, D_EXPERT), BF16),
                            pltpu.VMEM((D_EXPERT, D_MODEL), BF16)],
        ),
        out_shape=jax.ShapeDtypeStruct((n_tiles * SLOT_TILE, HALF), U32),
        compiler_params=pltpu.CompilerParams(
            dimension_semantics=("arbitrary",), vmem_limit_bytes=VMEM_LIMIT),
        name="experts",
    )(tile_expert, xs, wg, wu, wd)


def _final_kernel(h_ref, g1_ref, g2_ref, rw_ref, nw_ref, o_ref):
    rw = rw_ref[...]
    y = rw[:, 0:1] * _unpack_bf16_pairs(g1_ref[...]) + rw[:, 1:2] * _unpack_bf16_pairs(g2_ref[...])
    h = h_ref[...] + y
    o_ref[...] = h * lax.rsqrt(jnp.mean(h * h, axis=-1, keepdims=True) + EPS) * nw_ref[...]


def _final_call(h1, g1, g2, rw, nw, tm=512):
    m = h1.shape[0]
    return pl.pallas_call(
        _final_kernel,
        grid=(m // tm,),
        in_specs=[
            pl.BlockSpec((tm, D_MODEL), lambda i: (i, 0)),
            pl.BlockSpec((tm, HALF), lambda i: (i, 0)),
            pl.BlockSpec((tm, HALF), lambda i: (i, 0)),
            pl.BlockSpec((tm, LANES), lambda i: (i, 0)),
            pl.BlockSpec((1, D_MODEL), lambda i: (0, 0)),
        ],
        out_specs=pl.BlockSpec((tm, D_MODEL), lambda i: (i, 0)),
        out_shape=jax.ShapeDtypeStruct((m, D_MODEL), F32),
        compiler_params=pltpu.CompilerParams(dimension_semantics=("arbitrary",)),
        name="combine_final",
    )(h1, g1, g2, rw, nw)


def _pad_lanes(a):
    return jnp.pad(a, ((0, 0), (0, LANES - a.shape[1])))


def kernel(x, norm_mix_w, w_in, conv_w, A_log, dt_bias, gdn_norm_w, w_up_gdn, w_up_ret, w_out,
           norm_ffn_w, w_group, b_group, w_expert, b_expert, w_gate, w_up, w_down, norm_final_w):
    batch, seq, d = x.shape
    m = batch * seq
    h = x.reshape(m, d)
    depth = w_in.shape[0]
    sin, cos, inner, kdec, qdec, cdec = _retention_tables(seq)
    for l in range(depth):
        o_ab = 3 * GDN_QK
        o_z = o_ab + 2 * GDN_HEADS
        wl = w_in[l]
        w_main = jnp.concatenate([wl[:, :o_ab], wl[:, o_z:]], axis=1).astype(BF16)
        w_ab = _pad_lanes(wl[:, o_ab:o_z]).astype(BF16)
        proj, ab = _proj_call(h, norm_mix_w[l][None, :], w_main, w_ab)

        conv8 = jnp.pad(conv_w[l], ((0, 8 - GDN_CONV), (0, 0)))
        ya = _gdn_call(proj, ab, conv8, _pad_lanes(A_log[l][None, :]), _pad_lanes(dt_bias[l][None, :]),
                       gdn_norm_w[l][None, :], batch, seq)
        yb = _ret_call(proj, sin, cos, inner, kdec, qdec, cdec, batch, seq)

        h1, xn, ridx, rw = _merge_call(
            h, ya, yb, proj, w_up_gdn[l].astype(BF16), w_up_ret[l].astype(BF16), w_out[l].astype(BF16),
            norm_ffn_w[l][None, :], _pad_lanes(w_group[l]), _pad_lanes(b_group[l][None, :]),
            _pad_lanes(w_expert[l]), _pad_lanes(b_expert[l][None, :]))

        _, n_tiles, _ = _slot_counts(m)
        slots, tiles, pads = _plan_call(ridx)
        slot1, slot2 = slots[:, 0], slots[:, 1]
        pad_slots = pads[:, :N_EXPERTS].T.reshape(-1)
        xs = _sc_dispatch(lax.bitcast_convert_type(xn, I32), slot1, slot2, pad_slots,
                          jnp.zeros((SC_ROWS, HALF), I32))
        ys = _expert_call(tiles[:n_tiles, 0], lax.bitcast_convert_type(xs, U32),
                          w_gate[l], w_up[l], w_down[l], n_tiles)
        g1, g2 = _sc_collect(lax.bitcast_convert_type(ys, I32), slot1, slot2)

        assert depth == 1
        h = _final_call(h1, lax.bitcast_convert_type(g1, U32), lax.bitcast_convert_type(g2, U32),
                        rw, norm_final_w[None, :])
    return h.reshape(batch, seq, d)
```

```python
import functools
import math

import jax
import jax.numpy as jnp
from jax import lax
from jax.experimental import pallas as pl
from jax.experimental.pallas import tpu as pltpu
from jax.experimental.pallas import tpu_sc as plsc

F32 = jnp.float32
BF16 = jnp.bfloat16
I32 = jnp.int32
U32 = jnp.uint32

D_MODEL = 1024
EPS = 1e-6
GDN_HEADS = 4
GDN_DK = 128
GDN_DV = 128
GDN_CONV = 4
RET_HEADS = 4
RET_DK = 128
RET_DV = 256
ROPE_BASE = 10000.0
N_GROUPS = 4
EXPERTS_PER_GROUP = 8
N_EXPERTS = N_GROUPS * EXPERTS_PER_GROUP
D_EXPERT = 512

GDN_QK = GDN_HEADS * GDN_DK
GDN_V = GDN_HEADS * GDN_DV
RET_QK = RET_HEADS * RET_DK
RET_V = RET_HEADS * RET_DV

LANES = 128
CHUNK = 128
INV_BLOCK = 16
GDN_PREP_CHUNKS = 2
VMEM_LIMIT = 56 * 1024 * 1024

TOP_K = 2
SLOT_TILE = 256
PLAN_TILE = 256
HALF = D_MODEL // 2
SC_WORKERS = 32
SC_ROWS = 64

PROJ_COLS = 3 * GDN_QK + GDN_V + 2 * RET_QK + 2 * RET_V + 2 * D_MODEL


def _silu(x):
    return x / (1.0 + jnp.exp(-x))


def _sigmoid(x):
    return 1.0 / (1.0 + jnp.exp(-x))


def _dot(a, b):
    return jnp.dot(a, b, preferred_element_type=F32)


def _dot_nt(a, b):
    return lax.dot_general(a, b, (((1,), (1,)), ((), ())), preferred_element_type=F32)


def _pack_bf16_pairs(x):
    bits = lax.bitcast_convert_type(x.astype(BF16).astype(F32), U32)
    packed = (bits[:, :HALF] >> 16) | (bits[:, HALF:] & jnp.uint32(0xFFFF0000))
    return lax.bitcast_convert_type(packed, I32)


def _unpack_bf16_pairs(p):
    p = lax.bitcast_convert_type(p, U32)
    lo = lax.bitcast_convert_type(p << 16, F32)
    hi = lax.bitcast_convert_type(p & jnp.uint32(0xFFFF0000), F32)
    return jnp.concatenate([lo, hi], axis=1)


def _proj_kernel(x_ref, nw_ref, w_ref, wab_ref, proj_ref, ab_ref, u_ref):
    j = pl.program_id(1)

    @pl.when(j == 0)
    def _():
        x = x_ref[...]
        u = x * lax.rsqrt(jnp.mean(x * x, axis=-1, keepdims=True) + EPS) * nw_ref[...]
        ub = u.astype(BF16)
        u_ref[...] = ub
        ab_ref[...] = _dot(ub, wab_ref[...])

    proj_ref[...] = _dot(u_ref[...], w_ref[...]).astype(BF16)


def _proj_call(x2, norm_w, w_main, w_ab, tm=1024, tn=1024):
    m = x2.shape[0]
    return pl.pallas_call(
        _proj_kernel,
        grid=(m // tm, PROJ_COLS // tn),
        in_specs=[
            pl.BlockSpec((tm, D_MODEL), lambda i, j: (i, 0)),
            pl.BlockSpec((1, D_MODEL), lambda i, j: (0, 0)),
            pl.BlockSpec((D_MODEL, tn), lambda i, j: (0, j)),
            pl.BlockSpec((D_MODEL, LANES), lambda i, j: (0, 0)),
        ],
        out_specs=[
            pl.BlockSpec((tm, tn), lambda i, j: (i, j)),
            pl.BlockSpec((tm, LANES), lambda i, j: (i, 0)),
        ],
        out_shape=[
            jax.ShapeDtypeStruct((m, PROJ_COLS), BF16),
            jax.ShapeDtypeStruct((m, LANES), F32),
        ],
        scratch_shapes=[pltpu.VMEM((tm, D_MODEL), BF16)],
        compiler_params=pltpu.CompilerParams(
            dimension_semantics=("arbitrary", "arbitrary"), vmem_limit_bytes=VMEM_LIMIT),
        name="proj",
    )(x2, norm_w, w_main, w_ab)


def _unit_lower_inverses(lows, ii, jj):
    eye = jnp.where(ii == jj, 1.0, 0.0).astype(F32)
    in_block = (ii // INV_BLOCK) == (jj // INV_BLOCK)
    ps = [jnp.where(in_block, -low, 0.0) for low in lows]
    ts = [eye + p for p in ps]
    span = 2
    while span < INV_BLOCK:
        ps = [_dot(p, p) for p in ps]
        ts = [t + _dot(t, p) for t, p in zip(ts, ps)]
        span *= 2
    s = INV_BLOCK
    while s < CHUNK:
        off_diag = ((ii // (2 * s)) == (jj // (2 * s))) & ((ii // s) != (jj // s))
        xs = [_dot(jnp.where(off_diag, low, 0.0), t) for low, t in zip(lows, ts)]
        ts = [t - _dot(t, x) for t, x in zip(ts, xs)]
        s *= 2
    return ts


def _gdn_kernel(qkv_ref, z_ref, ab_ref, convw_ref, alog_ref, dtb_ref, normw_ref, o_ref,
                u_s, w_s, qg_s, kdt_s, attn_s, gl_s, state_s, *, seq):
    nchunk = seq // CHUNK
    ii = lax.broadcasted_iota(jnp.int32, (CHUNK, CHUNK), 0)
    jj = lax.broadcasted_iota(jnp.int32, (CHUNK, CHUNK), 1)
    causal = ii >= jj
    strict = ii > jj
    tri = jnp.where(causal, 1.0, 0.0).astype(F32)
    neg_a = -jnp.exp(alog_ref[...])
    dtb = dtb_ref[...]

    def conv_cols(c, r0, lo):
        x = qkv_ref[pl.ds(r0, CHUNK), lo:lo + LANES].astype(F32)
        prev0 = pl.multiple_of(jnp.maximum(r0 - 16, 0), 16)
        prev = qkv_ref[pl.ds(prev0, 16), lo:lo + LANES].astype(F32)
        prev = prev * jnp.where(c > 0, 1.0, 0.0)
        zf = jnp.concatenate([prev[8:16], x], axis=0)
        w = convw_ref[:, lo:lo + LANES]
        y = (w[3:4] * x + w[2:3] * zf[7:7 + CHUNK] + w[1:2] * zf[6:6 + CHUNK]
             + w[0:1] * zf[5:5 + CHUNK])
        return _silu(y)

    def l2n(x):
        return x * lax.rsqrt(jnp.sum(x * x, axis=-1, keepdims=True) + EPS)

    def prep(cc, carry):
        items = []
        for sub in range(GDN_PREP_CHUNKS):
            c = cc * GDN_PREP_CHUNKS + sub
            r0 = pl.multiple_of(c * CHUNK, CHUNK)
            ab = ab_ref[pl.ds(r0, CHUNK), :]
            xg = ab + dtb
            softplus = jnp.maximum(xg, 0.0) + jnp.log(1.0 + jnp.exp(-jnp.abs(xg)))
            g_all = neg_a * softplus
            beta_all = _sigmoid(ab)
            gc_all = jnp.dot(tri, g_all, preferred_element_type=F32, precision=lax.Precision.HIGHEST)
            gc_t = gc_all.T
            gl_s[c] = jnp.exp(gc_all[CHUNK - 1:CHUNK, :])
            for h in range(GDN_HEADS):
                q = l2n(conv_cols(c, r0, h * GDN_DK)) * (GDN_DK ** -0.5)
                k = l2n(conv_cols(c, r0, GDN_QK + h * GDN_DK))
                v = conv_cols(c, r0, 2 * GDN_QK + h * GDN_DV)
                gcol = gc_all[:, h:h + 1]
                grow = gc_t[h:h + 1, :]
                beta = beta_all[:, GDN_HEADS + h:GDN_HEADS + h + 1]
                decay = jnp.where(causal, jnp.exp(jnp.where(causal, gcol - grow, 0.0)), 0.0)
                eg = jnp.exp(gcol)
                kb = k * beta
                qg_s[c, h] = q * eg
                kd = k * jnp.exp(gc_all[CHUNK - 1:CHUNK, h:h + 1] - gcol)
                kdt_s[c, h] = kd.T
                items.append(dict(c=c, h=h, q=q, k=k, kb=kb, decay=decay,
                                  rhs=jnp.concatenate([v * beta, kb * eg], axis=1)))
        kks = [_dot_nt(it["kb"], it["k"]) for it in items]
        qks = [_dot_nt(it["q"], it["k"]) for it in items]
        lows = [jnp.where(strict, kk * it["decay"], 0.0) for kk, it in zip(kks, items)]
        for qk, it in zip(qks, items):
            attn_s[it["c"], it["h"]] = jnp.where(causal, qk * it["decay"], 0.0)
        ts = _unit_lower_inverses(lows, ii, jj)
        sols = [_dot(t, it["rhs"]) for t, it in zip(ts, items)]
        for sol, it in zip(sols, items):
            u_s[it["c"], it["h"]] = sol[:, :GDN_DV]
            w_s[it["c"], it["h"]] = sol[:, GDN_DV:]
        return carry

    lax.fori_loop(0, nchunk // GDN_PREP_CHUNKS, prep, 0)

    state_s[...] = jnp.zeros_like(state_s)
    normw = normw_ref[...]

    def scan(c, carry):
        r0 = pl.multiple_of(c * CHUNK, CHUNK)
        gl = gl_s[c]
        heads = range(GDN_HEADS)
        ss = [state_s[h] for h in heads]
        wss = [_dot(w_s[c, h], ss[h]) for h in heads]
        qss = [_dot(qg_s[c, h], ss[h]) for h in heads]
        vns = [u_s[c, h] - wss[h] for h in heads]
        avs = [_dot(attn_s[c, h], vns[h]) for h in heads]
        kvs = [_dot(kdt_s[c, h], vns[h]) for h in heads]
        for h in heads:
            state_s[h] = ss[h] * gl[:, h:h + 1] + kvs[h]
            o = qss[h] + avs[h]
            z = z_ref[pl.ds(r0, CHUNK), h * GDN_DV:(h + 1) * GDN_DV].astype(F32)
            on = o * lax.rsqrt(jnp.mean(o * o, axis=-1, keepdims=True) + EPS) * normw
            o_ref[pl.ds(r0, CHUNK), h * GDN_DV:(h + 1) * GDN_DV] = (on * _silu(z)).astype(BF16)
        return carry

    lax.fori_loop(0, nchunk, scan, 0)


def _gdn_call(proj, ab, conv_w8, alog_p, dtb_p, normw, batch, seq):
    nchunk = seq // CHUNK
    hs = (nchunk, GDN_HEADS, CHUNK, CHUNK)
    return pl.pallas_call(
        functools.partial(_gdn_kernel, seq=seq),
        grid=(batch,),
        in_specs=[
            pl.BlockSpec((seq, 3 * GDN_QK), lambda b: (b, 0)),
            pl.BlockSpec((seq, GDN_V), lambda b: (b, 3)),
            pl.BlockSpec((seq, LANES), lambda b: (b, 0)),
            pl.BlockSpec((8, 3 * GDN_QK), lambda b: (0, 0)),
            pl.BlockSpec((1, LANES), lambda b: (0, 0)),
            pl.BlockSpec((1, LANES), lambda b: (0, 0)),
            pl.BlockSpec((1, GDN_DV), lambda b: (0, 0)),
        ],
        out_specs=pl.BlockSpec((seq, GDN_V), lambda b: (b, 0)),
        out_shape=jax.ShapeDtypeStruct((batch * seq, GDN_V), BF16),
        scratch_shapes=[
            pltpu.VMEM(hs, F32), pltpu.VMEM(hs, F32), pltpu.VMEM(hs, F32), pltpu.VMEM(hs, F32),
            pltpu.VMEM(hs, F32),
            pltpu.VMEM((nchunk, 1, LANES), F32),
            pltpu.VMEM((GDN_HEADS, GDN_DK, GDN_DV), F32),
        ],
        compiler_params=pltpu.CompilerParams(
            dimension_semantics=("arbitrary",), vmem_limit_bytes=VMEM_LIMIT),
        name="gdn",
    )(proj, proj, ab, conv_w8, alog_p, dtb_p, normw)


def _ret_kernel(qk_ref, v_ref, g_ref, sin_ref, cos_ref, inner_ref, kdec_ref, qdec_ref, cdec_ref,
                o_ref, state_s, *, seq):
    nchunk = seq // CHUNK
    lane = lax.broadcasted_iota(jnp.int32, (CHUNK, RET_DK), 1)
    even = (lane % 2) == 0

    def rotate(x, sin, cos):
        nxt = pltpu.roll(x, RET_DK - 1, axis=1)
        prv = pltpu.roll(x, 1, axis=1)
        return x * cos + jnp.where(even, -nxt, prv) * sin

    state_s[...] = jnp.zeros_like(state_s)
    kdec = kdec_ref[...]
    qdec = qdec_ref[...]
    cdec = cdec_ref[...]

    def body(c, carry):
        r0 = pl.multiple_of(c * CHUNK, CHUNK)
        sin = sin_ref[pl.ds(r0, CHUNK), :]
        cos = cos_ref[pl.ds(r0, CHUNK), :]
        heads = range(RET_HEADS)
        qs = [rotate(qk_ref[pl.ds(r0, CHUNK), h * RET_DK:(h + 1) * RET_DK].astype(F32), sin, cos)
              for h in heads]
        ks = [rotate(qk_ref[pl.ds(r0, CHUNK), RET_QK + h * RET_DK:RET_QK + (h + 1) * RET_DK]
                     .astype(F32), sin, cos) * (RET_DK ** -0.5) for h in heads]
        vs = [v_ref[pl.ds(r0, CHUNK), h * RET_DV:(h + 1) * RET_DV].astype(F32) for h in heads]
        ss = [state_s[h] for h in heads]
        qks = [_dot_nt(qs[h], ks[h]) for h in heads]
        inters = [_dot(qs[h] * qdec[:, h:h + 1], ss[h]) for h in heads]
        kvs = [_dot((ks[h] * kdec[:, h:h + 1]).T, vs[h]) for h in heads]
        intras = [_dot(qks[h] * inner_ref[h], vs[h]) for h in heads]
        for h in heads:
            state_s[h] = ss[h] * cdec[:, h:h + 1] + kvs[h]
            o = intras[h] + inters[h]
            gate = g_ref[pl.ds(r0, CHUNK), h * RET_DV:(h + 1) * RET_DV].astype(F32)
            on = o * lax.rsqrt(jnp.mean(o * o, axis=-1, keepdims=True) + EPS)
            o_ref[pl.ds(r0, CHUNK), h * RET_DV:(h + 1) * RET_DV] = (on * _silu(gate)).astype(BF16)
        return carry

    lax.fori_loop(0, nchunk, body, 0)


def _ret_call(proj, sin, cos, inner, kdec, qdec, cdec, batch, seq):
    return pl.pallas_call(
        functools.partial(_ret_kernel, seq=seq),
        grid=(batch,),
        in_specs=[
            pl.BlockSpec((seq, 2 * RET_QK), lambda b: (b, 2)),
            pl.BlockSpec((seq, RET_V), lambda b: (b, 3)),
            pl.BlockSpec((seq, RET_V), lambda b: (b, 4)),
            pl.BlockSpec((seq, RET_DK), lambda b: (0, 0)),
            pl.BlockSpec((seq, RET_DK), lambda b: (0, 0)),
            pl.BlockSpec((RET_HEADS, CHUNK, CHUNK), lambda b: (0, 0, 0)),
            pl.BlockSpec((CHUNK, LANES), lambda b: (0, 0)),
            pl.BlockSpec((CHUNK, LANES), lambda b: (0, 0)),
            pl.BlockSpec((1, LANES), lambda b: (0, 0)),
        ],
        out_specs=pl.BlockSpec((seq, RET_V), lambda b: (b, 0)),
        out_shape=jax.ShapeDtypeStruct((batch * seq, RET_V), BF16),
        scratch_shapes=[pltpu.VMEM((RET_HEADS, RET_DK, RET_DV), F32)],
        compiler_params=pltpu.CompilerParams(
            dimension_semantics=("arbitrary",), vmem_limit_bytes=VMEM_LIMIT),
        name="retention",
    )(proj, proj, proj, sin, cos, inner, kdec, qdec, cdec)


def _retention_tables(seq):
    inv_freq = 1.0 / (ROPE_BASE ** jnp.linspace(0.0, 1.0, RET_DK // 2, dtype=F32))
    ang = jnp.arange(seq, dtype=F32)[:, None] * inv_freq[None, :]
    sin = jnp.repeat(jnp.sin(ang), 2, axis=-1)
    cos = jnp.repeat(jnp.cos(ang), 2, axis=-1)
    log_gamma = jnp.log(1.0 - 2.0 ** (-5.0 - jnp.arange(RET_HEADS, dtype=F32)))
    idx = jnp.arange(CHUNK, dtype=F32)
    causal = jnp.tril(jnp.ones((CHUNK, CHUNK), dtype=bool))
    rel = jnp.where(causal, idx[:, None] - idx[None, :], 0.0)
    inner = jnp.where(causal, jnp.exp(rel[None] * log_gamma[:, None, None]), 0.0)
    k_decay = jnp.exp(log_gamma[:, None] * (CHUNK - 1.0 - idx)[None, :])
    q_decay = jnp.exp(log_gamma[:, None] * (idx + 1.0)[None, :])
    chunk_decay = jnp.exp(log_gamma * CHUNK)
    pad = LANES - RET_HEADS
    kdec = jnp.pad(k_decay.T, ((0, 0), (0, pad)))
    qdec = jnp.pad(q_decay.T, ((0, 0), (0, pad)))
    cdec = jnp.pad(chunk_decay[None, :], ((0, 0), (0, pad)))
    return sin, cos, inner, kdec, qdec, cdec


def _merge_kernel(x_ref, ya_ref, yb_ref, ma_ref, mb_ref, wa_ref, wr_ref, wo_ref, nw_ref,
                  wg_ref, bg_ref, we_ref, be_ref, h_ref, xn_ref, ridx_ref, rw_ref):
    a = _dot(ya_ref[...], wa_ref[...])
    r = _dot(yb_ref[...], wr_ref[...])
    merged = _sigmoid(ma_ref[...].astype(F32)) * a + _sigmoid(mb_ref[...].astype(F32)) * r
    h = x_ref[...] + _dot(merged.astype(BF16), wo_ref[...])
    h_ref[...] = h
    xn = h * lax.rsqrt(jnp.mean(h * h, axis=-1, keepdims=True) + EPS) * nw_ref[...]
    xn_ref[...] = _pack_bf16_pairs(xn)

    hi = lax.Precision.HIGHEST
    lane = lax.broadcasted_iota(jnp.int32, (xn.shape[0], LANES), 1)
    neg = -jnp.inf
    gl = jnp.dot(xn, wg_ref[...], preferred_element_type=F32, precision=hi) + bg_ref[...]
    gl = jnp.where(lane < N_GROUPS, gl, neg)
    gmax = jnp.max(gl, axis=-1, keepdims=True)
    gidx = jnp.min(jnp.where(gl == gmax, lane, LANES), axis=-1, keepdims=True)
    g_w = 1.0 / jnp.sum(jnp.exp(gl - gmax), axis=-1, keepdims=True)
    el = jnp.dot(xn, we_ref[...], preferred_element_type=F32, precision=hi) + be_ref[...]
    el = jnp.where((lane // EXPERTS_PER_GROUP == gidx) & (lane < N_EXPERTS), el, neg)
    m1 = jnp.max(el, axis=-1, keepdims=True)
    i1 = jnp.min(jnp.where(el == m1, lane, LANES), axis=-1, keepdims=True)
    el2 = jnp.where(lane == i1, neg, el)
    m2 = jnp.max(el2, axis=-1, keepdims=True)
    i2 = jnp.min(jnp.where(el2 == m2, lane, LANES), axis=-1, keepdims=True)
    e2 = jnp.exp(m2 - m1)
    p1 = g_w / (1.0 + e2)
    p2 = g_w * e2 / (1.0 + e2)
    ridx_ref[...] = jnp.where(lane == 0, i1, jnp.where(lane == 1, i2, 0))
    rw_ref[...] = jnp.where(lane == 0, p1, jnp.where(lane == 1, p2, 0.0))


def _merge_call(x2, ya, yb, proj, wa, wr, wo, nw, wg, bg, we, be, tm=512):
    m = x2.shape[0]
    full = lambda shape: pl.BlockSpec(shape, lambda i: (0, 0))
    return pl.pallas_call(
        _merge_kernel,
        grid=(m // tm,),
        in_specs=[
            pl.BlockSpec((tm, D_MODEL), lambda i: (i, 0)),
            pl.BlockSpec((tm, GDN_V), lambda i: (i, 0)),
            pl.BlockSpec((tm, RET_V), lambda i: (i, 0)),
            pl.BlockSpec((tm, D_MODEL), lambda i: (i, 5)),
            pl.BlockSpec((tm, D_MODEL), lambda i: (i, 6)),
            full((GDN_V, D_MODEL)), full((RET_V, D_MODEL)), full((D_MODEL, D_MODEL)),
            full((1, D_MODEL)),
            full((D_MODEL, LANES)), full((1, LANES)), full((D_MODEL, LANES)), full((1, LANES)),
        ],
        out_specs=[
            pl.BlockSpec((tm, D_MODEL), lambda i: (i, 0)),
            pl.BlockSpec((tm, HALF), lambda i: (i, 0)),
            pl.BlockSpec((tm, LANES), lambda i: (i, 0)),
            pl.BlockSpec((tm, LANES), lambda i: (i, 0)),
        ],
        out_shape=[
            jax.ShapeDtypeStruct((m, D_MODEL), F32),
            jax.ShapeDtypeStruct((m, HALF), I32),
            jax.ShapeDtypeStruct((m, LANES), I32),
            jax.ShapeDtypeStruct((m, LANES), F32),
        ],
        compiler_params=pltpu.CompilerParams(
            dimension_semantics=("arbitrary",), vmem_limit_bytes=VMEM_LIMIT),
        name="merge_router",
    )(x2, ya, yb, proj, proj, wa, wr, wo, nw, wg, bg, we, be)


def _slot_counts(m):
    n_pad = N_EXPERTS * SLOT_TILE
    n_slots = TOP_K * m + n_pad
    return n_slots, n_slots // SLOT_TILE, n_pad


def _lane_prefix_sum(x, lane):
    s = 1
    while s < LANES:
        x = x + jnp.where(lane >= s, pltpu.roll(x, s, axis=1), 0.0)
        s *= 2
    return x


def _plan_kernel(ridx_ref, slots_ref, tile_ref, pad_ref, cnt_s, carry_s, off_s, *, n_slots):
    p = pl.program_id(0)
    i = pl.program_id(1)
    lane = lax.broadcasted_iota(I32, (PLAN_TILE, LANES), 1)
    row = lax.broadcasted_iota(I32, (PLAN_TILE, LANES), 0)
    lane1 = lane[0:1]
    e1 = ridx_ref[:, 0:1]
    e2 = ridx_ref[:, 1:2]
    onehot = jnp.where((lane == e1) | (lane == e2), 1.0, 0.0)
    colsum = jnp.sum(onehot, axis=0, keepdims=True)

    @pl.when((p == 0) & (i == 0))
    def _():
        cnt_s[...] = jnp.zeros_like(cnt_s)

    @pl.when(p == 0)
    def _():
        cnt_s[...] += colsum

    @pl.when((p == 1) & (i == 0))
    def _():
        cnt = cnt_s[...]
        tile = float(SLOT_TILE)
        padded = jnp.floor((cnt + (tile - 1.0)) / tile) * tile
        incl = _lane_prefix_sum(padded, lane1)
        off = incl - padded
        off_s[...] = off
        carry_s[...] = jnp.zeros_like(carry_s)
        first = (row * SLOT_TILE).astype(F32)
        ended = jnp.where((lane < N_EXPERTS) & (incl <= first), 1.0, 0.0)
        tile_ref[...] = jnp.broadcast_to(
            jnp.sum(ended, axis=-1, keepdims=True), (PLAN_TILE, LANES)).astype(I32)
        rowf = row.astype(F32)
        spill = (n_slots + lane * SLOT_TILE + row).astype(F32)
        pad_ref[...] = jnp.where(rowf < padded - cnt, off + cnt + rowf, spill).astype(I32)

    @pl.when(p == 1)
    def _():
        strict = jnp.where(row[:, 0:1] > lax.broadcasted_iota(I32, (PLAN_TILE, PLAN_TILE), 1),
                           1.0, 0.0).astype(BF16)
        rank = _dot(strict, onehot.astype(BF16)) + carry_s[...]
        pos = rank + off_s[...]
        s1 = jnp.sum(jnp.where(lane == e1, pos, 0.0), axis=-1, keepdims=True)
        s2 = jnp.sum(jnp.where(lane == e2, pos, 0.0), axis=-1, keepdims=True)
        slots_ref[...] = jnp.where(lane == 0, s1, jnp.where(lane == 1, s2, 0.0)).astype(I32)
        carry_s[...] += colsum


def _plan_call(ridx):
    m = ridx.shape[0]
    n_slots, n_tiles, _ = _slot_counts(m)
    assert n_tiles <= PLAN_TILE and SLOT_TILE == PLAN_TILE
    return pl.pallas_call(
        functools.partial(_plan_kernel, n_slots=n_slots),
        grid=(2, m // PLAN_TILE),
        in_specs=[pl.BlockSpec((PLAN_TILE, LANES), lambda p, i: (i, 0))],
        out_specs=[
            pl.BlockSpec((PLAN_TILE, LANES), lambda p, i: (i * p, 0)),
            pl.BlockSpec((PLAN_TILE, LANES), lambda p, i: (0, 0)),
            pl.BlockSpec((PLAN_TILE, LANES), lambda p, i: (0, 0)),
        ],
        out_shape=[
            jax.ShapeDtypeStruct((m, LANES), I32),
            jax.ShapeDtypeStruct((PLAN_TILE, LANES), I32),
            jax.ShapeDtypeStruct((PLAN_TILE, LANES), I32),
        ],
        scratch_shapes=[pltpu.VMEM((1, LANES), F32), pltpu.VMEM((1, LANES), F32),
                        pltpu.VMEM((1, LANES), F32)],
        compiler_params=pltpu.CompilerParams(dimension_semantics=("arbitrary", "arbitrary")),
        name="dispatch_plan",
    )(ridx)


def _sc_mesh():
    return plsc.VectorSubcoreMesh(core_axis_name="c", subcore_axis_name="s")


def _sc_worker():
    return lax.axis_index("s") * 2 + lax.axis_index("c")


def _sc_dispatch(xn, slot1, slot2, pad_slots, zeros):
    m = xn.shape[0]
    n_slots, _, n_pad = _slot_counts(m)
    per = m // SC_WORKERS
    pad_per = n_pad // SC_WORKERS

    @functools.partial(
        pl.kernel, mesh=_sc_mesh(),
        out_type=jax.ShapeDtypeStruct((n_slots + n_pad, HALF), I32),
        scratch_types=[pltpu.VMEM((SC_ROWS,), I32), pltpu.VMEM((SC_ROWS,), I32),
                       pltpu.VMEM((SC_ROWS, HALF), I32), pltpu.SemaphoreType.DMA],
        name="sc_dispatch")
    def k(x_hbm, s1_hbm, s2_hbm, p_hbm, z_hbm, o_hbm, i1_v, i2_v, rows_v, sem):
        wid = _sc_worker()

        @pl.loop(0, per // SC_ROWS)
        def _(ci):
            t0 = wid * per + ci * SC_ROWS
            pltpu.sync_copy(x_hbm.at[pl.ds(t0, SC_ROWS)], rows_v)
            pltpu.sync_copy(s1_hbm.at[pl.ds(t0, SC_ROWS)], i1_v)
            pltpu.sync_copy(s2_hbm.at[pl.ds(t0, SC_ROWS)], i2_v)
            c1 = pltpu.async_copy(rows_v, o_hbm.at[i1_v], sem)
            c2 = pltpu.async_copy(rows_v, o_hbm.at[i2_v], sem)
            c1.wait()
            c2.wait()

        pltpu.sync_copy(z_hbm, rows_v)

        @pl.loop(0, pad_per // SC_ROWS)
        def _(ci):
            t0 = wid * pad_per + ci * SC_ROWS
            pltpu.sync_copy(p_hbm.at[pl.ds(t0, SC_ROWS)], i1_v)
            pltpu.async_copy(rows_v, o_hbm.at[i1_v], sem).wait()

    return k(xn, slot1, slot2, pad_slots, zeros)


def _sc_collect(ys, slot1, slot2):
    m = slot1.shape[0]
    per = m // SC_WORKERS
    row = jax.ShapeDtypeStruct((m, HALF), I32)

    @functools.partial(
        pl.kernel, mesh=_sc_mesh(), out_type=[row, row],
        scratch_types=[pltpu.VMEM((SC_ROWS,), I32), pltpu.VMEM((SC_ROWS, HALF), I32),
                       pltpu.SemaphoreType.DMA],
        name="sc_collect")
    def k(y_hbm, s1_hbm, s2_hbm, g1_hbm, g2_hbm, i_v, rows_v, sem):
        wid = _sc_worker()

        @pl.loop(0, per // SC_ROWS)
        def _(ci):
            t0 = wid * per + ci * SC_ROWS
            for s_hbm, g_hbm in ((s1_hbm, g1_hbm), (s2_hbm, g2_hbm)):
                pltpu.sync_copy(s_hbm.at[pl.ds(t0, SC_ROWS)], i_v)
                pltpu.async_copy(y_hbm.at[i_v], rows_v, sem).wait()
                pltpu.sync_copy(rows_v, g_hbm.at[pl.ds(t0, SC_ROWS)])

    return k(ys, slot1, slot2)


def _expert_kernel(te_ref, xs_ref, wg_ref, wu_ref, wd_ref, ys_ref, wg_b, wu_b, wd_b):
    j = pl.program_id(0)
    e = te_ref[j]
    prev = te_ref[jnp.maximum(j - 1, 0)]

    @pl.when((j == 0) | (e != prev))
    def _():
        wg_b[...] = wg_ref[0].astype(BF16)
        wu_b[...] = wu_ref[0].astype(BF16)
        wd_b[...] = wd_ref[0].astype(BF16)

    @pl.when(e < N_EXPERTS)
    def _():
        x = _unpack_bf16_pairs(xs_ref[...]).astype(BF16)
        hid = _silu(_dot(x, wg_b[...])) * _dot(x, wu_b[...])
        ys_ref[...] = _pack_bf16_pairs(_dot(hid.astype(BF16), wd_b[...]))

    @pl.when(e >= N_EXPERTS)
    def _():
        ys_ref[...] = jnp.zeros_like(ys_ref)


def _expert_call(tile_expert, xs, wg, wu, wd, n_tiles):
    wmap = lambda j, te: (jnp.minimum(te[j], N_EXPERTS - 1), 0, 0)
    return pl.pallas_call(
        _expert_kernel,
        grid_spec=pltpu.PrefetchScalarGridSpec(
            num_scalar_prefetch=1,
            grid=(n_tiles,),
            in_specs=[
                pl.BlockSpec((SLOT_TILE, HALF), lambda j, te: (j, 0)),
                pl.BlockSpec((1, D_MODEL, D_EXPERT), wmap),
                pl.BlockSpec((1, D_MODEL, D_EXPERT), wmap),
                pl.BlockSpec((1, D_EXPERT, D_MODEL), wmap),
            ],
            out_specs=pl.BlockSpec((SLOT_TILE, HALF), lambda j, te: (j, 0)),
            scratch_shapes=[pltpu.VMEM((D_MODEL, D_EXPERT), BF16), pltpu.VMEM((D_MODEL, D_EXPERT), BF16),
                            pltpu.VMEM((D_EXPERT, D_MODEL), BF16)],
        ),
        out_shape=jax.ShapeDtypeStruct((n_tiles * SLOT_TILE, HALF), I32),
        compiler_params=pltpu.CompilerParams(
            dimension_semantics=("arbitrary",), vmem_limit_bytes=VMEM_LIMIT),
        name="experts",
    )(tile_expert, xs, wg, wu, wd)


def _final_kernel(h_ref, g1_ref, g2_ref, rw_ref, nw_ref, o_ref):
    rw = rw_ref[...]
    y = rw[:, 0:1] * _unpack_bf16_pairs(g1_ref[...]) + rw[:, 1:2] * _unpack_bf16_pairs(g2_ref[...])
    h = h_ref[...] + y
    o_ref[...] = h * lax.rsqrt(jnp.mean(h * h, axis=-1, keepdims=True) + EPS) * nw_ref[...]


def _final_call(h1, g1, g2, rw, nw, tm=512):
    m = h1.shape[0]
    return pl.pallas_call(
        _final_kernel,
        grid=(m // tm,),
        in_specs=[
            pl.BlockSpec((tm, D_MODEL), lambda i: (i, 0)),
            pl.BlockSpec((tm, HALF), lambda i: (i, 0)),
            pl.BlockSpec((tm, HALF), lambda i: (i, 0)),
            pl.BlockSpec((tm, LANES), lambda i: (i, 0)),
            pl.BlockSpec((1, D_MODEL), lambda i: (0, 0)),
        ],
        out_specs=pl.BlockSpec((tm, D_MODEL), lambda i: (i, 0)),
        out_shape=jax.ShapeDtypeStruct((m, D_MODEL), F32),
        compiler_params=pltpu.CompilerParams(dimension_semantics=("arbitrary",)),
        name="combine_final",
    )(h1, g1, g2, rw, nw)


def _pad_lanes(a):
    return jnp.pad(a, ((0, 0), (0, LANES - a.shape[1])))


def kernel(x, norm_mix_w, w_in, conv_w, A_log, dt_bias, gdn_norm_w, w_up_gdn, w_up_ret, w_out,
           norm_ffn_w, w_group, b_group, w_expert, b_expert, w_gate, w_up, w_down, norm_final_w):
    batch, seq, d = x.shape
    m = batch * seq
    h = x.reshape(m, d)
    depth = w_in.shape[0]
    sin, cos, inner, kdec, qdec, cdec = _retention_tables(seq)
    for l in range(depth):
        o_ab = 3 * GDN_QK
        o_z = o_ab + 2 * GDN_HEADS
        wl = w_in[l]
        w_main = jnp.concatenate([wl[:, :o_ab], wl[:, o_z:]], axis=1).astype(BF16)
        w_ab = _pad_lanes(wl[:, o_ab:o_z]).astype(BF16)
        proj, ab = _proj_call(h, norm_mix_w[l][None, :], w_main, w_ab)

        conv8 = jnp.pad(conv_w[l], ((0, 8 - GDN_CONV), (0, 0)))
        ya = _gdn_call(proj, ab, conv8, _pad_lanes(A_log[l][None, :]), _pad_lanes(dt_bias[l][None, :]),
                       gdn_norm_w[l][None, :], batch, seq)
        yb = _ret_call(proj, sin, cos, inner, kdec, qdec, cdec, batch, seq)

        h1, xn, ridx, rw = _merge_call(
            h, ya, yb, proj, w_up_gdn[l].astype(BF16), w_up_ret[l].astype(BF16), w_out[l].astype(BF16),
            norm_ffn_w[l][None, :], _pad_lanes(w_group[l]), _pad_lanes(b_group[l][None, :]),
            _pad_lanes(w_expert[l]), _pad_lanes(b_expert[l][None, :]))

        _, n_tiles, _ = _slot_counts(m)
        slots, tiles, pads = _plan_call(ridx)
        slot1, slot2 = slots[:, 0], slots[:, 1]
        pad_slots = pads[:, :N_EXPERTS].T.reshape(-1)
        xs = _sc_dispatch(xn, slot1, slot2, pad_slots, jnp.zeros((SC_ROWS, HALF), I32))
        ys = _expert_call(tiles[:n_tiles, 0], xs, w_gate[l], w_up[l], w_down[l], n_tiles)
        g1, g2 = _sc_collect(ys, slot1, slot2)

        assert depth == 1
        h = _final_call(h1, g1, g2, rw, norm_final_w[None, :])
    return h.reshape(batch, seq, d)
```

```python
import functools
import math

import jax
import jax.numpy as jnp
from jax import lax
from jax.experimental import pallas as pl
from jax.experimental.pallas import tpu as pltpu
from jax.experimental.pallas import tpu_sc as plsc

F32 = jnp.float32
BF16 = jnp.bfloat16
I32 = jnp.int32
U32 = jnp.uint32

D_MODEL = 1024
EPS = 1e-6
GDN_HEADS = 4
GDN_DK = 128
GDN_DV = 128
GDN_CONV = 4
RET_HEADS = 4
RET_DK = 128
RET_DV = 256
ROPE_BASE = 10000.0
N_GROUPS = 4
EXPERTS_PER_GROUP = 8
N_EXPERTS = N_GROUPS * EXPERTS_PER_GROUP
D_EXPERT = 512

GDN_QK = GDN_HEADS * GDN_DK
GDN_V = GDN_HEADS * GDN_DV
RET_QK = RET_HEADS * RET_DK
RET_V = RET_HEADS * RET_DV

LANES = 128
CHUNK = 128
INV_BLOCK = 16
GDN_PREP_CHUNKS = 2
VMEM_LIMIT = 56 * 1024 * 1024

TOP_K = 2
SLOT_TILE = 256
PLAN_TILE = 256
PLAN_STEP = 1024
HALF = D_MODEL // 2
SC_WORKERS = 32
SC_ROWS = 64

PROJ_COLS = 3 * GDN_QK + GDN_V + 2 * RET_QK + 2 * RET_V + 2 * D_MODEL


def _silu(x):
    return x / (1.0 + jnp.exp(-x))


def _sigmoid(x):
    return 1.0 / (1.0 + jnp.exp(-x))


def _dot(a, b):
    return jnp.dot(a, b, preferred_element_type=F32)


def _dot_nt(a, b):
    return lax.dot_general(a, b, (((1,), (1,)), ((), ())), preferred_element_type=F32)


def _pack_bf16_pairs(x):
    bits = lax.bitcast_convert_type(x.astype(BF16).astype(F32), U32)
    packed = (bits[:, :HALF] >> 16) | (bits[:, HALF:] & jnp.uint32(0xFFFF0000))
    return lax.bitcast_convert_type(packed, I32)


def _unpack_bf16_pairs(p):
    p = lax.bitcast_convert_type(p, U32)
    lo = lax.bitcast_convert_type(p << 16, F32)
    hi = lax.bitcast_convert_type(p & jnp.uint32(0xFFFF0000), F32)
    return jnp.concatenate([lo, hi], axis=1)


def _proj_kernel(x_ref, nw_ref, w_ref, wab_ref, proj_ref, ab_ref, u_ref):
    j = pl.program_id(1)

    @pl.when(j == 0)
    def _():
        x = x_ref[...]
        u = x * lax.rsqrt(jnp.mean(x * x, axis=-1, keepdims=True) + EPS) * nw_ref[...]
        ub = u.astype(BF16)
        u_ref[...] = ub
        ab_ref[...] = _dot(ub, wab_ref[...])

    proj_ref[...] = _dot(u_ref[...], w_ref[...]).astype(BF16)


def _proj_call(x2, norm_w, w_main, w_ab, tm=1024, tn=1024):
    m = x2.shape[0]
    return pl.pallas_call(
        _proj_kernel,
        grid=(m // tm, PROJ_COLS // tn),
        in_specs=[
            pl.BlockSpec((tm, D_MODEL), lambda i, j: (i, 0)),
            pl.BlockSpec((1, D_MODEL), lambda i, j: (0, 0)),
            pl.BlockSpec((D_MODEL, tn), lambda i, j: (0, j)),
            pl.BlockSpec((D_MODEL, LANES), lambda i, j: (0, 0)),
        ],
        out_specs=[
            pl.BlockSpec((tm, tn), lambda i, j: (i, j)),
            pl.BlockSpec((tm, LANES), lambda i, j: (i, 0)),
        ],
        out_shape=[
            jax.ShapeDtypeStruct((m, PROJ_COLS), BF16),
            jax.ShapeDtypeStruct((m, LANES), F32),
        ],
        scratch_shapes=[pltpu.VMEM((tm, D_MODEL), BF16)],
        compiler_params=pltpu.CompilerParams(
            dimension_semantics=("arbitrary", "arbitrary"), vmem_limit_bytes=VMEM_LIMIT),
        name="proj",
    )(x2, norm_w, w_main, w_ab)


def _unit_lower_inverses(lows, ii, jj):
    eye = jnp.where(ii == jj, 1.0, 0.0).astype(F32)
    in_block = (ii // INV_BLOCK) == (jj // INV_BLOCK)
    ps = [jnp.where(in_block, -low, 0.0) for low in lows]
    ts = [eye + p for p in ps]
    span = 2
    while span < INV_BLOCK:
        ps = [_dot(p, p) for p in ps]
        ts = [t + _dot(t, p) for t, p in zip(ts, ps)]
        span *= 2
    s = INV_BLOCK
    while s < CHUNK:
        off_diag = ((ii // (2 * s)) == (jj // (2 * s))) & ((ii // s) != (jj // s))
        xs = [_dot(jnp.where(off_diag, low, 0.0), t) for low, t in zip(lows, ts)]
        ts = [t - _dot(t, x) for t, x in zip(ts, xs)]
        s *= 2
    return ts


def _gdn_kernel(qkv_ref, z_ref, ab_ref, convw_ref, alog_ref, dtb_ref, normw_ref, o_ref,
                u_s, w_s, qg_s, kdt_s, attn_s, gl_s, state_s, *, seq):
    nchunk = seq // CHUNK
    ii = lax.broadcasted_iota(jnp.int32, (CHUNK, CHUNK), 0)
    jj = lax.broadcasted_iota(jnp.int32, (CHUNK, CHUNK), 1)
    causal = ii >= jj
    strict = ii > jj
    tri = jnp.where(causal, 1.0, 0.0).astype(F32)
    neg_a = -jnp.exp(alog_ref[...])
    dtb = dtb_ref[...]

    def conv_cols(c, r0, lo):
        x = qkv_ref[pl.ds(r0, CHUNK), lo:lo + LANES].astype(F32)
        prev0 = pl.multiple_of(jnp.maximum(r0 - 16, 0), 16)
        prev = qkv_ref[pl.ds(prev0, 16), lo:lo + LANES].astype(F32)
        prev = prev * jnp.where(c > 0, 1.0, 0.0)
        zf = jnp.concatenate([prev[8:16], x], axis=0)
        w = convw_ref[:, lo:lo + LANES]
        y = (w[3:4] * x + w[2:3] * zf[7:7 + CHUNK] + w[1:2] * zf[6:6 + CHUNK]
             + w[0:1] * zf[5:5 + CHUNK])
        return _silu(y)

    def l2n(x):
        return x * lax.rsqrt(jnp.sum(x * x, axis=-1, keepdims=True) + EPS)

    def prep(cc, carry):
        items = []
        for sub in range(GDN_PREP_CHUNKS):
            c = cc * GDN_PREP_CHUNKS + sub
            r0 = pl.multiple_of(c * CHUNK, CHUNK)
            ab = ab_ref[pl.ds(r0, CHUNK), :]
            xg = ab + dtb
            softplus = jnp.maximum(xg, 0.0) + jnp.log(1.0 + jnp.exp(-jnp.abs(xg)))
            g_all = neg_a * softplus
            beta_all = _sigmoid(ab)
            gc_all = jnp.dot(tri, g_all, preferred_element_type=F32, precision=lax.Precision.HIGHEST)
            gc_t = gc_all.T
            gl_s[c] = jnp.exp(gc_all[CHUNK - 1:CHUNK, :])
            for h in range(GDN_HEADS):
                q = l2n(conv_cols(c, r0, h * GDN_DK)) * (GDN_DK ** -0.5)
                k = l2n(conv_cols(c, r0, GDN_QK + h * GDN_DK))
                v = conv_cols(c, r0, 2 * GDN_QK + h * GDN_DV)
                gcol = gc_all[:, h:h + 1]
                grow = gc_t[h:h + 1, :]
                beta = beta_all[:, GDN_HEADS + h:GDN_HEADS + h + 1]
                decay = jnp.where(causal, jnp.exp(jnp.where(causal, gcol - grow, 0.0)), 0.0)
                eg = jnp.exp(gcol)
                kb = k * beta
                qg_s[c, h] = q * eg
                kd = k * jnp.exp(gc_all[CHUNK - 1:CHUNK, h:h + 1] - gcol)
                kdt_s[c, h] = kd.T
                items.append(dict(c=c, h=h, q=q, k=k, kb=kb, decay=decay,
                                  rhs=jnp.concatenate([v * beta, kb * eg], axis=1)))
        kks = [_dot_nt(it["kb"], it["k"]) for it in items]
        qks = [_dot_nt(it["q"], it["k"]) for it in items]
        lows = [jnp.where(strict, kk * it["decay"], 0.0) for kk, it in zip(kks, items)]
        for qk, it in zip(qks, items):
            attn_s[it["c"], it["h"]] = jnp.where(causal, qk * it["decay"], 0.0)
        ts = _unit_lower_inverses(lows, ii, jj)
        sols = [_dot(t, it["rhs"]) for t, it in zip(ts, items)]
        for sol, it in zip(sols, items):
            u_s[it["c"], it["h"]] = sol[:, :GDN_DV]
            w_s[it["c"], it["h"]] = sol[:, GDN_DV:]
        return carry

    lax.fori_loop(0, nchunk // GDN_PREP_CHUNKS, prep, 0)

    state_s[...] = jnp.zeros_like(state_s)
    normw = normw_ref[...]

    def scan(c, carry):
        r0 = pl.multiple_of(c * CHUNK, CHUNK)
        gl = gl_s[c]
        heads = range(GDN_HEADS)
        ss = [state_s[h] for h in heads]
        wss = [_dot(w_s[c, h], ss[h]) for h in heads]
        qss = [_dot(qg_s[c, h], ss[h]) for h in heads]
        vns = [u_s[c, h] - wss[h] for h in heads]
        avs = [_dot(attn_s[c, h], vns[h]) for h in heads]
        kvs = [_dot(kdt_s[c, h], vns[h]) for h in heads]
        for h in heads:
            state_s[h] = ss[h] * gl[:, h:h + 1] + kvs[h]
            o = qss[h] + avs[h]
            z = z_ref[pl.ds(r0, CHUNK), h * GDN_DV:(h + 1) * GDN_DV].astype(F32)
            on = o * lax.rsqrt(jnp.mean(o * o, axis=-1, keepdims=True) + EPS) * normw
            o_ref[pl.ds(r0, CHUNK), h * GDN_DV:(h + 1) * GDN_DV] = (on * _silu(z)).astype(BF16)
        return carry

    lax.fori_loop(0, nchunk, scan, 0)


def _gdn_call(proj, ab, conv_w8, alog_p, dtb_p, normw, batch, seq):
    nchunk = seq // CHUNK
    hs = (nchunk, GDN_HEADS, CHUNK, CHUNK)
    return pl.pallas_call(
        functools.partial(_gdn_kernel, seq=seq),
        grid=(batch,),
        in_specs=[
            pl.BlockSpec((seq, 3 * GDN_QK), lambda b: (b, 0)),
            pl.BlockSpec((seq, GDN_V), lambda b: (b, 3)),
            pl.BlockSpec((seq, LANES), lambda b: (b, 0)),
            pl.BlockSpec((8, 3 * GDN_QK), lambda b: (0, 0)),
            pl.BlockSpec((1, LANES), lambda b: (0, 0)),
            pl.BlockSpec((1, LANES), lambda b: (0, 0)),
            pl.BlockSpec((1, GDN_DV), lambda b: (0, 0)),
        ],
        out_specs=pl.BlockSpec((seq, GDN_V), lambda b: (b, 0)),
        out_shape=jax.ShapeDtypeStruct((batch * seq, GDN_V), BF16),
        scratch_shapes=[
            pltpu.VMEM(hs, F32), pltpu.VMEM(hs, F32), pltpu.VMEM(hs, F32), pltpu.VMEM(hs, F32),
            pltpu.VMEM(hs, F32),
            pltpu.VMEM((nchunk, 1, LANES), F32),
            pltpu.VMEM((GDN_HEADS, GDN_DK, GDN_DV), F32),
        ],
        compiler_params=pltpu.CompilerParams(
            dimension_semantics=("arbitrary",), vmem_limit_bytes=VMEM_LIMIT),
        name="gdn",
    )(proj, proj, ab, conv_w8, alog_p, dtb_p, normw)


def _ret_kernel(qk_ref, v_ref, g_ref, sin_ref, cos_ref, inner_ref, kdec_ref, qdec_ref, cdec_ref,
                o_ref, state_s, *, seq):
    nchunk = seq // CHUNK
    lane = lax.broadcasted_iota(jnp.int32, (CHUNK, RET_DK), 1)
    even = (lane % 2) == 0

    def rotate(x, sin, cos):
        nxt = pltpu.roll(x, RET_DK - 1, axis=1)
        prv = pltpu.roll(x, 1, axis=1)
        return x * cos + jnp.where(even, -nxt, prv) * sin

    state_s[...] = jnp.zeros_like(state_s)
    kdec = kdec_ref[...]
    qdec = qdec_ref[...]
    cdec = cdec_ref[...]

    def body(c, carry):
        r0 = pl.multiple_of(c * CHUNK, CHUNK)
        sin = sin_ref[pl.ds(r0, CHUNK), :]
        cos = cos_ref[pl.ds(r0, CHUNK), :]
        heads = range(RET_HEADS)
        qs = [rotate(qk_ref[pl.ds(r0, CHUNK), h * RET_DK:(h + 1) * RET_DK].astype(F32), sin, cos)
              for h in heads]
        ks = [rotate(qk_ref[pl.ds(r0, CHUNK), RET_QK + h * RET_DK:RET_QK + (h + 1) * RET_DK]
                     .astype(F32), sin, cos) * (RET_DK ** -0.5) for h in heads]
        vs = [v_ref[pl.ds(r0, CHUNK), h * RET_DV:(h + 1) * RET_DV].astype(F32) for h in heads]
        ss = [state_s[h] for h in heads]
        qks = [_dot_nt(qs[h], ks[h]) for h in heads]
        inters = [_dot(qs[h] * qdec[:, h:h + 1], ss[h]) for h in heads]
        kvs = [_dot((ks[h] * kdec[:, h:h + 1]).T, vs[h]) for h in heads]
        intras = [_dot(qks[h] * inner_ref[h], vs[h]) for h in heads]
        for h in heads:
            state_s[h] = ss[h] * cdec[:, h:h + 1] + kvs[h]
            o = intras[h] + inters[h]
            gate = g_ref[pl.ds(r0, CHUNK), h * RET_DV:(h + 1) * RET_DV].astype(F32)
            on = o * lax.rsqrt(jnp.mean(o * o, axis=-1, keepdims=True) + EPS)
            o_ref[pl.ds(r0, CHUNK), h * RET_DV:(h + 1) * RET_DV] = (on * _silu(gate)).astype(BF16)
        return carry

    lax.fori_loop(0, nchunk, body, 0)


def _ret_call(proj, sin, cos, inner, kdec, qdec, cdec, batch, seq):
    return pl.pallas_call(
        functools.partial(_ret_kernel, seq=seq),
        grid=(batch,),
        in_specs=[
            pl.BlockSpec((seq, 2 * RET_QK), lambda b: (b, 2)),
            pl.BlockSpec((seq, RET_V), lambda b: (b, 3)),
            pl.BlockSpec((seq, RET_V), lambda b: (b, 4)),
            pl.BlockSpec((seq, RET_DK), lambda b: (0, 0)),
            pl.BlockSpec((seq, RET_DK), lambda b: (0, 0)),
            pl.BlockSpec((RET_HEADS, CHUNK, CHUNK), lambda b: (0, 0, 0)),
            pl.BlockSpec((CHUNK, LANES), lambda b: (0, 0)),
            pl.BlockSpec((CHUNK, LANES), lambda b: (0, 0)),
            pl.BlockSpec((1, LANES), lambda b: (0, 0)),
        ],
        out_specs=pl.BlockSpec((seq, RET_V), lambda b: (b, 0)),
        out_shape=jax.ShapeDtypeStruct((batch * seq, RET_V), BF16),
        scratch_shapes=[pltpu.VMEM((RET_HEADS, RET_DK, RET_DV), F32)],
        compiler_params=pltpu.CompilerParams(
            dimension_semantics=("arbitrary",), vmem_limit_bytes=VMEM_LIMIT),
        name="retention",
    )(proj, proj, proj, sin, cos, inner, kdec, qdec, cdec)


def _retention_tables(seq):
    inv_freq = 1.0 / (ROPE_BASE ** jnp.linspace(0.0, 1.0, RET_DK // 2, dtype=F32))
    ang = jnp.arange(seq, dtype=F32)[:, None] * inv_freq[None, :]
    sin = jnp.repeat(jnp.sin(ang), 2, axis=-1)
    cos = jnp.repeat(jnp.cos(ang), 2, axis=-1)
    log_gamma = jnp.log(1.0 - 2.0 ** (-5.0 - jnp.arange(RET_HEADS, dtype=F32)))
    idx = jnp.arange(CHUNK, dtype=F32)
    causal = jnp.tril(jnp.ones((CHUNK, CHUNK), dtype=bool))
    rel = jnp.where(causal, idx[:, None] - idx[None, :], 0.0)
    inner = jnp.where(causal, jnp.exp(rel[None] * log_gamma[:, None, None]), 0.0)
    k_decay = jnp.exp(log_gamma[:, None] * (CHUNK - 1.0 - idx)[None, :])
    q_decay = jnp.exp(log_gamma[:, None] * (idx + 1.0)[None, :])
    chunk_decay = jnp.exp(log_gamma * CHUNK)
    pad = LANES - RET_HEADS
    kdec = jnp.pad(k_decay.T, ((0, 0), (0, pad)))
    qdec = jnp.pad(q_decay.T, ((0, 0), (0, pad)))
    cdec = jnp.pad(chunk_decay[None, :], ((0, 0), (0, pad)))
    return sin, cos, inner, kdec, qdec, cdec


def _merge_kernel(x_ref, ya_ref, yb_ref, ma_ref, mb_ref, wa_ref, wr_ref, wo_ref, nw_ref,
                  wrh_ref, wrl_ref, br_ref, h_ref, xn_ref, ridx_ref, rw_ref, cnt_ref):
    a = _dot(ya_ref[...], wa_ref[...])
    r = _dot(yb_ref[...], wr_ref[...])
    merged = _sigmoid(ma_ref[...].astype(F32)) * a + _sigmoid(mb_ref[...].astype(F32)) * r
    h = x_ref[...] + _dot(merged.astype(BF16), wo_ref[...])
    h_ref[...] = h
    xn = h * lax.rsqrt(jnp.mean(h * h, axis=-1, keepdims=True) + EPS) * nw_ref[...]
    xn_ref[...] = _pack_bf16_pairs(xn)

    xh = xn.astype(BF16)
    xl = (xn - xh.astype(F32)).astype(BF16)
    logits = (_dot(xh, wrh_ref[...]) + (_dot(xl, wrh_ref[...]) + _dot(xh, wrl_ref[...])
                                        + _dot(xl, wrl_ref[...]))) + br_ref[...]
    lane = lax.broadcasted_iota(jnp.int32, (xn.shape[0], LANES), 1)
    neg = -jnp.inf
    gl = jnp.where((lane >= N_EXPERTS) & (lane < N_EXPERTS + N_GROUPS), logits, neg)
    gmax = jnp.max(gl, axis=-1, keepdims=True)
    gidx = jnp.min(jnp.where(gl == gmax, lane, LANES), axis=-1, keepdims=True) - N_EXPERTS
    g_w = 1.0 / jnp.sum(jnp.exp(gl - gmax), axis=-1, keepdims=True)
    el = jnp.where((lane // EXPERTS_PER_GROUP == gidx) & (lane < N_EXPERTS), logits, neg)
    m1 = jnp.max(el, axis=-1, keepdims=True)
    i1 = jnp.min(jnp.where(el == m1, lane, LANES), axis=-1, keepdims=True)
    el2 = jnp.where(lane == i1, neg, el)
    m2 = jnp.max(el2, axis=-1, keepdims=True)
    i2 = jnp.min(jnp.where(el2 == m2, lane, LANES), axis=-1, keepdims=True)
    e2 = jnp.exp(m2 - m1)
    p1 = g_w / (1.0 + e2)
    p2 = g_w * e2 / (1.0 + e2)
    ridx_ref[...] = jnp.where(lane == 0, i1, jnp.where(lane == 1, i2, 0))
    rw_ref[...] = jnp.where(lane == 0, p1, jnp.where(lane == 1, p2, 0.0))

    @pl.when(pl.program_id(0) == 0)
    def _():
        cnt_ref[...] = jnp.zeros_like(cnt_ref)

    onehot = jnp.where((lane == i1) | (lane == i2), 1.0, 0.0)
    cnt_ref[...] += jnp.broadcast_to(jnp.sum(onehot, axis=0, keepdims=True), cnt_ref.shape)


def _merge_call(x2, ya, yb, proj, wa, wr, wo, nw, wr_hi, wr_lo, b_router, tm=512):
    m = x2.shape[0]
    full = lambda shape: pl.BlockSpec(shape, lambda i: (0, 0))
    return pl.pallas_call(
        _merge_kernel,
        grid=(m // tm,),
        in_specs=[
            pl.BlockSpec((tm, D_MODEL), lambda i: (i, 0)),
            pl.BlockSpec((tm, GDN_V), lambda i: (i, 0)),
            pl.BlockSpec((tm, RET_V), lambda i: (i, 0)),
            pl.BlockSpec((tm, D_MODEL), lambda i: (i, 5)),
            pl.BlockSpec((tm, D_MODEL), lambda i: (i, 6)),
            full((GDN_V, D_MODEL)), full((RET_V, D_MODEL)), full((D_MODEL, D_MODEL)),
            full((1, D_MODEL)),
            full((D_MODEL, LANES)), full((D_MODEL, LANES)), full((1, LANES)),
        ],
        out_specs=[
            pl.BlockSpec((tm, D_MODEL), lambda i: (i, 0)),
            pl.BlockSpec((tm, HALF), lambda i: (i, 0)),
            pl.BlockSpec((tm, LANES), lambda i: (i, 0)),
            pl.BlockSpec((tm, LANES), lambda i: (i, 0)),
            pl.BlockSpec((8, LANES), lambda i: (0, 0)),
        ],
        out_shape=[
            jax.ShapeDtypeStruct((m, D_MODEL), F32),
            jax.ShapeDtypeStruct((m, HALF), I32),
            jax.ShapeDtypeStruct((m, LANES), I32),
            jax.ShapeDtypeStruct((m, LANES), F32),
            jax.ShapeDtypeStruct((8, LANES), F32),
        ],
        compiler_params=pltpu.CompilerParams(
            dimension_semantics=("arbitrary",), vmem_limit_bytes=VMEM_LIMIT),
        name="merge_router",
    )(x2, ya, yb, proj, proj, wa, wr, wo, nw, wr_hi, wr_lo, b_router)


def _slot_counts(m):
    n_pad = N_EXPERTS * SLOT_TILE
    n_slots = TOP_K * m + n_pad
    return n_slots, n_slots // SLOT_TILE, n_pad


def _lane_prefix_sum(x, lane):
    s = 1
    while s < LANES:
        x = x + jnp.where(lane >= s, pltpu.roll(x, s, axis=1), 0.0)
        s *= 2
    return x


def _plan_kernel(ridx_ref, cnt_ref, slots_ref, tile_ref, pad_ref, carry_s, off_s, *, n_slots):
    i = pl.program_id(0)
    lane = lax.broadcasted_iota(I32, (PLAN_TILE, LANES), 1)
    row = lax.broadcasted_iota(I32, (PLAN_TILE, LANES), 0)
    lane1 = lane[0:1]

    @pl.when(i == 0)
    def _():
        cnt = cnt_ref[0:1, :]
        tile = float(SLOT_TILE)
        padded = jnp.floor((cnt + (tile - 1.0)) / tile) * tile
        incl = _lane_prefix_sum(padded, lane1)
        off = incl - padded
        off_s[...] = off
        carry_s[...] = jnp.zeros_like(carry_s)
        first = (row * SLOT_TILE).astype(F32)
        ended = jnp.where((lane < N_EXPERTS) & (incl <= first), 1.0, 0.0)
        tile_ref[...] = jnp.broadcast_to(
            jnp.sum(ended, axis=-1, keepdims=True), (PLAN_TILE, LANES)).astype(I32)
        rowf = row.astype(F32)
        spill = (n_slots + lane * SLOT_TILE + row).astype(F32)
        pad_ref[...] = jnp.where(rowf < padded - cnt, off + cnt + rowf, spill).astype(I32)

    strict = jnp.where(row[:, 0:1] > lax.broadcasted_iota(I32, (PLAN_TILE, PLAN_TILE), 1),
                       1.0, 0.0).astype(BF16)
    off = off_s[...]
    carry = carry_s[...]
    for sb in range(PLAN_STEP // PLAN_TILE):
        rows = pl.ds(sb * PLAN_TILE, PLAN_TILE)
        e1 = ridx_ref[rows, 0:1]
        e2 = ridx_ref[rows, 1:2]
        onehot = jnp.where((lane == e1) | (lane == e2), 1.0, 0.0)
        pos = _dot(strict, onehot.astype(BF16)) + (carry + off)
        s1 = jnp.sum(jnp.where(lane == e1, pos, 0.0), axis=-1, keepdims=True)
        s2 = jnp.sum(jnp.where(lane == e2, pos, 0.0), axis=-1, keepdims=True)
        both = jnp.where(lane == 0, s1, jnp.where(lane == 1, s2, 0.0))
        for q in range(PLAN_TILE // LANES):
            t = both[q * LANES:(q + 1) * LANES].T
            c0 = sb * PLAN_TILE + q * LANES
            slots_ref[:, c0:c0 + LANES] = t[0:8].astype(I32)
        carry = carry + jnp.sum(onehot, axis=0, keepdims=True)
    carry_s[...] = carry


def _plan_call(ridx, cnt):
    m = ridx.shape[0]
    n_slots, n_tiles, _ = _slot_counts(m)
    assert n_tiles <= PLAN_TILE and SLOT_TILE == PLAN_TILE
    return pl.pallas_call(
        functools.partial(_plan_kernel, n_slots=n_slots),
        grid=(m // PLAN_STEP,),
        in_specs=[pl.BlockSpec((PLAN_STEP, LANES), lambda i: (i, 0)),
                  pl.BlockSpec((8, LANES), lambda i: (0, 0))],
        out_specs=[
            pl.BlockSpec((8, PLAN_STEP), lambda i: (0, i)),
            pl.BlockSpec((PLAN_TILE, LANES), lambda i: (0, 0)),
            pl.BlockSpec((PLAN_TILE, LANES), lambda i: (0, 0)),
        ],
        out_shape=[
            jax.ShapeDtypeStruct((8, m), I32),
            jax.ShapeDtypeStruct((PLAN_TILE, LANES), I32),
            jax.ShapeDtypeStruct((PLAN_TILE, LANES), I32),
        ],
        scratch_shapes=[pltpu.VMEM((1, LANES), F32), pltpu.VMEM((1, LANES), F32)],
        compiler_params=pltpu.CompilerParams(dimension_semantics=("arbitrary",)),
        name="dispatch_plan",
    )(ridx, cnt)


def _sc_mesh():
    return plsc.VectorSubcoreMesh(core_axis_name="c", subcore_axis_name="s")


def _sc_worker():
    return lax.axis_index("s") * 2 + lax.axis_index("c")


def _sc_dispatch(xn, slot1, slot2, pad_slots, zeros):
    m = xn.shape[0]
    n_slots, _, n_pad = _slot_counts(m)
    per = m // SC_WORKERS
    pad_per = n_pad // SC_WORKERS

    @functools.partial(
        pl.kernel, mesh=_sc_mesh(),
        out_type=jax.ShapeDtypeStruct((n_slots + n_pad, HALF), I32),
        scratch_types=[pltpu.VMEM((SC_ROWS,), I32), pltpu.VMEM((SC_ROWS,), I32),
                       pltpu.VMEM((SC_ROWS, HALF), I32), pltpu.SemaphoreType.DMA],
        name="sc_dispatch")
    def k(x_hbm, s1_hbm, s2_hbm, p_hbm, z_hbm, o_hbm, i1_v, i2_v, rows_v, sem):
        wid = _sc_worker()

        @pl.loop(0, per // SC_ROWS)
        def _(ci):
            t0 = wid * per + ci * SC_ROWS
            pltpu.sync_copy(x_hbm.at[pl.ds(t0, SC_ROWS)], rows_v)
            pltpu.sync_copy(s1_hbm.at[pl.ds(t0, SC_ROWS)], i1_v)
            pltpu.sync_copy(s2_hbm.at[pl.ds(t0, SC_ROWS)], i2_v)
            c1 = pltpu.async_copy(rows_v, o_hbm.at[i1_v], sem)
            c2 = pltpu.async_copy(rows_v, o_hbm.at[i2_v], sem)
            c1.wait()
            c2.wait()

        pltpu.sync_copy(z_hbm, rows_v)

        @pl.loop(0, pad_per // SC_ROWS)
        def _(ci):
            t0 = wid * pad_per + ci * SC_ROWS
            pltpu.sync_copy(p_hbm.at[pl.ds(t0, SC_ROWS)], i1_v)
            pltpu.async_copy(rows_v, o_hbm.at[i1_v], sem).wait()

    return k(xn, slot1, slot2, pad_slots, zeros)


def _sc_collect(ys, slot1, slot2):
    m = slot1.shape[0]
    per = m // SC_WORKERS
    row = jax.ShapeDtypeStruct((m, HALF), I32)

    @functools.partial(
        pl.kernel, mesh=_sc_mesh(), out_type=[row, row],
        scratch_types=[pltpu.VMEM((SC_ROWS,), I32), pltpu.VMEM((SC_ROWS, HALF), I32),
                       pltpu.SemaphoreType.DMA],
        name="sc_collect")
    def k(y_hbm, s1_hbm, s2_hbm, g1_hbm, g2_hbm, i_v, rows_v, sem):
        wid = _sc_worker()

        @pl.loop(0, per // SC_ROWS)
        def _(ci):
            t0 = wid * per + ci * SC_ROWS
            for s_hbm, g_hbm in ((s1_hbm, g1_hbm), (s2_hbm, g2_hbm)):
                pltpu.sync_copy(s_hbm.at[pl.ds(t0, SC_ROWS)], i_v)
                pltpu.async_copy(y_hbm.at[i_v], rows_v, sem).wait()
                pltpu.sync_copy(rows_v, g_hbm.at[pl.ds(t0, SC_ROWS)])

    return k(ys, slot1, slot2)


def _expert_kernel(te_ref, xs_ref, wg_ref, wu_ref, wd_ref, ys_ref, wg_b, wu_b, wd_b):
    j = pl.program_id(0)
    e = te_ref[j]
    prev = te_ref[jnp.maximum(j - 1, 0)]

    @pl.when((j == 0) | (e != prev))
    def _():
        wg_b[...] = wg_ref[0].astype(BF16)
        wu_b[...] = wu_ref[0].astype(BF16)
        wd_b[...] = wd_ref[0].astype(BF16)

    @pl.when(e < N_EXPERTS)
    def _():
        x = _unpack_bf16_pairs(xs_ref[...]).astype(BF16)
        hid = _silu(_dot(x, wg_b[...])) * _dot(x, wu_b[...])
        ys_ref[...] = _pack_bf16_pairs(_dot(hid.astype(BF16), wd_b[...]))

    @pl.when(e >= N_EXPERTS)
    def _():
        ys_ref[...] = jnp.zeros_like(ys_ref)


def _expert_call(tile_expert, xs, wg, wu, wd, n_tiles):
    wmap = lambda j, te: (jnp.minimum(te[j], N_EXPERTS - 1), 0, 0)
    return pl.pallas_call(
        _expert_kernel,
        grid_spec=pltpu.PrefetchScalarGridSpec(
            num_scalar_prefetch=1,
            grid=(n_tiles,),
            in_specs=[
                pl.BlockSpec((SLOT_TILE, HALF), lambda j, te: (j, 0)),
                pl.BlockSpec((1, D_MODEL, D_EXPERT), wmap),
                pl.BlockSpec((1, D_MODEL, D_EXPERT), wmap),
                pl.BlockSpec((1, D_EXPERT, D_MODEL), wmap),
            ],
            out_specs=pl.BlockSpec((SLOT_TILE, HALF), lambda j, te: (j, 0)),
            scratch_shapes=[pltpu.VMEM((D_MODEL, D_EXPERT), BF16), pltpu.VMEM((D_MODEL, D_EXPERT), BF16),
                            pltpu.VMEM((D_EXPERT, D_MODEL), BF16)],
        ),
        out_shape=jax.ShapeDtypeStruct((n_tiles * SLOT_TILE, HALF), I32),
        compiler_params=pltpu.CompilerParams(
            dimension_semantics=("arbitrary",), vmem_limit_bytes=VMEM_LIMIT),
        name="experts",
    )(tile_expert, xs, wg, wu, wd)


def _final_kernel(h_ref, g1_ref, g2_ref, rw_ref, nw_ref, o_ref):
    rw = rw_ref[...]
    y = rw[:, 0:1] * _unpack_bf16_pairs(g1_ref[...]) + rw[:, 1:2] * _unpack_bf16_pairs(g2_ref[...])
    h = h_ref[...] + y
    o_ref[...] = h * lax.rsqrt(jnp.mean(h * h, axis=-1, keepdims=True) + EPS) * nw_ref[...]


def _final_call(h1, g1, g2, rw, nw, tm=512):
    m = h1.shape[0]
    return pl.pallas_call(
        _final_kernel,
        grid=(m // tm,),
        in_specs=[
            pl.BlockSpec((tm, D_MODEL), lambda i: (i, 0)),
            pl.BlockSpec((tm, HALF), lambda i: (i, 0)),
            pl.BlockSpec((tm, HALF), lambda i: (i, 0)),
            pl.BlockSpec((tm, LANES), lambda i: (i, 0)),
            pl.BlockSpec((1, D_MODEL), lambda i: (0, 0)),
        ],
        out_specs=pl.BlockSpec((tm, D_MODEL), lambda i: (i, 0)),
        out_shape=jax.ShapeDtypeStruct((m, D_MODEL), F32),
        compiler_params=pltpu.CompilerParams(dimension_semantics=("arbitrary",)),
        name="combine_final",
    )(h1, g1, g2, rw, nw)


def _pad_lanes(a):
    return jnp.pad(a, ((0, 0), (0, LANES - a.shape[1])))


def kernel(x, norm_mix_w, w_in, conv_w, A_log, dt_bias, gdn_norm_w, w_up_gdn, w_up_ret, w_out,
           norm_ffn_w, w_group, b_group, w_expert, b_expert, w_gate, w_up, w_down, norm_final_w):
    batch, seq, d = x.shape
    m = batch * seq
    h = x.reshape(m, d)
    depth = w_in.shape[0]
    sin, cos, inner, kdec, qdec, cdec = _retention_tables(seq)
    for l in range(depth):
        o_ab = 3 * GDN_QK
        o_z = o_ab + 2 * GDN_HEADS
        wl = w_in[l]
        w_main = jnp.concatenate([wl[:, :o_ab], wl[:, o_z:]], axis=1).astype(BF16)
        w_ab = _pad_lanes(wl[:, o_ab:o_z]).astype(BF16)
        proj, ab = _proj_call(h, norm_mix_w[l][None, :], w_main, w_ab)

        conv8 = jnp.pad(conv_w[l], ((0, 8 - GDN_CONV), (0, 0)))
        ya = _gdn_call(proj, ab, conv8, _pad_lanes(A_log[l][None, :]), _pad_lanes(dt_bias[l][None, :]),
                       gdn_norm_w[l][None, :], batch, seq)
        yb = _ret_call(proj, sin, cos, inner, kdec, qdec, cdec, batch, seq)

        w_router = _pad_lanes(jnp.concatenate([w_expert[l], w_group[l]], axis=1))
        wr_hi = w_router.astype(BF16)
        wr_lo = (w_router - wr_hi.astype(F32)).astype(BF16)
        b_router = _pad_lanes(jnp.concatenate([b_expert[l], b_group[l]])[None, :])
        h1, xn, ridx, rw, cnt = _merge_call(
            h, ya, yb, proj, w_up_gdn[l].astype(BF16), w_up_ret[l].astype(BF16), w_out[l].astype(BF16),
            norm_ffn_w[l][None, :], wr_hi, wr_lo, b_router)

        _, n_tiles, _ = _slot_counts(m)
        slots, tiles, pads = _plan_call(ridx, cnt)
        slot1, slot2 = slots[0], slots[1]
        pad_slots = pads[:, :N_EXPERTS].T.reshape(-1)
        xs = _sc_dispatch(xn, slot1, slot2, pad_slots, jnp.zeros((SC_ROWS, HALF), I32))
        ys = _expert_call(tiles[:n_tiles, 0], xs, w_gate[l], w_up[l], w_down[l], n_tiles)
        g1, g2 = _sc_collect(ys, slot1, slot2)

        assert depth == 1
        h = _final_call(h1, g1, g2, rw, norm_final_w[None, :])
    return h.reshape(batch, seq, d)
```

```python
import functools
import math

import jax
import jax.numpy as jnp
from jax import lax
from jax.experimental import pallas as pl
from jax.experimental.pallas import tpu as pltpu
from jax.experimental.pallas import tpu_sc as plsc

F32 = jnp.float32
BF16 = jnp.bfloat16
I32 = jnp.int32
U32 = jnp.uint32

D_MODEL = 1024
EPS = 1e-6
GDN_HEADS = 4
GDN_DK = 128
GDN_DV = 128
GDN_CONV = 4
RET_HEADS = 4
RET_DK = 128
RET_DV = 256
ROPE_BASE = 10000.0
N_GROUPS = 4
EXPERTS_PER_GROUP = 8
N_EXPERTS = N_GROUPS * EXPERTS_PER_GROUP
D_EXPERT = 512

GDN_QK = GDN_HEADS * GDN_DK
GDN_V = GDN_HEADS * GDN_DV
RET_QK = RET_HEADS * RET_DK
RET_V = RET_HEADS * RET_DV

LANES = 128
CHUNK = 128
INV_BLOCK = 16
GDN_PREP_CHUNKS = 2
VMEM_LIMIT = 56 * 1024 * 1024

TOP_K = 2
SLOT_TILE = 256
PLAN_TILE = 256
PLAN_STEP = 1024
HALF = D_MODEL // 2
SC_WORKERS = 32
SC_ROWS = 64

PROJ_COLS = 3 * GDN_QK + GDN_V + 2 * RET_QK + 2 * RET_V + 2 * D_MODEL


def _silu(x):
    return x / (1.0 + jnp.exp(-x))


def _sigmoid(x):
    return 1.0 / (1.0 + jnp.exp(-x))


def _dot(a, b):
    return jnp.dot(a, b, preferred_element_type=F32)


def _dot_nt(a, b):
    return lax.dot_general(a, b, (((1,), (1,)), ((), ())), preferred_element_type=F32)


def _pack_bf16_pairs(x):
    bits = lax.bitcast_convert_type(x.astype(BF16).astype(F32), U32)
    packed = (bits[:, :HALF] >> 16) | (bits[:, HALF:] & jnp.uint32(0xFFFF0000))
    return lax.bitcast_convert_type(packed, I32)


def _unpack_bf16_pairs(p):
    p = lax.bitcast_convert_type(p, U32)
    lo = lax.bitcast_convert_type(p << 16, F32)
    hi = lax.bitcast_convert_type(p & jnp.uint32(0xFFFF0000), F32)
    return jnp.concatenate([lo, hi], axis=1)


def _proj_kernel(x_ref, nw_ref, w_ref, wab_ref, proj_ref, ab_ref, u_ref):
    j = pl.program_id(1)

    @pl.when(j == 0)
    def _():
        x = x_ref[...]
        u = x * lax.rsqrt(jnp.mean(x * x, axis=-1, keepdims=True) + EPS) * nw_ref[...]
        ub = u.astype(BF16)
        u_ref[...] = ub
        ab_ref[...] = _dot(ub, wab_ref[...])

    proj_ref[...] = _dot(u_ref[...], w_ref[...]).astype(BF16)


def _proj_call(x2, norm_w, w_main, w_ab, tm=1024, tn=1024):
    m = x2.shape[0]
    return pl.pallas_call(
        _proj_kernel,
        grid=(m // tm, PROJ_COLS // tn),
        in_specs=[
            pl.BlockSpec((tm, D_MODEL), lambda i, j: (i, 0)),
            pl.BlockSpec((1, D_MODEL), lambda i, j: (0, 0)),
            pl.BlockSpec((D_MODEL, tn), lambda i, j: (0, j)),
            pl.BlockSpec((D_MODEL, LANES), lambda i, j: (0, 0)),
        ],
        out_specs=[
            pl.BlockSpec((tm, tn), lambda i, j: (i, j)),
            pl.BlockSpec((tm, LANES), lambda i, j: (i, 0)),
        ],
        out_shape=[
            jax.ShapeDtypeStruct((m, PROJ_COLS), BF16),
            jax.ShapeDtypeStruct((m, LANES), F32),
        ],
        scratch_shapes=[pltpu.VMEM((tm, D_MODEL), BF16)],
        compiler_params=pltpu.CompilerParams(
            dimension_semantics=("arbitrary", "arbitrary"), vmem_limit_bytes=VMEM_LIMIT),
        name="proj",
    )(x2, norm_w, w_main, w_ab)


def _unit_lower_inverses(lows, ii, jj):
    eye = jnp.where(ii == jj, 1.0, 0.0).astype(F32)
    in_block = (ii // INV_BLOCK) == (jj // INV_BLOCK)
    ps = [jnp.where(in_block, -low, 0.0) for low in lows]
    ts = [eye + p for p in ps]
    span = 2
    while span < INV_BLOCK:
        ps = [_dot(p, p) for p in ps]
        ts = [t + _dot(t, p) for t, p in zip(ts, ps)]
        span *= 2
    s = INV_BLOCK
    while s < CHUNK:
        off_diag = ((ii // (2 * s)) == (jj // (2 * s))) & ((ii // s) != (jj // s))
        xs = [_dot(jnp.where(off_diag, low, 0.0), t) for low, t in zip(lows, ts)]
        ts = [t - _dot(t, x) for t, x in zip(ts, xs)]
        s *= 2
    return ts


def _gdn_kernel(qkv_ref, z_ref, ab_ref, convw_ref, alog_ref, dtb_ref, normw_ref, o_ref,
                u_s, w_s, qg_s, kdt_s, attn_s, gl_s, state_s, q_s, k_s, kb_s, rhs_s, dec_s, cv_s,
                *, seq):
    nchunk = seq // CHUNK
    ii = lax.broadcasted_iota(jnp.int32, (CHUNK, CHUNK), 0)
    jj = lax.broadcasted_iota(jnp.int32, (CHUNK, CHUNK), 1)
    causal = ii >= jj
    strict = ii > jj
    tri = jnp.where(causal, 1.0, 0.0).astype(F32)
    neg_a = -jnp.exp(alog_ref[...])
    dtb = dtb_ref[...]

    def conv_cols(c, r0, lo, buf):
        x = qkv_ref[pl.ds(r0, CHUNK), lo:lo + LANES].astype(F32)
        prev0 = pl.multiple_of(jnp.maximum(r0 - 16, 0), 16)
        prev = qkv_ref[pl.ds(prev0, 16), lo:lo + LANES].astype(F32)
        buf[0:8, :] = prev[8:16] * jnp.where(c > 0, 1.0, 0.0)
        buf[8:8 + CHUNK, :] = x
        w = convw_ref[:, lo:lo + LANES]
        y = (w[3:4] * x + w[2:3] * buf[7:7 + CHUNK, :] + w[1:2] * buf[6:6 + CHUNK, :]
             + w[0:1] * buf[5:5 + CHUNK, :])
        return _silu(y)

    def l2n(x):
        return x * lax.rsqrt(jnp.sum(x * x, axis=-1, keepdims=True) + EPS)

    tri_b = tri.astype(BF16)

    def chunk_cumsum(g):
        g1 = g.astype(BF16)
        r1 = g - g1.astype(F32)
        g2 = r1.astype(BF16)
        g3 = (r1 - g2.astype(F32)).astype(BF16)
        return _dot(tri_b, g1) + (_dot(tri_b, g2) + _dot(tri_b, g3))

    def operands(cc, slot):
        for sub in range(GDN_PREP_CHUNKS):
            c = cc * GDN_PREP_CHUNKS + sub
            r0 = pl.multiple_of(c * CHUNK, CHUNK)
            ab = ab_ref[pl.ds(r0, CHUNK), :]
            xg = ab + dtb
            softplus = jnp.maximum(xg, 0.0) + jnp.log(1.0 + jnp.exp(-jnp.abs(xg)))
            g_all = neg_a * softplus
            beta_all = _sigmoid(ab)
            gc_all = chunk_cumsum(g_all)
            gc_t = gc_all.T
            gl_s[c] = jnp.exp(gc_all[CHUNK - 1:CHUNK, :])
            for h in range(GDN_HEADS):
                n = sub * GDN_HEADS + h
                bufs = [cv_s.at[slot, 3 * n + i] for i in range(3)]
                q = l2n(conv_cols(c, r0, h * GDN_DK, bufs[0])) * (GDN_DK ** -0.5)
                k = l2n(conv_cols(c, r0, GDN_QK + h * GDN_DK, bufs[1]))
                v = conv_cols(c, r0, 2 * GDN_QK + h * GDN_DV, bufs[2])
                gcol = gc_all[:, h:h + 1]
                grow = gc_t[h:h + 1, :]
                beta = beta_all[:, GDN_HEADS + h:GDN_HEADS + h + 1]
                dec_s[slot, n] = jnp.where(causal, jnp.exp(gcol - grow), 0.0)
                eg = jnp.exp(gcol)
                kb = k * beta
                q_s[slot, n] = q.astype(BF16)
                k_s[slot, n] = k.astype(BF16)
                kb_s[slot, n] = kb.astype(BF16)
                rhs_s[slot, n] = jnp.concatenate([v * beta, kb * eg], axis=1).astype(BF16)
                qg_s[c, h] = (q * eg).astype(BF16)
                kd = k * jnp.exp(gc_all[CHUNK - 1:CHUNK, h:h + 1] - gcol)
                kdt_s[c, h] = kd.T.astype(BF16)

    def solve(cc, slot):
        items = [(cc * GDN_PREP_CHUNKS + sub, h, sub * GDN_HEADS + h)
                 for sub in range(GDN_PREP_CHUNKS) for h in range(GDN_HEADS)]
        kks = [_dot_nt(kb_s[slot, n], k_s[slot, n]) for _, _, n in items]
        qks = [_dot_nt(q_s[slot, n], k_s[slot, n]) for _, _, n in items]
        lows = [jnp.where(strict, kk * dec_s[slot, n], 0.0) for kk, (_, _, n) in zip(kks, items)]
        for qk, (c, h, n) in zip(qks, items):
            attn_s[c, h] = (qk * dec_s[slot, n]).astype(BF16)
        ts = _unit_lower_inverses(lows, ii, jj)
        sols = [_dot(t.astype(BF16), rhs_s[slot, n]) for t, (_, _, n) in zip(ts, items)]
        for sol, (c, h, _) in zip(sols, items):
            u_s[c, h] = sol[:, :GDN_DV]
            w_s[c, h] = sol[:, GDN_DV:].astype(BF16)

    ngroup = nchunk // GDN_PREP_CHUNKS
    operands(0, 0)

    def prep(i, carry):
        operands(2 * i + 1, 1)
        solve(2 * i, 0)
        operands(jnp.minimum(2 * i + 2, ngroup - 1), 0)
        solve(2 * i + 1, 1)
        return carry

    lax.fori_loop(0, ngroup // 2, prep, 0)

    state_s[...] = jnp.zeros_like(state_s)
    normw = normw_ref[...]

    def scan(c, carry):
        r0 = pl.multiple_of(c * CHUNK, CHUNK)
        gl = gl_s[c]
        heads = range(GDN_HEADS)
        ss = [state_s[h] for h in heads]
        wss = [_dot(w_s[c, h], ss[h]) for h in heads]
        qss = [_dot(qg_s[c, h], ss[h]) for h in heads]
        vns = [u_s[c, h] - wss[h] for h in heads]
        avs = [_dot(attn_s[c, h], vns[h]) for h in heads]
        kvs = [_dot(kdt_s[c, h], vns[h]) for h in heads]
        for h in heads:
            state_s[h] = ss[h] * gl[:, h:h + 1] + kvs[h]
            o = qss[h] + avs[h]
            z = z_ref[pl.ds(r0, CHUNK), h * GDN_DV:(h + 1) * GDN_DV].astype(F32)
            on = o * lax.rsqrt(jnp.mean(o * o, axis=-1, keepdims=True) + EPS) * normw
            o_ref[pl.ds(r0, CHUNK), h * GDN_DV:(h + 1) * GDN_DV] = (on * _silu(z)).astype(BF16)
        return carry

    lax.fori_loop(0, nchunk, scan, 0)


def _gdn_call(proj, ab, conv_w8, alog_p, dtb_p, normw, batch, seq):
    nchunk = seq // CHUNK
    hs = (nchunk, GDN_HEADS, CHUNK, CHUNK)
    ops = (2, GDN_PREP_CHUNKS * GDN_HEADS, CHUNK, CHUNK)
    return pl.pallas_call(
        functools.partial(_gdn_kernel, seq=seq),
        grid=(batch,),
        in_specs=[
            pl.BlockSpec((seq, 3 * GDN_QK), lambda b: (b, 0)),
            pl.BlockSpec((seq, GDN_V), lambda b: (b, 3)),
            pl.BlockSpec((seq, LANES), lambda b: (b, 0)),
            pl.BlockSpec((8, 3 * GDN_QK), lambda b: (0, 0)),
            pl.BlockSpec((1, LANES), lambda b: (0, 0)),
            pl.BlockSpec((1, LANES), lambda b: (0, 0)),
            pl.BlockSpec((1, GDN_DV), lambda b: (0, 0)),
        ],
        out_specs=pl.BlockSpec((seq, GDN_V), lambda b: (b, 0)),
        out_shape=jax.ShapeDtypeStruct((batch * seq, GDN_V), BF16),
        scratch_shapes=[
            pltpu.VMEM(hs, F32), pltpu.VMEM(hs, BF16), pltpu.VMEM(hs, BF16), pltpu.VMEM(hs, BF16),
            pltpu.VMEM(hs, BF16),
            pltpu.VMEM((nchunk, 1, LANES), F32),
            pltpu.VMEM((GDN_HEADS, GDN_DK, GDN_DV), F32),
            pltpu.VMEM(ops, BF16), pltpu.VMEM(ops, BF16), pltpu.VMEM(ops, BF16),
            pltpu.VMEM(ops[:3] + (2 * CHUNK,), BF16),
            pltpu.VMEM(ops, F32),
            pltpu.VMEM((2, 3 * ops[1], 8 + CHUNK, LANES), F32),
        ],
        compiler_params=pltpu.CompilerParams(
            dimension_semantics=("arbitrary",), vmem_limit_bytes=VMEM_LIMIT),
        name="gdn",
    )(proj, proj, ab, conv_w8, alog_p, dtb_p, normw)


def _ret_kernel(qk_ref, v_ref, g_ref, sin_ref, cos_ref, inner_ref, kdec_ref, qdec_ref, cdec_ref,
                o_ref, state_s, *, seq):
    nchunk = seq // CHUNK
    lane = lax.broadcasted_iota(jnp.int32, (CHUNK, RET_DK), 1)
    even = (lane % 2) == 0

    def rotate(x, sin, cos):
        nxt = pltpu.roll(x, RET_DK - 1, axis=1)
        prv = pltpu.roll(x, 1, axis=1)
        return x * cos + jnp.where(even, -nxt, prv) * sin

    state_s[...] = jnp.zeros_like(state_s)
    kdec = kdec_ref[...]
    qdec = qdec_ref[...]
    cdec = cdec_ref[...]

    def body(c, carry):
        r0 = pl.multiple_of(c * CHUNK, CHUNK)
        sin = sin_ref[pl.ds(r0, CHUNK), :]
        cos = cos_ref[pl.ds(r0, CHUNK), :]
        heads = range(RET_HEADS)
        qs = [rotate(qk_ref[pl.ds(r0, CHUNK), h * RET_DK:(h + 1) * RET_DK].astype(F32), sin, cos)
              for h in heads]
        ks = [rotate(qk_ref[pl.ds(r0, CHUNK), RET_QK + h * RET_DK:RET_QK + (h + 1) * RET_DK]
                     .astype(F32), sin, cos) * (RET_DK ** -0.5) for h in heads]
        vs = [v_ref[pl.ds(r0, CHUNK), h * RET_DV:(h + 1) * RET_DV].astype(F32) for h in heads]
        ss = [state_s[h] for h in heads]
        qks = [_dot_nt(qs[h], ks[h]) for h in heads]
        inters = [_dot(qs[h] * qdec[:, h:h + 1], ss[h]) for h in heads]
        kvs = [_dot((ks[h] * kdec[:, h:h + 1]).T, vs[h]) for h in heads]
        intras = [_dot(qks[h] * inner_ref[h], vs[h]) for h in heads]
        for h in heads:
            state_s[h] = ss[h] * cdec[:, h:h + 1] + kvs[h]
            o = intras[h] + inters[h]
            gate = g_ref[pl.ds(r0, CHUNK), h * RET_DV:(h + 1) * RET_DV].astype(F32)
            on = o * lax.rsqrt(jnp.mean(o * o, axis=-1, keepdims=True) + EPS)
            o_ref[pl.ds(r0, CHUNK), h * RET_DV:(h + 1) * RET_DV] = (on * _silu(gate)).astype(BF16)
        return carry

    lax.fori_loop(0, nchunk, body, 0)


def _ret_call(proj, sin, cos, inner, kdec, qdec, cdec, batch, seq):
    return pl.pallas_call(
        functools.partial(_ret_kernel, seq=seq),
        grid=(batch,),
        in_specs=[
            pl.BlockSpec((seq, 2 * RET_QK), lambda b: (b, 2)),
            pl.BlockSpec((seq, RET_V), lambda b: (b, 3)),
            pl.BlockSpec((seq, RET_V), lambda b: (b, 4)),
            pl.BlockSpec((seq, RET_DK), lambda b: (0, 0)),
            pl.BlockSpec((seq, RET_DK), lambda b: (0, 0)),
            pl.BlockSpec((RET_HEADS, CHUNK, CHUNK), lambda b: (0, 0, 0)),
            pl.BlockSpec((CHUNK, LANES), lambda b: (0, 0)),
            pl.BlockSpec((CHUNK, LANES), lambda b: (0, 0)),
            pl.BlockSpec((1, LANES), lambda b: (0, 0)),
        ],
        out_specs=pl.BlockSpec((seq, RET_V), lambda b: (b, 0)),
        out_shape=jax.ShapeDtypeStruct((batch * seq, RET_V), BF16),
        scratch_shapes=[pltpu.VMEM((RET_HEADS, RET_DK, RET_DV), F32)],
        compiler_params=pltpu.CompilerParams(
            dimension_semantics=("arbitrary",), vmem_limit_bytes=VMEM_LIMIT),
        name="retention",
    )(proj, proj, proj, sin, cos, inner, kdec, qdec, cdec)


def _retention_tables(seq):
    inv_freq = 1.0 / (ROPE_BASE ** jnp.linspace(0.0, 1.0, RET_DK // 2, dtype=F32))
    ang = jnp.arange(seq, dtype=F32)[:, None] * inv_freq[None, :]
    sin = jnp.repeat(jnp.sin(ang), 2, axis=-1)
    cos = jnp.repeat(jnp.cos(ang), 2, axis=-1)
    log_gamma = jnp.log(1.0 - 2.0 ** (-5.0 - jnp.arange(RET_HEADS, dtype=F32)))
    idx = jnp.arange(CHUNK, dtype=F32)
    causal = jnp.tril(jnp.ones((CHUNK, CHUNK), dtype=bool))
    rel = jnp.where(causal, idx[:, None] - idx[None, :], 0.0)
    inner = jnp.where(causal, jnp.exp(rel[None] * log_gamma[:, None, None]), 0.0)
    k_decay = jnp.exp(log_gamma[:, None] * (CHUNK - 1.0 - idx)[None, :])
    q_decay = jnp.exp(log_gamma[:, None] * (idx + 1.0)[None, :])
    chunk_decay = jnp.exp(log_gamma * CHUNK)
    pad = LANES - RET_HEADS
    kdec = jnp.pad(k_decay.T, ((0, 0), (0, pad)))
    qdec = jnp.pad(q_decay.T, ((0, 0), (0, pad)))
    cdec = jnp.pad(chunk_decay[None, :], ((0, 0), (0, pad)))
    return sin, cos, inner, kdec, qdec, cdec


def _merge_kernel(x_ref, ya_ref, yb_ref, ma_ref, mb_ref, wa_ref, wr_ref, wo_ref, nw_ref,
                  wrh_ref, wrl_ref, br_ref, h_ref, xn_ref, ridx_ref, rw_ref, cnt_ref):
    a = _dot(ya_ref[...], wa_ref[...])
    r = _dot(yb_ref[...], wr_ref[...])
    merged = _sigmoid(ma_ref[...].astype(F32)) * a + _sigmoid(mb_ref[...].astype(F32)) * r
    h = x_ref[...] + _dot(merged.astype(BF16), wo_ref[...])
    h_ref[...] = h
    xn = h * lax.rsqrt(jnp.mean(h * h, axis=-1, keepdims=True) + EPS) * nw_ref[...]
    xn_ref[...] = _pack_bf16_pairs(xn)

    xh = xn.astype(BF16)
    xl = (xn - xh.astype(F32)).astype(BF16)
    logits = (_dot(xh, wrh_ref[...]) + (_dot(xl, wrh_ref[...]) + _dot(xh, wrl_ref[...])
                                        + _dot(xl, wrl_ref[...]))) + br_ref[...]
    lane = lax.broadcasted_iota(jnp.int32, (xn.shape[0], LANES), 1)
    neg = -jnp.inf
    gl = jnp.where((lane >= N_EXPERTS) & (lane < N_EXPERTS + N_GROUPS), logits, neg)
    gmax = jnp.max(gl, axis=-1, keepdims=True)
    gidx = jnp.min(jnp.where(gl == gmax, lane, LANES), axis=-1, keepdims=True) - N_EXPERTS
    g_w = 1.0 / jnp.sum(jnp.exp(gl - gmax), axis=-1, keepdims=True)
    el = jnp.where((lane // EXPERTS_PER_GROUP == gidx) & (lane < N_EXPERTS), logits, neg)
    m1 = jnp.max(el, axis=-1, keepdims=True)
    i1 = jnp.min(jnp.where(el == m1, lane, LANES), axis=-1, keepdims=True)
    el2 = jnp.where(lane == i1, neg, el)
    m2 = jnp.max(el2, axis=-1, keepdims=True)
    i2 = jnp.min(jnp.where(el2 == m2, lane, LANES), axis=-1, keepdims=True)
    e2 = jnp.exp(m2 - m1)
    p1 = g_w / (1.0 + e2)
    p2 = g_w * e2 / (1.0 + e2)
    ridx_ref[...] = jnp.where(lane == 0, i1, jnp.where(lane == 1, i2, 0))
    rw_ref[...] = jnp.where(lane == 0, p1, jnp.where(lane == 1, p2, 0.0))

    @pl.when(pl.program_id(0) == 0)
    def _():
        cnt_ref[...] = jnp.zeros_like(cnt_ref)

    onehot = jnp.where((lane == i1) | (lane == i2), 1.0, 0.0)
    cnt_ref[...] += jnp.broadcast_to(jnp.sum(onehot, axis=0, keepdims=True), cnt_ref.shape)


def _merge_call(x2, ya, yb, proj, wa, wr, wo, nw, wr_hi, wr_lo, b_router, tm=512):
    m = x2.shape[0]
    full = lambda shape: pl.BlockSpec(shape, lambda i: (0, 0))
    return pl.pallas_call(
        _merge_kernel,
        grid=(m // tm,),
        in_specs=[
            pl.BlockSpec((tm, D_MODEL), lambda i: (i, 0)),
            pl.BlockSpec((tm, GDN_V), lambda i: (i, 0)),
            pl.BlockSpec((tm, RET_V), lambda i: (i, 0)),
            pl.BlockSpec((tm, D_MODEL), lambda i: (i, 5)),
            pl.BlockSpec((tm, D_MODEL), lambda i: (i, 6)),
            full((GDN_V, D_MODEL)), full((RET_V, D_MODEL)), full((D_MODEL, D_MODEL)),
            full((1, D_MODEL)),
            full((D_MODEL, LANES)), full((D_MODEL, LANES)), full((1, LANES)),
        ],
        out_specs=[
            pl.BlockSpec((tm, D_MODEL), lambda i: (i, 0)),
            pl.BlockSpec((tm, HALF), lambda i: (i, 0)),
            pl.BlockSpec((tm, LANES), lambda i: (i, 0)),
            pl.BlockSpec((tm, LANES), lambda i: (i, 0)),
            pl.BlockSpec((8, LANES), lambda i: (0, 0)),
        ],
        out_shape=[
            jax.ShapeDtypeStruct((m, D_MODEL), F32),
            jax.ShapeDtypeStruct((m, HALF), I32),
            jax.ShapeDtypeStruct((m, LANES), I32),
            jax.ShapeDtypeStruct((m, LANES), F32),
            jax.ShapeDtypeStruct((8, LANES), F32),
        ],
        compiler_params=pltpu.CompilerParams(
            dimension_semantics=("arbitrary",), vmem_limit_bytes=VMEM_LIMIT),
        name="merge_router",
    )(x2, ya, yb, proj, proj, wa, wr, wo, nw, wr_hi, wr_lo, b_router)


def _slot_counts(m):
    n_pad = N_EXPERTS * SLOT_TILE
    n_slots = TOP_K * m + n_pad
    return n_slots, n_slots // SLOT_TILE, n_pad


def _lane_prefix_sum(x, lane):
    s = 1
    while s < LANES:
        x = x + jnp.where(lane >= s, pltpu.roll(x, s, axis=1), 0.0)
        s *= 2
    return x


def _plan_kernel(ridx_ref, cnt_ref, slots_ref, tile_ref, pad_ref, carry_s, off_s, *, n_slots):
    i = pl.program_id(0)
    lane = lax.broadcasted_iota(I32, (PLAN_TILE, LANES), 1)
    row = lax.broadcasted_iota(I32, (PLAN_TILE, LANES), 0)
    lane1 = lane[0:1]

    @pl.when(i == 0)
    def _():
        cnt = cnt_ref[0:1, :]
        tile = float(SLOT_TILE)
        padded = jnp.floor((cnt + (tile - 1.0)) / tile) * tile
        incl = _lane_prefix_sum(padded, lane1)
        off = incl - padded
        off_s[...] = off
        carry_s[...] = jnp.zeros_like(carry_s)
        first = (row * SLOT_TILE).astype(F32)
        ended = jnp.where((lane < N_EXPERTS) & (incl <= first), 1.0, 0.0)
        tile_ref[...] = jnp.broadcast_to(
            jnp.sum(ended, axis=-1, keepdims=True), (PLAN_TILE, LANES)).astype(I32)
        rowf = row.astype(F32)
        spill = (n_slots + lane * SLOT_TILE + row).astype(F32)
        pad_ref[...] = jnp.where(rowf < padded - cnt, off + cnt + rowf, spill).astype(I32)

    strict = jnp.where(row[:, 0:1] > lax.broadcasted_iota(I32, (PLAN_TILE, PLAN_TILE), 1),
                       1.0, 0.0).astype(BF16)
    off = off_s[...]
    carry = carry_s[...]
    for sb in range(PLAN_STEP // PLAN_TILE):
        rows = pl.ds(sb * PLAN_TILE, PLAN_TILE)
        e1 = ridx_ref[rows, 0:1]
        e2 = ridx_ref[rows, 1:2]
        onehot = jnp.where((lane == e1) | (lane == e2), 1.0, 0.0)
        pos = _dot(strict, onehot.astype(BF16)) + (carry + off)
        s1 = jnp.sum(jnp.where(lane == e1, pos, 0.0), axis=-1, keepdims=True)
        s2 = jnp.sum(jnp.where(lane == e2, pos, 0.0), axis=-1, keepdims=True)
        both = jnp.where(lane == 0, s1, jnp.where(lane == 1, s2, 0.0))
        for q in range(PLAN_TILE // LANES):
            t = both[q * LANES:(q + 1) * LANES].T
            c0 = sb * PLAN_TILE + q * LANES
            slots_ref[:, c0:c0 + LANES] = t[0:8].astype(I32)
        carry = carry + jnp.sum(onehot, axis=0, keepdims=True)
    carry_s[...] = carry


def _plan_call(ridx, cnt):
    m = ridx.shape[0]
    n_slots, n_tiles, _ = _slot_counts(m)
    assert n_tiles <= PLAN_TILE and SLOT_TILE == PLAN_TILE
    return pl.pallas_call(
        functools.partial(_plan_kernel, n_slots=n_slots),
        grid=(m // PLAN_STEP,),
        in_specs=[pl.BlockSpec((PLAN_STEP, LANES), lambda i: (i, 0)),
                  pl.BlockSpec((8, LANES), lambda i: (0, 0))],
        out_specs=[
            pl.BlockSpec((8, PLAN_STEP), lambda i: (0, i)),
            pl.BlockSpec((PLAN_TILE, LANES), lambda i: (0, 0)),
            pl.BlockSpec((PLAN_TILE, LANES), lambda i: (0, 0)),
        ],
        out_shape=[
            jax.ShapeDtypeStruct((8, m), I32),
            jax.ShapeDtypeStruct((PLAN_TILE, LANES), I32),
            jax.ShapeDtypeStruct((PLAN_TILE, LANES), I32),
        ],
        scratch_shapes=[pltpu.VMEM((1, LANES), F32), pltpu.VMEM((1, LANES), F32)],
        compiler_params=pltpu.CompilerParams(dimension_semantics=("arbitrary",)),
        name="dispatch_plan",
    )(ridx, cnt)


def _sc_mesh():
    return plsc.VectorSubcoreMesh(core_axis_name="c", subcore_axis_name="s")


def _sc_worker():
    return lax.axis_index("s") * 2 + lax.axis_index("c")


def _sc_dispatch(xn, slot1, slot2, pad_slots, zeros):
    m = xn.shape[0]
    n_slots, _, n_pad = _slot_counts(m)
    per = m // SC_WORKERS
    pad_per = n_pad // SC_WORKERS

    @functools.partial(
        pl.kernel, mesh=_sc_mesh(),
        out_type=jax.ShapeDtypeStruct((n_slots + n_pad, HALF), I32),
        scratch_types=[pltpu.VMEM((SC_ROWS,), I32), pltpu.VMEM((SC_ROWS,), I32),
                       pltpu.VMEM((SC_ROWS, HALF), I32), pltpu.SemaphoreType.DMA],
        name="sc_dispatch")
    def k(x_hbm, s1_hbm, s2_hbm, p_hbm, z_hbm, o_hbm, i1_v, i2_v, rows_v, sem):
        wid = _sc_worker()

        @pl.loop(0, per // SC_ROWS)
        def _(ci):
            t0 = wid * per + ci * SC_ROWS
            pltpu.sync_copy(x_hbm.at[pl.ds(t0, SC_ROWS)], rows_v)
            pltpu.sync_copy(s1_hbm.at[pl.ds(t0, SC_ROWS)], i1_v)
            pltpu.sync_copy(s2_hbm.at[pl.ds(t0, SC_ROWS)], i2_v)
            c1 = pltpu.async_copy(rows_v, o_hbm.at[i1_v], sem)
            c2 = pltpu.async_copy(rows_v, o_hbm.at[i2_v], sem)
            c1.wait()
            c2.wait()

        pltpu.sync_copy(z_hbm, rows_v)

        @pl.loop(0, pad_per // SC_ROWS)
        def _(ci):
            t0 = wid * pad_per + ci * SC_ROWS
            pltpu.sync_copy(p_hbm.at[pl.ds(t0, SC_ROWS)], i1_v)
            pltpu.async_copy(rows_v, o_hbm.at[i1_v], sem).wait()

    return k(xn, slot1, slot2, pad_slots, zeros)


def _sc_collect(ys, slot1, slot2):
    m = slot1.shape[0]
    per = m // SC_WORKERS
    row = jax.ShapeDtypeStruct((m, HALF), I32)

    @functools.partial(
        pl.kernel, mesh=_sc_mesh(), out_type=[row, row],
        scratch_types=[pltpu.VMEM((SC_ROWS,), I32), pltpu.VMEM((SC_ROWS, HALF), I32),
                       pltpu.SemaphoreType.DMA],
        name="sc_collect")
    def k(y_hbm, s1_hbm, s2_hbm, g1_hbm, g2_hbm, i_v, rows_v, sem):
        wid = _sc_worker()

        @pl.loop(0, per // SC_ROWS)
        def _(ci):
            t0 = wid * per + ci * SC_ROWS
            for s_hbm, g_hbm in ((s1_hbm, g1_hbm), (s2_hbm, g2_hbm)):
                pltpu.sync_copy(s_hbm.at[pl.ds(t0, SC_ROWS)], i_v)
                pltpu.async_copy(y_hbm.at[i_v], rows_v, sem).wait()
                pltpu.sync_copy(rows_v, g_hbm.at[pl.ds(t0, SC_ROWS)])

    return k(ys, slot1, slot2)


def _expert_kernel(te_ref, xs_ref, wg_ref, wu_ref, wd_ref, ys_ref, wg_b, wu_b, wd_b):
    j = pl.program_id(0)
    e = te_ref[j]
    prev = te_ref[jnp.maximum(j - 1, 0)]

    @pl.when((j == 0) | (e != prev))
    def _():
        wg_b[...] = wg_ref[0].astype(BF16)
        wu_b[...] = wu_ref[0].astype(BF16)
        wd_b[...] = wd_ref[0].astype(BF16)

    @pl.when(e < N_EXPERTS)
    def _():
        x = _unpack_bf16_pairs(xs_ref[...]).astype(BF16)
        hid = _silu(_dot(x, wg_b[...])) * _dot(x, wu_b[...])
        ys_ref[...] = _pack_bf16_pairs(_dot(hid.astype(BF16), wd_b[...]))

    @pl.when(e >= N_EXPERTS)
    def _():
        ys_ref[...] = jnp.zeros_like(ys_ref)


def _expert_call(tile_expert, xs, wg, wu, wd, n_tiles):
    wmap = lambda j, te: (jnp.minimum(te[j], N_EXPERTS - 1), 0, 0)
    return pl.pallas_call(
        _expert_kernel,
        grid_spec=pltpu.PrefetchScalarGridSpec(
            num_scalar_prefetch=1,
            grid=(n_tiles,),
            in_specs=[
                pl.BlockSpec((SLOT_TILE, HALF), lambda j, te: (j, 0)),
                pl.BlockSpec((1, D_MODEL, D_EXPERT), wmap),
                pl.BlockSpec((1, D_MODEL, D_EXPERT), wmap),
                pl.BlockSpec((1, D_EXPERT, D_MODEL), wmap),
            ],
            out_specs=pl.BlockSpec((SLOT_TILE, HALF), lambda j, te: (j, 0)),
            scratch_shapes=[pltpu.VMEM((D_MODEL, D_EXPERT), BF16), pltpu.VMEM((D_MODEL, D_EXPERT), BF16),
                            pltpu.VMEM((D_EXPERT, D_MODEL), BF16)],
        ),
        out_shape=jax.ShapeDtypeStruct((n_tiles * SLOT_TILE, HALF), I32),
        compiler_params=pltpu.CompilerParams(
            dimension_semantics=("arbitrary",), vmem_limit_bytes=VMEM_LIMIT),
        name="experts",
    )(tile_expert, xs, wg, wu, wd)


def _final_kernel(h_ref, g1_ref, g2_ref, rw_ref, nw_ref, o_ref):
    rw = rw_ref[...]
    y = rw[:, 0:1] * _unpack_bf16_pairs(g1_ref[...]) + rw[:, 1:2] * _unpack_bf16_pairs(g2_ref[...])
    h = h_ref[...] + y
    o_ref[...] = h * lax.rsqrt(jnp.mean(h * h, axis=-1, keepdims=True) + EPS) * nw_ref[...]


def _final_call(h1, g1, g2, rw, nw, tm=512):
    m = h1.shape[0]
    return pl.pallas_call(
        _final_kernel,
        grid=(m // tm,),
        in_specs=[
            pl.BlockSpec((tm, D_MODEL), lambda i: (i, 0)),
            pl.BlockSpec((tm, HALF), lambda i: (i, 0)),
            pl.BlockSpec((tm, HALF), lambda i: (i, 0)),
            pl.BlockSpec((tm, LANES), lambda i: (i, 0)),
            pl.BlockSpec((1, D_MODEL), lambda i: (0, 0)),
        ],
        out_specs=pl.BlockSpec((tm, D_MODEL), lambda i: (i, 0)),
        out_shape=jax.ShapeDtypeStruct((m, D_MODEL), F32),
        compiler_params=pltpu.CompilerParams(dimension_semantics=("arbitrary",)),
        name="combine_final",
    )(h1, g1, g2, rw, nw)


def _pad_lanes(a):
    return jnp.pad(a, ((0, 0), (0, LANES - a.shape[1])))


def kernel(x, norm_mix_w, w_in, conv_w, A_log, dt_bias, gdn_norm_w, w_up_gdn, w_up_ret, w_out,
           norm_ffn_w, w_group, b_group, w_expert, b_expert, w_gate, w_up, w_down, norm_final_w):
    batch, seq, d = x.shape
    m = batch * seq
    h = x.reshape(m, d)
    depth = w_in.shape[0]
    sin, cos, inner, kdec, qdec, cdec = _retention_tables(seq)
    for l in range(depth):
        o_ab = 3 * GDN_QK
        o_z = o_ab + 2 * GDN_HEADS
        wl = w_in[l]
        w_main = jnp.concatenate([wl[:, :o_ab], wl[:, o_z:]], axis=1).astype(BF16)
        w_ab = _pad_lanes(wl[:, o_ab:o_z]).astype(BF16)
        proj, ab = _proj_call(h, norm_mix_w[l][None, :], w_main, w_ab)

        conv8 = jnp.pad(conv_w[l], ((0, 8 - GDN_CONV), (0, 0)))
        ya = _gdn_call(proj, ab, conv8, _pad_lanes(A_log[l][None, :]), _pad_lanes(dt_bias[l][None, :]),
                       gdn_norm_w[l][None, :], batch, seq)
        yb = _ret_call(proj, sin, cos, inner, kdec, qdec, cdec, batch, seq)

        w_router = _pad_lanes(jnp.concatenate([w_expert[l], w_group[l]], axis=1))
        wr_hi = w_router.astype(BF16)
        wr_lo = (w_router - wr_hi.astype(F32)).astype(BF16)
        b_router = _pad_lanes(jnp.concatenate([b_expert[l], b_group[l]])[None, :])
        h1, xn, ridx, rw, cnt = _merge_call(
            h, ya, yb, proj, w_up_gdn[l].astype(BF16), w_up_ret[l].astype(BF16), w_out[l].astype(BF16),
            norm_ffn_w[l][None, :], wr_hi, wr_lo, b_router)

        _, n_tiles, _ = _slot_counts(m)
        slots, tiles, pads = _plan_call(ridx, cnt)
        slot1, slot2 = slots[0], slots[1]
        pad_slots = pads[:, :N_EXPERTS].T.reshape(-1)
        xs = _sc_dispatch(xn, slot1, slot2, pad_slots, jnp.zeros((SC_ROWS, HALF), I32))
        ys = _expert_call(tiles[:n_tiles, 0], xs, w_gate[l], w_up[l], w_down[l], n_tiles)
        g1, g2 = _sc_collect(ys, slot1, slot2)

        assert depth == 1
        h = _final_call(h1, g1, g2, rw, norm_final_w[None, :])
    return h.reshape(batch, seq, d)
```

```python
import functools
import math

import jax
import jax.numpy as jnp
from jax import lax
from jax.experimental import pallas as pl
from jax.experimental.pallas import tpu as pltpu
from jax.experimental.pallas import tpu_sc as plsc

F32 = jnp.float32
BF16 = jnp.bfloat16
I32 = jnp.int32
U32 = jnp.uint32

D_MODEL = 1024
EPS = 1e-6
GDN_HEADS = 4
GDN_DK = 128
GDN_DV = 128
GDN_CONV = 4
RET_HEADS = 4
RET_DK = 128
RET_DV = 256
ROPE_BASE = 10000.0
N_GROUPS = 4
EXPERTS_PER_GROUP = 8
N_EXPERTS = N_GROUPS * EXPERTS_PER_GROUP
D_EXPERT = 512

GDN_QK = GDN_HEADS * GDN_DK
GDN_V = GDN_HEADS * GDN_DV
RET_QK = RET_HEADS * RET_DK
RET_V = RET_HEADS * RET_DV

LANES = 128
CHUNK = 128
INV_BLOCK = 16
GDN_PREP_CHUNKS = 2
VMEM_LIMIT = 56 * 1024 * 1024

TOP_K = 2
SLOT_TILE = 256
PLAN_TILE = 256
PLAN_STEP = 1024
HALF = D_MODEL // 2
SC_WORKERS = 32
SC_ROWS = 64

PROJ_COLS = 3 * GDN_QK + GDN_V + 2 * RET_QK + 2 * RET_V + 2 * D_MODEL


def _silu(x):
    return x / (1.0 + jnp.exp(-x))


def _sigmoid(x):
    return 1.0 / (1.0 + jnp.exp(-x))


def _dot(a, b):
    return jnp.dot(a, b, preferred_element_type=F32)


def _dot_nt(a, b):
    return lax.dot_general(a, b, (((1,), (1,)), ((), ())), preferred_element_type=F32)


def _pack_bf16_pairs(x):
    bits = lax.bitcast_convert_type(x.astype(BF16).astype(F32), U32)
    packed = (bits[:, :HALF] >> 16) | (bits[:, HALF:] & jnp.uint32(0xFFFF0000))
    return lax.bitcast_convert_type(packed, I32)


def _unpack_bf16_pairs(p):
    p = lax.bitcast_convert_type(p, U32)
    lo = lax.bitcast_convert_type(p << 16, F32)
    hi = lax.bitcast_convert_type(p & jnp.uint32(0xFFFF0000), F32)
    return jnp.concatenate([lo, hi], axis=1)


def _proj_kernel(x_ref, nw_ref, w_ref, wab_ref, proj_ref, ab_ref, u_ref):
    j = pl.program_id(1)

    @pl.when(j == 0)
    def _():
        x = x_ref[...]
        u = x * lax.rsqrt(jnp.mean(x * x, axis=-1, keepdims=True) + EPS) * nw_ref[...]
        ub = u.astype(BF16)
        u_ref[...] = ub
        ab_ref[...] = _dot(ub, wab_ref[...])

    proj_ref[...] = _dot(u_ref[...], w_ref[...]).astype(BF16)


def _proj_call(x2, norm_w, w_main, w_ab, tm=1024, tn=1024):
    m = x2.shape[0]
    return pl.pallas_call(
        _proj_kernel,
        grid=(m // tm, PROJ_COLS // tn),
        in_specs=[
            pl.BlockSpec((tm, D_MODEL), lambda i, j: (i, 0)),
            pl.BlockSpec((1, D_MODEL), lambda i, j: (0, 0)),
            pl.BlockSpec((D_MODEL, tn), lambda i, j: (0, j)),
            pl.BlockSpec((D_MODEL, LANES), lambda i, j: (0, 0)),
        ],
        out_specs=[
            pl.BlockSpec((tm, tn), lambda i, j: (i, j)),
            pl.BlockSpec((tm, LANES), lambda i, j: (i, 0)),
        ],
        out_shape=[
            jax.ShapeDtypeStruct((m, PROJ_COLS), BF16),
            jax.ShapeDtypeStruct((m, LANES), F32),
        ],
        scratch_shapes=[pltpu.VMEM((tm, D_MODEL), BF16)],
        compiler_params=pltpu.CompilerParams(
            dimension_semantics=("arbitrary", "arbitrary"), vmem_limit_bytes=VMEM_LIMIT),
        name="proj",
    )(x2, norm_w, w_main, w_ab)


def _unit_lower_inverses(lows, ii, jj):
    eye = jnp.where(ii == jj, 1.0, 0.0).astype(F32)
    in_block = (ii // INV_BLOCK) == (jj // INV_BLOCK)
    ps = [jnp.where(in_block, -low, 0.0) for low in lows]
    ts = [eye + p for p in ps]
    span = 2
    while span < INV_BLOCK:
        ps = [_dot(p, p) for p in ps]
        ts = [t + _dot(t, p) for t, p in zip(ts, ps)]
        span *= 2
    s = INV_BLOCK
    while s < CHUNK:
        off_diag = ((ii // (2 * s)) == (jj // (2 * s))) & ((ii // s) != (jj // s))
        xs = [_dot(jnp.where(off_diag, low, 0.0), t) for low, t in zip(lows, ts)]
        ts = [t - _dot(t, x) for t, x in zip(ts, xs)]
        s *= 2
    return ts


def _gdn_kernel(qkv_ref, z_ref, ab_ref, convw_ref, alog_ref, dtb_ref, normw_ref, o_ref,
                u_s, w_s, qg_s, kdt_s, attn_s, gl_s, state_s, q_s, k_s, kb_s, rhs_s, dec_s, cv_s,
                *, seq):
    nchunk = seq // CHUNK
    ii = lax.broadcasted_iota(jnp.int32, (CHUNK, CHUNK), 0)
    jj = lax.broadcasted_iota(jnp.int32, (CHUNK, CHUNK), 1)
    causal = ii >= jj
    strict = ii > jj
    tri = jnp.where(causal, 1.0, 0.0).astype(F32)
    neg_a = -jnp.exp(alog_ref[...])
    dtb = dtb_ref[...]

    def conv_cols(c, r0, lo, buf):
        x = qkv_ref[pl.ds(r0, CHUNK), lo:lo + LANES].astype(F32)
        prev0 = pl.multiple_of(jnp.maximum(r0 - 16, 0), 16)
        prev = qkv_ref[pl.ds(prev0, 16), lo:lo + LANES].astype(F32)
        buf[0:8, :] = prev[8:16] * jnp.where(c > 0, 1.0, 0.0)
        buf[8:8 + CHUNK, :] = x
        w = convw_ref[:, lo:lo + LANES]
        y = (w[3:4] * x + w[2:3] * buf[7:7 + CHUNK, :] + w[1:2] * buf[6:6 + CHUNK, :]
             + w[0:1] * buf[5:5 + CHUNK, :])
        return _silu(y)

    def l2n(x):
        return x * lax.rsqrt(jnp.sum(x * x, axis=-1, keepdims=True) + EPS)

    tri_b = tri.astype(BF16)

    def chunk_cumsum(g):
        g1 = g.astype(BF16)
        r1 = g - g1.astype(F32)
        g2 = r1.astype(BF16)
        g3 = (r1 - g2.astype(F32)).astype(BF16)
        return _dot(tri_b, g1) + (_dot(tri_b, g2) + _dot(tri_b, g3))

    def operands(cc, slot):
        for sub in range(GDN_PREP_CHUNKS):
            c = cc * GDN_PREP_CHUNKS + sub
            r0 = pl.multiple_of(c * CHUNK, CHUNK)
            ab = ab_ref[pl.ds(r0, CHUNK), :]
            xg = ab + dtb
            softplus = jnp.maximum(xg, 0.0) + jnp.log(1.0 + jnp.exp(-jnp.abs(xg)))
            g_all = neg_a * softplus
            beta_all = _sigmoid(ab)
            gc_all = chunk_cumsum(g_all)
            gc_t = gc_all.T
            gl_s[c] = jnp.exp(gc_all[CHUNK - 1:CHUNK, :])
            for h in range(GDN_HEADS):
                n = sub * GDN_HEADS + h
                bufs = [cv_s.at[slot, 3 * n + i] for i in range(3)]
                q = l2n(conv_cols(c, r0, h * GDN_DK, bufs[0])) * (GDN_DK ** -0.5)
                k = l2n(conv_cols(c, r0, GDN_QK + h * GDN_DK, bufs[1]))
                v = conv_cols(c, r0, 2 * GDN_QK + h * GDN_DV, bufs[2])
                gcol = gc_all[:, h:h + 1]
                grow = gc_t[h:h + 1, :]
                beta = beta_all[:, GDN_HEADS + h:GDN_HEADS + h + 1]
                dec_s[slot, n] = jnp.where(causal, jnp.exp(gcol - grow), 0.0)
                eg = jnp.exp(gcol)
                kb = k * beta
                q_s[slot, n] = q.astype(BF16)
                k_s[slot, n] = k.astype(BF16)
                kb_s[slot, n] = kb.astype(BF16)
                rhs_s[slot, n] = jnp.concatenate([v * beta, kb * eg], axis=1).astype(BF16)
                qg_s[c, h] = (q * eg).astype(BF16)
                kd = k * jnp.exp(gc_all[CHUNK - 1:CHUNK, h:h + 1] - gcol)
                kdt_s[c, h] = kd.T.astype(BF16)

    def solve(cc, slot):
        items = [(cc * GDN_PREP_CHUNKS + sub, h, sub * GDN_HEADS + h)
                 for sub in range(GDN_PREP_CHUNKS) for h in range(GDN_HEADS)]
        kks = [_dot_nt(kb_s[slot, n], k_s[slot, n]) for _, _, n in items]
        qks = [_dot_nt(q_s[slot, n], k_s[slot, n]) for _, _, n in items]
        lows = [jnp.where(strict, kk * dec_s[slot, n], 0.0) for kk, (_, _, n) in zip(kks, items)]
        for qk, (c, h, n) in zip(qks, items):
            attn_s[c, h] = (qk * dec_s[slot, n]).astype(BF16)
        ts = _unit_lower_inverses(lows, ii, jj)
        sols = [_dot(t.astype(BF16), rhs_s[slot, n]) for t, (_, _, n) in zip(ts, items)]
        for sol, (c, h, _) in zip(sols, items):
            u_s[c, h] = sol[:, :GDN_DV]
            w_s[c, h] = sol[:, GDN_DV:].astype(BF16)

    ngroup = nchunk // GDN_PREP_CHUNKS
    operands(0, 0)

    def prep(i, carry):
        operands(2 * i + 1, 1)
        solve(2 * i, 0)
        operands(jnp.minimum(2 * i + 2, ngroup - 1), 0)
        solve(2 * i + 1, 1)
        return carry

    lax.fori_loop(0, ngroup // 2, prep, 0)

    state_s[...] = jnp.zeros_like(state_s)
    normw = normw_ref[...]

    def scan(c, carry):
        r0 = pl.multiple_of(c * CHUNK, CHUNK)
        gl = gl_s[c]
        heads = range(GDN_HEADS)
        ss = [state_s[h] for h in heads]
        wss = [_dot(w_s[c, h], ss[h]) for h in heads]
        qss = [_dot(qg_s[c, h], ss[h]) for h in heads]
        vns = [u_s[c, h] - wss[h] for h in heads]
        avs = [_dot(attn_s[c, h], vns[h]) for h in heads]
        kvs = [_dot(kdt_s[c, h], vns[h]) for h in heads]
        for h in heads:
            state_s[h] = ss[h] * gl[:, h:h + 1] + kvs[h]
            o = qss[h] + avs[h]
            z = z_ref[pl.ds(r0, CHUNK), h * GDN_DV:(h + 1) * GDN_DV].astype(F32)
            on = o * lax.rsqrt(jnp.mean(o * o, axis=-1, keepdims=True) + EPS) * normw
            o_ref[pl.ds(r0, CHUNK), h * GDN_DV:(h + 1) * GDN_DV] = (on * _silu(z)).astype(BF16)
        return carry

    lax.fori_loop(0, nchunk, scan, 0)


def _gdn_call(proj, ab, conv_w8, alog_p, dtb_p, normw, batch, seq):
    nchunk = seq // CHUNK
    hs = (nchunk, GDN_HEADS, CHUNK, CHUNK)
    ops = (2, GDN_PREP_CHUNKS * GDN_HEADS, CHUNK, CHUNK)
    return pl.pallas_call(
        functools.partial(_gdn_kernel, seq=seq),
        grid=(batch,),
        in_specs=[
            pl.BlockSpec((seq, 3 * GDN_QK), lambda b: (b, 0)),
            pl.BlockSpec((seq, GDN_V), lambda b: (b, 3)),
            pl.BlockSpec((seq, LANES), lambda b: (b, 0)),
            pl.BlockSpec((8, 3 * GDN_QK), lambda b: (0, 0)),
            pl.BlockSpec((1, LANES), lambda b: (0, 0)),
            pl.BlockSpec((1, LANES), lambda b: (0, 0)),
            pl.BlockSpec((1, GDN_DV), lambda b: (0, 0)),
        ],
        out_specs=pl.BlockSpec((seq, GDN_V), lambda b: (b, 0)),
        out_shape=jax.ShapeDtypeStruct((batch * seq, GDN_V), BF16),
        scratch_shapes=[
            pltpu.VMEM(hs, F32), pltpu.VMEM(hs, BF16), pltpu.VMEM(hs, BF16), pltpu.VMEM(hs, BF16),
            pltpu.VMEM(hs, BF16),
            pltpu.VMEM((nchunk, 1, LANES), F32),
            pltpu.VMEM((GDN_HEADS, GDN_DK, GDN_DV), F32),
            pltpu.VMEM(ops, BF16), pltpu.VMEM(ops, BF16), pltpu.VMEM(ops, BF16),
            pltpu.VMEM(ops[:3] + (2 * CHUNK,), BF16),
            pltpu.VMEM(ops, F32),
            pltpu.VMEM((2, 3 * ops[1], 8 + CHUNK, LANES), F32),
        ],
        compiler_params=pltpu.CompilerParams(
            dimension_semantics=("arbitrary",), vmem_limit_bytes=VMEM_LIMIT),
        name="gdn",
    )(proj, proj, ab, conv_w8, alog_p, dtb_p, normw)


def _ret_kernel(qk_ref, v_ref, g_ref, sin_ref, cos_ref, inner_ref, kdec_ref, qdec_ref, cdec_ref,
                o_ref, state_s, *, seq):
    nchunk = seq // CHUNK
    lane = lax.broadcasted_iota(jnp.int32, (CHUNK, RET_DK), 1)
    even = (lane % 2) == 0

    def rotate(x, sin, cos):
        nxt = pltpu.roll(x, RET_DK - 1, axis=1)
        prv = pltpu.roll(x, 1, axis=1)
        return x * cos + jnp.where(even, -nxt, prv) * sin

    state_s[...] = jnp.zeros_like(state_s)
    kdec = kdec_ref[...]
    qdec = qdec_ref[...]
    cdec = cdec_ref[...]

    def body(c, carry):
        r0 = pl.multiple_of(c * CHUNK, CHUNK)
        sin = sin_ref[pl.ds(r0, CHUNK), :]
        cos = cos_ref[pl.ds(r0, CHUNK), :]
        heads = range(RET_HEADS)
        qs = [rotate(qk_ref[pl.ds(r0, CHUNK), h * RET_DK:(h + 1) * RET_DK].astype(F32), sin, cos)
              for h in heads]
        ks = [rotate(qk_ref[pl.ds(r0, CHUNK), RET_QK + h * RET_DK:RET_QK + (h + 1) * RET_DK]
                     .astype(F32), sin, cos) * (RET_DK ** -0.5) for h in heads]
        vs = [v_ref[pl.ds(r0, CHUNK), h * RET_DV:(h + 1) * RET_DV].astype(F32) for h in heads]
        ss = [state_s[h] for h in heads]
        qks = [_dot_nt(qs[h], ks[h]) for h in heads]
        inters = [_dot(qs[h] * qdec[:, h:h + 1], ss[h]) for h in heads]
        kvs = [_dot((ks[h] * kdec[:, h:h + 1]).T, vs[h]) for h in heads]
        intras = [_dot(qks[h] * inner_ref[h], vs[h]) for h in heads]
        for h in heads:
            state_s[h] = ss[h] * cdec[:, h:h + 1] + kvs[h]
            o = intras[h] + inters[h]
            gate = g_ref[pl.ds(r0, CHUNK), h * RET_DV:(h + 1) * RET_DV].astype(F32)
            on = o * lax.rsqrt(jnp.mean(o * o, axis=-1, keepdims=True) + EPS)
            o_ref[pl.ds(r0, CHUNK), h * RET_DV:(h + 1) * RET_DV] = (on * _silu(gate)).astype(BF16)
        return carry

    lax.fori_loop(0, nchunk, body, 0)


def _ret_call(proj, sin, cos, inner, kdec, qdec, cdec, batch, seq):
    return pl.pallas_call(
        functools.partial(_ret_kernel, seq=seq),
        grid=(batch,),
        in_specs=[
            pl.BlockSpec((seq, 2 * RET_QK), lambda b: (b, 2)),
            pl.BlockSpec((seq, RET_V), lambda b: (b, 3)),
            pl.BlockSpec((seq, RET_V), lambda b: (b, 4)),
            pl.BlockSpec((seq, RET_DK), lambda b: (0, 0)),
            pl.BlockSpec((seq, RET_DK), lambda b: (0, 0)),
            pl.BlockSpec((RET_HEADS, CHUNK, CHUNK), lambda b: (0, 0, 0)),
            pl.BlockSpec((CHUNK, LANES), lambda b: (0, 0)),
            pl.BlockSpec((CHUNK, LANES), lambda b: (0, 0)),
            pl.BlockSpec((1, LANES), lambda b: (0, 0)),
        ],
        out_specs=pl.BlockSpec((seq, RET_V), lambda b: (b, 0)),
        out_shape=jax.ShapeDtypeStruct((batch * seq, RET_V), BF16),
        scratch_shapes=[pltpu.VMEM((RET_HEADS, RET_DK, RET_DV), F32)],
        compiler_params=pltpu.CompilerParams(
            dimension_semantics=("arbitrary",), vmem_limit_bytes=VMEM_LIMIT),
        name="retention",
    )(proj, proj, proj, sin, cos, inner, kdec, qdec, cdec)


def _retention_tables(seq):
    inv_freq = 1.0 / (ROPE_BASE ** jnp.linspace(0.0, 1.0, RET_DK // 2, dtype=F32))
    ang = jnp.arange(seq, dtype=F32)[:, None] * inv_freq[None, :]
    sin = jnp.repeat(jnp.sin(ang), 2, axis=-1)
    cos = jnp.repeat(jnp.cos(ang), 2, axis=-1)
    log_gamma = jnp.log(1.0 - 2.0 ** (-5.0 - jnp.arange(RET_HEADS, dtype=F32)))
    idx = jnp.arange(CHUNK, dtype=F32)
    causal = jnp.tril(jnp.ones((CHUNK, CHUNK), dtype=bool))
    rel = jnp.where(causal, idx[:, None] - idx[None, :], 0.0)
    inner = jnp.where(causal, jnp.exp(rel[None] * log_gamma[:, None, None]), 0.0)
    k_decay = jnp.exp(log_gamma[:, None] * (CHUNK - 1.0 - idx)[None, :])
    q_decay = jnp.exp(log_gamma[:, None] * (idx + 1.0)[None, :])
    chunk_decay = jnp.exp(log_gamma * CHUNK)
    pad = LANES - RET_HEADS
    kdec = jnp.pad(k_decay.T, ((0, 0), (0, pad)))
    qdec = jnp.pad(q_decay.T, ((0, 0), (0, pad)))
    cdec = jnp.pad(chunk_decay[None, :], ((0, 0), (0, pad)))
    return sin, cos, inner, kdec, qdec, cdec


def _merge_kernel(x_ref, ya_ref, yb_ref, ma_ref, mb_ref, wa_ref, wr_ref, wo_ref, nw_ref,
                  wrt_ref, br_ref, h_ref, xn_ref, ridx_ref, rw_ref, cnt_ref):
    a = _dot(ya_ref[...], wa_ref[...])
    r = _dot(yb_ref[...], wr_ref[...])
    merged = _sigmoid(ma_ref[...].astype(F32)) * a + _sigmoid(mb_ref[...].astype(F32)) * r
    h = x_ref[...] + _dot(merged.astype(BF16), wo_ref[...])
    h_ref[...] = h
    xn = h * lax.rsqrt(jnp.mean(h * h, axis=-1, keepdims=True) + EPS) * nw_ref[...]
    xn_ref[...] = _pack_bf16_pairs(xn)

    tm = xn.shape[0]
    xh = xn.astype(BF16)
    xl = (xn - xh.astype(F32)).astype(BF16)
    parts = _dot(jnp.concatenate([xh, xl], axis=0), wrt_ref[...])
    logits = (parts[:tm, :LANES] + (parts[tm:, :LANES] + parts[:tm, LANES:]
                                    + parts[tm:, LANES:])) + br_ref[...]
    lane = lax.broadcasted_iota(jnp.int32, (xn.shape[0], LANES), 1)
    neg = -jnp.inf
    gl = jnp.where((lane >= N_EXPERTS) & (lane < N_EXPERTS + N_GROUPS), logits, neg)
    gmax = jnp.max(gl, axis=-1, keepdims=True)
    gidx = jnp.min(jnp.where(gl == gmax, lane, LANES), axis=-1, keepdims=True) - N_EXPERTS
    g_w = 1.0 / jnp.sum(jnp.exp(gl - gmax), axis=-1, keepdims=True)
    el = jnp.where((lane // EXPERTS_PER_GROUP == gidx) & (lane < N_EXPERTS), logits, neg)
    m1 = jnp.max(el, axis=-1, keepdims=True)
    i1 = jnp.min(jnp.where(el == m1, lane, LANES), axis=-1, keepdims=True)
    el2 = jnp.where(lane == i1, neg, el)
    m2 = jnp.max(el2, axis=-1, keepdims=True)
    i2 = jnp.min(jnp.where(el2 == m2, lane, LANES), axis=-1, keepdims=True)
    e2 = jnp.exp(m2 - m1)
    p1 = g_w / (1.0 + e2)
    p2 = g_w * e2 / (1.0 + e2)
    ridx_ref[...] = jnp.where(lane == 0, i1, jnp.where(lane == 1, i2, 0))
    rw_ref[...] = jnp.where(lane == 0, p1, jnp.where(lane == 1, p2, 0.0))

    @pl.when(pl.program_id(0) == 0)
    def _():
        cnt_ref[...] = jnp.zeros_like(cnt_ref)

    onehot = jnp.where((lane == i1) | (lane == i2), 1.0, 0.0)
    cnt_ref[...] += jnp.broadcast_to(jnp.sum(onehot, axis=0, keepdims=True), cnt_ref.shape)


def _merge_call(x2, ya, yb, proj, wa, wr, wo, nw, w_router, b_router, tm=512):
    m = x2.shape[0]
    full = lambda shape: pl.BlockSpec(shape, lambda i: (0, 0))
    return pl.pallas_call(
        _merge_kernel,
        grid=(m // tm,),
        in_specs=[
            pl.BlockSpec((tm, D_MODEL), lambda i: (i, 0)),
            pl.BlockSpec((tm, GDN_V), lambda i: (i, 0)),
            pl.BlockSpec((tm, RET_V), lambda i: (i, 0)),
            pl.BlockSpec((tm, D_MODEL), lambda i: (i, 5)),
            pl.BlockSpec((tm, D_MODEL), lambda i: (i, 6)),
            full((GDN_V, D_MODEL)), full((RET_V, D_MODEL)), full((D_MODEL, D_MODEL)),
            full((1, D_MODEL)),
            full((D_MODEL, 2 * LANES)), full((1, LANES)),
        ],
        out_specs=[
            pl.BlockSpec((tm, D_MODEL), lambda i: (i, 0)),
            pl.BlockSpec((tm, HALF), lambda i: (i, 0)),
            pl.BlockSpec((tm, LANES), lambda i: (i, 0)),
            pl.BlockSpec((tm, LANES), lambda i: (i, 0)),
            pl.BlockSpec((8, LANES), lambda i: (0, 0)),
        ],
        out_shape=[
            jax.ShapeDtypeStruct((m, D_MODEL), F32),
            jax.ShapeDtypeStruct((m, HALF), I32),
            jax.ShapeDtypeStruct((m, LANES), I32),
            jax.ShapeDtypeStruct((m, LANES), F32),
            jax.ShapeDtypeStruct((8, LANES), F32),
        ],
        compiler_params=pltpu.CompilerParams(
            dimension_semantics=("arbitrary",), vmem_limit_bytes=VMEM_LIMIT),
        name="merge_router",
    )(x2, ya, yb, proj, proj, wa, wr, wo, nw, w_router, b_router)


def _slot_counts(m):
    n_pad = N_EXPERTS * SLOT_TILE
    n_slots = TOP_K * m + n_pad
    return n_slots, n_slots // SLOT_TILE, n_pad


def _lane_prefix_sum(x, lane):
    s = 1
    while s < LANES:
        x = x + jnp.where(lane >= s, pltpu.roll(x, s, axis=1), 0.0)
        s *= 2
    return x


def _plan_kernel(ridx_ref, cnt_ref, slots_ref, tile_ref, pad_ref, carry_s, off_s, *, n_slots):
    i = pl.program_id(0)
    lane = lax.broadcasted_iota(I32, (PLAN_TILE, LANES), 1)
    row = lax.broadcasted_iota(I32, (PLAN_TILE, LANES), 0)
    lane1 = lane[0:1]

    @pl.when(i == 0)
    def _():
        cnt = cnt_ref[0:1, :]
        tile = float(SLOT_TILE)
        padded = jnp.floor((cnt + (tile - 1.0)) / tile) * tile
        incl = _lane_prefix_sum(padded, lane1)
        off = incl - padded
        off_s[...] = off
        carry_s[...] = jnp.zeros_like(carry_s)
        first = (row * SLOT_TILE).astype(F32)
        ended = jnp.where((lane < N_EXPERTS) & (incl <= first), 1.0, 0.0)
        tile_ref[...] = jnp.broadcast_to(
            jnp.sum(ended, axis=-1, keepdims=True), (PLAN_TILE, LANES)).astype(I32)
        rowf = row.astype(F32)
        spill = (n_slots + lane * SLOT_TILE + row).astype(F32)
        pad_ref[...] = jnp.where(rowf < padded - cnt, off + cnt + rowf, spill).astype(I32)

    strict = jnp.where(row[:, 0:1] > lax.broadcasted_iota(I32, (PLAN_TILE, PLAN_TILE), 1),
                       1.0, 0.0).astype(BF16)
    off = off_s[...]
    carry = carry_s[...]
    for sb in range(PLAN_STEP // PLAN_TILE):
        rows = pl.ds(sb * PLAN_TILE, PLAN_TILE)
        e1 = ridx_ref[rows, 0:1]
        e2 = ridx_ref[rows, 1:2]
        onehot = jnp.where((lane == e1) | (lane == e2), 1.0, 0.0)
        pos = _dot(strict, onehot.astype(BF16)) + (carry + off)
        s1 = jnp.sum(jnp.where(lane == e1, pos, 0.0), axis=-1, keepdims=True)
        s2 = jnp.sum(jnp.where(lane == e2, pos, 0.0), axis=-1, keepdims=True)
        both = jnp.where(lane == 0, s1, jnp.where(lane == 1, s2, 0.0))
        for q in range(PLAN_TILE // LANES):
            t = both[q * LANES:(q + 1) * LANES].T
            c0 = sb * PLAN_TILE + q * LANES
            slots_ref[:, c0:c0 + LANES] = t[0:8].astype(I32)
        carry = carry + jnp.sum(onehot, axis=0, keepdims=True)
    carry_s[...] = carry


def _plan_call(ridx, cnt):
    m = ridx.shape[0]
    n_slots, n_tiles, _ = _slot_counts(m)
    assert n_tiles <= PLAN_TILE and SLOT_TILE == PLAN_TILE
    return pl.pallas_call(
        functools.partial(_plan_kernel, n_slots=n_slots),
        grid=(m // PLAN_STEP,),
        in_specs=[pl.BlockSpec((PLAN_STEP, LANES), lambda i: (i, 0)),
                  pl.BlockSpec((8, LANES), lambda i: (0, 0))],
        out_specs=[
            pl.BlockSpec((8, PLAN_STEP), lambda i: (0, i)),
            pl.BlockSpec((PLAN_TILE, LANES), lambda i: (0, 0)),
            pl.BlockSpec((PLAN_TILE, LANES), lambda i: (0, 0)),
        ],
        out_shape=[
            jax.ShapeDtypeStruct((8, m), I32),
            jax.ShapeDtypeStruct((PLAN_TILE, LANES), I32),
            jax.ShapeDtypeStruct((PLAN_TILE, LANES), I32),
        ],
        scratch_shapes=[pltpu.VMEM((1, LANES), F32), pltpu.VMEM((1, LANES), F32)],
        compiler_params=pltpu.CompilerParams(dimension_semantics=("arbitrary",)),
        name="dispatch_plan",
    )(ridx, cnt)


def _sc_mesh():
    return plsc.VectorSubcoreMesh(core_axis_name="c", subcore_axis_name="s")


def _sc_worker():
    return lax.axis_index("s") * 2 + lax.axis_index("c")


def _sc_dispatch(xn, slot1, slot2, pad_slots, zeros):
    m = xn.shape[0]
    n_slots, _, n_pad = _slot_counts(m)
    per = m // SC_WORKERS
    pad_per = n_pad // SC_WORKERS

    @functools.partial(
        pl.kernel, mesh=_sc_mesh(),
        out_type=jax.ShapeDtypeStruct((n_slots + n_pad, HALF), I32),
        scratch_types=[pltpu.VMEM((SC_ROWS,), I32), pltpu.VMEM((SC_ROWS,), I32),
                       pltpu.VMEM((SC_ROWS, HALF), I32), pltpu.SemaphoreType.DMA],
        name="sc_dispatch")
    def k(x_hbm, s1_hbm, s2_hbm, p_hbm, z_hbm, o_hbm, i1_v, i2_v, rows_v, sem):
        wid = _sc_worker()

        @pl.loop(0, per // SC_ROWS)
        def _(ci):
            t0 = wid * per + ci * SC_ROWS
            pltpu.sync_copy(x_hbm.at[pl.ds(t0, SC_ROWS)], rows_v)
            pltpu.sync_copy(s1_hbm.at[pl.ds(t0, SC_ROWS)], i1_v)
            pltpu.sync_copy(s2_hbm.at[pl.ds(t0, SC_ROWS)], i2_v)
            c1 = pltpu.async_copy(rows_v, o_hbm.at[i1_v], sem)
            c2 = pltpu.async_copy(rows_v, o_hbm.at[i2_v], sem)
            c1.wait()
            c2.wait()

        pltpu.sync_copy(z_hbm, rows_v)

        @pl.loop(0, pad_per // SC_ROWS)
        def _(ci):
            t0 = wid * pad_per + ci * SC_ROWS
            pltpu.sync_copy(p_hbm.at[pl.ds(t0, SC_ROWS)], i1_v)
            pltpu.async_copy(rows_v, o_hbm.at[i1_v], sem).wait()

    return k(xn, slot1, slot2, pad_slots, zeros)


def _sc_collect(ys, slot1, slot2):
    m = slot1.shape[0]
    per = m // SC_WORKERS
    row = jax.ShapeDtypeStruct((m, HALF), I32)

    @functools.partial(
        pl.kernel, mesh=_sc_mesh(), out_type=[row, row],
        scratch_types=[pltpu.VMEM((SC_ROWS,), I32), pltpu.VMEM((SC_ROWS, HALF), I32),
                       pltpu.SemaphoreType.DMA],
        name="sc_collect")
    def k(y_hbm, s1_hbm, s2_hbm, g1_hbm, g2_hbm, i_v, rows_v, sem):
        wid = _sc_worker()

        @pl.loop(0, per // SC_ROWS)
        def _(ci):
            t0 = wid * per + ci * SC_ROWS
            for s_hbm, g_hbm in ((s1_hbm, g1_hbm), (s2_hbm, g2_hbm)):
                pltpu.sync_copy(s_hbm.at[pl.ds(t0, SC_ROWS)], i_v)
                pltpu.async_copy(y_hbm.at[i_v], rows_v, sem).wait()
                pltpu.sync_copy(rows_v, g_hbm.at[pl.ds(t0, SC_ROWS)])

    return k(ys, slot1, slot2)


def _expert_kernel(te_ref, xs_ref, wg_hbm, wu_hbm, wd_hbm, ys_ref, wg_b, wu_b, wd_b,
                   wg_f, wu_f, wd_f, sem, slot_s):
    j = pl.program_id(0)
    n = pl.num_programs(0)
    e = te_ref[j]
    prev = te_ref[jnp.maximum(j - 1, 0)]
    valid = e < N_EXPERTS

    def weight_copies(expert, slot):
        return [pltpu.make_async_copy(hbm.at[expert], buf.at[slot], sem.at[slot, i])
                for i, (hbm, buf) in enumerate(((wg_hbm, wg_f), (wu_hbm, wu_f), (wd_hbm, wd_f)))]

    @pl.when((j == 0) & valid)
    def _():
        slot_s[0] = 0
        for c in weight_copies(e, 0):
            c.start()

    @pl.when(((j == 0) | (e != prev)) & valid)
    def _():
        slot = slot_s[0]
        for c in weight_copies(e, slot):
            c.wait()
        k = lax.while_loop(lambda k: (k < n) & (te_ref[jnp.minimum(k, n - 1)] == e),
                           lambda k: k + 1, j + 1)
        nxt = te_ref[jnp.minimum(k, n - 1)]

        @pl.when((k < n) & (nxt < N_EXPERTS))
        def _():
            for c in weight_copies(nxt, 1 - slot):
                c.start()

        wg_b[...] = wg_f[slot].astype(BF16)
        wu_b[...] = wu_f[slot].astype(BF16)
        wd_b[...] = wd_f[slot].astype(BF16)
        slot_s[0] = 1 - slot

    @pl.when(valid)
    def _():
        rows = [pl.ds(i * (SLOT_TILE // 2), SLOT_TILE // 2) for i in range(2)]
        xs = [_unpack_bf16_pairs(xs_ref[r, :]).astype(BF16) for r in rows]
        gs = [_dot(x, wg_b[...]) for x in xs]
        us = [_dot(x, wu_b[...]) for x in xs]
        hids = [(_silu(g) * u).astype(BF16) for g, u in zip(gs, us)]
        ys = [_dot(hid, wd_b[...]) for hid in hids]
        for r, y in zip(rows, ys):
            ys_ref[r, :] = _pack_bf16_pairs(y)

    @pl.when(e >= N_EXPERTS)
    def _():
        ys_ref[...] = jnp.zeros_like(ys_ref)


def _expert_call(tile_expert, xs, wg, wu, wd, n_tiles):
    hbm = pl.BlockSpec(memory_space=pl.ANY)
    return pl.pallas_call(
        _expert_kernel,
        grid_spec=pltpu.PrefetchScalarGridSpec(
            num_scalar_prefetch=1,
            grid=(n_tiles,),
            in_specs=[pl.BlockSpec((SLOT_TILE, HALF), lambda j, te: (j, 0)), hbm, hbm, hbm],
            out_specs=pl.BlockSpec((SLOT_TILE, HALF), lambda j, te: (j, 0)),
            scratch_shapes=[
                pltpu.VMEM((D_MODEL, D_EXPERT), BF16), pltpu.VMEM((D_MODEL, D_EXPERT), BF16),
                pltpu.VMEM((D_EXPERT, D_MODEL), BF16),
                pltpu.VMEM((2, D_MODEL, D_EXPERT), F32), pltpu.VMEM((2, D_MODEL, D_EXPERT), F32),
                pltpu.VMEM((2, D_EXPERT, D_MODEL), F32),
                pltpu.SemaphoreType.DMA((2, 3)),
                pltpu.SMEM((1,), I32),
            ],
        ),
        out_shape=jax.ShapeDtypeStruct((n_tiles * SLOT_TILE, HALF), I32),
        compiler_params=pltpu.CompilerParams(
            dimension_semantics=("arbitrary",), vmem_limit_bytes=VMEM_LIMIT),
        name="experts",
    )(tile_expert, xs, wg, wu, wd)


def _final_kernel(h_ref, g1_ref, g2_ref, rw_ref, nw_ref, o_ref):
    rw = rw_ref[...]
    y = rw[:, 0:1] * _unpack_bf16_pairs(g1_ref[...]) + rw[:, 1:2] * _unpack_bf16_pairs(g2_ref[...])
    h = h_ref[...] + y
    o_ref[...] = h * lax.rsqrt(jnp.mean(h * h, axis=-1, keepdims=True) + EPS) * nw_ref[...]


def _final_call(h1, g1, g2, rw, nw, tm=512):
    m = h1.shape[0]
    return pl.pallas_call(
        _final_kernel,
        grid=(m // tm,),
        in_specs=[
            pl.BlockSpec((tm, D_MODEL), lambda i: (i, 0)),
            pl.BlockSpec((tm, HALF), lambda i: (i, 0)),
            pl.BlockSpec((tm, HALF), lambda i: (i, 0)),
            pl.BlockSpec((tm, LANES), lambda i: (i, 0)),
            pl.BlockSpec((1, D_MODEL), lambda i: (0, 0)),
        ],
        out_specs=pl.BlockSpec((tm, D_MODEL), lambda i: (i, 0)),
        out_shape=jax.ShapeDtypeStruct((m, D_MODEL), F32),
        compiler_params=pltpu.CompilerParams(dimension_semantics=("arbitrary",)),
        name="combine_final",
    )(h1, g1, g2, rw, nw)


def _pad_lanes(a):
    return jnp.pad(a, ((0, 0), (0, LANES - a.shape[1])))


def kernel(x, norm_mix_w, w_in, conv_w, A_log, dt_bias, gdn_norm_w, w_up_gdn, w_up_ret, w_out,
           norm_ffn_w, w_group, b_group, w_expert, b_expert, w_gate, w_up, w_down, norm_final_w):
    batch, seq, d = x.shape
    m = batch * seq
    h = x.reshape(m, d)
    depth = w_in.shape[0]
    sin, cos, inner, kdec, qdec, cdec = _retention_tables(seq)
    for l in range(depth):
        o_ab = 3 * GDN_QK
        o_z = o_ab + 2 * GDN_HEADS
        wl = w_in[l]
        w_main = jnp.concatenate([wl[:, :o_ab], wl[:, o_z:]], axis=1).astype(BF16)
        w_ab = _pad_lanes(wl[:, o_ab:o_z]).astype(BF16)
        proj, ab = _proj_call(h, norm_mix_w[l][None, :], w_main, w_ab)

        conv8 = jnp.pad(conv_w[l], ((0, 8 - GDN_CONV), (0, 0)))
        ya = _gdn_call(proj, ab, conv8, _pad_lanes(A_log[l][None, :]), _pad_lanes(dt_bias[l][None, :]),
                       gdn_norm_w[l][None, :], batch, seq)
        yb = _ret_call(proj, sin, cos, inner, kdec, qdec, cdec, batch, seq)

        w_router = _pad_lanes(jnp.concatenate([w_expert[l], w_group[l]], axis=1))
        wr_hi = w_router.astype(BF16)
        wr_lo = (w_router - wr_hi.astype(F32)).astype(BF16)
        b_router = _pad_lanes(jnp.concatenate([b_expert[l], b_group[l]])[None, :])
        h1, xn, ridx, rw, cnt = _merge_call(
            h, ya, yb, proj, w_up_gdn[l].astype(BF16), w_up_ret[l].astype(BF16), w_out[l].astype(BF16),
            norm_ffn_w[l][None, :], jnp.concatenate([wr_hi, wr_lo], axis=1), b_router)

        _, n_tiles, _ = _slot_counts(m)
        slots, tiles, pads = _plan_call(ridx, cnt)
        slot1, slot2 = slots[0], slots[1]
        pad_slots = pads[:, :N_EXPERTS].T.reshape(-1)
        xs = _sc_dispatch(xn, slot1, slot2, pad_slots, jnp.zeros((SC_ROWS, HALF), I32))
        ys = _expert_call(tiles[:n_tiles, 0], xs, w_gate[l], w_up[l], w_down[l], n_tiles)
        g1, g2 = _sc_collect(ys, slot1, slot2)

        assert depth == 1
        h = _final_call(h1, g1, g2, rw, norm_final_w[None, :])
    return h.reshape(batch, seq, d)
```

```python
import functools
import math

import jax
import jax.numpy as jnp
from jax import lax
from jax.experimental import pallas as pl
from jax.experimental.pallas import tpu as pltpu
from jax.experimental.pallas import tpu_sc as plsc

F32 = jnp.float32
BF16 = jnp.bfloat16
I32 = jnp.int32
U32 = jnp.uint32

D_MODEL = 1024
EPS = 1e-6
GDN_HEADS = 4
GDN_DK = 128
GDN_DV = 128
GDN_CONV = 4
RET_HEADS = 4
RET_DK = 128
RET_DV = 256
ROPE_BASE = 10000.0
N_GROUPS = 4
EXPERTS_PER_GROUP = 8
N_EXPERTS = N_GROUPS * EXPERTS_PER_GROUP
D_EXPERT = 512

GDN_QK = GDN_HEADS * GDN_DK
GDN_V = GDN_HEADS * GDN_DV
RET_QK = RET_HEADS * RET_DK
RET_V = RET_HEADS * RET_DV

LANES = 128
CHUNK = 128
INV_BLOCK = 16
GDN_PREP_CHUNKS = 2
VMEM_LIMIT = 56 * 1024 * 1024

TOP_K = 2
SLOT_TILE = 256
PLAN_TILE = 256
PLAN_STEP = 1024
HALF = D_MODEL // 2
SC_WORKERS = 32
SC_ROWS = 64

PROJ_COLS = 3 * GDN_QK + GDN_V + 2 * RET_QK + 2 * RET_V + 2 * D_MODEL


def _silu(x):
    return x / (1.0 + jnp.exp(-x))


def _sigmoid(x):
    return 1.0 / (1.0 + jnp.exp(-x))


def _dot(a, b):
    return jnp.dot(a, b, preferred_element_type=F32)


def _dot_nt(a, b):
    return lax.dot_general(a, b, (((1,), (1,)), ((), ())), preferred_element_type=F32)


def _pack_bf16_pairs(x):
    bits = lax.bitcast_convert_type(x.astype(BF16).astype(F32), U32)
    packed = (bits[:, :HALF] >> 16) | (bits[:, HALF:] & jnp.uint32(0xFFFF0000))
    return lax.bitcast_convert_type(packed, I32)


def _unpack_bf16_pairs(p):
    p = lax.bitcast_convert_type(p, U32)
    lo = lax.bitcast_convert_type(p << 16, F32)
    hi = lax.bitcast_convert_type(p & jnp.uint32(0xFFFF0000), F32)
    return jnp.concatenate([lo, hi], axis=1)


def _proj_kernel(x_ref, nw_ref, w_ref, wab_ref, proj_ref, ab_ref, u_ref):
    j = pl.program_id(1)

    @pl.when(j == 0)
    def _():
        x = x_ref[...]
        u = x * lax.rsqrt(jnp.mean(x * x, axis=-1, keepdims=True) + EPS) * nw_ref[...]
        ub = u.astype(BF16)
        u_ref[...] = ub
        ab_ref[...] = _dot(ub, wab_ref[...])

    proj_ref[...] = _dot(u_ref[...], w_ref[...]).astype(BF16)


def _proj_call(x2, norm_w, w_main, w_ab, tm=1024, tn=1024):
    m = x2.shape[0]
    return pl.pallas_call(
        _proj_kernel,
        grid=(m // tm, PROJ_COLS // tn),
        in_specs=[
            pl.BlockSpec((tm, D_MODEL), lambda i, j: (i, 0)),
            pl.BlockSpec((1, D_MODEL), lambda i, j: (0, 0)),
            pl.BlockSpec((D_MODEL, tn), lambda i, j: (0, j)),
            pl.BlockSpec((D_MODEL, LANES), lambda i, j: (0, 0)),
        ],
        out_specs=[
            pl.BlockSpec((tm, tn), lambda i, j: (i, j)),
            pl.BlockSpec((tm, LANES), lambda i, j: (i, 0)),
        ],
        out_shape=[
            jax.ShapeDtypeStruct((m, PROJ_COLS), BF16),
            jax.ShapeDtypeStruct((m, LANES), F32),
        ],
        scratch_shapes=[pltpu.VMEM((tm, D_MODEL), BF16)],
        compiler_params=pltpu.CompilerParams(
            dimension_semantics=("arbitrary", "arbitrary"), vmem_limit_bytes=VMEM_LIMIT),
        name="proj",
    )(x2, norm_w, w_main, w_ab)


def _unit_lower_inverses(lows, ii, jj):
    eye = jnp.where(ii == jj, 1.0, 0.0).astype(F32)
    in_block = (ii // INV_BLOCK) == (jj // INV_BLOCK)
    ps = [jnp.where(in_block, -low, 0.0) for low in lows]
    ts = [eye + p for p in ps]
    span = 2
    while span < INV_BLOCK:
        ps = [_dot(p, p) for p in ps]
        ts = [t + _dot(t, p) for t, p in zip(ts, ps)]
        span *= 2
    s = INV_BLOCK
    while s < CHUNK:
        off_diag = ((ii // (2 * s)) == (jj // (2 * s))) & ((ii // s) != (jj // s))
        xs = [_dot(jnp.where(off_diag, low, 0.0), t) for low, t in zip(lows, ts)]
        ts = [t - _dot(t, x) for t, x in zip(ts, xs)]
        s *= 2
    return ts


def _gdn_kernel(qkv_ref, z_ref, ab_ref, convw_ref, alog_ref, dtb_ref, normw_ref, o_ref,
                b_s, o0_s, m_s, qp_s, gl_s, state_s, q_s, k_s, kb_s, rhs_s, dec_s, cv_s, qg_o, kdt_o,
                *, seq):
    nchunk = seq // CHUNK
    ii = lax.broadcasted_iota(jnp.int32, (CHUNK, CHUNK), 0)
    jj = lax.broadcasted_iota(jnp.int32, (CHUNK, CHUNK), 1)
    causal = ii >= jj
    strict = ii > jj
    tri = jnp.where(causal, 1.0, 0.0).astype(F32)
    neg_a = -jnp.exp(alog_ref[...])
    dtb = dtb_ref[...]

    def conv_cols(c, r0, lo, buf):
        x = qkv_ref[pl.ds(r0, CHUNK), lo:lo + LANES].astype(F32)
        prev0 = pl.multiple_of(jnp.maximum(r0 - 16, 0), 16)
        prev = qkv_ref[pl.ds(prev0, 16), lo:lo + LANES].astype(F32)
        buf[0:8, :] = prev[8:16] * jnp.where(c > 0, 1.0, 0.0)
        buf[8:8 + CHUNK, :] = x
        w = convw_ref[:, lo:lo + LANES]
        y = (w[3:4] * x + w[2:3] * buf[7:7 + CHUNK, :] + w[1:2] * buf[6:6 + CHUNK, :]
             + w[0:1] * buf[5:5 + CHUNK, :])
        return _silu(y)

    def l2n(x):
        return x * lax.rsqrt(jnp.sum(x * x, axis=-1, keepdims=True) + EPS)

    tri_b = tri.astype(BF16)

    def chunk_cumsum(g):
        g1 = g.astype(BF16)
        r1 = g - g1.astype(F32)
        g2 = r1.astype(BF16)
        g3 = (r1 - g2.astype(F32)).astype(BF16)
        return _dot(tri_b, g1) + (_dot(tri_b, g2) + _dot(tri_b, g3))

    def operands(cc, slot):
        for sub in range(GDN_PREP_CHUNKS):
            c = cc * GDN_PREP_CHUNKS + sub
            r0 = pl.multiple_of(c * CHUNK, CHUNK)
            ab = ab_ref[pl.ds(r0, CHUNK), :]
            xg = ab + dtb
            softplus = jnp.maximum(xg, 0.0) + jnp.log(1.0 + jnp.exp(-jnp.abs(xg)))
            g_all = neg_a * softplus
            beta_all = _sigmoid(ab)
            gc_all = chunk_cumsum(g_all)
            gc_t = gc_all.T
            gl_s[c] = jnp.exp(gc_all[CHUNK - 1:CHUNK, :])
            for h in range(GDN_HEADS):
                n = sub * GDN_HEADS + h
                bufs = [cv_s.at[slot, 3 * n + i] for i in range(3)]
                q = l2n(conv_cols(c, r0, h * GDN_DK, bufs[0])) * (GDN_DK ** -0.5)
                k = l2n(conv_cols(c, r0, GDN_QK + h * GDN_DK, bufs[1]))
                v = conv_cols(c, r0, 2 * GDN_QK + h * GDN_DV, bufs[2])
                gcol = gc_all[:, h:h + 1]
                grow = gc_t[h:h + 1, :]
                beta = beta_all[:, GDN_HEADS + h:GDN_HEADS + h + 1]
                dec_s[slot, n] = jnp.where(causal, jnp.exp(gcol - grow), 0.0)
                eg = jnp.exp(gcol)
                kb = k * beta
                q_s[slot, n] = q.astype(BF16)
                k_s[slot, n] = k.astype(BF16)
                kb_s[slot, n] = kb.astype(BF16)
                rhs_s[slot, n] = jnp.concatenate([v * beta, kb * eg], axis=1).astype(BF16)
                qg_o[slot, n] = q * eg
                kd = k * jnp.exp(gc_all[CHUNK - 1:CHUNK, h:h + 1] - gcol)
                kdt_o[slot, n] = kd.T.astype(BF16)

    def solve(cc, slot):
        items = [(cc * GDN_PREP_CHUNKS + sub, h, sub * GDN_HEADS + h)
                 for sub in range(GDN_PREP_CHUNKS) for h in range(GDN_HEADS)]
        kks = [_dot_nt(kb_s[slot, n], k_s[slot, n]) for _, _, n in items]
        qks = [_dot_nt(q_s[slot, n], k_s[slot, n]) for _, _, n in items]
        lows = [jnp.where(strict, kk * dec_s[slot, n], 0.0) for kk, (_, _, n) in zip(kks, items)]
        attns = [(qk * dec_s[slot, n]).astype(BF16) for qk, (_, _, n) in zip(qks, items)]
        ts = _unit_lower_inverses(lows, ii, jj)
        uws = [_dot(t.astype(BF16), rhs_s[slot, n]).astype(BF16) for t, (_, _, n) in zip(ts, items)]
        kds = [_dot(kdt_o[slot, n], uw) for uw, (_, _, n) in zip(uws, items)]
        ats = [_dot(attn, uw) for attn, uw in zip(attns, uws)]
        for kd_uw, at_uw, (c, h, n) in zip(kds, ats, items):
            b_s[c, h] = kd_uw[:, :GDN_DV]
            m_s[c, h] = (-kd_uw[:, GDN_DV:]).astype(BF16)
            o0_s[c, h] = at_uw[:, :GDN_DV]
            qp_s[c, h] = (qg_o[slot, n] - at_uw[:, GDN_DV:]).astype(BF16)

    ngroup = nchunk // GDN_PREP_CHUNKS
    operands(0, 0)

    def prep(i, carry):
        operands(2 * i + 1, 1)
        solve(2 * i, 0)
        operands(jnp.minimum(2 * i + 2, ngroup - 1), 0)
        solve(2 * i + 1, 1)
        return carry

    lax.fori_loop(0, ngroup // 2, prep, 0)

    state_s[...] = jnp.zeros_like(state_s)
    normw = normw_ref[...]

    def scan(c, carry):
        r0 = pl.multiple_of(c * CHUNK, CHUNK)
        gl = gl_s[c]
        heads = range(GDN_HEADS)
        ss = [state_s[h] for h in heads]
        sbs = [s.astype(BF16) for s in ss]
        mss = [_dot(m_s[c, h], sbs[h]) for h in heads]
        qss = [_dot(qp_s[c, h], sbs[h]) for h in heads]
        for h in heads:
            state_s[h] = ss[h] * gl[:, h:h + 1] + (mss[h] + b_s[c, h])
            o = qss[h] + o0_s[c, h]
            z = z_ref[pl.ds(r0, CHUNK), h * GDN_DV:(h + 1) * GDN_DV].astype(F32)
            on = o * lax.rsqrt(jnp.mean(o * o, axis=-1, keepdims=True) + EPS) * normw
            o_ref[pl.ds(r0, CHUNK), h * GDN_DV:(h + 1) * GDN_DV] = (on * _silu(z)).astype(BF16)
        return carry

    lax.fori_loop(0, nchunk, scan, 0)


def _gdn_call(proj, ab, conv_w8, alog_p, dtb_p, normw, batch, seq):
    nchunk = seq // CHUNK
    hs = (nchunk, GDN_HEADS, CHUNK, CHUNK)
    ops = (2, GDN_PREP_CHUNKS * GDN_HEADS, CHUNK, CHUNK)
    return pl.pallas_call(
        functools.partial(_gdn_kernel, seq=seq),
        grid=(batch,),
        in_specs=[
            pl.BlockSpec((seq, 3 * GDN_QK), lambda b: (b, 0)),
            pl.BlockSpec((seq, GDN_V), lambda b: (b, 3)),
            pl.BlockSpec((seq, LANES), lambda b: (b, 0)),
            pl.BlockSpec((8, 3 * GDN_QK), lambda b: (0, 0)),
            pl.BlockSpec((1, LANES), lambda b: (0, 0)),
            pl.BlockSpec((1, LANES), lambda b: (0, 0)),
            pl.BlockSpec((1, GDN_DV), lambda b: (0, 0)),
        ],
        out_specs=pl.BlockSpec((seq, GDN_V), lambda b: (b, 0)),
        out_shape=jax.ShapeDtypeStruct((batch * seq, GDN_V), BF16),
        scratch_shapes=[
            pltpu.VMEM(hs, F32), pltpu.VMEM(hs, F32), pltpu.VMEM(hs, BF16), pltpu.VMEM(hs, BF16),
            pltpu.VMEM((nchunk, 1, LANES), F32),
            pltpu.VMEM((GDN_HEADS, GDN_DK, GDN_DV), F32),
            pltpu.VMEM(ops, BF16), pltpu.VMEM(ops, BF16), pltpu.VMEM(ops, BF16),
            pltpu.VMEM(ops[:3] + (2 * CHUNK,), BF16),
            pltpu.VMEM(ops, F32),
            pltpu.VMEM((2, 3 * ops[1], 8 + CHUNK, LANES), F32),
            pltpu.VMEM(ops, F32), pltpu.VMEM(ops, BF16),
        ],
        compiler_params=pltpu.CompilerParams(
            dimension_semantics=("arbitrary",), vmem_limit_bytes=VMEM_LIMIT),
        name="gdn",
    )(proj, proj, ab, conv_w8, alog_p, dtb_p, normw)


def _ret_kernel(qk_ref, v_ref, g_ref, sin_ref, cos_ref, inner_ref, kdec_ref, qdec_ref, cdec_ref,
                o_ref, state_s, q_s, qd_s, k_s, kt_s, *, seq):
    nchunk = seq // CHUNK
    lane = lax.broadcasted_iota(jnp.int32, (CHUNK, RET_DK), 1)
    even = (lane % 2) == 0

    def rotate(x, sin, cos):
        nxt = pltpu.roll(x, RET_DK - 1, axis=1)
        prv = pltpu.roll(x, 1, axis=1)
        return x * cos + jnp.where(even, -nxt, prv) * sin

    state_s[...] = jnp.zeros_like(state_s)
    kdec = kdec_ref[...]
    qdec = qdec_ref[...]
    cdec = cdec_ref[...]

    heads = range(RET_HEADS)

    def operands(c, slot):
        r0 = pl.multiple_of(c * CHUNK, CHUNK)
        sin = sin_ref[pl.ds(r0, CHUNK), :]
        cos = cos_ref[pl.ds(r0, CHUNK), :]
        for h in heads:
            q = rotate(qk_ref[pl.ds(r0, CHUNK), h * RET_DK:(h + 1) * RET_DK].astype(F32), sin, cos)
            k = rotate(qk_ref[pl.ds(r0, CHUNK), RET_QK + h * RET_DK:RET_QK + (h + 1) * RET_DK]
                       .astype(F32), sin, cos) * (RET_DK ** -0.5)
            q_s[slot, h] = q.astype(BF16)
            qd_s[slot, h] = (q * qdec[:, h:h + 1]).astype(BF16)
            k_s[slot, h] = k.astype(BF16)
            kt_s[slot, h] = (k * kdec[:, h:h + 1]).T.astype(BF16)

    def outputs(c, slot):
        r0 = pl.multiple_of(c * CHUNK, CHUNK)
        vs = [v_ref[pl.ds(r0, CHUNK), h * RET_DV:(h + 1) * RET_DV] for h in heads]
        ss = [state_s[h] for h in heads]
        qks = [_dot_nt(q_s[slot, h], k_s[slot, h]) for h in heads]
        inters = [_dot(qd_s[slot, h], ss[h].astype(BF16)) for h in heads]
        kvs = [_dot(kt_s[slot, h], vs[h]) for h in heads]
        intras = [_dot((qks[h] * inner_ref[h]).astype(BF16), vs[h]) for h in heads]
        for h in heads:
            state_s[h] = ss[h] * cdec[:, h:h + 1] + kvs[h]
            o = intras[h] + inters[h]
            gate = g_ref[pl.ds(r0, CHUNK), h * RET_DV:(h + 1) * RET_DV].astype(F32)
            on = o * lax.rsqrt(jnp.mean(o * o, axis=-1, keepdims=True) + EPS)
            o_ref[pl.ds(r0, CHUNK), h * RET_DV:(h + 1) * RET_DV] = (on * _silu(gate)).astype(BF16)

    operands(0, 0)

    def body(i, carry):
        operands(2 * i + 1, 1)
        outputs(2 * i, 0)
        operands(jnp.minimum(2 * i + 2, nchunk - 1), 0)
        outputs(2 * i + 1, 1)
        return carry

    lax.fori_loop(0, nchunk // 2, body, 0)


def _ret_call(proj, sin, cos, inner, kdec, qdec, cdec, batch, seq):
    return pl.pallas_call(
        functools.partial(_ret_kernel, seq=seq),
        grid=(batch,),
        in_specs=[
            pl.BlockSpec((seq, 2 * RET_QK), lambda b: (b, 2)),
            pl.BlockSpec((seq, RET_V), lambda b: (b, 3)),
            pl.BlockSpec((seq, RET_V), lambda b: (b, 4)),
            pl.BlockSpec((seq, RET_DK), lambda b: (0, 0)),
            pl.BlockSpec((seq, RET_DK), lambda b: (0, 0)),
            pl.BlockSpec((RET_HEADS, CHUNK, CHUNK), lambda b: (0, 0, 0)),
            pl.BlockSpec((CHUNK, LANES), lambda b: (0, 0)),
            pl.BlockSpec((CHUNK, LANES), lambda b: (0, 0)),
            pl.BlockSpec((1, LANES), lambda b: (0, 0)),
        ],
        out_specs=pl.BlockSpec((seq, RET_V), lambda b: (b, 0)),
        out_shape=jax.ShapeDtypeStruct((batch * seq, RET_V), BF16),
        scratch_shapes=[pltpu.VMEM((RET_HEADS, RET_DK, RET_DV), F32)]
        + [pltpu.VMEM((2, RET_HEADS, CHUNK, RET_DK), BF16)] * 4,
        compiler_params=pltpu.CompilerParams(
            dimension_semantics=("arbitrary",), vmem_limit_bytes=VMEM_LIMIT),
        name="retention",
    )(proj, proj, proj, sin, cos, inner, kdec, qdec, cdec)


def _retention_tables(seq):
    inv_freq = 1.0 / (ROPE_BASE ** jnp.linspace(0.0, 1.0, RET_DK // 2, dtype=F32))
    ang = jnp.arange(seq, dtype=F32)[:, None] * inv_freq[None, :]
    sin = jnp.repeat(jnp.sin(ang), 2, axis=-1)
    cos = jnp.repeat(jnp.cos(ang), 2, axis=-1)
    log_gamma = jnp.log(1.0 - 2.0 ** (-5.0 - jnp.arange(RET_HEADS, dtype=F32)))
    idx = jnp.arange(CHUNK, dtype=F32)
    causal = jnp.tril(jnp.ones((CHUNK, CHUNK), dtype=bool))
    rel = jnp.where(causal, idx[:, None] - idx[None, :], 0.0)
    inner = jnp.where(causal, jnp.exp(rel[None] * log_gamma[:, None, None]), 0.0)
    k_decay = jnp.exp(log_gamma[:, None] * (CHUNK - 1.0 - idx)[None, :])
    q_decay = jnp.exp(log_gamma[:, None] * (idx + 1.0)[None, :])
    chunk_decay = jnp.exp(log_gamma * CHUNK)
    pad = LANES - RET_HEADS
    kdec = jnp.pad(k_decay.T, ((0, 0), (0, pad)))
    qdec = jnp.pad(q_decay.T, ((0, 0), (0, pad)))
    cdec = jnp.pad(chunk_decay[None, :], ((0, 0), (0, pad)))
    return sin, cos, inner, kdec, qdec, cdec


def _merge_kernel(x_ref, ya_ref, yb_ref, ma_ref, mb_ref, wa_ref, wr_ref, wo_ref, nw_ref,
                  wrt_ref, br_ref, h_ref, xn_ref, ridx_ref, rw_ref, cnt_ref):
    a = _dot(ya_ref[...], wa_ref[...])
    r = _dot(yb_ref[...], wr_ref[...])
    merged = _sigmoid(ma_ref[...].astype(F32)) * a + _sigmoid(mb_ref[...].astype(F32)) * r
    h = x_ref[...] + _dot(merged.astype(BF16), wo_ref[...])
    h_ref[...] = h
    xn = h * lax.rsqrt(jnp.mean(h * h, axis=-1, keepdims=True) + EPS) * nw_ref[...]
    xn_ref[...] = _pack_bf16_pairs(xn)

    tm = xn.shape[0]
    xh = xn.astype(BF16)
    xl = (xn - xh.astype(F32)).astype(BF16)
    parts = _dot(jnp.concatenate([xh, xl], axis=0), wrt_ref[...])
    logits = (parts[:tm, :LANES] + (parts[tm:, :LANES] + parts[:tm, LANES:]
                                    + parts[tm:, LANES:])) + br_ref[...]
    lane = lax.broadcasted_iota(jnp.int32, (xn.shape[0], LANES), 1)
    neg = -jnp.inf
    gl = jnp.where((lane >= N_EXPERTS) & (lane < N_EXPERTS + N_GROUPS), logits, neg)
    gmax = jnp.max(gl, axis=-1, keepdims=True)
    gidx = jnp.min(jnp.where(gl == gmax, lane, LANES), axis=-1, keepdims=True) - N_EXPERTS
    g_w = 1.0 / jnp.sum(jnp.exp(gl - gmax), axis=-1, keepdims=True)
    el = jnp.where((lane // EXPERTS_PER_GROUP == gidx) & (lane < N_EXPERTS), logits, neg)
    m1 = jnp.max(el, axis=-1, keepdims=True)
    i1 = jnp.min(jnp.where(el == m1, lane, LANES), axis=-1, keepdims=True)
    el2 = jnp.where(lane == i1, neg, el)
    m2 = jnp.max(el2, axis=-1, keepdims=True)
    i2 = jnp.min(jnp.where(el2 == m2, lane, LANES), axis=-1, keepdims=True)
    e2 = jnp.exp(m2 - m1)
    p1 = g_w / (1.0 + e2)
    p2 = g_w * e2 / (1.0 + e2)
    ridx_ref[...] = jnp.where(lane == 0, i1, jnp.where(lane == 1, i2, 0))
    rw_ref[...] = jnp.where(lane == 0, p1, jnp.where(lane == 1, p2, 0.0))

    @pl.when(pl.program_id(0) == 0)
    def _():
        cnt_ref[...] = jnp.zeros_like(cnt_ref)

    onehot = jnp.where((lane == i1) | (lane == i2), 1.0, 0.0)
    cnt_ref[...] += jnp.broadcast_to(jnp.sum(onehot, axis=0, keepdims=True), cnt_ref.shape)


def _merge_call(x2, ya, yb, proj, wa, wr, wo, nw, w_router, b_router, tm=512):
    m = x2.shape[0]
    full = lambda shape: pl.BlockSpec(shape, lambda i: (0, 0))
    return pl.pallas_call(
        _merge_kernel,
        grid=(m // tm,),
        in_specs=[
            pl.BlockSpec((tm, D_MODEL), lambda i: (i, 0)),
            pl.BlockSpec((tm, GDN_V), lambda i: (i, 0)),
            pl.BlockSpec((tm, RET_V), lambda i: (i, 0)),
            pl.BlockSpec((tm, D_MODEL), lambda i: (i, 5)),
            pl.BlockSpec((tm, D_MODEL), lambda i: (i, 6)),
            full((GDN_V, D_MODEL)), full((RET_V, D_MODEL)), full((D_MODEL, D_MODEL)),
            full((1, D_MODEL)),
            full((D_MODEL, 2 * LANES)), full((1, LANES)),
        ],
        out_specs=[
            pl.BlockSpec((tm, D_MODEL), lambda i: (i, 0)),
            pl.BlockSpec((tm, HALF), lambda i: (i, 0)),
            pl.BlockSpec((tm, LANES), lambda i: (i, 0)),
            pl.BlockSpec((tm, LANES), lambda i: (i, 0)),
            pl.BlockSpec((8, LANES), lambda i: (0, 0)),
        ],
        out_shape=[
            jax.ShapeDtypeStruct((m, D_MODEL), F32),
            jax.ShapeDtypeStruct((m, HALF), I32),
            jax.ShapeDtypeStruct((m, LANES), I32),
            jax.ShapeDtypeStruct((m, LANES), F32),
            jax.ShapeDtypeStruct((8, LANES), F32),
        ],
        compiler_params=pltpu.CompilerParams(
            dimension_semantics=("arbitrary",), vmem_limit_bytes=VMEM_LIMIT),
        name="merge_router",
    )(x2, ya, yb, proj, proj, wa, wr, wo, nw, w_router, b_router)


def _slot_counts(m):
    n_pad = N_EXPERTS * SLOT_TILE
    n_slots = TOP_K * m + n_pad
    return n_slots, n_slots // SLOT_TILE, n_pad


def _lane_prefix_sum(x, lane):
    s = 1
    while s < LANES:
        x = x + jnp.where(lane >= s, pltpu.roll(x, s, axis=1), 0.0)
        s *= 2
    return x


def _plan_kernel(ridx_ref, cnt_ref, slots_ref, tile_ref, pad_ref, carry_s, off_s, *, n_slots):
    i = pl.program_id(0)
    lane = lax.broadcasted_iota(I32, (PLAN_TILE, LANES), 1)
    row = lax.broadcasted_iota(I32, (PLAN_TILE, LANES), 0)
    lane1 = lane[0:1]

    @pl.when(i == 0)
    def _():
        cnt = cnt_ref[0:1, :]
        tile = float(SLOT_TILE)
        padded = jnp.floor((cnt + (tile - 1.0)) / tile) * tile
        incl = _lane_prefix_sum(padded, lane1)
        off = incl - padded
        off_s[...] = off
        carry_s[...] = jnp.zeros_like(carry_s)
        first = (row * SLOT_TILE).astype(F32)
        ended = jnp.where((lane < N_EXPERTS) & (incl <= first), 1.0, 0.0)
        tile_ref[...] = jnp.broadcast_to(
            jnp.sum(ended, axis=-1, keepdims=True), (PLAN_TILE, LANES)).astype(I32)
        rowf = row.astype(F32)
        spill = (n_slots + lane * SLOT_TILE + row).astype(F32)
        pad_ref[...] = jnp.where(rowf < padded - cnt, off + cnt + rowf, spill).astype(I32)

    strict = jnp.where(row[:, 0:1] > lax.broadcasted_iota(I32, (PLAN_TILE, PLAN_TILE), 1),
                       1.0, 0.0).astype(BF16)
    off = off_s[...]
    carry = carry_s[...]
    for sb in range(PLAN_STEP // PLAN_TILE):
        rows = pl.ds(sb * PLAN_TILE, PLAN_TILE)
        e1 = ridx_ref[rows, 0:1]
        e2 = ridx_ref[rows, 1:2]
        onehot = jnp.where((lane == e1) | (lane == e2), 1.0, 0.0)
        pos = _dot(strict, onehot.astype(BF16)) + (carry + off)
        s1 = jnp.sum(jnp.where(lane == e1, pos, 0.0), axis=-1, keepdims=True)
        s2 = jnp.sum(jnp.where(lane == e2, pos, 0.0), axis=-1, keepdims=True)
        both = jnp.where(lane == 0, s1, jnp.where(lane == 1, s2, 0.0))
        for q in range(PLAN_TILE // LANES):
            t = both[q * LANES:(q + 1) * LANES].T
            c0 = sb * PLAN_TILE + q * LANES
            slots_ref[:, c0:c0 + LANES] = t[0:8].astype(I32)
        carry = carry + jnp.sum(onehot, axis=0, keepdims=True)
    carry_s[...] = carry


def _plan_call(ridx, cnt):
    m = ridx.shape[0]
    n_slots, n_tiles, _ = _slot_counts(m)
    assert n_tiles <= PLAN_TILE and SLOT_TILE == PLAN_TILE
    return pl.pallas_call(
        functools.partial(_plan_kernel, n_slots=n_slots),
        grid=(m // PLAN_STEP,),
        in_specs=[pl.BlockSpec((PLAN_STEP, LANES), lambda i: (i, 0)),
                  pl.BlockSpec((8, LANES), lambda i: (0, 0))],
        out_specs=[
            pl.BlockSpec((8, PLAN_STEP), lambda i: (0, i)),
            pl.BlockSpec((PLAN_TILE, LANES), lambda i: (0, 0)),
            pl.BlockSpec((PLAN_TILE, LANES), lambda i: (0, 0)),
        ],
        out_shape=[
            jax.ShapeDtypeStruct((8, m), I32),
            jax.ShapeDtypeStruct((PLAN_TILE, LANES), I32),
            jax.ShapeDtypeStruct((PLAN_TILE, LANES), I32),
        ],
        scratch_shapes=[pltpu.VMEM((1, LANES), F32), pltpu.VMEM((1, LANES), F32)],
        compiler_params=pltpu.CompilerParams(dimension_semantics=("arbitrary",)),
        name="dispatch_plan",
    )(ridx, cnt)


def _sc_mesh():
    return plsc.VectorSubcoreMesh(core_axis_name="c", subcore_axis_name="s")


def _sc_worker():
    return lax.axis_index("s") * 2 + lax.axis_index("c")


def _sc_dispatch(xn, slot1, slot2, pad_slots, zeros):
    m = xn.shape[0]
    n_slots, _, n_pad = _slot_counts(m)
    per = m // SC_WORKERS
    pad_per = n_pad // SC_WORKERS

    @functools.partial(
        pl.kernel, mesh=_sc_mesh(),
        out_type=jax.ShapeDtypeStruct((n_slots + n_pad, HALF), I32),
        scratch_types=[pltpu.VMEM((SC_ROWS,), I32), pltpu.VMEM((SC_ROWS,), I32),
                       pltpu.VMEM((SC_ROWS, HALF), I32), pltpu.SemaphoreType.DMA],
        name="sc_dispatch")
    def k(x_hbm, s1_hbm, s2_hbm, p_hbm, z_hbm, o_hbm, i1_v, i2_v, rows_v, sem):
        wid = _sc_worker()

        @pl.loop(0, per // SC_ROWS)
        def _(ci):
            t0 = wid * per + ci * SC_ROWS
            pltpu.sync_copy(x_hbm.at[pl.ds(t0, SC_ROWS)], rows_v)
            pltpu.sync_copy(s1_hbm.at[pl.ds(t0, SC_ROWS)], i1_v)
            pltpu.sync_copy(s2_hbm.at[pl.ds(t0, SC_ROWS)], i2_v)
            c1 = pltpu.async_copy(rows_v, o_hbm.at[i1_v], sem)
            c2 = pltpu.async_copy(rows_v, o_hbm.at[i2_v], sem)
            c1.wait()
            c2.wait()

        pltpu.sync_copy(z_hbm, rows_v)

        @pl.loop(0, pad_per // SC_ROWS)
        def _(ci):
            t0 = wid * pad_per + ci * SC_ROWS
            pltpu.sync_copy(p_hbm.at[pl.ds(t0, SC_ROWS)], i1_v)
            pltpu.async_copy(rows_v, o_hbm.at[i1_v], sem).wait()

    return k(xn, slot1, slot2, pad_slots, zeros)


def _sc_collect(ys, slot1, slot2):
    m = slot1.shape[0]
    per = m // SC_WORKERS
    row = jax.ShapeDtypeStruct((m, HALF), I32)

    @functools.partial(
        pl.kernel, mesh=_sc_mesh(), out_type=[row, row],
        scratch_types=[pltpu.VMEM((SC_ROWS,), I32), pltpu.VMEM((SC_ROWS, HALF), I32),
                       pltpu.SemaphoreType.DMA],
        name="sc_collect")
    def k(y_hbm, s1_hbm, s2_hbm, g1_hbm, g2_hbm, i_v, rows_v, sem):
        wid = _sc_worker()

        @pl.loop(0, per // SC_ROWS)
        def _(ci):
            t0 = wid * per + ci * SC_ROWS
            for s_hbm, g_hbm in ((s1_hbm, g1_hbm), (s2_hbm, g2_hbm)):
                pltpu.sync_copy(s_hbm.at[pl.ds(t0, SC_ROWS)], i_v)
                pltpu.async_copy(y_hbm.at[i_v], rows_v, sem).wait()
                pltpu.sync_copy(rows_v, g_hbm.at[pl.ds(t0, SC_ROWS)])

    return k(ys, slot1, slot2)


def _expert_kernel(te_ref, xs_ref, wg_hbm, wu_hbm, wd_hbm, ys_ref, wg_b, wu_b, wd_b,
                   wg_f, wu_f, wd_f, sem, slot_s):
    j = pl.program_id(0)
    n = pl.num_programs(0)
    e = te_ref[j]
    prev = te_ref[jnp.maximum(j - 1, 0)]
    valid = e < N_EXPERTS

    def weight_copies(expert, slot):
        return [pltpu.make_async_copy(hbm.at[expert], buf.at[slot], sem.at[slot, i])
                for i, (hbm, buf) in enumerate(((wg_hbm, wg_f), (wu_hbm, wu_f), (wd_hbm, wd_f)))]

    @pl.when((j == 0) & valid)
    def _():
        slot_s[0] = 0
        for c in weight_copies(e, 0):
            c.start()

    @pl.when(((j == 0) | (e != prev)) & valid)
    def _():
        slot = slot_s[0]
        for c in weight_copies(e, slot):
            c.wait()
        k = lax.while_loop(lambda k: (k < n) & (te_ref[jnp.minimum(k, n - 1)] == e),
                           lambda k: k + 1, j + 1)
        nxt = te_ref[jnp.minimum(k, n - 1)]

        @pl.when((k < n) & (nxt < N_EXPERTS))
        def _():
            for c in weight_copies(nxt, 1 - slot):
                c.start()

        wg_b[...] = wg_f[slot].astype(BF16)
        wu_b[...] = wu_f[slot].astype(BF16)
        wd_b[...] = wd_f[slot].astype(BF16)
        slot_s[0] = 1 - slot

    @pl.when(valid)
    def _():
        rows = [pl.ds(i * (SLOT_TILE // 2), SLOT_TILE // 2) for i in range(2)]
        xs = [_unpack_bf16_pairs(xs_ref[r, :]).astype(BF16) for r in rows]
        gs = [_dot(x, wg_b[...]) for x in xs]
        us = [_dot(x, wu_b[...]) for x in xs]
        hids = [(_silu(g) * u).astype(BF16) for g, u in zip(gs, us)]
        ys = [_dot(hid, wd_b[...]) for hid in hids]
        for r, y in zip(rows, ys):
            ys_ref[r, :] = _pack_bf16_pairs(y)

    @pl.when(e >= N_EXPERTS)
    def _():
        ys_ref[...] = jnp.zeros_like(ys_ref)


def _expert_call(tile_expert, xs, wg, wu, wd, n_tiles):
    hbm = pl.BlockSpec(memory_space=pl.ANY)
    return pl.pallas_call(
        _expert_kernel,
        grid_spec=pltpu.PrefetchScalarGridSpec(
            num_scalar_prefetch=1,
            grid=(n_tiles,),
            in_specs=[pl.BlockSpec((SLOT_TILE, HALF), lambda j, te: (j, 0)), hbm, hbm, hbm],
            out_specs=pl.BlockSpec((SLOT_TILE, HALF), lambda j, te: (j, 0)),
            scratch_shapes=[
                pltpu.VMEM((D_MODEL, D_EXPERT), BF16), pltpu.VMEM((D_MODEL, D_EXPERT), BF16),
                pltpu.VMEM((D_EXPERT, D_MODEL), BF16),
                pltpu.VMEM((2, D_MODEL, D_EXPERT), F32), pltpu.VMEM((2, D_MODEL, D_EXPERT), F32),
                pltpu.VMEM((2, D_EXPERT, D_MODEL), F32),
                pltpu.SemaphoreType.DMA((2, 3)),
                pltpu.SMEM((1,), I32),
            ],
        ),
        out_shape=jax.ShapeDtypeStruct((n_tiles * SLOT_TILE, HALF), I32),
        compiler_params=pltpu.CompilerParams(
            dimension_semantics=("arbitrary",), vmem_limit_bytes=VMEM_LIMIT),
        name="experts",
    )(tile_expert, xs, wg, wu, wd)


def _final_kernel(h_ref, g1_ref, g2_ref, rw_ref, nw_ref, o_ref):
    rw = rw_ref[...]
    y = rw[:, 0:1] * _unpack_bf16_pairs(g1_ref[...]) + rw[:, 1:2] * _unpack_bf16_pairs(g2_ref[...])
    h = h_ref[...] + y
    o_ref[...] = h * lax.rsqrt(jnp.mean(h * h, axis=-1, keepdims=True) + EPS) * nw_ref[...]


def _final_call(h1, g1, g2, rw, nw, tm=512):
    m = h1.shape[0]
    return pl.pallas_call(
        _final_kernel,
        grid=(m // tm,),
        in_specs=[
            pl.BlockSpec((tm, D_MODEL), lambda i: (i, 0)),
            pl.BlockSpec((tm, HALF), lambda i: (i, 0)),
            pl.BlockSpec((tm, HALF), lambda i: (i, 0)),
            pl.BlockSpec((tm, LANES), lambda i: (i, 0)),
            pl.BlockSpec((1, D_MODEL), lambda i: (0, 0)),
        ],
        out_specs=pl.BlockSpec((tm, D_MODEL), lambda i: (i, 0)),
        out_shape=jax.ShapeDtypeStruct((m, D_MODEL), F32),
        compiler_params=pltpu.CompilerParams(dimension_semantics=("arbitrary",)),
        name="combine_final",
    )(h1, g1, g2, rw, nw)


def _pad_lanes(a):
    return jnp.pad(a, ((0, 0), (0, LANES - a.shape[1])))


def kernel(x, norm_mix_w, w_in, conv_w, A_log, dt_bias, gdn_norm_w, w_up_gdn, w_up_ret, w_out,
           norm_ffn_w, w_group, b_group, w_expert, b_expert, w_gate, w_up, w_down, norm_final_w):
    batch, seq, d = x.shape
    m = batch * seq
    h = x.reshape(m, d)
    depth = w_in.shape[0]
    sin, cos, inner, kdec, qdec, cdec = _retention_tables(seq)
    for l in range(depth):
        o_ab = 3 * GDN_QK
        o_z = o_ab + 2 * GDN_HEADS
        wl = w_in[l]
        w_main = jnp.concatenate([wl[:, :o_ab], wl[:, o_z:]], axis=1).astype(BF16)
        w_ab = _pad_lanes(wl[:, o_ab:o_z]).astype(BF16)
        proj, ab = _proj_call(h, norm_mix_w[l][None, :], w_main, w_ab)

        conv8 = jnp.pad(conv_w[l], ((0, 8 - GDN_CONV), (0, 0)))
        ya = _gdn_call(proj, ab, conv8, _pad_lanes(A_log[l][None, :]), _pad_lanes(dt_bias[l][None, :]),
                       gdn_norm_w[l][None, :], batch, seq)
        yb = _ret_call(proj, sin, cos, inner, kdec, qdec, cdec, batch, seq)

        w_router = _pad_lanes(jnp.concatenate([w_expert[l], w_group[l]], axis=1))
        wr_hi = w_router.astype(BF16)
        wr_lo = (w_router - wr_hi.astype(F32)).astype(BF16)
        b_router = _pad_lanes(jnp.concatenate([b_expert[l], b_group[l]])[None, :])
        h1, xn, ridx, rw, cnt = _merge_call(
            h, ya, yb, proj, w_up_gdn[l].astype(BF16), w_up_ret[l].astype(BF16), w_out[l].astype(BF16),
            norm_ffn_w[l][None, :], jnp.concatenate([wr_hi, wr_lo], axis=1), b_router)

        _, n_tiles, _ = _slot_counts(m)
        slots, tiles, pads = _plan_call(ridx, cnt)
        slot1, slot2 = slots[0], slots[1]
        pad_slots = pads[:, :N_EXPERTS].T.reshape(-1)
        xs = _sc_dispatch(xn, slot1, slot2, pad_slots, jnp.zeros((SC_ROWS, HALF), I32))
        ys = _expert_call(tiles[:n_tiles, 0], xs, w_gate[l], w_up[l], w_down[l], n_tiles)
        g1, g2 = _sc_collect(ys, slot1, slot2)

        assert depth == 1
        h = _final_call(h1, g1, g2, rw, norm_final_w[None, :])
    return h.reshape(batch, seq, d)
```

```python
import functools
import math

import jax
import jax.numpy as jnp
from jax import lax
from jax.experimental import pallas as pl
from jax.experimental.pallas import tpu as pltpu
from jax.experimental.pallas import tpu_sc as plsc

F32 = jnp.float32
BF16 = jnp.bfloat16
I32 = jnp.int32
U32 = jnp.uint32

D_MODEL = 1024
EPS = 1e-6
GDN_HEADS = 4
GDN_DK = 128
GDN_DV = 128
GDN_CONV = 4
RET_HEADS = 4
RET_DK = 128
RET_DV = 256
ROPE_BASE = 10000.0
N_GROUPS = 4
EXPERTS_PER_GROUP = 8
N_EXPERTS = N_GROUPS * EXPERTS_PER_GROUP
D_EXPERT = 512

GDN_QK = GDN_HEADS * GDN_DK
GDN_V = GDN_HEADS * GDN_DV
RET_QK = RET_HEADS * RET_DK
RET_V = RET_HEADS * RET_DV

LANES = 128
CHUNK = 128
INV_BLOCK = 16
GDN_PREP_CHUNKS = 2
VMEM_LIMIT = 56 * 1024 * 1024

MERGE_PARTS = 1
TOP_K = 2
SLOT_TILE = 256
PLAN_TILE = 256
PLAN_STEP = 1024
HALF = D_MODEL // 2
SC_WORKERS = 32
SC_ROWS = 64

PROJ_COLS = 3 * GDN_QK + GDN_V + 2 * RET_QK + 2 * RET_V + 2 * D_MODEL


def _silu(x):
    return x / (1.0 + jnp.exp(-x))


def _sigmoid(x):
    return 1.0 / (1.0 + jnp.exp(-x))


def _dot(a, b):
    return jnp.dot(a, b, preferred_element_type=F32)


def _dot_nt(a, b):
    return lax.dot_general(a, b, (((1,), (1,)), ((), ())), preferred_element_type=F32)


def _pack_bf16_pairs(x):
    bits = lax.bitcast_convert_type(x.astype(BF16).astype(F32), U32)
    packed = (bits[:, :HALF] >> 16) | (bits[:, HALF:] & jnp.uint32(0xFFFF0000))
    return lax.bitcast_convert_type(packed, I32)


def _unpack_bf16_pairs(p):
    p = lax.bitcast_convert_type(p, U32)
    lo = lax.bitcast_convert_type(p << 16, F32)
    hi = lax.bitcast_convert_type(p & jnp.uint32(0xFFFF0000), F32)
    return jnp.concatenate([lo, hi], axis=1)


def _proj_kernel(x_ref, nw_ref, w_ref, wab_ref, proj_ref, ab_ref, u_ref):
    j = pl.program_id(1)

    @pl.when(j == 0)
    def _():
        x = x_ref[...]
        u = x * lax.rsqrt(jnp.mean(x * x, axis=-1, keepdims=True) + EPS) * nw_ref[...]
        ub = u.astype(BF16)
        u_ref[...] = ub
        ab_ref[...] = _dot(ub, wab_ref[...])

    proj_ref[...] = _dot(u_ref[...], w_ref[...]).astype(BF16)


def _proj_call(x2, norm_w, w_main, w_ab, tm=1024, tn=1024):
    m = x2.shape[0]
    return pl.pallas_call(
        _proj_kernel,
        grid=(m // tm, PROJ_COLS // tn),
        in_specs=[
            pl.BlockSpec((tm, D_MODEL), lambda i, j: (i, 0)),
            pl.BlockSpec((1, D_MODEL), lambda i, j: (0, 0)),
            pl.BlockSpec((D_MODEL, tn), lambda i, j: (0, j)),
            pl.BlockSpec((D_MODEL, LANES), lambda i, j: (0, 0)),
        ],
        out_specs=[
            pl.BlockSpec((tm, tn), lambda i, j: (i, j)),
            pl.BlockSpec((tm, LANES), lambda i, j: (i, 0)),
        ],
        out_shape=[
            jax.ShapeDtypeStruct((m, PROJ_COLS), BF16),
            jax.ShapeDtypeStruct((m, LANES), F32),
        ],
        scratch_shapes=[pltpu.VMEM((tm, D_MODEL), BF16)],
        compiler_params=pltpu.CompilerParams(
            dimension_semantics=("arbitrary", "arbitrary"), vmem_limit_bytes=VMEM_LIMIT),
        name="proj",
    )(x2, norm_w, w_main, w_ab)


def _unit_lower_inverses(lows, ii, jj):
    eye = jnp.where(ii == jj, 1.0, 0.0).astype(F32)
    in_block = (ii // INV_BLOCK) == (jj // INV_BLOCK)
    ps = [jnp.where(in_block, -low, 0.0) for low in lows]
    ts = [eye + p for p in ps]
    span = 2
    while span < INV_BLOCK:
        ps = [_dot(p, p) for p in ps]
        ts = [t + _dot(t, p) for t, p in zip(ts, ps)]
        span *= 2
    s = INV_BLOCK
    while s < CHUNK:
        off_diag = ((ii // (2 * s)) == (jj // (2 * s))) & ((ii // s) != (jj // s))
        xs = [_dot(jnp.where(off_diag, low, 0.0), t) for low, t in zip(lows, ts)]
        ts = [t - _dot(t, x) for t, x in zip(ts, xs)]
        s *= 2
    return ts


def _gdn_kernel(qkv_ref, z_ref, ab_ref, convw_ref, alog_ref, dtb_ref, normw_ref, o_ref,
                b_s, o0_s, m_s, qp_s, gl_s, state_s, q_s, k_s, kb_s, rhs_s, dec_s, cv_s, qg_o, kdt_o,
                *, seq):
    nchunk = seq // CHUNK
    ii = lax.broadcasted_iota(jnp.int32, (CHUNK, CHUNK), 0)
    jj = lax.broadcasted_iota(jnp.int32, (CHUNK, CHUNK), 1)
    causal = ii >= jj
    strict = ii > jj
    tri = jnp.where(causal, 1.0, 0.0).astype(F32)
    neg_a = -jnp.exp(alog_ref[...])
    dtb = dtb_ref[...]

    def conv_cols(c, r0, lo, buf):
        x = qkv_ref[pl.ds(r0, CHUNK), lo:lo + LANES].astype(F32)
        prev0 = pl.multiple_of(jnp.maximum(r0 - 16, 0), 16)
        prev = qkv_ref[pl.ds(prev0, 16), lo:lo + LANES].astype(F32)
        buf[0:8, :] = prev[8:16] * jnp.where(c > 0, 1.0, 0.0)
        buf[8:8 + CHUNK, :] = x
        w = convw_ref[:, lo:lo + LANES]
        y = (w[3:4] * x + w[2:3] * buf[7:7 + CHUNK, :] + w[1:2] * buf[6:6 + CHUNK, :]
             + w[0:1] * buf[5:5 + CHUNK, :])
        return _silu(y)

    def l2n(x):
        return x * lax.rsqrt(jnp.sum(x * x, axis=-1, keepdims=True) + EPS)

    tri_b = tri.astype(BF16)

    def chunk_cumsum(g):
        g1 = g.astype(BF16)
        r1 = g - g1.astype(F32)
        g2 = r1.astype(BF16)
        g3 = (r1 - g2.astype(F32)).astype(BF16)
        return _dot(tri_b, g1) + (_dot(tri_b, g2) + _dot(tri_b, g3))

    def operands(cc, slot):
        for sub in range(GDN_PREP_CHUNKS):
            c = cc * GDN_PREP_CHUNKS + sub
            r0 = pl.multiple_of(c * CHUNK, CHUNK)
            ab = ab_ref[pl.ds(r0, CHUNK), :]
            xg = ab + dtb
            softplus = jnp.maximum(xg, 0.0) + jnp.log(1.0 + jnp.exp(-jnp.abs(xg)))
            g_all = neg_a * softplus
            beta_all = _sigmoid(ab)
            gc_all = chunk_cumsum(g_all)
            gc_t = gc_all.T
            gl_s[c] = jnp.exp(gc_all[CHUNK - 1:CHUNK, :])
            for h in range(GDN_HEADS):
                n = sub * GDN_HEADS + h
                bufs = [cv_s.at[slot, 3 * n + i] for i in range(3)]
                q = l2n(conv_cols(c, r0, h * GDN_DK, bufs[0])) * (GDN_DK ** -0.5)
                k = l2n(conv_cols(c, r0, GDN_QK + h * GDN_DK, bufs[1]))
                v = conv_cols(c, r0, 2 * GDN_QK + h * GDN_DV, bufs[2])
                gcol = gc_all[:, h:h + 1]
                grow = gc_t[h:h + 1, :]
                beta = beta_all[:, GDN_HEADS + h:GDN_HEADS + h + 1]
                dec_s[slot, n] = jnp.where(causal, jnp.exp(gcol - grow), 0.0)
                eg = jnp.exp(gcol)
                kb = k * beta
                q_s[slot, n] = q.astype(BF16)
                k_s[slot, n] = k.astype(BF16)
                kb_s[slot, n] = kb.astype(BF16)
                rhs_s[slot, n] = jnp.concatenate([v * beta, kb * eg], axis=1).astype(BF16)
                qg_o[slot, n] = q * eg
                kd = k * jnp.exp(gc_all[CHUNK - 1:CHUNK, h:h + 1] - gcol)
                kdt_o[slot, n] = kd.T.astype(BF16)

    def solve(cc, slot):
        items = [(cc * GDN_PREP_CHUNKS + sub, h, sub * GDN_HEADS + h)
                 for sub in range(GDN_PREP_CHUNKS) for h in range(GDN_HEADS)]
        kks = [_dot_nt(kb_s[slot, n], k_s[slot, n]) for _, _, n in items]
        qks = [_dot_nt(q_s[slot, n], k_s[slot, n]) for _, _, n in items]
        lows = [jnp.where(strict, kk * dec_s[slot, n], 0.0) for kk, (_, _, n) in zip(kks, items)]
        attns = [(qk * dec_s[slot, n]).astype(BF16) for qk, (_, _, n) in zip(qks, items)]
        ts = _unit_lower_inverses(lows, ii, jj)
        uws = [_dot(t.astype(BF16), rhs_s[slot, n]).astype(BF16) for t, (_, _, n) in zip(ts, items)]
        kds = [_dot(kdt_o[slot, n], uw) for uw, (_, _, n) in zip(uws, items)]
        ats = [_dot(attn, uw) for attn, uw in zip(attns, uws)]
        for kd_uw, at_uw, (c, h, n) in zip(kds, ats, items):
            b_s[c, h] = kd_uw[:, :GDN_DV]
            m_s[c, h] = (-kd_uw[:, GDN_DV:]).astype(BF16)
            o0_s[c, h] = at_uw[:, :GDN_DV]
            qp_s[c, h] = (qg_o[slot, n] - at_uw[:, GDN_DV:]).astype(BF16)

    ngroup = nchunk // GDN_PREP_CHUNKS
    operands(0, 0)

    def prep(i, carry):
        operands(2 * i + 1, 1)
        solve(2 * i, 0)
        operands(jnp.minimum(2 * i + 2, ngroup - 1), 0)
        solve(2 * i + 1, 1)
        return carry

    lax.fori_loop(0, ngroup // 2, prep, 0)

    state_s[...] = jnp.zeros_like(state_s)
    normw = normw_ref[...]

    def scan(c, carry):
        r0 = pl.multiple_of(c * CHUNK, CHUNK)
        gl = gl_s[c]
        heads = range(GDN_HEADS)
        ss = [state_s[h] for h in heads]
        sbs = [s.astype(BF16) for s in ss]
        mss = [_dot(m_s[c, h], sbs[h]) for h in heads]
        qss = [_dot(qp_s[c, h], sbs[h]) for h in heads]
        for h in heads:
            state_s[h] = ss[h] * gl[:, h:h + 1] + (mss[h] + b_s[c, h])
            o = qss[h] + o0_s[c, h]
            z = z_ref[pl.ds(r0, CHUNK), h * GDN_DV:(h + 1) * GDN_DV].astype(F32)
            on = o * lax.rsqrt(jnp.mean(o * o, axis=-1, keepdims=True) + EPS) * normw
            o_ref[pl.ds(r0, CHUNK), h * GDN_DV:(h + 1) * GDN_DV] = (on * _silu(z)).astype(BF16)
        return carry

    lax.fori_loop(0, nchunk, scan, 0)


def _gdn_call(proj, ab, conv_w8, alog_p, dtb_p, normw, batch, seq):
    nchunk = seq // CHUNK
    hs = (nchunk, GDN_HEADS, CHUNK, CHUNK)
    ops = (2, GDN_PREP_CHUNKS * GDN_HEADS, CHUNK, CHUNK)
    return pl.pallas_call(
        functools.partial(_gdn_kernel, seq=seq),
        grid=(batch,),
        in_specs=[
            pl.BlockSpec((seq, 3 * GDN_QK), lambda b: (b, 0)),
            pl.BlockSpec((seq, GDN_V), lambda b: (b, 3)),
            pl.BlockSpec((seq, LANES), lambda b: (b, 0)),
            pl.BlockSpec((8, 3 * GDN_QK), lambda b: (0, 0)),
            pl.BlockSpec((1, LANES), lambda b: (0, 0)),
            pl.BlockSpec((1, LANES), lambda b: (0, 0)),
            pl.BlockSpec((1, GDN_DV), lambda b: (0, 0)),
        ],
        out_specs=pl.BlockSpec((seq, GDN_V), lambda b: (b, 0)),
        out_shape=jax.ShapeDtypeStruct((batch * seq, GDN_V), BF16),
        scratch_shapes=[
            pltpu.VMEM(hs, F32), pltpu.VMEM(hs, F32), pltpu.VMEM(hs, BF16), pltpu.VMEM(hs, BF16),
            pltpu.VMEM((nchunk, 1, LANES), F32),
            pltpu.VMEM((GDN_HEADS, GDN_DK, GDN_DV), F32),
            pltpu.VMEM(ops, BF16), pltpu.VMEM(ops, BF16), pltpu.VMEM(ops, BF16),
            pltpu.VMEM(ops[:3] + (2 * CHUNK,), BF16),
            pltpu.VMEM(ops, F32),
            pltpu.VMEM((2, 3 * ops[1], 8 + CHUNK, LANES), F32),
            pltpu.VMEM(ops, F32), pltpu.VMEM(ops, BF16),
        ],
        compiler_params=pltpu.CompilerParams(
            dimension_semantics=("arbitrary",), vmem_limit_bytes=VMEM_LIMIT),
        name="gdn",
    )(proj, proj, ab, conv_w8, alog_p, dtb_p, normw)


def _ret_kernel(qk_ref, v_ref, g_ref, sin_ref, cos_ref, inner_ref, kdec_ref, qdec_ref, cdec_ref,
                o_ref, state_s, q_s, qd_s, k_s, kt_s, *, seq):
    nchunk = seq // CHUNK
    lane = lax.broadcasted_iota(jnp.int32, (CHUNK, RET_DK), 1)
    even = (lane % 2) == 0

    def rotate(x, sin, cos):
        nxt = pltpu.roll(x, RET_DK - 1, axis=1)
        prv = pltpu.roll(x, 1, axis=1)
        return x * cos + jnp.where(even, -nxt, prv) * sin

    state_s[...] = jnp.zeros_like(state_s)
    kdec = kdec_ref[...]
    qdec = qdec_ref[...]
    cdec = cdec_ref[...]

    heads = range(RET_HEADS)

    def operands(c, slot):
        r0 = pl.multiple_of(c * CHUNK, CHUNK)
        sin = sin_ref[pl.ds(r0, CHUNK), :]
        cos = cos_ref[pl.ds(r0, CHUNK), :]
        for h in heads:
            q = rotate(qk_ref[pl.ds(r0, CHUNK), h * RET_DK:(h + 1) * RET_DK].astype(F32), sin, cos)
            k = rotate(qk_ref[pl.ds(r0, CHUNK), RET_QK + h * RET_DK:RET_QK + (h + 1) * RET_DK]
                       .astype(F32), sin, cos) * (RET_DK ** -0.5)
            q_s[slot, h] = q.astype(BF16)
            qd_s[slot, h] = (q * qdec[:, h:h + 1]).astype(BF16)
            k_s[slot, h] = k.astype(BF16)
            kt_s[slot, h] = (k * kdec[:, h:h + 1]).T.astype(BF16)

    def outputs(c, slot):
        r0 = pl.multiple_of(c * CHUNK, CHUNK)
        vs = [v_ref[pl.ds(r0, CHUNK), h * RET_DV:(h + 1) * RET_DV] for h in heads]
        ss = [state_s[h] for h in heads]
        qks = [_dot_nt(q_s[slot, h], k_s[slot, h]) for h in heads]
        inters = [_dot(qd_s[slot, h], ss[h].astype(BF16)) for h in heads]
        kvs = [_dot(kt_s[slot, h], vs[h]) for h in heads]
        intras = [_dot((qks[h] * inner_ref[h]).astype(BF16), vs[h]) for h in heads]
        for h in heads:
            state_s[h] = ss[h] * cdec[:, h:h + 1] + kvs[h]
            o = intras[h] + inters[h]
            gate = g_ref[pl.ds(r0, CHUNK), h * RET_DV:(h + 1) * RET_DV].astype(F32)
            on = o * lax.rsqrt(jnp.mean(o * o, axis=-1, keepdims=True) + EPS)
            o_ref[pl.ds(r0, CHUNK), h * RET_DV:(h + 1) * RET_DV] = (on * _silu(gate)).astype(BF16)

    operands(0, 0)

    def body(i, carry):
        operands(2 * i + 1, 1)
        outputs(2 * i, 0)
        operands(jnp.minimum(2 * i + 2, nchunk - 1), 0)
        outputs(2 * i + 1, 1)
        return carry

    lax.fori_loop(0, nchunk // 2, body, 0)


def _ret_call(proj, sin, cos, inner, kdec, qdec, cdec, batch, seq):
    return pl.pallas_call(
        functools.partial(_ret_kernel, seq=seq),
        grid=(batch,),
        in_specs=[
            pl.BlockSpec((seq, 2 * RET_QK), lambda b: (b, 2)),
            pl.BlockSpec((seq, RET_V), lambda b: (b, 3)),
            pl.BlockSpec((seq, RET_V), lambda b: (b, 4)),
            pl.BlockSpec((seq, RET_DK), lambda b: (0, 0)),
            pl.BlockSpec((seq, RET_DK), lambda b: (0, 0)),
            pl.BlockSpec((RET_HEADS, CHUNK, CHUNK), lambda b: (0, 0, 0)),
            pl.BlockSpec((CHUNK, LANES), lambda b: (0, 0)),
            pl.BlockSpec((CHUNK, LANES), lambda b: (0, 0)),
            pl.BlockSpec((1, LANES), lambda b: (0, 0)),
        ],
        out_specs=pl.BlockSpec((seq, RET_V), lambda b: (b, 0)),
        out_shape=jax.ShapeDtypeStruct((batch * seq, RET_V), BF16),
        scratch_shapes=[pltpu.VMEM((RET_HEADS, RET_DK, RET_DV), F32)]
        + [pltpu.VMEM((2, RET_HEADS, CHUNK, RET_DK), BF16)] * 4,
        compiler_params=pltpu.CompilerParams(
            dimension_semantics=("arbitrary",), vmem_limit_bytes=VMEM_LIMIT),
        name="retention",
    )(proj, proj, proj, sin, cos, inner, kdec, qdec, cdec)


def _retention_tables(seq):
    inv_freq = 1.0 / (ROPE_BASE ** jnp.linspace(0.0, 1.0, RET_DK // 2, dtype=F32))
    ang = jnp.arange(seq, dtype=F32)[:, None] * inv_freq[None, :]
    sin = jnp.repeat(jnp.sin(ang), 2, axis=-1)
    cos = jnp.repeat(jnp.cos(ang), 2, axis=-1)
    log_gamma = jnp.log(1.0 - 2.0 ** (-5.0 - jnp.arange(RET_HEADS, dtype=F32)))
    idx = jnp.arange(CHUNK, dtype=F32)
    causal = jnp.tril(jnp.ones((CHUNK, CHUNK), dtype=bool))
    rel = jnp.where(causal, idx[:, None] - idx[None, :], 0.0)
    inner = jnp.where(causal, jnp.exp(rel[None] * log_gamma[:, None, None]), 0.0)
    k_decay = jnp.exp(log_gamma[:, None] * (CHUNK - 1.0 - idx)[None, :])
    q_decay = jnp.exp(log_gamma[:, None] * (idx + 1.0)[None, :])
    chunk_decay = jnp.exp(log_gamma * CHUNK)
    pad = LANES - RET_HEADS
    kdec = jnp.pad(k_decay.T, ((0, 0), (0, pad)))
    qdec = jnp.pad(q_decay.T, ((0, 0), (0, pad)))
    cdec = jnp.pad(chunk_decay[None, :], ((0, 0), (0, pad)))
    return sin, cos, inner, kdec, qdec, cdec


def _merge_kernel(x_ref, ya_ref, yb_ref, ma_ref, mb_ref, wa_ref, wr_ref, wo_ref, nw_ref,
                  wrt_ref, br_ref, h_ref, xn_ref, ridx_ref, rw_ref, cnt_ref):
    tm = x_ref.shape[0] // MERGE_PARTS
    rows = [pl.ds(i * tm, tm) for i in range(MERGE_PARTS)]
    a_ = [_dot(ya_ref[r, :], wa_ref[...]) for r in rows]
    r_ = [_dot(yb_ref[r, :], wr_ref[...]) for r in rows]
    merged = [(_sigmoid(ma_ref[r, :].astype(F32)) * a + _sigmoid(mb_ref[r, :].astype(F32)) * rr)
              .astype(BF16) for r, a, rr in zip(rows, a_, r_)]
    hs = [x_ref[r, :] + _dot(mg, wo_ref[...]) for r, mg in zip(rows, merged)]
    xcats = []
    for r, h in zip(rows, hs):
        h_ref[r, :] = h
        xn = h * lax.rsqrt(jnp.mean(h * h, axis=-1, keepdims=True) + EPS) * nw_ref[...]
        xn_ref[r, :] = _pack_bf16_pairs(xn)
        xh = xn.astype(BF16)
        xl = (xn - xh.astype(F32)).astype(BF16)
        xcats.append(jnp.concatenate([xh, xl], axis=0))
    parts_ = [_dot(xc, wrt_ref[...]) for xc in xcats]
    counts = jnp.zeros((1, LANES), F32)
    for r, parts in zip(rows, parts_):
        counts = counts + _route(parts, tm, br_ref[...], ridx_ref.at[r, :], rw_ref.at[r, :])

    @pl.when(pl.program_id(0) == 0)
    def _():
        cnt_ref[...] = jnp.zeros_like(cnt_ref)

    cnt_ref[...] += jnp.broadcast_to(counts, cnt_ref.shape)


def _route(parts, tm, bias, ridx_ref, rw_ref):
    logits = (parts[:tm, :LANES] + (parts[tm:, :LANES] + parts[:tm, LANES:]
                                    + parts[tm:, LANES:])) + bias
    lane = lax.broadcasted_iota(jnp.int32, (tm, LANES), 1)
    neg = -jnp.inf
    gl = jnp.where((lane >= N_EXPERTS) & (lane < N_EXPERTS + N_GROUPS), logits, neg)
    gmax = jnp.max(gl, axis=-1, keepdims=True)
    gidx = jnp.min(jnp.where(gl == gmax, lane, LANES), axis=-1, keepdims=True) - N_EXPERTS
    g_w = 1.0 / jnp.sum(jnp.exp(gl - gmax), axis=-1, keepdims=True)
    el = jnp.where((lane // EXPERTS_PER_GROUP == gidx) & (lane < N_EXPERTS), logits, neg)
    m1 = jnp.max(el, axis=-1, keepdims=True)
    i1 = jnp.min(jnp.where(el == m1, lane, LANES), axis=-1, keepdims=True)
    el2 = jnp.where(lane == i1, neg, el)
    m2 = jnp.max(el2, axis=-1, keepdims=True)
    i2 = jnp.min(jnp.where(el2 == m2, lane, LANES), axis=-1, keepdims=True)
    e2 = jnp.exp(m2 - m1)
    p1 = g_w / (1.0 + e2)
    p2 = g_w * e2 / (1.0 + e2)
    ridx_ref[...] = jnp.where(lane == 0, i1, jnp.where(lane == 1, i2, 0))
    rw_ref[...] = jnp.where(lane == 0, p1, jnp.where(lane == 1, p2, 0.0))
    onehot = jnp.where((lane == i1) | (lane == i2), 1.0, 0.0)
    return jnp.sum(onehot, axis=0, keepdims=True)


def _merge_call(x2, ya, yb, proj, wa, wr, wo, nw, w_router, b_router, tm=1024):
    m = x2.shape[0]
    full = lambda shape: pl.BlockSpec(shape, lambda i: (0, 0))
    return pl.pallas_call(
        _merge_kernel,
        grid=(m // tm,),
        in_specs=[
            pl.BlockSpec((tm, D_MODEL), lambda i: (i, 0)),
            pl.BlockSpec((tm, GDN_V), lambda i: (i, 0)),
            pl.BlockSpec((tm, RET_V), lambda i: (i, 0)),
            pl.BlockSpec((tm, D_MODEL), lambda i: (i, 5)),
            pl.BlockSpec((tm, D_MODEL), lambda i: (i, 6)),
            full((GDN_V, D_MODEL)), full((RET_V, D_MODEL)), full((D_MODEL, D_MODEL)),
            full((1, D_MODEL)),
            full((D_MODEL, 2 * LANES)), full((1, LANES)),
        ],
        out_specs=[
            pl.BlockSpec((tm, D_MODEL), lambda i: (i, 0)),
            pl.BlockSpec((tm, HALF), lambda i: (i, 0)),
            pl.BlockSpec((tm, LANES), lambda i: (i, 0)),
            pl.BlockSpec((tm, LANES), lambda i: (i, 0)),
            pl.BlockSpec((8, LANES), lambda i: (0, 0)),
        ],
        out_shape=[
            jax.ShapeDtypeStruct((m, D_MODEL), F32),
            jax.ShapeDtypeStruct((m, HALF), I32),
            jax.ShapeDtypeStruct((m, LANES), I32),
            jax.ShapeDtypeStruct((m, LANES), F32),
            jax.ShapeDtypeStruct((8, LANES), F32),
        ],
        compiler_params=pltpu.CompilerParams(
            dimension_semantics=("arbitrary",), vmem_limit_bytes=VMEM_LIMIT),
        name="merge_router",
    )(x2, ya, yb, proj, proj, wa, wr, wo, nw, w_router, b_router)


def _slot_counts(m):
    n_slots = TOP_K * m + N_EXPERTS * SLOT_TILE
    return n_slots, n_slots // SLOT_TILE


def _lane_prefix_sum(x, lane):
    s = 1
    while s < LANES:
        x = x + jnp.where(lane >= s, pltpu.roll(x, s, axis=1), 0.0)
        s *= 2
    return x


def _plan_kernel(ridx_ref, cnt_ref, slots_ref, tile_ref, carry_s, off_s):
    i = pl.program_id(0)
    lane = lax.broadcasted_iota(I32, (PLAN_TILE, LANES), 1)
    row = lax.broadcasted_iota(I32, (PLAN_TILE, LANES), 0)
    lane1 = lane[0:1]

    @pl.when(i == 0)
    def _():
        cnt = cnt_ref[0:1, :]
        tile = float(SLOT_TILE)
        padded = jnp.floor((cnt + (tile - 1.0)) / tile) * tile
        incl = _lane_prefix_sum(padded, lane1)
        off = incl - padded
        off_s[...] = off
        carry_s[...] = jnp.zeros_like(carry_s)
        first = (row * SLOT_TILE).astype(F32)
        ended = jnp.where((lane < N_EXPERTS) & (incl <= first), 1.0, 0.0)
        tile_e = jnp.sum(ended, axis=-1, keepdims=True)
        last = jnp.sum(jnp.where(lane.astype(F32) == tile_e, off + cnt, 0.0), axis=-1, keepdims=True)
        used = jnp.clip(last - first[:, 0:1], 0.0, tile)
        tile_ref[...] = jnp.where(lane == 0, tile_e, jnp.where(lane == 1, used, 0.0)).astype(I32)

    strict = jnp.where(row[:, 0:1] > lax.broadcasted_iota(I32, (PLAN_TILE, PLAN_TILE), 1),
                       1.0, 0.0).astype(BF16)
    off = off_s[...]
    carry = carry_s[...]
    for sb in range(PLAN_STEP // PLAN_TILE):
        rows = pl.ds(sb * PLAN_TILE, PLAN_TILE)
        e1 = ridx_ref[rows, 0:1]
        e2 = ridx_ref[rows, 1:2]
        onehot = jnp.where((lane == e1) | (lane == e2), 1.0, 0.0)
        pos = _dot(strict, onehot.astype(BF16)) + (carry + off)
        s1 = jnp.sum(jnp.where(lane == e1, pos, 0.0), axis=-1, keepdims=True)
        s2 = jnp.sum(jnp.where(lane == e2, pos, 0.0), axis=-1, keepdims=True)
        both = jnp.where(lane == 0, s1, jnp.where(lane == 1, s2, 0.0))
        for q in range(PLAN_TILE // LANES):
            t = both[q * LANES:(q + 1) * LANES].T
            c0 = sb * PLAN_TILE + q * LANES
            slots_ref[:, c0:c0 + LANES] = t[0:8].astype(I32)
        carry = carry + jnp.sum(onehot, axis=0, keepdims=True)
    carry_s[...] = carry


def _plan_call(ridx, cnt):
    m = ridx.shape[0]
    _, n_tiles = _slot_counts(m)
    assert n_tiles <= PLAN_TILE and SLOT_TILE == PLAN_TILE
    return pl.pallas_call(
        _plan_kernel,
        grid=(m // PLAN_STEP,),
        in_specs=[pl.BlockSpec((PLAN_STEP, LANES), lambda i: (i, 0)),
                  pl.BlockSpec((8, LANES), lambda i: (0, 0))],
        out_specs=[
            pl.BlockSpec((8, PLAN_STEP), lambda i: (0, i)),
            pl.BlockSpec((PLAN_TILE, LANES), lambda i: (0, 0)),
        ],
        out_shape=[
            jax.ShapeDtypeStruct((8, m), I32),
            jax.ShapeDtypeStruct((PLAN_TILE, LANES), I32),
        ],
        scratch_shapes=[pltpu.VMEM((1, LANES), F32), pltpu.VMEM((1, LANES), F32)],
        compiler_params=pltpu.CompilerParams(dimension_semantics=("arbitrary",)),
        name="dispatch_plan",
    )(ridx, cnt)


def _sc_mesh():
    return plsc.VectorSubcoreMesh(core_axis_name="c", subcore_axis_name="s")


def _sc_worker():
    return lax.axis_index("s") * 2 + lax.axis_index("c")


def _sc_dispatch(xn, slot1, slot2, n_rows):
    m = xn.shape[0]
    per = m // SC_WORKERS
    n_pairs = per // (2 * SC_ROWS)

    @functools.partial(
        pl.kernel, mesh=_sc_mesh(),
        out_type=jax.ShapeDtypeStruct((n_rows, HALF), I32),
        scratch_types=[pltpu.VMEM((SC_ROWS,), I32), pltpu.VMEM((SC_ROWS,), I32),
                       pltpu.VMEM((SC_ROWS, HALF), I32), pltpu.VMEM((SC_ROWS, HALF), I32),
                       pltpu.SemaphoreType.DMA, pltpu.SemaphoreType.DMA, pltpu.SemaphoreType.DMA],
        name="sc_dispatch")
    def k(x_hbm, s1_hbm, s2_hbm, o_hbm, i1_v, i2_v, rows0, rows1, sem_r0, sem_r1, sem_w):
        base = _sc_worker() * per

        def read(chunk, rows_v, sem):
            return pltpu.make_async_copy(x_hbm.at[pl.ds(base + chunk * SC_ROWS, SC_ROWS)], rows_v, sem)

        def scatter(chunk, rows_v):
            t0 = base + chunk * SC_ROWS
            pltpu.sync_copy(s1_hbm.at[pl.ds(t0, SC_ROWS)], i1_v)
            pltpu.sync_copy(s2_hbm.at[pl.ds(t0, SC_ROWS)], i2_v)
            c1 = pltpu.async_copy(rows_v, o_hbm.at[i1_v], sem_w)
            c2 = pltpu.async_copy(rows_v, o_hbm.at[i2_v], sem_w)
            c1.wait()
            c2.wait()

        read(0, rows0, sem_r0).start()

        @pl.loop(0, n_pairs)
        def _(i):
            read(2 * i, rows0, sem_r0).wait()
            read(2 * i + 1, rows1, sem_r1).start()
            scatter(2 * i, rows0)
            read(2 * i + 1, rows1, sem_r1).wait()

            @pl.when(i + 1 < n_pairs)
            def _():
                read(2 * i + 2, rows0, sem_r0).start()

            scatter(2 * i + 1, rows1)

    return k(xn, slot1, slot2)


def _sc_collect(ys, slot1, slot2):
    m = slot1.shape[0]
    per = m // SC_WORKERS
    row = jax.ShapeDtypeStruct((m, HALF), I32)

    @functools.partial(
        pl.kernel, mesh=_sc_mesh(), out_type=[row, row],
        scratch_types=[pltpu.VMEM((SC_ROWS,), I32), pltpu.VMEM((SC_ROWS,), I32),
                       pltpu.VMEM((SC_ROWS, HALF), I32), pltpu.VMEM((SC_ROWS, HALF), I32),
                       pltpu.SemaphoreType.DMA, pltpu.SemaphoreType.DMA],
        name="sc_collect")
    def k(y_hbm, s1_hbm, s2_hbm, g1_hbm, g2_hbm, i1_v, i2_v, rows1, rows2, sem_g, sem_w):
        base = _sc_worker() * per

        @pl.loop(0, per // SC_ROWS)
        def _(ci):
            t0 = base + ci * SC_ROWS
            pltpu.sync_copy(s1_hbm.at[pl.ds(t0, SC_ROWS)], i1_v)
            pltpu.sync_copy(s2_hbm.at[pl.ds(t0, SC_ROWS)], i2_v)
            a1 = pltpu.async_copy(y_hbm.at[i1_v], rows1, sem_g)
            a2 = pltpu.async_copy(y_hbm.at[i2_v], rows2, sem_g)
            a1.wait()
            a2.wait()
            w1 = pltpu.async_copy(rows1, g1_hbm.at[pl.ds(t0, SC_ROWS)], sem_w)
            w2 = pltpu.async_copy(rows2, g2_hbm.at[pl.ds(t0, SC_ROWS)], sem_w)
            w1.wait()
            w2.wait()

    return k(ys, slot1, slot2)


def _expert_kernel(te_ref, used_ref, xs_ref, wg_hbm, wu_hbm, wd_hbm, ys_ref, wg_b, wu_b, wd_b,
                   wg_f, wu_f, wd_f, sem, slot_s):
    j = pl.program_id(0)
    n = pl.num_programs(0)
    e = te_ref[j]
    prev = te_ref[jnp.maximum(j - 1, 0)]
    valid = e < N_EXPERTS

    def weight_copies(expert, slot):
        return [pltpu.make_async_copy(hbm.at[expert], buf.at[slot], sem.at[slot, i])
                for i, (hbm, buf) in enumerate(((wg_hbm, wg_f), (wu_hbm, wu_f), (wd_hbm, wd_f)))]

    @pl.when((j == 0) & valid)
    def _():
        slot_s[0] = 0
        for c in weight_copies(e, 0):
            c.start()

    @pl.when(((j == 0) | (e != prev)) & valid)
    def _():
        slot = slot_s[0]
        for c in weight_copies(e, slot):
            c.wait()
        k = lax.while_loop(lambda k: (k < n) & (te_ref[jnp.minimum(k, n - 1)] == e),
                           lambda k: k + 1, j + 1)
        nxt = te_ref[jnp.minimum(k, n - 1)]

        @pl.when((k < n) & (nxt < N_EXPERTS))
        def _():
            for c in weight_copies(nxt, 1 - slot):
                c.start()

        wg_b[...] = wg_f[slot].astype(BF16)
        wu_b[...] = wu_f[slot].astype(BF16)
        wd_b[...] = wd_f[slot].astype(BF16)
        slot_s[0] = 1 - slot

    @pl.when(valid)
    def _():
        half = SLOT_TILE // 2
        rows = [pl.ds(i * half, half) for i in range(2)]
        row_id = lax.broadcasted_iota(I32, (half, HALF), 0)
        xs = [_unpack_bf16_pairs(jnp.where(row_id + i * half < used_ref[j], xs_ref[r, :], 0))
              .astype(BF16) for i, r in enumerate(rows)]
        gs = [_dot(x, wg_b[...]) for x in xs]
        us = [_dot(x, wu_b[...]) for x in xs]
        hids = [(_silu(g) * u).astype(BF16) for g, u in zip(gs, us)]
        ys = [_dot(hid, wd_b[...]) for hid in hids]
        for r, y in zip(rows, ys):
            ys_ref[r, :] = _pack_bf16_pairs(y)

    @pl.when(e >= N_EXPERTS)
    def _():
        ys_ref[...] = jnp.zeros_like(ys_ref)


def _expert_call(tile_expert, tile_used, xs, wg, wu, wd, n_tiles):
    hbm = pl.BlockSpec(memory_space=pl.ANY)
    return pl.pallas_call(
        _expert_kernel,
        grid_spec=pltpu.PrefetchScalarGridSpec(
            num_scalar_prefetch=2,
            grid=(n_tiles,),
            in_specs=[pl.BlockSpec((SLOT_TILE, HALF), lambda j, te, used: (j, 0)), hbm, hbm, hbm],
            out_specs=pl.BlockSpec((SLOT_TILE, HALF), lambda j, te, used: (j, 0)),
            scratch_shapes=[
                pltpu.VMEM((D_MODEL, D_EXPERT), BF16), pltpu.VMEM((D_MODEL, D_EXPERT), BF16),
                pltpu.VMEM((D_EXPERT, D_MODEL), BF16),
                pltpu.VMEM((2, D_MODEL, D_EXPERT), F32), pltpu.VMEM((2, D_MODEL, D_EXPERT), F32),
                pltpu.VMEM((2, D_EXPERT, D_MODEL), F32),
                pltpu.SemaphoreType.DMA((2, 3)),
                pltpu.SMEM((1,), I32),
            ],
        ),
        out_shape=jax.ShapeDtypeStruct((n_tiles * SLOT_TILE, HALF), I32),
        compiler_params=pltpu.CompilerParams(
            dimension_semantics=("arbitrary",), vmem_limit_bytes=VMEM_LIMIT),
        name="experts",
    )(tile_expert, tile_used, xs, wg, wu, wd)


def _final_kernel(h_ref, g1_ref, g2_ref, rw_ref, nw_ref, o_ref):
    rw = rw_ref[...]
    y = rw[:, 0:1] * _unpack_bf16_pairs(g1_ref[...]) + rw[:, 1:2] * _unpack_bf16_pairs(g2_ref[...])
    h = h_ref[...] + y
    o_ref[...] = h * lax.rsqrt(jnp.mean(h * h, axis=-1, keepdims=True) + EPS) * nw_ref[...]


def _final_call(h1, g1, g2, rw, nw, tm=512):
    m = h1.shape[0]
    return pl.pallas_call(
        _final_kernel,
        grid=(m // tm,),
        in_specs=[
            pl.BlockSpec((tm, D_MODEL), lambda i: (i, 0)),
            pl.BlockSpec((tm, HALF), lambda i: (i, 0)),
            pl.BlockSpec((tm, HALF), lambda i: (i, 0)),
            pl.BlockSpec((tm, LANES), lambda i: (i, 0)),
            pl.BlockSpec((1, D_MODEL), lambda i: (0, 0)),
        ],
        out_specs=pl.BlockSpec((tm, D_MODEL), lambda i: (i, 0)),
        out_shape=jax.ShapeDtypeStruct((m, D_MODEL), F32),
        compiler_params=pltpu.CompilerParams(dimension_semantics=("arbitrary",)),
        name="combine_final",
    )(h1, g1, g2, rw, nw)


def _pad_lanes(a):
    return jnp.pad(a, ((0, 0), (0, LANES - a.shape[1])))


def kernel(x, norm_mix_w, w_in, conv_w, A_log, dt_bias, gdn_norm_w, w_up_gdn, w_up_ret, w_out,
           norm_ffn_w, w_group, b_group, w_expert, b_expert, w_gate, w_up, w_down, norm_final_w):
    batch, seq, d = x.shape
    m = batch * seq
    h = x.reshape(m, d)
    depth = w_in.shape[0]
    sin, cos, inner, kdec, qdec, cdec = _retention_tables(seq)
    for l in range(depth):
        o_ab = 3 * GDN_QK
        o_z = o_ab + 2 * GDN_HEADS
        wl = w_in[l]
        w_main = jnp.concatenate([wl[:, :o_ab], wl[:, o_z:]], axis=1).astype(BF16)
        w_ab = _pad_lanes(wl[:, o_ab:o_z]).astype(BF16)
        proj, ab = _proj_call(h, norm_mix_w[l][None, :], w_main, w_ab)

        conv8 = jnp.pad(conv_w[l], ((0, 8 - GDN_CONV), (0, 0)))
        ya = _gdn_call(proj, ab, conv8, _pad_lanes(A_log[l][None, :]), _pad_lanes(dt_bias[l][None, :]),
                       gdn_norm_w[l][None, :], batch, seq)
        yb = _ret_call(proj, sin, cos, inner, kdec, qdec, cdec, batch, seq)

        w_router = _pad_lanes(jnp.concatenate([w_expert[l], w_group[l]], axis=1))
        wr_hi = w_router.astype(BF16)
        wr_lo = (w_router - wr_hi.astype(F32)).astype(BF16)
        b_router = _pad_lanes(jnp.concatenate([b_expert[l], b_group[l]])[None, :])
        h1, xn, ridx, rw, cnt = _merge_call(
            h, ya, yb, proj, w_up_gdn[l].astype(BF16), w_up_ret[l].astype(BF16), w_out[l].astype(BF16),
            norm_ffn_w[l][None, :], jnp.concatenate([wr_hi, wr_lo], axis=1), b_router)

        n_slots, n_tiles = _slot_counts(m)
        slots, tiles = _plan_call(ridx, cnt)
        slot1, slot2 = slots[0], slots[1]
        xs = _sc_dispatch(xn, slot1, slot2, n_slots)
        ys = _expert_call(tiles[:n_tiles, 0], tiles[:n_tiles, 1], xs,
                          w_gate[l], w_up[l], w_down[l], n_tiles)
        g1, g2 = _sc_collect(ys, slot1, slot2)

        assert depth == 1
        h = _final_call(h1, g1, g2, rw, norm_final_w[None, :])
    return h.reshape(batch, seq, d)
```

```python
import functools
import math

import jax
import jax.numpy as jnp
from jax import lax
from jax.experimental import pallas as pl
from jax.experimental.pallas import tpu as pltpu
from jax.experimental.pallas import tpu_sc as plsc

F32 = jnp.float32
BF16 = jnp.bfloat16
I32 = jnp.int32
U32 = jnp.uint32

D_MODEL = 1024
EPS = 1e-6
GDN_HEADS = 4
GDN_DK = 128
GDN_DV = 128
GDN_CONV = 4
RET_HEADS = 4
RET_DK = 128
RET_DV = 256
ROPE_BASE = 10000.0
N_GROUPS = 4
EXPERTS_PER_GROUP = 8
N_EXPERTS = N_GROUPS * EXPERTS_PER_GROUP
D_EXPERT = 512

GDN_QK = GDN_HEADS * GDN_DK
GDN_V = GDN_HEADS * GDN_DV
RET_QK = RET_HEADS * RET_DK
RET_V = RET_HEADS * RET_DV

LANES = 128
CHUNK = 128
INV_BLOCK = 16
GDN_PREP_CHUNKS = 2
VMEM_LIMIT = 56 * 1024 * 1024

MERGE_PARTS = 1
TOP_K = 2
SLOT_TILE = 256
PLAN_TILE = 256
PLAN_STEP = 1024
HALF = D_MODEL // 2
SC_WORKERS = 32
SC_ROWS = 64

PROJ_COLS = 3 * GDN_QK + GDN_V + 2 * RET_QK + 2 * RET_V + 2 * D_MODEL


def _silu(x):
    return x / (1.0 + jnp.exp(-x))


def _sigmoid(x):
    return 1.0 / (1.0 + jnp.exp(-x))


def _dot(a, b):
    return jnp.dot(a, b, preferred_element_type=F32)


def _dot_nt(a, b):
    return lax.dot_general(a, b, (((1,), (1,)), ((), ())), preferred_element_type=F32)


def _pack_bf16_pairs(x):
    bits = lax.bitcast_convert_type(x.astype(BF16).astype(F32), U32)
    packed = (bits[:, :HALF] >> 16) | (bits[:, HALF:] & jnp.uint32(0xFFFF0000))
    return lax.bitcast_convert_type(packed, I32)


def _unpack_bf16_pairs(p):
    p = lax.bitcast_convert_type(p, U32)
    lo = lax.bitcast_convert_type(p << 16, F32)
    hi = lax.bitcast_convert_type(p & jnp.uint32(0xFFFF0000), F32)
    return jnp.concatenate([lo, hi], axis=1)


def _proj_kernel(x0_ref, xn_ref, nw_ref, w_ref, wab_ref, proj_ref, ab_ref, u_ref):
    i = pl.program_id(0)
    j = pl.program_id(1)
    last = pl.num_programs(1) - 1

    def normed(x_ref):
        x = x_ref[...]
        return (x * lax.rsqrt(jnp.mean(x * x, axis=-1, keepdims=True) + EPS) * nw_ref[...]).astype(BF16)

    @pl.when((i == 0) & (j == 0))
    def _():
        u_ref[0] = normed(x0_ref)

    for slot in range(2):
        @pl.when((i % 2 == slot) & (j == 0))
        def _():
            ab_ref[...] = _dot(u_ref[slot], wab_ref[...])

        @pl.when((i % 2 == slot) & (j != last))
        def _():
            proj_ref[...] = _dot(u_ref[slot], w_ref[...]).astype(BF16)

        @pl.when((i % 2 == slot) & (j == last))
        def _():
            proj_ref[...] = _dot(u_ref[slot], w_ref[...]).astype(BF16)
            u_ref[1 - slot] = normed(xn_ref)


def _proj_call(x2, norm_w, w_main, w_ab, tm=1024, tn=1024):
    m = x2.shape[0]
    n_i = m // tm
    return pl.pallas_call(
        _proj_kernel,
        grid=(n_i, PROJ_COLS // tn),
        in_specs=[
            pl.BlockSpec((tm, D_MODEL), lambda i, j: (0, 0)),
            pl.BlockSpec((tm, D_MODEL), lambda i, j: (jnp.minimum(i + 1, n_i - 1), 0)),
            pl.BlockSpec((1, D_MODEL), lambda i, j: (0, 0)),
            pl.BlockSpec((D_MODEL, tn), lambda i, j: (0, j)),
            pl.BlockSpec((D_MODEL, LANES), lambda i, j: (0, 0)),
        ],
        out_specs=[
            pl.BlockSpec((tm, tn), lambda i, j: (i, j)),
            pl.BlockSpec((tm, LANES), lambda i, j: (i, 0)),
        ],
        out_shape=[
            jax.ShapeDtypeStruct((m, PROJ_COLS), BF16),
            jax.ShapeDtypeStruct((m, LANES), F32),
        ],
        scratch_shapes=[pltpu.VMEM((2, tm, D_MODEL), BF16)],
        compiler_params=pltpu.CompilerParams(
            dimension_semantics=("arbitrary", "arbitrary"), vmem_limit_bytes=VMEM_LIMIT),
        name="proj",
    )(x2, x2, norm_w, w_main, w_ab)


def _unit_lower_inverses(lows, ii, jj):
    eye = jnp.where(ii == jj, 1.0, 0.0).astype(F32)
    in_block = (ii // INV_BLOCK) == (jj // INV_BLOCK)
    ps = [jnp.where(in_block, -low, 0.0) for low in lows]
    ts = [eye + p for p in ps]
    span = 2
    while span < INV_BLOCK:
        ps = [_dot(p, p) for p in ps]
        ts = [t + _dot(t, p) for t, p in zip(ts, ps)]
        span *= 2
    s = INV_BLOCK
    while s < CHUNK:
        off_diag = ((ii // (2 * s)) == (jj // (2 * s))) & ((ii // s) != (jj // s))
        xs = [_dot(jnp.where(off_diag, low, 0.0), t) for low, t in zip(lows, ts)]
        ts = [t - _dot(t, x) for t, x in zip(ts, xs)]
        s *= 2
    return ts


def _gdn_kernel(qkv_ref, z_ref, ab_ref, convw_ref, alog_ref, dtb_ref, normw_ref, o_ref,
                b_s, o0_s, m_s, qp_s, gl_s, state_s, q_s, k_s, kb_s, rhs_s, dec_s, cv_s, qg_o, kdt_o,
                *, seq):
    nchunk = seq // CHUNK
    ii = lax.broadcasted_iota(jnp.int32, (CHUNK, CHUNK), 0)
    jj = lax.broadcasted_iota(jnp.int32, (CHUNK, CHUNK), 1)
    causal = ii >= jj
    strict = ii > jj
    tri = jnp.where(causal, 1.0, 0.0).astype(F32)
    neg_a = -jnp.exp(alog_ref[...])
    dtb = dtb_ref[...]

    def conv_cols(c, r0, lo, buf):
        x = qkv_ref[pl.ds(r0, CHUNK), lo:lo + LANES].astype(F32)
        prev0 = pl.multiple_of(jnp.maximum(r0 - 16, 0), 16)
        prev = qkv_ref[pl.ds(prev0, 16), lo:lo + LANES].astype(F32)
        buf[0:8, :] = prev[8:16] * jnp.where(c > 0, 1.0, 0.0)
        buf[8:8 + CHUNK, :] = x
        w = convw_ref[:, lo:lo + LANES]
        y = (w[3:4] * x + w[2:3] * buf[7:7 + CHUNK, :] + w[1:2] * buf[6:6 + CHUNK, :]
             + w[0:1] * buf[5:5 + CHUNK, :])
        return _silu(y)

    def l2n(x):
        return x * lax.rsqrt(jnp.sum(x * x, axis=-1, keepdims=True) + EPS)

    tri_b = tri.astype(BF16)

    def chunk_cumsum(g):
        g1 = g.astype(BF16)
        r1 = g - g1.astype(F32)
        g2 = r1.astype(BF16)
        g3 = (r1 - g2.astype(F32)).astype(BF16)
        return _dot(tri_b, g1) + (_dot(tri_b, g2) + _dot(tri_b, g3))

    def operands(cc, slot):
        for sub in range(GDN_PREP_CHUNKS):
            c = cc * GDN_PREP_CHUNKS + sub
            r0 = pl.multiple_of(c * CHUNK, CHUNK)
            ab = ab_ref[pl.ds(r0, CHUNK), :]
            xg = ab + dtb
            softplus = jnp.maximum(xg, 0.0) + jnp.log(1.0 + jnp.exp(-jnp.abs(xg)))
            g_all = neg_a * softplus
            beta_all = _sigmoid(ab)
            gc_all = chunk_cumsum(g_all)
            gc_t = gc_all.T
            gl_s[c] = jnp.exp(gc_all[CHUNK - 1:CHUNK, :])
            for h in range(GDN_HEADS):
                n = sub * GDN_HEADS + h
                bufs = [cv_s.at[slot, 3 * n + i] for i in range(3)]
                q = l2n(conv_cols(c, r0, h * GDN_DK, bufs[0])) * (GDN_DK ** -0.5)
                k = l2n(conv_cols(c, r0, GDN_QK + h * GDN_DK, bufs[1]))
                v = conv_cols(c, r0, 2 * GDN_QK + h * GDN_DV, bufs[2])
                gcol = gc_all[:, h:h + 1]
                grow = gc_t[h:h + 1, :]
                beta = beta_all[:, GDN_HEADS + h:GDN_HEADS + h + 1]
                dec_s[slot, n] = jnp.where(causal, jnp.exp(gcol - grow), 0.0)
                eg = jnp.exp(gcol)
                kb = k * beta
                q_s[slot, n] = q.astype(BF16)
                k_s[slot, n] = k.astype(BF16)
                kb_s[slot, n] = kb.astype(BF16)
                rhs_s[slot, n] = jnp.concatenate([v * beta, kb * eg], axis=1).astype(BF16)
                qg_o[slot, n] = q * eg
                kd = k * jnp.exp(gc_all[CHUNK - 1:CHUNK, h:h + 1] - gcol)
                kdt_o[slot, n] = kd.T.astype(BF16)

    def solve(cc, slot):
        items = [(cc * GDN_PREP_CHUNKS + sub, h, sub * GDN_HEADS + h)
                 for sub in range(GDN_PREP_CHUNKS) for h in range(GDN_HEADS)]
        kks = [_dot_nt(kb_s[slot, n], k_s[slot, n]) for _, _, n in items]
        qks = [_dot_nt(q_s[slot, n], k_s[slot, n]) for _, _, n in items]
        lows = [jnp.where(strict, kk * dec_s[slot, n], 0.0) for kk, (_, _, n) in zip(kks, items)]
        attns = [(qk * dec_s[slot, n]).astype(BF16) for qk, (_, _, n) in zip(qks, items)]
        ts = _unit_lower_inverses(lows, ii, jj)
        uws = [_dot(t.astype(BF16), rhs_s[slot, n]).astype(BF16) for t, (_, _, n) in zip(ts, items)]
        kds = [_dot(kdt_o[slot, n], uw) for uw, (_, _, n) in zip(uws, items)]
        ats = [_dot(attn, uw) for attn, uw in zip(attns, uws)]
        for kd_uw, at_uw, (c, h, n) in zip(kds, ats, items):
            b_s[c, h] = kd_uw[:, :GDN_DV]
            m_s[c, h] = (-kd_uw[:, GDN_DV:]).astype(BF16)
            o0_s[c, h] = at_uw[:, :GDN_DV]
            qp_s[c, h] = (qg_o[slot, n] - at_uw[:, GDN_DV:]).astype(BF16)

    ngroup = nchunk // GDN_PREP_CHUNKS
    operands(0, 0)

    def prep(i, carry):
        operands(2 * i + 1, 1)
        solve(2 * i, 0)
        operands(jnp.minimum(2 * i + 2, ngroup - 1), 0)
        solve(2 * i + 1, 1)
        return carry

    lax.fori_loop(0, ngroup // 2, prep, 0)

    state_s[...] = jnp.zeros_like(state_s)
    normw = normw_ref[...]

    def scan(c, carry):
        r0 = pl.multiple_of(c * CHUNK, CHUNK)
        gl = gl_s[c]
        heads = range(GDN_HEADS)
        ss = [state_s[h] for h in heads]
        sbs = [s.astype(BF16) for s in ss]
        mss = [_dot(m_s[c, h], sbs[h]) for h in heads]
        qss = [_dot(qp_s[c, h], sbs[h]) for h in heads]
        for h in heads:
            state_s[h] = ss[h] * gl[:, h:h + 1] + (mss[h] + b_s[c, h])
            o = qss[h] + o0_s[c, h]
            z = z_ref[pl.ds(r0, CHUNK), h * GDN_DV:(h + 1) * GDN_DV].astype(F32)
            on = o * lax.rsqrt(jnp.mean(o * o, axis=-1, keepdims=True) + EPS) * normw
            o_ref[pl.ds(r0, CHUNK), h * GDN_DV:(h + 1) * GDN_DV] = (on * _silu(z)).astype(BF16)
        return carry

    lax.fori_loop(0, nchunk, scan, 0)


def _gdn_call(proj, ab, conv_w8, alog_p, dtb_p, normw, batch, seq):
    nchunk = seq // CHUNK
    hs = (nchunk, GDN_HEADS, CHUNK, CHUNK)
    ops = (2, GDN_PREP_CHUNKS * GDN_HEADS, CHUNK, CHUNK)
    return pl.pallas_call(
        functools.partial(_gdn_kernel, seq=seq),
        grid=(batch,),
        in_specs=[
            pl.BlockSpec((seq, 3 * GDN_QK), lambda b: (b, 0)),
            pl.BlockSpec((seq, GDN_V), lambda b: (b, 3)),
            pl.BlockSpec((seq, LANES), lambda b: (b, 0)),
            pl.BlockSpec((8, 3 * GDN_QK), lambda b: (0, 0)),
            pl.BlockSpec((1, LANES), lambda b: (0, 0)),
            pl.BlockSpec((1, LANES), lambda b: (0, 0)),
            pl.BlockSpec((1, GDN_DV), lambda b: (0, 0)),
        ],
        out_specs=pl.BlockSpec((seq, GDN_V), lambda b: (b, 0)),
        out_shape=jax.ShapeDtypeStruct((batch * seq, GDN_V), BF16),
        scratch_shapes=[
            pltpu.VMEM(hs, F32), pltpu.VMEM(hs, F32), pltpu.VMEM(hs, BF16), pltpu.VMEM(hs, BF16),
            pltpu.VMEM((nchunk, 1, LANES), F32),
            pltpu.VMEM((GDN_HEADS, GDN_DK, GDN_DV), F32),
            pltpu.VMEM(ops, BF16), pltpu.VMEM(ops, BF16), pltpu.VMEM(ops, BF16),
            pltpu.VMEM(ops[:3] + (2 * CHUNK,), BF16),
            pltpu.VMEM(ops, F32),
            pltpu.VMEM((2, 3 * ops[1], 8 + CHUNK, LANES), F32),
            pltpu.VMEM(ops, F32), pltpu.VMEM(ops, BF16),
        ],
        compiler_params=pltpu.CompilerParams(
            dimension_semantics=("arbitrary",), vmem_limit_bytes=VMEM_LIMIT),
        name="gdn",
    )(proj, proj, ab, conv_w8, alog_p, dtb_p, normw)


def _ret_kernel(qk_ref, v_ref, g_ref, sin_ref, cos_ref, inner_ref, kdec_ref, qdec_ref, cdec_ref,
                o_ref, state_s, q_s, qd_s, k_s, kt_s, *, seq):
    nchunk = seq // CHUNK
    lane = lax.broadcasted_iota(jnp.int32, (CHUNK, RET_DK), 1)
    even = (lane % 2) == 0

    def rotate(x, sin, cos):
        nxt = pltpu.roll(x, RET_DK - 1, axis=1)
        prv = pltpu.roll(x, 1, axis=1)
        return x * cos + jnp.where(even, -nxt, prv) * sin

    state_s[...] = jnp.zeros_like(state_s)
    kdec = kdec_ref[...]
    qdec = qdec_ref[...]
    cdec = cdec_ref[...]

    heads = range(RET_HEADS)

    def operands(c, slot):
        r0 = pl.multiple_of(c * CHUNK, CHUNK)
        sin = sin_ref[pl.ds(r0, CHUNK), :]
        cos = cos_ref[pl.ds(r0, CHUNK), :]
        for h in heads:
            q = rotate(qk_ref[pl.ds(r0, CHUNK), h * RET_DK:(h + 1) * RET_DK].astype(F32), sin, cos)
            k = rotate(qk_ref[pl.ds(r0, CHUNK), RET_QK + h * RET_DK:RET_QK + (h + 1) * RET_DK]
                       .astype(F32), sin, cos) * (RET_DK ** -0.5)
            q_s[slot, h] = q.astype(BF16)
            qd_s[slot, h] = (q * qdec[:, h:h + 1]).astype(BF16)
            k_s[slot, h] = k.astype(BF16)
            kt_s[slot, h] = (k * kdec[:, h:h + 1]).T.astype(BF16)

    def outputs(c, slot):
        r0 = pl.multiple_of(c * CHUNK, CHUNK)
        vs = [v_ref[pl.ds(r0, CHUNK), h * RET_DV:(h + 1) * RET_DV] for h in heads]
        ss = [state_s[h] for h in heads]
        qks = [_dot_nt(q_s[slot, h], k_s[slot, h]) for h in heads]
        inters = [_dot(qd_s[slot, h], ss[h].astype(BF16)) for h in heads]
        kvs = [_dot(kt_s[slot, h], vs[h]) for h in heads]
        intras = [_dot((qks[h] * inner_ref[h]).astype(BF16), vs[h]) for h in heads]
        for h in heads:
            state_s[h] = ss[h] * cdec[:, h:h + 1] + kvs[h]
            o = intras[h] + inters[h]
            gate = g_ref[pl.ds(r0, CHUNK), h * RET_DV:(h + 1) * RET_DV].astype(F32)
            on = o * lax.rsqrt(jnp.mean(o * o, axis=-1, keepdims=True) + EPS)
            o_ref[pl.ds(r0, CHUNK), h * RET_DV:(h + 1) * RET_DV] = (on * _silu(gate)).astype(BF16)

    operands(0, 0)

    def body(i, carry):
        operands(2 * i + 1, 1)
        outputs(2 * i, 0)
        operands(jnp.minimum(2 * i + 2, nchunk - 1), 0)
        outputs(2 * i + 1, 1)
        return carry

    lax.fori_loop(0, nchunk // 2, body, 0)


def _ret_call(proj, sin, cos, inner, kdec, qdec, cdec, batch, seq):
    return pl.pallas_call(
        functools.partial(_ret_kernel, seq=seq),
        grid=(batch,),
        in_specs=[
            pl.BlockSpec((seq, 2 * RET_QK), lambda b: (b, 2)),
            pl.BlockSpec((seq, RET_V), lambda b: (b, 3)),
            pl.BlockSpec((seq, RET_V), lambda b: (b, 4)),
            pl.BlockSpec((seq, RET_DK), lambda b: (0, 0)),
            pl.BlockSpec((seq, RET_DK), lambda b: (0, 0)),
            pl.BlockSpec((RET_HEADS, CHUNK, CHUNK), lambda b: (0, 0, 0)),
            pl.BlockSpec((CHUNK, LANES), lambda b: (0, 0)),
            pl.BlockSpec((CHUNK, LANES), lambda b: (0, 0)),
            pl.BlockSpec((1, LANES), lambda b: (0, 0)),
        ],
        out_specs=pl.BlockSpec((seq, RET_V), lambda b: (b, 0)),
        out_shape=jax.ShapeDtypeStruct((batch * seq, RET_V), BF16),
        scratch_shapes=[pltpu.VMEM((RET_HEADS, RET_DK, RET_DV), F32)]
        + [pltpu.VMEM((2, RET_HEADS, CHUNK, RET_DK), BF16)] * 4,
        compiler_params=pltpu.CompilerParams(
            dimension_semantics=("arbitrary",), vmem_limit_bytes=VMEM_LIMIT),
        name="retention",
    )(proj, proj, proj, sin, cos, inner, kdec, qdec, cdec)


def _retention_tables(seq):
    inv_freq = 1.0 / (ROPE_BASE ** jnp.linspace(0.0, 1.0, RET_DK // 2, dtype=F32))
    ang = jnp.arange(seq, dtype=F32)[:, None] * inv_freq[None, :]
    sin = jnp.repeat(jnp.sin(ang), 2, axis=-1)
    cos = jnp.repeat(jnp.cos(ang), 2, axis=-1)
    log_gamma = jnp.log(1.0 - 2.0 ** (-5.0 - jnp.arange(RET_HEADS, dtype=F32)))
    idx = jnp.arange(CHUNK, dtype=F32)
    causal = jnp.tril(jnp.ones((CHUNK, CHUNK), dtype=bool))
    rel = jnp.where(causal, idx[:, None] - idx[None, :], 0.0)
    inner = jnp.where(causal, jnp.exp(rel[None] * log_gamma[:, None, None]), 0.0)
    k_decay = jnp.exp(log_gamma[:, None] * (CHUNK - 1.0 - idx)[None, :])
    q_decay = jnp.exp(log_gamma[:, None] * (idx + 1.0)[None, :])
    chunk_decay = jnp.exp(log_gamma * CHUNK)
    pad = LANES - RET_HEADS
    kdec = jnp.pad(k_decay.T, ((0, 0), (0, pad)))
    qdec = jnp.pad(q_decay.T, ((0, 0), (0, pad)))
    cdec = jnp.pad(chunk_decay[None, :], ((0, 0), (0, pad)))
    return sin, cos, inner, kdec, qdec, cdec


def _merge_kernel(x_ref, ya_ref, yb_ref, ma_ref, mb_ref, wa_ref, wr_ref, wo_ref, nw_ref,
                  wrt_ref, br_ref, h_ref, xn_ref, ridx_ref, rw_ref, cnt_ref):
    tm = x_ref.shape[0] // MERGE_PARTS
    rows = [pl.ds(i * tm, tm) for i in range(MERGE_PARTS)]
    a_ = [_dot(ya_ref[r, :], wa_ref[...]) for r in rows]
    r_ = [_dot(yb_ref[r, :], wr_ref[...]) for r in rows]
    merged = [(_sigmoid(ma_ref[r, :].astype(F32)) * a + _sigmoid(mb_ref[r, :].astype(F32)) * rr)
              .astype(BF16) for r, a, rr in zip(rows, a_, r_)]
    hs = [x_ref[r, :] + _dot(mg, wo_ref[...]) for r, mg in zip(rows, merged)]
    xcats = []
    for r, h in zip(rows, hs):
        h_ref[r, :] = h
        xn = h * lax.rsqrt(jnp.mean(h * h, axis=-1, keepdims=True) + EPS) * nw_ref[...]
        xn_ref[r, :] = _pack_bf16_pairs(xn)
        xh = xn.astype(BF16)
        xl = (xn - xh.astype(F32)).astype(BF16)
        xcats.append(jnp.concatenate([xh, xl], axis=0))
    parts_ = [_dot(xc, wrt_ref[...]) for xc in xcats]
    counts = jnp.zeros((1, LANES), F32)
    for r, parts in zip(rows, parts_):
        counts = counts + _route(parts, tm, br_ref[...], ridx_ref.at[r, :], rw_ref.at[r, :])

    @pl.when(pl.program_id(0) == 0)
    def _():
        cnt_ref[...] = jnp.zeros_like(cnt_ref)

    cnt_ref[...] += jnp.broadcast_to(counts, cnt_ref.shape)


def _route(parts, tm, bias, ridx_ref, rw_ref):
    logits = (parts[:tm, :LANES] + (parts[tm:, :LANES] + parts[:tm, LANES:]
                                    + parts[tm:, LANES:])) + bias
    lane = lax.broadcasted_iota(jnp.int32, (tm, LANES), 1)
    neg = -jnp.inf
    gl = jnp.where((lane >= N_EXPERTS) & (lane < N_EXPERTS + N_GROUPS), logits, neg)
    gmax = jnp.max(gl, axis=-1, keepdims=True)
    gidx = jnp.min(jnp.where(gl == gmax, lane, LANES), axis=-1, keepdims=True) - N_EXPERTS
    g_w = 1.0 / jnp.sum(jnp.exp(gl - gmax), axis=-1, keepdims=True)
    el = jnp.where((lane // EXPERTS_PER_GROUP == gidx) & (lane < N_EXPERTS), logits, neg)
    m1 = jnp.max(el, axis=-1, keepdims=True)
    i1 = jnp.min(jnp.where(el == m1, lane, LANES), axis=-1, keepdims=True)
    el2 = jnp.where(lane == i1, neg, el)
    m2 = jnp.max(el2, axis=-1, keepdims=True)
    i2 = jnp.min(jnp.where(el2 == m2, lane, LANES), axis=-1, keepdims=True)
    e2 = jnp.exp(m2 - m1)
    p1 = g_w / (1.0 + e2)
    p2 = g_w * e2 / (1.0 + e2)
    ridx_ref[...] = jnp.where(lane == 0, i1, jnp.where(lane == 1, i2, 0))
    rw_ref[...] = jnp.where(lane == 0, p1, jnp.where(lane == 1, p2, 0.0))
    onehot = jnp.where((lane == i1) | (lane == i2), 1.0, 0.0)
    return jnp.sum(onehot, axis=0, keepdims=True)


def _merge_call(x2, ya, yb, proj, wa, wr, wo, nw, w_router, b_router, tm=1024):
    m = x2.shape[0]
    full = lambda shape: pl.BlockSpec(shape, lambda i: (0, 0))
    return pl.pallas_call(
        _merge_kernel,
        grid=(m // tm,),
        in_specs=[
            pl.BlockSpec((tm, D_MODEL), lambda i: (i, 0)),
            pl.BlockSpec((tm, GDN_V), lambda i: (i, 0)),
            pl.BlockSpec((tm, RET_V), lambda i: (i, 0)),
            pl.BlockSpec((tm, D_MODEL), lambda i: (i, 5)),
            pl.BlockSpec((tm, D_MODEL), lambda i: (i, 6)),
            full((GDN_V, D_MODEL)), full((RET_V, D_MODEL)), full((D_MODEL, D_MODEL)),
            full((1, D_MODEL)),
            full((D_MODEL, 2 * LANES)), full((1, LANES)),
        ],
        out_specs=[
            pl.BlockSpec((tm, D_MODEL), lambda i: (i, 0)),
            pl.BlockSpec((tm, HALF), lambda i: (i, 0)),
            pl.BlockSpec((tm, LANES), lambda i: (i, 0)),
            pl.BlockSpec((tm, LANES), lambda i: (i, 0)),
            pl.BlockSpec((8, LANES), lambda i: (0, 0)),
        ],
        out_shape=[
            jax.ShapeDtypeStruct((m, D_MODEL), F32),
            jax.ShapeDtypeStruct((m, HALF), I32),
            jax.ShapeDtypeStruct((m, LANES), I32),
            jax.ShapeDtypeStruct((m, LANES), F32),
            jax.ShapeDtypeStruct((8, LANES), F32),
        ],
        compiler_params=pltpu.CompilerParams(
            dimension_semantics=("arbitrary",), vmem_limit_bytes=VMEM_LIMIT),
        name="merge_router",
    )(x2, ya, yb, proj, proj, wa, wr, wo, nw, w_router, b_router)


def _slot_counts(m):
    n_slots = TOP_K * m + N_EXPERTS * SLOT_TILE
    return n_slots, n_slots // SLOT_TILE


def _lane_prefix_sum(x, lane):
    s = 1
    while s < LANES:
        x = x + jnp.where(lane >= s, pltpu.roll(x, s, axis=1), 0.0)
        s *= 2
    return x


def _plan_kernel(ridx_ref, cnt_ref, slots_ref, tile_ref, carry_s, off_s):
    i = pl.program_id(0)
    lane = lax.broadcasted_iota(I32, (PLAN_TILE, LANES), 1)
    row = lax.broadcasted_iota(I32, (PLAN_TILE, LANES), 0)
    lane1 = lane[0:1]

    @pl.when(i == 0)
    def _():
        cnt = cnt_ref[0:1, :]
        tile = float(SLOT_TILE)
        padded = jnp.floor((cnt + (tile - 1.0)) / tile) * tile
        incl = _lane_prefix_sum(padded, lane1)
        off = incl - padded
        off_s[...] = off
        carry_s[...] = jnp.zeros_like(carry_s)
        first = (row * SLOT_TILE).astype(F32)
        ended = jnp.where((lane < N_EXPERTS) & (incl <= first), 1.0, 0.0)
        tile_e = jnp.sum(ended, axis=-1, keepdims=True)
        last = jnp.sum(jnp.where(lane.astype(F32) == tile_e, off + cnt, 0.0), axis=-1, keepdims=True)
        used = jnp.clip(last - first[:, 0:1], 0.0, tile)
        tile_ref[...] = jnp.where(lane == 0, tile_e, jnp.where(lane == 1, used, 0.0)).astype(I32)

    strict = jnp.where(row[:, 0:1] > lax.broadcasted_iota(I32, (PLAN_TILE, PLAN_TILE), 1),
                       1.0, 0.0).astype(BF16)
    off = off_s[...]
    carry = carry_s[...]
    for sb in range(PLAN_STEP // PLAN_TILE):
        rows = pl.ds(sb * PLAN_TILE, PLAN_TILE)
        e1 = ridx_ref[rows, 0:1]
        e2 = ridx_ref[rows, 1:2]
        onehot = jnp.where((lane == e1) | (lane == e2), 1.0, 0.0)
        pos = _dot(strict, onehot.astype(BF16)) + (carry + off)
        s1 = jnp.sum(jnp.where(lane == e1, pos, 0.0), axis=-1, keepdims=True)
        s2 = jnp.sum(jnp.where(lane == e2, pos, 0.0), axis=-1, keepdims=True)
        both = jnp.where(lane == 0, s1, jnp.where(lane == 1, s2, 0.0))
        for q in range(PLAN_TILE // LANES):
            t = both[q * LANES:(q + 1) * LANES].T
            c0 = sb * PLAN_TILE + q * LANES
            slots_ref[:, c0:c0 + LANES] = t[0:8].astype(I32)
        carry = carry + jnp.sum(onehot, axis=0, keepdims=True)
    carry_s[...] = carry


def _plan_call(ridx, cnt):
    m = ridx.shape[0]
    _, n_tiles = _slot_counts(m)
    assert n_tiles <= PLAN_TILE and SLOT_TILE == PLAN_TILE
    return pl.pallas_call(
        _plan_kernel,
        grid=(m // PLAN_STEP,),
        in_specs=[pl.BlockSpec((PLAN_STEP, LANES), lambda i: (i, 0)),
                  pl.BlockSpec((8, LANES), lambda i: (0, 0))],
        out_specs=[
            pl.BlockSpec((8, PLAN_STEP), lambda i: (0, i)),
            pl.BlockSpec((PLAN_TILE, LANES), lambda i: (0, 0)),
        ],
        out_shape=[
            jax.ShapeDtypeStruct((8, m), I32),
            jax.ShapeDtypeStruct((PLAN_TILE, LANES), I32),
        ],
        scratch_shapes=[pltpu.VMEM((1, LANES), F32), pltpu.VMEM((1, LANES), F32)],
        compiler_params=pltpu.CompilerParams(dimension_semantics=("arbitrary",)),
        name="dispatch_plan",
    )(ridx, cnt)


def _sc_mesh():
    return plsc.VectorSubcoreMesh(core_axis_name="c", subcore_axis_name="s")


def _sc_worker():
    return lax.axis_index("s") * 2 + lax.axis_index("c")


def _sc_dispatch(xn, slot1, slot2, n_rows):
    m = xn.shape[0]
    per = m // SC_WORKERS
    n_pairs = per // (2 * SC_ROWS)

    @functools.partial(
        pl.kernel, mesh=_sc_mesh(),
        out_type=jax.ShapeDtypeStruct((n_rows, HALF), I32),
        scratch_types=[pltpu.VMEM((SC_ROWS,), I32), pltpu.VMEM((SC_ROWS,), I32),
                       pltpu.VMEM((SC_ROWS, HALF), I32), pltpu.VMEM((SC_ROWS, HALF), I32),
                       pltpu.SemaphoreType.DMA, pltpu.SemaphoreType.DMA, pltpu.SemaphoreType.DMA],
        name="sc_dispatch")
    def k(x_hbm, s1_hbm, s2_hbm, o_hbm, i1_v, i2_v, rows0, rows1, sem_r0, sem_r1, sem_w):
        base = _sc_worker() * per

        def read(chunk, rows_v, sem):
            return pltpu.make_async_copy(x_hbm.at[pl.ds(base + chunk * SC_ROWS, SC_ROWS)], rows_v, sem)

        def scatter(chunk, rows_v):
            t0 = base + chunk * SC_ROWS
            pltpu.sync_copy(s1_hbm.at[pl.ds(t0, SC_ROWS)], i1_v)
            pltpu.sync_copy(s2_hbm.at[pl.ds(t0, SC_ROWS)], i2_v)
            c1 = pltpu.async_copy(rows_v, o_hbm.at[i1_v], sem_w)
            c2 = pltpu.async_copy(rows_v, o_hbm.at[i2_v], sem_w)
            c1.wait()
            c2.wait()

        read(0, rows0, sem_r0).start()

        @pl.loop(0, n_pairs)
        def _(i):
            read(2 * i, rows0, sem_r0).wait()
            read(2 * i + 1, rows1, sem_r1).start()
            scatter(2 * i, rows0)
            read(2 * i + 1, rows1, sem_r1).wait()

            @pl.when(i + 1 < n_pairs)
            def _():
                read(2 * i + 2, rows0, sem_r0).start()

            scatter(2 * i + 1, rows1)

    return k(xn, slot1, slot2)


def _sc_collect(ys, slot1, slot2):
    m = slot1.shape[0]
    per = m // SC_WORKERS
    row = jax.ShapeDtypeStruct((m, HALF), I32)

    @functools.partial(
        pl.kernel, mesh=_sc_mesh(), out_type=[row, row],
        scratch_types=[pltpu.VMEM((SC_ROWS,), I32), pltpu.VMEM((SC_ROWS,), I32),
                       pltpu.VMEM((SC_ROWS, HALF), I32), pltpu.VMEM((SC_ROWS, HALF), I32),
                       pltpu.SemaphoreType.DMA, pltpu.SemaphoreType.DMA],
        name="sc_collect")
    def k(y_hbm, s1_hbm, s2_hbm, g1_hbm, g2_hbm, i1_v, i2_v, rows1, rows2, sem_g, sem_w):
        base = _sc_worker() * per

        @pl.loop(0, per // SC_ROWS)
        def _(ci):
            t0 = base + ci * SC_ROWS
            pltpu.sync_copy(s1_hbm.at[pl.ds(t0, SC_ROWS)], i1_v)
            pltpu.sync_copy(s2_hbm.at[pl.ds(t0, SC_ROWS)], i2_v)
            a1 = pltpu.async_copy(y_hbm.at[i1_v], rows1, sem_g)
            a2 = pltpu.async_copy(y_hbm.at[i2_v], rows2, sem_g)
            a1.wait()
            a2.wait()
            w1 = pltpu.async_copy(rows1, g1_hbm.at[pl.ds(t0, SC_ROWS)], sem_w)
            w2 = pltpu.async_copy(rows2, g2_hbm.at[pl.ds(t0, SC_ROWS)], sem_w)
            w1.wait()
            w2.wait()

    return k(ys, slot1, slot2)


def _expert_kernel(te_ref, used_ref, xs_ref, wg_hbm, wu_hbm, wd_hbm, ys_ref, wg_b, wu_b, wd_b,
                   wg_f, wu_f, wd_f, sem, slot_s):
    j = pl.program_id(0)
    n = pl.num_programs(0)
    e = te_ref[j]
    prev = te_ref[jnp.maximum(j - 1, 0)]
    valid = e < N_EXPERTS

    def weight_copies(expert, slot):
        return [pltpu.make_async_copy(hbm.at[expert], buf.at[slot], sem.at[slot, i])
                for i, (hbm, buf) in enumerate(((wg_hbm, wg_f), (wu_hbm, wu_f), (wd_hbm, wd_f)))]

    @pl.when((j == 0) & valid)
    def _():
        slot_s[0] = 0
        for c in weight_copies(e, 0):
            c.start(priority=1)

    @pl.when(((j == 0) | (e != prev)) & valid)
    def _():
        slot = slot_s[0]
        for c in weight_copies(e, slot):
            c.wait()
        k = lax.while_loop(lambda k: (k < n) & (te_ref[jnp.minimum(k, n - 1)] == e),
                           lambda k: k + 1, j + 1)
        nxt = te_ref[jnp.minimum(k, n - 1)]

        @pl.when((k < n) & (nxt < N_EXPERTS))
        def _():
            for c in weight_copies(nxt, 1 - slot):
                c.start(priority=1)

        wg_b[...] = wg_f[slot].astype(BF16)
        wu_b[...] = wu_f[slot].astype(BF16)
        wd_b[...] = wd_f[slot].astype(BF16)
        slot_s[0] = 1 - slot

    @pl.when(valid)
    def _():
        half = SLOT_TILE // 2
        rows = [pl.ds(i * half, half) for i in range(2)]
        row_id = lax.broadcasted_iota(I32, (half, HALF), 0)
        xs = [_unpack_bf16_pairs(jnp.where(row_id + i * half < used_ref[j], xs_ref[r, :], 0))
              .astype(BF16) for i, r in enumerate(rows)]
        gs = [_dot(x, wg_b[...]) for x in xs]
        us = [_dot(x, wu_b[...]) for x in xs]
        hids = [(_silu(g) * u).astype(BF16) for g, u in zip(gs, us)]
        ys = [_dot(hid, wd_b[...]) for hid in hids]
        for r, y in zip(rows, ys):
            ys_ref[r, :] = _pack_bf16_pairs(y)

    @pl.when(e >= N_EXPERTS)
    def _():
        ys_ref[...] = jnp.zeros_like(ys_ref)


def _expert_call(tile_expert, tile_used, xs, wg, wu, wd, n_tiles):
    hbm = pl.BlockSpec(memory_space=pl.ANY)
    return pl.pallas_call(
        _expert_kernel,
        grid_spec=pltpu.PrefetchScalarGridSpec(
            num_scalar_prefetch=2,
            grid=(n_tiles,),
            in_specs=[pl.BlockSpec((SLOT_TILE, HALF), lambda j, te, used: (j, 0)), hbm, hbm, hbm],
            out_specs=pl.BlockSpec((SLOT_TILE, HALF), lambda j, te, used: (j, 0)),
            scratch_shapes=[
                pltpu.VMEM((D_MODEL, D_EXPERT), BF16), pltpu.VMEM((D_MODEL, D_EXPERT), BF16),
                pltpu.VMEM((D_EXPERT, D_MODEL), BF16),
                pltpu.VMEM((2, D_MODEL, D_EXPERT), F32), pltpu.VMEM((2, D_MODEL, D_EXPERT), F32),
                pltpu.VMEM((2, D_EXPERT, D_MODEL), F32),
                pltpu.SemaphoreType.DMA((2, 3)),
                pltpu.SMEM((1,), I32),
            ],
        ),
        out_shape=jax.ShapeDtypeStruct((n_tiles * SLOT_TILE, HALF), I32),
        compiler_params=pltpu.CompilerParams(
            dimension_semantics=("arbitrary",), vmem_limit_bytes=VMEM_LIMIT),
        name="experts",
    )(tile_expert, tile_used, xs, wg, wu, wd)


def _final_kernel(h_ref, g1_ref, g2_ref, rw_ref, nw_ref, o_ref):
    rw = rw_ref[...]
    y = rw[:, 0:1] * _unpack_bf16_pairs(g1_ref[...]) + rw[:, 1:2] * _unpack_bf16_pairs(g2_ref[...])
    h = h_ref[...] + y
    o_ref[...] = h * lax.rsqrt(jnp.mean(h * h, axis=-1, keepdims=True) + EPS) * nw_ref[...]


def _final_call(h1, g1, g2, rw, nw, tm=512):
    m = h1.shape[0]
    return pl.pallas_call(
        _final_kernel,
        grid=(m // tm,),
        in_specs=[
            pl.BlockSpec((tm, D_MODEL), lambda i: (i, 0)),
            pl.BlockSpec((tm, HALF), lambda i: (i, 0)),
            pl.BlockSpec((tm, HALF), lambda i: (i, 0)),
            pl.BlockSpec((tm, LANES), lambda i: (i, 0)),
            pl.BlockSpec((1, D_MODEL), lambda i: (0, 0)),
        ],
        out_specs=pl.BlockSpec((tm, D_MODEL), lambda i: (i, 0)),
        out_shape=jax.ShapeDtypeStruct((m, D_MODEL), F32),
        compiler_params=pltpu.CompilerParams(dimension_semantics=("arbitrary",)),
        name="combine_final",
    )(h1, g1, g2, rw, nw)


def _pad_lanes(a):
    return jnp.pad(a, ((0, 0), (0, LANES - a.shape[1])))


def kernel(x, norm_mix_w, w_in, conv_w, A_log, dt_bias, gdn_norm_w, w_up_gdn, w_up_ret, w_out,
           norm_ffn_w, w_group, b_group, w_expert, b_expert, w_gate, w_up, w_down, norm_final_w):
    batch, seq, d = x.shape
    m = batch * seq
    h = x.reshape(m, d)
    depth = w_in.shape[0]
    sin, cos, inner, kdec, qdec, cdec = _retention_tables(seq)
    for l in range(depth):
        o_ab = 3 * GDN_QK
        o_z = o_ab + 2 * GDN_HEADS
        wl = w_in[l]
        w_main = jnp.concatenate([wl[:, :o_ab], wl[:, o_z:]], axis=1).astype(BF16)
        w_ab = _pad_lanes(wl[:, o_ab:o_z]).astype(BF16)
        proj, ab = _proj_call(h, norm_mix_w[l][None, :], w_main, w_ab)

        conv8 = jnp.pad(conv_w[l], ((0, 8 - GDN_CONV), (0, 0)))
        ya = _gdn_call(proj, ab, conv8, _pad_lanes(A_log[l][None, :]), _pad_lanes(dt_bias[l][None, :]),
                       gdn_norm_w[l][None, :], batch, seq)
        yb = _ret_call(proj, sin, cos, inner, kdec, qdec, cdec, batch, seq)

        w_router = _pad_lanes(jnp.concatenate([w_expert[l], w_group[l]], axis=1))
        wr_hi = w_router.astype(BF16)
        wr_lo = (w_router - wr_hi.astype(F32)).astype(BF16)
        b_router = _pad_lanes(jnp.concatenate([b_expert[l], b_group[l]])[None, :])
        h1, xn, ridx, rw, cnt = _merge_call(
            h, ya, yb, proj, w_up_gdn[l].astype(BF16), w_up_ret[l].astype(BF16), w_out[l].astype(BF16),
            norm_ffn_w[l][None, :], jnp.concatenate([wr_hi, wr_lo], axis=1), b_router)

        n_slots, n_tiles = _slot_counts(m)
        slots, tiles = _plan_call(ridx, cnt)
        slot1, slot2 = slots[0], slots[1]
        xs = _sc_dispatch(xn, slot1, slot2, n_slots)
        ys = _expert_call(tiles[:n_tiles, 0], tiles[:n_tiles, 1], xs,
                          w_gate[l], w_up[l], w_down[l], n_tiles)
        g1, g2 = _sc_collect(ys, slot1, slot2)

        assert depth == 1
        h = _final_call(h1, g1, g2, rw, norm_final_w[None, :])
    return h.reshape(batch, seq, d)
```

```python
import functools
import math

import jax
import jax.numpy as jnp
from jax import lax
from jax.experimental import pallas as pl
from jax.experimental.pallas import tpu as pltpu
from jax.experimental.pallas import tpu_sc as plsc

F32 = jnp.float32
BF16 = jnp.bfloat16
I32 = jnp.int32
U32 = jnp.uint32

D_MODEL = 1024
EPS = 1e-6
GDN_HEADS = 4
GDN_DK = 128
GDN_DV = 128
GDN_CONV = 4
RET_HEADS = 4
RET_DK = 128
RET_DV = 256
ROPE_BASE = 10000.0
N_GROUPS = 4
EXPERTS_PER_GROUP = 8
N_EXPERTS = N_GROUPS * EXPERTS_PER_GROUP
D_EXPERT = 512

GDN_QK = GDN_HEADS * GDN_DK
GDN_V = GDN_HEADS * GDN_DV
RET_QK = RET_HEADS * RET_DK
RET_V = RET_HEADS * RET_DV

LANES = 128
CHUNK = 128
INV_BLOCK = 16
GDN_PREP_CHUNKS = 2
VMEM_LIMIT = 56 * 1024 * 1024

MERGE_PARTS = 1
TOP_K = 2
SLOT_TILE = 256
PLAN_TILE = 256
PLAN_STEP = 1024
HALF = D_MODEL // 2
SC_WORKERS = 32
SC_ROWS = 64

PROJ_COLS = 3 * GDN_QK + GDN_V + 2 * RET_QK + 2 * RET_V + 2 * D_MODEL


def _silu(x):
    return x / (1.0 + jnp.exp(-x))


def _sigmoid(x):
    return 1.0 / (1.0 + jnp.exp(-x))


def _dot(a, b):
    return jnp.dot(a, b, preferred_element_type=F32)


def _dot_nt(a, b):
    return lax.dot_general(a, b, (((1,), (1,)), ((), ())), preferred_element_type=F32)


def _pack_bf16_pairs(x):
    bits = lax.bitcast_convert_type(x.astype(BF16).astype(F32), U32)
    packed = (bits[:, :HALF] >> 16) | (bits[:, HALF:] & jnp.uint32(0xFFFF0000))
    return lax.bitcast_convert_type(packed, I32)


def _unpack_bf16_pairs(p):
    p = lax.bitcast_convert_type(p, U32)
    lo = lax.bitcast_convert_type(p << 16, F32)
    hi = lax.bitcast_convert_type(p & jnp.uint32(0xFFFF0000), F32)
    return jnp.concatenate([lo, hi], axis=1)


def _proj_kernel(x_ref, nw_ref, w_ref, wab_ref, proj_ref, ab_ref, u_ref):
    j = pl.program_id(1)

    @pl.when(j == 0)
    def _():
        x = x_ref[...]
        u = x * lax.rsqrt(jnp.mean(x * x, axis=-1, keepdims=True) + EPS) * nw_ref[...]
        ub = u.astype(BF16)
        u_ref[...] = ub
        ab_ref[...] = _dot(ub, wab_ref[...])

    proj_ref[...] = _dot(u_ref[...], w_ref[...]).astype(BF16)


def _proj_call(x2, norm_w, w_main, w_ab, tm=1024, tn=3584):
    m = x2.shape[0]
    return pl.pallas_call(
        _proj_kernel,
        grid=(m // tm, PROJ_COLS // tn),
        in_specs=[
            pl.BlockSpec((tm, D_MODEL), lambda i, j: (i, 0)),
            pl.BlockSpec((1, D_MODEL), lambda i, j: (0, 0)),
            pl.BlockSpec((D_MODEL, tn), lambda i, j: (0, j)),
            pl.BlockSpec((D_MODEL, LANES), lambda i, j: (0, 0)),
        ],
        out_specs=[
            pl.BlockSpec((tm, tn), lambda i, j: (i, j)),
            pl.BlockSpec((tm, LANES), lambda i, j: (i, 0)),
        ],
        out_shape=[
            jax.ShapeDtypeStruct((m, PROJ_COLS), BF16),
            jax.ShapeDtypeStruct((m, LANES), F32),
        ],
        scratch_shapes=[pltpu.VMEM((tm, D_MODEL), BF16)],
        compiler_params=pltpu.CompilerParams(
            dimension_semantics=("arbitrary", "arbitrary"), vmem_limit_bytes=VMEM_LIMIT),
        name="proj",
    )(x2, norm_w, w_main, w_ab)


def _unit_lower_inverses(lows, ii, jj):
    eye = jnp.where(ii == jj, 1.0, 0.0).astype(F32)
    in_block = (ii // INV_BLOCK) == (jj // INV_BLOCK)
    ps = [jnp.where(in_block, -low, 0.0) for low in lows]
    ts = [eye + p for p in ps]
    span = 2
    while span < INV_BLOCK:
        ps = [_dot(p, p) for p in ps]
        ts = [t + _dot(t, p) for t, p in zip(ts, ps)]
        span *= 2
    s = INV_BLOCK
    while s < CHUNK:
        off_diag = ((ii // (2 * s)) == (jj // (2 * s))) & ((ii // s) != (jj // s))
        xs = [_dot(jnp.where(off_diag, low, 0.0), t) for low, t in zip(lows, ts)]
        ts = [t - _dot(t, x) for t, x in zip(ts, xs)]
        s *= 2
    return ts


def _gdn_kernel(qkv_ref, z_ref, ab_ref, convw_ref, alog_ref, dtb_ref, normw_ref, o_ref,
                b_s, o0_s, m_s, qp_s, gl_s, state_s, q_s, k_s, kb_s, rhs_s, dec_s, cv_s, qg_o, kdt_o,
                *, seq):
    nchunk = seq // CHUNK
    ii = lax.broadcasted_iota(jnp.int32, (CHUNK, CHUNK), 0)
    jj = lax.broadcasted_iota(jnp.int32, (CHUNK, CHUNK), 1)
    causal = ii >= jj
    strict = ii > jj
    tri = jnp.where(causal, 1.0, 0.0).astype(F32)
    neg_a = -jnp.exp(alog_ref[...])
    dtb = dtb_ref[...]

    def conv_cols(c, r0, lo, buf):
        x = qkv_ref[pl.ds(r0, CHUNK), lo:lo + LANES].astype(F32)
        prev0 = pl.multiple_of(jnp.maximum(r0 - 16, 0), 16)
        prev = qkv_ref[pl.ds(prev0, 16), lo:lo + LANES].astype(F32)
        buf[0:8, :] = prev[8:16] * jnp.where(c > 0, 1.0, 0.0)
        buf[8:8 + CHUNK, :] = x
        w = convw_ref[:, lo:lo + LANES]
        y = (w[3:4] * x + w[2:3] * buf[7:7 + CHUNK, :] + w[1:2] * buf[6:6 + CHUNK, :]
             + w[0:1] * buf[5:5 + CHUNK, :])
        return _silu(y)

    def l2n(x):
        return x * lax.rsqrt(jnp.sum(x * x, axis=-1, keepdims=True) + EPS)

    tri_b = tri.astype(BF16)

    def chunk_cumsum(g):
        g1 = g.astype(BF16)
        r1 = g - g1.astype(F32)
        g2 = r1.astype(BF16)
        g3 = (r1 - g2.astype(F32)).astype(BF16)
        return _dot(tri_b, g1) + (_dot(tri_b, g2) + _dot(tri_b, g3))

    def operands(cc, slot):
        for sub in range(GDN_PREP_CHUNKS):
            c = cc * GDN_PREP_CHUNKS + sub
            r0 = pl.multiple_of(c * CHUNK, CHUNK)
            ab = ab_ref[pl.ds(r0, CHUNK), :]
            xg = ab + dtb
            softplus = jnp.maximum(xg, 0.0) + jnp.log(1.0 + jnp.exp(-jnp.abs(xg)))
            g_all = neg_a * softplus
            beta_all = _sigmoid(ab)
            gc_all = chunk_cumsum(g_all)
            gc_t = gc_all.T
            gl_s[c] = jnp.exp(gc_all[CHUNK - 1:CHUNK, :])
            for h in range(GDN_HEADS):
                n = sub * GDN_HEADS + h
                bufs = [cv_s.at[slot, 3 * n + i] for i in range(3)]
                q = l2n(conv_cols(c, r0, h * GDN_DK, bufs[0])) * (GDN_DK ** -0.5)
                k = l2n(conv_cols(c, r0, GDN_QK + h * GDN_DK, bufs[1]))
                v = conv_cols(c, r0, 2 * GDN_QK + h * GDN_DV, bufs[2])
                gcol = gc_all[:, h:h + 1]
                grow = gc_t[h:h + 1, :]
                beta = beta_all[:, GDN_HEADS + h:GDN_HEADS + h + 1]
                dec_s[slot, n] = jnp.where(causal, jnp.exp(gcol - grow), 0.0)
                eg = jnp.exp(gcol)
                kb = k * beta
                q_s[slot, n] = q.astype(BF16)
                k_s[slot, n] = k.astype(BF16)
                kb_s[slot, n] = kb.astype(BF16)
                rhs_s[slot, n] = jnp.concatenate([v * beta, kb * eg], axis=1).astype(BF16)
                qg_o[slot, n] = q * eg
                kd = k * jnp.exp(gc_all[CHUNK - 1:CHUNK, h:h + 1] - gcol)
                kdt_o[slot, n] = kd.T.astype(BF16)

    def solve(cc, slot):
        items = [(cc * GDN_PREP_CHUNKS + sub, h, sub * GDN_HEADS + h)
                 for sub in range(GDN_PREP_CHUNKS) for h in range(GDN_HEADS)]
        kks = [_dot_nt(kb_s[slot, n], k_s[slot, n]) for _, _, n in items]
        qks = [_dot_nt(q_s[slot, n], k_s[slot, n]) for _, _, n in items]
        lows = [jnp.where(strict, kk * dec_s[slot, n], 0.0) for kk, (_, _, n) in zip(kks, items)]
        attns = [(qk * dec_s[slot, n]).astype(BF16) for qk, (_, _, n) in zip(qks, items)]
        ts = _unit_lower_inverses(lows, ii, jj)
        uws = [_dot(t.astype(BF16), rhs_s[slot, n]).astype(BF16) for t, (_, _, n) in zip(ts, items)]
        kds = [_dot(kdt_o[slot, n], uw) for uw, (_, _, n) in zip(uws, items)]
        ats = [_dot(attn, uw) for attn, uw in zip(attns, uws)]
        for kd_uw, at_uw, (c, h, n) in zip(kds, ats, items):
            b_s[c, h] = kd_uw[:, :GDN_DV]
            m_s[c, h] = (-kd_uw[:, GDN_DV:]).astype(BF16)
            o0_s[c, h] = at_uw[:, :GDN_DV]
            qp_s[c, h] = (qg_o[slot, n] - at_uw[:, GDN_DV:]).astype(BF16)

    ngroup = nchunk // GDN_PREP_CHUNKS
    operands(0, 0)

    def prep(i, carry):
        operands(2 * i + 1, 1)
        solve(2 * i, 0)
        operands(jnp.minimum(2 * i + 2, ngroup - 1), 0)
        solve(2 * i + 1, 1)
        return carry

    lax.fori_loop(0, ngroup // 2, prep, 0)

    state_s[...] = jnp.zeros_like(state_s)
    normw = normw_ref[...]

    def scan(c, carry):
        r0 = pl.multiple_of(c * CHUNK, CHUNK)
        gl = gl_s[c]
        heads = range(GDN_HEADS)
        ss = [state_s[h] for h in heads]
        sbs = [s.astype(BF16) for s in ss]
        mss = [_dot(m_s[c, h], sbs[h]) for h in heads]
        qss = [_dot(qp_s[c, h], sbs[h]) for h in heads]
        for h in heads:
            state_s[h] = ss[h] * gl[:, h:h + 1] + (mss[h] + b_s[c, h])
            o = qss[h] + o0_s[c, h]
            z = z_ref[pl.ds(r0, CHUNK), h * GDN_DV:(h + 1) * GDN_DV].astype(F32)
            on = o * lax.rsqrt(jnp.mean(o * o, axis=-1, keepdims=True) + EPS) * normw
            o_ref[pl.ds(r0, CHUNK), h * GDN_DV:(h + 1) * GDN_DV] = (on * _silu(z)).astype(BF16)
        return carry

    lax.fori_loop(0, nchunk, scan, 0)


def _gdn_call(proj, ab, conv_w8, alog_p, dtb_p, normw, batch, seq):
    nchunk = seq // CHUNK
    hs = (nchunk, GDN_HEADS, CHUNK, CHUNK)
    ops = (2, GDN_PREP_CHUNKS * GDN_HEADS, CHUNK, CHUNK)
    return pl.pallas_call(
        functools.partial(_gdn_kernel, seq=seq),
        grid=(batch,),
        in_specs=[
            pl.BlockSpec((seq, 3 * GDN_QK), lambda b: (b, 0)),
            pl.BlockSpec((seq, GDN_V), lambda b: (b, 3)),
            pl.BlockSpec((seq, LANES), lambda b: (b, 0)),
            pl.BlockSpec((8, 3 * GDN_QK), lambda b: (0, 0)),
            pl.BlockSpec((1, LANES), lambda b: (0, 0)),
            pl.BlockSpec((1, LANES), lambda b: (0, 0)),
            pl.BlockSpec((1, GDN_DV), lambda b: (0, 0)),
        ],
        out_specs=pl.BlockSpec((seq, GDN_V), lambda b: (b, 0)),
        out_shape=jax.ShapeDtypeStruct((batch * seq, GDN_V), BF16),
        scratch_shapes=[
            pltpu.VMEM(hs, F32), pltpu.VMEM(hs, F32), pltpu.VMEM(hs, BF16), pltpu.VMEM(hs, BF16),
            pltpu.VMEM((nchunk, 1, LANES), F32),
            pltpu.VMEM((GDN_HEADS, GDN_DK, GDN_DV), F32),
            pltpu.VMEM(ops, BF16), pltpu.VMEM(ops, BF16), pltpu.VMEM(ops, BF16),
            pltpu.VMEM(ops[:3] + (2 * CHUNK,), BF16),
            pltpu.VMEM(ops, F32),
            pltpu.VMEM((2, 3 * ops[1], 8 + CHUNK, LANES), F32),
            pltpu.VMEM(ops, F32), pltpu.VMEM(ops, BF16),
        ],
        compiler_params=pltpu.CompilerParams(
            dimension_semantics=("arbitrary",), vmem_limit_bytes=VMEM_LIMIT),
        name="gdn",
    )(proj, proj, ab, conv_w8, alog_p, dtb_p, normw)


def _ret_kernel(qk_ref, v_ref, g_ref, sin_ref, cos_ref, inner_ref, kdec_ref, qdec_ref, cdec_ref,
                o_ref, state_s, q_s, qd_s, k_s, kt_s, *, seq):
    nchunk = seq // CHUNK
    lane = lax.broadcasted_iota(jnp.int32, (CHUNK, RET_DK), 1)
    even = (lane % 2) == 0

    def rotate(x, sin, cos):
        nxt = pltpu.roll(x, RET_DK - 1, axis=1)
        prv = pltpu.roll(x, 1, axis=1)
        return x * cos + jnp.where(even, -nxt, prv) * sin

    state_s[...] = jnp.zeros_like(state_s)
    kdec = kdec_ref[...]
    qdec = qdec_ref[...]
    cdec = cdec_ref[...]

    heads = range(RET_HEADS)

    def operands(c, slot):
        r0 = pl.multiple_of(c * CHUNK, CHUNK)
        sin = sin_ref[pl.ds(r0, CHUNK), :]
        cos = cos_ref[pl.ds(r0, CHUNK), :]
        for h in heads:
            q = rotate(qk_ref[pl.ds(r0, CHUNK), h * RET_DK:(h + 1) * RET_DK].astype(F32), sin, cos)
            k = rotate(qk_ref[pl.ds(r0, CHUNK), RET_QK + h * RET_DK:RET_QK + (h + 1) * RET_DK]
                       .astype(F32), sin, cos) * (RET_DK ** -0.5)
            q_s[slot, h] = q.astype(BF16)
            qd_s[slot, h] = (q * qdec[:, h:h + 1]).astype(BF16)
            k_s[slot, h] = k.astype(BF16)
            kt_s[slot, h] = (k * kdec[:, h:h + 1]).T.astype(BF16)

    def outputs(c, slot):
        r0 = pl.multiple_of(c * CHUNK, CHUNK)
        vs = [v_ref[pl.ds(r0, CHUNK), h * RET_DV:(h + 1) * RET_DV] for h in heads]
        ss = [state_s[h] for h in heads]
        qks = [_dot_nt(q_s[slot, h], k_s[slot, h]) for h in heads]
        inters = [_dot(qd_s[slot, h], ss[h].astype(BF16)) for h in heads]
        kvs = [_dot(kt_s[slot, h], vs[h]) for h in heads]
        intras = [_dot((qks[h] * inner_ref[h]).astype(BF16), vs[h]) for h in heads]
        for h in heads:
            state_s[h] = ss[h] * cdec[:, h:h + 1] + kvs[h]
            o = intras[h] + inters[h]
            gate = g_ref[pl.ds(r0, CHUNK), h * RET_DV:(h + 1) * RET_DV].astype(F32)
            on = o * lax.rsqrt(jnp.mean(o * o, axis=-1, keepdims=True) + EPS)
            o_ref[pl.ds(r0, CHUNK), h * RET_DV:(h + 1) * RET_DV] = (on * _silu(gate)).astype(BF16)

    operands(0, 0)

    def body(i, carry):
        operands(2 * i + 1, 1)
        outputs(2 * i, 0)
        operands(jnp.minimum(2 * i + 2, nchunk - 1), 0)
        outputs(2 * i + 1, 1)
        return carry

    lax.fori_loop(0, nchunk // 2, body, 0)


def _ret_call(proj, sin, cos, inner, kdec, qdec, cdec, batch, seq):
    return pl.pallas_call(
        functools.partial(_ret_kernel, seq=seq),
        grid=(batch,),
        in_specs=[
            pl.BlockSpec((seq, 2 * RET_QK), lambda b: (b, 2)),
            pl.BlockSpec((seq, RET_V), lambda b: (b, 3)),
            pl.BlockSpec((seq, RET_V), lambda b: (b, 4)),
            pl.BlockSpec((seq, RET_DK), lambda b: (0, 0)),
            pl.BlockSpec((seq, RET_DK), lambda b: (0, 0)),
            pl.BlockSpec((RET_HEADS, CHUNK, CHUNK), lambda b: (0, 0, 0)),
            pl.BlockSpec((CHUNK, LANES), lambda b: (0, 0)),
            pl.BlockSpec((CHUNK, LANES), lambda b: (0, 0)),
            pl.BlockSpec((1, LANES), lambda b: (0, 0)),
        ],
        out_specs=pl.BlockSpec((seq, RET_V), lambda b: (b, 0)),
        out_shape=jax.ShapeDtypeStruct((batch * seq, RET_V), BF16),
        scratch_shapes=[pltpu.VMEM((RET_HEADS, RET_DK, RET_DV), F32)]
        + [pltpu.VMEM((2, RET_HEADS, CHUNK, RET_DK), BF16)] * 4,
        compiler_params=pltpu.CompilerParams(
            dimension_semantics=("arbitrary",), vmem_limit_bytes=VMEM_LIMIT),
        name="retention",
    )(proj, proj, proj, sin, cos, inner, kdec, qdec, cdec)


def _retention_tables(seq):
    inv_freq = 1.0 / (ROPE_BASE ** jnp.linspace(0.0, 1.0, RET_DK // 2, dtype=F32))
    ang = jnp.arange(seq, dtype=F32)[:, None] * inv_freq[None, :]
    sin = jnp.repeat(jnp.sin(ang), 2, axis=-1)
    cos = jnp.repeat(jnp.cos(ang), 2, axis=-1)
    log_gamma = jnp.log(1.0 - 2.0 ** (-5.0 - jnp.arange(RET_HEADS, dtype=F32)))
    idx = jnp.arange(CHUNK, dtype=F32)
    causal = jnp.tril(jnp.ones((CHUNK, CHUNK), dtype=bool))
    rel = jnp.where(causal, idx[:, None] - idx[None, :], 0.0)
    inner = jnp.where(causal, jnp.exp(rel[None] * log_gamma[:, None, None]), 0.0)
    k_decay = jnp.exp(log_gamma[:, None] * (CHUNK - 1.0 - idx)[None, :])
    q_decay = jnp.exp(log_gamma[:, None] * (idx + 1.0)[None, :])
    chunk_decay = jnp.exp(log_gamma * CHUNK)
    pad = LANES - RET_HEADS
    kdec = jnp.pad(k_decay.T, ((0, 0), (0, pad)))
    qdec = jnp.pad(q_decay.T, ((0, 0), (0, pad)))
    cdec = jnp.pad(chunk_decay[None, :], ((0, 0), (0, pad)))
    return sin, cos, inner, kdec, qdec, cdec


def _merge_kernel(x_ref, ya_ref, yb_ref, ma_ref, mb_ref, wa_ref, wr_ref, wo_ref, nw_ref,
                  wrt_ref, br_ref, h_ref, xn_ref, ridx_ref, rw_ref, cnt_ref):
    tm = x_ref.shape[0] // MERGE_PARTS
    rows = [pl.ds(i * tm, tm) for i in range(MERGE_PARTS)]
    a_ = [_dot(ya_ref[r, :], wa_ref[...]) for r in rows]
    r_ = [_dot(yb_ref[r, :], wr_ref[...]) for r in rows]
    merged = [(_sigmoid(ma_ref[r, :].astype(F32)) * a + _sigmoid(mb_ref[r, :].astype(F32)) * rr)
              .astype(BF16) for r, a, rr in zip(rows, a_, r_)]
    hs = [x_ref[r, :] + _dot(mg, wo_ref[...]) for r, mg in zip(rows, merged)]
    xcats = []
    for r, h in zip(rows, hs):
        h_ref[r, :] = h
        xn = h * lax.rsqrt(jnp.mean(h * h, axis=-1, keepdims=True) + EPS) * nw_ref[...]
        xn_ref[r, :] = _pack_bf16_pairs(xn)
        xh = xn.astype(BF16)
        xl = (xn - xh.astype(F32)).astype(BF16)
        xcats.append(jnp.concatenate([xh, xl], axis=0))
    parts_ = [_dot(xc, wrt_ref[...]) for xc in xcats]
    counts = jnp.zeros((1, LANES), F32)
    for r, parts in zip(rows, parts_):
        counts = counts + _route(parts, tm, br_ref[...], ridx_ref.at[r, :], rw_ref.at[r, :])

    @pl.when(pl.program_id(0) == 0)
    def _():
        cnt_ref[...] = jnp.zeros_like(cnt_ref)

    cnt_ref[...] += jnp.broadcast_to(counts, cnt_ref.shape)


def _route(parts, tm, bias, ridx_ref, rw_ref):
    logits = (parts[:tm, :LANES] + (parts[tm:, :LANES] + parts[:tm, LANES:]
                                    + parts[tm:, LANES:])) + bias
    lane = lax.broadcasted_iota(jnp.int32, (tm, LANES), 1)
    neg = -jnp.inf
    gl = jnp.where((lane >= N_EXPERTS) & (lane < N_EXPERTS + N_GROUPS), logits, neg)
    gmax = jnp.max(gl, axis=-1, keepdims=True)
    gidx = jnp.min(jnp.where(gl == gmax, lane, LANES), axis=-1, keepdims=True) - N_EXPERTS
    g_w = 1.0 / jnp.sum(jnp.exp(gl - gmax), axis=-1, keepdims=True)
    el = jnp.where((lane // EXPERTS_PER_GROUP == gidx) & (lane < N_EXPERTS), logits, neg)
    m1 = jnp.max(el, axis=-1, keepdims=True)
    i1 = jnp.min(jnp.where(el == m1, lane, LANES), axis=-1, keepdims=True)
    el2 = jnp.where(lane == i1, neg, el)
    m2 = jnp.max(el2, axis=-1, keepdims=True)
    i2 = jnp.min(jnp.where(el2 == m2, lane, LANES), axis=-1, keepdims=True)
    e2 = jnp.exp(m2 - m1)
    p1 = g_w / (1.0 + e2)
    p2 = g_w * e2 / (1.0 + e2)
    ridx_ref[...] = jnp.where(lane == 0, i1, jnp.where(lane == 1, i2, 0))
    rw_ref[...] = jnp.where(lane == 0, p1, jnp.where(lane == 1, p2, 0.0))
    onehot = jnp.where((lane == i1) | (lane == i2), 1.0, 0.0)
    return jnp.sum(onehot, axis=0, keepdims=True)


def _merge_call(x2, ya, yb, proj, wa, wr, wo, nw, w_router, b_router, tm=1024):
    m = x2.shape[0]
    full = lambda shape: pl.BlockSpec(shape, lambda i: (0, 0))
    return pl.pallas_call(
        _merge_kernel,
        grid=(m // tm,),
        in_specs=[
            pl.BlockSpec((tm, D_MODEL), lambda i: (i, 0)),
            pl.BlockSpec((tm, GDN_V), lambda i: (i, 0)),
            pl.BlockSpec((tm, RET_V), lambda i: (i, 0)),
            pl.BlockSpec((tm, D_MODEL), lambda i: (i, 5)),
            pl.BlockSpec((tm, D_MODEL), lambda i: (i, 6)),
            full((GDN_V, D_MODEL)), full((RET_V, D_MODEL)), full((D_MODEL, D_MODEL)),
            full((1, D_MODEL)),
            full((D_MODEL, 2 * LANES)), full((1, LANES)),
        ],
        out_specs=[
            pl.BlockSpec((tm, D_MODEL), lambda i: (i, 0)),
            pl.BlockSpec((tm, HALF), lambda i: (i, 0)),
            pl.BlockSpec((tm, LANES), lambda i: (i, 0)),
            pl.BlockSpec((tm, LANES), lambda i: (i, 0)),
            pl.BlockSpec((8, LANES), lambda i: (0, 0)),
        ],
        out_shape=[
            jax.ShapeDtypeStruct((m, D_MODEL), F32),
            jax.ShapeDtypeStruct((m, HALF), I32),
            jax.ShapeDtypeStruct((m, LANES), I32),
            jax.ShapeDtypeStruct((m, LANES), F32),
            jax.ShapeDtypeStruct((8, LANES), F32),
        ],
        compiler_params=pltpu.CompilerParams(
            dimension_semantics=("arbitrary",), vmem_limit_bytes=VMEM_LIMIT),
        name="merge_router",
    )(x2, ya, yb, proj, proj, wa, wr, wo, nw, w_router, b_router)


def _slot_counts(m):
    n_slots = TOP_K * m + N_EXPERTS * SLOT_TILE
    return n_slots, n_slots // SLOT_TILE


def _lane_prefix_sum(x, lane):
    s = 1
    while s < LANES:
        x = x + jnp.where(lane >= s, pltpu.roll(x, s, axis=1), 0.0)
        s *= 2
    return x


def _plan_kernel(ridx_ref, cnt_ref, slots_ref, tile_ref, carry_s, off_s):
    i = pl.program_id(0)
    lane = lax.broadcasted_iota(I32, (PLAN_TILE, LANES), 1)
    row = lax.broadcasted_iota(I32, (PLAN_TILE, LANES), 0)
    lane1 = lane[0:1]

    @pl.when(i == 0)
    def _():
        cnt = cnt_ref[0:1, :]
        tile = float(SLOT_TILE)
        padded = jnp.floor((cnt + (tile - 1.0)) / tile) * tile
        incl = _lane_prefix_sum(padded, lane1)
        off = incl - padded
        off_s[...] = off
        carry_s[...] = jnp.zeros_like(carry_s)
        first = (row * SLOT_TILE).astype(F32)
        ended = jnp.where((lane < N_EXPERTS) & (incl <= first), 1.0, 0.0)
        tile_e = jnp.sum(ended, axis=-1, keepdims=True)
        last = jnp.sum(jnp.where(lane.astype(F32) == tile_e, off + cnt, 0.0), axis=-1, keepdims=True)
        used = jnp.clip(last - first[:, 0:1], 0.0, tile)
        tile_ref[...] = jnp.where(lane == 0, tile_e, jnp.where(lane == 1, used, 0.0)).astype(I32)

    strict = jnp.where(row[:, 0:1] > lax.broadcasted_iota(I32, (PLAN_TILE, PLAN_TILE), 1),
                       1.0, 0.0).astype(BF16)
    off = off_s[...]
    carry = carry_s[...]
    for sb in range(PLAN_STEP // PLAN_TILE):
        rows = pl.ds(sb * PLAN_TILE, PLAN_TILE)
        e1 = ridx_ref[rows, 0:1]
        e2 = ridx_ref[rows, 1:2]
        onehot = jnp.where((lane == e1) | (lane == e2), 1.0, 0.0)
        pos = _dot(strict, onehot.astype(BF16)) + (carry + off)
        s1 = jnp.sum(jnp.where(lane == e1, pos, 0.0), axis=-1, keepdims=True)
        s2 = jnp.sum(jnp.where(lane == e2, pos, 0.0), axis=-1, keepdims=True)
        both = jnp.where(lane == 0, s1, jnp.where(lane == 1, s2, 0.0))
        for q in range(PLAN_TILE // LANES):
            t = both[q * LANES:(q + 1) * LANES].T
            c0 = sb * PLAN_TILE + q * LANES
            slots_ref[:, c0:c0 + LANES] = t[0:8].astype(I32)
        carry = carry + jnp.sum(onehot, axis=0, keepdims=True)
    carry_s[...] = carry


def _plan_call(ridx, cnt):
    m = ridx.shape[0]
    _, n_tiles = _slot_counts(m)
    assert n_tiles <= PLAN_TILE and SLOT_TILE == PLAN_TILE
    return pl.pallas_call(
        _plan_kernel,
        grid=(m // PLAN_STEP,),
        in_specs=[pl.BlockSpec((PLAN_STEP, LANES), lambda i: (i, 0)),
                  pl.BlockSpec((8, LANES), lambda i: (0, 0))],
        out_specs=[
            pl.BlockSpec((8, PLAN_STEP), lambda i: (0, i)),
            pl.BlockSpec((PLAN_TILE, LANES), lambda i: (0, 0)),
        ],
        out_shape=[
            jax.ShapeDtypeStruct((8, m), I32),
            jax.ShapeDtypeStruct((PLAN_TILE, LANES), I32),
        ],
        scratch_shapes=[pltpu.VMEM((1, LANES), F32), pltpu.VMEM((1, LANES), F32)],
        compiler_params=pltpu.CompilerParams(dimension_semantics=("arbitrary",)),
        name="dispatch_plan",
    )(ridx, cnt)


def _sc_mesh():
    return plsc.VectorSubcoreMesh(core_axis_name="c", subcore_axis_name="s")


def _sc_worker():
    return lax.axis_index("s") * 2 + lax.axis_index("c")


def _sc_dispatch(xn, slot1, slot2, n_rows):
    m = xn.shape[0]
    per = m // SC_WORKERS
    n_pairs = per // (2 * SC_ROWS)

    @functools.partial(
        pl.kernel, mesh=_sc_mesh(),
        out_type=jax.ShapeDtypeStruct((n_rows, HALF), I32),
        scratch_types=[pltpu.VMEM((per // SC_ROWS, SC_ROWS), I32), pltpu.VMEM((per // SC_ROWS, SC_ROWS), I32),
                       pltpu.VMEM((SC_ROWS, HALF), I32), pltpu.VMEM((SC_ROWS, HALF), I32),
                       pltpu.SemaphoreType.DMA, pltpu.SemaphoreType.DMA, pltpu.SemaphoreType.DMA],
        name="sc_dispatch")
    def k(x_hbm, s1_hbm, s2_hbm, o_hbm, i1_v, i2_v, rows0, rows1, sem_r0, sem_r1, sem_w):
        wid = _sc_worker()
        base = wid * per

        def read(chunk, rows_v, sem):
            return pltpu.make_async_copy(x_hbm.at[pl.ds(base + chunk * SC_ROWS, SC_ROWS)], rows_v, sem)

        def scatter(chunk, rows_v):
            c1 = pltpu.async_copy(rows_v, o_hbm.at[i1_v.at[chunk]], sem_w)
            c2 = pltpu.async_copy(rows_v, o_hbm.at[i2_v.at[chunk]], sem_w)
            c1.wait()
            c2.wait()

        read(0, rows0, sem_r0).start()
        pltpu.sync_copy(s1_hbm.at[pl.ds(wid * (per // SC_ROWS), per // SC_ROWS)], i1_v)
        pltpu.sync_copy(s2_hbm.at[pl.ds(wid * (per // SC_ROWS), per // SC_ROWS)], i2_v)

        @pl.loop(0, n_pairs)
        def _(i):
            read(2 * i, rows0, sem_r0).wait()
            read(2 * i + 1, rows1, sem_r1).start()
            scatter(2 * i, rows0)
            read(2 * i + 1, rows1, sem_r1).wait()

            @pl.when(i + 1 < n_pairs)
            def _():
                read(2 * i + 2, rows0, sem_r0).start()

            scatter(2 * i + 1, rows1)

    return k(xn, slot1, slot2)


def _sc_collect(ys, slot1, slot2):
    m = slot1.size
    per = m // SC_WORKERS
    row = jax.ShapeDtypeStruct((m, HALF), I32)

    @functools.partial(
        pl.kernel, mesh=_sc_mesh(), out_type=[row, row],
        scratch_types=[pltpu.VMEM((per // SC_ROWS, SC_ROWS), I32), pltpu.VMEM((per // SC_ROWS, SC_ROWS), I32),
                       pltpu.VMEM((SC_ROWS, HALF), I32), pltpu.VMEM((SC_ROWS, HALF), I32),
                       pltpu.SemaphoreType.DMA, pltpu.SemaphoreType.DMA],
        name="sc_collect")
    def k(y_hbm, s1_hbm, s2_hbm, g1_hbm, g2_hbm, i1_v, i2_v, rows1, rows2, sem_g, sem_w):
        wid = _sc_worker()
        base = wid * per
        pltpu.sync_copy(s1_hbm.at[pl.ds(wid * (per // SC_ROWS), per // SC_ROWS)], i1_v)
        pltpu.sync_copy(s2_hbm.at[pl.ds(wid * (per // SC_ROWS), per // SC_ROWS)], i2_v)

        @pl.loop(0, per // SC_ROWS)
        def _(ci):
            t0 = base + ci * SC_ROWS
            a1 = pltpu.async_copy(y_hbm.at[i1_v.at[ci]], rows1, sem_g)
            a2 = pltpu.async_copy(y_hbm.at[i2_v.at[ci]], rows2, sem_g)
            a1.wait()
            a2.wait()
            w1 = pltpu.async_copy(rows1, g1_hbm.at[pl.ds(t0, SC_ROWS)], sem_w)
            w2 = pltpu.async_copy(rows2, g2_hbm.at[pl.ds(t0, SC_ROWS)], sem_w)
            w1.wait()
            w2.wait()

    return k(ys, slot1, slot2)


def _expert_kernel(te_ref, used_ref, xs_ref, wg_hbm, wu_hbm, wd_hbm, ys_ref, wg_b, wu_b, wd_b,
                   wg_f, wu_f, wd_f, sem, slot_s):
    j = pl.program_id(0)
    n = pl.num_programs(0)
    e = te_ref[j]
    prev = te_ref[jnp.maximum(j - 1, 0)]
    valid = e < N_EXPERTS

    def weight_copies(expert, slot):
        return [pltpu.make_async_copy(hbm.at[expert], buf.at[slot], sem.at[slot, i])
                for i, (hbm, buf) in enumerate(((wg_hbm, wg_f), (wu_hbm, wu_f), (wd_hbm, wd_f)))]

    @pl.when((j == 0) & valid)
    def _():
        slot_s[0] = 0
        for c in weight_copies(e, 0):
            c.start()

    @pl.when(((j == 0) | (e != prev)) & valid)
    def _():
        slot = slot_s[0]
        for c in weight_copies(e, slot):
            c.wait()
        k = lax.while_loop(lambda k: (k < n) & (te_ref[jnp.minimum(k, n - 1)] == e),
                           lambda k: k + 1, j + 1)
        nxt = te_ref[jnp.minimum(k, n - 1)]

        @pl.when((k < n) & (nxt < N_EXPERTS))
        def _():
            for c in weight_copies(nxt, 1 - slot):
                c.start()

        wg_b[...] = wg_f[slot].astype(BF16)
        wu_b[...] = wu_f[slot].astype(BF16)
        wd_b[...] = wd_f[slot].astype(BF16)
        slot_s[0] = 1 - slot

    @pl.when(valid)
    def _():
        half = SLOT_TILE // 2
        rows = [pl.ds(i * half, half) for i in range(2)]
        row_id = lax.broadcasted_iota(I32, (half, HALF), 0)
        xs = [_unpack_bf16_pairs(jnp.where(row_id + i * half < used_ref[j], xs_ref[r, :], 0))
              .astype(BF16) for i, r in enumerate(rows)]
        gs = [_dot(x, wg_b[...]) for x in xs]
        us = [_dot(x, wu_b[...]) for x in xs]
        hids = [(_silu(g) * u).astype(BF16) for g, u in zip(gs, us)]
        ys = [_dot(hid, wd_b[...]) for hid in hids]
        for r, y in zip(rows, ys):
            ys_ref[r, :] = _pack_bf16_pairs(y)

    @pl.when(e >= N_EXPERTS)
    def _():
        ys_ref[...] = jnp.zeros_like(ys_ref)


def _expert_call(tile_expert, tile_used, xs, wg, wu, wd, n_tiles):
    hbm = pl.BlockSpec(memory_space=pl.ANY)
    return pl.pallas_call(
        _expert_kernel,
        grid_spec=pltpu.PrefetchScalarGridSpec(
            num_scalar_prefetch=2,
            grid=(n_tiles,),
            in_specs=[pl.BlockSpec((SLOT_TILE, HALF), lambda j, te, used: (j, 0)), hbm, hbm, hbm],
            out_specs=pl.BlockSpec((SLOT_TILE, HALF), lambda j, te, used: (j, 0)),
            scratch_shapes=[
                pltpu.VMEM((D_MODEL, D_EXPERT), BF16), pltpu.VMEM((D_MODEL, D_EXPERT), BF16),
                pltpu.VMEM((D_EXPERT, D_MODEL), BF16),
                pltpu.VMEM((2, D_MODEL, D_EXPERT), F32), pltpu.VMEM((2, D_MODEL, D_EXPERT), F32),
                pltpu.VMEM((2, D_EXPERT, D_MODEL), F32),
                pltpu.SemaphoreType.DMA((2, 3)),
                pltpu.SMEM((1,), I32),
            ],
        ),
        out_shape=jax.ShapeDtypeStruct((n_tiles * SLOT_TILE, HALF), I32),
        compiler_params=pltpu.CompilerParams(
            dimension_semantics=("arbitrary",), vmem_limit_bytes=VMEM_LIMIT),
        name="experts",
    )(tile_expert, tile_used, xs, wg, wu, wd)


def _final_kernel(h_ref, g1_ref, g2_ref, rw_ref, nw_ref, o_ref):
    rw = rw_ref[...]
    y = rw[:, 0:1] * _unpack_bf16_pairs(g1_ref[...]) + rw[:, 1:2] * _unpack_bf16_pairs(g2_ref[...])
    h = h_ref[...] + y
    o_ref[...] = h * lax.rsqrt(jnp.mean(h * h, axis=-1, keepdims=True) + EPS) * nw_ref[...]


def _final_call(h1, g1, g2, rw, nw, tm=512):
    m = h1.shape[0]
    return pl.pallas_call(
        _final_kernel,
        grid=(m // tm,),
        in_specs=[
            pl.BlockSpec((tm, D_MODEL), lambda i: (i, 0)),
            pl.BlockSpec((tm, HALF), lambda i: (i, 0)),
            pl.BlockSpec((tm, HALF), lambda i: (i, 0)),
            pl.BlockSpec((tm, LANES), lambda i: (i, 0)),
            pl.BlockSpec((1, D_MODEL), lambda i: (0, 0)),
        ],
        out_specs=pl.BlockSpec((tm, D_MODEL), lambda i: (i, 0)),
        out_shape=jax.ShapeDtypeStruct((m, D_MODEL), F32),
        compiler_params=pltpu.CompilerParams(dimension_semantics=("arbitrary",)),
        name="combine_final",
    )(h1, g1, g2, rw, nw)


def _pad_lanes(a):
    return jnp.pad(a, ((0, 0), (0, LANES - a.shape[1])))


def kernel(x, norm_mix_w, w_in, conv_w, A_log, dt_bias, gdn_norm_w, w_up_gdn, w_up_ret, w_out,
           norm_ffn_w, w_group, b_group, w_expert, b_expert, w_gate, w_up, w_down, norm_final_w):
    batch, seq, d = x.shape
    m = batch * seq
    h = x.reshape(m, d)
    depth = w_in.shape[0]
    sin, cos, inner, kdec, qdec, cdec = _retention_tables(seq)
    for l in range(depth):
        o_ab = 3 * GDN_QK
        o_z = o_ab + 2 * GDN_HEADS
        wl = w_in[l]
        w_main = jnp.concatenate([wl[:, :o_ab], wl[:, o_z:]], axis=1).astype(BF16)
        w_ab = _pad_lanes(wl[:, o_ab:o_z]).astype(BF16)
        proj, ab = _proj_call(h, norm_mix_w[l][None, :], w_main, w_ab)

        conv8 = jnp.pad(conv_w[l], ((0, 8 - GDN_CONV), (0, 0)))
        ya = _gdn_call(proj, ab, conv8, _pad_lanes(A_log[l][None, :]), _pad_lanes(dt_bias[l][None, :]),
                       gdn_norm_w[l][None, :], batch, seq)
        yb = _ret_call(proj, sin, cos, inner, kdec, qdec, cdec, batch, seq)

        w_router = _pad_lanes(jnp.concatenate([w_expert[l], w_group[l]], axis=1))
        wr_hi = w_router.astype(BF16)
        wr_lo = (w_router - wr_hi.astype(F32)).astype(BF16)
        b_router = _pad_lanes(jnp.concatenate([b_expert[l], b_group[l]])[None, :])
        h1, xn, ridx, rw, cnt = _merge_call(
            h, ya, yb, proj, w_up_gdn[l].astype(BF16), w_up_ret[l].astype(BF16), w_out[l].astype(BF16),
            norm_ffn_w[l][None, :], jnp.concatenate([wr_hi, wr_lo], axis=1), b_router)

        n_slots, n_tiles = _slot_counts(m)
        slots, tiles = _plan_call(ridx, cnt)
        slot1, slot2 = slots[0].reshape(-1, SC_ROWS), slots[1].reshape(-1, SC_ROWS)
        xs = _sc_dispatch(xn, slot1, slot2, n_slots)
        ys = _expert_call(tiles[:n_tiles, 0], tiles[:n_tiles, 1], xs,
                          w_gate[l], w_up[l], w_down[l], n_tiles)
        g1, g2 = _sc_collect(ys, slot1, slot2)

        assert depth == 1
        h = _final_call(h1, g1, g2, rw, norm_final_w[None, :])
    return h.reshape(batch, seq, d)
```

```python
import functools
import math

import jax
import jax.numpy as jnp
from jax import lax
from jax.experimental import pallas as pl
from jax.experimental.pallas import tpu as pltpu
from jax.experimental.pallas import tpu_sc as plsc

F32 = jnp.float32
BF16 = jnp.bfloat16
I32 = jnp.int32
U32 = jnp.uint32

D_MODEL = 1024
EPS = 1e-6
GDN_HEADS = 4
GDN_DK = 128
GDN_DV = 128
GDN_CONV = 4
RET_HEADS = 4
RET_DK = 128
RET_DV = 256
ROPE_BASE = 10000.0
N_GROUPS = 4
EXPERTS_PER_GROUP = 8
N_EXPERTS = N_GROUPS * EXPERTS_PER_GROUP
D_EXPERT = 512

GDN_QK = GDN_HEADS * GDN_DK
GDN_V = GDN_HEADS * GDN_DV
RET_QK = RET_HEADS * RET_DK
RET_V = RET_HEADS * RET_DV

LANES = 128
CHUNK = 128
INV_BLOCK = 16
GDN_PREP_CHUNKS = 2
VMEM_LIMIT = 56 * 1024 * 1024

MERGE_PARTS = 1
TOP_K = 2
SLOT_TILE = 256
PLAN_TILE = 256
PLAN_STEP = 1024
HALF = D_MODEL // 2
SC_WORKERS = 32
SC_ROWS = 64

PROJ_COLS = 3 * GDN_QK + GDN_V + 2 * RET_QK + 2 * RET_V + 2 * D_MODEL


def _silu(x):
    return x / (1.0 + jnp.exp(-x))


def _sigmoid(x):
    return 1.0 / (1.0 + jnp.exp(-x))


def _dot(a, b):
    return jnp.dot(a, b, preferred_element_type=F32)


def _dot_nt(a, b):
    return lax.dot_general(a, b, (((1,), (1,)), ((), ())), preferred_element_type=F32)


def _pack_bf16_pairs(x):
    bits = lax.bitcast_convert_type(x.astype(BF16).astype(F32), U32)
    packed = (bits[:, :HALF] >> 16) | (bits[:, HALF:] & jnp.uint32(0xFFFF0000))
    return lax.bitcast_convert_type(packed, I32)


def _unpack_bf16_pairs(p):
    p = lax.bitcast_convert_type(p, U32)
    lo = lax.bitcast_convert_type(p << 16, F32)
    hi = lax.bitcast_convert_type(p & jnp.uint32(0xFFFF0000), F32)
    return jnp.concatenate([lo, hi], axis=1)


def _proj_kernel(x_ref, nw_ref, w_ref, wab_ref, proj_ref, ab_ref, u_ref):
    j = pl.program_id(1)

    @pl.when(j == 0)
    def _():
        x = x_ref[...]
        u = x * lax.rsqrt(jnp.mean(x * x, axis=-1, keepdims=True) + EPS) * nw_ref[...]
        ub = u.astype(BF16)
        u_ref[...] = ub
        ab_ref[...] = _dot(ub, wab_ref[...])

    proj_ref[...] = _dot(u_ref[...], w_ref[...]).astype(BF16)


def _proj_call(x2, norm_w, w_main, w_ab, tm=1024, tn=3584):
    m = x2.shape[0]
    return pl.pallas_call(
        _proj_kernel,
        grid=(m // tm, PROJ_COLS // tn),
        in_specs=[
            pl.BlockSpec((tm, D_MODEL), lambda i, j: (i, 0)),
            pl.BlockSpec((1, D_MODEL), lambda i, j: (0, 0)),
            pl.BlockSpec((D_MODEL, tn), lambda i, j: (0, j)),
            pl.BlockSpec((D_MODEL, LANES), lambda i, j: (0, 0)),
        ],
        out_specs=[
            pl.BlockSpec((tm, tn), lambda i, j: (i, j)),
            pl.BlockSpec((tm, LANES), lambda i, j: (i, 0)),
        ],
        out_shape=[
            jax.ShapeDtypeStruct((m, PROJ_COLS), BF16),
            jax.ShapeDtypeStruct((m, LANES), F32),
        ],
        scratch_shapes=[pltpu.VMEM((tm, D_MODEL), BF16)],
        compiler_params=pltpu.CompilerParams(
            dimension_semantics=("arbitrary", "arbitrary"), vmem_limit_bytes=VMEM_LIMIT),
        name="proj",
    )(x2, norm_w, w_main, w_ab)


def _unit_lower_inverses(lows, ii, jj):
    eye = jnp.where(ii == jj, 1.0, 0.0).astype(F32)
    in_block = (ii // INV_BLOCK) == (jj // INV_BLOCK)
    ps = [jnp.where(in_block, -low, 0.0) for low in lows]
    ts = [eye + p for p in ps]
    span = 2
    while span < INV_BLOCK:
        ps = [_dot(p, p) for p in ps]
        ts = [t + _dot(t, p) for t, p in zip(ts, ps)]
        span *= 2
    s = INV_BLOCK
    while s < CHUNK:
        off_diag = ((ii // (2 * s)) == (jj // (2 * s))) & ((ii // s) != (jj // s))
        xs = [_dot(jnp.where(off_diag, low, 0.0), t) for low, t in zip(lows, ts)]
        ts = [t - _dot(t, x) for t, x in zip(ts, xs)]
        s *= 2
    return ts


def _gdn_kernel(qkv_ref, z_ref, ab_ref, convw_ref, alog_ref, dtb_ref, normw_ref, o_ref,
                b_s, o0_s, m_s, qp_s, gl_s, state_s, q_s, k_s, kb_s, rhs_s, dec_s, cv_s, qg_o, kdt_o,
                *, seq):
    nchunk = seq // CHUNK
    ii = lax.broadcasted_iota(jnp.int32, (CHUNK, CHUNK), 0)
    jj = lax.broadcasted_iota(jnp.int32, (CHUNK, CHUNK), 1)
    causal = ii >= jj
    strict = ii > jj
    tri = jnp.where(causal, 1.0, 0.0).astype(F32)
    neg_a = -jnp.exp(alog_ref[...])
    dtb = dtb_ref[...]

    def conv_cols(c, r0, lo, buf):
        x = qkv_ref[pl.ds(r0, CHUNK), lo:lo + LANES].astype(F32)
        prev0 = pl.multiple_of(jnp.maximum(r0 - 16, 0), 16)
        prev = qkv_ref[pl.ds(prev0, 16), lo:lo + LANES].astype(F32)
        buf[0:8, :] = prev[8:16] * jnp.where(c > 0, 1.0, 0.0)
        buf[8:8 + CHUNK, :] = x
        w = convw_ref[:, lo:lo + LANES]
        y = (w[3:4] * x + w[2:3] * buf[7:7 + CHUNK, :] + w[1:2] * buf[6:6 + CHUNK, :]
             + w[0:1] * buf[5:5 + CHUNK, :])
        return _silu(y)

    def l2n(x):
        return x * lax.rsqrt(jnp.sum(x * x, axis=-1, keepdims=True) + EPS)

    tri_b = tri.astype(BF16)

    def chunk_cumsum(g):
        g1 = g.astype(BF16)
        r1 = g - g1.astype(F32)
        g2 = r1.astype(BF16)
        g3 = (r1 - g2.astype(F32)).astype(BF16)
        return _dot(tri_b, g1) + (_dot(tri_b, g2) + _dot(tri_b, g3))

    def operands(cc, slot):
        for sub in range(GDN_PREP_CHUNKS):
            c = cc * GDN_PREP_CHUNKS + sub
            r0 = pl.multiple_of(c * CHUNK, CHUNK)
            ab = ab_ref[pl.ds(r0, CHUNK), :]
            xg = ab + dtb
            softplus = jnp.maximum(xg, 0.0) + jnp.log(1.0 + jnp.exp(-jnp.abs(xg)))
            g_all = neg_a * softplus
            beta_all = _sigmoid(ab)
            gc_all = chunk_cumsum(g_all)
            gc_t = gc_all.T
            gl_s[c] = jnp.exp(gc_all[CHUNK - 1:CHUNK, :])
            for h in range(GDN_HEADS):
                n = sub * GDN_HEADS + h
                bufs = [cv_s.at[slot, 3 * n + i] for i in range(3)]
                q = l2n(conv_cols(c, r0, h * GDN_DK, bufs[0])) * (GDN_DK ** -0.5)
                k = l2n(conv_cols(c, r0, GDN_QK + h * GDN_DK, bufs[1]))
                v = conv_cols(c, r0, 2 * GDN_QK + h * GDN_DV, bufs[2])
                gcol = gc_all[:, h:h + 1]
                grow = gc_t[h:h + 1, :]
                beta = beta_all[:, GDN_HEADS + h:GDN_HEADS + h + 1]
                dec_s[slot, n] = jnp.where(causal, jnp.exp(gcol - grow), 0.0)
                eg = jnp.exp(gcol)
                kb = k * beta
                q_s[slot, n] = q.astype(BF16)
                k_s[slot, n] = k.astype(BF16)
                kb_s[slot, n] = kb.astype(BF16)
                rhs_s[slot, n] = jnp.concatenate([v * beta, kb * eg], axis=1).astype(BF16)
                qg_o[slot, n] = q * eg
                kd = k * jnp.exp(gc_all[CHUNK - 1:CHUNK, h:h + 1] - gcol)
                kdt_o[slot, n] = kd.T.astype(BF16)

    def solve(cc, slot, between):
        items = [(cc * GDN_PREP_CHUNKS + sub, h, sub * GDN_HEADS + h)
                 for sub in range(GDN_PREP_CHUNKS) for h in range(GDN_HEADS)]
        kks = [_dot_nt(kb_s[slot, n], k_s[slot, n]) for _, _, n in items]
        qks = [_dot_nt(q_s[slot, n], k_s[slot, n]) for _, _, n in items]
        lows = [jnp.where(strict, kk * dec_s[slot, n], 0.0) for kk, (_, _, n) in zip(kks, items)]
        attns = [(qk * dec_s[slot, n]).astype(BF16) for qk, (_, _, n) in zip(qks, items)]
        for step in between[:len(between) // 2]:
            step()
        ts = _unit_lower_inverses(lows, ii, jj)
        for step in between[len(between) // 2:]:
            step()
        uws = [_dot(t.astype(BF16), rhs_s[slot, n]).astype(BF16) for t, (_, _, n) in zip(ts, items)]
        kds = [_dot(kdt_o[slot, n], uw) for uw, (_, _, n) in zip(uws, items)]
        ats = [_dot(attn, uw) for attn, uw in zip(attns, uws)]
        for kd_uw, at_uw, (c, h, n) in zip(kds, ats, items):
            b_s[c, h] = kd_uw[:, :GDN_DV]
            m_s[c, h] = (-kd_uw[:, GDN_DV:]).astype(BF16)
            o0_s[c, h] = at_uw[:, :GDN_DV]
            qp_s[c, h] = (qg_o[slot, n] - at_uw[:, GDN_DV:]).astype(BF16)

    ngroup = nchunk // GDN_PREP_CHUNKS
    per_trip = 2 * GDN_PREP_CHUNKS
    normw = normw_ref[...]
    state_s[...] = jnp.zeros_like(state_s)
    for c0 in range(per_trip):
        gl_s[c0] = jnp.zeros((1, LANES), F32)
        for h in range(GDN_HEADS):
            b_s[c0, h] = jnp.zeros((CHUNK, GDN_DV), F32)
            o0_s[c0, h] = jnp.zeros((CHUNK, GDN_DV), F32)
            m_s[c0, h] = jnp.zeros((CHUNK, GDN_DK), BF16)
            qp_s[c0, h] = jnp.zeros((CHUNK, GDN_DK), BF16)
    operands(0, 0)

    def prep(i, carry):
        first = jnp.maximum(per_trip * (i - 1), 0)
        steps = [functools.partial(scan, first + k, 0) for k in range(per_trip)]
        operands(2 * i + 1, 1)
        solve(2 * i, 0, steps[:per_trip // 2])
        operands(jnp.minimum(2 * i + 2, ngroup - 1), 0)
        solve(2 * i + 1, 1, steps[per_trip // 2:])
        return carry

    def scan(c, carry):
        r0 = pl.multiple_of(c * CHUNK, CHUNK)
        gl = gl_s[c]
        heads = range(GDN_HEADS)
        ss = [state_s[h] for h in heads]
        sbs = [s.astype(BF16) for s in ss]
        mss = [_dot(m_s[c, h], sbs[h]) for h in heads]
        qss = [_dot(qp_s[c, h], sbs[h]) for h in heads]
        for h in heads:
            state_s[h] = ss[h] * gl[:, h:h + 1] + (mss[h] + b_s[c, h])
            o = qss[h] + o0_s[c, h]
            z = z_ref[pl.ds(r0, CHUNK), h * GDN_DV:(h + 1) * GDN_DV].astype(F32)
            on = o * lax.rsqrt(jnp.mean(o * o, axis=-1, keepdims=True) + EPS) * normw
            o_ref[pl.ds(r0, CHUNK), h * GDN_DV:(h + 1) * GDN_DV] = (on * _silu(z)).astype(BF16)
        return carry

    ntrip = ngroup // 2
    lax.fori_loop(0, ntrip, prep, 0)
    lax.fori_loop(per_trip * (ntrip - 1), nchunk, scan, 0)


def _gdn_call(proj, ab, conv_w8, alog_p, dtb_p, normw, batch, seq):
    nchunk = seq // CHUNK
    hs = (nchunk, GDN_HEADS, CHUNK, CHUNK)
    ops = (2, GDN_PREP_CHUNKS * GDN_HEADS, CHUNK, CHUNK)
    return pl.pallas_call(
        functools.partial(_gdn_kernel, seq=seq),
        grid=(batch,),
        in_specs=[
            pl.BlockSpec((seq, 3 * GDN_QK), lambda b: (b, 0)),
            pl.BlockSpec((seq, GDN_V), lambda b: (b, 3)),
            pl.BlockSpec((seq, LANES), lambda b: (b, 0)),
            pl.BlockSpec((8, 3 * GDN_QK), lambda b: (0, 0)),
            pl.BlockSpec((1, LANES), lambda b: (0, 0)),
            pl.BlockSpec((1, LANES), lambda b: (0, 0)),
            pl.BlockSpec((1, GDN_DV), lambda b: (0, 0)),
        ],
        out_specs=pl.BlockSpec((seq, GDN_V), lambda b: (b, 0)),
        out_shape=jax.ShapeDtypeStruct((batch * seq, GDN_V), BF16),
        scratch_shapes=[
            pltpu.VMEM(hs, F32), pltpu.VMEM(hs, F32), pltpu.VMEM(hs, BF16), pltpu.VMEM(hs, BF16),
            pltpu.VMEM((nchunk, 1, LANES), F32),
            pltpu.VMEM((GDN_HEADS, GDN_DK, GDN_DV), F32),
            pltpu.VMEM(ops, BF16), pltpu.VMEM(ops, BF16), pltpu.VMEM(ops, BF16),
            pltpu.VMEM(ops[:3] + (2 * CHUNK,), BF16),
            pltpu.VMEM(ops, F32),
            pltpu.VMEM((2, 3 * ops[1], 8 + CHUNK, LANES), F32),
            pltpu.VMEM(ops, F32), pltpu.VMEM(ops, BF16),
        ],
        compiler_params=pltpu.CompilerParams(
            dimension_semantics=("arbitrary",), vmem_limit_bytes=VMEM_LIMIT),
        name="gdn",
    )(proj, proj, ab, conv_w8, alog_p, dtb_p, normw)


def _ret_kernel(qk_ref, v_ref, g_ref, sin_ref, cos_ref, inner_ref, kdec_ref, qdec_ref, cdec_ref,
                o_ref, state_s, q_s, qd_s, k_s, kt_s, *, seq):
    nchunk = seq // CHUNK
    lane = lax.broadcasted_iota(jnp.int32, (CHUNK, RET_DK), 1)
    even = (lane % 2) == 0

    def rotate(x, sin, cos):
        nxt = pltpu.roll(x, RET_DK - 1, axis=1)
        prv = pltpu.roll(x, 1, axis=1)
        return x * cos + jnp.where(even, -nxt, prv) * sin

    state_s[...] = jnp.zeros_like(state_s)
    kdec = kdec_ref[...]
    qdec = qdec_ref[...]
    cdec = cdec_ref[...]

    heads = range(RET_HEADS)

    def operands(c, slot):
        r0 = pl.multiple_of(c * CHUNK, CHUNK)
        sin = sin_ref[pl.ds(r0, CHUNK), :]
        cos = cos_ref[pl.ds(r0, CHUNK), :]
        for h in heads:
            q = rotate(qk_ref[pl.ds(r0, CHUNK), h * RET_DK:(h + 1) * RET_DK].astype(F32), sin, cos)
            k = rotate(qk_ref[pl.ds(r0, CHUNK), RET_QK + h * RET_DK:RET_QK + (h + 1) * RET_DK]
                       .astype(F32), sin, cos) * (RET_DK ** -0.5)
            q_s[slot, h] = q.astype(BF16)
            qd_s[slot, h] = (q * qdec[:, h:h + 1]).astype(BF16)
            k_s[slot, h] = k.astype(BF16)
            kt_s[slot, h] = (k * kdec[:, h:h + 1]).T.astype(BF16)

    def outputs(c, slot):
        r0 = pl.multiple_of(c * CHUNK, CHUNK)
        vs = [v_ref[pl.ds(r0, CHUNK), h * RET_DV:(h + 1) * RET_DV] for h in heads]
        ss = [state_s[h] for h in heads]
        qks = [_dot_nt(q_s[slot, h], k_s[slot, h]) for h in heads]
        inters = [_dot(qd_s[slot, h], ss[h].astype(BF16)) for h in heads]
        kvs = [_dot(kt_s[slot, h], vs[h]) for h in heads]
        intras = [_dot((qks[h] * inner_ref[h]).astype(BF16), vs[h]) for h in heads]
        for h in heads:
            state_s[h] = ss[h] * cdec[:, h:h + 1] + kvs[h]
            o = intras[h] + inters[h]
            gate = g_ref[pl.ds(r0, CHUNK), h * RET_DV:(h + 1) * RET_DV].astype(F32)
            on = o * lax.rsqrt(jnp.mean(o * o, axis=-1, keepdims=True) + EPS)
            o_ref[pl.ds(r0, CHUNK), h * RET_DV:(h + 1) * RET_DV] = (on * _silu(gate)).astype(BF16)

    operands(0, 0)

    def body(i, carry):
        operands(2 * i + 1, 1)
        outputs(2 * i, 0)
        operands(jnp.minimum(2 * i + 2, nchunk - 1), 0)
        outputs(2 * i + 1, 1)
        return carry

    lax.fori_loop(0, nchunk // 2, body, 0)


def _ret_call(proj, sin, cos, inner, kdec, qdec, cdec, batch, seq):
    return pl.pallas_call(
        functools.partial(_ret_kernel, seq=seq),
        grid=(batch,),
        in_specs=[
            pl.BlockSpec((seq, 2 * RET_QK), lambda b: (b, 2)),
            pl.BlockSpec((seq, RET_V), lambda b: (b, 3)),
            pl.BlockSpec((seq, RET_V), lambda b: (b, 4)),
            pl.BlockSpec((seq, RET_DK), lambda b: (0, 0)),
            pl.BlockSpec((seq, RET_DK), lambda b: (0, 0)),
            pl.BlockSpec((RET_HEADS, CHUNK, CHUNK), lambda b: (0, 0, 0)),
            pl.BlockSpec((CHUNK, LANES), lambda b: (0, 0)),
            pl.BlockSpec((CHUNK, LANES), lambda b: (0, 0)),
            pl.BlockSpec((1, LANES), lambda b: (0, 0)),
        ],
        out_specs=pl.BlockSpec((seq, RET_V), lambda b: (b, 0)),
        out_shape=jax.ShapeDtypeStruct((batch * seq, RET_V), BF16),
        scratch_shapes=[pltpu.VMEM((RET_HEADS, RET_DK, RET_DV), F32)]
        + [pltpu.VMEM((2, RET_HEADS, CHUNK, RET_DK), BF16)] * 4,
        compiler_params=pltpu.CompilerParams(
            dimension_semantics=("arbitrary",), vmem_limit_bytes=VMEM_LIMIT),
        name="retention",
    )(proj, proj, proj, sin, cos, inner, kdec, qdec, cdec)


def _retention_tables(seq):
    inv_freq = 1.0 / (ROPE_BASE ** jnp.linspace(0.0, 1.0, RET_DK // 2, dtype=F32))
    ang = jnp.arange(seq, dtype=F32)[:, None] * inv_freq[None, :]
    sin = jnp.repeat(jnp.sin(ang), 2, axis=-1)
    cos = jnp.repeat(jnp.cos(ang), 2, axis=-1)
    log_gamma = jnp.log(1.0 - 2.0 ** (-5.0 - jnp.arange(RET_HEADS, dtype=F32)))
    idx = jnp.arange(CHUNK, dtype=F32)
    causal = jnp.tril(jnp.ones((CHUNK, CHUNK), dtype=bool))
    rel = jnp.where(causal, idx[:, None] - idx[None, :], 0.0)
    inner = jnp.where(causal, jnp.exp(rel[None] * log_gamma[:, None, None]), 0.0)
    k_decay = jnp.exp(log_gamma[:, None] * (CHUNK - 1.0 - idx)[None, :])
    q_decay = jnp.exp(log_gamma[:, None] * (idx + 1.0)[None, :])
    chunk_decay = jnp.exp(log_gamma * CHUNK)
    pad = LANES - RET_HEADS
    kdec = jnp.pad(k_decay.T, ((0, 0), (0, pad)))
    qdec = jnp.pad(q_decay.T, ((0, 0), (0, pad)))
    cdec = jnp.pad(chunk_decay[None, :], ((0, 0), (0, pad)))
    return sin, cos, inner, kdec, qdec, cdec


def _merge_kernel(x_ref, ya_ref, yb_ref, ma_ref, mb_ref, wa_ref, wr_ref, wo_ref, nw_ref,
                  wrt_ref, br_ref, h_ref, xn_ref, ridx_ref, rw_ref, cnt_ref):
    tm = x_ref.shape[0] // MERGE_PARTS
    rows = [pl.ds(i * tm, tm) for i in range(MERGE_PARTS)]
    a_ = [_dot(ya_ref[r, :], wa_ref[...]) for r in rows]
    r_ = [_dot(yb_ref[r, :], wr_ref[...]) for r in rows]
    merged = [(_sigmoid(ma_ref[r, :].astype(F32)) * a + _sigmoid(mb_ref[r, :].astype(F32)) * rr)
              .astype(BF16) for r, a, rr in zip(rows, a_, r_)]
    hs = [x_ref[r, :] + _dot(mg, wo_ref[...]) for r, mg in zip(rows, merged)]
    xcats = []
    for r, h in zip(rows, hs):
        h_ref[r, :] = h
        xn = h * lax.rsqrt(jnp.mean(h * h, axis=-1, keepdims=True) + EPS) * nw_ref[...]
        xn_ref[r, :] = _pack_bf16_pairs(xn)
        xh = xn.astype(BF16)
        xl = (xn - xh.astype(F32)).astype(BF16)
        xcats.append(jnp.concatenate([xh, xl], axis=0))
    parts_ = [_dot(xc, wrt_ref[...]) for xc in xcats]
    counts = jnp.zeros((1, LANES), F32)
    for r, parts in zip(rows, parts_):
        counts = counts + _route(parts, tm, br_ref[...], ridx_ref.at[r, :], rw_ref.at[r, :])

    @pl.when(pl.program_id(0) == 0)
    def _():
        cnt_ref[...] = jnp.zeros_like(cnt_ref)

    cnt_ref[...] += jnp.broadcast_to(counts, cnt_ref.shape)


def _route(parts, tm, bias, ridx_ref, rw_ref):
    logits = (parts[:tm, :LANES] + (parts[tm:, :LANES] + parts[:tm, LANES:]
                                    + parts[tm:, LANES:])) + bias
    lane = lax.broadcasted_iota(jnp.int32, (tm, LANES), 1)
    neg = -jnp.inf
    gl = jnp.where((lane >= N_EXPERTS) & (lane < N_EXPERTS + N_GROUPS), logits, neg)
    gmax = jnp.max(gl, axis=-1, keepdims=True)
    gidx = jnp.min(jnp.where(gl == gmax, lane, LANES), axis=-1, keepdims=True) - N_EXPERTS
    g_w = 1.0 / jnp.sum(jnp.exp(gl - gmax), axis=-1, keepdims=True)
    el = jnp.where((lane // EXPERTS_PER_GROUP == gidx) & (lane < N_EXPERTS), logits, neg)
    m1 = jnp.max(el, axis=-1, keepdims=True)
    i1 = jnp.min(jnp.where(el == m1, lane, LANES), axis=-1, keepdims=True)
    el2 = jnp.where(lane == i1, neg, el)
    m2 = jnp.max(el2, axis=-1, keepdims=True)
    i2 = jnp.min(jnp.where(el2 == m2, lane, LANES), axis=-1, keepdims=True)
    e2 = jnp.exp(m2 - m1)
    p1 = g_w / (1.0 + e2)
    p2 = g_w * e2 / (1.0 + e2)
    ridx_ref[...] = jnp.where(lane == 0, i1, jnp.where(lane == 1, i2, 0))
    rw_ref[...] = jnp.where(lane == 0, p1, jnp.where(lane == 1, p2, 0.0))
    onehot = jnp.where((lane == i1) | (lane == i2), 1.0, 0.0)
    return jnp.sum(onehot, axis=0, keepdims=True)


def _merge_call(x2, ya, yb, proj, wa, wr, wo, nw, w_router, b_router, tm=1024):
    m = x2.shape[0]
    full = lambda shape: pl.BlockSpec(shape, lambda i: (0, 0))
    return pl.pallas_call(
        _merge_kernel,
        grid=(m // tm,),
        in_specs=[
            pl.BlockSpec((tm, D_MODEL), lambda i: (i, 0)),
            pl.BlockSpec((tm, GDN_V), lambda i: (i, 0)),
            pl.BlockSpec((tm, RET_V), lambda i: (i, 0)),
            pl.BlockSpec((tm, D_MODEL), lambda i: (i, 5)),
            pl.BlockSpec((tm, D_MODEL), lambda i: (i, 6)),
            full((GDN_V, D_MODEL)), full((RET_V, D_MODEL)), full((D_MODEL, D_MODEL)),
            full((1, D_MODEL)),
            full((D_MODEL, 2 * LANES)), full((1, LANES)),
        ],
        out_specs=[
            pl.BlockSpec((tm, D_MODEL), lambda i: (i, 0)),
            pl.BlockSpec((tm, HALF), lambda i: (i, 0)),
            pl.BlockSpec((tm, LANES), lambda i: (i, 0)),
            pl.BlockSpec((tm, LANES), lambda i: (i, 0)),
            pl.BlockSpec((8, LANES), lambda i: (0, 0)),
        ],
        out_shape=[
            jax.ShapeDtypeStruct((m, D_MODEL), F32),
            jax.ShapeDtypeStruct((m, HALF), I32),
            jax.ShapeDtypeStruct((m, LANES), I32),
            jax.ShapeDtypeStruct((m, LANES), F32),
            jax.ShapeDtypeStruct((8, LANES), F32),
        ],
        compiler_params=pltpu.CompilerParams(
            dimension_semantics=("arbitrary",), vmem_limit_bytes=VMEM_LIMIT),
        name="merge_router",
    )(x2, ya, yb, proj, proj, wa, wr, wo, nw, w_router, b_router)


def _slot_counts(m):
    n_slots = TOP_K * m + N_EXPERTS * SLOT_TILE
    return n_slots, n_slots // SLOT_TILE


def _lane_prefix_sum(x, lane):
    s = 1
    while s < LANES:
        x = x + jnp.where(lane >= s, pltpu.roll(x, s, axis=1), 0.0)
        s *= 2
    return x


def _plan_kernel(ridx_ref, cnt_ref, slots_ref, tile_ref, carry_s, off_s):
    i = pl.program_id(0)
    lane = lax.broadcasted_iota(I32, (PLAN_TILE, LANES), 1)
    row = lax.broadcasted_iota(I32, (PLAN_TILE, LANES), 0)
    lane1 = lane[0:1]

    @pl.when(i == 0)
    def _():
        cnt = cnt_ref[0:1, :]
        tile = float(SLOT_TILE)
        padded = jnp.floor((cnt + (tile - 1.0)) / tile) * tile
        incl = _lane_prefix_sum(padded, lane1)
        off = incl - padded
        off_s[...] = off
        carry_s[...] = jnp.zeros_like(carry_s)
        first = (row * SLOT_TILE).astype(F32)
        ended = jnp.where((lane < N_EXPERTS) & (incl <= first), 1.0, 0.0)
        tile_e = jnp.sum(ended, axis=-1, keepdims=True)
        last = jnp.sum(jnp.where(lane.astype(F32) == tile_e, off + cnt, 0.0), axis=-1, keepdims=True)
        used = jnp.clip(last - first[:, 0:1], 0.0, tile)
        tile_ref[...] = jnp.where(lane == 0, tile_e, jnp.where(lane == 1, used, 0.0)).astype(I32)

    strict = jnp.where(row[:, 0:1] > lax.broadcasted_iota(I32, (PLAN_TILE, PLAN_TILE), 1),
                       1.0, 0.0).astype(BF16)
    off = off_s[...]
    carry = carry_s[...]
    for sb in range(PLAN_STEP // PLAN_TILE):
        rows = pl.ds(sb * PLAN_TILE, PLAN_TILE)
        e1 = ridx_ref[rows, 0:1]
        e2 = ridx_ref[rows, 1:2]
        onehot = jnp.where((lane == e1) | (lane == e2), 1.0, 0.0)
        pos = _dot(strict, onehot.astype(BF16)) + (carry + off)
        s1 = jnp.sum(jnp.where(lane == e1, pos, 0.0), axis=-1, keepdims=True)
        s2 = jnp.sum(jnp.where(lane == e2, pos, 0.0), axis=-1, keepdims=True)
        both = jnp.where(lane == 0, s1, jnp.where(lane == 1, s2, 0.0))
        for q in range(PLAN_TILE // LANES):
            t = both[q * LANES:(q + 1) * LANES].T
            c0 = sb * PLAN_TILE + q * LANES
            slots_ref[:, c0:c0 + LANES] = t[0:8].astype(I32)
        carry = carry + jnp.sum(onehot, axis=0, keepdims=True)
    carry_s[...] = carry


def _plan_call(ridx, cnt):
    m = ridx.shape[0]
    _, n_tiles = _slot_counts(m)
    assert n_tiles <= PLAN_TILE and SLOT_TILE == PLAN_TILE
    return pl.pallas_call(
        _plan_kernel,
        grid=(m // PLAN_STEP,),
        in_specs=[pl.BlockSpec((PLAN_STEP, LANES), lambda i: (i, 0)),
                  pl.BlockSpec((8, LANES), lambda i: (0, 0))],
        out_specs=[
            pl.BlockSpec((8, PLAN_STEP), lambda i: (0, i)),
            pl.BlockSpec((PLAN_TILE, LANES), lambda i: (0, 0)),
        ],
        out_shape=[
            jax.ShapeDtypeStruct((8, m), I32),
            jax.ShapeDtypeStruct((PLAN_TILE, LANES), I32),
        ],
        scratch_shapes=[pltpu.VMEM((1, LANES), F32), pltpu.VMEM((1, LANES), F32)],
        compiler_params=pltpu.CompilerParams(dimension_semantics=("arbitrary",)),
        name="dispatch_plan",
    )(ridx, cnt)


def _sc_mesh():
    return plsc.VectorSubcoreMesh(core_axis_name="c", subcore_axis_name="s")


def _sc_worker():
    return lax.axis_index("s") * 2 + lax.axis_index("c")


def _sc_dispatch(xn, slot1, slot2, n_rows):
    m = xn.shape[0]
    per = m // SC_WORKERS
    n_pairs = per // (2 * SC_ROWS)

    @functools.partial(
        pl.kernel, mesh=_sc_mesh(),
        out_type=jax.ShapeDtypeStruct((n_rows, HALF), I32),
        scratch_types=[pltpu.VMEM((per // SC_ROWS, SC_ROWS), I32), pltpu.VMEM((per // SC_ROWS, SC_ROWS), I32),
                       pltpu.VMEM((SC_ROWS, HALF), I32), pltpu.VMEM((SC_ROWS, HALF), I32),
                       pltpu.SemaphoreType.DMA, pltpu.SemaphoreType.DMA, pltpu.SemaphoreType.DMA],
        name="sc_dispatch")
    def k(x_hbm, s1_hbm, s2_hbm, o_hbm, i1_v, i2_v, rows0, rows1, sem_r0, sem_r1, sem_w):
        wid = _sc_worker()
        base = wid * per

        def read(chunk, rows_v, sem):
            return pltpu.make_async_copy(x_hbm.at[pl.ds(base + chunk * SC_ROWS, SC_ROWS)], rows_v, sem)

        def scatter(chunk, rows_v):
            c1 = pltpu.async_copy(rows_v, o_hbm.at[i1_v.at[chunk]], sem_w)
            c2 = pltpu.async_copy(rows_v, o_hbm.at[i2_v.at[chunk]], sem_w)
            c1.wait()
            c2.wait()

        read(0, rows0, sem_r0).start()
        pltpu.sync_copy(s1_hbm.at[pl.ds(wid * (per // SC_ROWS), per // SC_ROWS)], i1_v)
        pltpu.sync_copy(s2_hbm.at[pl.ds(wid * (per // SC_ROWS), per // SC_ROWS)], i2_v)

        @pl.loop(0, n_pairs)
        def _(i):
            read(2 * i, rows0, sem_r0).wait()
            read(2 * i + 1, rows1, sem_r1).start()
            scatter(2 * i, rows0)
            read(2 * i + 1, rows1, sem_r1).wait()

            @pl.when(i + 1 < n_pairs)
            def _():
                read(2 * i + 2, rows0, sem_r0).start()

            scatter(2 * i + 1, rows1)

    return k(xn, slot1, slot2)


def _sc_collect(ys, slot1, slot2):
    m = slot1.size
    per = m // SC_WORKERS
    row = jax.ShapeDtypeStruct((m, HALF), I32)

    @functools.partial(
        pl.kernel, mesh=_sc_mesh(), out_type=[row, row],
        scratch_types=[pltpu.VMEM((per // SC_ROWS, SC_ROWS), I32), pltpu.VMEM((per // SC_ROWS, SC_ROWS), I32),
                       pltpu.VMEM((SC_ROWS, HALF), I32), pltpu.VMEM((SC_ROWS, HALF), I32),
                       pltpu.SemaphoreType.DMA, pltpu.SemaphoreType.DMA],
        name="sc_collect")
    def k(y_hbm, s1_hbm, s2_hbm, g1_hbm, g2_hbm, i1_v, i2_v, rows1, rows2, sem_g, sem_w):
        wid = _sc_worker()
        base = wid * per
        pltpu.sync_copy(s1_hbm.at[pl.ds(wid * (per // SC_ROWS), per // SC_ROWS)], i1_v)
        pltpu.sync_copy(s2_hbm.at[pl.ds(wid * (per // SC_ROWS), per // SC_ROWS)], i2_v)

        @pl.loop(0, per // SC_ROWS)
        def _(ci):
            t0 = base + ci * SC_ROWS
            a1 = pltpu.async_copy(y_hbm.at[i1_v.at[ci]], rows1, sem_g)
            a2 = pltpu.async_copy(y_hbm.at[i2_v.at[ci]], rows2, sem_g)
            a1.wait()
            a2.wait()
            w1 = pltpu.async_copy(rows1, g1_hbm.at[pl.ds(t0, SC_ROWS)], sem_w)
            w2 = pltpu.async_copy(rows2, g2_hbm.at[pl.ds(t0, SC_ROWS)], sem_w)
            w1.wait()
            w2.wait()

    return k(ys, slot1, slot2)


def _expert_kernel(te_ref, used_ref, xs_ref, wg_hbm, wu_hbm, wd_hbm, ys_ref, wg_b, wu_b, wd_b,
                   wg_f, wu_f, wd_f, sem, slot_s):
    j = pl.program_id(0)
    n = pl.num_programs(0)
    e = te_ref[j]
    prev = te_ref[jnp.maximum(j - 1, 0)]
    valid = e < N_EXPERTS

    def weight_copies(expert, slot):
        return [pltpu.make_async_copy(hbm.at[expert], buf.at[slot], sem.at[slot, i])
                for i, (hbm, buf) in enumerate(((wg_hbm, wg_f), (wu_hbm, wu_f), (wd_hbm, wd_f)))]

    @pl.when((j == 0) & valid)
    def _():
        slot_s[0] = 0
        for c in weight_copies(e, 0):
            c.start()

    @pl.when(((j == 0) | (e != prev)) & valid)
    def _():
        slot = slot_s[0]
        for c in weight_copies(e, slot):
            c.wait()
        k = lax.while_loop(lambda k: (k < n) & (te_ref[jnp.minimum(k, n - 1)] == e),
                           lambda k: k + 1, j + 1)
        nxt = te_ref[jnp.minimum(k, n - 1)]

        @pl.when((k < n) & (nxt < N_EXPERTS))
        def _():
            for c in weight_copies(nxt, 1 - slot):
                c.start()

        wg_b[...] = wg_f[slot].astype(BF16)
        wu_b[...] = wu_f[slot].astype(BF16)
        wd_b[...] = wd_f[slot].astype(BF16)
        slot_s[0] = 1 - slot

    @pl.when(valid)
    def _():
        half = SLOT_TILE // 2
        rows = [pl.ds(i * half, half) for i in range(2)]
        row_id = lax.broadcasted_iota(I32, (half, HALF), 0)
        xs = [_unpack_bf16_pairs(jnp.where(row_id + i * half < used_ref[j], xs_ref[r, :], 0))
              .astype(BF16) for i, r in enumerate(rows)]
        gs = [_dot(x, wg_b[...]) for x in xs]
        us = [_dot(x, wu_b[...]) for x in xs]
        hids = [(_silu(g) * u).astype(BF16) for g, u in zip(gs, us)]
        ys = [_dot(hid, wd_b[...]) for hid in hids]
        for r, y in zip(rows, ys):
            ys_ref[r, :] = _pack_bf16_pairs(y)

    @pl.when(e >= N_EXPERTS)
    def _():
        ys_ref[...] = jnp.zeros_like(ys_ref)


def _expert_call(tile_expert, tile_used, xs, wg, wu, wd, n_tiles):
    hbm = pl.BlockSpec(memory_space=pl.ANY)
    return pl.pallas_call(
        _expert_kernel,
        grid_spec=pltpu.PrefetchScalarGridSpec(
            num_scalar_prefetch=2,
            grid=(n_tiles,),
            in_specs=[pl.BlockSpec((SLOT_TILE, HALF), lambda j, te, used: (j, 0)), hbm, hbm, hbm],
            out_specs=pl.BlockSpec((SLOT_TILE, HALF), lambda j, te, used: (j, 0)),
            scratch_shapes=[
                pltpu.VMEM((D_MODEL, D_EXPERT), BF16), pltpu.VMEM((D_MODEL, D_EXPERT), BF16),
                pltpu.VMEM((D_EXPERT, D_MODEL), BF16),
                pltpu.VMEM((2, D_MODEL, D_EXPERT), F32), pltpu.VMEM((2, D_MODEL, D_EXPERT), F32),
                pltpu.VMEM((2, D_EXPERT, D_MODEL), F32),
                pltpu.SemaphoreType.DMA((2, 3)),
                pltpu.SMEM((1,), I32),
            ],
        ),
        out_shape=jax.ShapeDtypeStruct((n_tiles * SLOT_TILE, HALF), I32),
        compiler_params=pltpu.CompilerParams(
            dimension_semantics=("arbitrary",), vmem_limit_bytes=VMEM_LIMIT),
        name="experts",
    )(tile_expert, tile_used, xs, wg, wu, wd)


def _final_kernel(h_ref, g1_ref, g2_ref, rw_ref, nw_ref, o_ref):
    rw = rw_ref[...]
    y = rw[:, 0:1] * _unpack_bf16_pairs(g1_ref[...]) + rw[:, 1:2] * _unpack_bf16_pairs(g2_ref[...])
    h = h_ref[...] + y
    o_ref[...] = h * lax.rsqrt(jnp.mean(h * h, axis=-1, keepdims=True) + EPS) * nw_ref[...]


def _final_call(h1, g1, g2, rw, nw, tm=512):
    m = h1.shape[0]
    return pl.pallas_call(
        _final_kernel,
        grid=(m // tm,),
        in_specs=[
            pl.BlockSpec((tm, D_MODEL), lambda i: (i, 0)),
            pl.BlockSpec((tm, HALF), lambda i: (i, 0)),
            pl.BlockSpec((tm, HALF), lambda i: (i, 0)),
            pl.BlockSpec((tm, LANES), lambda i: (i, 0)),
            pl.BlockSpec((1, D_MODEL), lambda i: (0, 0)),
        ],
        out_specs=pl.BlockSpec((tm, D_MODEL), lambda i: (i, 0)),
        out_shape=jax.ShapeDtypeStruct((m, D_MODEL), F32),
        compiler_params=pltpu.CompilerParams(dimension_semantics=("arbitrary",)),
        name="combine_final",
    )(h1, g1, g2, rw, nw)


def _pad_lanes(a):
    return jnp.pad(a, ((0, 0), (0, LANES - a.shape[1])))


def kernel(x, norm_mix_w, w_in, conv_w, A_log, dt_bias, gdn_norm_w, w_up_gdn, w_up_ret, w_out,
           norm_ffn_w, w_group, b_group, w_expert, b_expert, w_gate, w_up, w_down, norm_final_w):
    batch, seq, d = x.shape
    m = batch * seq
    h = x.reshape(m, d)
    depth = w_in.shape[0]
    sin, cos, inner, kdec, qdec, cdec = _retention_tables(seq)
    for l in range(depth):
        o_ab = 3 * GDN_QK
        o_z = o_ab + 2 * GDN_HEADS
        wl = w_in[l]
        w_main = jnp.concatenate([wl[:, :o_ab], wl[:, o_z:]], axis=1).astype(BF16)
        w_ab = _pad_lanes(wl[:, o_ab:o_z]).astype(BF16)
        proj, ab = _proj_call(h, norm_mix_w[l][None, :], w_main, w_ab)

        conv8 = jnp.pad(conv_w[l], ((0, 8 - GDN_CONV), (0, 0)))
        ya = _gdn_call(proj, ab, conv8, _pad_lanes(A_log[l][None, :]), _pad_lanes(dt_bias[l][None, :]),
                       gdn_norm_w[l][None, :], batch, seq)
        yb = _ret_call(proj, sin, cos, inner, kdec, qdec, cdec, batch, seq)

        w_router = _pad_lanes(jnp.concatenate([w_expert[l], w_group[l]], axis=1))
        wr_hi = w_router.astype(BF16)
        wr_lo = (w_router - wr_hi.astype(F32)).astype(BF16)
        b_router = _pad_lanes(jnp.concatenate([b_expert[l], b_group[l]])[None, :])
        h1, xn, ridx, rw, cnt = _merge_call(
            h, ya, yb, proj, w_up_gdn[l].astype(BF16), w_up_ret[l].astype(BF16), w_out[l].astype(BF16),
            norm_ffn_w[l][None, :], jnp.concatenate([wr_hi, wr_lo], axis=1), b_router)

        n_slots, n_tiles = _slot_counts(m)
        slots, tiles = _plan_call(ridx, cnt)
        slot1, slot2 = slots[0].reshape(-1, SC_ROWS), slots[1].reshape(-1, SC_ROWS)
        xs = _sc_dispatch(xn, slot1, slot2, n_slots)
        ys = _expert_call(tiles[:n_tiles, 0], tiles[:n_tiles, 1], xs,
                          w_gate[l], w_up[l], w_down[l], n_tiles)
        g1, g2 = _sc_collect(ys, slot1, slot2)

        assert depth == 1
        h = _final_call(h1, g1, g2, rw, norm_final_w[None, :])
    return h.reshape(batch, seq, d)
```

```python
import functools
import math

import jax
import jax.numpy as jnp
from jax import lax
from jax.experimental import pallas as pl
from jax.experimental.pallas import tpu as pltpu
from jax.experimental.pallas import tpu_sc as plsc

F32 = jnp.float32
BF16 = jnp.bfloat16
I32 = jnp.int32
U32 = jnp.uint32

D_MODEL = 1024
EPS = 1e-6
GDN_HEADS = 4
GDN_DK = 128
GDN_DV = 128
GDN_CONV = 4
RET_HEADS = 4
RET_DK = 128
RET_DV = 256
ROPE_BASE = 10000.0
N_GROUPS = 4
EXPERTS_PER_GROUP = 8
N_EXPERTS = N_GROUPS * EXPERTS_PER_GROUP
D_EXPERT = 512

GDN_QK = GDN_HEADS * GDN_DK
GDN_V = GDN_HEADS * GDN_DV
RET_QK = RET_HEADS * RET_DK
RET_V = RET_HEADS * RET_DV

LANES = 128
CHUNK = 128
INV_BLOCK = 16
GDN_PREP_CHUNKS = 2
VMEM_LIMIT = 56 * 1024 * 1024

MERGE_PARTS = 1
TOP_K = 2
SLOT_TILE = 256
PLAN_TILE = 256
PLAN_STEP = 1024
HALF = D_MODEL // 2
SC_WORKERS = 32
SC_ROWS = 64

PROJ_COLS = 3 * GDN_QK + GDN_V + 2 * RET_QK + 2 * RET_V + 2 * D_MODEL


def _silu(x):
    return x / (1.0 + jnp.exp(-x))


def _sigmoid(x):
    return 1.0 / (1.0 + jnp.exp(-x))


def _dot(a, b):
    return jnp.dot(a, b, preferred_element_type=F32)


def _dot_nt(a, b):
    return lax.dot_general(a, b, (((1,), (1,)), ((), ())), preferred_element_type=F32)


def _pack_bf16_pairs(x):
    bits = lax.bitcast_convert_type(x.astype(BF16).astype(F32), U32)
    packed = (bits[:, :HALF] >> 16) | (bits[:, HALF:] & jnp.uint32(0xFFFF0000))
    return lax.bitcast_convert_type(packed, I32)


def _unpack_bf16_pairs(p):
    p = lax.bitcast_convert_type(p, U32)
    lo = lax.bitcast_convert_type(p << 16, F32)
    hi = lax.bitcast_convert_type(p & jnp.uint32(0xFFFF0000), F32)
    return jnp.concatenate([lo, hi], axis=1)


def _proj_kernel(x_ref, nw_ref, w_ref, wab_ref, proj_ref, ab_ref, u_ref):
    j = pl.program_id(1)

    @pl.when(j == 0)
    def _():
        x = x_ref[...]
        u = x * lax.rsqrt(jnp.mean(x * x, axis=-1, keepdims=True) + EPS) * nw_ref[...]
        ub = u.astype(BF16)
        u_ref[...] = ub
        ab_ref[...] = _dot(ub, wab_ref[...])

    proj_ref[...] = _dot(u_ref[...], w_ref[...]).astype(BF16)


def _repack_kernel(w_ref, main_ref, ab_ref):
    o_ab = 3 * GDN_QK
    o_z = o_ab + 2 * GDN_HEADS
    w = w_ref[...]
    main_ref[:, :o_ab] = w[:, :o_ab].astype(BF16)
    main_ref[:, o_ab:] = w[:, o_z:].astype(BF16)
    ab = jnp.concatenate([w[:, o_ab:o_z], jnp.zeros((w.shape[0], LANES - (o_z - o_ab)), F32)], axis=1)
    ab_ref[...] = ab.astype(BF16)


def _repack_call(w_in, rows=128):
    d, d_in = w_in.shape
    return pl.pallas_call(
        _repack_kernel,
        grid=(d // rows,),
        in_specs=[pl.BlockSpec((rows, d_in), lambda i: (i, 0))],
        out_specs=[pl.BlockSpec((rows, PROJ_COLS), lambda i: (i, 0)),
                   pl.BlockSpec((rows, LANES), lambda i: (i, 0))],
        out_shape=[jax.ShapeDtypeStruct((d, PROJ_COLS), BF16), jax.ShapeDtypeStruct((d, LANES), BF16)],
        compiler_params=pltpu.CompilerParams(dimension_semantics=("arbitrary",)),
        name="repack_w_in",
    )(w_in)


def _proj_call(x2, norm_w, w_main, w_ab, tm=1024, tn=3584):
    m = x2.shape[0]
    return pl.pallas_call(
        _proj_kernel,
        grid=(m // tm, PROJ_COLS // tn),
        in_specs=[
            pl.BlockSpec((tm, D_MODEL), lambda i, j: (i, 0)),
            pl.BlockSpec((1, D_MODEL), lambda i, j: (0, 0)),
            pl.BlockSpec((D_MODEL, tn), lambda i, j: (0, j)),
            pl.BlockSpec((D_MODEL, LANES), lambda i, j: (0, 0)),
        ],
        out_specs=[
            pl.BlockSpec((tm, tn), lambda i, j: (i, j)),
            pl.BlockSpec((tm, LANES), lambda i, j: (i, 0)),
        ],
        out_shape=[
            jax.ShapeDtypeStruct((m, PROJ_COLS), BF16),
            jax.ShapeDtypeStruct((m, LANES), F32),
        ],
        scratch_shapes=[pltpu.VMEM((tm, D_MODEL), BF16)],
        compiler_params=pltpu.CompilerParams(
            dimension_semantics=("arbitrary", "arbitrary"), vmem_limit_bytes=VMEM_LIMIT),
        name="proj",
    )(x2, norm_w, w_main, w_ab)


def _unit_lower_inverses(lows, ii, jj):
    eye = jnp.where(ii == jj, 1.0, 0.0).astype(F32)
    in_block = (ii // INV_BLOCK) == (jj // INV_BLOCK)
    ps = [jnp.where(in_block, -low, 0.0) for low in lows]
    ts = [eye + p for p in ps]
    span = 2
    while span < INV_BLOCK:
        ps = [_dot(p, p) for p in ps]
        ts = [t + _dot(t, p) for t, p in zip(ts, ps)]
        span *= 2
    s = INV_BLOCK
    while s < CHUNK:
        off_diag = ((ii // (2 * s)) == (jj // (2 * s))) & ((ii // s) != (jj // s))
        xs = [_dot(jnp.where(off_diag, low, 0.0), t) for low, t in zip(lows, ts)]
        ts = [t - _dot(t, x) for t, x in zip(ts, xs)]
        s *= 2
    return ts


def _gdn_kernel(qkv_ref, z_ref, ab_ref, convw_ref, alog_ref, dtb_ref, normw_ref, o_ref,
                b_s, o0_s, m_s, qp_s, gl_s, state_s, q_s, k_s, kb_s, rhs_s, dec_s, cv_s, qg_o, kdt_o,
                *, seq):
    nchunk = seq // CHUNK
    ii = lax.broadcasted_iota(jnp.int32, (CHUNK, CHUNK), 0)
    jj = lax.broadcasted_iota(jnp.int32, (CHUNK, CHUNK), 1)
    causal = ii >= jj
    strict = ii > jj
    tri = jnp.where(causal, 1.0, 0.0).astype(F32)
    neg_a = -jnp.exp(alog_ref[...])
    dtb = dtb_ref[...]

    def conv_cols(c, r0, lo, buf):
        x = qkv_ref[pl.ds(r0, CHUNK), lo:lo + LANES].astype(F32)
        prev0 = pl.multiple_of(jnp.maximum(r0 - 16, 0), 16)
        prev = qkv_ref[pl.ds(prev0, 16), lo:lo + LANES].astype(F32)
        buf[0:8, :] = prev[8:16] * jnp.where(c > 0, 1.0, 0.0)
        buf[8:8 + CHUNK, :] = x
        w = convw_ref[:, lo:lo + LANES]
        y = (w[3:4] * x + w[2:3] * buf[7:7 + CHUNK, :] + w[1:2] * buf[6:6 + CHUNK, :]
             + w[0:1] * buf[5:5 + CHUNK, :])
        return _silu(y)

    def l2n(x):
        return x * lax.rsqrt(jnp.sum(x * x, axis=-1, keepdims=True) + EPS)

    tri_b = tri.astype(BF16)

    def chunk_cumsum(g):
        g1 = g.astype(BF16)
        r1 = g - g1.astype(F32)
        g2 = r1.astype(BF16)
        g3 = (r1 - g2.astype(F32)).astype(BF16)
        return _dot(tri_b, g1) + (_dot(tri_b, g2) + _dot(tri_b, g3))

    def operands(cc, slot):
        for sub in range(GDN_PREP_CHUNKS):
            c = cc * GDN_PREP_CHUNKS + sub
            r0 = pl.multiple_of(c * CHUNK, CHUNK)
            ab = ab_ref[pl.ds(r0, CHUNK), :]
            xg = ab + dtb
            softplus = jnp.maximum(xg, 0.0) + jnp.log(1.0 + jnp.exp(-jnp.abs(xg)))
            g_all = neg_a * softplus
            beta_all = _sigmoid(ab)
            gc_all = chunk_cumsum(g_all)
            gc_t = gc_all.T
            gl_s[c] = jnp.exp(gc_all[CHUNK - 1:CHUNK, :])
            for h in range(GDN_HEADS):
                n = sub * GDN_HEADS + h
                bufs = [cv_s.at[slot, 3 * n + i] for i in range(3)]
                q = l2n(conv_cols(c, r0, h * GDN_DK, bufs[0])) * (GDN_DK ** -0.5)
                k = l2n(conv_cols(c, r0, GDN_QK + h * GDN_DK, bufs[1]))
                v = conv_cols(c, r0, 2 * GDN_QK + h * GDN_DV, bufs[2])
                gcol = gc_all[:, h:h + 1]
                grow = gc_t[h:h + 1, :]
                beta = beta_all[:, GDN_HEADS + h:GDN_HEADS + h + 1]
                dec_s[slot, n] = jnp.where(causal, jnp.exp(gcol - grow), 0.0)
                eg = jnp.exp(gcol)
                kb = k * beta
                q_s[slot, n] = q.astype(BF16)
                k_s[slot, n] = k.astype(BF16)
                kb_s[slot, n] = kb.astype(BF16)
                rhs_s[slot, n] = jnp.concatenate([v * beta, kb * eg], axis=1).astype(BF16)
                qg_o[slot, n] = q * eg
                kd = k * jnp.exp(gc_all[CHUNK - 1:CHUNK, h:h + 1] - gcol)
                kdt_o[slot, n] = kd.T.astype(BF16)

    def solve(cc, slot, between):
        items = [(cc * GDN_PREP_CHUNKS + sub, h, sub * GDN_HEADS + h)
                 for sub in range(GDN_PREP_CHUNKS) for h in range(GDN_HEADS)]
        kks = [_dot_nt(kb_s[slot, n], k_s[slot, n]) for _, _, n in items]
        qks = [_dot_nt(q_s[slot, n], k_s[slot, n]) for _, _, n in items]
        lows = [jnp.where(strict, kk * dec_s[slot, n], 0.0) for kk, (_, _, n) in zip(kks, items)]
        attns = [(qk * dec_s[slot, n]).astype(BF16) for qk, (_, _, n) in zip(qks, items)]
        for step in between[:len(between) // 2]:
            step()
        ts = _unit_lower_inverses(lows, ii, jj)
        for step in between[len(between) // 2:]:
            step()
        uws = [_dot(t.astype(BF16), rhs_s[slot, n]).astype(BF16) for t, (_, _, n) in zip(ts, items)]
        kds = [_dot(kdt_o[slot, n], uw) for uw, (_, _, n) in zip(uws, items)]
        ats = [_dot(attn, uw) for attn, uw in zip(attns, uws)]
        for kd_uw, at_uw, (c, h, n) in zip(kds, ats, items):
            b_s[c, h] = kd_uw[:, :GDN_DV]
            m_s[c, h] = (-kd_uw[:, GDN_DV:]).astype(BF16)
            o0_s[c, h] = at_uw[:, :GDN_DV]
            qp_s[c, h] = (qg_o[slot, n] - at_uw[:, GDN_DV:]).astype(BF16)

    ngroup = nchunk // GDN_PREP_CHUNKS
    per_trip = 2 * GDN_PREP_CHUNKS
    normw = normw_ref[...]
    state_s[...] = jnp.zeros_like(state_s)
    for c0 in range(per_trip):
        gl_s[c0] = jnp.zeros((1, LANES), F32)
        for h in range(GDN_HEADS):
            b_s[c0, h] = jnp.zeros((CHUNK, GDN_DV), F32)
            o0_s[c0, h] = jnp.zeros((CHUNK, GDN_DV), F32)
            m_s[c0, h] = jnp.zeros((CHUNK, GDN_DK), BF16)
            qp_s[c0, h] = jnp.zeros((CHUNK, GDN_DK), BF16)
    operands(0, 0)

    def prep(i, carry):
        first = jnp.maximum(per_trip * (i - 1), 0)
        steps = [functools.partial(scan, first + k, 0) for k in range(per_trip)]
        operands(2 * i + 1, 1)
        solve(2 * i, 0, steps[:per_trip // 2])
        operands(jnp.minimum(2 * i + 2, ngroup - 1), 0)
        solve(2 * i + 1, 1, steps[per_trip // 2:])
        return carry

    def scan(c, carry):
        r0 = pl.multiple_of(c * CHUNK, CHUNK)
        gl = gl_s[c]
        heads = range(GDN_HEADS)
        ss = [state_s[h] for h in heads]
        sbs = [s.astype(BF16) for s in ss]
        mss = [_dot(m_s[c, h], sbs[h]) for h in heads]
        qss = [_dot(qp_s[c, h], sbs[h]) for h in heads]
        for h in heads:
            state_s[h] = ss[h] * gl[:, h:h + 1] + (mss[h] + b_s[c, h])
            o = qss[h] + o0_s[c, h]
            z = z_ref[pl.ds(r0, CHUNK), h * GDN_DV:(h + 1) * GDN_DV].astype(F32)
            on = o * lax.rsqrt(jnp.mean(o * o, axis=-1, keepdims=True) + EPS) * normw
            o_ref[pl.ds(r0, CHUNK), h * GDN_DV:(h + 1) * GDN_DV] = (on * _silu(z)).astype(BF16)
        return carry

    ntrip = ngroup // 2
    lax.fori_loop(0, ntrip, prep, 0)
    lax.fori_loop(per_trip * (ntrip - 1), nchunk, scan, 0)


def _gdn_call(proj, ab, conv_w8, alog_p, dtb_p, normw, batch, seq):
    nchunk = seq // CHUNK
    hs = (nchunk, GDN_HEADS, CHUNK, CHUNK)
    ops = (2, GDN_PREP_CHUNKS * GDN_HEADS, CHUNK, CHUNK)
    return pl.pallas_call(
        functools.partial(_gdn_kernel, seq=seq),
        grid=(batch,),
        in_specs=[
            pl.BlockSpec((seq, 3 * GDN_QK), lambda b: (b, 0)),
            pl.BlockSpec((seq, GDN_V), lambda b: (b, 3)),
            pl.BlockSpec((seq, LANES), lambda b: (b, 0)),
            pl.BlockSpec((8, 3 * GDN_QK), lambda b: (0, 0)),
            pl.BlockSpec((1, LANES), lambda b: (0, 0)),
            pl.BlockSpec((1, LANES), lambda b: (0, 0)),
            pl.BlockSpec((1, GDN_DV), lambda b: (0, 0)),
        ],
        out_specs=pl.BlockSpec((seq, GDN_V), lambda b: (b, 0)),
        out_shape=jax.ShapeDtypeStruct((batch * seq, GDN_V), BF16),
        scratch_shapes=[
            pltpu.VMEM(hs, F32), pltpu.VMEM(hs, F32), pltpu.VMEM(hs, BF16), pltpu.VMEM(hs, BF16),
            pltpu.VMEM((nchunk, 1, LANES), F32),
            pltpu.VMEM((GDN_HEADS, GDN_DK, GDN_DV), F32),
            pltpu.VMEM(ops, BF16), pltpu.VMEM(ops, BF16), pltpu.VMEM(ops, BF16),
            pltpu.VMEM(ops[:3] + (2 * CHUNK,), BF16),
            pltpu.VMEM(ops, F32),
            pltpu.VMEM((2, 3 * ops[1], 8 + CHUNK, LANES), F32),
            pltpu.VMEM(ops, F32), pltpu.VMEM(ops, BF16),
        ],
        compiler_params=pltpu.CompilerParams(
            dimension_semantics=("arbitrary",), vmem_limit_bytes=VMEM_LIMIT),
        name="gdn",
    )(proj, proj, ab, conv_w8, alog_p, dtb_p, normw)


def _ret_kernel(qk_ref, v_ref, g_ref, sin_ref, cos_ref, inner_ref, kdec_ref, qdec_ref, cdec_ref,
                o_ref, state_s, q_s, qd_s, k_s, kt_s, *, seq):
    nchunk = seq // CHUNK
    lane = lax.broadcasted_iota(jnp.int32, (CHUNK, RET_DK), 1)
    even = (lane % 2) == 0

    def rotate(x, sin, cos):
        nxt = pltpu.roll(x, RET_DK - 1, axis=1)
        prv = pltpu.roll(x, 1, axis=1)
        return x * cos + jnp.where(even, -nxt, prv) * sin

    state_s[...] = jnp.zeros_like(state_s)
    kdec = kdec_ref[...]
    qdec = qdec_ref[...]
    cdec = cdec_ref[...]

    heads = range(RET_HEADS)

    def operands(c, slot):
        r0 = pl.multiple_of(c * CHUNK, CHUNK)
        sin = sin_ref[pl.ds(r0, CHUNK), :]
        cos = cos_ref[pl.ds(r0, CHUNK), :]
        for h in heads:
            q = rotate(qk_ref[pl.ds(r0, CHUNK), h * RET_DK:(h + 1) * RET_DK].astype(F32), sin, cos)
            k = rotate(qk_ref[pl.ds(r0, CHUNK), RET_QK + h * RET_DK:RET_QK + (h + 1) * RET_DK]
                       .astype(F32), sin, cos) * (RET_DK ** -0.5)
            q_s[slot, h] = q.astype(BF16)
            qd_s[slot, h] = (q * qdec[:, h:h + 1]).astype(BF16)
            k_s[slot, h] = k.astype(BF16)
            kt_s[slot, h] = (k * kdec[:, h:h + 1]).T.astype(BF16)

    def outputs(c, slot):
        r0 = pl.multiple_of(c * CHUNK, CHUNK)
        vs = [v_ref[pl.ds(r0, CHUNK), h * RET_DV:(h + 1) * RET_DV] for h in heads]
        ss = [state_s[h] for h in heads]
        qks = [_dot_nt(q_s[slot, h], k_s[slot, h]) for h in heads]
        inters = [_dot(qd_s[slot, h], ss[h].astype(BF16)) for h in heads]
        kvs = [_dot(kt_s[slot, h], vs[h]) for h in heads]
        intras = [_dot((qks[h] * inner_ref[h]).astype(BF16), vs[h]) for h in heads]
        for h in heads:
            state_s[h] = ss[h] * cdec[:, h:h + 1] + kvs[h]
            o = intras[h] + inters[h]
            gate = g_ref[pl.ds(r0, CHUNK), h * RET_DV:(h + 1) * RET_DV].astype(F32)
            on = o * lax.rsqrt(jnp.mean(o * o, axis=-1, keepdims=True) + EPS)
            o_ref[pl.ds(r0, CHUNK), h * RET_DV:(h + 1) * RET_DV] = (on * _silu(gate)).astype(BF16)

    operands(0, 0)

    def body(i, carry):
        operands(2 * i + 1, 1)
        outputs(2 * i, 0)
        operands(jnp.minimum(2 * i + 2, nchunk - 1), 0)
        outputs(2 * i + 1, 1)
        return carry

    lax.fori_loop(0, nchunk // 2, body, 0)


def _ret_call(proj, sin, cos, inner, kdec, qdec, cdec, batch, seq):
    return pl.pallas_call(
        functools.partial(_ret_kernel, seq=seq),
        grid=(batch,),
        in_specs=[
            pl.BlockSpec((seq, 2 * RET_QK), lambda b: (b, 2)),
            pl.BlockSpec((seq, RET_V), lambda b: (b, 3)),
            pl.BlockSpec((seq, RET_V), lambda b: (b, 4)),
            pl.BlockSpec((seq, RET_DK), lambda b: (0, 0)),
            pl.BlockSpec((seq, RET_DK), lambda b: (0, 0)),
            pl.BlockSpec((RET_HEADS, CHUNK, CHUNK), lambda b: (0, 0, 0)),
            pl.BlockSpec((CHUNK, LANES), lambda b: (0, 0)),
            pl.BlockSpec((CHUNK, LANES), lambda b: (0, 0)),
            pl.BlockSpec((1, LANES), lambda b: (0, 0)),
        ],
        out_specs=pl.BlockSpec((seq, RET_V), lambda b: (b, 0)),
        out_shape=jax.ShapeDtypeStruct((batch * seq, RET_V), BF16),
        scratch_shapes=[pltpu.VMEM((RET_HEADS, RET_DK, RET_DV), F32)]
        + [pltpu.VMEM((2, RET_HEADS, CHUNK, RET_DK), BF16)] * 4,
        compiler_params=pltpu.CompilerParams(
            dimension_semantics=("arbitrary",), vmem_limit_bytes=VMEM_LIMIT),
        name="retention",
    )(proj, proj, proj, sin, cos, inner, kdec, qdec, cdec)


def _retention_tables(seq):
    inv_freq = 1.0 / (ROPE_BASE ** jnp.linspace(0.0, 1.0, RET_DK // 2, dtype=F32))
    ang = jnp.arange(seq, dtype=F32)[:, None] * inv_freq[None, :]
    sin = jnp.repeat(jnp.sin(ang), 2, axis=-1)
    cos = jnp.repeat(jnp.cos(ang), 2, axis=-1)
    log_gamma = jnp.log(1.0 - 2.0 ** (-5.0 - jnp.arange(RET_HEADS, dtype=F32)))
    idx = jnp.arange(CHUNK, dtype=F32)
    causal = jnp.tril(jnp.ones((CHUNK, CHUNK), dtype=bool))
    rel = jnp.where(causal, idx[:, None] - idx[None, :], 0.0)
    inner = jnp.where(causal, jnp.exp(rel[None] * log_gamma[:, None, None]), 0.0)
    k_decay = jnp.exp(log_gamma[:, None] * (CHUNK - 1.0 - idx)[None, :])
    q_decay = jnp.exp(log_gamma[:, None] * (idx + 1.0)[None, :])
    chunk_decay = jnp.exp(log_gamma * CHUNK)
    pad = LANES - RET_HEADS
    kdec = jnp.pad(k_decay.T, ((0, 0), (0, pad)))
    qdec = jnp.pad(q_decay.T, ((0, 0), (0, pad)))
    cdec = jnp.pad(chunk_decay[None, :], ((0, 0), (0, pad)))
    return sin, cos, inner, kdec, qdec, cdec


def _merge_kernel(x_ref, ya_ref, yb_ref, ma_ref, mb_ref, wa_ref, wr_ref, wo_ref, nw_ref,
                  wrt_ref, br_ref, h_ref, xn_ref, ridx_ref, rw_ref, cnt_ref):
    tm = x_ref.shape[0] // MERGE_PARTS
    rows = [pl.ds(i * tm, tm) for i in range(MERGE_PARTS)]
    a_ = [_dot(ya_ref[r, :], wa_ref[...]) for r in rows]
    r_ = [_dot(yb_ref[r, :], wr_ref[...]) for r in rows]
    merged = [(_sigmoid(ma_ref[r, :].astype(F32)) * a + _sigmoid(mb_ref[r, :].astype(F32)) * rr)
              .astype(BF16) for r, a, rr in zip(rows, a_, r_)]
    hs = [x_ref[r, :] + _dot(mg, wo_ref[...]) for r, mg in zip(rows, merged)]
    xcats = []
    for r, h in zip(rows, hs):
        h_ref[r, :] = h
        xn = h * lax.rsqrt(jnp.mean(h * h, axis=-1, keepdims=True) + EPS) * nw_ref[...]
        xn_ref[r, :] = _pack_bf16_pairs(xn)
        xh = xn.astype(BF16)
        xl = (xn - xh.astype(F32)).astype(BF16)
        xcats.append(jnp.concatenate([xh, xl], axis=0))
    parts_ = [_dot(xc, wrt_ref[...]) for xc in xcats]
    counts = jnp.zeros((1, LANES), F32)
    for r, parts in zip(rows, parts_):
        counts = counts + _route(parts, tm, br_ref[...], ridx_ref.at[r, :], rw_ref.at[r, :])

    @pl.when(pl.program_id(0) == 0)
    def _():
        cnt_ref[...] = jnp.zeros_like(cnt_ref)

    cnt_ref[...] += jnp.broadcast_to(counts, cnt_ref.shape)


def _route(parts, tm, bias, ridx_ref, rw_ref):
    logits = (parts[:tm, :LANES] + (parts[tm:, :LANES] + parts[:tm, LANES:]
                                    + parts[tm:, LANES:])) + bias
    lane = lax.broadcasted_iota(jnp.int32, (tm, LANES), 1)
    neg = -jnp.inf
    gl = jnp.where((lane >= N_EXPERTS) & (lane < N_EXPERTS + N_GROUPS), logits, neg)
    gmax = jnp.max(gl, axis=-1, keepdims=True)
    gidx = jnp.min(jnp.where(gl == gmax, lane, LANES), axis=-1, keepdims=True) - N_EXPERTS
    g_w = 1.0 / jnp.sum(jnp.exp(gl - gmax), axis=-1, keepdims=True)
    el = jnp.where((lane // EXPERTS_PER_GROUP == gidx) & (lane < N_EXPERTS), logits, neg)
    m1 = jnp.max(el, axis=-1, keepdims=True)
    i1 = jnp.min(jnp.where(el == m1, lane, LANES), axis=-1, keepdims=True)
    el2 = jnp.where(lane == i1, neg, el)
    m2 = jnp.max(el2, axis=-1, keepdims=True)
    i2 = jnp.min(jnp.where(el2 == m2, lane, LANES), axis=-1, keepdims=True)
    e2 = jnp.exp(m2 - m1)
    p1 = g_w / (1.0 + e2)
    p2 = g_w * e2 / (1.0 + e2)
    ridx_ref[...] = jnp.where(lane == 0, i1, jnp.where(lane == 1, i2, 0))
    rw_ref[...] = jnp.where(lane == 0, p1, jnp.where(lane == 1, p2, 0.0))
    onehot = jnp.where((lane == i1) | (lane == i2), 1.0, 0.0)
    return jnp.sum(onehot, axis=0, keepdims=True)


def _merge_call(x2, ya, yb, proj, wa, wr, wo, nw, w_router, b_router, tm=1024):
    m = x2.shape[0]
    full = lambda shape: pl.BlockSpec(shape, lambda i: (0, 0))
    return pl.pallas_call(
        _merge_kernel,
        grid=(m // tm,),
        in_specs=[
            pl.BlockSpec((tm, D_MODEL), lambda i: (i, 0)),
            pl.BlockSpec((tm, GDN_V), lambda i: (i, 0)),
            pl.BlockSpec((tm, RET_V), lambda i: (i, 0)),
            pl.BlockSpec((tm, D_MODEL), lambda i: (i, 5)),
            pl.BlockSpec((tm, D_MODEL), lambda i: (i, 6)),
            full((GDN_V, D_MODEL)), full((RET_V, D_MODEL)), full((D_MODEL, D_MODEL)),
            full((1, D_MODEL)),
            full((D_MODEL, 2 * LANES)), full((1, LANES)),
        ],
        out_specs=[
            pl.BlockSpec((tm, D_MODEL), lambda i: (i, 0)),
            pl.BlockSpec((tm, HALF), lambda i: (i, 0)),
            pl.BlockSpec((tm, LANES), lambda i: (i, 0)),
            pl.BlockSpec((tm, LANES), lambda i: (i, 0)),
            pl.BlockSpec((8, LANES), lambda i: (0, 0)),
        ],
        out_shape=[
            jax.ShapeDtypeStruct((m, D_MODEL), F32),
            jax.ShapeDtypeStruct((m, HALF), I32),
            jax.ShapeDtypeStruct((m, LANES), I32),
            jax.ShapeDtypeStruct((m, LANES), F32),
            jax.ShapeDtypeStruct((8, LANES), F32),
        ],
        compiler_params=pltpu.CompilerParams(
            dimension_semantics=("arbitrary",), vmem_limit_bytes=VMEM_LIMIT),
        name="merge_router",
    )(x2, ya, yb, proj, proj, wa, wr, wo, nw, w_router, b_router)


def _slot_counts(m):
    n_slots = TOP_K * m + N_EXPERTS * SLOT_TILE
    return n_slots, n_slots // SLOT_TILE


def _lane_prefix_sum(x, lane):
    s = 1
    while s < LANES:
        x = x + jnp.where(lane >= s, pltpu.roll(x, s, axis=1), 0.0)
        s *= 2
    return x


def _plan_kernel(ridx_ref, cnt_ref, slots_ref, tile_ref, carry_s, off_s):
    i = pl.program_id(0)
    lane = lax.broadcasted_iota(I32, (PLAN_TILE, LANES), 1)
    row = lax.broadcasted_iota(I32, (PLAN_TILE, LANES), 0)
    lane1 = lane[0:1]

    @pl.when(i == 0)
    def _():
        cnt = cnt_ref[0:1, :]
        tile = float(SLOT_TILE)
        padded = jnp.floor((cnt + (tile - 1.0)) / tile) * tile
        incl = _lane_prefix_sum(padded, lane1)
        off = incl - padded
        off_s[...] = off
        carry_s[...] = jnp.zeros_like(carry_s)
        first = (row * SLOT_TILE).astype(F32)
        ended = jnp.where((lane < N_EXPERTS) & (incl <= first), 1.0, 0.0)
        tile_e = jnp.sum(ended, axis=-1, keepdims=True)
        last = jnp.sum(jnp.where(lane.astype(F32) == tile_e, off + cnt, 0.0), axis=-1, keepdims=True)
        used = jnp.clip(last - first[:, 0:1], 0.0, tile)
        tile_ref[...] = jnp.where(lane == 0, tile_e, jnp.where(lane == 1, used, 0.0)).astype(I32)

    strict = jnp.where(row[:, 0:1] > lax.broadcasted_iota(I32, (PLAN_TILE, PLAN_TILE), 1),
                       1.0, 0.0).astype(BF16)
    off = off_s[...]
    carry = carry_s[...]
    for sb in range(PLAN_STEP // PLAN_TILE):
        rows = pl.ds(sb * PLAN_TILE, PLAN_TILE)
        e1 = ridx_ref[rows, 0:1]
        e2 = ridx_ref[rows, 1:2]
        onehot = jnp.where((lane == e1) | (lane == e2), 1.0, 0.0)
        pos = _dot(strict, onehot.astype(BF16)) + (carry + off)
        s1 = jnp.sum(jnp.where(lane == e1, pos, 0.0), axis=-1, keepdims=True)
        s2 = jnp.sum(jnp.where(lane == e2, pos, 0.0), axis=-1, keepdims=True)
        both = jnp.where(lane == 0, s1, jnp.where(lane == 1, s2, 0.0))
        for q in range(PLAN_TILE // LANES):
            t = both[q * LANES:(q + 1) * LANES].T
            c0 = sb * PLAN_TILE + q * LANES
            slots_ref[:, c0:c0 + LANES] = t[0:8].astype(I32)
        carry = carry + jnp.sum(onehot, axis=0, keepdims=True)
    carry_s[...] = carry


def _plan_call(ridx, cnt):
    m = ridx.shape[0]
    _, n_tiles = _slot_counts(m)
    assert n_tiles <= PLAN_TILE and SLOT_TILE == PLAN_TILE
    return pl.pallas_call(
        _plan_kernel,
        grid=(m // PLAN_STEP,),
        in_specs=[pl.BlockSpec((PLAN_STEP, LANES), lambda i: (i, 0)),
                  pl.BlockSpec((8, LANES), lambda i: (0, 0))],
        out_specs=[
            pl.BlockSpec((8, PLAN_STEP), lambda i: (0, i)),
            pl.BlockSpec((PLAN_TILE, LANES), lambda i: (0, 0)),
        ],
        out_shape=[
            jax.ShapeDtypeStruct((8, m), I32),
            jax.ShapeDtypeStruct((PLAN_TILE, LANES), I32),
        ],
        scratch_shapes=[pltpu.VMEM((1, LANES), F32), pltpu.VMEM((1, LANES), F32)],
        compiler_params=pltpu.CompilerParams(dimension_semantics=("arbitrary",)),
        name="dispatch_plan",
    )(ridx, cnt)


def _sc_mesh():
    return plsc.VectorSubcoreMesh(core_axis_name="c", subcore_axis_name="s")


def _sc_worker():
    return lax.axis_index("s") * 2 + lax.axis_index("c")


def _sc_dispatch(xn, slot1, slot2, n_rows):
    m = xn.shape[0]
    per = m // SC_WORKERS
    n_pairs = per // (2 * SC_ROWS)

    @functools.partial(
        pl.kernel, mesh=_sc_mesh(),
        out_type=jax.ShapeDtypeStruct((n_rows, HALF), I32),
        scratch_types=[pltpu.VMEM((per // SC_ROWS, SC_ROWS), I32), pltpu.VMEM((per // SC_ROWS, SC_ROWS), I32),
                       pltpu.VMEM((SC_ROWS, HALF), I32), pltpu.VMEM((SC_ROWS, HALF), I32),
                       pltpu.SemaphoreType.DMA, pltpu.SemaphoreType.DMA, pltpu.SemaphoreType.DMA],
        name="sc_dispatch")
    def k(x_hbm, s1_hbm, s2_hbm, o_hbm, i1_v, i2_v, rows0, rows1, sem_r0, sem_r1, sem_w):
        wid = _sc_worker()
        base = wid * per

        def read(chunk, rows_v, sem):
            return pltpu.make_async_copy(x_hbm.at[pl.ds(base + chunk * SC_ROWS, SC_ROWS)], rows_v, sem)

        def scatter(chunk, rows_v):
            c1 = pltpu.async_copy(rows_v, o_hbm.at[i1_v.at[chunk]], sem_w)
            c2 = pltpu.async_copy(rows_v, o_hbm.at[i2_v.at[chunk]], sem_w)
            c1.wait()
            c2.wait()

        read(0, rows0, sem_r0).start()
        pltpu.sync_copy(s1_hbm.at[pl.ds(wid * (per // SC_ROWS), per // SC_ROWS)], i1_v)
        pltpu.sync_copy(s2_hbm.at[pl.ds(wid * (per // SC_ROWS), per // SC_ROWS)], i2_v)

        @pl.loop(0, n_pairs)
        def _(i):
            read(2 * i, rows0, sem_r0).wait()
            read(2 * i + 1, rows1, sem_r1).start()
            scatter(2 * i, rows0)
            read(2 * i + 1, rows1, sem_r1).wait()

            @pl.when(i + 1 < n_pairs)
            def _():
                read(2 * i + 2, rows0, sem_r0).start()

            scatter(2 * i + 1, rows1)

    return k(xn, slot1, slot2)


def _sc_collect(ys, slot1, slot2):
    m = slot1.size
    per = m // SC_WORKERS
    row = jax.ShapeDtypeStruct((m, HALF), I32)

    @functools.partial(
        pl.kernel, mesh=_sc_mesh(), out_type=[row, row],
        scratch_types=[pltpu.VMEM((per // SC_ROWS, SC_ROWS), I32), pltpu.VMEM((per // SC_ROWS, SC_ROWS), I32),
                       pltpu.VMEM((SC_ROWS, HALF), I32), pltpu.VMEM((SC_ROWS, HALF), I32),
                       pltpu.SemaphoreType.DMA, pltpu.SemaphoreType.DMA],
        name="sc_collect")
    def k(y_hbm, s1_hbm, s2_hbm, g1_hbm, g2_hbm, i1_v, i2_v, rows1, rows2, sem_g, sem_w):
        wid = _sc_worker()
        base = wid * per
        pltpu.sync_copy(s1_hbm.at[pl.ds(wid * (per // SC_ROWS), per // SC_ROWS)], i1_v)
        pltpu.sync_copy(s2_hbm.at[pl.ds(wid * (per // SC_ROWS), per // SC_ROWS)], i2_v)

        @pl.loop(0, per // SC_ROWS)
        def _(ci):
            t0 = base + ci * SC_ROWS
            a1 = pltpu.async_copy(y_hbm.at[i1_v.at[ci]], rows1, sem_g)
            a2 = pltpu.async_copy(y_hbm.at[i2_v.at[ci]], rows2, sem_g)
            a1.wait()
            a2.wait()
            w1 = pltpu.async_copy(rows1, g1_hbm.at[pl.ds(t0, SC_ROWS)], sem_w)
            w2 = pltpu.async_copy(rows2, g2_hbm.at[pl.ds(t0, SC_ROWS)], sem_w)
            w1.wait()
            w2.wait()

    return k(ys, slot1, slot2)


def _expert_kernel(te_ref, used_ref, xs_ref, wg_hbm, wu_hbm, wd_hbm, ys_ref, wg_b, wu_b, wd_b,
                   wg_f, wu_f, wd_f, sem, slot_s):
    j = pl.program_id(0)
    n = pl.num_programs(0)
    e = te_ref[j]
    prev = te_ref[jnp.maximum(j - 1, 0)]
    valid = e < N_EXPERTS

    def weight_copies(expert, slot):
        return [pltpu.make_async_copy(hbm.at[expert], buf.at[slot], sem.at[slot, i])
                for i, (hbm, buf) in enumerate(((wg_hbm, wg_f), (wu_hbm, wu_f), (wd_hbm, wd_f)))]

    @pl.when((j == 0) & valid)
    def _():
        slot_s[0] = 0
        for c in weight_copies(e, 0):
            c.start()

    @pl.when(((j == 0) | (e != prev)) & valid)
    def _():
        slot = slot_s[0]
        for c in weight_copies(e, slot):
            c.wait()
        k = lax.while_loop(lambda k: (k < n) & (te_ref[jnp.minimum(k, n - 1)] == e),
                           lambda k: k + 1, j + 1)
        nxt = te_ref[jnp.minimum(k, n - 1)]

        @pl.when((k < n) & (nxt < N_EXPERTS))
        def _():
            for c in weight_copies(nxt, 1 - slot):
                c.start()

        wg_b[...] = wg_f[slot].astype(BF16)
        wu_b[...] = wu_f[slot].astype(BF16)
        wd_b[...] = wd_f[slot].astype(BF16)
        slot_s[0] = 1 - slot

    @pl.when(valid)
    def _():
        half = SLOT_TILE // 2
        rows = [pl.ds(i * half, half) for i in range(2)]
        row_id = lax.broadcasted_iota(I32, (half, HALF), 0)
        xs = [_unpack_bf16_pairs(jnp.where(row_id + i * half < used_ref[j], xs_ref[r, :], 0))
              .astype(BF16) for i, r in enumerate(rows)]
        gs = [_dot(x, wg_b[...]) for x in xs]
        us = [_dot(x, wu_b[...]) for x in xs]
        hids = [(_silu(g) * u).astype(BF16) for g, u in zip(gs, us)]
        ys = [_dot(hid, wd_b[...]) for hid in hids]
        for r, y in zip(rows, ys):
            ys_ref[r, :] = _pack_bf16_pairs(y)

    @pl.when(e >= N_EXPERTS)
    def _():
        ys_ref[...] = jnp.zeros_like(ys_ref)


def _expert_call(tile_expert, tile_used, xs, wg, wu, wd, n_tiles):
    hbm = pl.BlockSpec(memory_space=pl.ANY)
    return pl.pallas_call(
        _expert_kernel,
        grid_spec=pltpu.PrefetchScalarGridSpec(
            num_scalar_prefetch=2,
            grid=(n_tiles,),
            in_specs=[pl.BlockSpec((SLOT_TILE, HALF), lambda j, te, used: (j, 0)), hbm, hbm, hbm],
            out_specs=pl.BlockSpec((SLOT_TILE, HALF), lambda j, te, used: (j, 0)),
            scratch_shapes=[
                pltpu.VMEM((D_MODEL, D_EXPERT), BF16), pltpu.VMEM((D_MODEL, D_EXPERT), BF16),
                pltpu.VMEM((D_EXPERT, D_MODEL), BF16),
                pltpu.VMEM((2, D_MODEL, D_EXPERT), F32), pltpu.VMEM((2, D_MODEL, D_EXPERT), F32),
                pltpu.VMEM((2, D_EXPERT, D_MODEL), F32),
                pltpu.SemaphoreType.DMA((2, 3)),
                pltpu.SMEM((1,), I32),
            ],
        ),
        out_shape=jax.ShapeDtypeStruct((n_tiles * SLOT_TILE, HALF), I32),
        compiler_params=pltpu.CompilerParams(
            dimension_semantics=("arbitrary",), vmem_limit_bytes=VMEM_LIMIT),
        name="experts",
    )(tile_expert, tile_used, xs, wg, wu, wd)


def _final_kernel(h_ref, g1_ref, g2_ref, rw_ref, nw_ref, o_ref):
    rw = rw_ref[...]
    y = rw[:, 0:1] * _unpack_bf16_pairs(g1_ref[...]) + rw[:, 1:2] * _unpack_bf16_pairs(g2_ref[...])
    h = h_ref[...] + y
    o_ref[...] = h * lax.rsqrt(jnp.mean(h * h, axis=-1, keepdims=True) + EPS) * nw_ref[...]


def _final_call(h1, g1, g2, rw, nw, tm=1024):
    m = h1.shape[0]
    return pl.pallas_call(
        _final_kernel,
        grid=(m // tm,),
        in_specs=[
            pl.BlockSpec((tm, D_MODEL), lambda i: (i, 0)),
            pl.BlockSpec((tm, HALF), lambda i: (i, 0)),
            pl.BlockSpec((tm, HALF), lambda i: (i, 0)),
            pl.BlockSpec((tm, LANES), lambda i: (i, 0)),
            pl.BlockSpec((1, D_MODEL), lambda i: (0, 0)),
        ],
        out_specs=pl.BlockSpec((tm, D_MODEL), lambda i: (i, 0)),
        out_shape=jax.ShapeDtypeStruct((m, D_MODEL), F32),
        compiler_params=pltpu.CompilerParams(dimension_semantics=("arbitrary",)),
        name="combine_final",
    )(h1, g1, g2, rw, nw)


def _pad_lanes(a):
    return jnp.pad(a, ((0, 0), (0, LANES - a.shape[1])))


def kernel(x, norm_mix_w, w_in, conv_w, A_log, dt_bias, gdn_norm_w, w_up_gdn, w_up_ret, w_out,
           norm_ffn_w, w_group, b_group, w_expert, b_expert, w_gate, w_up, w_down, norm_final_w):
    batch, seq, d = x.shape
    m = batch * seq
    h = x.reshape(m, d)
    depth = w_in.shape[0]
    sin, cos, inner, kdec, qdec, cdec = _retention_tables(seq)
    for l in range(depth):
        w_main, w_ab = _repack_call(w_in[l])
        proj, ab = _proj_call(h, norm_mix_w[l][None, :], w_main, w_ab)

        conv8 = jnp.pad(conv_w[l], ((0, 8 - GDN_CONV), (0, 0)))
        ya = _gdn_call(proj, ab, conv8, _pad_lanes(A_log[l][None, :]), _pad_lanes(dt_bias[l][None, :]),
                       gdn_norm_w[l][None, :], batch, seq)
        yb = _ret_call(proj, sin, cos, inner, kdec, qdec, cdec, batch, seq)

        w_router = _pad_lanes(jnp.concatenate([w_expert[l], w_group[l]], axis=1))
        wr_hi = w_router.astype(BF16)
        wr_lo = (w_router - wr_hi.astype(F32)).astype(BF16)
        b_router = _pad_lanes(jnp.concatenate([b_expert[l], b_group[l]])[None, :])
        h1, xn, ridx, rw, cnt = _merge_call(
            h, ya, yb, proj, w_up_gdn[l].astype(BF16), w_up_ret[l].astype(BF16), w_out[l].astype(BF16),
            norm_ffn_w[l][None, :], jnp.concatenate([wr_hi, wr_lo], axis=1), b_router)

        n_slots, n_tiles = _slot_counts(m)
        slots, tiles = _plan_call(ridx, cnt)
        slot1, slot2 = slots[0].reshape(-1, SC_ROWS), slots[1].reshape(-1, SC_ROWS)
        xs = _sc_dispatch(xn, slot1, slot2, n_slots)
        ys = _expert_call(tiles[:n_tiles, 0], tiles[:n_tiles, 1], xs,
                          w_gate[l], w_up[l], w_down[l], n_tiles)
        g1, g2 = _sc_collect(ys, slot1, slot2)

        assert depth == 1
        h = _final_call(h1, g1, g2, rw, norm_final_w[None, :])
    return h.reshape(batch, seq, d)
```

```python
import functools
import math

import jax
import jax.numpy as jnp
from jax import lax
from jax.experimental import pallas as pl
from jax.experimental.pallas import tpu as pltpu
from jax.experimental.pallas import tpu_sc as plsc

F32 = jnp.float32
BF16 = jnp.bfloat16
I32 = jnp.int32
U32 = jnp.uint32

D_MODEL = 1024
EPS = 1e-6
GDN_HEADS = 4
GDN_DK = 128
GDN_DV = 128
GDN_CONV = 4
RET_HEADS = 4
RET_DK = 128
RET_DV = 256
ROPE_BASE = 10000.0
N_GROUPS = 4
EXPERTS_PER_GROUP = 8
N_EXPERTS = N_GROUPS * EXPERTS_PER_GROUP
D_EXPERT = 512

GDN_QK = GDN_HEADS * GDN_DK
GDN_V = GDN_HEADS * GDN_DV
RET_QK = RET_HEADS * RET_DK
RET_V = RET_HEADS * RET_DV

LANES = 128
CHUNK = 128
INV_BLOCK = 16
GDN_PREP_CHUNKS = 2
VMEM_LIMIT = 56 * 1024 * 1024

REPACK_COLS = 512
MERGE_PARTS = 1
TOP_K = 2
SLOT_TILE = 256
PLAN_TILE = 256
PLAN_STEP = 1024
HALF = D_MODEL // 2
SC_WORKERS = 32
SC_ROWS = 64

PROJ_COLS = 3 * GDN_QK + GDN_V + 2 * RET_QK + 2 * RET_V + 2 * D_MODEL


def _silu(x):
    return x / (1.0 + jnp.exp(-x))


def _sigmoid(x):
    return 1.0 / (1.0 + jnp.exp(-x))


def _dot(a, b):
    return jnp.dot(a, b, preferred_element_type=F32)


def _dot_nt(a, b):
    return lax.dot_general(a, b, (((1,), (1,)), ((), ())), preferred_element_type=F32)


def _pack_bf16_pairs(x):
    bits = lax.bitcast_convert_type(x.astype(BF16).astype(F32), U32)
    packed = (bits[:, :HALF] >> 16) | (bits[:, HALF:] & jnp.uint32(0xFFFF0000))
    return lax.bitcast_convert_type(packed, I32)


def _unpack_bf16_pairs(p):
    p = lax.bitcast_convert_type(p, U32)
    lo = lax.bitcast_convert_type(p << 16, F32)
    hi = lax.bitcast_convert_type(p & jnp.uint32(0xFFFF0000), F32)
    return jnp.concatenate([lo, hi], axis=1)


def _proj_kernel(x_ref, nw_ref, w_ref, wab_ref, proj_ref, ab_ref, u_ref):
    j = pl.program_id(1)

    @pl.when(j == 0)
    def _():
        x = x_ref[...]
        u = x * lax.rsqrt(jnp.mean(x * x, axis=-1, keepdims=True) + EPS) * nw_ref[...]
        ub = u.astype(BF16)
        u_ref[...] = ub
        ab_ref[...] = _dot(ub, wab_ref[...])

    proj_ref[...] = _dot(u_ref[...], w_ref[...]).astype(BF16)


def _repack_kernel(wt_ref, abt_ref, main_ref, ab_ref):
    main_ref[...] = wt_ref[...].T.astype(BF16)

    @pl.when(pl.program_id(0) == 0)
    def _():
        ab = abt_ref[...].T
        ab_ref[...] = jnp.concatenate(
            [ab, jnp.zeros((ab.shape[0], LANES - ab.shape[1]), F32)], axis=1).astype(BF16)


def _repack_call(w_in_t):
    d_in, d = w_in_t.shape
    o_ab = 3 * GDN_QK
    n_ab = 2 * GDN_HEADS
    src = lambda r: pl.multiple_of(r * REPACK_COLS + jnp.where(r * REPACK_COLS >= o_ab, n_ab, 0), 8)
    return pl.pallas_call(
        _repack_kernel,
        grid=(PROJ_COLS // REPACK_COLS,),
        in_specs=[pl.BlockSpec((pl.Element(REPACK_COLS), pl.Element(d)), lambda r: (src(r), 0)),
                  pl.BlockSpec((pl.Element(n_ab), pl.Element(d)), lambda r: (o_ab, 0))],
        out_specs=[pl.BlockSpec((d, REPACK_COLS), lambda r: (0, r)),
                   pl.BlockSpec((d, LANES), lambda r: (0, 0))],
        out_shape=[jax.ShapeDtypeStruct((d, PROJ_COLS), BF16), jax.ShapeDtypeStruct((d, LANES), BF16)],
        compiler_params=pltpu.CompilerParams(dimension_semantics=("arbitrary",)),
        name="repack_w_in",
    )(w_in_t, w_in_t)


def _proj_call(x2, norm_w, w_main, w_ab, tm=1024, tn=3584):
    m = x2.shape[0]
    return pl.pallas_call(
        _proj_kernel,
        grid=(m // tm, PROJ_COLS // tn),
        in_specs=[
            pl.BlockSpec((tm, D_MODEL), lambda i, j: (i, 0)),
            pl.BlockSpec((1, D_MODEL), lambda i, j: (0, 0)),
            pl.BlockSpec((D_MODEL, tn), lambda i, j: (0, j)),
            pl.BlockSpec((D_MODEL, LANES), lambda i, j: (0, 0)),
        ],
        out_specs=[
            pl.BlockSpec((tm, tn), lambda i, j: (i, j)),
            pl.BlockSpec((tm, LANES), lambda i, j: (i, 0)),
        ],
        out_shape=[
            jax.ShapeDtypeStruct((m, PROJ_COLS), BF16),
            jax.ShapeDtypeStruct((m, LANES), F32),
        ],
        scratch_shapes=[pltpu.VMEM((tm, D_MODEL), BF16)],
        compiler_params=pltpu.CompilerParams(
            dimension_semantics=("arbitrary", "arbitrary"), vmem_limit_bytes=VMEM_LIMIT),
        name="proj",
    )(x2, norm_w, w_main, w_ab)


def _unit_lower_inverses(lows, ii, jj):
    eye = jnp.where(ii == jj, 1.0, 0.0).astype(F32)
    in_block = (ii // INV_BLOCK) == (jj // INV_BLOCK)
    ps = [jnp.where(in_block, -low, 0.0) for low in lows]
    ts = [eye + p for p in ps]
    span = 2
    while span < INV_BLOCK:
        ps = [_dot(p, p) for p in ps]
        ts = [t + _dot(t, p) for t, p in zip(ts, ps)]
        span *= 2
    s = INV_BLOCK
    while s < CHUNK:
        off_diag = ((ii // (2 * s)) == (jj // (2 * s))) & ((ii // s) != (jj // s))
        xs = [_dot(jnp.where(off_diag, low, 0.0), t) for low, t in zip(lows, ts)]
        ts = [t - _dot(t, x) for t, x in zip(ts, xs)]
        s *= 2
    return ts


def _gdn_kernel(qkv_ref, z_ref, ab_ref, convw_ref, alog_ref, dtb_ref, normw_ref, o_ref,
                b_s, o0_s, m_s, qp_s, gl_s, state_s, q_s, k_s, kb_s, rhs_s, dec_s, cv_s, qg_o, kdt_o,
                *, seq):
    nchunk = seq // CHUNK
    ii = lax.broadcasted_iota(jnp.int32, (CHUNK, CHUNK), 0)
    jj = lax.broadcasted_iota(jnp.int32, (CHUNK, CHUNK), 1)
    causal = ii >= jj
    strict = ii > jj
    tri = jnp.where(causal, 1.0, 0.0).astype(F32)
    neg_a = -jnp.exp(alog_ref[...])
    dtb = dtb_ref[...]

    def conv_cols(c, r0, lo, buf):
        x = qkv_ref[pl.ds(r0, CHUNK), lo:lo + LANES].astype(F32)
        prev0 = pl.multiple_of(jnp.maximum(r0 - 16, 0), 16)
        prev = qkv_ref[pl.ds(prev0, 16), lo:lo + LANES].astype(F32)
        buf[0:8, :] = prev[8:16] * jnp.where(c > 0, 1.0, 0.0)
        buf[8:8 + CHUNK, :] = x
        w = convw_ref[:, lo:lo + LANES]
        y = (w[3:4] * x + w[2:3] * buf[7:7 + CHUNK, :] + w[1:2] * buf[6:6 + CHUNK, :]
             + w[0:1] * buf[5:5 + CHUNK, :])
        return _silu(y)

    def l2n(x):
        return x * lax.rsqrt(jnp.sum(x * x, axis=-1, keepdims=True) + EPS)

    tri_b = tri.astype(BF16)

    def chunk_cumsum(g):
        g1 = g.astype(BF16)
        r1 = g - g1.astype(F32)
        g2 = r1.astype(BF16)
        g3 = (r1 - g2.astype(F32)).astype(BF16)
        return _dot(tri_b, g1) + (_dot(tri_b, g2) + _dot(tri_b, g3))

    def operands(cc, slot):
        for sub in range(GDN_PREP_CHUNKS):
            c = cc * GDN_PREP_CHUNKS + sub
            r0 = pl.multiple_of(c * CHUNK, CHUNK)
            ab = ab_ref[pl.ds(r0, CHUNK), :]
            xg = ab + dtb
            softplus = jnp.maximum(xg, 0.0) + jnp.log(1.0 + jnp.exp(-jnp.abs(xg)))
            g_all = neg_a * softplus
            beta_all = _sigmoid(ab)
            gc_all = chunk_cumsum(g_all)
            gc_t = gc_all.T
            gl_s[c] = jnp.exp(gc_all[CHUNK - 1:CHUNK, :])
            for h in range(GDN_HEADS):
                n = sub * GDN_HEADS + h
                bufs = [cv_s.at[slot, 3 * n + i] for i in range(3)]
                q = l2n(conv_cols(c, r0, h * GDN_DK, bufs[0])) * (GDN_DK ** -0.5)
                k = l2n(conv_cols(c, r0, GDN_QK + h * GDN_DK, bufs[1]))
                v = conv_cols(c, r0, 2 * GDN_QK + h * GDN_DV, bufs[2])
                gcol = gc_all[:, h:h + 1]
                grow = gc_t[h:h + 1, :]
                beta = beta_all[:, GDN_HEADS + h:GDN_HEADS + h + 1]
                dec_s[slot, n] = jnp.where(causal, jnp.exp(gcol - grow), 0.0)
                eg = jnp.exp(gcol)
                kb = k * beta
                q_s[slot, n] = q.astype(BF16)
                k_s[slot, n] = k.astype(BF16)
                kb_s[slot, n] = kb.astype(BF16)
                rhs_s[slot, n] = jnp.concatenate([v * beta, kb * eg], axis=1).astype(BF16)
                qg_o[slot, n] = q * eg
                kd = k * jnp.exp(gc_all[CHUNK - 1:CHUNK, h:h + 1] - gcol)
                kdt_o[slot, n] = kd.T.astype(BF16)

    def solve(cc, slot, between):
        items = [(cc * GDN_PREP_CHUNKS + sub, h, sub * GDN_HEADS + h)
                 for sub in range(GDN_PREP_CHUNKS) for h in range(GDN_HEADS)]
        kks = [_dot_nt(kb_s[slot, n], k_s[slot, n]) for _, _, n in items]
        qks = [_dot_nt(q_s[slot, n], k_s[slot, n]) for _, _, n in items]
        lows = [jnp.where(strict, kk * dec_s[slot, n], 0.0) for kk, (_, _, n) in zip(kks, items)]
        attns = [(qk * dec_s[slot, n]).astype(BF16) for qk, (_, _, n) in zip(qks, items)]
        for step in between[:len(between) // 2]:
            step()
        ts = _unit_lower_inverses(lows, ii, jj)
        for step in between[len(between) // 2:]:
            step()
        uws = [_dot(t.astype(BF16), rhs_s[slot, n]).astype(BF16) for t, (_, _, n) in zip(ts, items)]
        kds = [_dot(kdt_o[slot, n], uw) for uw, (_, _, n) in zip(uws, items)]
        ats = [_dot(attn, uw) for attn, uw in zip(attns, uws)]
        for kd_uw, at_uw, (c, h, n) in zip(kds, ats, items):
            b_s[c, h] = kd_uw[:, :GDN_DV]
            m_s[c, h] = (-kd_uw[:, GDN_DV:]).astype(BF16)
            o0_s[c, h] = at_uw[:, :GDN_DV]
            qp_s[c, h] = (qg_o[slot, n] - at_uw[:, GDN_DV:]).astype(BF16)

    ngroup = nchunk // GDN_PREP_CHUNKS
    per_trip = 2 * GDN_PREP_CHUNKS
    normw = normw_ref[...]
    state_s[...] = jnp.zeros_like(state_s)
    for c0 in range(per_trip):
        gl_s[c0] = jnp.zeros((1, LANES), F32)
        for h in range(GDN_HEADS):
            b_s[c0, h] = jnp.zeros((CHUNK, GDN_DV), F32)
            o0_s[c0, h] = jnp.zeros((CHUNK, GDN_DV), F32)
            m_s[c0, h] = jnp.zeros((CHUNK, GDN_DK), BF16)
            qp_s[c0, h] = jnp.zeros((CHUNK, GDN_DK), BF16)
    operands(0, 0)

    def prep(i, carry):
        first = jnp.maximum(per_trip * (i - 1), 0)
        steps = [functools.partial(scan, first + k, 0) for k in range(per_trip)]
        operands(2 * i + 1, 1)
        solve(2 * i, 0, steps[:per_trip // 2])
        operands(jnp.minimum(2 * i + 2, ngroup - 1), 0)
        solve(2 * i + 1, 1, steps[per_trip // 2:])
        return carry

    def scan(c, carry):
        r0 = pl.multiple_of(c * CHUNK, CHUNK)
        gl = gl_s[c]
        heads = range(GDN_HEADS)
        ss = [state_s[h] for h in heads]
        sbs = [s.astype(BF16) for s in ss]
        mss = [_dot(m_s[c, h], sbs[h]) for h in heads]
        qss = [_dot(qp_s[c, h], sbs[h]) for h in heads]
        for h in heads:
            state_s[h] = ss[h] * gl[:, h:h + 1] + (mss[h] + b_s[c, h])
            o = qss[h] + o0_s[c, h]
            z = z_ref[pl.ds(r0, CHUNK), h * GDN_DV:(h + 1) * GDN_DV].astype(F32)
            on = o * lax.rsqrt(jnp.mean(o * o, axis=-1, keepdims=True) + EPS) * normw
            o_ref[pl.ds(r0, CHUNK), h * GDN_DV:(h + 1) * GDN_DV] = (on * _silu(z)).astype(BF16)
        return carry

    ntrip = ngroup // 2
    lax.fori_loop(0, ntrip, prep, 0)
    lax.fori_loop(per_trip * (ntrip - 1), nchunk, scan, 0)


def _gdn_call(proj, ab, conv_w8, alog_p, dtb_p, normw, batch, seq):
    nchunk = seq // CHUNK
    hs = (nchunk, GDN_HEADS, CHUNK, CHUNK)
    ops = (2, GDN_PREP_CHUNKS * GDN_HEADS, CHUNK, CHUNK)
    return pl.pallas_call(
        functools.partial(_gdn_kernel, seq=seq),
        grid=(batch,),
        in_specs=[
            pl.BlockSpec((seq, 3 * GDN_QK), lambda b: (b, 0)),
            pl.BlockSpec((seq, GDN_V), lambda b: (b, 3)),
            pl.BlockSpec((seq, LANES), lambda b: (b, 0)),
            pl.BlockSpec((8, 3 * GDN_QK), lambda b: (0, 0)),
            pl.BlockSpec((1, LANES), lambda b: (0, 0)),
            pl.BlockSpec((1, LANES), lambda b: (0, 0)),
            pl.BlockSpec((1, GDN_DV), lambda b: (0, 0)),
        ],
        out_specs=pl.BlockSpec((seq, GDN_V), lambda b: (b, 0)),
        out_shape=jax.ShapeDtypeStruct((batch * seq, GDN_V), BF16),
        scratch_shapes=[
            pltpu.VMEM(hs, F32), pltpu.VMEM(hs, F32), pltpu.VMEM(hs, BF16), pltpu.VMEM(hs, BF16),
            pltpu.VMEM((nchunk, 1, LANES), F32),
            pltpu.VMEM((GDN_HEADS, GDN_DK, GDN_DV), F32),
            pltpu.VMEM(ops, BF16), pltpu.VMEM(ops, BF16), pltpu.VMEM(ops, BF16),
            pltpu.VMEM(ops[:3] + (2 * CHUNK,), BF16),
            pltpu.VMEM(ops, F32),
            pltpu.VMEM((2, 3 * ops[1], 8 + CHUNK, LANES), F32),
            pltpu.VMEM(ops, F32), pltpu.VMEM(ops, BF16),
        ],
        compiler_params=pltpu.CompilerParams(
            dimension_semantics=("arbitrary",), vmem_limit_bytes=VMEM_LIMIT),
        name="gdn",
    )(proj, proj, ab, conv_w8, alog_p, dtb_p, normw)


def _ret_kernel(qk_ref, v_ref, g_ref, sin_ref, cos_ref, inner_ref, kdec_ref, qdec_ref, cdec_ref,
                o_ref, state_s, q_s, qd_s, k_s, kt_s, *, seq):
    nchunk = seq // CHUNK
    lane = lax.broadcasted_iota(jnp.int32, (CHUNK, RET_DK), 1)
    even = (lane % 2) == 0

    def rotate(x, sin, cos):
        nxt = pltpu.roll(x, RET_DK - 1, axis=1)
        prv = pltpu.roll(x, 1, axis=1)
        return x * cos + jnp.where(even, -nxt, prv) * sin

    state_s[...] = jnp.zeros_like(state_s)
    kdec = kdec_ref[...]
    qdec = qdec_ref[...]
    cdec = cdec_ref[...]

    heads = range(RET_HEADS)

    def operands(c, slot):
        r0 = pl.multiple_of(c * CHUNK, CHUNK)
        sin = sin_ref[pl.ds(r0, CHUNK), :]
        cos = cos_ref[pl.ds(r0, CHUNK), :]
        for h in heads:
            q = rotate(qk_ref[pl.ds(r0, CHUNK), h * RET_DK:(h + 1) * RET_DK].astype(F32), sin, cos)
            k = rotate(qk_ref[pl.ds(r0, CHUNK), RET_QK + h * RET_DK:RET_QK + (h + 1) * RET_DK]
                       .astype(F32), sin, cos) * (RET_DK ** -0.5)
            q_s[slot, h] = q.astype(BF16)
            qd_s[slot, h] = (q * qdec[:, h:h + 1]).astype(BF16)
            k_s[slot, h] = k.astype(BF16)
            kt_s[slot, h] = (k * kdec[:, h:h + 1]).T.astype(BF16)

    def outputs(c, slot):
        r0 = pl.multiple_of(c * CHUNK, CHUNK)
        vs = [v_ref[pl.ds(r0, CHUNK), h * RET_DV:(h + 1) * RET_DV] for h in heads]
        ss = [state_s[h] for h in heads]
        qks = [_dot_nt(q_s[slot, h], k_s[slot, h]) for h in heads]
        inters = [_dot(qd_s[slot, h], ss[h].astype(BF16)) for h in heads]
        kvs = [_dot(kt_s[slot, h], vs[h]) for h in heads]
        intras = [_dot((qks[h] * inner_ref[h]).astype(BF16), vs[h]) for h in heads]
        for h in heads:
            state_s[h] = ss[h] * cdec[:, h:h + 1] + kvs[h]
            o = intras[h] + inters[h]
            gate = g_ref[pl.ds(r0, CHUNK), h * RET_DV:(h + 1) * RET_DV].astype(F32)
            on = o * lax.rsqrt(jnp.mean(o * o, axis=-1, keepdims=True) + EPS)
            o_ref[pl.ds(r0, CHUNK), h * RET_DV:(h + 1) * RET_DV] = (on * _silu(gate)).astype(BF16)

    operands(0, 0)

    def body(i, carry):
        operands(2 * i + 1, 1)
        outputs(2 * i, 0)
        operands(jnp.minimum(2 * i + 2, nchunk - 1), 0)
        outputs(2 * i + 1, 1)
        return carry

    lax.fori_loop(0, nchunk // 2, body, 0)


def _ret_call(proj, sin, cos, inner, kdec, qdec, cdec, batch, seq):
    return pl.pallas_call(
        functools.partial(_ret_kernel, seq=seq),
        grid=(batch,),
        in_specs=[
            pl.BlockSpec((seq, 2 * RET_QK), lambda b: (b, 2)),
            pl.BlockSpec((seq, RET_V), lambda b: (b, 3)),
            pl.BlockSpec((seq, RET_V), lambda b: (b, 4)),
            pl.BlockSpec((seq, RET_DK), lambda b: (0, 0)),
            pl.BlockSpec((seq, RET_DK), lambda b: (0, 0)),
            pl.BlockSpec((RET_HEADS, CHUNK, CHUNK), lambda b: (0, 0, 0)),
            pl.BlockSpec((CHUNK, LANES), lambda b: (0, 0)),
            pl.BlockSpec((CHUNK, LANES), lambda b: (0, 0)),
            pl.BlockSpec((1, LANES), lambda b: (0, 0)),
        ],
        out_specs=pl.BlockSpec((seq, RET_V), lambda b: (b, 0)),
        out_shape=jax.ShapeDtypeStruct((batch * seq, RET_V), BF16),
        scratch_shapes=[pltpu.VMEM((RET_HEADS, RET_DK, RET_DV), F32)]
        + [pltpu.VMEM((2, RET_HEADS, CHUNK, RET_DK), BF16)] * 4,
        compiler_params=pltpu.CompilerParams(
            dimension_semantics=("arbitrary",), vmem_limit_bytes=VMEM_LIMIT),
        name="retention",
    )(proj, proj, proj, sin, cos, inner, kdec, qdec, cdec)


def _retention_tables(seq):
    inv_freq = 1.0 / (ROPE_BASE ** jnp.linspace(0.0, 1.0, RET_DK // 2, dtype=F32))
    ang = jnp.arange(seq, dtype=F32)[:, None] * inv_freq[None, :]
    sin = jnp.repeat(jnp.sin(ang), 2, axis=-1)
    cos = jnp.repeat(jnp.cos(ang), 2, axis=-1)
    log_gamma = jnp.log(1.0 - 2.0 ** (-5.0 - jnp.arange(RET_HEADS, dtype=F32)))
    idx = jnp.arange(CHUNK, dtype=F32)
    causal = jnp.tril(jnp.ones((CHUNK, CHUNK), dtype=bool))
    rel = jnp.where(causal, idx[:, None] - idx[None, :], 0.0)
    inner = jnp.where(causal, jnp.exp(rel[None] * log_gamma[:, None, None]), 0.0)
    k_decay = jnp.exp(log_gamma[:, None] * (CHUNK - 1.0 - idx)[None, :])
    q_decay = jnp.exp(log_gamma[:, None] * (idx + 1.0)[None, :])
    chunk_decay = jnp.exp(log_gamma * CHUNK)
    pad = LANES - RET_HEADS
    kdec = jnp.pad(k_decay.T, ((0, 0), (0, pad)))
    qdec = jnp.pad(q_decay.T, ((0, 0), (0, pad)))
    cdec = jnp.pad(chunk_decay[None, :], ((0, 0), (0, pad)))
    return sin, cos, inner, kdec, qdec, cdec


def _merge_kernel(x_ref, ya_ref, yb_ref, ma_ref, mb_ref, wa_ref, wr_ref, wo_ref, nw_ref,
                  wrt_ref, br_ref, h_ref, xn_ref, ridx_ref, rw_ref, cnt_ref):
    tm = x_ref.shape[0] // MERGE_PARTS
    rows = [pl.ds(i * tm, tm) for i in range(MERGE_PARTS)]
    a_ = [_dot(ya_ref[r, :], wa_ref[...]) for r in rows]
    r_ = [_dot(yb_ref[r, :], wr_ref[...]) for r in rows]
    merged = [(_sigmoid(ma_ref[r, :].astype(F32)) * a + _sigmoid(mb_ref[r, :].astype(F32)) * rr)
              .astype(BF16) for r, a, rr in zip(rows, a_, r_)]
    hs = [x_ref[r, :] + _dot(mg, wo_ref[...]) for r, mg in zip(rows, merged)]
    xcats = []
    for r, h in zip(rows, hs):
        h_ref[r, :] = h
        xn = h * lax.rsqrt(jnp.mean(h * h, axis=-1, keepdims=True) + EPS) * nw_ref[...]
        xn_ref[r, :] = _pack_bf16_pairs(xn)
        xh = xn.astype(BF16)
        xl = (xn - xh.astype(F32)).astype(BF16)
        xcats.append(jnp.concatenate([xh, xl], axis=0))
    parts_ = [_dot(xc, wrt_ref[...]) for xc in xcats]
    counts = jnp.zeros((1, LANES), F32)
    for r, parts in zip(rows, parts_):
        counts = counts + _route(parts, tm, br_ref[...], ridx_ref.at[r, :], rw_ref.at[r, :])

    @pl.when(pl.program_id(0) == 0)
    def _():
        cnt_ref[...] = jnp.zeros_like(cnt_ref)

    cnt_ref[...] += jnp.broadcast_to(counts, cnt_ref.shape)


def _route(parts, tm, bias, ridx_ref, rw_ref):
    logits = (parts[:tm, :LANES] + (parts[tm:, :LANES] + parts[:tm, LANES:]
                                    + parts[tm:, LANES:])) + bias
    lane = lax.broadcasted_iota(jnp.int32, (tm, LANES), 1)
    neg = -jnp.inf
    gl = jnp.where((lane >= N_EXPERTS) & (lane < N_EXPERTS + N_GROUPS), logits, neg)
    gmax = jnp.max(gl, axis=-1, keepdims=True)
    gidx = jnp.min(jnp.where(gl == gmax, lane, LANES), axis=-1, keepdims=True) - N_EXPERTS
    g_w = 1.0 / jnp.sum(jnp.exp(gl - gmax), axis=-1, keepdims=True)
    el = jnp.where((lane // EXPERTS_PER_GROUP == gidx) & (lane < N_EXPERTS), logits, neg)
    m1 = jnp.max(el, axis=-1, keepdims=True)
    i1 = jnp.min(jnp.where(el == m1, lane, LANES), axis=-1, keepdims=True)
    el2 = jnp.where(lane == i1, neg, el)
    m2 = jnp.max(el2, axis=-1, keepdims=True)
    i2 = jnp.min(jnp.where(el2 == m2, lane, LANES), axis=-1, keepdims=True)
    e2 = jnp.exp(m2 - m1)
    p1 = g_w / (1.0 + e2)
    p2 = g_w * e2 / (1.0 + e2)
    ridx_ref[...] = jnp.where(lane == 0, i1, jnp.where(lane == 1, i2, 0))
    rw_ref[...] = jnp.where(lane == 0, p1, jnp.where(lane == 1, p2, 0.0))
    onehot = jnp.where((lane == i1) | (lane == i2), 1.0, 0.0)
    return jnp.sum(onehot, axis=0, keepdims=True)


def _merge_call(x2, ya, yb, proj, wa, wr, wo, nw, w_router, b_router, tm=1024):
    m = x2.shape[0]
    full = lambda shape: pl.BlockSpec(shape, lambda i: (0, 0))
    return pl.pallas_call(
        _merge_kernel,
        grid=(m // tm,),
        in_specs=[
            pl.BlockSpec((tm, D_MODEL), lambda i: (i, 0)),
            pl.BlockSpec((tm, GDN_V), lambda i: (i, 0)),
            pl.BlockSpec((tm, RET_V), lambda i: (i, 0)),
            pl.BlockSpec((tm, D_MODEL), lambda i: (i, 5)),
            pl.BlockSpec((tm, D_MODEL), lambda i: (i, 6)),
            full((GDN_V, D_MODEL)), full((RET_V, D_MODEL)), full((D_MODEL, D_MODEL)),
            full((1, D_MODEL)),
            full((D_MODEL, 2 * LANES)), full((1, LANES)),
        ],
        out_specs=[
            pl.BlockSpec((tm, D_MODEL), lambda i: (i, 0)),
            pl.BlockSpec((tm, HALF), lambda i: (i, 0)),
            pl.BlockSpec((tm, LANES), lambda i: (i, 0)),
            pl.BlockSpec((tm, LANES), lambda i: (i, 0)),
            pl.BlockSpec((8, LANES), lambda i: (0, 0)),
        ],
        out_shape=[
            jax.ShapeDtypeStruct((m, D_MODEL), F32),
            jax.ShapeDtypeStruct((m, HALF), I32),
            jax.ShapeDtypeStruct((m, LANES), I32),
            jax.ShapeDtypeStruct((m, LANES), F32),
            jax.ShapeDtypeStruct((8, LANES), F32),
        ],
        compiler_params=pltpu.CompilerParams(
            dimension_semantics=("arbitrary",), vmem_limit_bytes=VMEM_LIMIT),
        name="merge_router",
    )(x2, ya, yb, proj, proj, wa, wr, wo, nw, w_router, b_router)


def _slot_counts(m):
    n_slots = TOP_K * m + N_EXPERTS * SLOT_TILE
    return n_slots, n_slots // SLOT_TILE


def _lane_prefix_sum(x, lane):
    s = 1
    while s < LANES:
        x = x + jnp.where(lane >= s, pltpu.roll(x, s, axis=1), 0.0)
        s *= 2
    return x


def _plan_kernel(ridx_ref, cnt_ref, slots_ref, tile_ref, carry_s, off_s):
    i = pl.program_id(0)
    lane = lax.broadcasted_iota(I32, (PLAN_TILE, LANES), 1)
    row = lax.broadcasted_iota(I32, (PLAN_TILE, LANES), 0)
    lane1 = lane[0:1]

    @pl.when(i == 0)
    def _():
        cnt = cnt_ref[0:1, :]
        tile = float(SLOT_TILE)
        padded = jnp.floor((cnt + (tile - 1.0)) / tile) * tile
        incl = _lane_prefix_sum(padded, lane1)
        off = incl - padded
        off_s[...] = off
        carry_s[...] = jnp.zeros_like(carry_s)
        first = (row * SLOT_TILE).astype(F32)
        ended = jnp.where((lane < N_EXPERTS) & (incl <= first), 1.0, 0.0)
        tile_e = jnp.sum(ended, axis=-1, keepdims=True)
        last = jnp.sum(jnp.where(lane.astype(F32) == tile_e, off + cnt, 0.0), axis=-1, keepdims=True)
        used = jnp.clip(last - first[:, 0:1], 0.0, tile)
        tile_ref[...] = jnp.where(lane == 0, tile_e, jnp.where(lane == 1, used, 0.0)).astype(I32)

    strict = jnp.where(row[:, 0:1] > lax.broadcasted_iota(I32, (PLAN_TILE, PLAN_TILE), 1),
                       1.0, 0.0).astype(BF16)
    off = off_s[...]
    carry = carry_s[...]
    for sb in range(PLAN_STEP // PLAN_TILE):
        rows = pl.ds(sb * PLAN_TILE, PLAN_TILE)
        e1 = ridx_ref[rows, 0:1]
        e2 = ridx_ref[rows, 1:2]
        onehot = jnp.where((lane == e1) | (lane == e2), 1.0, 0.0)
        pos = _dot(strict, onehot.astype(BF16)) + (carry + off)
        s1 = jnp.sum(jnp.where(lane == e1, pos, 0.0), axis=-1, keepdims=True)
        s2 = jnp.sum(jnp.where(lane == e2, pos, 0.0), axis=-1, keepdims=True)
        both = jnp.where(lane == 0, s1, jnp.where(lane == 1, s2, 0.0))
        for q in range(PLAN_TILE // LANES):
            t = both[q * LANES:(q + 1) * LANES].T
            c0 = sb * PLAN_TILE + q * LANES
            slots_ref[:, c0:c0 + LANES] = t[0:8].astype(I32)
        carry = carry + jnp.sum(onehot, axis=0, keepdims=True)
    carry_s[...] = carry


def _plan_call(ridx, cnt):
    m = ridx.shape[0]
    _, n_tiles = _slot_counts(m)
    assert n_tiles <= PLAN_TILE and SLOT_TILE == PLAN_TILE
    return pl.pallas_call(
        _plan_kernel,
        grid=(m // PLAN_STEP,),
        in_specs=[pl.BlockSpec((PLAN_STEP, LANES), lambda i: (i, 0)),
                  pl.BlockSpec((8, LANES), lambda i: (0, 0))],
        out_specs=[
            pl.BlockSpec((8, PLAN_STEP), lambda i: (0, i)),
            pl.BlockSpec((PLAN_TILE, LANES), lambda i: (0, 0)),
        ],
        out_shape=[
            jax.ShapeDtypeStruct((8, m), I32),
            jax.ShapeDtypeStruct((PLAN_TILE, LANES), I32),
        ],
        scratch_shapes=[pltpu.VMEM((1, LANES), F32), pltpu.VMEM((1, LANES), F32)],
        compiler_params=pltpu.CompilerParams(dimension_semantics=("arbitrary",)),
        name="dispatch_plan",
    )(ridx, cnt)


def _sc_mesh():
    return plsc.VectorSubcoreMesh(core_axis_name="c", subcore_axis_name="s")


def _sc_worker():
    return lax.axis_index("s") * 2 + lax.axis_index("c")


def _sc_dispatch(xn, slot1, slot2, n_rows):
    m = xn.shape[0]
    per = m // SC_WORKERS
    n_pairs = per // (2 * SC_ROWS)

    @functools.partial(
        pl.kernel, mesh=_sc_mesh(),
        out_type=jax.ShapeDtypeStruct((n_rows, HALF), I32),
        scratch_types=[pltpu.VMEM((per // SC_ROWS, SC_ROWS), I32), pltpu.VMEM((per // SC_ROWS, SC_ROWS), I32),
                       pltpu.VMEM((SC_ROWS, HALF), I32), pltpu.VMEM((SC_ROWS, HALF), I32),
                       pltpu.SemaphoreType.DMA, pltpu.SemaphoreType.DMA, pltpu.SemaphoreType.DMA],
        name="sc_dispatch")
    def k(x_hbm, s1_hbm, s2_hbm, o_hbm, i1_v, i2_v, rows0, rows1, sem_r0, sem_r1, sem_w):
        wid = _sc_worker()
        base = wid * per

        def read(chunk, rows_v, sem):
            return pltpu.make_async_copy(x_hbm.at[pl.ds(base + chunk * SC_ROWS, SC_ROWS)], rows_v, sem)

        def scatter(chunk, rows_v):
            c1 = pltpu.async_copy(rows_v, o_hbm.at[i1_v.at[chunk]], sem_w)
            c2 = pltpu.async_copy(rows_v, o_hbm.at[i2_v.at[chunk]], sem_w)
            c1.wait()
            c2.wait()

        read(0, rows0, sem_r0).start()
        pltpu.sync_copy(s1_hbm.at[pl.ds(wid * (per // SC_ROWS), per // SC_ROWS)], i1_v)
        pltpu.sync_copy(s2_hbm.at[pl.ds(wid * (per // SC_ROWS), per // SC_ROWS)], i2_v)

        @pl.loop(0, n_pairs)
        def _(i):
            read(2 * i, rows0, sem_r0).wait()
            read(2 * i + 1, rows1, sem_r1).start()
            scatter(2 * i, rows0)
            read(2 * i + 1, rows1, sem_r1).wait()

            @pl.when(i + 1 < n_pairs)
            def _():
                read(2 * i + 2, rows0, sem_r0).start()

            scatter(2 * i + 1, rows1)

    return k(xn, slot1, slot2)


def _sc_collect(ys, slot1, slot2):
    m = slot1.size
    per = m // SC_WORKERS
    row = jax.ShapeDtypeStruct((m, HALF), I32)

    @functools.partial(
        pl.kernel, mesh=_sc_mesh(), out_type=[row, row],
        scratch_types=[pltpu.VMEM((per // SC_ROWS, SC_ROWS), I32), pltpu.VMEM((per // SC_ROWS, SC_ROWS), I32),
                       pltpu.VMEM((SC_ROWS, HALF), I32), pltpu.VMEM((SC_ROWS, HALF), I32),
                       pltpu.SemaphoreType.DMA, pltpu.SemaphoreType.DMA],
        name="sc_collect")
    def k(y_hbm, s1_hbm, s2_hbm, g1_hbm, g2_hbm, i1_v, i2_v, rows1, rows2, sem_g, sem_w):
        wid = _sc_worker()
        base = wid * per
        pltpu.sync_copy(s1_hbm.at[pl.ds(wid * (per // SC_ROWS), per // SC_ROWS)], i1_v)
        pltpu.sync_copy(s2_hbm.at[pl.ds(wid * (per // SC_ROWS), per // SC_ROWS)], i2_v)

        @pl.loop(0, per // SC_ROWS)
        def _(ci):
            t0 = base + ci * SC_ROWS
            a1 = pltpu.async_copy(y_hbm.at[i1_v.at[ci]], rows1, sem_g)
            a2 = pltpu.async_copy(y_hbm.at[i2_v.at[ci]], rows2, sem_g)
            a1.wait()
            a2.wait()
            w1 = pltpu.async_copy(rows1, g1_hbm.at[pl.ds(t0, SC_ROWS)], sem_w)
            w2 = pltpu.async_copy(rows2, g2_hbm.at[pl.ds(t0, SC_ROWS)], sem_w)
            w1.wait()
            w2.wait()

    return k(ys, slot1, slot2)


def _expert_kernel(te_ref, used_ref, xs_ref, wg_hbm, wu_hbm, wd_hbm, ys_ref, wg_b, wu_b, wd_b,
                   wg_f, wu_f, wd_f, sem, slot_s):
    j = pl.program_id(0)
    n = pl.num_programs(0)
    e = te_ref[j]
    prev = te_ref[jnp.maximum(j - 1, 0)]
    valid = e < N_EXPERTS

    def weight_copies(expert, slot):
        return [pltpu.make_async_copy(hbm.at[expert], buf.at[slot], sem.at[slot, i])
                for i, (hbm, buf) in enumerate(((wg_hbm, wg_f), (wu_hbm, wu_f), (wd_hbm, wd_f)))]

    @pl.when((j == 0) & valid)
    def _():
        slot_s[0] = 0
        for c in weight_copies(e, 0):
            c.start()

    @pl.when(((j == 0) | (e != prev)) & valid)
    def _():
        slot = slot_s[0]
        for c in weight_copies(e, slot):
            c.wait()
        k = lax.while_loop(lambda k: (k < n) & (te_ref[jnp.minimum(k, n - 1)] == e),
                           lambda k: k + 1, j + 1)
        nxt = te_ref[jnp.minimum(k, n - 1)]

        @pl.when((k < n) & (nxt < N_EXPERTS))
        def _():
            for c in weight_copies(nxt, 1 - slot):
                c.start()

        wg_b[...] = wg_f[slot].astype(BF16)
        wu_b[...] = wu_f[slot].astype(BF16)
        wd_b[...] = wd_f[slot].astype(BF16)
        slot_s[0] = 1 - slot

    @pl.when(valid)
    def _():
        half = SLOT_TILE // 2
        rows = [pl.ds(i * half, half) for i in range(2)]
        row_id = lax.broadcasted_iota(I32, (half, HALF), 0)
        xs = [_unpack_bf16_pairs(jnp.where(row_id + i * half < used_ref[j], xs_ref[r, :], 0))
              .astype(BF16) for i, r in enumerate(rows)]
        gs = [_dot(x, wg_b[...]) for x in xs]
        us = [_dot(x, wu_b[...]) for x in xs]
        hids = [(_silu(g) * u).astype(BF16) for g, u in zip(gs, us)]
        ys = [_dot(hid, wd_b[...]) for hid in hids]
        for r, y in zip(rows, ys):
            ys_ref[r, :] = _pack_bf16_pairs(y)

    @pl.when(e >= N_EXPERTS)
    def _():
        ys_ref[...] = jnp.zeros_like(ys_ref)


def _expert_call(tile_expert, tile_used, xs, wg, wu, wd, n_tiles):
    hbm = pl.BlockSpec(memory_space=pl.ANY)
    return pl.pallas_call(
        _expert_kernel,
        grid_spec=pltpu.PrefetchScalarGridSpec(
            num_scalar_prefetch=2,
            grid=(n_tiles,),
            in_specs=[pl.BlockSpec((SLOT_TILE, HALF), lambda j, te, used: (j, 0)), hbm, hbm, hbm],
            out_specs=pl.BlockSpec((SLOT_TILE, HALF), lambda j, te, used: (j, 0)),
            scratch_shapes=[
                pltpu.VMEM((D_MODEL, D_EXPERT), BF16), pltpu.VMEM((D_MODEL, D_EXPERT), BF16),
                pltpu.VMEM((D_EXPERT, D_MODEL), BF16),
                pltpu.VMEM((2, D_MODEL, D_EXPERT), F32), pltpu.VMEM((2, D_MODEL, D_EXPERT), F32),
                pltpu.VMEM((2, D_EXPERT, D_MODEL), F32),
                pltpu.SemaphoreType.DMA((2, 3)),
                pltpu.SMEM((1,), I32),
            ],
        ),
        out_shape=jax.ShapeDtypeStruct((n_tiles * SLOT_TILE, HALF), I32),
        compiler_params=pltpu.CompilerParams(
            dimension_semantics=("arbitrary",), vmem_limit_bytes=VMEM_LIMIT),
        name="experts",
    )(tile_expert, tile_used, xs, wg, wu, wd)


def _final_kernel(h_ref, g1_ref, g2_ref, rw_ref, nw_ref, o_ref):
    rw = rw_ref[...]
    y = rw[:, 0:1] * _unpack_bf16_pairs(g1_ref[...]) + rw[:, 1:2] * _unpack_bf16_pairs(g2_ref[...])
    h = h_ref[...] + y
    o_ref[...] = h * lax.rsqrt(jnp.mean(h * h, axis=-1, keepdims=True) + EPS) * nw_ref[...]


def _final_call(h1, g1, g2, rw, nw, tm=1024):
    m = h1.shape[0]
    return pl.pallas_call(
        _final_kernel,
        grid=(m // tm,),
        in_specs=[
            pl.BlockSpec((tm, D_MODEL), lambda i: (i, 0)),
            pl.BlockSpec((tm, HALF), lambda i: (i, 0)),
            pl.BlockSpec((tm, HALF), lambda i: (i, 0)),
            pl.BlockSpec((tm, LANES), lambda i: (i, 0)),
            pl.BlockSpec((1, D_MODEL), lambda i: (0, 0)),
        ],
        out_specs=pl.BlockSpec((tm, D_MODEL), lambda i: (i, 0)),
        out_shape=jax.ShapeDtypeStruct((m, D_MODEL), F32),
        compiler_params=pltpu.CompilerParams(dimension_semantics=("arbitrary",)),
        name="combine_final",
    )(h1, g1, g2, rw, nw)


def _pad_lanes(a):
    return jnp.pad(a, ((0, 0), (0, LANES - a.shape[1])))


def kernel(x, norm_mix_w, w_in, conv_w, A_log, dt_bias, gdn_norm_w, w_up_gdn, w_up_ret, w_out,
           norm_ffn_w, w_group, b_group, w_expert, b_expert, w_gate, w_up, w_down, norm_final_w):
    batch, seq, d = x.shape
    m = batch * seq
    h = x.reshape(m, d)
    depth = w_in.shape[0]
    sin, cos, inner, kdec, qdec, cdec = _retention_tables(seq)
    for l in range(depth):
        w_main, w_ab = _repack_call(jnp.transpose(w_in[l]))
        proj, ab = _proj_call(h, norm_mix_w[l][None, :], w_main, w_ab)

        conv8 = jnp.pad(conv_w[l], ((0, 8 - GDN_CONV), (0, 0)))
        ya = _gdn_call(proj, ab, conv8, _pad_lanes(A_log[l][None, :]), _pad_lanes(dt_bias[l][None, :]),
                       gdn_norm_w[l][None, :], batch, seq)
        yb = _ret_call(proj, sin, cos, inner, kdec, qdec, cdec, batch, seq)

        w_router = _pad_lanes(jnp.concatenate([w_expert[l], w_group[l]], axis=1))
        wr_hi = w_router.astype(BF16)
        wr_lo = (w_router - wr_hi.astype(F32)).astype(BF16)
        b_router = _pad_lanes(jnp.concatenate([b_expert[l], b_group[l]])[None, :])
        h1, xn, ridx, rw, cnt = _merge_call(
            h, ya, yb, proj, w_up_gdn[l].astype(BF16), w_up_ret[l].astype(BF16), w_out[l].astype(BF16),
            norm_ffn_w[l][None, :], jnp.concatenate([wr_hi, wr_lo], axis=1), b_router)

        n_slots, n_tiles = _slot_counts(m)
        slots, tiles = _plan_call(ridx, cnt)
        slot1, slot2 = slots[0].reshape(-1, SC_ROWS), slots[1].reshape(-1, SC_ROWS)
        xs = _sc_dispatch(xn, slot1, slot2, n_slots)
        ys = _expert_call(tiles[:n_tiles, 0], tiles[:n_tiles, 1], xs,
                          w_gate[l], w_up[l], w_down[l], n_tiles)
        g1, g2 = _sc_collect(ys, slot1, slot2)

        assert depth == 1
        h = _final_call(h1, g1, g2, rw, norm_final_w[None, :])
    return h.reshape(batch, seq, d)
```

```python
import functools
import math

import jax
import jax.numpy as jnp
from jax import lax
from jax.experimental import pallas as pl
from jax.experimental.pallas import tpu as pltpu
from jax.experimental.pallas import tpu_sc as plsc

F32 = jnp.float32
BF16 = jnp.bfloat16
I32 = jnp.int32
U32 = jnp.uint32

D_MODEL = 1024
EPS = 1e-6
GDN_HEADS = 4
GDN_DK = 128
GDN_DV = 128
GDN_CONV = 4
RET_HEADS = 4
RET_DK = 128
RET_DV = 256
ROPE_BASE = 10000.0
N_GROUPS = 4
EXPERTS_PER_GROUP = 8
N_EXPERTS = N_GROUPS * EXPERTS_PER_GROUP
D_EXPERT = 512

GDN_QK = GDN_HEADS * GDN_DK
GDN_V = GDN_HEADS * GDN_DV
RET_QK = RET_HEADS * RET_DK
RET_V = RET_HEADS * RET_DV

LANES = 128
CHUNK = 128
INV_BLOCK = 16
GDN_PREP_CHUNKS = 2
VMEM_LIMIT = 56 * 1024 * 1024

REPACK_COLS = 512
MERGE_PARTS = 1
TOP_K = 2
SLOT_TILE = 256
PLAN_TILE = 256
PLAN_STEP = 1024
HALF = D_MODEL // 2
SC_WORKERS = 32
SC_ROWS = 64
COLLECT_PARTS = 2
SC_COLLECT_CHUNKS = 8

PROJ_COLS = 3 * GDN_QK + GDN_V + 2 * RET_QK + 2 * RET_V + 2 * D_MODEL


def _silu(x):
    return x / (1.0 + jnp.exp(-x))


def _sigmoid(x):
    return 1.0 / (1.0 + jnp.exp(-x))


def _dot(a, b):
    return jnp.dot(a, b, preferred_element_type=F32)


def _dot_nt(a, b):
    return lax.dot_general(a, b, (((1,), (1,)), ((), ())), preferred_element_type=F32)


def _pack_bf16_pairs(x):
    bits = lax.bitcast_convert_type(x.astype(BF16).astype(F32), U32)
    packed = (bits[:, :HALF] >> 16) | (bits[:, HALF:] & jnp.uint32(0xFFFF0000))
    return lax.bitcast_convert_type(packed, I32)


def _unpack_bf16_pairs(p):
    p = lax.bitcast_convert_type(p, U32)
    lo = lax.bitcast_convert_type(p << 16, F32)
    hi = lax.bitcast_convert_type(p & jnp.uint32(0xFFFF0000), F32)
    return jnp.concatenate([lo, hi], axis=1)


def _proj_kernel(x_ref, nw_ref, w_ref, wab_ref, proj_ref, ab_ref, u_ref):
    j = pl.program_id(1)

    @pl.when(j == 0)
    def _():
        x = x_ref[...]
        u = x * lax.rsqrt(jnp.mean(x * x, axis=-1, keepdims=True) + EPS) * nw_ref[...]
        ub = u.astype(BF16)
        u_ref[...] = ub
        ab_ref[...] = _dot(ub, wab_ref[...])

    proj_ref[...] = _dot(u_ref[...], w_ref[...]).astype(BF16)


def _repack_kernel(wt_ref, abt_ref, main_ref, ab_ref):
    main_ref[...] = wt_ref[...].T.astype(BF16)

    @pl.when(pl.program_id(0) == 0)
    def _():
        ab = abt_ref[...].T
        ab_ref[...] = jnp.concatenate(
            [ab, jnp.zeros((ab.shape[0], LANES - ab.shape[1]), F32)], axis=1).astype(BF16)


def _repack_call(w_in_t):
    d_in, d = w_in_t.shape
    o_ab = 3 * GDN_QK
    n_ab = 2 * GDN_HEADS
    src = lambda r: pl.multiple_of(r * REPACK_COLS + jnp.where(r * REPACK_COLS >= o_ab, n_ab, 0), 8)
    return pl.pallas_call(
        _repack_kernel,
        grid=(PROJ_COLS // REPACK_COLS,),
        in_specs=[pl.BlockSpec((pl.Element(REPACK_COLS), pl.Element(d)), lambda r: (src(r), 0)),
                  pl.BlockSpec((pl.Element(n_ab), pl.Element(d)), lambda r: (o_ab, 0))],
        out_specs=[pl.BlockSpec((d, REPACK_COLS), lambda r: (0, r)),
                   pl.BlockSpec((d, LANES), lambda r: (0, 0))],
        out_shape=[jax.ShapeDtypeStruct((d, PROJ_COLS), BF16), jax.ShapeDtypeStruct((d, LANES), BF16)],
        compiler_params=pltpu.CompilerParams(dimension_semantics=("arbitrary",)),
        name="repack_w_in",
    )(w_in_t, w_in_t)


def _proj_call(x2, norm_w, w_main, w_ab, tm=1024, tn=3584):
    m = x2.shape[0]
    return pl.pallas_call(
        _proj_kernel,
        grid=(m // tm, PROJ_COLS // tn),
        in_specs=[
            pl.BlockSpec((tm, D_MODEL), lambda i, j: (i, 0)),
            pl.BlockSpec((1, D_MODEL), lambda i, j: (0, 0)),
            pl.BlockSpec((D_MODEL, tn), lambda i, j: (0, j)),
            pl.BlockSpec((D_MODEL, LANES), lambda i, j: (0, 0)),
        ],
        out_specs=[
            pl.BlockSpec((tm, tn), lambda i, j: (i, j)),
            pl.BlockSpec((tm, LANES), lambda i, j: (i, 0)),
        ],
        out_shape=[
            jax.ShapeDtypeStruct((m, PROJ_COLS), BF16),
            jax.ShapeDtypeStruct((m, LANES), F32),
        ],
        scratch_shapes=[pltpu.VMEM((tm, D_MODEL), BF16)],
        compiler_params=pltpu.CompilerParams(
            dimension_semantics=("arbitrary", "arbitrary"), vmem_limit_bytes=VMEM_LIMIT),
        name="proj",
    )(x2, norm_w, w_main, w_ab)


def _unit_lower_inverses(lows, ii, jj):
    eye = jnp.where(ii == jj, 1.0, 0.0).astype(F32)
    in_block = (ii // INV_BLOCK) == (jj // INV_BLOCK)
    ps = [jnp.where(in_block, -low, 0.0) for low in lows]
    ts = [eye + p for p in ps]
    span = 2
    while span < INV_BLOCK:
        ps = [_dot(p, p) for p in ps]
        ts = [t + _dot(t, p) for t, p in zip(ts, ps)]
        span *= 2
    s = INV_BLOCK
    while s < CHUNK:
        off_diag = ((ii // (2 * s)) == (jj // (2 * s))) & ((ii // s) != (jj // s))
        xs = [_dot(jnp.where(off_diag, low, 0.0), t) for low, t in zip(lows, ts)]
        ts = [t - _dot(t, x) for t, x in zip(ts, xs)]
        s *= 2
    return ts


def _gdn_kernel(qkv_ref, z_ref, ab_ref, convw_ref, alog_ref, dtb_ref, normw_ref, o_ref,
                b_s, o0_s, m_s, qp_s, gl_s, state_s, q_s, k_s, kb_s, rhs_s, dec_s, cv_s, qg_o, kdt_o,
                *, seq):
    nchunk = seq // CHUNK
    ii = lax.broadcasted_iota(jnp.int32, (CHUNK, CHUNK), 0)
    jj = lax.broadcasted_iota(jnp.int32, (CHUNK, CHUNK), 1)
    causal = ii >= jj
    strict = ii > jj
    tri = jnp.where(causal, 1.0, 0.0).astype(F32)
    neg_a = -jnp.exp(alog_ref[...])
    dtb = dtb_ref[...]

    def conv_cols(c, r0, lo, buf):
        x = qkv_ref[pl.ds(r0, CHUNK), lo:lo + LANES].astype(F32)
        prev0 = pl.multiple_of(jnp.maximum(r0 - 16, 0), 16)
        prev = qkv_ref[pl.ds(prev0, 16), lo:lo + LANES].astype(F32)
        buf[0:8, :] = prev[8:16] * jnp.where(c > 0, 1.0, 0.0)
        buf[8:8 + CHUNK, :] = x
        w = convw_ref[:, lo:lo + LANES]
        y = (w[3:4] * x + w[2:3] * buf[7:7 + CHUNK, :] + w[1:2] * buf[6:6 + CHUNK, :]
             + w[0:1] * buf[5:5 + CHUNK, :])
        return _silu(y)

    def l2n(x):
        return x * lax.rsqrt(jnp.sum(x * x, axis=-1, keepdims=True) + EPS)

    tri_b = tri.astype(BF16)

    def chunk_cumsum(g):
        g1 = g.astype(BF16)
        r1 = g - g1.astype(F32)
        g2 = r1.astype(BF16)
        g3 = (r1 - g2.astype(F32)).astype(BF16)
        return _dot(tri_b, g1) + (_dot(tri_b, g2) + _dot(tri_b, g3))

    def operands(cc, slot):
        for sub in range(GDN_PREP_CHUNKS):
            c = cc * GDN_PREP_CHUNKS + sub
            r0 = pl.multiple_of(c * CHUNK, CHUNK)
            ab = ab_ref[pl.ds(r0, CHUNK), :]
            xg = ab + dtb
            softplus = jnp.maximum(xg, 0.0) + jnp.log(1.0 + jnp.exp(-jnp.abs(xg)))
            g_all = neg_a * softplus
            beta_all = _sigmoid(ab)
            gc_all = chunk_cumsum(g_all)
            gc_t = gc_all.T
            gl_s[c] = jnp.exp(gc_all[CHUNK - 1:CHUNK, :])
            for h in range(GDN_HEADS):
                n = sub * GDN_HEADS + h
                bufs = [cv_s.at[slot, 3 * n + i] for i in range(3)]
                q = l2n(conv_cols(c, r0, h * GDN_DK, bufs[0])) * (GDN_DK ** -0.5)
                k = l2n(conv_cols(c, r0, GDN_QK + h * GDN_DK, bufs[1]))
                v = conv_cols(c, r0, 2 * GDN_QK + h * GDN_DV, bufs[2])
                gcol = gc_all[:, h:h + 1]
                grow = gc_t[h:h + 1, :]
                beta = beta_all[:, GDN_HEADS + h:GDN_HEADS + h + 1]
                dec_s[slot, n] = jnp.where(causal, jnp.exp(gcol - grow), 0.0)
                eg = jnp.exp(gcol)
                kb = k * beta
                q_s[slot, n] = q.astype(BF16)
                k_s[slot, n] = k.astype(BF16)
                kb_s[slot, n] = kb.astype(BF16)
                rhs_s[slot, n] = jnp.concatenate([v * beta, kb * eg], axis=1).astype(BF16)
                qg_o[slot, n] = q * eg
                kd = k * jnp.exp(gc_all[CHUNK - 1:CHUNK, h:h + 1] - gcol)
                kdt_o[slot, n] = kd.T.astype(BF16)

    def solve(cc, slot, between):
        items = [(cc * GDN_PREP_CHUNKS + sub, h, sub * GDN_HEADS + h)
                 for sub in range(GDN_PREP_CHUNKS) for h in range(GDN_HEADS)]
        kks = [_dot_nt(kb_s[slot, n], k_s[slot, n]) for _, _, n in items]
        qks = [_dot_nt(q_s[slot, n], k_s[slot, n]) for _, _, n in items]
        lows = [jnp.where(strict, kk * dec_s[slot, n], 0.0) for kk, (_, _, n) in zip(kks, items)]
        attns = [(qk * dec_s[slot, n]).astype(BF16) for qk, (_, _, n) in zip(qks, items)]
        for step in between[:len(between) // 2]:
            step()
        ts = _unit_lower_inverses(lows, ii, jj)
        for step in between[len(between) // 2:]:
            step()
        uws = [_dot(t.astype(BF16), rhs_s[slot, n]).astype(BF16) for t, (_, _, n) in zip(ts, items)]
        kds = [_dot(kdt_o[slot, n], uw) for uw, (_, _, n) in zip(uws, items)]
        ats = [_dot(attn, uw) for attn, uw in zip(attns, uws)]
        for kd_uw, at_uw, (c, h, n) in zip(kds, ats, items):
            b_s[c, h] = kd_uw[:, :GDN_DV]
            m_s[c, h] = (-kd_uw[:, GDN_DV:]).astype(BF16)
            o0_s[c, h] = at_uw[:, :GDN_DV]
            qp_s[c, h] = (qg_o[slot, n] - at_uw[:, GDN_DV:]).astype(BF16)

    ngroup = nchunk // GDN_PREP_CHUNKS
    per_trip = 2 * GDN_PREP_CHUNKS
    normw = normw_ref[...]
    state_s[...] = jnp.zeros_like(state_s)
    for c0 in range(per_trip):
        gl_s[c0] = jnp.zeros((1, LANES), F32)
        for h in range(GDN_HEADS):
            b_s[c0, h] = jnp.zeros((CHUNK, GDN_DV), F32)
            o0_s[c0, h] = jnp.zeros((CHUNK, GDN_DV), F32)
            m_s[c0, h] = jnp.zeros((CHUNK, GDN_DK), BF16)
            qp_s[c0, h] = jnp.zeros((CHUNK, GDN_DK), BF16)
    operands(0, 0)

    def prep(i, carry):
        first = jnp.maximum(per_trip * (i - 1), 0)
        steps = [functools.partial(scan, first + k, 0) for k in range(per_trip)]
        operands(2 * i + 1, 1)
        solve(2 * i, 0, steps[:per_trip // 2])
        operands(jnp.minimum(2 * i + 2, ngroup - 1), 0)
        solve(2 * i + 1, 1, steps[per_trip // 2:])
        return carry

    def scan(c, carry):
        r0 = pl.multiple_of(c * CHUNK, CHUNK)
        gl = gl_s[c]
        heads = range(GDN_HEADS)
        ss = [state_s[h] for h in heads]
        sbs = [s.astype(BF16) for s in ss]
        mss = [_dot(m_s[c, h], sbs[h]) for h in heads]
        qss = [_dot(qp_s[c, h], sbs[h]) for h in heads]
        for h in heads:
            state_s[h] = ss[h] * gl[:, h:h + 1] + (mss[h] + b_s[c, h])
            o = qss[h] + o0_s[c, h]
            z = z_ref[pl.ds(r0, CHUNK), h * GDN_DV:(h + 1) * GDN_DV].astype(F32)
            on = o * lax.rsqrt(jnp.mean(o * o, axis=-1, keepdims=True) + EPS) * normw
            o_ref[pl.ds(r0, CHUNK), h * GDN_DV:(h + 1) * GDN_DV] = (on * _silu(z)).astype(BF16)
        return carry

    ntrip = ngroup // 2
    lax.fori_loop(0, ntrip, prep, 0)
    lax.fori_loop(per_trip * (ntrip - 1), nchunk, scan, 0)


def _gdn_call(proj, ab, conv_w8, alog_p, dtb_p, normw, batch, seq):
    nchunk = seq // CHUNK
    hs = (nchunk, GDN_HEADS, CHUNK, CHUNK)
    ops = (2, GDN_PREP_CHUNKS * GDN_HEADS, CHUNK, CHUNK)
    return pl.pallas_call(
        functools.partial(_gdn_kernel, seq=seq),
        grid=(batch,),
        in_specs=[
            pl.BlockSpec((seq, 3 * GDN_QK), lambda b: (b, 0)),
            pl.BlockSpec((seq, GDN_V), lambda b: (b, 3)),
            pl.BlockSpec((seq, LANES), lambda b: (b, 0)),
            pl.BlockSpec((8, 3 * GDN_QK), lambda b: (0, 0)),
            pl.BlockSpec((1, LANES), lambda b: (0, 0)),
            pl.BlockSpec((1, LANES), lambda b: (0, 0)),
            pl.BlockSpec((1, GDN_DV), lambda b: (0, 0)),
        ],
        out_specs=pl.BlockSpec((seq, GDN_V), lambda b: (b, 0)),
        out_shape=jax.ShapeDtypeStruct((batch * seq, GDN_V), BF16),
        scratch_shapes=[
            pltpu.VMEM(hs, F32), pltpu.VMEM(hs, F32), pltpu.VMEM(hs, BF16), pltpu.VMEM(hs, BF16),
            pltpu.VMEM((nchunk, 1, LANES), F32),
            pltpu.VMEM((GDN_HEADS, GDN_DK, GDN_DV), F32),
            pltpu.VMEM(ops, BF16), pltpu.VMEM(ops, BF16), pltpu.VMEM(ops, BF16),
            pltpu.VMEM(ops[:3] + (2 * CHUNK,), BF16),
            pltpu.VMEM(ops, F32),
            pltpu.VMEM((2, 3 * ops[1], 8 + CHUNK, LANES), F32),
            pltpu.VMEM(ops, F32), pltpu.VMEM(ops, BF16),
        ],
        compiler_params=pltpu.CompilerParams(
            dimension_semantics=("arbitrary",), vmem_limit_bytes=VMEM_LIMIT),
        name="gdn",
    )(proj, proj, ab, conv_w8, alog_p, dtb_p, normw)


def _ret_kernel(qk_ref, v_ref, g_ref, sin_ref, cos_ref, inner_ref, kdec_ref, qdec_ref, cdec_ref,
                o_ref, state_s, q_s, qd_s, k_s, kt_s, *, seq):
    nchunk = seq // CHUNK
    lane = lax.broadcasted_iota(jnp.int32, (CHUNK, RET_DK), 1)
    even = (lane % 2) == 0

    def rotate(x, sin, cos):
        nxt = pltpu.roll(x, RET_DK - 1, axis=1)
        prv = pltpu.roll(x, 1, axis=1)
        return x * cos + jnp.where(even, -nxt, prv) * sin

    state_s[...] = jnp.zeros_like(state_s)
    kdec = kdec_ref[...]
    qdec = qdec_ref[...]
    cdec = cdec_ref[...]

    heads = range(RET_HEADS)

    def operands(c, slot):
        r0 = pl.multiple_of(c * CHUNK, CHUNK)
        sin = sin_ref[pl.ds(r0, CHUNK), :]
        cos = cos_ref[pl.ds(r0, CHUNK), :]
        for h in heads:
            q = rotate(qk_ref[pl.ds(r0, CHUNK), h * RET_DK:(h + 1) * RET_DK].astype(F32), sin, cos)
            k = rotate(qk_ref[pl.ds(r0, CHUNK), RET_QK + h * RET_DK:RET_QK + (h + 1) * RET_DK]
                       .astype(F32), sin, cos) * (RET_DK ** -0.5)
            q_s[slot, h] = q.astype(BF16)
            qd_s[slot, h] = (q * qdec[:, h:h + 1]).astype(BF16)
            k_s[slot, h] = k.astype(BF16)
            kt_s[slot, h] = (k * kdec[:, h:h + 1]).T.astype(BF16)

    def outputs(c, slot):
        r0 = pl.multiple_of(c * CHUNK, CHUNK)
        vs = [v_ref[pl.ds(r0, CHUNK), h * RET_DV:(h + 1) * RET_DV] for h in heads]
        ss = [state_s[h] for h in heads]
        qks = [_dot_nt(q_s[slot, h], k_s[slot, h]) for h in heads]
        inters = [_dot(qd_s[slot, h], ss[h].astype(BF16)) for h in heads]
        kvs = [_dot(kt_s[slot, h], vs[h]) for h in heads]
        intras = [_dot((qks[h] * inner_ref[h]).astype(BF16), vs[h]) for h in heads]
        for h in heads:
            state_s[h] = ss[h] * cdec[:, h:h + 1] + kvs[h]
            o = intras[h] + inters[h]
            gate = g_ref[pl.ds(r0, CHUNK), h * RET_DV:(h + 1) * RET_DV].astype(F32)
            on = o * lax.rsqrt(jnp.mean(o * o, axis=-1, keepdims=True) + EPS)
            o_ref[pl.ds(r0, CHUNK), h * RET_DV:(h + 1) * RET_DV] = (on * _silu(gate)).astype(BF16)

    operands(0, 0)

    def body(i, carry):
        operands(2 * i + 1, 1)
        outputs(2 * i, 0)
        operands(jnp.minimum(2 * i + 2, nchunk - 1), 0)
        outputs(2 * i + 1, 1)
        return carry

    lax.fori_loop(0, nchunk // 2, body, 0)


def _ret_call(proj, sin, cos, inner, kdec, qdec, cdec, batch, seq):
    return pl.pallas_call(
        functools.partial(_ret_kernel, seq=seq),
        grid=(batch,),
        in_specs=[
            pl.BlockSpec((seq, 2 * RET_QK), lambda b: (b, 2)),
            pl.BlockSpec((seq, RET_V), lambda b: (b, 3)),
            pl.BlockSpec((seq, RET_V), lambda b: (b, 4)),
            pl.BlockSpec((seq, RET_DK), lambda b: (0, 0)),
            pl.BlockSpec((seq, RET_DK), lambda b: (0, 0)),
            pl.BlockSpec((RET_HEADS, CHUNK, CHUNK), lambda b: (0, 0, 0)),
            pl.BlockSpec((CHUNK, LANES), lambda b: (0, 0)),
            pl.BlockSpec((CHUNK, LANES), lambda b: (0, 0)),
            pl.BlockSpec((1, LANES), lambda b: (0, 0)),
        ],
        out_specs=pl.BlockSpec((seq, RET_V), lambda b: (b, 0)),
        out_shape=jax.ShapeDtypeStruct((batch * seq, RET_V), BF16),
        scratch_shapes=[pltpu.VMEM((RET_HEADS, RET_DK, RET_DV), F32)]
        + [pltpu.VMEM((2, RET_HEADS, CHUNK, RET_DK), BF16)] * 4,
        compiler_params=pltpu.CompilerParams(
            dimension_semantics=("arbitrary",), vmem_limit_bytes=VMEM_LIMIT),
        name="retention",
    )(proj, proj, proj, sin, cos, inner, kdec, qdec, cdec)


def _retention_tables(seq):
    inv_freq = 1.0 / (ROPE_BASE ** jnp.linspace(0.0, 1.0, RET_DK // 2, dtype=F32))
    ang = jnp.arange(seq, dtype=F32)[:, None] * inv_freq[None, :]
    sin = jnp.repeat(jnp.sin(ang), 2, axis=-1)
    cos = jnp.repeat(jnp.cos(ang), 2, axis=-1)
    log_gamma = jnp.log(1.0 - 2.0 ** (-5.0 - jnp.arange(RET_HEADS, dtype=F32)))
    idx = jnp.arange(CHUNK, dtype=F32)
    causal = jnp.tril(jnp.ones((CHUNK, CHUNK), dtype=bool))
    rel = jnp.where(causal, idx[:, None] - idx[None, :], 0.0)
    inner = jnp.where(causal, jnp.exp(rel[None] * log_gamma[:, None, None]), 0.0)
    k_decay = jnp.exp(log_gamma[:, None] * (CHUNK - 1.0 - idx)[None, :])
    q_decay = jnp.exp(log_gamma[:, None] * (idx + 1.0)[None, :])
    chunk_decay = jnp.exp(log_gamma * CHUNK)
    pad = LANES - RET_HEADS
    kdec = jnp.pad(k_decay.T, ((0, 0), (0, pad)))
    qdec = jnp.pad(q_decay.T, ((0, 0), (0, pad)))
    cdec = jnp.pad(chunk_decay[None, :], ((0, 0), (0, pad)))
    return sin, cos, inner, kdec, qdec, cdec


def _merge_kernel(x_ref, ya_ref, yb_ref, ma_ref, mb_ref, wa_ref, wr_ref, wo_ref, nw_ref,
                  wrt_ref, br_ref, h_ref, xn_ref, ridx_ref, rw_ref, cnt_ref):
    tm = x_ref.shape[0] // MERGE_PARTS
    rows = [pl.ds(i * tm, tm) for i in range(MERGE_PARTS)]
    a_ = [_dot(ya_ref[r, :], wa_ref[...]) for r in rows]
    r_ = [_dot(yb_ref[r, :], wr_ref[...]) for r in rows]
    merged = [(_sigmoid(ma_ref[r, :].astype(F32)) * a + _sigmoid(mb_ref[r, :].astype(F32)) * rr)
              .astype(BF16) for r, a, rr in zip(rows, a_, r_)]
    hs = [x_ref[r, :] + _dot(mg, wo_ref[...]) for r, mg in zip(rows, merged)]
    xcats = []
    for r, h in zip(rows, hs):
        h_ref[r, :] = h
        xn = h * lax.rsqrt(jnp.mean(h * h, axis=-1, keepdims=True) + EPS) * nw_ref[...]
        xn_ref[r, :] = _pack_bf16_pairs(xn)
        xh = xn.astype(BF16)
        xl = (xn - xh.astype(F32)).astype(BF16)
        xcats.append(jnp.concatenate([xh, xl], axis=0))
    parts_ = [_dot(xc, wrt_ref[...]) for xc in xcats]
    counts = jnp.zeros((1, LANES), F32)
    for r, parts in zip(rows, parts_):
        counts = counts + _route(parts, tm, br_ref[...], ridx_ref.at[r, :], rw_ref.at[r, :])

    @pl.when(pl.program_id(0) == 0)
    def _():
        cnt_ref[...] = jnp.zeros_like(cnt_ref)

    cnt_ref[...] += jnp.broadcast_to(counts, cnt_ref.shape)


def _route(parts, tm, bias, ridx_ref, rw_ref):
    logits = (parts[:tm, :LANES] + (parts[tm:, :LANES] + parts[:tm, LANES:]
                                    + parts[tm:, LANES:])) + bias
    lane = lax.broadcasted_iota(jnp.int32, (tm, LANES), 1)
    neg = -jnp.inf
    gl = jnp.where((lane >= N_EXPERTS) & (lane < N_EXPERTS + N_GROUPS), logits, neg)
    gmax = jnp.max(gl, axis=-1, keepdims=True)
    gidx = jnp.min(jnp.where(gl == gmax, lane, LANES), axis=-1, keepdims=True) - N_EXPERTS
    g_w = 1.0 / jnp.sum(jnp.exp(gl - gmax), axis=-1, keepdims=True)
    el = jnp.where((lane // EXPERTS_PER_GROUP == gidx) & (lane < N_EXPERTS), logits, neg)
    m1 = jnp.max(el, axis=-1, keepdims=True)
    i1 = jnp.min(jnp.where(el == m1, lane, LANES), axis=-1, keepdims=True)
    el2 = jnp.where(lane == i1, neg, el)
    m2 = jnp.max(el2, axis=-1, keepdims=True)
    i2 = jnp.min(jnp.where(el2 == m2, lane, LANES), axis=-1, keepdims=True)
    e2 = jnp.exp(m2 - m1)
    p1 = g_w / (1.0 + e2)
    p2 = g_w * e2 / (1.0 + e2)
    ridx_ref[...] = jnp.where(lane == 0, i1, jnp.where(lane == 1, i2, 0))
    rw_ref[...] = jnp.where(lane == 0, p1, jnp.where(lane == 1, p2, 0.0))
    onehot = jnp.where((lane == i1) | (lane == i2), 1.0, 0.0)
    return jnp.sum(onehot, axis=0, keepdims=True)


def _merge_call(x2, ya, yb, proj, wa, wr, wo, nw, w_router, b_router, tm=1024):
    m = x2.shape[0]
    full = lambda shape: pl.BlockSpec(shape, lambda i: (0, 0))
    return pl.pallas_call(
        _merge_kernel,
        grid=(m // tm,),
        in_specs=[
            pl.BlockSpec((tm, D_MODEL), lambda i: (i, 0)),
            pl.BlockSpec((tm, GDN_V), lambda i: (i, 0)),
            pl.BlockSpec((tm, RET_V), lambda i: (i, 0)),
            pl.BlockSpec((tm, D_MODEL), lambda i: (i, 5)),
            pl.BlockSpec((tm, D_MODEL), lambda i: (i, 6)),
            full((GDN_V, D_MODEL)), full((RET_V, D_MODEL)), full((D_MODEL, D_MODEL)),
            full((1, D_MODEL)),
            full((D_MODEL, 2 * LANES)), full((1, LANES)),
        ],
        out_specs=[
            pl.BlockSpec((tm, D_MODEL), lambda i: (i, 0)),
            pl.BlockSpec((tm, HALF), lambda i: (i, 0)),
            pl.BlockSpec((tm, LANES), lambda i: (i, 0)),
            pl.BlockSpec((tm, LANES), lambda i: (i, 0)),
            pl.BlockSpec((8, LANES), lambda i: (0, 0)),
        ],
        out_shape=[
            jax.ShapeDtypeStruct((m, D_MODEL), F32),
            jax.ShapeDtypeStruct((m, HALF), I32),
            jax.ShapeDtypeStruct((m, LANES), I32),
            jax.ShapeDtypeStruct((m, LANES), F32),
            jax.ShapeDtypeStruct((8, LANES), F32),
        ],
        compiler_params=pltpu.CompilerParams(
            dimension_semantics=("arbitrary",), vmem_limit_bytes=VMEM_LIMIT),
        name="merge_router",
    )(x2, ya, yb, proj, proj, wa, wr, wo, nw, w_router, b_router)


def _slot_counts(m):
    n_slots = TOP_K * m + N_EXPERTS * SLOT_TILE
    return n_slots, n_slots // SLOT_TILE


def _lane_prefix_sum(x, lane):
    s = 1
    while s < LANES:
        x = x + jnp.where(lane >= s, pltpu.roll(x, s, axis=1), 0.0)
        s *= 2
    return x


def _plan_kernel(ridx_ref, cnt_ref, slots_ref, tile_ref, carry_s, off_s):
    i = pl.program_id(0)
    lane = lax.broadcasted_iota(I32, (PLAN_TILE, LANES), 1)
    row = lax.broadcasted_iota(I32, (PLAN_TILE, LANES), 0)
    lane1 = lane[0:1]

    @pl.when(i == 0)
    def _():
        cnt = cnt_ref[0:1, :]
        tile = float(SLOT_TILE)
        padded = jnp.floor((cnt + (tile - 1.0)) / tile) * tile
        incl = _lane_prefix_sum(padded, lane1)
        off = incl - padded
        off_s[...] = off
        carry_s[...] = jnp.zeros_like(carry_s)
        first = (row * SLOT_TILE).astype(F32)
        ended = jnp.where((lane < N_EXPERTS) & (incl <= first), 1.0, 0.0)
        tile_e = jnp.sum(ended, axis=-1, keepdims=True)
        last = jnp.sum(jnp.where(lane.astype(F32) == tile_e, off + cnt, 0.0), axis=-1, keepdims=True)
        used = jnp.clip(last - first[:, 0:1], 0.0, tile)
        tile_ref[...] = jnp.where(lane == 0, tile_e, jnp.where(lane == 1, used, 0.0)).astype(I32)

    strict = jnp.where(row[:, 0:1] > lax.broadcasted_iota(I32, (PLAN_TILE, PLAN_TILE), 1),
                       1.0, 0.0).astype(BF16)
    off = off_s[...]
    carry = carry_s[...]
    for sb in range(PLAN_STEP // PLAN_TILE):
        rows = pl.ds(sb * PLAN_TILE, PLAN_TILE)
        e1 = ridx_ref[rows, 0:1]
        e2 = ridx_ref[rows, 1:2]
        onehot = jnp.where((lane == e1) | (lane == e2), 1.0, 0.0)
        pos = _dot(strict, onehot.astype(BF16)) + (carry + off)
        s1 = jnp.sum(jnp.where(lane == e1, pos, 0.0), axis=-1, keepdims=True)
        s2 = jnp.sum(jnp.where(lane == e2, pos, 0.0), axis=-1, keepdims=True)
        both = jnp.where(lane == 0, s1, jnp.where(lane == 1, s2, 0.0))
        for q in range(PLAN_TILE // LANES):
            t = both[q * LANES:(q + 1) * LANES].T
            c0 = sb * PLAN_TILE + q * LANES
            slots_ref[:, c0:c0 + LANES] = t[0:8].astype(I32)
        carry = carry + jnp.sum(onehot, axis=0, keepdims=True)
    carry_s[...] = carry


def _plan_call(ridx, cnt):
    m = ridx.shape[0]
    _, n_tiles = _slot_counts(m)
    assert n_tiles <= PLAN_TILE and SLOT_TILE == PLAN_TILE
    return pl.pallas_call(
        _plan_kernel,
        grid=(m // PLAN_STEP,),
        in_specs=[pl.BlockSpec((PLAN_STEP, LANES), lambda i: (i, 0)),
                  pl.BlockSpec((8, LANES), lambda i: (0, 0))],
        out_specs=[
            pl.BlockSpec((8, PLAN_STEP), lambda i: (0, i)),
            pl.BlockSpec((PLAN_TILE, LANES), lambda i: (0, 0)),
        ],
        out_shape=[
            jax.ShapeDtypeStruct((8, m), I32),
            jax.ShapeDtypeStruct((PLAN_TILE, LANES), I32),
        ],
        scratch_shapes=[pltpu.VMEM((1, LANES), F32), pltpu.VMEM((1, LANES), F32)],
        compiler_params=pltpu.CompilerParams(dimension_semantics=("arbitrary",)),
        name="dispatch_plan",
    )(ridx, cnt)


def _sc_mesh():
    return plsc.VectorSubcoreMesh(core_axis_name="c", subcore_axis_name="s")


def _sc_worker():
    return lax.axis_index("s") * 2 + lax.axis_index("c")


def _sc_dispatch(xn, slot1, slot2, n_rows):
    m = xn.shape[0]
    per = m // SC_WORKERS
    n_pairs = per // (2 * SC_ROWS)

    @functools.partial(
        pl.kernel, mesh=_sc_mesh(),
        out_type=jax.ShapeDtypeStruct((n_rows, HALF), I32),
        scratch_types=[pltpu.VMEM((per // SC_ROWS, SC_ROWS), I32), pltpu.VMEM((per // SC_ROWS, SC_ROWS), I32),
                       pltpu.VMEM((SC_ROWS, HALF), I32), pltpu.VMEM((SC_ROWS, HALF), I32),
                       pltpu.SemaphoreType.DMA, pltpu.SemaphoreType.DMA, pltpu.SemaphoreType.DMA],
        name="sc_dispatch")
    def k(x_hbm, s1_hbm, s2_hbm, o_hbm, i1_v, i2_v, rows0, rows1, sem_r0, sem_r1, sem_w):
        wid = _sc_worker()
        base = wid * per

        def read(chunk, rows_v, sem):
            return pltpu.make_async_copy(x_hbm.at[pl.ds(base + chunk * SC_ROWS, SC_ROWS)], rows_v, sem)

        def scatter(chunk, rows_v):
            c1 = pltpu.async_copy(rows_v, o_hbm.at[i1_v.at[chunk]], sem_w)
            c2 = pltpu.async_copy(rows_v, o_hbm.at[i2_v.at[chunk]], sem_w)
            c1.wait()
            c2.wait()

        read(0, rows0, sem_r0).start()
        pltpu.sync_copy(s1_hbm.at[pl.ds(wid * (per // SC_ROWS), per // SC_ROWS)], i1_v)
        pltpu.sync_copy(s2_hbm.at[pl.ds(wid * (per // SC_ROWS), per // SC_ROWS)], i2_v)

        @pl.loop(0, n_pairs)
        def _(i):
            read(2 * i, rows0, sem_r0).wait()
            read(2 * i + 1, rows1, sem_r1).start()
            scatter(2 * i, rows0)
            read(2 * i + 1, rows1, sem_r1).wait()

            @pl.when(i + 1 < n_pairs)
            def _():
                read(2 * i + 2, rows0, sem_r0).start()

            scatter(2 * i + 1, rows1)

    return k(xn, slot1, slot2)


def _sc_collect(ys, slot1, slot2):
    n_chunks, rows = slot1.shape
    m = n_chunks * rows
    per = n_chunks // SC_WORKERS
    n_pairs = per // 2
    out = jax.ShapeDtypeStruct((m, HALF), I32)
    buf = pltpu.VMEM((rows, HALF), I32)

    @functools.partial(
        pl.kernel, mesh=_sc_mesh(), out_type=[out, out],
        scratch_types=[pltpu.VMEM((per, rows), I32), pltpu.VMEM((per, rows), I32), buf, buf, buf, buf,
                       pltpu.SemaphoreType.DMA, pltpu.SemaphoreType.DMA, pltpu.SemaphoreType.DMA],
        name="sc_collect")
    def k(y_hbm, s1_hbm, s2_hbm, g1_hbm, g2_hbm, i1_v, i2_v, a1_v, a2_v, b1_v, b2_v,
          sem_a, sem_b, sem_w):
        wid = _sc_worker()
        pltpu.sync_copy(s1_hbm.at[pl.ds(wid * per, per)], i1_v)
        pltpu.sync_copy(s2_hbm.at[pl.ds(wid * per, per)], i2_v)

        def gathers(chunk, r1_v, r2_v, sem):
            return (pltpu.make_async_copy(y_hbm.at[i1_v.at[chunk]], r1_v, sem),
                    pltpu.make_async_copy(y_hbm.at[i2_v.at[chunk]], r2_v, sem))

        def write_out(chunk, r1_v, r2_v):
            t0 = (wid * per + chunk) * rows
            w1 = pltpu.async_copy(r1_v, g1_hbm.at[pl.ds(t0, rows)], sem_w)
            w2 = pltpu.async_copy(r2_v, g2_hbm.at[pl.ds(t0, rows)], sem_w)
            w1.wait()
            w2.wait()

        for c in gathers(0, a1_v, a2_v, sem_a):
            c.start()

        @pl.loop(0, n_pairs)
        def _(i):
            for c in gathers(2 * i, a1_v, a2_v, sem_a):
                c.wait()
            for c in gathers(2 * i + 1, b1_v, b2_v, sem_b):
                c.start()
            write_out(2 * i, a1_v, a2_v)
            for c in gathers(2 * i + 1, b1_v, b2_v, sem_b):
                c.wait()

            @pl.when(i + 1 < n_pairs)
            def _():
                for c in gathers(2 * i + 2, a1_v, a2_v, sem_a):
                    c.start()

            write_out(2 * i + 1, b1_v, b2_v)

    return k(ys, slot1, slot2)


def _expert_kernel(te_ref, used_ref, xs_ref, wg_hbm, wu_hbm, wd_hbm, ys_ref, wg_b, wu_b, wd_b,
                   wg_f, wu_f, wd_f, sem, slot_s):
    j = pl.program_id(0)
    n = pl.num_programs(0)
    e = te_ref[j]
    prev = te_ref[jnp.maximum(j - 1, 0)]
    valid = e < N_EXPERTS

    def weight_copies(expert, slot):
        return [pltpu.make_async_copy(hbm.at[expert], buf.at[slot], sem.at[slot, i])
                for i, (hbm, buf) in enumerate(((wg_hbm, wg_f), (wu_hbm, wu_f), (wd_hbm, wd_f)))]

    @pl.when((j == 0) & valid)
    def _():
        slot_s[0] = 0
        for c in weight_copies(e, 0):
            c.start()

    @pl.when(((j == 0) | (e != prev)) & valid)
    def _():
        slot = slot_s[0]
        for c in weight_copies(e, slot):
            c.wait()
        k = lax.while_loop(lambda k: (k < n) & (te_ref[jnp.minimum(k, n - 1)] == e),
                           lambda k: k + 1, j + 1)
        nxt = te_ref[jnp.minimum(k, n - 1)]

        @pl.when((k < n) & (nxt < N_EXPERTS))
        def _():
            for c in weight_copies(nxt, 1 - slot):
                c.start()

        wg_b[...] = wg_f[slot].astype(BF16)
        wu_b[...] = wu_f[slot].astype(BF16)
        wd_b[...] = wd_f[slot].astype(BF16)
        slot_s[0] = 1 - slot

    @pl.when(valid)
    def _():
        half = SLOT_TILE // 2
        rows = [pl.ds(i * half, half) for i in range(2)]
        row_id = lax.broadcasted_iota(I32, (half, HALF), 0)
        xs = [_unpack_bf16_pairs(jnp.where(row_id + i * half < used_ref[j], xs_ref[r, :], 0))
              .astype(BF16) for i, r in enumerate(rows)]
        gs = [_dot(x, wg_b[...]) for x in xs]
        us = [_dot(x, wu_b[...]) for x in xs]
        hids = [(_silu(g) * u).astype(BF16) for g, u in zip(gs, us)]
        ys = [_dot(hid, wd_b[...]) for hid in hids]
        for r, y in zip(rows, ys):
            ys_ref[r, :] = _pack_bf16_pairs(y)

    @pl.when(e >= N_EXPERTS)
    def _():
        ys_ref[...] = jnp.zeros_like(ys_ref)


def _expert_call(tile_expert, tile_used, xs, wg, wu, wd, n_tiles):
    hbm = pl.BlockSpec(memory_space=pl.ANY)
    return pl.pallas_call(
        _expert_kernel,
        grid_spec=pltpu.PrefetchScalarGridSpec(
            num_scalar_prefetch=2,
            grid=(n_tiles,),
            in_specs=[pl.BlockSpec((SLOT_TILE, HALF), lambda j, te, used: (j, 0)), hbm, hbm, hbm],
            out_specs=pl.BlockSpec((SLOT_TILE, HALF), lambda j, te, used: (j, 0)),
            scratch_shapes=[
                pltpu.VMEM((D_MODEL, D_EXPERT), BF16), pltpu.VMEM((D_MODEL, D_EXPERT), BF16),
                pltpu.VMEM((D_EXPERT, D_MODEL), BF16),
                pltpu.VMEM((2, D_MODEL, D_EXPERT), F32), pltpu.VMEM((2, D_MODEL, D_EXPERT), F32),
                pltpu.VMEM((2, D_EXPERT, D_MODEL), F32),
                pltpu.SemaphoreType.DMA((2, 3)),
                pltpu.SMEM((1,), I32),
            ],
        ),
        out_shape=jax.ShapeDtypeStruct((n_tiles * SLOT_TILE, HALF), I32),
        compiler_params=pltpu.CompilerParams(
            dimension_semantics=("arbitrary",), vmem_limit_bytes=VMEM_LIMIT),
        name="experts",
    )(tile_expert, tile_used, xs, wg, wu, wd)


def _final_kernel(h_ref, g1_ref, g2_ref, rw_ref, nw_ref, *rest):
    o_ref = rest[-1]
    rw = rw_ref[...]
    y = rw[:, 0:1] * _unpack_bf16_pairs(g1_ref[...]) + rw[:, 1:2] * _unpack_bf16_pairs(g2_ref[...])
    h = h_ref[...] + y
    o_ref[...] = h * lax.rsqrt(jnp.mean(h * h, axis=-1, keepdims=True) + EPS) * nw_ref[...]


def _final_call(h1, g1, g2, rw, nw, part, out_so_far, tm=1024):
    m = h1.shape[0]
    steps = g1.shape[0] // tm
    off = part * steps
    in_specs = [
        pl.BlockSpec((tm, D_MODEL), lambda i: (i + off, 0)),
        pl.BlockSpec((tm, HALF), lambda i: (i, 0)),
        pl.BlockSpec((tm, HALF), lambda i: (i, 0)),
        pl.BlockSpec((tm, LANES), lambda i: (i + off, 0)),
        pl.BlockSpec((1, D_MODEL), lambda i: (0, 0)),
    ]
    args = [h1, g1, g2, rw, nw]
    aliases = {}
    if out_so_far is not None:
        in_specs.append(pl.BlockSpec(memory_space=pl.ANY))
        args.append(out_so_far)
        aliases = {len(args) - 1: 0}
    return pl.pallas_call(
        _final_kernel,
        grid=(steps,),
        in_specs=in_specs,
        out_specs=pl.BlockSpec((tm, D_MODEL), lambda i: (i + off, 0)),
        out_shape=jax.ShapeDtypeStruct((m, D_MODEL), F32),
        input_output_aliases=aliases,
        compiler_params=pltpu.CompilerParams(dimension_semantics=("arbitrary",)),
        name="combine_final",
    )(*args)


def _pad_lanes(a):
    return jnp.pad(a, ((0, 0), (0, LANES - a.shape[1])))


def kernel(x, norm_mix_w, w_in, conv_w, A_log, dt_bias, gdn_norm_w, w_up_gdn, w_up_ret, w_out,
           norm_ffn_w, w_group, b_group, w_expert, b_expert, w_gate, w_up, w_down, norm_final_w):
    batch, seq, d = x.shape
    m = batch * seq
    h = x.reshape(m, d)
    depth = w_in.shape[0]
    sin, cos, inner, kdec, qdec, cdec = _retention_tables(seq)
    for l in range(depth):
        w_main, w_ab = _repack_call(jnp.transpose(w_in[l]))
        proj, ab = _proj_call(h, norm_mix_w[l][None, :], w_main, w_ab)

        conv8 = jnp.pad(conv_w[l], ((0, 8 - GDN_CONV), (0, 0)))
        ya = _gdn_call(proj, ab, conv8, _pad_lanes(A_log[l][None, :]), _pad_lanes(dt_bias[l][None, :]),
                       gdn_norm_w[l][None, :], batch, seq)
        yb = _ret_call(proj, sin, cos, inner, kdec, qdec, cdec, batch, seq)

        w_router = _pad_lanes(jnp.concatenate([w_expert[l], w_group[l]], axis=1))
        wr_hi = w_router.astype(BF16)
        wr_lo = (w_router - wr_hi.astype(F32)).astype(BF16)
        b_router = _pad_lanes(jnp.concatenate([b_expert[l], b_group[l]])[None, :])
        h1, xn, ridx, rw, cnt = _merge_call(
            h, ya, yb, proj, w_up_gdn[l].astype(BF16), w_up_ret[l].astype(BF16), w_out[l].astype(BF16),
            norm_ffn_w[l][None, :], jnp.concatenate([wr_hi, wr_lo], axis=1), b_router)

        n_slots, n_tiles = _slot_counts(m)
        slots, tiles = _plan_call(ridx, cnt)
        slot1, slot2 = slots[0].reshape(-1, SC_ROWS), slots[1].reshape(-1, SC_ROWS)
        xs = _sc_dispatch(xn, slot1, slot2, n_slots)
        ys = _expert_call(tiles[:n_tiles, 0], tiles[:n_tiles, 1], xs,
                          w_gate[l], w_up[l], w_down[l], n_tiles)
        assert depth == 1
        rows = m // (COLLECT_PARTS * SC_WORKERS * SC_COLLECT_CHUNKS)
        c1 = slots[0].reshape(COLLECT_PARTS, -1, rows)
        c2 = slots[1].reshape(COLLECT_PARTS, -1, rows)
        h = None
        for p in range(COLLECT_PARTS):
            g1, g2 = _sc_collect(ys, c1[p], c2[p])
            h = _final_call(h1, g1, g2, rw, norm_final_w[None, :], p, h)
    return h.reshape(batch, seq, d)
```

```python
import functools
import math

import jax
import jax.numpy as jnp
from jax import lax
from jax.experimental import pallas as pl
from jax.experimental.pallas import tpu as pltpu
from jax.experimental.pallas import tpu_sc as plsc

F32 = jnp.float32
BF16 = jnp.bfloat16
I32 = jnp.int32
U32 = jnp.uint32

D_MODEL = 1024
EPS = 1e-6
GDN_HEADS = 4
GDN_DK = 128
GDN_DV = 128
GDN_CONV = 4
RET_HEADS = 4
RET_DK = 128
RET_DV = 256
ROPE_BASE = 10000.0
N_GROUPS = 4
EXPERTS_PER_GROUP = 8
N_EXPERTS = N_GROUPS * EXPERTS_PER_GROUP
D_EXPERT = 512

GDN_QK = GDN_HEADS * GDN_DK
GDN_V = GDN_HEADS * GDN_DV
RET_QK = RET_HEADS * RET_DK
RET_V = RET_HEADS * RET_DV

LANES = 128
CHUNK = 128
INV_BLOCK = 16
GDN_PREP_CHUNKS = 2
VMEM_LIMIT = 56 * 1024 * 1024

REPACK_COLS = 512
MERGE_PARTS = 1
TOP_K = 2
SLOT_TILE = 256
PLAN_TILE = 256
PLAN_STEP = 1024
HALF = D_MODEL // 2
SC_WORKERS = 32
SC_ROWS = 64
COLLECT_PARTS = 1
SC_COLLECT_CHUNKS = 16

PROJ_COLS = 3 * GDN_QK + GDN_V + 2 * RET_QK + 2 * RET_V + 2 * D_MODEL


def _silu(x):
    return x / (1.0 + jnp.exp(-x))


def _sigmoid(x):
    return 1.0 / (1.0 + jnp.exp(-x))


def _dot(a, b):
    return jnp.dot(a, b, preferred_element_type=F32)


def _dot_nt(a, b):
    return lax.dot_general(a, b, (((1,), (1,)), ((), ())), preferred_element_type=F32)


def _pack_bf16_pairs(x):
    bits = lax.bitcast_convert_type(x.astype(BF16).astype(F32), U32)
    packed = (bits[:, :HALF] >> 16) | (bits[:, HALF:] & jnp.uint32(0xFFFF0000))
    return lax.bitcast_convert_type(packed, I32)


def _unpack_bf16_pairs(p):
    p = lax.bitcast_convert_type(p, U32)
    lo = lax.bitcast_convert_type(p << 16, F32)
    hi = lax.bitcast_convert_type(p & jnp.uint32(0xFFFF0000), F32)
    return jnp.concatenate([lo, hi], axis=1)


def _proj_kernel(x_ref, nw_ref, w_ref, wab_ref, proj_ref, ab_ref, u_ref):
    j = pl.program_id(1)

    @pl.when(j == 0)
    def _():
        x = x_ref[...]
        u = x * lax.rsqrt(jnp.mean(x * x, axis=-1, keepdims=True) + EPS) * nw_ref[...]
        ub = u.astype(BF16)
        u_ref[...] = ub
        ab_ref[...] = _dot(ub, wab_ref[...])

    proj_ref[...] = _dot(u_ref[...], w_ref[...]).astype(BF16)


def _repack_kernel(wt_ref, abt_ref, main_ref, ab_ref):
    main_ref[...] = wt_ref[...].T.astype(BF16)

    @pl.when(pl.program_id(0) == 0)
    def _():
        ab = abt_ref[...].T
        ab_ref[...] = jnp.concatenate(
            [ab, jnp.zeros((ab.shape[0], LANES - ab.shape[1]), F32)], axis=1).astype(BF16)


def _repack_call(w_in_t):
    d_in, d = w_in_t.shape
    o_ab = 3 * GDN_QK
    n_ab = 2 * GDN_HEADS
    src = lambda r: pl.multiple_of(r * REPACK_COLS + jnp.where(r * REPACK_COLS >= o_ab, n_ab, 0), 8)
    return pl.pallas_call(
        _repack_kernel,
        grid=(PROJ_COLS // REPACK_COLS,),
        in_specs=[pl.BlockSpec((pl.Element(REPACK_COLS), pl.Element(d)), lambda r: (src(r), 0)),
                  pl.BlockSpec((pl.Element(n_ab), pl.Element(d)), lambda r: (o_ab, 0))],
        out_specs=[pl.BlockSpec((d, REPACK_COLS), lambda r: (0, r)),
                   pl.BlockSpec((d, LANES), lambda r: (0, 0))],
        out_shape=[jax.ShapeDtypeStruct((d, PROJ_COLS), BF16), jax.ShapeDtypeStruct((d, LANES), BF16)],
        compiler_params=pltpu.CompilerParams(dimension_semantics=("arbitrary",)),
        name="repack_w_in",
    )(w_in_t, w_in_t)


def _proj_call(x2, norm_w, w_main, w_ab, tm=1024, tn=3584):
    m = x2.shape[0]
    return pl.pallas_call(
        _proj_kernel,
        grid=(m // tm, PROJ_COLS // tn),
        in_specs=[
            pl.BlockSpec((tm, D_MODEL), lambda i, j: (i, 0)),
            pl.BlockSpec((1, D_MODEL), lambda i, j: (0, 0)),
            pl.BlockSpec((D_MODEL, tn), lambda i, j: (0, j)),
            pl.BlockSpec((D_MODEL, LANES), lambda i, j: (0, 0)),
        ],
        out_specs=[
            pl.BlockSpec((tm, tn), lambda i, j: (i, j)),
            pl.BlockSpec((tm, LANES), lambda i, j: (i, 0)),
        ],
        out_shape=[
            jax.ShapeDtypeStruct((m, PROJ_COLS), BF16),
            jax.ShapeDtypeStruct((m, LANES), F32),
        ],
        scratch_shapes=[pltpu.VMEM((tm, D_MODEL), BF16)],
        compiler_params=pltpu.CompilerParams(
            dimension_semantics=("arbitrary", "arbitrary"), vmem_limit_bytes=VMEM_LIMIT),
        name="proj",
    )(x2, norm_w, w_main, w_ab)


def _unit_lower_inverses(lows, ii, jj):
    eye = jnp.where(ii == jj, 1.0, 0.0).astype(F32)
    in_block = (ii // INV_BLOCK) == (jj // INV_BLOCK)
    ps = [jnp.where(in_block, -low, 0.0) for low in lows]
    ts = [eye + p for p in ps]
    span = 2
    while span < INV_BLOCK:
        ps = [_dot(p, p) for p in ps]
        ts = [t + _dot(t, p) for t, p in zip(ts, ps)]
        span *= 2
    s = INV_BLOCK
    while s < CHUNK:
        off_diag = ((ii // (2 * s)) == (jj // (2 * s))) & ((ii // s) != (jj // s))
        xs = [_dot(jnp.where(off_diag, low, 0.0), t) for low, t in zip(lows, ts)]
        ts = [t - _dot(t, x) for t, x in zip(ts, xs)]
        s *= 2
    return ts


def _gdn_kernel(qkv_ref, z_ref, ab_ref, convw_ref, alog_ref, dtb_ref, normw_ref, o_ref,
                b_s, o0_s, m_s, qp_s, gl_s, state_s, q_s, k_s, kb_s, rhs_s, dec_s, cv_s, qg_o, kdt_o,
                *, seq):
    nchunk = seq // CHUNK
    ii = lax.broadcasted_iota(jnp.int32, (CHUNK, CHUNK), 0)
    jj = lax.broadcasted_iota(jnp.int32, (CHUNK, CHUNK), 1)
    causal = ii >= jj
    strict = ii > jj
    tri = jnp.where(causal, 1.0, 0.0).astype(F32)
    neg_a = -jnp.exp(alog_ref[...])
    dtb = dtb_ref[...]

    def conv_cols(c, r0, lo, buf):
        x = qkv_ref[pl.ds(r0, CHUNK), lo:lo + LANES].astype(F32)
        prev0 = pl.multiple_of(jnp.maximum(r0 - 16, 0), 16)
        prev = qkv_ref[pl.ds(prev0, 16), lo:lo + LANES].astype(F32)
        buf[0:8, :] = prev[8:16] * jnp.where(c > 0, 1.0, 0.0)
        buf[8:8 + CHUNK, :] = x
        w = convw_ref[:, lo:lo + LANES]
        y = (w[3:4] * x + w[2:3] * buf[7:7 + CHUNK, :] + w[1:2] * buf[6:6 + CHUNK, :]
             + w[0:1] * buf[5:5 + CHUNK, :])
        return _silu(y)

    def l2n(x):
        return x * lax.rsqrt(jnp.sum(x * x, axis=-1, keepdims=True) + EPS)

    tri_b = tri.astype(BF16)

    def chunk_cumsum(g):
        g1 = g.astype(BF16)
        r1 = g - g1.astype(F32)
        g2 = r1.astype(BF16)
        g3 = (r1 - g2.astype(F32)).astype(BF16)
        return _dot(tri_b, g1) + (_dot(tri_b, g2) + _dot(tri_b, g3))

    def operands(cc, slot):
        for sub in range(GDN_PREP_CHUNKS):
            c = cc * GDN_PREP_CHUNKS + sub
            r0 = pl.multiple_of(c * CHUNK, CHUNK)
            ab = ab_ref[pl.ds(r0, CHUNK), :]
            xg = ab + dtb
            softplus = jnp.maximum(xg, 0.0) + jnp.log(1.0 + jnp.exp(-jnp.abs(xg)))
            g_all = neg_a * softplus
            beta_all = _sigmoid(ab)
            gc_all = chunk_cumsum(g_all)
            gc_t = gc_all.T
            gl_s[c] = jnp.exp(gc_all[CHUNK - 1:CHUNK, :])
            for h in range(GDN_HEADS):
                n = sub * GDN_HEADS + h
                bufs = [cv_s.at[slot, 3 * n + i] for i in range(3)]
                q = l2n(conv_cols(c, r0, h * GDN_DK, bufs[0])) * (GDN_DK ** -0.5)
                k = l2n(conv_cols(c, r0, GDN_QK + h * GDN_DK, bufs[1]))
                v = conv_cols(c, r0, 2 * GDN_QK + h * GDN_DV, bufs[2])
                gcol = gc_all[:, h:h + 1]
                grow = gc_t[h:h + 1, :]
                beta = beta_all[:, GDN_HEADS + h:GDN_HEADS + h + 1]
                dec_s[slot, n] = jnp.where(causal, jnp.exp(gcol - grow), 0.0)
                eg = jnp.exp(gcol)
                kb = k * beta
                q_s[slot, n] = q.astype(BF16)
                k_s[slot, n] = k.astype(BF16)
                kb_s[slot, n] = kb.astype(BF16)
                rhs_s[slot, n] = jnp.concatenate([v * beta, kb * eg], axis=1).astype(BF16)
                qg_o[slot, n] = q * eg
                kd = k * jnp.exp(gc_all[CHUNK - 1:CHUNK, h:h + 1] - gcol)
                kdt_o[slot, n] = kd.T.astype(BF16)

    def solve(cc, slot, between):
        items = [(cc * GDN_PREP_CHUNKS + sub, h, sub * GDN_HEADS + h)
                 for sub in range(GDN_PREP_CHUNKS) for h in range(GDN_HEADS)]
        kks = [_dot_nt(kb_s[slot, n], k_s[slot, n]) for _, _, n in items]
        qks = [_dot_nt(q_s[slot, n], k_s[slot, n]) for _, _, n in items]
        lows = [jnp.where(strict, kk * dec_s[slot, n], 0.0) for kk, (_, _, n) in zip(kks, items)]
        attns = [(qk * dec_s[slot, n]).astype(BF16) for qk, (_, _, n) in zip(qks, items)]
        for step in between[:len(between) // 2]:
            step()
        ts = _unit_lower_inverses(lows, ii, jj)
        for step in between[len(between) // 2:]:
            step()
        uws = [_dot(t.astype(BF16), rhs_s[slot, n]).astype(BF16) for t, (_, _, n) in zip(ts, items)]
        kds = [_dot(kdt_o[slot, n], uw) for uw, (_, _, n) in zip(uws, items)]
        ats = [_dot(attn, uw) for attn, uw in zip(attns, uws)]
        for kd_uw, at_uw, (c, h, n) in zip(kds, ats, items):
            b_s[c, h] = kd_uw[:, :GDN_DV]
            m_s[c, h] = (-kd_uw[:, GDN_DV:]).astype(BF16)
            o0_s[c, h] = at_uw[:, :GDN_DV]
            qp_s[c, h] = (qg_o[slot, n] - at_uw[:, GDN_DV:]).astype(BF16)

    ngroup = nchunk // GDN_PREP_CHUNKS
    per_trip = 2 * GDN_PREP_CHUNKS
    normw = normw_ref[...]
    state_s[...] = jnp.zeros_like(state_s)
    for c0 in range(per_trip):
        gl_s[c0] = jnp.zeros((1, LANES), F32)
        for h in range(GDN_HEADS):
            b_s[c0, h] = jnp.zeros((CHUNK, GDN_DV), F32)
            o0_s[c0, h] = jnp.zeros((CHUNK, GDN_DV), F32)
            m_s[c0, h] = jnp.zeros((CHUNK, GDN_DK), BF16)
            qp_s[c0, h] = jnp.zeros((CHUNK, GDN_DK), BF16)
    operands(0, 0)

    def prep(i, carry):
        first = jnp.maximum(per_trip * (i - 1), 0)
        steps = [functools.partial(scan, first + k, 0) for k in range(per_trip)]
        operands(2 * i + 1, 1)
        solve(2 * i, 0, steps[:per_trip // 2])
        operands(jnp.minimum(2 * i + 2, ngroup - 1), 0)
        solve(2 * i + 1, 1, steps[per_trip // 2:])
        return carry

    def scan(c, carry):
        r0 = pl.multiple_of(c * CHUNK, CHUNK)
        gl = gl_s[c]
        heads = range(GDN_HEADS)
        ss = [state_s[h] for h in heads]
        sbs = [s.astype(BF16) for s in ss]
        mss = [_dot(m_s[c, h], sbs[h]) for h in heads]
        qss = [_dot(qp_s[c, h], sbs[h]) for h in heads]
        for h in heads:
            state_s[h] = ss[h] * gl[:, h:h + 1] + (mss[h] + b_s[c, h])
            o = qss[h] + o0_s[c, h]
            z = z_ref[pl.ds(r0, CHUNK), h * GDN_DV:(h + 1) * GDN_DV].astype(F32)
            on = o * lax.rsqrt(jnp.mean(o * o, axis=-1, keepdims=True) + EPS) * normw
            o_ref[pl.ds(r0, CHUNK), h * GDN_DV:(h + 1) * GDN_DV] = (on * _silu(z)).astype(BF16)
        return carry

    ntrip = ngroup // 2
    lax.fori_loop(0, ntrip, prep, 0)
    lax.fori_loop(per_trip * (ntrip - 1), nchunk, scan, 0)


def _gdn_call(proj, ab, conv_w8, alog_p, dtb_p, normw, batch, seq):
    nchunk = seq // CHUNK
    hs = (nchunk, GDN_HEADS, CHUNK, CHUNK)
    ops = (2, GDN_PREP_CHUNKS * GDN_HEADS, CHUNK, CHUNK)
    return pl.pallas_call(
        functools.partial(_gdn_kernel, seq=seq),
        grid=(batch,),
        in_specs=[
            pl.BlockSpec((seq, 3 * GDN_QK), lambda b: (b, 0)),
            pl.BlockSpec((seq, GDN_V), lambda b: (b, 3)),
            pl.BlockSpec((seq, LANES), lambda b: (b, 0)),
            pl.BlockSpec((8, 3 * GDN_QK), lambda b: (0, 0)),
            pl.BlockSpec((1, LANES), lambda b: (0, 0)),
            pl.BlockSpec((1, LANES), lambda b: (0, 0)),
            pl.BlockSpec((1, GDN_DV), lambda b: (0, 0)),
        ],
        out_specs=pl.BlockSpec((seq, GDN_V), lambda b: (b, 0)),
        out_shape=jax.ShapeDtypeStruct((batch * seq, GDN_V), BF16),
        scratch_shapes=[
            pltpu.VMEM(hs, F32), pltpu.VMEM(hs, F32), pltpu.VMEM(hs, BF16), pltpu.VMEM(hs, BF16),
            pltpu.VMEM((nchunk, 1, LANES), F32),
            pltpu.VMEM((GDN_HEADS, GDN_DK, GDN_DV), F32),
            pltpu.VMEM(ops, BF16), pltpu.VMEM(ops, BF16), pltpu.VMEM(ops, BF16),
            pltpu.VMEM(ops[:3] + (2 * CHUNK,), BF16),
            pltpu.VMEM(ops, F32),
            pltpu.VMEM((2, 3 * ops[1], 8 + CHUNK, LANES), F32),
            pltpu.VMEM(ops, F32), pltpu.VMEM(ops, BF16),
        ],
        compiler_params=pltpu.CompilerParams(
            dimension_semantics=("arbitrary",), vmem_limit_bytes=VMEM_LIMIT),
        name="gdn",
    )(proj, proj, ab, conv_w8, alog_p, dtb_p, normw)


def _ret_kernel(qk_ref, v_ref, g_ref, sin_ref, cos_ref, inner_ref, kdec_ref, qdec_ref, cdec_ref,
                o_ref, state_s, q_s, qd_s, k_s, kt_s, *, seq):
    nchunk = seq // CHUNK
    lane = lax.broadcasted_iota(jnp.int32, (CHUNK, RET_DK), 1)
    even = (lane % 2) == 0

    def rotate(x, sin, cos):
        nxt = pltpu.roll(x, RET_DK - 1, axis=1)
        prv = pltpu.roll(x, 1, axis=1)
        return x * cos + jnp.where(even, -nxt, prv) * sin

    state_s[...] = jnp.zeros_like(state_s)
    kdec = kdec_ref[...]
    qdec = qdec_ref[...]
    cdec = cdec_ref[...]

    heads = range(RET_HEADS)

    def operands(c, slot):
        r0 = pl.multiple_of(c * CHUNK, CHUNK)
        sin = sin_ref[pl.ds(r0, CHUNK), :]
        cos = cos_ref[pl.ds(r0, CHUNK), :]
        for h in heads:
            q = rotate(qk_ref[pl.ds(r0, CHUNK), h * RET_DK:(h + 1) * RET_DK].astype(F32), sin, cos)
            k = rotate(qk_ref[pl.ds(r0, CHUNK), RET_QK + h * RET_DK:RET_QK + (h + 1) * RET_DK]
                       .astype(F32), sin, cos) * (RET_DK ** -0.5)
            q_s[slot, h] = q.astype(BF16)
            qd_s[slot, h] = (q * qdec[:, h:h + 1]).astype(BF16)
            k_s[slot, h] = k.astype(BF16)
            kt_s[slot, h] = (k * kdec[:, h:h + 1]).T.astype(BF16)

    def outputs(c, slot):
        r0 = pl.multiple_of(c * CHUNK, CHUNK)
        vs = [v_ref[pl.ds(r0, CHUNK), h * RET_DV:(h + 1) * RET_DV] for h in heads]
        ss = [state_s[h] for h in heads]
        qks = [_dot_nt(q_s[slot, h], k_s[slot, h]) for h in heads]
        inters = [_dot(qd_s[slot, h], ss[h].astype(BF16)) for h in heads]
        kvs = [_dot(kt_s[slot, h], vs[h]) for h in heads]
        intras = [_dot((qks[h] * inner_ref[h]).astype(BF16), vs[h]) for h in heads]
        for h in heads:
            state_s[h] = ss[h] * cdec[:, h:h + 1] + kvs[h]
            o = intras[h] + inters[h]
            gate = g_ref[pl.ds(r0, CHUNK), h * RET_DV:(h + 1) * RET_DV].astype(F32)
            on = o * lax.rsqrt(jnp.mean(o * o, axis=-1, keepdims=True) + EPS)
            o_ref[pl.ds(r0, CHUNK), h * RET_DV:(h + 1) * RET_DV] = (on * _silu(gate)).astype(BF16)

    operands(0, 0)

    def body(i, carry):
        operands(2 * i + 1, 1)
        outputs(2 * i, 0)
        operands(jnp.minimum(2 * i + 2, nchunk - 1), 0)
        outputs(2 * i + 1, 1)
        return carry

    lax.fori_loop(0, nchunk // 2, body, 0)


def _ret_call(proj, sin, cos, inner, kdec, qdec, cdec, batch, seq):
    return pl.pallas_call(
        functools.partial(_ret_kernel, seq=seq),
        grid=(batch,),
        in_specs=[
            pl.BlockSpec((seq, 2 * RET_QK), lambda b: (b, 2)),
            pl.BlockSpec((seq, RET_V), lambda b: (b, 3)),
            pl.BlockSpec((seq, RET_V), lambda b: (b, 4)),
            pl.BlockSpec((seq, RET_DK), lambda b: (0, 0)),
            pl.BlockSpec((seq, RET_DK), lambda b: (0, 0)),
            pl.BlockSpec((RET_HEADS, CHUNK, CHUNK), lambda b: (0, 0, 0)),
            pl.BlockSpec((CHUNK, LANES), lambda b: (0, 0)),
            pl.BlockSpec((CHUNK, LANES), lambda b: (0, 0)),
            pl.BlockSpec((1, LANES), lambda b: (0, 0)),
        ],
        out_specs=pl.BlockSpec((seq, RET_V), lambda b: (b, 0)),
        out_shape=jax.ShapeDtypeStruct((batch * seq, RET_V), BF16),
        scratch_shapes=[pltpu.VMEM((RET_HEADS, RET_DK, RET_DV), F32)]
        + [pltpu.VMEM((2, RET_HEADS, CHUNK, RET_DK), BF16)] * 4,
        compiler_params=pltpu.CompilerParams(
            dimension_semantics=("arbitrary",), vmem_limit_bytes=VMEM_LIMIT),
        name="retention",
    )(proj, proj, proj, sin, cos, inner, kdec, qdec, cdec)


def _retention_tables(seq):
    inv_freq = 1.0 / (ROPE_BASE ** jnp.linspace(0.0, 1.0, RET_DK // 2, dtype=F32))
    ang = jnp.arange(seq, dtype=F32)[:, None] * inv_freq[None, :]
    sin = jnp.repeat(jnp.sin(ang), 2, axis=-1)
    cos = jnp.repeat(jnp.cos(ang), 2, axis=-1)
    log_gamma = jnp.log(1.0 - 2.0 ** (-5.0 - jnp.arange(RET_HEADS, dtype=F32)))
    idx = jnp.arange(CHUNK, dtype=F32)
    causal = jnp.tril(jnp.ones((CHUNK, CHUNK), dtype=bool))
    rel = jnp.where(causal, idx[:, None] - idx[None, :], 0.0)
    inner = jnp.where(causal, jnp.exp(rel[None] * log_gamma[:, None, None]), 0.0)
    k_decay = jnp.exp(log_gamma[:, None] * (CHUNK - 1.0 - idx)[None, :])
    q_decay = jnp.exp(log_gamma[:, None] * (idx + 1.0)[None, :])
    chunk_decay = jnp.exp(log_gamma * CHUNK)
    pad = LANES - RET_HEADS
    kdec = jnp.pad(k_decay.T, ((0, 0), (0, pad)))
    qdec = jnp.pad(q_decay.T, ((0, 0), (0, pad)))
    cdec = jnp.pad(chunk_decay[None, :], ((0, 0), (0, pad)))
    return sin, cos, inner, kdec, qdec, cdec


def _merge_kernel(x_ref, ya_ref, yb_ref, ma_ref, mb_ref, wa_ref, wr_ref, wo_ref, nw_ref,
                  wrt_ref, br_ref, h_ref, xn_ref, ridx_ref, rw_ref, cnt_ref):
    tm = x_ref.shape[0] // MERGE_PARTS
    rows = [pl.ds(i * tm, tm) for i in range(MERGE_PARTS)]
    a_ = [_dot(ya_ref[r, :], wa_ref[...]) for r in rows]
    r_ = [_dot(yb_ref[r, :], wr_ref[...]) for r in rows]
    merged = [(_sigmoid(ma_ref[r, :].astype(F32)) * a + _sigmoid(mb_ref[r, :].astype(F32)) * rr)
              .astype(BF16) for r, a, rr in zip(rows, a_, r_)]
    hs = [x_ref[r, :] + _dot(mg, wo_ref[...]) for r, mg in zip(rows, merged)]
    xcats = []
    for r, h in zip(rows, hs):
        h_ref[r, :] = h
        xn = h * lax.rsqrt(jnp.mean(h * h, axis=-1, keepdims=True) + EPS) * nw_ref[...]
        xn_ref[r, :] = _pack_bf16_pairs(xn)
        xh = xn.astype(BF16)
        xl = (xn - xh.astype(F32)).astype(BF16)
        xcats.append(jnp.concatenate([xh, xl], axis=0))
    parts_ = [_dot(xc, wrt_ref[...]) for xc in xcats]
    counts = jnp.zeros((1, LANES), F32)
    for r, parts in zip(rows, parts_):
        counts = counts + _route(parts, tm, br_ref[...], ridx_ref.at[r, :], rw_ref.at[r, :])

    @pl.when(pl.program_id(0) == 0)
    def _():
        cnt_ref[...] = jnp.zeros_like(cnt_ref)

    cnt_ref[...] += jnp.broadcast_to(counts, cnt_ref.shape)


def _route(parts, tm, bias, ridx_ref, rw_ref):
    logits = (parts[:tm, :LANES] + (parts[tm:, :LANES] + parts[:tm, LANES:]
                                    + parts[tm:, LANES:])) + bias
    lane = lax.broadcasted_iota(jnp.int32, (tm, LANES), 1)
    neg = -jnp.inf
    gl = jnp.where((lane >= N_EXPERTS) & (lane < N_EXPERTS + N_GROUPS), logits, neg)
    gmax = jnp.max(gl, axis=-1, keepdims=True)
    gidx = jnp.min(jnp.where(gl == gmax, lane, LANES), axis=-1, keepdims=True) - N_EXPERTS
    g_w = 1.0 / jnp.sum(jnp.exp(gl - gmax), axis=-1, keepdims=True)
    el = jnp.where((lane // EXPERTS_PER_GROUP == gidx) & (lane < N_EXPERTS), logits, neg)
    m1 = jnp.max(el, axis=-1, keepdims=True)
    i1 = jnp.min(jnp.where(el == m1, lane, LANES), axis=-1, keepdims=True)
    el2 = jnp.where(lane == i1, neg, el)
    m2 = jnp.max(el2, axis=-1, keepdims=True)
    i2 = jnp.min(jnp.where(el2 == m2, lane, LANES), axis=-1, keepdims=True)
    e2 = jnp.exp(m2 - m1)
    p1 = g_w / (1.0 + e2)
    p2 = g_w * e2 / (1.0 + e2)
    ridx_ref[...] = jnp.where(lane == 0, i1, jnp.where(lane == 1, i2, 0))
    rw_ref[...] = jnp.where(lane == 0, p1, jnp.where(lane == 1, p2, 0.0))
    onehot = jnp.where((lane == i1) | (lane == i2), 1.0, 0.0)
    return jnp.sum(onehot, axis=0, keepdims=True)


def _merge_call(x2, ya, yb, proj, wa, wr, wo, nw, w_router, b_router, tm=1024):
    m = x2.shape[0]
    full = lambda shape: pl.BlockSpec(shape, lambda i: (0, 0))
    return pl.pallas_call(
        _merge_kernel,
        grid=(m // tm,),
        in_specs=[
            pl.BlockSpec((tm, D_MODEL), lambda i: (i, 0)),
            pl.BlockSpec((tm, GDN_V), lambda i: (i, 0)),
            pl.BlockSpec((tm, RET_V), lambda i: (i, 0)),
            pl.BlockSpec((tm, D_MODEL), lambda i: (i, 5)),
            pl.BlockSpec((tm, D_MODEL), lambda i: (i, 6)),
            full((GDN_V, D_MODEL)), full((RET_V, D_MODEL)), full((D_MODEL, D_MODEL)),
            full((1, D_MODEL)),
            full((D_MODEL, 2 * LANES)), full((1, LANES)),
        ],
        out_specs=[
            pl.BlockSpec((tm, D_MODEL), lambda i: (i, 0)),
            pl.BlockSpec((tm, HALF), lambda i: (i, 0)),
            pl.BlockSpec((tm, LANES), lambda i: (i, 0)),
            pl.BlockSpec((tm, LANES), lambda i: (i, 0)),
            pl.BlockSpec((8, LANES), lambda i: (0, 0)),
        ],
        out_shape=[
            jax.ShapeDtypeStruct((m, D_MODEL), F32),
            jax.ShapeDtypeStruct((m, HALF), I32),
            jax.ShapeDtypeStruct((m, LANES), I32),
            jax.ShapeDtypeStruct((m, LANES), F32),
            jax.ShapeDtypeStruct((8, LANES), F32),
        ],
        compiler_params=pltpu.CompilerParams(
            dimension_semantics=("arbitrary",), vmem_limit_bytes=VMEM_LIMIT),
        name="merge_router",
    )(x2, ya, yb, proj, proj, wa, wr, wo, nw, w_router, b_router)


def _slot_counts(m):
    n_slots = TOP_K * m + N_EXPERTS * SLOT_TILE
    return n_slots, n_slots // SLOT_TILE


def _lane_prefix_sum(x, lane):
    s = 1
    while s < LANES:
        x = x + jnp.where(lane >= s, pltpu.roll(x, s, axis=1), 0.0)
        s *= 2
    return x


def _plan_kernel(ridx_ref, cnt_ref, slots_ref, tile_ref, carry_s, off_s):
    i = pl.program_id(0)
    lane = lax.broadcasted_iota(I32, (PLAN_TILE, LANES), 1)
    row = lax.broadcasted_iota(I32, (PLAN_TILE, LANES), 0)
    lane1 = lane[0:1]

    @pl.when(i == 0)
    def _():
        cnt = cnt_ref[0:1, :]
        tile = float(SLOT_TILE)
        padded = jnp.floor((cnt + (tile - 1.0)) / tile) * tile
        incl = _lane_prefix_sum(padded, lane1)
        off = incl - padded
        off_s[...] = off
        carry_s[...] = jnp.zeros_like(carry_s)
        first = (row * SLOT_TILE).astype(F32)
        ended = jnp.where((lane < N_EXPERTS) & (incl <= first), 1.0, 0.0)
        tile_e = jnp.sum(ended, axis=-1, keepdims=True)
        last = jnp.sum(jnp.where(lane.astype(F32) == tile_e, off + cnt, 0.0), axis=-1, keepdims=True)
        used = jnp.clip(last - first[:, 0:1], 0.0, tile)
        tile_ref[...] = jnp.where(lane == 0, tile_e, jnp.where(lane == 1, used, 0.0)).astype(I32)

    strict = jnp.where(row[:, 0:1] > lax.broadcasted_iota(I32, (PLAN_TILE, PLAN_TILE), 1),
                       1.0, 0.0).astype(BF16)
    off = off_s[...]
    carry = carry_s[...]
    for sb in range(PLAN_STEP // PLAN_TILE):
        rows = pl.ds(sb * PLAN_TILE, PLAN_TILE)
        e1 = ridx_ref[rows, 0:1]
        e2 = ridx_ref[rows, 1:2]
        onehot = jnp.where((lane == e1) | (lane == e2), 1.0, 0.0)
        pos = _dot(strict, onehot.astype(BF16)) + (carry + off)
        s1 = jnp.sum(jnp.where(lane == e1, pos, 0.0), axis=-1, keepdims=True)
        s2 = jnp.sum(jnp.where(lane == e2, pos, 0.0), axis=-1, keepdims=True)
        both = jnp.where(lane == 0, s1, jnp.where(lane == 1, s2, 0.0))
        for q in range(PLAN_TILE // LANES):
            t = both[q * LANES:(q + 1) * LANES].T
            c0 = sb * PLAN_TILE + q * LANES
            slots_ref[:, c0:c0 + LANES] = t[0:8].astype(I32)
        carry = carry + jnp.sum(onehot, axis=0, keepdims=True)
    carry_s[...] = carry


def _plan_call(ridx, cnt):
    m = ridx.shape[0]
    _, n_tiles = _slot_counts(m)
    assert n_tiles <= PLAN_TILE and SLOT_TILE == PLAN_TILE
    return pl.pallas_call(
        _plan_kernel,
        grid=(m // PLAN_STEP,),
        in_specs=[pl.BlockSpec((PLAN_STEP, LANES), lambda i: (i, 0)),
                  pl.BlockSpec((8, LANES), lambda i: (0, 0))],
        out_specs=[
            pl.BlockSpec((8, PLAN_STEP), lambda i: (0, i)),
            pl.BlockSpec((PLAN_TILE, LANES), lambda i: (0, 0)),
        ],
        out_shape=[
            jax.ShapeDtypeStruct((8, m), I32),
            jax.ShapeDtypeStruct((PLAN_TILE, LANES), I32),
        ],
        scratch_shapes=[pltpu.VMEM((1, LANES), F32), pltpu.VMEM((1, LANES), F32)],
        compiler_params=pltpu.CompilerParams(dimension_semantics=("arbitrary",)),
        name="dispatch_plan",
    )(ridx, cnt)


def _sc_mesh():
    return plsc.VectorSubcoreMesh(core_axis_name="c", subcore_axis_name="s")


def _sc_worker():
    return lax.axis_index("s") * 2 + lax.axis_index("c")


def _sc_dispatch(xn, slot1, slot2, n_rows):
    m = xn.shape[0]
    per = m // SC_WORKERS
    n_pairs = per // (2 * SC_ROWS)

    @functools.partial(
        pl.kernel, mesh=_sc_mesh(),
        out_type=jax.ShapeDtypeStruct((n_rows, HALF), I32),
        scratch_types=[pltpu.VMEM((per // SC_ROWS, SC_ROWS), I32), pltpu.VMEM((per // SC_ROWS, SC_ROWS), I32),
                       pltpu.VMEM((SC_ROWS, HALF), I32), pltpu.VMEM((SC_ROWS, HALF), I32),
                       pltpu.SemaphoreType.DMA, pltpu.SemaphoreType.DMA, pltpu.SemaphoreType.DMA],
        name="sc_dispatch")
    def k(x_hbm, s1_hbm, s2_hbm, o_hbm, i1_v, i2_v, rows0, rows1, sem_r0, sem_r1, sem_w):
        wid = _sc_worker()
        base = wid * per

        def read(chunk, rows_v, sem):
            return pltpu.make_async_copy(x_hbm.at[pl.ds(base + chunk * SC_ROWS, SC_ROWS)], rows_v, sem)

        def scatter(chunk, rows_v):
            c1 = pltpu.async_copy(rows_v, o_hbm.at[i1_v.at[chunk]], sem_w)
            c2 = pltpu.async_copy(rows_v, o_hbm.at[i2_v.at[chunk]], sem_w)
            c1.wait()
            c2.wait()

        read(0, rows0, sem_r0).start()
        pltpu.sync_copy(s1_hbm.at[pl.ds(wid * (per // SC_ROWS), per // SC_ROWS)], i1_v)
        pltpu.sync_copy(s2_hbm.at[pl.ds(wid * (per // SC_ROWS), per // SC_ROWS)], i2_v)

        @pl.loop(0, n_pairs)
        def _(i):
            read(2 * i, rows0, sem_r0).wait()
            read(2 * i + 1, rows1, sem_r1).start()
            scatter(2 * i, rows0)
            read(2 * i + 1, rows1, sem_r1).wait()

            @pl.when(i + 1 < n_pairs)
            def _():
                read(2 * i + 2, rows0, sem_r0).start()

            scatter(2 * i + 1, rows1)

    return k(xn, slot1, slot2)


def _sc_collect(ys, slot1, slot2):
    n_chunks, rows = slot1.shape
    m = n_chunks * rows
    per = n_chunks // SC_WORKERS
    n_pairs = per // 2
    out = jax.ShapeDtypeStruct((m, HALF), I32)
    buf = pltpu.VMEM((rows, HALF), I32)

    @functools.partial(
        pl.kernel, mesh=_sc_mesh(), out_type=[out, out],
        scratch_types=[pltpu.VMEM((per, rows), I32), pltpu.VMEM((per, rows), I32), buf, buf, buf, buf,
                       pltpu.SemaphoreType.DMA, pltpu.SemaphoreType.DMA, pltpu.SemaphoreType.DMA],
        name="sc_collect")
    def k(y_hbm, s1_hbm, s2_hbm, g1_hbm, g2_hbm, i1_v, i2_v, a1_v, a2_v, b1_v, b2_v,
          sem_a, sem_b, sem_w):
        wid = _sc_worker()
        pltpu.sync_copy(s1_hbm.at[pl.ds(wid * per, per)], i1_v)
        pltpu.sync_copy(s2_hbm.at[pl.ds(wid * per, per)], i2_v)

        def gathers(chunk, r1_v, r2_v, sem):
            return (pltpu.make_async_copy(y_hbm.at[i1_v.at[chunk]], r1_v, sem),
                    pltpu.make_async_copy(y_hbm.at[i2_v.at[chunk]], r2_v, sem))

        def write_out(chunk, r1_v, r2_v):
            t0 = (wid * per + chunk) * rows
            w1 = pltpu.async_copy(r1_v, g1_hbm.at[pl.ds(t0, rows)], sem_w)
            w2 = pltpu.async_copy(r2_v, g2_hbm.at[pl.ds(t0, rows)], sem_w)
            w1.wait()
            w2.wait()

        for c in gathers(0, a1_v, a2_v, sem_a):
            c.start()

        @pl.loop(0, n_pairs)
        def _(i):
            for c in gathers(2 * i, a1_v, a2_v, sem_a):
                c.wait()
            for c in gathers(2 * i + 1, b1_v, b2_v, sem_b):
                c.start()
            write_out(2 * i, a1_v, a2_v)
            for c in gathers(2 * i + 1, b1_v, b2_v, sem_b):
                c.wait()

            @pl.when(i + 1 < n_pairs)
            def _():
                for c in gathers(2 * i + 2, a1_v, a2_v, sem_a):
                    c.start()

            write_out(2 * i + 1, b1_v, b2_v)

    return k(ys, slot1, slot2)


def _expert_kernel(te_ref, used_ref, xs_ref, wg_hbm, wu_hbm, wd_hbm, ys_ref, wg_b, wu_b, wd_b,
                   wg_f, wu_f, wd_f, sem, slot_s):
    j = pl.program_id(0)
    n = pl.num_programs(0)
    e = te_ref[j]
    prev = te_ref[jnp.maximum(j - 1, 0)]
    valid = e < N_EXPERTS

    def weight_copies(expert, slot):
        return [pltpu.make_async_copy(hbm.at[expert], buf.at[slot], sem.at[slot, i])
                for i, (hbm, buf) in enumerate(((wg_hbm, wg_f), (wu_hbm, wu_f), (wd_hbm, wd_f)))]

    @pl.when((j == 0) & valid)
    def _():
        slot_s[0] = 0
        for c in weight_copies(e, 0):
            c.start()

    @pl.when(((j == 0) | (e != prev)) & valid)
    def _():
        slot = slot_s[0]
        for c in weight_copies(e, slot):
            c.wait()
        k = lax.while_loop(lambda k: (k < n) & (te_ref[jnp.minimum(k, n - 1)] == e),
                           lambda k: k + 1, j + 1)
        nxt = te_ref[jnp.minimum(k, n - 1)]

        @pl.when((k < n) & (nxt < N_EXPERTS))
        def _():
            for c in weight_copies(nxt, 1 - slot):
                c.start()

        wg_b[...] = wg_f[slot].astype(BF16)
        wu_b[...] = wu_f[slot].astype(BF16)
        wd_b[...] = wd_f[slot].astype(BF16)
        slot_s[0] = 1 - slot

    @pl.when(valid)
    def _():
        half = SLOT_TILE // 2
        rows = [pl.ds(i * half, half) for i in range(2)]
        row_id = lax.broadcasted_iota(I32, (half, HALF), 0)
        xs = [_unpack_bf16_pairs(jnp.where(row_id + i * half < used_ref[j], xs_ref[r, :], 0))
              .astype(BF16) for i, r in enumerate(rows)]
        gs = [_dot(x, wg_b[...]) for x in xs]
        us = [_dot(x, wu_b[...]) for x in xs]
        hids = [(_silu(g) * u).astype(BF16) for g, u in zip(gs, us)]
        ys = [_dot(hid, wd_b[...]) for hid in hids]
        for r, y in zip(rows, ys):
            ys_ref[r, :] = _pack_bf16_pairs(y)

    @pl.when(e >= N_EXPERTS)
    def _():
        ys_ref[...] = jnp.zeros_like(ys_ref)


def _expert_call(tile_expert, tile_used, xs, wg, wu, wd, n_tiles):
    hbm = pl.BlockSpec(memory_space=pl.ANY)
    return pl.pallas_call(
        _expert_kernel,
        grid_spec=pltpu.PrefetchScalarGridSpec(
            num_scalar_prefetch=2,
            grid=(n_tiles,),
            in_specs=[pl.BlockSpec((SLOT_TILE, HALF), lambda j, te, used: (j, 0)), hbm, hbm, hbm],
            out_specs=pl.BlockSpec((SLOT_TILE, HALF), lambda j, te, used: (j, 0)),
            scratch_shapes=[
                pltpu.VMEM((D_MODEL, D_EXPERT), BF16), pltpu.VMEM((D_MODEL, D_EXPERT), BF16),
                pltpu.VMEM((D_EXPERT, D_MODEL), BF16),
                pltpu.VMEM((2, D_MODEL, D_EXPERT), F32), pltpu.VMEM((2, D_MODEL, D_EXPERT), F32),
                pltpu.VMEM((2, D_EXPERT, D_MODEL), F32),
                pltpu.SemaphoreType.DMA((2, 3)),
                pltpu.SMEM((1,), I32),
            ],
        ),
        out_shape=jax.ShapeDtypeStruct((n_tiles * SLOT_TILE, HALF), I32),
        compiler_params=pltpu.CompilerParams(
            dimension_semantics=("arbitrary",), vmem_limit_bytes=VMEM_LIMIT),
        name="experts",
    )(tile_expert, tile_used, xs, wg, wu, wd)


def _final_kernel(h_ref, g1_ref, g2_ref, rw_ref, nw_ref, *rest):
    o_ref = rest[-1]
    rw = rw_ref[...]
    y = rw[:, 0:1] * _unpack_bf16_pairs(g1_ref[...]) + rw[:, 1:2] * _unpack_bf16_pairs(g2_ref[...])
    h = h_ref[...] + y
    o_ref[...] = h * lax.rsqrt(jnp.mean(h * h, axis=-1, keepdims=True) + EPS) * nw_ref[...]


def _final_call(h1, g1, g2, rw, nw, part, out_so_far, tm=1024):
    m = h1.shape[0]
    steps = g1.shape[0] // tm
    off = part * steps
    in_specs = [
        pl.BlockSpec((tm, D_MODEL), lambda i: (i + off, 0)),
        pl.BlockSpec((tm, HALF), lambda i: (i, 0)),
        pl.BlockSpec((tm, HALF), lambda i: (i, 0)),
        pl.BlockSpec((tm, LANES), lambda i: (i + off, 0)),
        pl.BlockSpec((1, D_MODEL), lambda i: (0, 0)),
    ]
    args = [h1, g1, g2, rw, nw]
    aliases = {}
    if out_so_far is not None:
        in_specs.append(pl.BlockSpec(memory_space=pl.ANY))
        args.append(out_so_far)
        aliases = {len(args) - 1: 0}
    return pl.pallas_call(
        _final_kernel,
        grid=(steps,),
        in_specs=in_specs,
        out_specs=pl.BlockSpec((tm, D_MODEL), lambda i: (i + off, 0)),
        out_shape=jax.ShapeDtypeStruct((m, D_MODEL), F32),
        input_output_aliases=aliases,
        compiler_params=pltpu.CompilerParams(dimension_semantics=("arbitrary",)),
        name="combine_final",
    )(*args)


def _pad_lanes(a):
    return jnp.pad(a, ((0, 0), (0, LANES - a.shape[1])))


def kernel(x, norm_mix_w, w_in, conv_w, A_log, dt_bias, gdn_norm_w, w_up_gdn, w_up_ret, w_out,
           norm_ffn_w, w_group, b_group, w_expert, b_expert, w_gate, w_up, w_down, norm_final_w):
    batch, seq, d = x.shape
    m = batch * seq
    h = x.reshape(m, d)
    depth = w_in.shape[0]
    sin, cos, inner, kdec, qdec, cdec = _retention_tables(seq)
    for l in range(depth):
        w_main, w_ab = _repack_call(jnp.transpose(w_in[l]))
        proj, ab = _proj_call(h, norm_mix_w[l][None, :], w_main, w_ab)

        conv8 = jnp.pad(conv_w[l], ((0, 8 - GDN_CONV), (0, 0)))
        ya = _gdn_call(proj, ab, conv8, _pad_lanes(A_log[l][None, :]), _pad_lanes(dt_bias[l][None, :]),
                       gdn_norm_w[l][None, :], batch, seq)
        yb = _ret_call(proj, sin, cos, inner, kdec, qdec, cdec, batch, seq)

        w_router = _pad_lanes(jnp.concatenate([w_expert[l], w_group[l]], axis=1))
        wr_hi = w_router.astype(BF16)
        wr_lo = (w_router - wr_hi.astype(F32)).astype(BF16)
        b_router = _pad_lanes(jnp.concatenate([b_expert[l], b_group[l]])[None, :])
        h1, xn, ridx, rw, cnt = _merge_call(
            h, ya, yb, proj, w_up_gdn[l].astype(BF16), w_up_ret[l].astype(BF16), w_out[l].astype(BF16),
            norm_ffn_w[l][None, :], jnp.concatenate([wr_hi, wr_lo], axis=1), b_router)

        n_slots, n_tiles = _slot_counts(m)
        slots, tiles = _plan_call(ridx, cnt)
        slot1, slot2 = slots[0].reshape(-1, SC_ROWS), slots[1].reshape(-1, SC_ROWS)
        xs = _sc_dispatch(xn, slot1, slot2, n_slots)
        ys = _expert_call(tiles[:n_tiles, 0], tiles[:n_tiles, 1], xs,
                          w_gate[l], w_up[l], w_down[l], n_tiles)
        assert depth == 1
        rows = m // (COLLECT_PARTS * SC_WORKERS * SC_COLLECT_CHUNKS)
        c1 = slots[0].reshape(COLLECT_PARTS, -1, rows)
        c2 = slots[1].reshape(COLLECT_PARTS, -1, rows)
        h = None
        for p in range(COLLECT_PARTS):
            g1, g2 = _sc_collect(ys, c1[p], c2[p])
            h = _final_call(h1, g1, g2, rw, norm_final_w[None, :], p, h)
    return h.reshape(batch, seq, d)
```

```python
import functools

import jax
import jax.numpy as jnp
from jax import lax
from jax.experimental import pallas as pl
from jax.experimental.pallas import tpu as pltpu
from jax.experimental.pallas import tpu_sc as plsc

F32 = jnp.float32
BF16 = jnp.bfloat16
I32 = jnp.int32
U32 = jnp.uint32

D_MODEL = 1024
EPS = 1e-6
GDN_HEADS = 4
GDN_DK = 128
GDN_DV = 128
GDN_CONV = 4
RET_HEADS = 4
RET_DK = 128
RET_DV = 256
ROPE_BASE = 10000.0
N_GROUPS = 4
EXPERTS_PER_GROUP = 8
N_EXPERTS = N_GROUPS * EXPERTS_PER_GROUP
D_EXPERT = 512

GDN_QK = GDN_HEADS * GDN_DK
GDN_V = GDN_HEADS * GDN_DV
RET_QK = RET_HEADS * RET_DK
RET_V = RET_HEADS * RET_DV

LANES = 128
CHUNK = 128
INV_BLOCK = 16
GDN_PREP_CHUNKS = 2
VMEM_LIMIT = 56 * 1024 * 1024

REPACK_COLS = 512
MERGE_PARTS = 1
TOP_K = 2
SLOT_TILE = 256
PLAN_TILE = 256
PLAN_STEP = 2048
HALF = D_MODEL // 2
SC_WORKERS = 32
SC_ROWS = 64

PROJ_COLS = 3 * GDN_QK + GDN_V + 2 * RET_QK + 2 * RET_V + 2 * D_MODEL


def _silu(x):
    return x / (1.0 + jnp.exp(-x))


def _sigmoid(x):
    return 1.0 / (1.0 + jnp.exp(-x))


def _dot(a, b):
    return jnp.dot(a, b, preferred_element_type=F32)


def _dot_nt(a, b):
    return lax.dot_general(a, b, (((1,), (1,)), ((), ())), preferred_element_type=F32)


def _pack_bf16_pairs(x):
    bits = lax.bitcast_convert_type(x.astype(BF16).astype(F32), U32)
    packed = (bits[:, :HALF] >> 16) | (bits[:, HALF:] & jnp.uint32(0xFFFF0000))
    return lax.bitcast_convert_type(packed, I32)


def _unpack_bf16_pairs(p):
    p = lax.bitcast_convert_type(p, U32)
    lo = lax.bitcast_convert_type(p << 16, F32)
    hi = lax.bitcast_convert_type(p & jnp.uint32(0xFFFF0000), F32)
    return jnp.concatenate([lo, hi], axis=1)


def _proj_kernel(x_ref, nw_ref, w_ref, wab_ref, proj_ref, ab_ref, u_ref):
    j = pl.program_id(1)

    @pl.when(j == 0)
    def _():
        x = x_ref[...]
        u = x * lax.rsqrt(jnp.mean(x * x, axis=-1, keepdims=True) + EPS) * nw_ref[...]
        ub = u.astype(BF16)
        u_ref[...] = ub
        ab_ref[...] = _dot(ub, wab_ref[...])

    proj_ref[...] = _dot(u_ref[...], w_ref[...]).astype(BF16)


def _repack_kernel(wt_ref, abt_ref, main_ref, ab_ref):
    main_ref[...] = wt_ref[...].T.astype(BF16)

    @pl.when(pl.program_id(0) == 0)
    def _():
        ab = abt_ref[...].T
        ab_ref[...] = jnp.concatenate(
            [ab, jnp.zeros((ab.shape[0], LANES - ab.shape[1]), F32)], axis=1).astype(BF16)


def _repack_call(w_in_t):
    d_in, d = w_in_t.shape
    o_ab = 3 * GDN_QK
    n_ab = 2 * GDN_HEADS
    src = lambda r: pl.multiple_of(r * REPACK_COLS + jnp.where(r * REPACK_COLS >= o_ab, n_ab, 0), 8)
    return pl.pallas_call(
        _repack_kernel,
        grid=(PROJ_COLS // REPACK_COLS,),
        in_specs=[pl.BlockSpec((pl.Element(REPACK_COLS), pl.Element(d)), lambda r: (src(r), 0)),
                  pl.BlockSpec((pl.Element(n_ab), pl.Element(d)), lambda r: (o_ab, 0))],
        out_specs=[pl.BlockSpec((d, REPACK_COLS), lambda r: (0, r)),
                   pl.BlockSpec((d, LANES), lambda r: (0, 0))],
        out_shape=[jax.ShapeDtypeStruct((d, PROJ_COLS), BF16), jax.ShapeDtypeStruct((d, LANES), BF16)],
        compiler_params=pltpu.CompilerParams(dimension_semantics=("arbitrary",)),
        name="repack_w_in",
    )(w_in_t, w_in_t)


def _proj_call(x2, norm_w, w_main, w_ab, tm=1024, tn=3584):
    m = x2.shape[0]
    return pl.pallas_call(
        _proj_kernel,
        grid=(m // tm, PROJ_COLS // tn),
        in_specs=[
            pl.BlockSpec((tm, D_MODEL), lambda i, j: (i, 0)),
            pl.BlockSpec((1, D_MODEL), lambda i, j: (0, 0)),
            pl.BlockSpec((D_MODEL, tn), lambda i, j: (0, j)),
            pl.BlockSpec((D_MODEL, LANES), lambda i, j: (0, 0)),
        ],
        out_specs=[
            pl.BlockSpec((tm, tn), lambda i, j: (i, j)),
            pl.BlockSpec((tm, LANES), lambda i, j: (i, 0)),
        ],
        out_shape=[
            jax.ShapeDtypeStruct((m, PROJ_COLS), BF16),
            jax.ShapeDtypeStruct((m, LANES), F32),
        ],
        scratch_shapes=[pltpu.VMEM((tm, D_MODEL), BF16)],
        compiler_params=pltpu.CompilerParams(
            dimension_semantics=("arbitrary", "arbitrary"), vmem_limit_bytes=VMEM_LIMIT),
        name="proj",
    )(x2, norm_w, w_main, w_ab)


def _unit_lower_inverses(lows, ii, jj, tick=lambda: None):
    eye = jnp.where(ii == jj, 1.0, 0.0).astype(F32)
    in_block = (ii // INV_BLOCK) == (jj // INV_BLOCK)
    ps = [jnp.where(in_block, -low, 0.0) for low in lows]
    ts = [eye + p for p in ps]
    span = 2
    while span < INV_BLOCK:
        ps = [_dot(p, p) for p in ps]
        tick()
        ts = [t + _dot(t, p) for t, p in zip(ts, ps)]
        tick()
        span *= 2
    s = INV_BLOCK
    while s < CHUNK:
        off_diag = ((ii // (2 * s)) == (jj // (2 * s))) & ((ii // s) != (jj // s))
        xs = [_dot(jnp.where(off_diag, low, 0.0), t) for low, t in zip(lows, ts)]
        tick()
        ts = [t - _dot(t, x) for t, x in zip(ts, xs)]
        tick()
        s *= 2
    return ts


def _gdn_kernel(qkv_ref, z_ref, ab_ref, convw_ref, alog_ref, dtb_ref, normw_ref, o_ref,
                b_s, o0_s, m_s, qp_s, gl_s, state_s, q_s, k_s, kb_s, rhs_s, dec_s, cv_s, qg_o, kdt_o,
                *, seq):
    nchunk = seq // CHUNK
    ii = lax.broadcasted_iota(jnp.int32, (CHUNK, CHUNK), 0)
    jj = lax.broadcasted_iota(jnp.int32, (CHUNK, CHUNK), 1)
    causal = ii >= jj
    strict = ii > jj
    tri = jnp.where(causal, 1.0, 0.0).astype(F32)
    neg_a = -jnp.exp(alog_ref[...])
    dtb = dtb_ref[...]

    def conv_cols(c, r0, lo, buf):
        x = qkv_ref[pl.ds(r0, CHUNK), lo:lo + LANES].astype(F32)
        prev0 = pl.multiple_of(jnp.maximum(r0 - 16, 0), 16)
        prev = qkv_ref[pl.ds(prev0, 16), lo:lo + LANES].astype(F32)
        buf[0:8, :] = prev[8:16] * jnp.where(c > 0, 1.0, 0.0)
        buf[8:8 + CHUNK, :] = x
        w = convw_ref[:, lo:lo + LANES]
        y = (w[3:4] * x + w[2:3] * buf[7:7 + CHUNK, :] + w[1:2] * buf[6:6 + CHUNK, :]
             + w[0:1] * buf[5:5 + CHUNK, :])
        return _silu(y)

    def l2n(x):
        return x * lax.rsqrt(jnp.sum(x * x, axis=-1, keepdims=True) + EPS)

    tri_b = tri.astype(BF16)

    def chunk_cumsum(g):
        g1 = g.astype(BF16)
        r1 = g - g1.astype(F32)
        g2 = r1.astype(BF16)
        g3 = (r1 - g2.astype(F32)).astype(BF16)
        return _dot(tri_b, g1) + (_dot(tri_b, g2) + _dot(tri_b, g3))

    def operand_pieces(cc, slot):
        pieces = []
        for sub in range(GDN_PREP_CHUNKS):
            c = cc * GDN_PREP_CHUNKS + sub
            r0 = pl.multiple_of(c * CHUNK, CHUNK)
            gates = {}

            def chunk_gates(c=c, r0=r0, gates=gates):
                ab = ab_ref[pl.ds(r0, CHUNK), :]
                xg = ab + dtb
                softplus = jnp.maximum(xg, 0.0) + jnp.log(1.0 + jnp.exp(-jnp.abs(xg)))
                g_all = neg_a * softplus
                gates["beta"] = _sigmoid(ab)
                gates["gc"] = chunk_cumsum(g_all)
                gates["gc_t"] = gates["gc"].T
                gl_s[c] = jnp.exp(gates["gc"][CHUNK - 1:CHUNK, :])

            def head(h, c=c, r0=r0, sub=sub, gates=gates):
                n = sub * GDN_HEADS + h
                gc_all = gates["gc"]
                bufs = [cv_s.at[slot, 3 * n + i] for i in range(3)]
                q = l2n(conv_cols(c, r0, h * GDN_DK, bufs[0])) * (GDN_DK ** -0.5)
                k = l2n(conv_cols(c, r0, GDN_QK + h * GDN_DK, bufs[1]))
                v = conv_cols(c, r0, 2 * GDN_QK + h * GDN_DV, bufs[2])
                gcol = gc_all[:, h:h + 1]
                grow = gates["gc_t"][h:h + 1, :]
                beta = gates["beta"][:, GDN_HEADS + h:GDN_HEADS + h + 1]
                dec_s[slot, n] = jnp.where(causal, jnp.exp(gcol - grow), 0.0)
                eg = jnp.exp(gcol)
                kb = k * beta
                q_s[slot, n] = q.astype(BF16)
                k_s[slot, n] = k.astype(BF16)
                kb_s[slot, n] = kb.astype(BF16)
                rhs_s[slot, n] = jnp.concatenate([v * beta, kb * eg], axis=1).astype(BF16)
                qg_o[slot, n] = q * eg
                kd = k * jnp.exp(gc_all[CHUNK - 1:CHUNK, h:h + 1] - gcol)
                kdt_o[slot, n] = kd.T.astype(BF16)

            pieces.append(chunk_gates)
            pieces += [functools.partial(head, h) for h in range(GDN_HEADS)]
        return pieces

    def operands(cc, slot):
        for piece in operand_pieces(cc, slot):
            piece()

    def solve(cc, slot, between):
        pending = list(between)

        def tick():
            if pending:
                pending.pop(0)()

        items = [(cc * GDN_PREP_CHUNKS + sub, h, sub * GDN_HEADS + h)
                 for sub in range(GDN_PREP_CHUNKS) for h in range(GDN_HEADS)]
        kks = [_dot_nt(kb_s[slot, n], k_s[slot, n]) for _, _, n in items]
        qks = [_dot_nt(q_s[slot, n], k_s[slot, n]) for _, _, n in items]
        lows = [jnp.where(strict, kk * dec_s[slot, n], 0.0) for kk, (_, _, n) in zip(kks, items)]
        attns = [(qk * dec_s[slot, n]).astype(BF16) for qk, (_, _, n) in zip(qks, items)]
        tick()
        ts = _unit_lower_inverses(lows, ii, jj, tick)
        while pending:
            tick()
        uws = [_dot(t.astype(BF16), rhs_s[slot, n]).astype(BF16) for t, (_, _, n) in zip(ts, items)]
        kds = [_dot(kdt_o[slot, n], uw) for uw, (_, _, n) in zip(uws, items)]
        ats = [_dot(attn, uw) for attn, uw in zip(attns, uws)]
        for kd_uw, at_uw, (c, h, n) in zip(kds, ats, items):
            b_s[c, h] = kd_uw[:, :GDN_DV]
            m_s[c, h] = (-kd_uw[:, GDN_DV:]).astype(BF16)
            o0_s[c, h] = at_uw[:, :GDN_DV]
            qp_s[c, h] = (qg_o[slot, n] - at_uw[:, GDN_DV:]).astype(BF16)

    ngroup = nchunk // GDN_PREP_CHUNKS
    per_trip = 2 * GDN_PREP_CHUNKS
    normw = normw_ref[...]
    state_s[...] = jnp.zeros_like(state_s)
    for c0 in range(per_trip):
        gl_s[c0] = jnp.zeros((1, LANES), F32)
        for h in range(GDN_HEADS):
            b_s[c0, h] = jnp.zeros((CHUNK, GDN_DV), F32)
            o0_s[c0, h] = jnp.zeros((CHUNK, GDN_DV), F32)
            m_s[c0, h] = jnp.zeros((CHUNK, GDN_DK), BF16)
            qp_s[c0, h] = jnp.zeros((CHUNK, GDN_DK), BF16)
    operands(0, 0)

    def prep(i, carry):
        first = jnp.maximum(per_trip * (i - 1), 0)
        steps = [functools.partial(scan, first + k, 0) for k in range(per_trip)]

        def mixed(scans, pieces):
            out = list(pieces)
            for n, s in enumerate(scans):
                out.insert((n + 1) * len(out) // (len(scans) + 1), s)
            return out

        solve(2 * i, 0, mixed(steps[:per_trip // 2], operand_pieces(2 * i + 1, 1)))
        solve(2 * i + 1, 1, mixed(steps[per_trip // 2:],
                                  operand_pieces(jnp.minimum(2 * i + 2, ngroup - 1), 0)))
        return carry

    def scan(c, carry):
        r0 = pl.multiple_of(c * CHUNK, CHUNK)
        gl = gl_s[c]
        heads = range(GDN_HEADS)
        ss = [state_s[h] for h in heads]
        sbs = [s.astype(BF16) for s in ss]
        mss = [_dot(m_s[c, h], sbs[h]) for h in heads]
        qss = [_dot(qp_s[c, h], sbs[h]) for h in heads]
        for h in heads:
            state_s[h] = ss[h] * gl[:, h:h + 1] + (mss[h] + b_s[c, h])
            o = qss[h] + o0_s[c, h]
            z = z_ref[pl.ds(r0, CHUNK), h * GDN_DV:(h + 1) * GDN_DV].astype(F32)
            on = o * lax.rsqrt(jnp.mean(o * o, axis=-1, keepdims=True) + EPS) * normw
            o_ref[pl.ds(r0, CHUNK), h * GDN_DV:(h + 1) * GDN_DV] = (on * _silu(z)).astype(BF16)
        return carry

    ntrip = ngroup // 2
    lax.fori_loop(0, ntrip, prep, 0)
    lax.fori_loop(per_trip * (ntrip - 1), nchunk, scan, 0)


def _gdn_call(proj, ab, conv_w8, alog_p, dtb_p, normw, batch, seq):
    nchunk = seq // CHUNK
    hs = (nchunk, GDN_HEADS, CHUNK, CHUNK)
    ops = (2, GDN_PREP_CHUNKS * GDN_HEADS, CHUNK, CHUNK)
    return pl.pallas_call(
        functools.partial(_gdn_kernel, seq=seq),
        grid=(batch,),
        in_specs=[
            pl.BlockSpec((seq, 3 * GDN_QK), lambda b: (b, 0)),
            pl.BlockSpec((seq, GDN_V), lambda b: (b, 3)),
            pl.BlockSpec((seq, LANES), lambda b: (b, 0)),
            pl.BlockSpec((8, 3 * GDN_QK), lambda b: (0, 0)),
            pl.BlockSpec((1, LANES), lambda b: (0, 0)),
            pl.BlockSpec((1, LANES), lambda b: (0, 0)),
            pl.BlockSpec((1, GDN_DV), lambda b: (0, 0)),
        ],
        out_specs=pl.BlockSpec((seq, GDN_V), lambda b: (b, 0)),
        out_shape=jax.ShapeDtypeStruct((batch * seq, GDN_V), BF16),
        scratch_shapes=[
            pltpu.VMEM(hs, F32), pltpu.VMEM(hs, F32), pltpu.VMEM(hs, BF16), pltpu.VMEM(hs, BF16),
            pltpu.VMEM((nchunk, 1, LANES), F32),
            pltpu.VMEM((GDN_HEADS, GDN_DK, GDN_DV), F32),
            pltpu.VMEM(ops, BF16), pltpu.VMEM(ops, BF16), pltpu.VMEM(ops, BF16),
            pltpu.VMEM(ops[:3] + (2 * CHUNK,), BF16),
            pltpu.VMEM(ops, F32),
            pltpu.VMEM((2, 3 * ops[1], 8 + CHUNK, LANES), F32),
            pltpu.VMEM(ops, F32), pltpu.VMEM(ops, BF16),
        ],
        compiler_params=pltpu.CompilerParams(
            dimension_semantics=("arbitrary",), vmem_limit_bytes=VMEM_LIMIT),
        name="gdn",
    )(proj, proj, ab, conv_w8, alog_p, dtb_p, normw)


def _ret_kernel(qk_ref, v_ref, g_ref, sinq_ref, cosq_ref, sink_ref, cosk_ref, inner_ref, kdec_ref,
                qdec_ref, cdec_ref, o_ref, state_s, q_s, qd_s, k_s, kt_s, *, seq):
    nchunk = seq // CHUNK
    lane = lax.broadcasted_iota(jnp.int32, (CHUNK, RET_DK), 1)
    even = (lane % 2) == 0

    def rotate(x, sin, cos):
        nxt = pltpu.roll(x, RET_DK - 1, axis=1)
        prv = pltpu.roll(x, 1, axis=1)
        return x * cos + jnp.where(even, nxt, prv) * sin

    state_s[...] = jnp.zeros_like(state_s)
    kdec = kdec_ref[...]
    qdec = qdec_ref[...]
    cdec = cdec_ref[...]

    heads = range(RET_HEADS)

    def operands(c, slot):
        r0 = pl.multiple_of(c * CHUNK, CHUNK)
        sinq = sinq_ref[pl.ds(r0, CHUNK), :]
        cosq = cosq_ref[pl.ds(r0, CHUNK), :]
        sink = sink_ref[pl.ds(r0, CHUNK), :]
        cosk = cosk_ref[pl.ds(r0, CHUNK), :]
        for h in heads:
            q = rotate(qk_ref[pl.ds(r0, CHUNK), h * RET_DK:(h + 1) * RET_DK].astype(F32), sinq, cosq)
            k = rotate(qk_ref[pl.ds(r0, CHUNK), RET_QK + h * RET_DK:RET_QK + (h + 1) * RET_DK]
                       .astype(F32), sink, cosk)
            q_s[slot, h] = q.astype(BF16)
            qd_s[slot, h] = (q * qdec[:, h:h + 1]).astype(BF16)
            k_s[slot, h] = k.astype(BF16)
            kt_s[slot, h] = (k * kdec[:, h:h + 1]).T.astype(BF16)

    def outputs(c, slot):
        r0 = pl.multiple_of(c * CHUNK, CHUNK)
        vs = [v_ref[pl.ds(r0, CHUNK), h * RET_DV:(h + 1) * RET_DV] for h in heads]
        ss = [state_s[h] for h in heads]
        qks = [_dot_nt(q_s[slot, h], k_s[slot, h]) for h in heads]
        inters = [_dot(qd_s[slot, h], ss[h].astype(BF16)) for h in heads]
        kvs = [_dot(kt_s[slot, h], vs[h]) for h in heads]
        intras = [_dot((qks[h] * inner_ref[h]).astype(BF16), vs[h]) for h in heads]
        for h in heads:
            state_s[h] = ss[h] * cdec[:, h:h + 1] + kvs[h]
            o = intras[h] + inters[h]
            gate = g_ref[pl.ds(r0, CHUNK), h * RET_DV:(h + 1) * RET_DV].astype(F32)
            on = o * lax.rsqrt(jnp.mean(o * o, axis=-1, keepdims=True) + EPS)
            o_ref[pl.ds(r0, CHUNK), h * RET_DV:(h + 1) * RET_DV] = (on * _silu(gate)).astype(BF16)

    operands(0, 0)

    def body(i, carry):
        operands(2 * i + 1, 1)
        outputs(2 * i, 0)
        operands(jnp.minimum(2 * i + 2, nchunk - 1), 0)
        outputs(2 * i + 1, 1)
        return carry

    lax.fori_loop(0, nchunk // 2, body, 0)


def _ret_call(proj, tables, batch, seq):
    table = pl.BlockSpec((seq, RET_DK), lambda b: (0, 0))
    return pl.pallas_call(
        functools.partial(_ret_kernel, seq=seq),
        grid=(batch,),
        in_specs=[
            pl.BlockSpec((seq, 2 * RET_QK), lambda b: (b, 2)),
            pl.BlockSpec((seq, RET_V), lambda b: (b, 3)),
            pl.BlockSpec((seq, RET_V), lambda b: (b, 4)),
            table, table, table, table,
            pl.BlockSpec((RET_HEADS, CHUNK, CHUNK), lambda b: (0, 0, 0)),
            pl.BlockSpec((CHUNK, LANES), lambda b: (0, 0)),
            pl.BlockSpec((CHUNK, LANES), lambda b: (0, 0)),
            pl.BlockSpec((1, LANES), lambda b: (0, 0)),
        ],
        out_specs=pl.BlockSpec((seq, RET_V), lambda b: (b, 0)),
        out_shape=jax.ShapeDtypeStruct((batch * seq, RET_V), BF16),
        scratch_shapes=[pltpu.VMEM((RET_HEADS, RET_DK, RET_DV), F32)]
        + [pltpu.VMEM((2, RET_HEADS, CHUNK, RET_DK), BF16)] * 4,
        compiler_params=pltpu.CompilerParams(
            dimension_semantics=("arbitrary",), vmem_limit_bytes=VMEM_LIMIT),
        name="retention",
    )(proj, proj, proj, *tables)


def _retention_tables(seq):
    inv_freq = 1.0 / (ROPE_BASE ** jnp.linspace(0.0, 1.0, RET_DK // 2, dtype=F32))
    ang = jnp.arange(seq, dtype=F32)[:, None] * inv_freq[None, :]
    sin = jnp.repeat(jnp.sin(ang), 2, axis=-1)
    cos = jnp.repeat(jnp.cos(ang), 2, axis=-1)
    sin = jnp.where(jnp.arange(RET_DK) % 2 == 0, -sin, sin)
    k_scale = RET_DK ** -0.5
    log_gamma = jnp.log(1.0 - 2.0 ** (-5.0 - jnp.arange(RET_HEADS, dtype=F32)))
    idx = jnp.arange(CHUNK, dtype=F32)
    causal = jnp.tril(jnp.ones((CHUNK, CHUNK), dtype=bool))
    rel = jnp.where(causal, idx[:, None] - idx[None, :], 0.0)
    inner = jnp.where(causal, jnp.exp(rel[None] * log_gamma[:, None, None]), 0.0)
    k_decay = jnp.exp(log_gamma[:, None] * (CHUNK - 1.0 - idx)[None, :])
    q_decay = jnp.exp(log_gamma[:, None] * (idx + 1.0)[None, :])
    chunk_decay = jnp.exp(log_gamma * CHUNK)
    pad = LANES - RET_HEADS
    kdec = jnp.pad(k_decay.T, ((0, 0), (0, pad)))
    qdec = jnp.pad(q_decay.T, ((0, 0), (0, pad)))
    cdec = jnp.pad(chunk_decay[None, :], ((0, 0), (0, pad)))
    return sin, cos, sin * k_scale, cos * k_scale, inner, kdec, qdec, cdec


def _merge_kernel(x_ref, ya_ref, yb_ref, ma_ref, mb_ref, wa_ref, wr_ref, wo_ref, nw_ref,
                  wrt_ref, br_ref, h_ref, xn_ref, ridx_ref, rw_ref, cnt_ref):
    tm = x_ref.shape[0] // MERGE_PARTS
    rows = [pl.ds(i * tm, tm) for i in range(MERGE_PARTS)]
    a_ = [_dot(ya_ref[r, :], wa_ref[...]) for r in rows]
    r_ = [_dot(yb_ref[r, :], wr_ref[...]) for r in rows]
    merged = [(_sigmoid(ma_ref[r, :].astype(F32)) * a + _sigmoid(mb_ref[r, :].astype(F32)) * rr)
              .astype(BF16) for r, a, rr in zip(rows, a_, r_)]
    hs = [x_ref[r, :] + _dot(mg, wo_ref[...]) for r, mg in zip(rows, merged)]
    xcats = []
    for r, h in zip(rows, hs):
        h_ref[r, :] = h
        xn = h * lax.rsqrt(jnp.mean(h * h, axis=-1, keepdims=True) + EPS) * nw_ref[...]
        xn_ref[r, :] = _pack_bf16_pairs(xn)
        xh = xn.astype(BF16)
        xl = (xn - xh.astype(F32)).astype(BF16)
        xcats.append(jnp.concatenate([xh, xl], axis=0))
    parts_ = [_dot(xc, wrt_ref[...]) for xc in xcats]
    counts = jnp.zeros((1, LANES), F32)
    for r, parts in zip(rows, parts_):
        counts = counts + _route(parts, tm, br_ref[...], ridx_ref.at[r, :], rw_ref.at[r, :])

    @pl.when(pl.program_id(0) == 0)
    def _():
        cnt_ref[...] = jnp.zeros_like(cnt_ref)

    cnt_ref[...] += jnp.broadcast_to(counts, cnt_ref.shape)


def _route(parts, tm, bias, ridx_ref, rw_ref):
    logits = (parts[:tm, :LANES] + (parts[tm:, :LANES] + parts[:tm, LANES:]
                                    + parts[tm:, LANES:])) + bias
    lane = lax.broadcasted_iota(jnp.int32, (tm, LANES), 1)
    neg = -jnp.inf
    gl = jnp.where((lane >= N_EXPERTS) & (lane < N_EXPERTS + N_GROUPS), logits, neg)
    gmax = jnp.max(gl, axis=-1, keepdims=True)
    gidx = jnp.min(jnp.where(gl == gmax, lane, LANES), axis=-1, keepdims=True) - N_EXPERTS
    g_w = 1.0 / jnp.sum(jnp.exp(gl - gmax), axis=-1, keepdims=True)
    el = jnp.where((lane // EXPERTS_PER_GROUP == gidx) & (lane < N_EXPERTS), logits, neg)
    m1 = jnp.max(el, axis=-1, keepdims=True)
    i1 = jnp.min(jnp.where(el == m1, lane, LANES), axis=-1, keepdims=True)
    el2 = jnp.where(lane == i1, neg, el)
    m2 = jnp.max(el2, axis=-1, keepdims=True)
    i2 = jnp.min(jnp.where(el2 == m2, lane, LANES), axis=-1, keepdims=True)
    e2 = jnp.exp(m2 - m1)
    p1 = g_w / (1.0 + e2)
    p2 = g_w * e2 / (1.0 + e2)
    ridx_ref[...] = jnp.where(lane == 0, i1, jnp.where(lane == 1, i2, 0))
    rw_ref[...] = jnp.where(lane == 0, p1, jnp.where(lane == 1, p2, 0.0))
    onehot = jnp.where((lane == i1) | (lane == i2), 1.0, 0.0)
    return jnp.sum(onehot, axis=0, keepdims=True)


def _merge_call(x2, ya, yb, proj, wa, wr, wo, nw, w_router, b_router, tm=1024):
    m = x2.shape[0]
    full = lambda shape: pl.BlockSpec(shape, lambda i: (0, 0))
    return pl.pallas_call(
        _merge_kernel,
        grid=(m // tm,),
        in_specs=[
            pl.BlockSpec((tm, D_MODEL), lambda i: (i, 0)),
            pl.BlockSpec((tm, GDN_V), lambda i: (i, 0)),
            pl.BlockSpec((tm, RET_V), lambda i: (i, 0)),
            pl.BlockSpec((tm, D_MODEL), lambda i: (i, 5)),
            pl.BlockSpec((tm, D_MODEL), lambda i: (i, 6)),
            full((GDN_V, D_MODEL)), full((RET_V, D_MODEL)), full((D_MODEL, D_MODEL)),
            full((1, D_MODEL)),
            full((D_MODEL, 2 * LANES)), full((1, LANES)),
        ],
        out_specs=[
            pl.BlockSpec((tm, D_MODEL), lambda i: (i, 0)),
            pl.BlockSpec((tm, HALF), lambda i: (i, 0)),
            pl.BlockSpec((tm, LANES), lambda i: (i, 0)),
            pl.BlockSpec((tm, LANES), lambda i: (i, 0)),
            pl.BlockSpec((8, LANES), lambda i: (0, 0)),
        ],
        out_shape=[
            jax.ShapeDtypeStruct((m, D_MODEL), F32),
            jax.ShapeDtypeStruct((m, HALF), I32),
            jax.ShapeDtypeStruct((m, LANES), I32),
            jax.ShapeDtypeStruct((m, LANES), F32),
            jax.ShapeDtypeStruct((8, LANES), F32),
        ],
        compiler_params=pltpu.CompilerParams(
            dimension_semantics=("arbitrary",), vmem_limit_bytes=VMEM_LIMIT),
        name="merge_router",
    )(x2, ya, yb, proj, proj, wa, wr, wo, nw, w_router, b_router)


def _slot_counts(m):
    n_slots = TOP_K * m + N_EXPERTS * SLOT_TILE
    return n_slots, n_slots // SLOT_TILE


def _lane_prefix_sum(x, lane):
    s = 1
    while s < LANES:
        x = x + jnp.where(lane >= s, pltpu.roll(x, s, axis=1), 0.0)
        s *= 2
    return x


def _plan_kernel(ridx_ref, cnt_ref, slots_ref, tile_ref, carry_s, off_s):
    i = pl.program_id(0)
    lane = lax.broadcasted_iota(I32, (PLAN_TILE, LANES), 1)
    row = lax.broadcasted_iota(I32, (PLAN_TILE, LANES), 0)
    lane1 = lane[0:1]

    @pl.when(i == 0)
    def _():
        cnt = cnt_ref[0:1, :]
        tile = float(SLOT_TILE)
        padded = jnp.floor((cnt + (tile - 1.0)) / tile) * tile
        incl = _lane_prefix_sum(padded, lane1)
        off = incl - padded
        off_s[...] = off
        carry_s[...] = jnp.zeros_like(carry_s)
        first = (row * SLOT_TILE).astype(F32)
        ended = jnp.where((lane < N_EXPERTS) & (incl <= first), 1.0, 0.0)
        tile_e = jnp.sum(ended, axis=-1, keepdims=True)
        last = jnp.sum(jnp.where(lane.astype(F32) == tile_e, off + cnt, 0.0), axis=-1, keepdims=True)
        used = jnp.clip(last - first[:, 0:1], 0.0, tile)
        tile_ref[...] = jnp.where(lane == 0, tile_e, jnp.where(lane == 1, used, 0.0)).astype(I32)

    strict = jnp.where(row[:, 0:1] > lax.broadcasted_iota(I32, (PLAN_TILE, PLAN_TILE), 1),
                       1.0, 0.0).astype(BF16)
    off = off_s[...]
    carry = carry_s[...]
    for sb in range(PLAN_STEP // PLAN_TILE):
        rows = pl.ds(sb * PLAN_TILE, PLAN_TILE)
        e1 = ridx_ref[rows, 0:1]
        e2 = ridx_ref[rows, 1:2]
        onehot = jnp.where((lane == e1) | (lane == e2), 1.0, 0.0)
        pos = _dot(strict, onehot.astype(BF16)) + (carry + off)
        s1 = jnp.sum(jnp.where(lane == e1, pos, 0.0), axis=-1, keepdims=True)
        s2 = jnp.sum(jnp.where(lane == e2, pos, 0.0), axis=-1, keepdims=True)
        both = jnp.where(lane == 0, s1, jnp.where(lane == 1, s2, 0.0))
        for q in range(PLAN_TILE // LANES):
            t = both[q * LANES:(q + 1) * LANES].T
            c0 = sb * PLAN_TILE + q * LANES
            slots_ref[:, c0:c0 + LANES] = t[0:8].astype(I32)
        carry = carry + jnp.sum(onehot, axis=0, keepdims=True)
    carry_s[...] = carry


def _plan_call(ridx, cnt):
    m = ridx.shape[0]
    _, n_tiles = _slot_counts(m)
    assert n_tiles <= PLAN_TILE and SLOT_TILE == PLAN_TILE
    return pl.pallas_call(
        _plan_kernel,
        grid=(m // PLAN_STEP,),
        in_specs=[pl.BlockSpec((PLAN_STEP, LANES), lambda i: (i, 0)),
                  pl.BlockSpec((8, LANES), lambda i: (0, 0))],
        out_specs=[
            pl.BlockSpec((8, PLAN_STEP), lambda i: (0, i)),
            pl.BlockSpec((PLAN_TILE, LANES), lambda i: (0, 0)),
        ],
        out_shape=[
            jax.ShapeDtypeStruct((8, m), I32),
            jax.ShapeDtypeStruct((PLAN_TILE, LANES), I32),
        ],
        scratch_shapes=[pltpu.VMEM((1, LANES), F32), pltpu.VMEM((1, LANES), F32)],
        compiler_params=pltpu.CompilerParams(dimension_semantics=("arbitrary",)),
        name="dispatch_plan",
    )(ridx, cnt)


def _sc_mesh():
    return plsc.VectorSubcoreMesh(core_axis_name="c", subcore_axis_name="s")


def _sc_worker():
    return lax.axis_index("s") * 2 + lax.axis_index("c")


def _sc_dispatch(xn, slot1, slot2, n_rows):
    m = xn.shape[0]
    per = m // SC_WORKERS
    n_pairs = per // (2 * SC_ROWS)

    @functools.partial(
        pl.kernel, mesh=_sc_mesh(),
        out_type=jax.ShapeDtypeStruct((n_rows, HALF), I32),
        scratch_types=[pltpu.VMEM((per // SC_ROWS, SC_ROWS), I32), pltpu.VMEM((per // SC_ROWS, SC_ROWS), I32),
                       pltpu.VMEM((SC_ROWS, HALF), I32), pltpu.VMEM((SC_ROWS, HALF), I32),
                       pltpu.SemaphoreType.DMA, pltpu.SemaphoreType.DMA, pltpu.SemaphoreType.DMA],
        name="sc_dispatch")
    def k(x_hbm, s1_hbm, s2_hbm, o_hbm, i1_v, i2_v, rows0, rows1, sem_r0, sem_r1, sem_w):
        wid = _sc_worker()
        base = wid * per

        def read(chunk, rows_v, sem):
            return pltpu.make_async_copy(x_hbm.at[pl.ds(base + chunk * SC_ROWS, SC_ROWS)], rows_v, sem)

        def scatter(chunk, rows_v):
            c1 = pltpu.async_copy(rows_v, o_hbm.at[i1_v.at[chunk]], sem_w)
            c2 = pltpu.async_copy(rows_v, o_hbm.at[i2_v.at[chunk]], sem_w)
            c1.wait()
            c2.wait()

        read(0, rows0, sem_r0).start()
        pltpu.sync_copy(s1_hbm.at[pl.ds(wid * (per // SC_ROWS), per // SC_ROWS)], i1_v)
        pltpu.sync_copy(s2_hbm.at[pl.ds(wid * (per // SC_ROWS), per // SC_ROWS)], i2_v)

        @pl.loop(0, n_pairs)
        def _(i):
            read(2 * i, rows0, sem_r0).wait()
            read(2 * i + 1, rows1, sem_r1).start()
            scatter(2 * i, rows0)
            read(2 * i + 1, rows1, sem_r1).wait()

            @pl.when(i + 1 < n_pairs)
            def _():
                read(2 * i + 2, rows0, sem_r0).start()

            scatter(2 * i + 1, rows1)

    return k(xn, slot1, slot2)


def _sc_collect(ys, slot1, slot2):
    m = slot1.size
    per = m // SC_WORKERS
    row = jax.ShapeDtypeStruct((m, HALF), I32)

    @functools.partial(
        pl.kernel, mesh=_sc_mesh(), out_type=[row, row],
        scratch_types=[pltpu.VMEM((per // SC_ROWS, SC_ROWS), I32), pltpu.VMEM((per // SC_ROWS, SC_ROWS), I32),
                       pltpu.VMEM((SC_ROWS, HALF), I32), pltpu.VMEM((SC_ROWS, HALF), I32),
                       pltpu.SemaphoreType.DMA, pltpu.SemaphoreType.DMA],
        name="sc_collect")
    def k(y_hbm, s1_hbm, s2_hbm, g1_hbm, g2_hbm, i1_v, i2_v, rows1, rows2, sem_g, sem_w):
        wid = _sc_worker()
        base = wid * per
        pltpu.sync_copy(s1_hbm.at[pl.ds(wid * (per // SC_ROWS), per // SC_ROWS)], i1_v)
        pltpu.sync_copy(s2_hbm.at[pl.ds(wid * (per // SC_ROWS), per // SC_ROWS)], i2_v)

        @pl.loop(0, per // SC_ROWS)
        def _(ci):
            t0 = base + ci * SC_ROWS
            a1 = pltpu.async_copy(y_hbm.at[i1_v.at[ci]], rows1, sem_g)
            a2 = pltpu.async_copy(y_hbm.at[i2_v.at[ci]], rows2, sem_g)
            a1.wait()
            a2.wait()
            w1 = pltpu.async_copy(rows1, g1_hbm.at[pl.ds(t0, SC_ROWS)], sem_w)
            w2 = pltpu.async_copy(rows2, g2_hbm.at[pl.ds(t0, SC_ROWS)], sem_w)
            w1.wait()
            w2.wait()

    return k(ys, slot1, slot2)


def _expert_kernel(te_ref, used_ref, xs_ref, wg_hbm, wu_hbm, wd_hbm, ys_ref, wg_b, wu_b, wd_b,
                   wg_f, wu_f, wd_f, sem, slot_s):
    j = pl.program_id(0)
    n = pl.num_programs(0)
    e = te_ref[j]
    prev = te_ref[jnp.maximum(j - 1, 0)]
    valid = e < N_EXPERTS

    def weight_copies(expert, slot):
        return [pltpu.make_async_copy(hbm.at[expert], buf.at[slot], sem.at[slot, i])
                for i, (hbm, buf) in enumerate(((wg_hbm, wg_f), (wu_hbm, wu_f), (wd_hbm, wd_f)))]

    @pl.when((j == 0) & valid)
    def _():
        slot_s[0] = 0
        for c in weight_copies(e, 0):
            c.start()

    @pl.when(((j == 0) | (e != prev)) & valid)
    def _():
        slot = slot_s[0]
        for c in weight_copies(e, slot):
            c.wait()
        k = lax.while_loop(lambda k: (k < n) & (te_ref[jnp.minimum(k, n - 1)] == e),
                           lambda k: k + 1, j + 1)
        nxt = te_ref[jnp.minimum(k, n - 1)]

        @pl.when((k < n) & (nxt < N_EXPERTS))
        def _():
            for c in weight_copies(nxt, 1 - slot):
                c.start()

        wg_b[...] = wg_f[slot].astype(BF16)
        wu_b[...] = wu_f[slot].astype(BF16)
        wd_b[...] = wd_f[slot].astype(BF16)
        slot_s[0] = 1 - slot

    @pl.when(valid)
    def _():
        half = SLOT_TILE // 2
        rows = [pl.ds(i * half, half) for i in range(2)]
        row_id = lax.broadcasted_iota(I32, (half, HALF), 0)
        xs = [_unpack_bf16_pairs(jnp.where(row_id + i * half < used_ref[j], xs_ref[r, :], 0))
              .astype(BF16) for i, r in enumerate(rows)]
        gs = [_dot(x, wg_b[...]) for x in xs]
        us = [_dot(x, wu_b[...]) for x in xs]
        hids = [(_silu(g) * u).astype(BF16) for g, u in zip(gs, us)]
        ys = [_dot(hid, wd_b[...]) for hid in hids]
        for r, y in zip(rows, ys):
            ys_ref[r, :] = _pack_bf16_pairs(y)

    @pl.when(e >= N_EXPERTS)
    def _():
        ys_ref[...] = jnp.zeros_like(ys_ref)


def _expert_call(tile_expert, tile_used, xs, wg, wu, wd, n_tiles):
    hbm = pl.BlockSpec(memory_space=pl.ANY)
    return pl.pallas_call(
        _expert_kernel,
        grid_spec=pltpu.PrefetchScalarGridSpec(
            num_scalar_prefetch=2,
            grid=(n_tiles,),
            in_specs=[pl.BlockSpec((SLOT_TILE, HALF), lambda j, te, used: (j, 0)), hbm, hbm, hbm],
            out_specs=pl.BlockSpec((SLOT_TILE, HALF), lambda j, te, used: (j, 0)),
            scratch_shapes=[
                pltpu.VMEM((D_MODEL, D_EXPERT), BF16), pltpu.VMEM((D_MODEL, D_EXPERT), BF16),
                pltpu.VMEM((D_EXPERT, D_MODEL), BF16),
                pltpu.VMEM((2, D_MODEL, D_EXPERT), F32), pltpu.VMEM((2, D_MODEL, D_EXPERT), F32),
                pltpu.VMEM((2, D_EXPERT, D_MODEL), F32),
                pltpu.SemaphoreType.DMA((2, 3)),
                pltpu.SMEM((1,), I32),
            ],
        ),
        out_shape=jax.ShapeDtypeStruct((n_tiles * SLOT_TILE, HALF), I32),
        compiler_params=pltpu.CompilerParams(
            dimension_semantics=("arbitrary",), vmem_limit_bytes=VMEM_LIMIT),
        name="experts",
    )(tile_expert, tile_used, xs, wg, wu, wd)


def _final_kernel(h_ref, g1_ref, g2_ref, rw_ref, nw_ref, o_ref):
    rw = rw_ref[...]
    y = rw[:, 0:1] * _unpack_bf16_pairs(g1_ref[...]) + rw[:, 1:2] * _unpack_bf16_pairs(g2_ref[...])
    h = h_ref[...] + y
    o_ref[...] = h * lax.rsqrt(jnp.mean(h * h, axis=-1, keepdims=True) + EPS) * nw_ref[...]


def _final_call(h1, g1, g2, rw, nw, tm=1024):
    m = h1.shape[0]
    return pl.pallas_call(
        _final_kernel,
        grid=(m // tm,),
        in_specs=[
            pl.BlockSpec((tm, D_MODEL), lambda i: (i, 0)),
            pl.BlockSpec((tm, HALF), lambda i: (i, 0)),
            pl.BlockSpec((tm, HALF), lambda i: (i, 0)),
            pl.BlockSpec((tm, LANES), lambda i: (i, 0)),
            pl.BlockSpec((1, D_MODEL), lambda i: (0, 0)),
        ],
        out_specs=pl.BlockSpec((tm, D_MODEL), lambda i: (i, 0)),
        out_shape=jax.ShapeDtypeStruct((m, D_MODEL), F32),
        compiler_params=pltpu.CompilerParams(dimension_semantics=("arbitrary",)),
        name="combine_final",
    )(h1, g1, g2, rw, nw)


def _pad_lanes(a):
    return jnp.pad(a, ((0, 0), (0, LANES - a.shape[1])))


def kernel(x, norm_mix_w, w_in, conv_w, A_log, dt_bias, gdn_norm_w, w_up_gdn, w_up_ret, w_out,
           norm_ffn_w, w_group, b_group, w_expert, b_expert, w_gate, w_up, w_down, norm_final_w):
    batch, seq, d = x.shape
    m = batch * seq
    h = x.reshape(m, d)
    depth = w_in.shape[0]
    ret_tables = _retention_tables(seq)
    for l in range(depth):
        w_main, w_ab = _repack_call(jnp.transpose(w_in[l]))
        proj, ab = _proj_call(h, norm_mix_w[l][None, :], w_main, w_ab)

        conv8 = jnp.pad(conv_w[l], ((0, 8 - GDN_CONV), (0, 0)))
        ya = _gdn_call(proj, ab, conv8, _pad_lanes(A_log[l][None, :]), _pad_lanes(dt_bias[l][None, :]),
                       gdn_norm_w[l][None, :], batch, seq)
        yb = _ret_call(proj, ret_tables, batch, seq)

        w_router = _pad_lanes(jnp.concatenate([w_expert[l], w_group[l]], axis=1))
        wr_hi = w_router.astype(BF16)
        wr_lo = (w_router - wr_hi.astype(F32)).astype(BF16)
        b_router = _pad_lanes(jnp.concatenate([b_expert[l], b_group[l]])[None, :])
        h1, xn, ridx, rw, cnt = _merge_call(
            h, ya, yb, proj, w_up_gdn[l].astype(BF16), w_up_ret[l].astype(BF16), w_out[l].astype(BF16),
            norm_ffn_w[l][None, :], jnp.concatenate([wr_hi, wr_lo], axis=1), b_router)

        n_slots, n_tiles = _slot_counts(m)
        slots, tiles = _plan_call(ridx, cnt)
        slot1, slot2 = slots[0].reshape(-1, SC_ROWS), slots[1].reshape(-1, SC_ROWS)
        xs = _sc_dispatch(xn, slot1, slot2, n_slots)
        ys = _expert_call(tiles[:n_tiles, 0], tiles[:n_tiles, 1], xs,
                          w_gate[l], w_up[l], w_down[l], n_tiles)
        g1, g2 = _sc_collect(ys, slot1, slot2)

        assert depth == 1
        h = _final_call(h1, g1, g2, rw, norm_final_w[None, :])
    return h.reshape(batch, seq, d)
```

```python
import functools
import math

import jax
import jax.numpy as jnp
from jax import lax
from jax.experimental import pallas as pl
from jax.experimental.pallas import tpu as pltpu
from jax.experimental.pallas import tpu_sc as plsc

F32 = jnp.float32
BF16 = jnp.bfloat16
I32 = jnp.int32
U32 = jnp.uint32

D_MODEL = 1024
EPS = 1e-6
GDN_HEADS = 4
GDN_DK = 128
GDN_DV = 128
GDN_CONV = 4
RET_HEADS = 4
RET_DK = 128
RET_DV = 256
ROPE_BASE = 10000.0
N_GROUPS = 4
EXPERTS_PER_GROUP = 8
N_EXPERTS = N_GROUPS * EXPERTS_PER_GROUP
D_EXPERT = 512

GDN_QK = GDN_HEADS * GDN_DK
GDN_V = GDN_HEADS * GDN_DV
RET_QK = RET_HEADS * RET_DK
RET_V = RET_HEADS * RET_DV

LANES = 128
CHUNK = 128
INV_BLOCK = 16
GDN_PREP_CHUNKS = 2
VMEM_LIMIT = 56 * 1024 * 1024

REPACK_COLS = 512
MERGE_PARTS = 1
TOP_K = 2
SLOT_TILE = 256
PLAN_TILE = 256
PLAN_STEP = 1024
HALF = D_MODEL // 2
SC_WORKERS = 32
SC_ROWS = 64

PROJ_COLS = 3 * GDN_QK + GDN_V + 2 * RET_QK + 2 * RET_V + 2 * D_MODEL


def _silu(x):
    return x / (1.0 + jnp.exp(-x))


def _sigmoid(x):
    return 1.0 / (1.0 + jnp.exp(-x))


def _dot(a, b):
    return jnp.dot(a, b, preferred_element_type=F32)


def _dot_nt(a, b):
    return lax.dot_general(a, b, (((1,), (1,)), ((), ())), preferred_element_type=F32)


def _pack_bf16_pairs(x):
    bits = lax.bitcast_convert_type(x.astype(BF16).astype(F32), U32)
    packed = (bits[:, :HALF] >> 16) | (bits[:, HALF:] & jnp.uint32(0xFFFF0000))
    return lax.bitcast_convert_type(packed, I32)


def _unpack_bf16_pairs(p):
    p = lax.bitcast_convert_type(p, U32)
    lo = lax.bitcast_convert_type(p << 16, F32)
    hi = lax.bitcast_convert_type(p & jnp.uint32(0xFFFF0000), F32)
    return jnp.concatenate([lo, hi], axis=1)


def _proj_kernel(x_ref, nw_ref, w_ref, wab_ref, proj_ref, ab_ref, u_ref):
    j = pl.program_id(1)

    @pl.when(j == 0)
    def _():
        x = x_ref[...]
        u = x * lax.rsqrt(jnp.mean(x * x, axis=-1, keepdims=True) + EPS) * nw_ref[...]
        ub = u.astype(BF16)
        u_ref[...] = ub
        ab_ref[...] = _dot(ub, wab_ref[...])

    proj_ref[...] = _dot(u_ref[...], w_ref[...]).astype(BF16)


def _repack_kernel(wt_ref, abt_ref, main_ref, ab_ref):
    main_ref[...] = wt_ref[...].T.astype(BF16)

    @pl.when(pl.program_id(0) == 0)
    def _():
        ab = abt_ref[...].T
        ab_ref[...] = jnp.concatenate(
            [ab, jnp.zeros((ab.shape[0], LANES - ab.shape[1]), F32)], axis=1).astype(BF16)


def _repack_call(w_in_t):
    d_in, d = w_in_t.shape
    o_ab = 3 * GDN_QK
    n_ab = 2 * GDN_HEADS
    src = lambda r: pl.multiple_of(r * REPACK_COLS + jnp.where(r * REPACK_COLS >= o_ab, n_ab, 0), 8)
    return pl.pallas_call(
        _repack_kernel,
        grid=(PROJ_COLS // REPACK_COLS,),
        in_specs=[pl.BlockSpec((pl.Element(REPACK_COLS), pl.Element(d)), lambda r: (src(r), 0)),
                  pl.BlockSpec((pl.Element(n_ab), pl.Element(d)), lambda r: (o_ab, 0))],
        out_specs=[pl.BlockSpec((d, REPACK_COLS), lambda r: (0, r)),
                   pl.BlockSpec((d, LANES), lambda r: (0, 0))],
        out_shape=[jax.ShapeDtypeStruct((d, PROJ_COLS), BF16), jax.ShapeDtypeStruct((d, LANES), BF16)],
        compiler_params=pltpu.CompilerParams(dimension_semantics=("arbitrary",)),
        name="repack_w_in",
    )(w_in_t, w_in_t)


def _proj_call(x2, norm_w, w_main, w_ab, tm=1024, tn=3584):
    m = x2.shape[0]
    return pl.pallas_call(
        _proj_kernel,
        grid=(m // tm, PROJ_COLS // tn),
        in_specs=[
            pl.BlockSpec((tm, D_MODEL), lambda i, j: (i, 0)),
            pl.BlockSpec((1, D_MODEL), lambda i, j: (0, 0)),
            pl.BlockSpec((D_MODEL, tn), lambda i, j: (0, j)),
            pl.BlockSpec((D_MODEL, LANES), lambda i, j: (0, 0)),
        ],
        out_specs=[
            pl.BlockSpec((tm, tn), lambda i, j: (i, j)),
            pl.BlockSpec((tm, LANES), lambda i, j: (i, 0)),
        ],
        out_shape=[
            jax.ShapeDtypeStruct((m, PROJ_COLS), BF16),
            jax.ShapeDtypeStruct((m, LANES), F32),
        ],
        scratch_shapes=[pltpu.VMEM((tm, D_MODEL), BF16)],
        compiler_params=pltpu.CompilerParams(
            dimension_semantics=("arbitrary", "arbitrary"), vmem_limit_bytes=VMEM_LIMIT),
        name="proj",
    )(x2, norm_w, w_main, w_ab)


def _unit_lower_inverses(lows, ii, jj):
    eye = jnp.where(ii == jj, 1.0, 0.0).astype(F32)
    in_block = (ii // INV_BLOCK) == (jj // INV_BLOCK)
    ps = [jnp.where(in_block, -low, 0.0) for low in lows]
    ts = [eye + p for p in ps]
    span = 2
    while span < INV_BLOCK:
        ps = [_dot(p, p) for p in ps]
        ts = [t + _dot(t, p) for t, p in zip(ts, ps)]
        span *= 2
    s = INV_BLOCK
    while s < CHUNK:
        off_diag = ((ii // (2 * s)) == (jj // (2 * s))) & ((ii // s) != (jj // s))
        xs = [_dot(jnp.where(off_diag, low, 0.0), t) for low, t in zip(lows, ts)]
        ts = [t - _dot(t, x) for t, x in zip(ts, xs)]
        s *= 2
    return ts


def _gdn_kernel(qkv_ref, z_ref, ab_ref, convw_ref, alog_ref, dtb_ref, normw_ref, o_ref,
                b_s, o0_s, m_s, qp_s, gl_s, state_s, q_s, k_s, kb_s, rhs_s, dec_s, cv_s, qg_o, kdt_o,
                *, seq):
    nchunk = seq // CHUNK
    ii = lax.broadcasted_iota(jnp.int32, (CHUNK, CHUNK), 0)
    jj = lax.broadcasted_iota(jnp.int32, (CHUNK, CHUNK), 1)
    causal = ii >= jj
    strict = ii > jj
    tri = jnp.where(causal, 1.0, 0.0).astype(F32)
    neg_a = -jnp.exp(alog_ref[...])
    dtb = dtb_ref[...]

    def conv_cols(c, r0, lo, buf):
        x = qkv_ref[pl.ds(r0, CHUNK), lo:lo + LANES].astype(F32)
        prev0 = pl.multiple_of(jnp.maximum(r0 - 16, 0), 16)
        prev = qkv_ref[pl.ds(prev0, 16), lo:lo + LANES].astype(F32)
        buf[0:8, :] = prev[8:16] * jnp.where(c > 0, 1.0, 0.0)
        buf[8:8 + CHUNK, :] = x
        w = convw_ref[:, lo:lo + LANES]
        y = (w[3:4] * x + w[2:3] * buf[7:7 + CHUNK, :] + w[1:2] * buf[6:6 + CHUNK, :]
             + w[0:1] * buf[5:5 + CHUNK, :])
        return _silu(y)

    def l2n(x):
        return x * lax.rsqrt(jnp.sum(x * x, axis=-1, keepdims=True) + EPS)

    tri_b = tri.astype(BF16)

    def chunk_cumsum(g):
        g1 = g.astype(BF16)
        r1 = g - g1.astype(F32)
        g2 = r1.astype(BF16)
        g3 = (r1 - g2.astype(F32)).astype(BF16)
        return _dot(tri_b, g1) + (_dot(tri_b, g2) + _dot(tri_b, g3))

    def operands(cc, slot):
        for sub in range(GDN_PREP_CHUNKS):
            c = cc * GDN_PREP_CHUNKS + sub
            r0 = pl.multiple_of(c * CHUNK, CHUNK)
            ab = ab_ref[pl.ds(r0, CHUNK), :]
            xg = ab + dtb
            softplus = jnp.maximum(xg, 0.0) + jnp.log(1.0 + jnp.exp(-jnp.abs(xg)))
            g_all = neg_a * softplus
            beta_all = _sigmoid(ab)
            gc_all = chunk_cumsum(g_all)
            gc_t = gc_all.T
            gl_s[c] = jnp.exp(gc_all[CHUNK - 1:CHUNK, :])
            for h in range(GDN_HEADS):
                n = sub * GDN_HEADS + h
                bufs = [cv_s.at[slot, 3 * n + i] for i in range(3)]
                q = l2n(conv_cols(c, r0, h * GDN_DK, bufs[0])) * (GDN_DK ** -0.5)
                k = l2n(conv_cols(c, r0, GDN_QK + h * GDN_DK, bufs[1]))
                v = conv_cols(c, r0, 2 * GDN_QK + h * GDN_DV, bufs[2])
                gcol = gc_all[:, h:h + 1]
                grow = gc_t[h:h + 1, :]
                beta = beta_all[:, GDN_HEADS + h:GDN_HEADS + h + 1]
                dec_s[slot, n] = jnp.where(causal, jnp.exp(gcol - grow), 0.0)
                eg = jnp.exp(gcol)
                kb = k * beta
                q_s[slot, n] = q.astype(BF16)
                k_s[slot, n] = k.astype(BF16)
                kb_s[slot, n] = kb.astype(BF16)
                rhs_s[slot, n] = jnp.concatenate([v * beta, kb * eg], axis=1).astype(BF16)
                qg_o[slot, n] = q * eg
                kd = k * jnp.exp(gc_all[CHUNK - 1:CHUNK, h:h + 1] - gcol)
                kdt_o[slot, n] = kd.T.astype(BF16)

    def solve(cc, slot, between):
        items = [(cc * GDN_PREP_CHUNKS + sub, h, sub * GDN_HEADS + h)
                 for sub in range(GDN_PREP_CHUNKS) for h in range(GDN_HEADS)]
        kks = [_dot_nt(kb_s[slot, n], k_s[slot, n]) for _, _, n in items]
        qks = [_dot_nt(q_s[slot, n], k_s[slot, n]) for _, _, n in items]
        lows = [jnp.where(strict, kk * dec_s[slot, n], 0.0) for kk, (_, _, n) in zip(kks, items)]
        attns = [(qk * dec_s[slot, n]).astype(BF16) for qk, (_, _, n) in zip(qks, items)]
        for step in between[:len(between) // 2]:
            step()
        ts = _unit_lower_inverses(lows, ii, jj)
        for step in between[len(between) // 2:]:
            step()
        uws = [_dot(t.astype(BF16), rhs_s[slot, n]).astype(BF16) for t, (_, _, n) in zip(ts, items)]
        kds = [_dot(kdt_o[slot, n], uw) for uw, (_, _, n) in zip(uws, items)]
        ats = [_dot(attn, uw) for attn, uw in zip(attns, uws)]
        for kd_uw, at_uw, (c, h, n) in zip(kds, ats, items):
            b_s[c, h] = kd_uw[:, :GDN_DV]
            m_s[c, h] = (-kd_uw[:, GDN_DV:]).astype(BF16)
            o0_s[c, h] = at_uw[:, :GDN_DV]
            qp_s[c, h] = (qg_o[slot, n] - at_uw[:, GDN_DV:]).astype(BF16)

    ngroup = nchunk // GDN_PREP_CHUNKS
    per_trip = 2 * GDN_PREP_CHUNKS
    normw = normw_ref[...]
    state_s[...] = jnp.zeros_like(state_s)
    for c0 in range(per_trip):
        gl_s[c0] = jnp.zeros((1, LANES), F32)
        for h in range(GDN_HEADS):
            b_s[c0, h] = jnp.zeros((CHUNK, GDN_DV), F32)
            o0_s[c0, h] = jnp.zeros((CHUNK, GDN_DV), F32)
            m_s[c0, h] = jnp.zeros((CHUNK, GDN_DK), BF16)
            qp_s[c0, h] = jnp.zeros((CHUNK, GDN_DK), BF16)
    operands(0, 0)

    def prep(i, carry):
        first = jnp.maximum(per_trip * (i - 1), 0)
        steps = [functools.partial(scan, first + k, 0) for k in range(per_trip)]
        operands(2 * i + 1, 1)
        solve(2 * i, 0, steps[:per_trip // 2])
        operands(jnp.minimum(2 * i + 2, ngroup - 1), 0)
        solve(2 * i + 1, 1, steps[per_trip // 2:])
        return carry

    def scan(c, carry):
        r0 = pl.multiple_of(c * CHUNK, CHUNK)
        gl = gl_s[c]
        heads = range(GDN_HEADS)
        ss = [state_s[h] for h in heads]
        sbs = [s.astype(BF16) for s in ss]
        mss = [_dot(m_s[c, h], sbs[h]) for h in heads]
        qss = [_dot(qp_s[c, h], sbs[h]) for h in heads]
        for h in heads:
            state_s[h] = ss[h] * gl[:, h:h + 1] + (mss[h] + b_s[c, h])
            o = qss[h] + o0_s[c, h]
            z = z_ref[pl.ds(r0, CHUNK), h * GDN_DV:(h + 1) * GDN_DV].astype(F32)
            on = o * lax.rsqrt(jnp.mean(o * o, axis=-1, keepdims=True) + EPS) * normw
            o_ref[pl.ds(r0, CHUNK), h * GDN_DV:(h + 1) * GDN_DV] = (on * _silu(z)).astype(BF16)
        return carry

    ntrip = ngroup // 2
    lax.fori_loop(0, ntrip, prep, 0)
    lax.fori_loop(per_trip * (ntrip - 1), nchunk, scan, 0)


def _gdn_call(proj, ab, conv_w8, alog_p, dtb_p, normw, batch, seq):
    nchunk = seq // CHUNK
    hs = (nchunk, GDN_HEADS, CHUNK, CHUNK)
    ops = (2, GDN_PREP_CHUNKS * GDN_HEADS, CHUNK, CHUNK)
    return pl.pallas_call(
        functools.partial(_gdn_kernel, seq=seq),
        grid=(batch,),
        in_specs=[
            pl.BlockSpec((seq, 3 * GDN_QK), lambda b: (b, 0)),
            pl.BlockSpec((seq, GDN_V), lambda b: (b, 3)),
            pl.BlockSpec((seq, LANES), lambda b: (b, 0)),
            pl.BlockSpec((8, 3 * GDN_QK), lambda b: (0, 0)),
            pl.BlockSpec((1, LANES), lambda b: (0, 0)),
            pl.BlockSpec((1, LANES), lambda b: (0, 0)),
            pl.BlockSpec((1, GDN_DV), lambda b: (0, 0)),
        ],
        out_specs=pl.BlockSpec((seq, GDN_V), lambda b: (b, 0)),
        out_shape=jax.ShapeDtypeStruct((batch * seq, GDN_V), BF16),
        scratch_shapes=[
            pltpu.VMEM(hs, F32), pltpu.VMEM(hs, F32), pltpu.VMEM(hs, BF16), pltpu.VMEM(hs, BF16),
            pltpu.VMEM((nchunk, 1, LANES), F32),
            pltpu.VMEM((GDN_HEADS, GDN_DK, GDN_DV), F32),
            pltpu.VMEM(ops, BF16), pltpu.VMEM(ops, BF16), pltpu.VMEM(ops, BF16),
            pltpu.VMEM(ops[:3] + (2 * CHUNK,), BF16),
            pltpu.VMEM(ops, F32),
            pltpu.VMEM((2, 3 * ops[1], 8 + CHUNK, LANES), F32),
            pltpu.VMEM(ops, F32), pltpu.VMEM(ops, BF16),
        ],
        compiler_params=pltpu.CompilerParams(
            dimension_semantics=("arbitrary",), vmem_limit_bytes=VMEM_LIMIT),
        name="gdn",
    )(proj, proj, ab, conv_w8, alog_p, dtb_p, normw)


def _ret_kernel(qk_ref, v_ref, g_ref, sin_ref, cos_ref, inner_ref, kdec_ref, qdec_ref, cdec_ref,
                o_ref, state_s, q_s, qd_s, k_s, kt_s, *, seq):
    nchunk = seq // CHUNK
    lane = lax.broadcasted_iota(jnp.int32, (CHUNK, RET_DK), 1)
    even = (lane % 2) == 0

    def rotate(x, sin, cos):
        nxt = pltpu.roll(x, RET_DK - 1, axis=1)
        prv = pltpu.roll(x, 1, axis=1)
        return x * cos + jnp.where(even, -nxt, prv) * sin

    state_s[...] = jnp.zeros_like(state_s)
    kdec = kdec_ref[...]
    qdec = qdec_ref[...]
    cdec = cdec_ref[...]

    heads = range(RET_HEADS)

    def operands(c, slot):
        r0 = pl.multiple_of(c * CHUNK, CHUNK)
        sin = sin_ref[pl.ds(r0, CHUNK), :]
        cos = cos_ref[pl.ds(r0, CHUNK), :]
        for h in heads:
            q = rotate(qk_ref[pl.ds(r0, CHUNK), h * RET_DK:(h + 1) * RET_DK].astype(F32), sin, cos)
            k = rotate(qk_ref[pl.ds(r0, CHUNK), RET_QK + h * RET_DK:RET_QK + (h + 1) * RET_DK]
                       .astype(F32), sin, cos) * (RET_DK ** -0.5)
            q_s[slot, h] = q.astype(BF16)
            qd_s[slot, h] = (q * qdec[:, h:h + 1]).astype(BF16)
            k_s[slot, h] = k.astype(BF16)
            kt_s[slot, h] = (k * kdec[:, h:h + 1]).T.astype(BF16)

    def outputs(c, slot):
        r0 = pl.multiple_of(c * CHUNK, CHUNK)
        vs = [v_ref[pl.ds(r0, CHUNK), h * RET_DV:(h + 1) * RET_DV] for h in heads]
        ss = [state_s[h] for h in heads]
        qks = [_dot_nt(q_s[slot, h], k_s[slot, h]) for h in heads]
        inters = [_dot(qd_s[slot, h], ss[h].astype(BF16)) for h in heads]
        kvs = [_dot(kt_s[slot, h], vs[h]) for h in heads]
        intras = [_dot((qks[h] * inner_ref[h]).astype(BF16), vs[h]) for h in heads]
        for h in heads:
            state_s[h] = ss[h] * cdec[:, h:h + 1] + kvs[h]
            o = intras[h] + inters[h]
            gate = g_ref[pl.ds(r0, CHUNK), h * RET_DV:(h + 1) * RET_DV].astype(F32)
            on = o * lax.rsqrt(jnp.mean(o * o, axis=-1, keepdims=True) + EPS)
            o_ref[pl.ds(r0, CHUNK), h * RET_DV:(h + 1) * RET_DV] = (on * _silu(gate)).astype(BF16)

    operands(0, 0)

    def body(i, carry):
        operands(2 * i + 1, 1)
        outputs(2 * i, 0)
        operands(jnp.minimum(2 * i + 2, nchunk - 1), 0)
        outputs(2 * i + 1, 1)
        return carry

    lax.fori_loop(0, nchunk // 2, body, 0)


def _ret_call(proj, sin, cos, inner, kdec, qdec, cdec, batch, seq):
    return pl.pallas_call(
        functools.partial(_ret_kernel, seq=seq),
        grid=(batch,),
        in_specs=[
            pl.BlockSpec((seq, 2 * RET_QK), lambda b: (b, 2)),
            pl.BlockSpec((seq, RET_V), lambda b: (b, 3)),
            pl.BlockSpec((seq, RET_V), lambda b: (b, 4)),
            pl.BlockSpec((seq, RET_DK), lambda b: (0, 0)),
            pl.BlockSpec((seq, RET_DK), lambda b: (0, 0)),
            pl.BlockSpec((RET_HEADS, CHUNK, CHUNK), lambda b: (0, 0, 0)),
            pl.BlockSpec((CHUNK, LANES), lambda b: (0, 0)),
            pl.BlockSpec((CHUNK, LANES), lambda b: (0, 0)),
            pl.BlockSpec((1, LANES), lambda b: (0, 0)),
        ],
        out_specs=pl.BlockSpec((seq, RET_V), lambda b: (b, 0)),
        out_shape=jax.ShapeDtypeStruct((batch * seq, RET_V), BF16),
        scratch_shapes=[pltpu.VMEM((RET_HEADS, RET_DK, RET_DV), F32)]
        + [pltpu.VMEM((2, RET_HEADS, CHUNK, RET_DK), BF16)] * 4,
        compiler_params=pltpu.CompilerParams(
            dimension_semantics=("arbitrary",), vmem_limit_bytes=VMEM_LIMIT),
        name="retention",
    )(proj, proj, proj, sin, cos, inner, kdec, qdec, cdec)


def _retention_tables(seq):
    inv_freq = 1.0 / (ROPE_BASE ** jnp.linspace(0.0, 1.0, RET_DK // 2, dtype=F32))
    ang = jnp.arange(seq, dtype=F32)[:, None] * inv_freq[None, :]
    sin = jnp.repeat(jnp.sin(ang), 2, axis=-1)
    cos = jnp.repeat(jnp.cos(ang), 2, axis=-1)
    log_gamma = jnp.log(1.0 - 2.0 ** (-5.0 - jnp.arange(RET_HEADS, dtype=F32)))
    idx = jnp.arange(CHUNK, dtype=F32)
    causal = jnp.tril(jnp.ones((CHUNK, CHUNK), dtype=bool))
    rel = jnp.where(causal, idx[:, None] - idx[None, :], 0.0)
    inner = jnp.where(causal, jnp.exp(rel[None] * log_gamma[:, None, None]), 0.0)
    k_decay = jnp.exp(log_gamma[:, None] * (CHUNK - 1.0 - idx)[None, :])
    q_decay = jnp.exp(log_gamma[:, None] * (idx + 1.0)[None, :])
    chunk_decay = jnp.exp(log_gamma * CHUNK)
    pad = LANES - RET_HEADS
    kdec = jnp.pad(k_decay.T, ((0, 0), (0, pad)))
    qdec = jnp.pad(q_decay.T, ((0, 0), (0, pad)))
    cdec = jnp.pad(chunk_decay[None, :], ((0, 0), (0, pad)))
    return sin, cos, inner, kdec, qdec, cdec


def _merge_kernel(x_ref, ya_ref, yb_ref, ma_ref, mb_ref, wa_ref, wr_ref, wo_ref, nw_ref,
                  wrt_ref, br_ref, h_ref, xn_ref, ridx_ref, rw_ref, cnt_ref):
    tm = x_ref.shape[0] // MERGE_PARTS
    rows = [pl.ds(i * tm, tm) for i in range(MERGE_PARTS)]
    a_ = [_dot(ya_ref[r, :], wa_ref[...]) for r in rows]
    r_ = [_dot(yb_ref[r, :], wr_ref[...]) for r in rows]
    merged = [(_sigmoid(ma_ref[r, :].astype(F32)) * a + _sigmoid(mb_ref[r, :].astype(F32)) * rr)
              .astype(BF16) for r, a, rr in zip(rows, a_, r_)]
    hs = [x_ref[r, :] + _dot(mg, wo_ref[...]) for r, mg in zip(rows, merged)]
    xcats = []
    for r, h in zip(rows, hs):
        h_ref[r, :] = h
        xn = h * lax.rsqrt(jnp.mean(h * h, axis=-1, keepdims=True) + EPS) * nw_ref[...]
        xn_ref[r, :] = _pack_bf16_pairs(xn)
        xh = xn.astype(BF16)
        xl = (xn - xh.astype(F32)).astype(BF16)
        xcats.append(jnp.concatenate([xh, xl], axis=0))
    parts_ = [_dot(xc, wrt_ref[...]) for xc in xcats]
    counts = jnp.zeros((1, LANES), F32)
    for r, parts in zip(rows, parts_):
        counts = counts + _route(parts, tm, br_ref[...], ridx_ref.at[r, :], rw_ref.at[r, :])

    @pl.when(pl.program_id(0) == 0)
    def _():
        cnt_ref[...] = jnp.zeros_like(cnt_ref)

    cnt_ref[...] += jnp.broadcast_to(counts, cnt_ref.shape)


def _route(parts, tm, bias, ridx_ref, rw_ref):
    logits = (parts[:tm, :LANES] + (parts[tm:, :LANES] + parts[:tm, LANES:]
                                    + parts[tm:, LANES:])) + bias
    lane = lax.broadcasted_iota(jnp.int32, (tm, LANES), 1)
    neg = -jnp.inf
    gl = jnp.where((lane >= N_EXPERTS) & (lane < N_EXPERTS + N_GROUPS), logits, neg)
    gmax = jnp.max(gl, axis=-1, keepdims=True)
    gidx = jnp.min(jnp.where(gl == gmax, lane, LANES), axis=-1, keepdims=True) - N_EXPERTS
    g_w = 1.0 / jnp.sum(jnp.exp(gl - gmax), axis=-1, keepdims=True)
    el = jnp.where((lane // EXPERTS_PER_GROUP == gidx) & (lane < N_EXPERTS), logits, neg)
    m1 = jnp.max(el, axis=-1, keepdims=True)
    i1 = jnp.min(jnp.where(el == m1, lane, LANES), axis=-1, keepdims=True)
    el2 = jnp.where(lane == i1, neg, el)
    m2 = jnp.max(el2, axis=-1, keepdims=True)
    i2 = jnp.min(jnp.where(el2 == m2, lane, LANES), axis=-1, keepdims=True)
    e2 = jnp.exp(m2 - m1)
    p1 = g_w / (1.0 + e2)
    p2 = g_w * e2 / (1.0 + e2)
    ridx_ref[...] = jnp.where(lane == 0, i1, jnp.where(lane == 1, i2, 0))
    rw_ref[...] = jnp.where(lane == 0, p1, jnp.where(lane == 1, p2, 0.0))
    onehot = jnp.where((lane == i1) | (lane == i2), 1.0, 0.0)
    return jnp.sum(onehot, axis=0, keepdims=True)


def _merge_call(x2, ya, yb, proj, wa, wr, wo, nw, w_router, b_router, tm=1024):
    m = x2.shape[0]
    full = lambda shape: pl.BlockSpec(shape, lambda i: (0, 0))
    return pl.pallas_call(
        _merge_kernel,
        grid=(m // tm,),
        in_specs=[
            pl.BlockSpec((tm, D_MODEL), lambda i: (i, 0)),
            pl.BlockSpec((tm, GDN_V), lambda i: (i, 0)),
            pl.BlockSpec((tm, RET_V), lambda i: (i, 0)),
            pl.BlockSpec((tm, D_MODEL), lambda i: (i, 5)),
            pl.BlockSpec((tm, D_MODEL), lambda i: (i, 6)),
            full((GDN_V, D_MODEL)), full((RET_V, D_MODEL)), full((D_MODEL, D_MODEL)),
            full((1, D_MODEL)),
            full((D_MODEL, 2 * LANES)), full((1, LANES)),
        ],
        out_specs=[
            pl.BlockSpec((tm, D_MODEL), lambda i: (i, 0)),
            pl.BlockSpec((tm, HALF), lambda i: (i, 0)),
            pl.BlockSpec((tm, LANES), lambda i: (i, 0)),
            pl.BlockSpec((tm, LANES), lambda i: (i, 0)),
            pl.BlockSpec((8, LANES), lambda i: (0, 0)),
        ],
        out_shape=[
            jax.ShapeDtypeStruct((m, D_MODEL), F32),
            jax.ShapeDtypeStruct((m, HALF), I32),
            jax.ShapeDtypeStruct((m, LANES), I32),
            jax.ShapeDtypeStruct((m, LANES), F32),
            jax.ShapeDtypeStruct((8, LANES), F32),
        ],
        compiler_params=pltpu.CompilerParams(
            dimension_semantics=("arbitrary",), vmem_limit_bytes=VMEM_LIMIT),
        name="merge_router",
    )(x2, ya, yb, proj, proj, wa, wr, wo, nw, w_router, b_router)


def _slot_counts(m):
    n_slots = TOP_K * m + N_EXPERTS * SLOT_TILE
    return n_slots, n_slots // SLOT_TILE


def _lane_prefix_sum(x, lane):
    s = 1
    while s < LANES:
        x = x + jnp.where(lane >= s, pltpu.roll(x, s, axis=1), 0.0)
        s *= 2
    return x


def _plan_kernel(ridx_ref, cnt_ref, slots_ref, tile_ref, seg_ref, carry_s, off_s):
    i = pl.program_id(0)
    lane = lax.broadcasted_iota(I32, (PLAN_TILE, LANES), 1)
    row = lax.broadcasted_iota(I32, (PLAN_TILE, LANES), 0)
    lane1 = lane[0:1]

    @pl.when(i == 0)
    def _():
        cnt = cnt_ref[0:1, :]
        tile = float(SLOT_TILE)
        padded = jnp.floor((cnt + (tile - 1.0)) / tile) * tile
        incl = _lane_prefix_sum(padded, lane1)
        off = incl - padded
        off_s[...] = off
        carry_s[...] = jnp.zeros_like(carry_s)
        seg_row = lax.broadcasted_iota(I32, (8, LANES), 0)
        seg_ref[...] = jnp.where(seg_row == 0, off / tile, jnp.where(seg_row == 1, padded / tile, 0.0)
                                 ).astype(I32)
        first = (row * SLOT_TILE).astype(F32)
        ended = jnp.where((lane < N_EXPERTS) & (incl <= first), 1.0, 0.0)
        tile_e = jnp.sum(ended, axis=-1, keepdims=True)
        last = jnp.sum(jnp.where(lane.astype(F32) == tile_e, off + cnt, 0.0), axis=-1, keepdims=True)
        used = jnp.clip(last - first[:, 0:1], 0.0, tile)
        tile_ref[...] = jnp.where(lane == 0, tile_e, jnp.where(lane == 1, used, 0.0)).astype(I32)

    strict = jnp.where(row[:, 0:1] > lax.broadcasted_iota(I32, (PLAN_TILE, PLAN_TILE), 1),
                       1.0, 0.0).astype(BF16)
    off = off_s[...]
    carry = carry_s[...]
    for sb in range(PLAN_STEP // PLAN_TILE):
        rows = pl.ds(sb * PLAN_TILE, PLAN_TILE)
        e1 = ridx_ref[rows, 0:1]
        e2 = ridx_ref[rows, 1:2]
        onehot = jnp.where((lane == e1) | (lane == e2), 1.0, 0.0)
        pos = _dot(strict, onehot.astype(BF16)) + (carry + off)
        s1 = jnp.sum(jnp.where(lane == e1, pos, 0.0), axis=-1, keepdims=True)
        s2 = jnp.sum(jnp.where(lane == e2, pos, 0.0), axis=-1, keepdims=True)
        both = jnp.where(lane == 0, s1, jnp.where(lane == 1, s2, 0.0))
        for q in range(PLAN_TILE // LANES):
            t = both[q * LANES:(q + 1) * LANES].T
            c0 = sb * PLAN_TILE + q * LANES
            slots_ref[:, c0:c0 + LANES] = t[0:8].astype(I32)
        carry = carry + jnp.sum(onehot, axis=0, keepdims=True)
    carry_s[...] = carry


def _plan_call(ridx, cnt):
    m = ridx.shape[0]
    _, n_tiles = _slot_counts(m)
    assert n_tiles <= PLAN_TILE and SLOT_TILE == PLAN_TILE
    return pl.pallas_call(
        _plan_kernel,
        grid=(m // PLAN_STEP,),
        in_specs=[pl.BlockSpec((PLAN_STEP, LANES), lambda i: (i, 0)),
                  pl.BlockSpec((8, LANES), lambda i: (0, 0))],
        out_specs=[
            pl.BlockSpec((8, PLAN_STEP), lambda i: (0, i)),
            pl.BlockSpec((PLAN_TILE, LANES), lambda i: (0, 0)),
            pl.BlockSpec((8, LANES), lambda i: (0, 0)),
        ],
        out_shape=[
            jax.ShapeDtypeStruct((8, m), I32),
            jax.ShapeDtypeStruct((PLAN_TILE, LANES), I32),
            jax.ShapeDtypeStruct((8, LANES), I32),
        ],
        scratch_shapes=[pltpu.VMEM((1, LANES), F32), pltpu.VMEM((1, LANES), F32)],
        compiler_params=pltpu.CompilerParams(dimension_semantics=("arbitrary",)),
        name="dispatch_plan",
    )(ridx, cnt)


def _sc_mesh():
    return plsc.VectorSubcoreMesh(core_axis_name="c", subcore_axis_name="s")


def _sc_worker():
    return lax.axis_index("s") * 2 + lax.axis_index("c")


def _sc_dispatch(xn, slot1, slot2, n_rows):
    m = xn.shape[0]
    per = m // SC_WORKERS
    n_pairs = per // (2 * SC_ROWS)

    @functools.partial(
        pl.kernel, mesh=_sc_mesh(),
        out_type=jax.ShapeDtypeStruct((n_rows, HALF), I32),
        scratch_types=[pltpu.VMEM((per // SC_ROWS, SC_ROWS), I32), pltpu.VMEM((per // SC_ROWS, SC_ROWS), I32),
                       pltpu.VMEM((SC_ROWS, HALF), I32), pltpu.VMEM((SC_ROWS, HALF), I32),
                       pltpu.SemaphoreType.DMA, pltpu.SemaphoreType.DMA, pltpu.SemaphoreType.DMA],
        name="sc_dispatch")
    def k(x_hbm, s1_hbm, s2_hbm, o_hbm, i1_v, i2_v, rows0, rows1, sem_r0, sem_r1, sem_w):
        wid = _sc_worker()
        base = wid * per

        def read(chunk, rows_v, sem):
            return pltpu.make_async_copy(x_hbm.at[pl.ds(base + chunk * SC_ROWS, SC_ROWS)], rows_v, sem)

        def scatter(chunk, rows_v):
            c1 = pltpu.async_copy(rows_v, o_hbm.at[i1_v.at[chunk]], sem_w)
            c2 = pltpu.async_copy(rows_v, o_hbm.at[i2_v.at[chunk]], sem_w)
            c1.wait()
            c2.wait()

        read(0, rows0, sem_r0).start()
        pltpu.sync_copy(s1_hbm.at[pl.ds(wid * (per // SC_ROWS), per // SC_ROWS)], i1_v)
        pltpu.sync_copy(s2_hbm.at[pl.ds(wid * (per // SC_ROWS), per // SC_ROWS)], i2_v)

        @pl.loop(0, n_pairs)
        def _(i):
            read(2 * i, rows0, sem_r0).wait()
            read(2 * i + 1, rows1, sem_r1).start()
            scatter(2 * i, rows0)
            read(2 * i + 1, rows1, sem_r1).wait()

            @pl.when(i + 1 < n_pairs)
            def _():
                read(2 * i + 2, rows0, sem_r0).start()

            scatter(2 * i + 1, rows1)

    return k(xn, slot1, slot2)


def _sc_collect(ys, slot1, slot2):
    m = slot1.size
    per = m // SC_WORKERS
    row = jax.ShapeDtypeStruct((m, HALF), I32)

    @functools.partial(
        pl.kernel, mesh=_sc_mesh(), out_type=[row, row],
        scratch_types=[pltpu.VMEM((per // SC_ROWS, SC_ROWS), I32), pltpu.VMEM((per // SC_ROWS, SC_ROWS), I32),
                       pltpu.VMEM((SC_ROWS, HALF), I32), pltpu.VMEM((SC_ROWS, HALF), I32),
                       pltpu.SemaphoreType.DMA, pltpu.SemaphoreType.DMA],
        name="sc_collect")
    def k(y_hbm, s1_hbm, s2_hbm, g1_hbm, g2_hbm, i1_v, i2_v, rows1, rows2, sem_g, sem_w):
        wid = _sc_worker()
        base = wid * per
        pltpu.sync_copy(s1_hbm.at[pl.ds(wid * (per // SC_ROWS), per // SC_ROWS)], i1_v)
        pltpu.sync_copy(s2_hbm.at[pl.ds(wid * (per // SC_ROWS), per // SC_ROWS)], i2_v)

        @pl.loop(0, per // SC_ROWS)
        def _(ci):
            t0 = base + ci * SC_ROWS
            a1 = pltpu.async_copy(y_hbm.at[i1_v.at[ci]], rows1, sem_g)
            a2 = pltpu.async_copy(y_hbm.at[i2_v.at[ci]], rows2, sem_g)
            a1.wait()
            a2.wait()
            w1 = pltpu.async_copy(rows1, g1_hbm.at[pl.ds(t0, SC_ROWS)], sem_w)
            w2 = pltpu.async_copy(rows2, g2_hbm.at[pl.ds(t0, SC_ROWS)], sem_w)
            w1.wait()
            w2.wait()

    return k(ys, slot1, slot2)


def _expert_kernel(first_ref, count_ref, used_ref, xs_hbm, wg_ref, wu_ref, wd_ref, ys_hbm,
                   wg_b, wu_b, wd_b, x_buf, y_buf, sem_x, sem_y, *, n_tiles):
    e = pl.program_id(0)
    first = first_ref[e]
    count = count_ref[e]
    wg_b[...] = wg_ref[0].astype(BF16)
    wu_b[...] = wu_ref[0].astype(BF16)
    wd_b[...] = wd_ref[0].astype(BF16)

    def tile_rows(t):
        return pl.ds(pl.multiple_of((first + t) * SLOT_TILE, SLOT_TILE), SLOT_TILE)

    def load(t, slot):
        return pltpu.make_async_copy(xs_hbm.at[tile_rows(t)], x_buf.at[slot], sem_x.at[slot])

    def store(t, slot):
        return pltpu.make_async_copy(y_buf.at[slot], ys_hbm.at[tile_rows(t)], sem_y.at[slot])

    @pl.when(count > 0)
    def _():
        load(0, 0).start()

    def tile(t, carry):
        slot = t % 2
        load(t, slot).wait()

        @pl.when(t + 1 < count)
        def _():
            load(t + 1, 1 - slot).start()

        @pl.when(t >= 2)
        def _():
            store(t - 2, slot).wait()

        half = SLOT_TILE // 2
        used = used_ref[first + t]
        row_id = lax.broadcasted_iota(I32, (half, HALF), 0)
        xs = [_unpack_bf16_pairs(jnp.where(row_id + i * half < used,
                                           x_buf[slot, i * half:(i + 1) * half, :], 0)).astype(BF16)
              for i in range(2)]
        gs = [_dot(x, wg_b[...]) for x in xs]
        us = [_dot(x, wu_b[...]) for x in xs]
        hids = [(_silu(g) * u).astype(BF16) for g, u in zip(gs, us)]
        ys = [_dot(hid, wd_b[...]) for hid in hids]
        for i, y in enumerate(ys):
            y_buf[slot, i * half:(i + 1) * half, :] = _pack_bf16_pairs(y)
        store(t, slot).start()
        return carry

    lax.fori_loop(0, count, tile, 0)

    @pl.when(count >= 1)
    def _():
        store(count - 1, (count - 1) % 2).wait()

    @pl.when(count >= 2)
    def _():
        store(count - 2, count % 2).wait()

    @pl.when(e == pl.num_programs(0) - 1)
    def _():
        y_buf[0] = jnp.zeros((SLOT_TILE, HALF), I32)

        def zero_tile(t, carry):
            tail = pltpu.make_async_copy(y_buf.at[0], ys_hbm.at[tile_rows(t)], sem_y.at[0])
            tail.start()
            tail.wait()
            return carry

        lax.fori_loop(count, n_tiles - first, zero_tile, 0)


def _expert_call(seg_first, seg_count, tile_used, xs, wg, wu, wd, n_tiles):
    hbm = pl.BlockSpec(memory_space=pl.ANY)
    wmap = lambda e, first, count, used: (e, 0, 0)
    return pl.pallas_call(
        functools.partial(_expert_kernel, n_tiles=n_tiles),
        grid_spec=pltpu.PrefetchScalarGridSpec(
            num_scalar_prefetch=3,
            grid=(N_EXPERTS,),
            in_specs=[hbm,
                      pl.BlockSpec((1, D_MODEL, D_EXPERT), wmap),
                      pl.BlockSpec((1, D_MODEL, D_EXPERT), wmap),
                      pl.BlockSpec((1, D_EXPERT, D_MODEL), wmap)],
            out_specs=hbm,
            scratch_shapes=[
                pltpu.VMEM((D_MODEL, D_EXPERT), BF16), pltpu.VMEM((D_MODEL, D_EXPERT), BF16),
                pltpu.VMEM((D_EXPERT, D_MODEL), BF16),
                pltpu.VMEM((2, SLOT_TILE, HALF), I32), pltpu.VMEM((2, SLOT_TILE, HALF), I32),
                pltpu.SemaphoreType.DMA((2,)), pltpu.SemaphoreType.DMA((2,)),
            ],
        ),
        out_shape=jax.ShapeDtypeStruct((n_tiles * SLOT_TILE, HALF), I32),
        compiler_params=pltpu.CompilerParams(
            dimension_semantics=("arbitrary",), vmem_limit_bytes=VMEM_LIMIT),
        name="experts",
    )(seg_first, seg_count, tile_used, xs, wg, wu, wd)


def _final_kernel(h_ref, g1_ref, g2_ref, rw_ref, nw_ref, o_ref):
    rw = rw_ref[...]
    y = rw[:, 0:1] * _unpack_bf16_pairs(g1_ref[...]) + rw[:, 1:2] * _unpack_bf16_pairs(g2_ref[...])
    h = h_ref[...] + y
    o_ref[...] = h * lax.rsqrt(jnp.mean(h * h, axis=-1, keepdims=True) + EPS) * nw_ref[...]


def _final_call(h1, g1, g2, rw, nw, tm=1024):
    m = h1.shape[0]
    return pl.pallas_call(
        _final_kernel,
        grid=(m // tm,),
        in_specs=[
            pl.BlockSpec((tm, D_MODEL), lambda i: (i, 0)),
            pl.BlockSpec((tm, HALF), lambda i: (i, 0)),
            pl.BlockSpec((tm, HALF), lambda i: (i, 0)),
            pl.BlockSpec((tm, LANES), lambda i: (i, 0)),
            pl.BlockSpec((1, D_MODEL), lambda i: (0, 0)),
        ],
        out_specs=pl.BlockSpec((tm, D_MODEL), lambda i: (i, 0)),
        out_shape=jax.ShapeDtypeStruct((m, D_MODEL), F32),
        compiler_params=pltpu.CompilerParams(dimension_semantics=("arbitrary",)),
        name="combine_final",
    )(h1, g1, g2, rw, nw)


def _pad_lanes(a):
    return jnp.pad(a, ((0, 0), (0, LANES - a.shape[1])))


def kernel(x, norm_mix_w, w_in, conv_w, A_log, dt_bias, gdn_norm_w, w_up_gdn, w_up_ret, w_out,
           norm_ffn_w, w_group, b_group, w_expert, b_expert, w_gate, w_up, w_down, norm_final_w):
    batch, seq, d = x.shape
    m = batch * seq
    h = x.reshape(m, d)
    depth = w_in.shape[0]
    sin, cos, inner, kdec, qdec, cdec = _retention_tables(seq)
    for l in range(depth):
        w_main, w_ab = _repack_call(jnp.transpose(w_in[l]))
        proj, ab = _proj_call(h, norm_mix_w[l][None, :], w_main, w_ab)

        conv8 = jnp.pad(conv_w[l], ((0, 8 - GDN_CONV), (0, 0)))
        ya = _gdn_call(proj, ab, conv8, _pad_lanes(A_log[l][None, :]), _pad_lanes(dt_bias[l][None, :]),
                       gdn_norm_w[l][None, :], batch, seq)
        yb = _ret_call(proj, sin, cos, inner, kdec, qdec, cdec, batch, seq)

        w_router = _pad_lanes(jnp.concatenate([w_expert[l], w_group[l]], axis=1))
        wr_hi = w_router.astype(BF16)
        wr_lo = (w_router - wr_hi.astype(F32)).astype(BF16)
        b_router = _pad_lanes(jnp.concatenate([b_expert[l], b_group[l]])[None, :])
        h1, xn, ridx, rw, cnt = _merge_call(
            h, ya, yb, proj, w_up_gdn[l].astype(BF16), w_up_ret[l].astype(BF16), w_out[l].astype(BF16),
            norm_ffn_w[l][None, :], jnp.concatenate([wr_hi, wr_lo], axis=1), b_router)

        n_slots, n_tiles = _slot_counts(m)
        slots, tiles, seg = _plan_call(ridx, cnt)
        slot1, slot2 = slots[0].reshape(-1, SC_ROWS), slots[1].reshape(-1, SC_ROWS)
        xs = _sc_dispatch(xn, slot1, slot2, n_slots)
        ys = _expert_call(seg[0, :N_EXPERTS], seg[1, :N_EXPERTS], tiles[:n_tiles, 1], xs,
                          w_gate[l], w_up[l], w_down[l], n_tiles)
        g1, g2 = _sc_collect(ys, slot1, slot2)

        assert depth == 1
        h = _final_call(h1, g1, g2, rw, norm_final_w[None, :])
    return h.reshape(batch, seq, d)
```

```python
import functools
import math

import jax
import jax.numpy as jnp
from jax import lax
from jax.experimental import pallas as pl
from jax.experimental.pallas import tpu as pltpu
from jax.experimental.pallas import tpu_sc as plsc

F32 = jnp.float32
BF16 = jnp.bfloat16
I32 = jnp.int32
U32 = jnp.uint32

D_MODEL = 1024
EPS = 1e-6
GDN_HEADS = 4
GDN_DK = 128
GDN_DV = 128
GDN_CONV = 4
RET_HEADS = 4
RET_DK = 128
RET_DV = 256
ROPE_BASE = 10000.0
N_GROUPS = 4
EXPERTS_PER_GROUP = 8
N_EXPERTS = N_GROUPS * EXPERTS_PER_GROUP
D_EXPERT = 512

GDN_QK = GDN_HEADS * GDN_DK
GDN_V = GDN_HEADS * GDN_DV
RET_QK = RET_HEADS * RET_DK
RET_V = RET_HEADS * RET_DV

LANES = 128
CHUNK = 128
INV_BLOCK = 16
GDN_PREP_CHUNKS = 2
VMEM_LIMIT = 56 * 1024 * 1024

REPACK_COLS = 512
MERGE_PARTS = 1
TOP_K = 2
SLOT_TILE = 256
PLAN_TILE = 256
PLAN_STEP = 1024
HALF = D_MODEL // 2
SC_WORKERS = 32
SC_ROWS = 64

PROJ_COLS = 3 * GDN_QK + GDN_V + 2 * RET_QK + 2 * RET_V + 2 * D_MODEL


def _silu(x):
    return x / (1.0 + jnp.exp(-x))


def _sigmoid(x):
    return 1.0 / (1.0 + jnp.exp(-x))


def _dot(a, b):
    return jnp.dot(a, b, preferred_element_type=F32)


def _dot_nt(a, b):
    return lax.dot_general(a, b, (((1,), (1,)), ((), ())), preferred_element_type=F32)


def _pack_bf16_pairs(x):
    bits = lax.bitcast_convert_type(x.astype(BF16).astype(F32), U32)
    packed = (bits[:, :HALF] >> 16) | (bits[:, HALF:] & jnp.uint32(0xFFFF0000))
    return lax.bitcast_convert_type(packed, I32)


def _unpack_bf16_pairs(p):
    p = lax.bitcast_convert_type(p, U32)
    lo = lax.bitcast_convert_type(p << 16, F32)
    hi = lax.bitcast_convert_type(p & jnp.uint32(0xFFFF0000), F32)
    return jnp.concatenate([lo, hi], axis=1)


def _proj_kernel(x_ref, nw_ref, w_ref, wab_ref, proj_ref, ab_ref, u_ref):
    j = pl.program_id(1)

    @pl.when(j == 0)
    def _():
        x = x_ref[...]
        u = x * lax.rsqrt(jnp.mean(x * x, axis=-1, keepdims=True) + EPS) * nw_ref[...]
        ub = u.astype(BF16)
        u_ref[...] = ub
        ab_ref[...] = _dot(ub, wab_ref[...])

    proj_ref[...] = _dot(u_ref[...], w_ref[...]).astype(BF16)


def _repack_kernel(wt_ref, abt_ref, main_ref, ab_ref):
    main_ref[...] = wt_ref[...].T.astype(BF16)

    @pl.when(pl.program_id(0) == 0)
    def _():
        ab = abt_ref[...].T
        ab_ref[...] = jnp.concatenate(
            [ab, jnp.zeros((ab.shape[0], LANES - ab.shape[1]), F32)], axis=1).astype(BF16)


def _repack_call(w_in_t):
    d_in, d = w_in_t.shape
    o_ab = 3 * GDN_QK
    n_ab = 2 * GDN_HEADS
    src = lambda r: pl.multiple_of(r * REPACK_COLS + jnp.where(r * REPACK_COLS >= o_ab, n_ab, 0), 8)
    return pl.pallas_call(
        _repack_kernel,
        grid=(PROJ_COLS // REPACK_COLS,),
        in_specs=[pl.BlockSpec((pl.Element(REPACK_COLS), pl.Element(d)), lambda r: (src(r), 0)),
                  pl.BlockSpec((pl.Element(n_ab), pl.Element(d)), lambda r: (o_ab, 0))],
        out_specs=[pl.BlockSpec((d, REPACK_COLS), lambda r: (0, r)),
                   pl.BlockSpec((d, LANES), lambda r: (0, 0))],
        out_shape=[jax.ShapeDtypeStruct((d, PROJ_COLS), BF16), jax.ShapeDtypeStruct((d, LANES), BF16)],
        compiler_params=pltpu.CompilerParams(dimension_semantics=("arbitrary",)),
        name="repack_w_in",
    )(w_in_t, w_in_t)


def _proj_call(x2, norm_w, w_main, w_ab, tm=1024, tn=3584):
    m = x2.shape[0]
    return pl.pallas_call(
        _proj_kernel,
        grid=(m // tm, PROJ_COLS // tn),
        in_specs=[
            pl.BlockSpec((tm, D_MODEL), lambda i, j: (i, 0)),
            pl.BlockSpec((1, D_MODEL), lambda i, j: (0, 0)),
            pl.BlockSpec((D_MODEL, tn), lambda i, j: (0, j)),
            pl.BlockSpec((D_MODEL, LANES), lambda i, j: (0, 0)),
        ],
        out_specs=[
            pl.BlockSpec((tm, tn), lambda i, j: (i, j)),
            pl.BlockSpec((tm, LANES), lambda i, j: (i, 0)),
        ],
        out_shape=[
            jax.ShapeDtypeStruct((m, PROJ_COLS), BF16),
            jax.ShapeDtypeStruct((m, LANES), F32),
        ],
        scratch_shapes=[pltpu.VMEM((tm, D_MODEL), BF16)],
        compiler_params=pltpu.CompilerParams(
            dimension_semantics=("arbitrary", "arbitrary"), vmem_limit_bytes=VMEM_LIMIT),
        name="proj",
    )(x2, norm_w, w_main, w_ab)


def _unit_lower_inverses(lows, ii, jj):
    eye = jnp.where(ii == jj, 1.0, 0.0).astype(F32)
    in_block = (ii // INV_BLOCK) == (jj // INV_BLOCK)
    ps = [jnp.where(in_block, -low, 0.0) for low in lows]
    ts = [eye + p for p in ps]
    span = 2
    while span < INV_BLOCK:
        ps = [_dot(p, p) for p in ps]
        ts = [t + _dot(t, p) for t, p in zip(ts, ps)]
        span *= 2
    s = INV_BLOCK
    while s < CHUNK:
        off_diag = ((ii // (2 * s)) == (jj // (2 * s))) & ((ii // s) != (jj // s))
        xs = [_dot(jnp.where(off_diag, low, 0.0), t) for low, t in zip(lows, ts)]
        ts = [t - _dot(t, x) for t, x in zip(ts, xs)]
        s *= 2
    return ts


def _gdn_kernel(qkv_ref, z_ref, ab_ref, convw_ref, alog_ref, dtb_ref, normw_ref, o_ref,
                b_s, o0_s, m_s, qp_s, gl_s, state_s, q_s, k_s, kb_s, rhs_s, dec_s, cv_s, qg_o, kdt_o,
                *, seq):
    nchunk = seq // CHUNK
    ii = lax.broadcasted_iota(jnp.int32, (CHUNK, CHUNK), 0)
    jj = lax.broadcasted_iota(jnp.int32, (CHUNK, CHUNK), 1)
    causal = ii >= jj
    strict = ii > jj
    tri = jnp.where(causal, 1.0, 0.0).astype(F32)
    neg_a = -jnp.exp(alog_ref[...])
    dtb = dtb_ref[...]

    def conv_cols(c, r0, lo, buf):
        x = qkv_ref[pl.ds(r0, CHUNK), lo:lo + LANES].astype(F32)
        prev0 = pl.multiple_of(jnp.maximum(r0 - 16, 0), 16)
        prev = qkv_ref[pl.ds(prev0, 16), lo:lo + LANES].astype(F32)
        buf[0:8, :] = prev[8:16] * jnp.where(c > 0, 1.0, 0.0)
        buf[8:8 + CHUNK, :] = x
        w = convw_ref[:, lo:lo + LANES]
        y = (w[3:4] * x + w[2:3] * buf[7:7 + CHUNK, :] + w[1:2] * buf[6:6 + CHUNK, :]
             + w[0:1] * buf[5:5 + CHUNK, :])
        return _silu(y)

    def l2n(x):
        return x * lax.rsqrt(jnp.sum(x * x, axis=-1, keepdims=True) + EPS)

    tri_b = tri.astype(BF16)

    def chunk_cumsum(g):
        g1 = g.astype(BF16)
        r1 = g - g1.astype(F32)
        g2 = r1.astype(BF16)
        g3 = (r1 - g2.astype(F32)).astype(BF16)
        return _dot(tri_b, g1) + (_dot(tri_b, g2) + _dot(tri_b, g3))

    def operands(cc, slot):
        for sub in range(GDN_PREP_CHUNKS):
            c = cc * GDN_PREP_CHUNKS + sub
            r0 = pl.multiple_of(c * CHUNK, CHUNK)
            ab = ab_ref[pl.ds(r0, CHUNK), :]
            xg = ab + dtb
            softplus = jnp.maximum(xg, 0.0) + jnp.log(1.0 + jnp.exp(-jnp.abs(xg)))
            g_all = neg_a * softplus
            beta_all = _sigmoid(ab)
            gc_all = chunk_cumsum(g_all)
            gc_t = gc_all.T
            gl_s[c] = jnp.exp(gc_all[CHUNK - 1:CHUNK, :])
            for h in range(GDN_HEADS):
                n = sub * GDN_HEADS + h
                bufs = [cv_s.at[slot, 3 * n + i] for i in range(3)]
                q = l2n(conv_cols(c, r0, h * GDN_DK, bufs[0])) * (GDN_DK ** -0.5)
                k = l2n(conv_cols(c, r0, GDN_QK + h * GDN_DK, bufs[1]))
                v = conv_cols(c, r0, 2 * GDN_QK + h * GDN_DV, bufs[2])
                gcol = gc_all[:, h:h + 1]
                grow = gc_t[h:h + 1, :]
                beta = beta_all[:, GDN_HEADS + h:GDN_HEADS + h + 1]
                dec_s[slot, n] = jnp.where(causal, jnp.exp(gcol - grow), 0.0)
                eg = jnp.exp(gcol)
                kb = k * beta
                q_s[slot, n] = q.astype(BF16)
                k_s[slot, n] = k.astype(BF16)
                kb_s[slot, n] = kb.astype(BF16)
                rhs_s[slot, n] = jnp.concatenate([v * beta, kb * eg], axis=1).astype(BF16)
                qg_o[slot, n] = q * eg
                kd = k * jnp.exp(gc_all[CHUNK - 1:CHUNK, h:h + 1] - gcol)
                kdt_o[slot, n] = kd.T.astype(BF16)

    def solve(cc, slot, between):
        items = [(cc * GDN_PREP_CHUNKS + sub, h, sub * GDN_HEADS + h)
                 for sub in range(GDN_PREP_CHUNKS) for h in range(GDN_HEADS)]
        kks = [_dot_nt(kb_s[slot, n], k_s[slot, n]) for _, _, n in items]
        qks = [_dot_nt(q_s[slot, n], k_s[slot, n]) for _, _, n in items]
        lows = [jnp.where(strict, kk * dec_s[slot, n], 0.0) for kk, (_, _, n) in zip(kks, items)]
        attns = [(qk * dec_s[slot, n]).astype(BF16) for qk, (_, _, n) in zip(qks, items)]
        for step in between[:len(between) // 2]:
            step()
        ts = _unit_lower_inverses(lows, ii, jj)
        for step in between[len(between) // 2:]:
            step()
        uws = [_dot(t.astype(BF16), rhs_s[slot, n]).astype(BF16) for t, (_, _, n) in zip(ts, items)]
        kds = [_dot(kdt_o[slot, n], uw) for uw, (_, _, n) in zip(uws, items)]
        ats = [_dot(attn, uw) for attn, uw in zip(attns, uws)]
        for kd_uw, at_uw, (c, h, n) in zip(kds, ats, items):
            b_s[c, h] = kd_uw[:, :GDN_DV]
            m_s[c, h] = (-kd_uw[:, GDN_DV:]).astype(BF16)
            o0_s[c, h] = at_uw[:, :GDN_DV]
            qp_s[c, h] = (qg_o[slot, n] - at_uw[:, GDN_DV:]).astype(BF16)

    ngroup = nchunk // GDN_PREP_CHUNKS
    per_trip = 2 * GDN_PREP_CHUNKS
    normw = normw_ref[...]
    state_s[...] = jnp.zeros_like(state_s)
    for c0 in range(per_trip):
        gl_s[c0] = jnp.zeros((1, LANES), F32)
        for h in range(GDN_HEADS):
            b_s[c0, h] = jnp.zeros((CHUNK, GDN_DV), F32)
            o0_s[c0, h] = jnp.zeros((CHUNK, GDN_DV), F32)
            m_s[c0, h] = jnp.zeros((CHUNK, GDN_DK), BF16)
            qp_s[c0, h] = jnp.zeros((CHUNK, GDN_DK), BF16)
    operands(0, 0)

    def prep(i, carry):
        first = jnp.maximum(per_trip * (i - 1), 0)
        steps = [functools.partial(scan, first + k, 0) for k in range(per_trip)]
        operands(2 * i + 1, 1)
        solve(2 * i, 0, steps[:per_trip // 2])
        operands(jnp.minimum(2 * i + 2, ngroup - 1), 0)
        solve(2 * i + 1, 1, steps[per_trip // 2:])
        return carry

    def scan(c, carry):
        r0 = pl.multiple_of(c * CHUNK, CHUNK)
        gl = gl_s[c]
        heads = range(GDN_HEADS)
        ss = [state_s[h] for h in heads]
        sbs = [s.astype(BF16) for s in ss]
        mss = [_dot(m_s[c, h], sbs[h]) for h in heads]
        qss = [_dot(qp_s[c, h], sbs[h]) for h in heads]
        for h in heads:
            state_s[h] = ss[h] * gl[:, h:h + 1] + (mss[h] + b_s[c, h])
            o = qss[h] + o0_s[c, h]
            z = z_ref[pl.ds(r0, CHUNK), h * GDN_DV:(h + 1) * GDN_DV].astype(F32)
            on = o * lax.rsqrt(jnp.mean(o * o, axis=-1, keepdims=True) + EPS) * normw
            o_ref[pl.ds(r0, CHUNK), h * GDN_DV:(h + 1) * GDN_DV] = (on * _silu(z)).astype(BF16)
        return carry

    ntrip = ngroup // 2
    lax.fori_loop(0, ntrip, prep, 0)
    lax.fori_loop(per_trip * (ntrip - 1), nchunk, scan, 0)


def _gdn_call(proj, ab, conv_w8, alog_p, dtb_p, normw, batch, seq):
    nchunk = seq // CHUNK
    hs = (nchunk, GDN_HEADS, CHUNK, CHUNK)
    ops = (2, GDN_PREP_CHUNKS * GDN_HEADS, CHUNK, CHUNK)
    return pl.pallas_call(
        functools.partial(_gdn_kernel, seq=seq),
        grid=(batch,),
        in_specs=[
            pl.BlockSpec((seq, 3 * GDN_QK), lambda b: (b, 0)),
            pl.BlockSpec((seq, GDN_V), lambda b: (b, 3)),
            pl.BlockSpec((seq, LANES), lambda b: (b, 0)),
            pl.BlockSpec((8, 3 * GDN_QK), lambda b: (0, 0)),
            pl.BlockSpec((1, LANES), lambda b: (0, 0)),
            pl.BlockSpec((1, LANES), lambda b: (0, 0)),
            pl.BlockSpec((1, GDN_DV), lambda b: (0, 0)),
        ],
        out_specs=pl.BlockSpec((seq, GDN_V), lambda b: (b, 0)),
        out_shape=jax.ShapeDtypeStruct((batch * seq, GDN_V), BF16),
        scratch_shapes=[
            pltpu.VMEM(hs, F32), pltpu.VMEM(hs, F32), pltpu.VMEM(hs, BF16), pltpu.VMEM(hs, BF16),
            pltpu.VMEM((nchunk, 1, LANES), F32),
            pltpu.VMEM((GDN_HEADS, GDN_DK, GDN_DV), F32),
            pltpu.VMEM(ops, BF16), pltpu.VMEM(ops, BF16), pltpu.VMEM(ops, BF16),
            pltpu.VMEM(ops[:3] + (2 * CHUNK,), BF16),
            pltpu.VMEM(ops, F32),
            pltpu.VMEM((2, 3 * ops[1], 8 + CHUNK, LANES), F32),
            pltpu.VMEM(ops, F32), pltpu.VMEM(ops, BF16),
        ],
        compiler_params=pltpu.CompilerParams(
            dimension_semantics=("arbitrary",), vmem_limit_bytes=VMEM_LIMIT),
        name="gdn",
    )(proj, proj, ab, conv_w8, alog_p, dtb_p, normw)


def _ret_kernel(qk_ref, v_ref, g_ref, sin_ref, cos_ref, inner_ref, kdec_ref, qdec_ref, cdec_ref,
                o_ref, state_s, q_s, qd_s, k_s, kt_s, *, seq):
    nchunk = seq // CHUNK
    lane = lax.broadcasted_iota(jnp.int32, (CHUNK, RET_DK), 1)
    even = (lane % 2) == 0

    def rotate(x, sin, cos):
        nxt = pltpu.roll(x, RET_DK - 1, axis=1)
        prv = pltpu.roll(x, 1, axis=1)
        return x * cos + jnp.where(even, -nxt, prv) * sin

    state_s[...] = jnp.zeros_like(state_s)
    kdec = kdec_ref[...]
    qdec = qdec_ref[...]
    cdec = cdec_ref[...]

    heads = range(RET_HEADS)

    def operands(c, slot):
        r0 = pl.multiple_of(c * CHUNK, CHUNK)
        sin = sin_ref[pl.ds(r0, CHUNK), :]
        cos = cos_ref[pl.ds(r0, CHUNK), :]
        for h in heads:
            q = rotate(qk_ref[pl.ds(r0, CHUNK), h * RET_DK:(h + 1) * RET_DK].astype(F32), sin, cos)
            k = rotate(qk_ref[pl.ds(r0, CHUNK), RET_QK + h * RET_DK:RET_QK + (h + 1) * RET_DK]
                       .astype(F32), sin, cos) * (RET_DK ** -0.5)
            q_s[slot, h] = q.astype(BF16)
            qd_s[slot, h] = (q * qdec[:, h:h + 1]).astype(BF16)
            k_s[slot, h] = k.astype(BF16)
            kt_s[slot, h] = (k * kdec[:, h:h + 1]).T.astype(BF16)

    def outputs(c, slot):
        r0 = pl.multiple_of(c * CHUNK, CHUNK)
        vs = [v_ref[pl.ds(r0, CHUNK), h * RET_DV:(h + 1) * RET_DV] for h in heads]
        ss = [state_s[h] for h in heads]
        qks = [_dot_nt(q_s[slot, h], k_s[slot, h]) for h in heads]
        inters = [_dot(qd_s[slot, h], ss[h].astype(BF16)) for h in heads]
        kvs = [_dot(kt_s[slot, h], vs[h]) for h in heads]
        intras = [_dot((qks[h] * inner_ref[h]).astype(BF16), vs[h]) for h in heads]
        for h in heads:
            state_s[h] = ss[h] * cdec[:, h:h + 1] + kvs[h]
            o = intras[h] + inters[h]
            gate = g_ref[pl.ds(r0, CHUNK), h * RET_DV:(h + 1) * RET_DV].astype(F32)
            on = o * lax.rsqrt(jnp.mean(o * o, axis=-1, keepdims=True) + EPS)
            o_ref[pl.ds(r0, CHUNK), h * RET_DV:(h + 1) * RET_DV] = (on * _silu(gate)).astype(BF16)

    operands(0, 0)

    def body(i, carry):
        operands(2 * i + 1, 1)
        outputs(2 * i, 0)
        operands(jnp.minimum(2 * i + 2, nchunk - 1), 0)
        outputs(2 * i + 1, 1)
        return carry

    lax.fori_loop(0, nchunk // 2, body, 0)


def _ret_call(proj, sin, cos, inner, kdec, qdec, cdec, batch, seq):
    return pl.pallas_call(
        functools.partial(_ret_kernel, seq=seq),
        grid=(batch,),
        in_specs=[
            pl.BlockSpec((seq, 2 * RET_QK), lambda b: (b, 2)),
            pl.BlockSpec((seq, RET_V), lambda b: (b, 3)),
            pl.BlockSpec((seq, RET_V), lambda b: (b, 4)),
            pl.BlockSpec((seq, RET_DK), lambda b: (0, 0)),
            pl.BlockSpec((seq, RET_DK), lambda b: (0, 0)),
            pl.BlockSpec((RET_HEADS, CHUNK, CHUNK), lambda b: (0, 0, 0)),
            pl.BlockSpec((CHUNK, LANES), lambda b: (0, 0)),
            pl.BlockSpec((CHUNK, LANES), lambda b: (0, 0)),
            pl.BlockSpec((1, LANES), lambda b: (0, 0)),
        ],
        out_specs=pl.BlockSpec((seq, RET_V), lambda b: (b, 0)),
        out_shape=jax.ShapeDtypeStruct((batch * seq, RET_V), BF16),
        scratch_shapes=[pltpu.VMEM((RET_HEADS, RET_DK, RET_DV), F32)]
        + [pltpu.VMEM((2, RET_HEADS, CHUNK, RET_DK), BF16)] * 4,
        compiler_params=pltpu.CompilerParams(
            dimension_semantics=("arbitrary",), vmem_limit_bytes=VMEM_LIMIT),
        name="retention",
    )(proj, proj, proj, sin, cos, inner, kdec, qdec, cdec)


def _retention_tables(seq):
    inv_freq = 1.0 / (ROPE_BASE ** jnp.linspace(0.0, 1.0, RET_DK // 2, dtype=F32))
    ang = jnp.arange(seq, dtype=F32)[:, None] * inv_freq[None, :]
    sin = jnp.repeat(jnp.sin(ang), 2, axis=-1)
    cos = jnp.repeat(jnp.cos(ang), 2, axis=-1)
    log_gamma = jnp.log(1.0 - 2.0 ** (-5.0 - jnp.arange(RET_HEADS, dtype=F32)))
    idx = jnp.arange(CHUNK, dtype=F32)
    causal = jnp.tril(jnp.ones((CHUNK, CHUNK), dtype=bool))
    rel = jnp.where(causal, idx[:, None] - idx[None, :], 0.0)
    inner = jnp.where(causal, jnp.exp(rel[None] * log_gamma[:, None, None]), 0.0)
    k_decay = jnp.exp(log_gamma[:, None] * (CHUNK - 1.0 - idx)[None, :])
    q_decay = jnp.exp(log_gamma[:, None] * (idx + 1.0)[None, :])
    chunk_decay = jnp.exp(log_gamma * CHUNK)
    pad = LANES - RET_HEADS
    kdec = jnp.pad(k_decay.T, ((0, 0), (0, pad)))
    qdec = jnp.pad(q_decay.T, ((0, 0), (0, pad)))
    cdec = jnp.pad(chunk_decay[None, :], ((0, 0), (0, pad)))
    return sin, cos, inner, kdec, qdec, cdec


def _merge_kernel(x_ref, ya_ref, yb_ref, ma_ref, mb_ref, wa_ref, wr_ref, wo_ref, nw_ref,
                  wrt_ref, br_ref, h_ref, xn_ref, ridx_ref, rw_ref, cnt_ref):
    tm = x_ref.shape[0] // MERGE_PARTS
    rows = [pl.ds(i * tm, tm) for i in range(MERGE_PARTS)]
    a_ = [_dot(ya_ref[r, :], wa_ref[...]) for r in rows]
    r_ = [_dot(yb_ref[r, :], wr_ref[...]) for r in rows]
    merged = [(_sigmoid(ma_ref[r, :].astype(F32)) * a + _sigmoid(mb_ref[r, :].astype(F32)) * rr)
              .astype(BF16) for r, a, rr in zip(rows, a_, r_)]
    hs = [x_ref[r, :] + _dot(mg, wo_ref[...]) for r, mg in zip(rows, merged)]
    xcats = []
    for r, h in zip(rows, hs):
        h_ref[r, :] = h
        xn = h * lax.rsqrt(jnp.mean(h * h, axis=-1, keepdims=True) + EPS) * nw_ref[...]
        xn_ref[r, :] = _pack_bf16_pairs(xn)
        xh = xn.astype(BF16)
        xl = (xn - xh.astype(F32)).astype(BF16)
        xcats.append(jnp.concatenate([xh, xl], axis=0))
    parts_ = [_dot(xc, wrt_ref[...]) for xc in xcats]
    counts = jnp.zeros((1, LANES), F32)
    for r, parts in zip(rows, parts_):
        counts = counts + _route(parts, tm, br_ref[...], ridx_ref.at[r, :], rw_ref.at[r, :])

    @pl.when(pl.program_id(0) == 0)
    def _():
        cnt_ref[...] = jnp.zeros_like(cnt_ref)

    cnt_ref[...] += jnp.broadcast_to(counts, cnt_ref.shape)


def _route(parts, tm, bias, ridx_ref, rw_ref):
    logits = (parts[:tm, :LANES] + (parts[tm:, :LANES] + parts[:tm, LANES:]
                                    + parts[tm:, LANES:])) + bias
    lane = lax.broadcasted_iota(jnp.int32, (tm, LANES), 1)
    neg = -jnp.inf
    gl = jnp.where((lane >= N_EXPERTS) & (lane < N_EXPERTS + N_GROUPS), logits, neg)
    gmax = jnp.max(gl, axis=-1, keepdims=True)
    gidx = jnp.min(jnp.where(gl == gmax, lane, LANES), axis=-1, keepdims=True) - N_EXPERTS
    g_w = 1.0 / jnp.sum(jnp.exp(gl - gmax), axis=-1, keepdims=True)
    el = jnp.where((lane // EXPERTS_PER_GROUP == gidx) & (lane < N_EXPERTS), logits, neg)
    m1 = jnp.max(el, axis=-1, keepdims=True)
    i1 = jnp.min(jnp.where(el == m1, lane, LANES), axis=-1, keepdims=True)
    el2 = jnp.where(lane == i1, neg, el)
    m2 = jnp.max(el2, axis=-1, keepdims=True)
    i2 = jnp.min(jnp.where(el2 == m2, lane, LANES), axis=-1, keepdims=True)
    e2 = jnp.exp(m2 - m1)
    p1 = g_w / (1.0 + e2)
    p2 = g_w * e2 / (1.0 + e2)
    ridx_ref[...] = jnp.where(lane == 0, i1, jnp.where(lane == 1, i2, 0))
    rw_ref[...] = jnp.where(lane == 0, p1, jnp.where(lane == 1, p2, 0.0))
    onehot = jnp.where((lane == i1) | (lane == i2), 1.0, 0.0)
    return jnp.sum(onehot, axis=0, keepdims=True)


def _merge_call(x2, ya, yb, proj, wa, wr, wo, nw, w_router, b_router, tm=1024):
    m = x2.shape[0]
    full = lambda shape: pl.BlockSpec(shape, lambda i: (0, 0))
    return pl.pallas_call(
        _merge_kernel,
        grid=(m // tm,),
        in_specs=[
            pl.BlockSpec((tm, D_MODEL), lambda i: (i, 0)),
            pl.BlockSpec((tm, GDN_V), lambda i: (i, 0)),
            pl.BlockSpec((tm, RET_V), lambda i: (i, 0)),
            pl.BlockSpec((tm, D_MODEL), lambda i: (i, 5)),
            pl.BlockSpec((tm, D_MODEL), lambda i: (i, 6)),
            full((GDN_V, D_MODEL)), full((RET_V, D_MODEL)), full((D_MODEL, D_MODEL)),
            full((1, D_MODEL)),
            full((D_MODEL, 2 * LANES)), full((1, LANES)),
        ],
        out_specs=[
            pl.BlockSpec((tm, D_MODEL), lambda i: (i, 0)),
            pl.BlockSpec((tm, HALF), lambda i: (i, 0)),
            pl.BlockSpec((tm, LANES), lambda i: (i, 0)),
            pl.BlockSpec((tm, LANES), lambda i: (i, 0)),
            pl.BlockSpec((8, LANES), lambda i: (0, 0)),
        ],
        out_shape=[
            jax.ShapeDtypeStruct((m, D_MODEL), F32),
            jax.ShapeDtypeStruct((m, HALF), I32),
            jax.ShapeDtypeStruct((m, LANES), I32),
            jax.ShapeDtypeStruct((m, LANES), F32),
            jax.ShapeDtypeStruct((8, LANES), F32),
        ],
        compiler_params=pltpu.CompilerParams(
            dimension_semantics=("arbitrary",), vmem_limit_bytes=VMEM_LIMIT),
        name="merge_router",
    )(x2, ya, yb, proj, proj, wa, wr, wo, nw, w_router, b_router)


def _slot_counts(m):
    n_slots = TOP_K * m + N_EXPERTS * SLOT_TILE
    return n_slots, n_slots // SLOT_TILE


def _lane_prefix_sum(x, lane):
    s = 1
    while s < LANES:
        x = x + jnp.where(lane >= s, pltpu.roll(x, s, axis=1), 0.0)
        s *= 2
    return x


def _plan_kernel(ridx_ref, cnt_ref, slots_ref, tile_ref, seg_ref, carry_s, off_s):
    i = pl.program_id(0)
    lane = lax.broadcasted_iota(I32, (PLAN_TILE, LANES), 1)
    row = lax.broadcasted_iota(I32, (PLAN_TILE, LANES), 0)
    lane1 = lane[0:1]

    @pl.when(i == 0)
    def _():
        cnt = cnt_ref[0:1, :]
        tile = float(SLOT_TILE)
        padded = jnp.floor((cnt + (tile - 1.0)) / tile) * tile
        incl = _lane_prefix_sum(padded, lane1)
        off = incl - padded
        off_s[...] = off
        carry_s[...] = jnp.zeros_like(carry_s)
        seg_row = lax.broadcasted_iota(I32, (8, LANES), 0)
        seg_ref[...] = jnp.where(seg_row == 0, off / tile, jnp.where(seg_row == 1, padded / tile, 0.0)
                                 ).astype(I32)
        first = (row * SLOT_TILE).astype(F32)
        ended = jnp.where((lane < N_EXPERTS) & (incl <= first), 1.0, 0.0)
        tile_e = jnp.sum(ended, axis=-1, keepdims=True)
        last = jnp.sum(jnp.where(lane.astype(F32) == tile_e, off + cnt, 0.0), axis=-1, keepdims=True)
        used = jnp.clip(last - first[:, 0:1], 0.0, tile)
        tile_ref[...] = jnp.where(lane == 0, tile_e, jnp.where(lane == 1, used, 0.0)).astype(I32)

    strict = jnp.where(row[:, 0:1] > lax.broadcasted_iota(I32, (PLAN_TILE, PLAN_TILE), 1),
                       1.0, 0.0).astype(BF16)
    off = off_s[...]
    carry = carry_s[...]
    for sb in range(PLAN_STEP // PLAN_TILE):
        rows = pl.ds(sb * PLAN_TILE, PLAN_TILE)
        e1 = ridx_ref[rows, 0:1]
        e2 = ridx_ref[rows, 1:2]
        onehot = jnp.where((lane == e1) | (lane == e2), 1.0, 0.0)
        pos = _dot(strict, onehot.astype(BF16)) + (carry + off)
        s1 = jnp.sum(jnp.where(lane == e1, pos, 0.0), axis=-1, keepdims=True)
        s2 = jnp.sum(jnp.where(lane == e2, pos, 0.0), axis=-1, keepdims=True)
        both = jnp.where(lane == 0, s1, jnp.where(lane == 1, s2, 0.0))
        for q in range(PLAN_TILE // LANES):
            t = both[q * LANES:(q + 1) * LANES].T
            c0 = sb * PLAN_TILE + q * LANES
            slots_ref[:, c0:c0 + LANES] = t[0:8].astype(I32)
        carry = carry + jnp.sum(onehot, axis=0, keepdims=True)
    carry_s[...] = carry


def _plan_call(ridx, cnt):
    m = ridx.shape[0]
    _, n_tiles = _slot_counts(m)
    assert n_tiles <= PLAN_TILE and SLOT_TILE == PLAN_TILE
    return pl.pallas_call(
        _plan_kernel,
        grid=(m // PLAN_STEP,),
        in_specs=[pl.BlockSpec((PLAN_STEP, LANES), lambda i: (i, 0)),
                  pl.BlockSpec((8, LANES), lambda i: (0, 0))],
        out_specs=[
            pl.BlockSpec((8, PLAN_STEP), lambda i: (0, i)),
            pl.BlockSpec((PLAN_TILE, LANES), lambda i: (0, 0)),
            pl.BlockSpec((8, LANES), lambda i: (0, 0)),
        ],
        out_shape=[
            jax.ShapeDtypeStruct((8, m), I32),
            jax.ShapeDtypeStruct((PLAN_TILE, LANES), I32),
            jax.ShapeDtypeStruct((8, LANES), I32),
        ],
        scratch_shapes=[pltpu.VMEM((1, LANES), F32), pltpu.VMEM((1, LANES), F32)],
        compiler_params=pltpu.CompilerParams(dimension_semantics=("arbitrary",)),
        name="dispatch_plan",
    )(ridx, cnt)


def _sc_mesh():
    return plsc.VectorSubcoreMesh(core_axis_name="c", subcore_axis_name="s")


def _sc_worker():
    return lax.axis_index("s") * 2 + lax.axis_index("c")


def _sc_dispatch(xn, slot1, slot2, n_rows):
    m = xn.shape[0]
    per = m // SC_WORKERS
    n_pairs = per // (2 * SC_ROWS)

    @functools.partial(
        pl.kernel, mesh=_sc_mesh(),
        out_type=jax.ShapeDtypeStruct((n_rows, HALF), I32),
        scratch_types=[pltpu.VMEM((per // SC_ROWS, SC_ROWS), I32), pltpu.VMEM((per // SC_ROWS, SC_ROWS), I32),
                       pltpu.VMEM((SC_ROWS, HALF), I32), pltpu.VMEM((SC_ROWS, HALF), I32),
                       pltpu.SemaphoreType.DMA, pltpu.SemaphoreType.DMA, pltpu.SemaphoreType.DMA],
        name="sc_dispatch")
    def k(x_hbm, s1_hbm, s2_hbm, o_hbm, i1_v, i2_v, rows0, rows1, sem_r0, sem_r1, sem_w):
        wid = _sc_worker()
        base = wid * per

        def read(chunk, rows_v, sem):
            return pltpu.make_async_copy(x_hbm.at[pl.ds(base + chunk * SC_ROWS, SC_ROWS)], rows_v, sem)

        def scatter(chunk, rows_v):
            c1 = pltpu.async_copy(rows_v, o_hbm.at[i1_v.at[chunk]], sem_w)
            c2 = pltpu.async_copy(rows_v, o_hbm.at[i2_v.at[chunk]], sem_w)
            c1.wait()
            c2.wait()

        read(0, rows0, sem_r0).start()
        pltpu.sync_copy(s1_hbm.at[pl.ds(wid * (per // SC_ROWS), per // SC_ROWS)], i1_v)
        pltpu.sync_copy(s2_hbm.at[pl.ds(wid * (per // SC_ROWS), per // SC_ROWS)], i2_v)

        @pl.loop(0, n_pairs)
        def _(i):
            read(2 * i, rows0, sem_r0).wait()
            read(2 * i + 1, rows1, sem_r1).start()
            scatter(2 * i, rows0)
            read(2 * i + 1, rows1, sem_r1).wait()

            @pl.when(i + 1 < n_pairs)
            def _():
                read(2 * i + 2, rows0, sem_r0).start()

            scatter(2 * i + 1, rows1)

    return k(xn, slot1, slot2)


def _sc_collect(ys, slot1, slot2):
    m = slot1.size
    per = m // SC_WORKERS
    row = jax.ShapeDtypeStruct((m, HALF), I32)

    @functools.partial(
        pl.kernel, mesh=_sc_mesh(), out_type=[row, row],
        scratch_types=[pltpu.VMEM((per // SC_ROWS, SC_ROWS), I32), pltpu.VMEM((per // SC_ROWS, SC_ROWS), I32),
                       pltpu.VMEM((SC_ROWS, HALF), I32), pltpu.VMEM((SC_ROWS, HALF), I32),
                       pltpu.SemaphoreType.DMA, pltpu.SemaphoreType.DMA],
        name="sc_collect")
    def k(y_hbm, s1_hbm, s2_hbm, g1_hbm, g2_hbm, i1_v, i2_v, rows1, rows2, sem_g, sem_w):
        wid = _sc_worker()
        base = wid * per
        pltpu.sync_copy(s1_hbm.at[pl.ds(wid * (per // SC_ROWS), per // SC_ROWS)], i1_v)
        pltpu.sync_copy(s2_hbm.at[pl.ds(wid * (per // SC_ROWS), per // SC_ROWS)], i2_v)

        @pl.loop(0, per // SC_ROWS)
        def _(ci):
            t0 = base + ci * SC_ROWS
            a1 = pltpu.async_copy(y_hbm.at[i1_v.at[ci]], rows1, sem_g)
            a2 = pltpu.async_copy(y_hbm.at[i2_v.at[ci]], rows2, sem_g)
            a1.wait()
            a2.wait()
            w1 = pltpu.async_copy(rows1, g1_hbm.at[pl.ds(t0, SC_ROWS)], sem_w)
            w2 = pltpu.async_copy(rows2, g2_hbm.at[pl.ds(t0, SC_ROWS)], sem_w)
            w1.wait()
            w2.wait()

    return k(ys, slot1, slot2)


def _expert_kernel(first_ref, count_ref, used_ref, xs_hbm, wg_ref, wu_ref, wd_ref, ys_hbm,
                   wg_b, wu_b, wd_b, x_buf, y_buf, sem_x, sem_y, *, n_tiles):
    e = pl.program_id(0)
    first = first_ref[e]
    count = count_ref[e]
    wg_b[...] = wg_ref[0].astype(BF16)
    wu_b[...] = wu_ref[0].astype(BF16)
    wd_b[...] = wd_ref[0].astype(BF16)

    def tile_rows(t):
        return pl.ds(pl.multiple_of((first + t) * SLOT_TILE, SLOT_TILE), SLOT_TILE)

    def load(t, slot):
        return pltpu.make_async_copy(xs_hbm.at[tile_rows(t)], x_buf.at[slot], sem_x.at[slot])

    def store(t, slot):
        return pltpu.make_async_copy(y_buf.at[slot], ys_hbm.at[tile_rows(t)], sem_y.at[slot])

    for t0 in range(2):
        @pl.when(count > t0)
        def _():
            load(t0, t0).start()

    def tile(t, carry):
        slot = t % 2
        xslot = t % 3
        load(t, xslot).wait()

        @pl.when(t + 2 < count)
        def _():
            load(t + 2, (t + 2) % 3).start()

        @pl.when(t >= 2)
        def _():
            store(t - 2, slot).wait()

        half = SLOT_TILE // 2
        used = used_ref[first + t]
        row_id = lax.broadcasted_iota(I32, (half, HALF), 0)
        xs = [_unpack_bf16_pairs(jnp.where(row_id + i * half < used,
                                           x_buf[xslot, i * half:(i + 1) * half, :], 0)).astype(BF16)
              for i in range(2)]
        gs = [_dot(x, wg_b[...]) for x in xs]
        us = [_dot(x, wu_b[...]) for x in xs]
        hids = [(_silu(g) * u).astype(BF16) for g, u in zip(gs, us)]
        ys = [_dot(hid, wd_b[...]) for hid in hids]
        for i, y in enumerate(ys):
            y_buf[slot, i * half:(i + 1) * half, :] = _pack_bf16_pairs(y)
        store(t, slot).start()
        return carry

    lax.fori_loop(0, count, tile, 0)

    @pl.when(count >= 1)
    def _():
        store(count - 1, (count - 1) % 2).wait()

    @pl.when(count >= 2)
    def _():
        store(count - 2, count % 2).wait()

    @pl.when(e == pl.num_programs(0) - 1)
    def _():
        y_buf[0] = jnp.zeros((SLOT_TILE, HALF), I32)

        def zero_tile(t, carry):
            tail = pltpu.make_async_copy(y_buf.at[0], ys_hbm.at[tile_rows(t)], sem_y.at[0])
            tail.start()
            tail.wait()
            return carry

        lax.fori_loop(count, n_tiles - first, zero_tile, 0)


def _expert_call(seg_first, seg_count, tile_used, xs, wg, wu, wd, n_tiles):
    hbm = pl.BlockSpec(memory_space=pl.ANY)
    wmap = lambda e, first, count, used: (e, 0, 0)
    return pl.pallas_call(
        functools.partial(_expert_kernel, n_tiles=n_tiles),
        grid_spec=pltpu.PrefetchScalarGridSpec(
            num_scalar_prefetch=3,
            grid=(N_EXPERTS,),
            in_specs=[hbm,
                      pl.BlockSpec((1, D_MODEL, D_EXPERT), wmap),
                      pl.BlockSpec((1, D_MODEL, D_EXPERT), wmap),
                      pl.BlockSpec((1, D_EXPERT, D_MODEL), wmap)],
            out_specs=hbm,
            scratch_shapes=[
                pltpu.VMEM((D_MODEL, D_EXPERT), BF16), pltpu.VMEM((D_MODEL, D_EXPERT), BF16),
                pltpu.VMEM((D_EXPERT, D_MODEL), BF16),
                pltpu.VMEM((3, SLOT_TILE, HALF), I32), pltpu.VMEM((2, SLOT_TILE, HALF), I32),
                pltpu.SemaphoreType.DMA((3,)), pltpu.SemaphoreType.DMA((2,)),
            ],
        ),
        out_shape=jax.ShapeDtypeStruct((n_tiles * SLOT_TILE, HALF), I32),
        compiler_params=pltpu.CompilerParams(
            dimension_semantics=("arbitrary",), vmem_limit_bytes=VMEM_LIMIT),
        name="experts",
    )(seg_first, seg_count, tile_used, xs, wg, wu, wd)


def _final_kernel(h_ref, g1_ref, g2_ref, rw_ref, nw_ref, o_ref):
    rw = rw_ref[...]
    y = rw[:, 0:1] * _unpack_bf16_pairs(g1_ref[...]) + rw[:, 1:2] * _unpack_bf16_pairs(g2_ref[...])
    h = h_ref[...] + y
    o_ref[...] = h * lax.rsqrt(jnp.mean(h * h, axis=-1, keepdims=True) + EPS) * nw_ref[...]


def _final_call(h1, g1, g2, rw, nw, tm=1024):
    m = h1.shape[0]
    return pl.pallas_call(
        _final_kernel,
        grid=(m // tm,),
        in_specs=[
            pl.BlockSpec((tm, D_MODEL), lambda i: (i, 0)),
            pl.BlockSpec((tm, HALF), lambda i: (i, 0)),
            pl.BlockSpec((tm, HALF), lambda i: (i, 0)),
            pl.BlockSpec((tm, LANES), lambda i: (i, 0)),
            pl.BlockSpec((1, D_MODEL), lambda i: (0, 0)),
        ],
        out_specs=pl.BlockSpec((tm, D_MODEL), lambda i: (i, 0)),
        out_shape=jax.ShapeDtypeStruct((m, D_MODEL), F32),
        compiler_params=pltpu.CompilerParams(dimension_semantics=("arbitrary",)),
        name="combine_final",
    )(h1, g1, g2, rw, nw)


def _pad_lanes(a):
    return jnp.pad(a, ((0, 0), (0, LANES - a.shape[1])))


def kernel(x, norm_mix_w, w_in, conv_w, A_log, dt_bias, gdn_norm_w, w_up_gdn, w_up_ret, w_out,
           norm_ffn_w, w_group, b_group, w_expert, b_expert, w_gate, w_up, w_down, norm_final_w):
    batch, seq, d = x.shape
    m = batch * seq
    h = x.reshape(m, d)
    depth = w_in.shape[0]
    sin, cos, inner, kdec, qdec, cdec = _retention_tables(seq)
    for l in range(depth):
        w_main, w_ab = _repack_call(jnp.transpose(w_in[l]))
        proj, ab = _proj_call(h, norm_mix_w[l][None, :], w_main, w_ab)

        conv8 = jnp.pad(conv_w[l], ((0, 8 - GDN_CONV), (0, 0)))
        ya = _gdn_call(proj, ab, conv8, _pad_lanes(A_log[l][None, :]), _pad_lanes(dt_bias[l][None, :]),
                       gdn_norm_w[l][None, :], batch, seq)
        yb = _ret_call(proj, sin, cos, inner, kdec, qdec, cdec, batch, seq)

        w_router = _pad_lanes(jnp.concatenate([w_expert[l], w_group[l]], axis=1))
        wr_hi = w_router.astype(BF16)
        wr_lo = (w_router - wr_hi.astype(F32)).astype(BF16)
        b_router = _pad_lanes(jnp.concatenate([b_expert[l], b_group[l]])[None, :])
        h1, xn, ridx, rw, cnt = _merge_call(
            h, ya, yb, proj, w_up_gdn[l].astype(BF16), w_up_ret[l].astype(BF16), w_out[l].astype(BF16),
            norm_ffn_w[l][None, :], jnp.concatenate([wr_hi, wr_lo], axis=1), b_router)

        n_slots, n_tiles = _slot_counts(m)
        slots, tiles, seg = _plan_call(ridx, cnt)
        slot1, slot2 = slots[0].reshape(-1, SC_ROWS), slots[1].reshape(-1, SC_ROWS)
        xs = _sc_dispatch(xn, slot1, slot2, n_slots)
        ys = _expert_call(seg[0, :N_EXPERTS], seg[1, :N_EXPERTS], tiles[:n_tiles, 1], xs,
                          w_gate[l], w_up[l], w_down[l], n_tiles)
        g1, g2 = _sc_collect(ys, slot1, slot2)

        assert depth == 1
        h = _final_call(h1, g1, g2, rw, norm_final_w[None, :])
    return h.reshape(batch, seq, d)
```

```python
import functools
import math

import jax
import jax.numpy as jnp
from jax import lax
from jax.experimental import pallas as pl
from jax.experimental.pallas import tpu as pltpu
from jax.experimental.pallas import tpu_sc as plsc

F32 = jnp.float32
BF16 = jnp.bfloat16
I32 = jnp.int32
U32 = jnp.uint32

D_MODEL = 1024
EPS = 1e-6
GDN_HEADS = 4
GDN_DK = 128
GDN_DV = 128
GDN_CONV = 4
RET_HEADS = 4
RET_DK = 128
RET_DV = 256
ROPE_BASE = 10000.0
N_GROUPS = 4
EXPERTS_PER_GROUP = 8
N_EXPERTS = N_GROUPS * EXPERTS_PER_GROUP
D_EXPERT = 512

GDN_QK = GDN_HEADS * GDN_DK
GDN_V = GDN_HEADS * GDN_DV
RET_QK = RET_HEADS * RET_DK
RET_V = RET_HEADS * RET_DV

LANES = 128
CHUNK = 128
INV_BLOCK = 16
GDN_PREP_CHUNKS = 2
VMEM_LIMIT = 56 * 1024 * 1024

REPACK_COLS = 512
MERGE_PARTS = 1
TOP_K = 2
SLOT_TILE = 512
PLAN_TILE = 256
PLAN_STEP = 1024
HALF = D_MODEL // 2
SC_WORKERS = 32
SC_ROWS = 64

PROJ_COLS = 3 * GDN_QK + GDN_V + 2 * RET_QK + 2 * RET_V + 2 * D_MODEL


def _silu(x):
    return x / (1.0 + jnp.exp(-x))


def _sigmoid(x):
    return 1.0 / (1.0 + jnp.exp(-x))


def _dot(a, b):
    return jnp.dot(a, b, preferred_element_type=F32)


def _dot_nt(a, b):
    return lax.dot_general(a, b, (((1,), (1,)), ((), ())), preferred_element_type=F32)


def _pack_bf16_pairs(x):
    bits = lax.bitcast_convert_type(x.astype(BF16).astype(F32), U32)
    packed = (bits[:, :HALF] >> 16) | (bits[:, HALF:] & jnp.uint32(0xFFFF0000))
    return lax.bitcast_convert_type(packed, I32)


def _unpack_bf16_pairs(p):
    p = lax.bitcast_convert_type(p, U32)
    lo = lax.bitcast_convert_type(p << 16, F32)
    hi = lax.bitcast_convert_type(p & jnp.uint32(0xFFFF0000), F32)
    return jnp.concatenate([lo, hi], axis=1)


def _proj_kernel(x_ref, nw_ref, w_ref, wab_ref, proj_ref, ab_ref, u_ref):
    j = pl.program_id(1)

    @pl.when(j == 0)
    def _():
        x = x_ref[...]
        u = x * lax.rsqrt(jnp.mean(x * x, axis=-1, keepdims=True) + EPS) * nw_ref[...]
        ub = u.astype(BF16)
        u_ref[...] = ub
        ab_ref[...] = _dot(ub, wab_ref[...])

    proj_ref[...] = _dot(u_ref[...], w_ref[...]).astype(BF16)


def _repack_kernel(wt_ref, abt_ref, main_ref, ab_ref):
    main_ref[...] = wt_ref[...].T.astype(BF16)

    @pl.when(pl.program_id(0) == 0)
    def _():
        ab = abt_ref[...].T
        ab_ref[...] = jnp.concatenate(
            [ab, jnp.zeros((ab.shape[0], LANES - ab.shape[1]), F32)], axis=1).astype(BF16)


def _repack_call(w_in_t):
    d_in, d = w_in_t.shape
    o_ab = 3 * GDN_QK
    n_ab = 2 * GDN_HEADS
    src = lambda r: pl.multiple_of(r * REPACK_COLS + jnp.where(r * REPACK_COLS >= o_ab, n_ab, 0), 8)
    return pl.pallas_call(
        _repack_kernel,
        grid=(PROJ_COLS // REPACK_COLS,),
        in_specs=[pl.BlockSpec((pl.Element(REPACK_COLS), pl.Element(d)), lambda r: (src(r), 0)),
                  pl.BlockSpec((pl.Element(n_ab), pl.Element(d)), lambda r: (o_ab, 0))],
        out_specs=[pl.BlockSpec((d, REPACK_COLS), lambda r: (0, r)),
                   pl.BlockSpec((d, LANES), lambda r: (0, 0))],
        out_shape=[jax.ShapeDtypeStruct((d, PROJ_COLS), BF16), jax.ShapeDtypeStruct((d, LANES), BF16)],
        compiler_params=pltpu.CompilerParams(dimension_semantics=("arbitrary",)),
        name="repack_w_in",
    )(w_in_t, w_in_t)


def _proj_call(x2, norm_w, w_main, w_ab, tm=1024, tn=3584):
    m = x2.shape[0]
    return pl.pallas_call(
        _proj_kernel,
        grid=(m // tm, PROJ_COLS // tn),
        in_specs=[
            pl.BlockSpec((tm, D_MODEL), lambda i, j: (i, 0)),
            pl.BlockSpec((1, D_MODEL), lambda i, j: (0, 0)),
            pl.BlockSpec((D_MODEL, tn), lambda i, j: (0, j)),
            pl.BlockSpec((D_MODEL, LANES), lambda i, j: (0, 0)),
        ],
        out_specs=[
            pl.BlockSpec((tm, tn), lambda i, j: (i, j)),
            pl.BlockSpec((tm, LANES), lambda i, j: (i, 0)),
        ],
        out_shape=[
            jax.ShapeDtypeStruct((m, PROJ_COLS), BF16),
            jax.ShapeDtypeStruct((m, LANES), F32),
        ],
        scratch_shapes=[pltpu.VMEM((tm, D_MODEL), BF16)],
        compiler_params=pltpu.CompilerParams(
            dimension_semantics=("arbitrary", "arbitrary"), vmem_limit_bytes=VMEM_LIMIT),
        name="proj",
    )(x2, norm_w, w_main, w_ab)


def _unit_lower_inverses(lows, ii, jj):
    eye = jnp.where(ii == jj, 1.0, 0.0).astype(F32)
    in_block = (ii // INV_BLOCK) == (jj // INV_BLOCK)
    ps = [jnp.where(in_block, -low, 0.0) for low in lows]
    ts = [eye + p for p in ps]
    span = 2
    while span < INV_BLOCK:
        ps = [_dot(p, p) for p in ps]
        ts = [t + _dot(t, p) for t, p in zip(ts, ps)]
        span *= 2
    s = INV_BLOCK
    while s < CHUNK:
        off_diag = ((ii // (2 * s)) == (jj // (2 * s))) & ((ii // s) != (jj // s))
        xs = [_dot(jnp.where(off_diag, low, 0.0), t) for low, t in zip(lows, ts)]
        ts = [t - _dot(t, x) for t, x in zip(ts, xs)]
        s *= 2
    return ts


def _gdn_kernel(qkv_ref, z_ref, ab_ref, convw_ref, alog_ref, dtb_ref, normw_ref, o_ref,
                b_s, o0_s, m_s, qp_s, gl_s, state_s, q_s, k_s, kb_s, rhs_s, dec_s, cv_s, qg_o, kdt_o,
                *, seq):
    nchunk = seq // CHUNK
    ii = lax.broadcasted_iota(jnp.int32, (CHUNK, CHUNK), 0)
    jj = lax.broadcasted_iota(jnp.int32, (CHUNK, CHUNK), 1)
    causal = ii >= jj
    strict = ii > jj
    tri = jnp.where(causal, 1.0, 0.0).astype(F32)
    neg_a = -jnp.exp(alog_ref[...])
    dtb = dtb_ref[...]

    def conv_cols(c, r0, lo, buf):
        x = qkv_ref[pl.ds(r0, CHUNK), lo:lo + LANES].astype(F32)
        prev0 = pl.multiple_of(jnp.maximum(r0 - 16, 0), 16)
        prev = qkv_ref[pl.ds(prev0, 16), lo:lo + LANES].astype(F32)
        buf[0:8, :] = prev[8:16] * jnp.where(c > 0, 1.0, 0.0)
        buf[8:8 + CHUNK, :] = x
        w = convw_ref[:, lo:lo + LANES]
        y = (w[3:4] * x + w[2:3] * buf[7:7 + CHUNK, :] + w[1:2] * buf[6:6 + CHUNK, :]
             + w[0:1] * buf[5:5 + CHUNK, :])
        return _silu(y)

    def l2n(x):
        return x * lax.rsqrt(jnp.sum(x * x, axis=-1, keepdims=True) + EPS)

    tri_b = tri.astype(BF16)

    def chunk_cumsum(g):
        g1 = g.astype(BF16)
        r1 = g - g1.astype(F32)
        g2 = r1.astype(BF16)
        g3 = (r1 - g2.astype(F32)).astype(BF16)
        return _dot(tri_b, g1) + (_dot(tri_b, g2) + _dot(tri_b, g3))

    def operands(cc, slot):
        for sub in range(GDN_PREP_CHUNKS):
            c = cc * GDN_PREP_CHUNKS + sub
            r0 = pl.multiple_of(c * CHUNK, CHUNK)
            ab = ab_ref[pl.ds(r0, CHUNK), :]
            xg = ab + dtb
            softplus = jnp.maximum(xg, 0.0) + jnp.log(1.0 + jnp.exp(-jnp.abs(xg)))
            g_all = neg_a * softplus
            beta_all = _sigmoid(ab)
            gc_all = chunk_cumsum(g_all)
            gc_t = gc_all.T
            gl_s[c] = jnp.exp(gc_all[CHUNK - 1:CHUNK, :])
            for h in range(GDN_HEADS):
                n = sub * GDN_HEADS + h
                bufs = [cv_s.at[slot, 3 * n + i] for i in range(3)]
                q = l2n(conv_cols(c, r0, h * GDN_DK, bufs[0])) * (GDN_DK ** -0.5)
                k = l2n(conv_cols(c, r0, GDN_QK + h * GDN_DK, bufs[1]))
                v = conv_cols(c, r0, 2 * GDN_QK + h * GDN_DV, bufs[2])
                gcol = gc_all[:, h:h + 1]
                grow = gc_t[h:h + 1, :]
                beta = beta_all[:, GDN_HEADS + h:GDN_HEADS + h + 1]
                dec_s[slot, n] = jnp.where(causal, jnp.exp(gcol - grow), 0.0)
                eg = jnp.exp(gcol)
                kb = k * beta
                q_s[slot, n] = q.astype(BF16)
                k_s[slot, n] = k.astype(BF16)
                kb_s[slot, n] = kb.astype(BF16)
                rhs_s[slot, n] = jnp.concatenate([v * beta, kb * eg], axis=1).astype(BF16)
                qg_o[slot, n] = q * eg
                kd = k * jnp.exp(gc_all[CHUNK - 1:CHUNK, h:h + 1] - gcol)
                kdt_o[slot, n] = kd.T.astype(BF16)

    def solve(cc, slot, between):
        items = [(cc * GDN_PREP_CHUNKS + sub, h, sub * GDN_HEADS + h)
                 for sub in range(GDN_PREP_CHUNKS) for h in range(GDN_HEADS)]
        kks = [_dot_nt(kb_s[slot, n], k_s[slot, n]) for _, _, n in items]
        qks = [_dot_nt(q_s[slot, n], k_s[slot, n]) for _, _, n in items]
        lows = [jnp.where(strict, kk * dec_s[slot, n], 0.0) for kk, (_, _, n) in zip(kks, items)]
        attns = [(qk * dec_s[slot, n]).astype(BF16) for qk, (_, _, n) in zip(qks, items)]
        for step in between[:len(between) // 2]:
            step()
        ts = _unit_lower_inverses(lows, ii, jj)
        for step in between[len(between) // 2:]:
            step()
        uws = [_dot(t.astype(BF16), rhs_s[slot, n]).astype(BF16) for t, (_, _, n) in zip(ts, items)]
        kds = [_dot(kdt_o[slot, n], uw) for uw, (_, _, n) in zip(uws, items)]
        ats = [_dot(attn, uw) for attn, uw in zip(attns, uws)]
        for kd_uw, at_uw, (c, h, n) in zip(kds, ats, items):
            b_s[c, h] = kd_uw[:, :GDN_DV]
            m_s[c, h] = (-kd_uw[:, GDN_DV:]).astype(BF16)
            o0_s[c, h] = at_uw[:, :GDN_DV]
            qp_s[c, h] = (qg_o[slot, n] - at_uw[:, GDN_DV:]).astype(BF16)

    ngroup = nchunk // GDN_PREP_CHUNKS
    per_trip = 2 * GDN_PREP_CHUNKS
    normw = normw_ref[...]
    state_s[...] = jnp.zeros_like(state_s)
    for c0 in range(per_trip):
        gl_s[c0] = jnp.zeros((1, LANES), F32)
        for h in range(GDN_HEADS):
            b_s[c0, h] = jnp.zeros((CHUNK, GDN_DV), F32)
            o0_s[c0, h] = jnp.zeros((CHUNK, GDN_DV), F32)
            m_s[c0, h] = jnp.zeros((CHUNK, GDN_DK), BF16)
            qp_s[c0, h] = jnp.zeros((CHUNK, GDN_DK), BF16)
    operands(0, 0)

    def prep(i, carry):
        first = jnp.maximum(per_trip * (i - 1), 0)
        steps = [functools.partial(scan, first + k, 0) for k in range(per_trip)]
        operands(2 * i + 1, 1)
        solve(2 * i, 0, steps[:per_trip // 2])
        operands(jnp.minimum(2 * i + 2, ngroup - 1), 0)
        solve(2 * i + 1, 1, steps[per_trip // 2:])
        return carry

    def scan(c, carry):
        r0 = pl.multiple_of(c * CHUNK, CHUNK)
        gl = gl_s[c]
        heads = range(GDN_HEADS)
        ss = [state_s[h] for h in heads]
        sbs = [s.astype(BF16) for s in ss]
        mss = [_dot(m_s[c, h], sbs[h]) for h in heads]
        qss = [_dot(qp_s[c, h], sbs[h]) for h in heads]
        for h in heads:
            state_s[h] = ss[h] * gl[:, h:h + 1] + (mss[h] + b_s[c, h])
            o = qss[h] + o0_s[c, h]
            z = z_ref[pl.ds(r0, CHUNK), h * GDN_DV:(h + 1) * GDN_DV].astype(F32)
            on = o * lax.rsqrt(jnp.mean(o * o, axis=-1, keepdims=True) + EPS) * normw
            o_ref[pl.ds(r0, CHUNK), h * GDN_DV:(h + 1) * GDN_DV] = (on * _silu(z)).astype(BF16)
        return carry

    ntrip = ngroup // 2
    lax.fori_loop(0, ntrip, prep, 0)
    lax.fori_loop(per_trip * (ntrip - 1), nchunk, scan, 0)


def _gdn_call(proj, ab, conv_w8, alog_p, dtb_p, normw, batch, seq):
    nchunk = seq // CHUNK
    hs = (nchunk, GDN_HEADS, CHUNK, CHUNK)
    ops = (2, GDN_PREP_CHUNKS * GDN_HEADS, CHUNK, CHUNK)
    return pl.pallas_call(
        functools.partial(_gdn_kernel, seq=seq),
        grid=(batch,),
        in_specs=[
            pl.BlockSpec((seq, 3 * GDN_QK), lambda b: (b, 0)),
            pl.BlockSpec((seq, GDN_V), lambda b: (b, 3)),
            pl.BlockSpec((seq, LANES), lambda b: (b, 0)),
            pl.BlockSpec((8, 3 * GDN_QK), lambda b: (0, 0)),
            pl.BlockSpec((1, LANES), lambda b: (0, 0)),
            pl.BlockSpec((1, LANES), lambda b: (0, 0)),
            pl.BlockSpec((1, GDN_DV), lambda b: (0, 0)),
        ],
        out_specs=pl.BlockSpec((seq, GDN_V), lambda b: (b, 0)),
        out_shape=jax.ShapeDtypeStruct((batch * seq, GDN_V), BF16),
        scratch_shapes=[
            pltpu.VMEM(hs, F32), pltpu.VMEM(hs, F32), pltpu.VMEM(hs, BF16), pltpu.VMEM(hs, BF16),
            pltpu.VMEM((nchunk, 1, LANES), F32),
            pltpu.VMEM((GDN_HEADS, GDN_DK, GDN_DV), F32),
            pltpu.VMEM(ops, BF16), pltpu.VMEM(ops, BF16), pltpu.VMEM(ops, BF16),
            pltpu.VMEM(ops[:3] + (2 * CHUNK,), BF16),
            pltpu.VMEM(ops, F32),
            pltpu.VMEM((2, 3 * ops[1], 8 + CHUNK, LANES), F32),
            pltpu.VMEM(ops, F32), pltpu.VMEM(ops, BF16),
        ],
        compiler_params=pltpu.CompilerParams(
            dimension_semantics=("arbitrary",), vmem_limit_bytes=VMEM_LIMIT),
        name="gdn",
    )(proj, proj, ab, conv_w8, alog_p, dtb_p, normw)


def _ret_kernel(qk_ref, v_ref, g_ref, sin_ref, cos_ref, inner_ref, kdec_ref, qdec_ref, cdec_ref,
                o_ref, state_s, q_s, qd_s, k_s, kt_s, *, seq):
    nchunk = seq // CHUNK
    lane = lax.broadcasted_iota(jnp.int32, (CHUNK, RET_DK), 1)
    even = (lane % 2) == 0

    def rotate(x, sin, cos):
        nxt = pltpu.roll(x, RET_DK - 1, axis=1)
        prv = pltpu.roll(x, 1, axis=1)
        return x * cos + jnp.where(even, -nxt, prv) * sin

    state_s[...] = jnp.zeros_like(state_s)
    kdec = kdec_ref[...]
    qdec = qdec_ref[...]
    cdec = cdec_ref[...]

    heads = range(RET_HEADS)

    def operands(c, slot):
        r0 = pl.multiple_of(c * CHUNK, CHUNK)
        sin = sin_ref[pl.ds(r0, CHUNK), :]
        cos = cos_ref[pl.ds(r0, CHUNK), :]
        for h in heads:
            q = rotate(qk_ref[pl.ds(r0, CHUNK), h * RET_DK:(h + 1) * RET_DK].astype(F32), sin, cos)
            k = rotate(qk_ref[pl.ds(r0, CHUNK), RET_QK + h * RET_DK:RET_QK + (h + 1) * RET_DK]
                       .astype(F32), sin, cos) * (RET_DK ** -0.5)
            q_s[slot, h] = q.astype(BF16)
            qd_s[slot, h] = (q * qdec[:, h:h + 1]).astype(BF16)
            k_s[slot, h] = k.astype(BF16)
            kt_s[slot, h] = (k * kdec[:, h:h + 1]).T.astype(BF16)

    def outputs(c, slot):
        r0 = pl.multiple_of(c * CHUNK, CHUNK)
        vs = [v_ref[pl.ds(r0, CHUNK), h * RET_DV:(h + 1) * RET_DV] for h in heads]
        ss = [state_s[h] for h in heads]
        qks = [_dot_nt(q_s[slot, h], k_s[slot, h]) for h in heads]
        inters = [_dot(qd_s[slot, h], ss[h].astype(BF16)) for h in heads]
        kvs = [_dot(kt_s[slot, h], vs[h]) for h in heads]
        intras = [_dot((qks[h] * inner_ref[h]).astype(BF16), vs[h]) for h in heads]
        for h in heads:
            state_s[h] = ss[h] * cdec[:, h:h + 1] + kvs[h]
            o = intras[h] + inters[h]
            gate = g_ref[pl.ds(r0, CHUNK), h * RET_DV:(h + 1) * RET_DV].astype(F32)
            on = o * lax.rsqrt(jnp.mean(o * o, axis=-1, keepdims=True) + EPS)
            o_ref[pl.ds(r0, CHUNK), h * RET_DV:(h + 1) * RET_DV] = (on * _silu(gate)).astype(BF16)

    operands(0, 0)

    def body(i, carry):
        operands(2 * i + 1, 1)
        outputs(2 * i, 0)
        operands(jnp.minimum(2 * i + 2, nchunk - 1), 0)
        outputs(2 * i + 1, 1)
        return carry

    lax.fori_loop(0, nchunk // 2, body, 0)


def _ret_call(proj, sin, cos, inner, kdec, qdec, cdec, batch, seq):
    return pl.pallas_call(
        functools.partial(_ret_kernel, seq=seq),
        grid=(batch,),
        in_specs=[
            pl.BlockSpec((seq, 2 * RET_QK), lambda b: (b, 2)),
            pl.BlockSpec((seq, RET_V), lambda b: (b, 3)),
            pl.BlockSpec((seq, RET_V), lambda b: (b, 4)),
            pl.BlockSpec((seq, RET_DK), lambda b: (0, 0)),
            pl.BlockSpec((seq, RET_DK), lambda b: (0, 0)),
            pl.BlockSpec((RET_HEADS, CHUNK, CHUNK), lambda b: (0, 0, 0)),
            pl.BlockSpec((CHUNK, LANES), lambda b: (0, 0)),
            pl.BlockSpec((CHUNK, LANES), lambda b: (0, 0)),
            pl.BlockSpec((1, LANES), lambda b: (0, 0)),
        ],
        out_specs=pl.BlockSpec((seq, RET_V), lambda b: (b, 0)),
        out_shape=jax.ShapeDtypeStruct((batch * seq, RET_V), BF16),
        scratch_shapes=[pltpu.VMEM((RET_HEADS, RET_DK, RET_DV), F32)]
        + [pltpu.VMEM((2, RET_HEADS, CHUNK, RET_DK), BF16)] * 4,
        compiler_params=pltpu.CompilerParams(
            dimension_semantics=("arbitrary",), vmem_limit_bytes=VMEM_LIMIT),
        name="retention",
    )(proj, proj, proj, sin, cos, inner, kdec, qdec, cdec)


def _retention_tables(seq):
    inv_freq = 1.0 / (ROPE_BASE ** jnp.linspace(0.0, 1.0, RET_DK // 2, dtype=F32))
    ang = jnp.arange(seq, dtype=F32)[:, None] * inv_freq[None, :]
    sin = jnp.repeat(jnp.sin(ang), 2, axis=-1)
    cos = jnp.repeat(jnp.cos(ang), 2, axis=-1)
    log_gamma = jnp.log(1.0 - 2.0 ** (-5.0 - jnp.arange(RET_HEADS, dtype=F32)))
    idx = jnp.arange(CHUNK, dtype=F32)
    causal = jnp.tril(jnp.ones((CHUNK, CHUNK), dtype=bool))
    rel = jnp.where(causal, idx[:, None] - idx[None, :], 0.0)
    inner = jnp.where(causal, jnp.exp(rel[None] * log_gamma[:, None, None]), 0.0)
    k_decay = jnp.exp(log_gamma[:, None] * (CHUNK - 1.0 - idx)[None, :])
    q_decay = jnp.exp(log_gamma[:, None] * (idx + 1.0)[None, :])
    chunk_decay = jnp.exp(log_gamma * CHUNK)
    pad = LANES - RET_HEADS
    kdec = jnp.pad(k_decay.T, ((0, 0), (0, pad)))
    qdec = jnp.pad(q_decay.T, ((0, 0), (0, pad)))
    cdec = jnp.pad(chunk_decay[None, :], ((0, 0), (0, pad)))
    return sin, cos, inner, kdec, qdec, cdec


def _merge_kernel(x_ref, ya_ref, yb_ref, ma_ref, mb_ref, wa_ref, wr_ref, wo_ref, nw_ref,
                  wrt_ref, br_ref, h_ref, xn_ref, ridx_ref, rw_ref, cnt_ref):
    tm = x_ref.shape[0] // MERGE_PARTS
    rows = [pl.ds(i * tm, tm) for i in range(MERGE_PARTS)]
    a_ = [_dot(ya_ref[r, :], wa_ref[...]) for r in rows]
    r_ = [_dot(yb_ref[r, :], wr_ref[...]) for r in rows]
    merged = [(_sigmoid(ma_ref[r, :].astype(F32)) * a + _sigmoid(mb_ref[r, :].astype(F32)) * rr)
              .astype(BF16) for r, a, rr in zip(rows, a_, r_)]
    hs = [x_ref[r, :] + _dot(mg, wo_ref[...]) for r, mg in zip(rows, merged)]
    xcats = []
    for r, h in zip(rows, hs):
        h_ref[r, :] = h
        xn = h * lax.rsqrt(jnp.mean(h * h, axis=-1, keepdims=True) + EPS) * nw_ref[...]
        xn_ref[r, :] = _pack_bf16_pairs(xn)
        xh = xn.astype(BF16)
        xl = (xn - xh.astype(F32)).astype(BF16)
        xcats.append(jnp.concatenate([xh, xl], axis=0))
    parts_ = [_dot(xc, wrt_ref[...]) for xc in xcats]
    counts = jnp.zeros((1, LANES), F32)
    for r, parts in zip(rows, parts_):
        counts = counts + _route(parts, tm, br_ref[...], ridx_ref.at[r, :], rw_ref.at[r, :])

    @pl.when(pl.program_id(0) == 0)
    def _():
        cnt_ref[...] = jnp.zeros_like(cnt_ref)

    cnt_ref[...] += jnp.broadcast_to(counts, cnt_ref.shape)


def _route(parts, tm, bias, ridx_ref, rw_ref):
    logits = (parts[:tm, :LANES] + (parts[tm:, :LANES] + parts[:tm, LANES:]
                                    + parts[tm:, LANES:])) + bias
    lane = lax.broadcasted_iota(jnp.int32, (tm, LANES), 1)
    neg = -jnp.inf
    gl = jnp.where((lane >= N_EXPERTS) & (lane < N_EXPERTS + N_GROUPS), logits, neg)
    gmax = jnp.max(gl, axis=-1, keepdims=True)
    gidx = jnp.min(jnp.where(gl == gmax, lane, LANES), axis=-1, keepdims=True) - N_EXPERTS
    g_w = 1.0 / jnp.sum(jnp.exp(gl - gmax), axis=-1, keepdims=True)
    el = jnp.where((lane // EXPERTS_PER_GROUP == gidx) & (lane < N_EXPERTS), logits, neg)
    m1 = jnp.max(el, axis=-1, keepdims=True)
    i1 = jnp.min(jnp.where(el == m1, lane, LANES), axis=-1, keepdims=True)
    el2 = jnp.where(lane == i1, neg, el)
    m2 = jnp.max(el2, axis=-1, keepdims=True)
    i2 = jnp.min(jnp.where(el2 == m2, lane, LANES), axis=-1, keepdims=True)
    e2 = jnp.exp(m2 - m1)
    p1 = g_w / (1.0 + e2)
    p2 = g_w * e2 / (1.0 + e2)
    ridx_ref[...] = jnp.where(lane == 0, i1, jnp.where(lane == 1, i2, 0))
    rw_ref[...] = jnp.where(lane == 0, p1, jnp.where(lane == 1, p2, 0.0))
    onehot = jnp.where((lane == i1) | (lane == i2), 1.0, 0.0)
    return jnp.sum(onehot, axis=0, keepdims=True)


def _merge_call(x2, ya, yb, proj, wa, wr, wo, nw, w_router, b_router, tm=1024):
    m = x2.shape[0]
    full = lambda shape: pl.BlockSpec(shape, lambda i: (0, 0))
    return pl.pallas_call(
        _merge_kernel,
        grid=(m // tm,),
        in_specs=[
            pl.BlockSpec((tm, D_MODEL), lambda i: (i, 0)),
            pl.BlockSpec((tm, GDN_V), lambda i: (i, 0)),
            pl.BlockSpec((tm, RET_V), lambda i: (i, 0)),
            pl.BlockSpec((tm, D_MODEL), lambda i: (i, 5)),
            pl.BlockSpec((tm, D_MODEL), lambda i: (i, 6)),
            full((GDN_V, D_MODEL)), full((RET_V, D_MODEL)), full((D_MODEL, D_MODEL)),
            full((1, D_MODEL)),
            full((D_MODEL, 2 * LANES)), full((1, LANES)),
        ],
        out_specs=[
            pl.BlockSpec((tm, D_MODEL), lambda i: (i, 0)),
            pl.BlockSpec((tm, HALF), lambda i: (i, 0)),
            pl.BlockSpec((tm, LANES), lambda i: (i, 0)),
            pl.BlockSpec((tm, LANES), lambda i: (i, 0)),
            pl.BlockSpec((8, LANES), lambda i: (0, 0)),
        ],
        out_shape=[
            jax.ShapeDtypeStruct((m, D_MODEL), F32),
            jax.ShapeDtypeStruct((m, HALF), I32),
            jax.ShapeDtypeStruct((m, LANES), I32),
            jax.ShapeDtypeStruct((m, LANES), F32),
            jax.ShapeDtypeStruct((8, LANES), F32),
        ],
        compiler_params=pltpu.CompilerParams(
            dimension_semantics=("arbitrary",), vmem_limit_bytes=VMEM_LIMIT),
        name="merge_router",
    )(x2, ya, yb, proj, proj, wa, wr, wo, nw, w_router, b_router)


def _slot_counts(m):
    n_slots = TOP_K * m + N_EXPERTS * SLOT_TILE
    return n_slots, n_slots // SLOT_TILE


def _lane_prefix_sum(x, lane):
    s = 1
    while s < LANES:
        x = x + jnp.where(lane >= s, pltpu.roll(x, s, axis=1), 0.0)
        s *= 2
    return x


def _plan_kernel(ridx_ref, cnt_ref, slots_ref, tile_ref, carry_s, off_s):
    i = pl.program_id(0)
    lane = lax.broadcasted_iota(I32, (PLAN_TILE, LANES), 1)
    row = lax.broadcasted_iota(I32, (PLAN_TILE, LANES), 0)
    lane1 = lane[0:1]

    @pl.when(i == 0)
    def _():
        cnt = cnt_ref[0:1, :]
        tile = float(SLOT_TILE)
        padded = jnp.floor((cnt + (tile - 1.0)) / tile) * tile
        incl = _lane_prefix_sum(padded, lane1)
        off = incl - padded
        off_s[...] = off
        carry_s[...] = jnp.zeros_like(carry_s)
        first = (row * SLOT_TILE).astype(F32)
        ended = jnp.where((lane < N_EXPERTS) & (incl <= first), 1.0, 0.0)
        tile_e = jnp.sum(ended, axis=-1, keepdims=True)
        last = jnp.sum(jnp.where(lane.astype(F32) == tile_e, off + cnt, 0.0), axis=-1, keepdims=True)
        used = jnp.clip(last - first[:, 0:1], 0.0, tile)
        tile_ref[...] = jnp.where(lane == 0, tile_e, jnp.where(lane == 1, used, 0.0)).astype(I32)

    strict = jnp.where(row[:, 0:1] > lax.broadcasted_iota(I32, (PLAN_TILE, PLAN_TILE), 1),
                       1.0, 0.0).astype(BF16)
    off = off_s[...]
    carry = carry_s[...]
    for sb in range(PLAN_STEP // PLAN_TILE):
        rows = pl.ds(sb * PLAN_TILE, PLAN_TILE)
        e1 = ridx_ref[rows, 0:1]
        e2 = ridx_ref[rows, 1:2]
        onehot = jnp.where((lane == e1) | (lane == e2), 1.0, 0.0)
        pos = _dot(strict, onehot.astype(BF16)) + (carry + off)
        s1 = jnp.sum(jnp.where(lane == e1, pos, 0.0), axis=-1, keepdims=True)
        s2 = jnp.sum(jnp.where(lane == e2, pos, 0.0), axis=-1, keepdims=True)
        both = jnp.where(lane == 0, s1, jnp.where(lane == 1, s2, 0.0))
        for q in range(PLAN_TILE // LANES):
            t = both[q * LANES:(q + 1) * LANES].T
            c0 = sb * PLAN_TILE + q * LANES
            slots_ref[:, c0:c0 + LANES] = t[0:8].astype(I32)
        carry = carry + jnp.sum(onehot, axis=0, keepdims=True)
    carry_s[...] = carry


def _plan_call(ridx, cnt):
    m = ridx.shape[0]
    _, n_tiles = _slot_counts(m)
    assert n_tiles <= PLAN_TILE
    return pl.pallas_call(
        _plan_kernel,
        grid=(m // PLAN_STEP,),
        in_specs=[pl.BlockSpec((PLAN_STEP, LANES), lambda i: (i, 0)),
                  pl.BlockSpec((8, LANES), lambda i: (0, 0))],
        out_specs=[
            pl.BlockSpec((8, PLAN_STEP), lambda i: (0, i)),
            pl.BlockSpec((PLAN_TILE, LANES), lambda i: (0, 0)),
        ],
        out_shape=[
            jax.ShapeDtypeStruct((8, m), I32),
            jax.ShapeDtypeStruct((PLAN_TILE, LANES), I32),
        ],
        scratch_shapes=[pltpu.VMEM((1, LANES), F32), pltpu.VMEM((1, LANES), F32)],
        compiler_params=pltpu.CompilerParams(dimension_semantics=("arbitrary",)),
        name="dispatch_plan",
    )(ridx, cnt)


def _sc_mesh():
    return plsc.VectorSubcoreMesh(core_axis_name="c", subcore_axis_name="s")


def _sc_worker():
    return lax.axis_index("s") * 2 + lax.axis_index("c")


def _sc_dispatch(xn, slot1, slot2, n_rows):
    m = xn.shape[0]
    per = m // SC_WORKERS
    n_pairs = per // (2 * SC_ROWS)

    @functools.partial(
        pl.kernel, mesh=_sc_mesh(),
        out_type=jax.ShapeDtypeStruct((n_rows, HALF), I32),
        scratch_types=[pltpu.VMEM((per // SC_ROWS, SC_ROWS), I32), pltpu.VMEM((per // SC_ROWS, SC_ROWS), I32),
                       pltpu.VMEM((SC_ROWS, HALF), I32), pltpu.VMEM((SC_ROWS, HALF), I32),
                       pltpu.SemaphoreType.DMA, pltpu.SemaphoreType.DMA, pltpu.SemaphoreType.DMA],
        name="sc_dispatch")
    def k(x_hbm, s1_hbm, s2_hbm, o_hbm, i1_v, i2_v, rows0, rows1, sem_r0, sem_r1, sem_w):
        wid = _sc_worker()
        base = wid * per

        def read(chunk, rows_v, sem):
            return pltpu.make_async_copy(x_hbm.at[pl.ds(base + chunk * SC_ROWS, SC_ROWS)], rows_v, sem)

        def scatter(chunk, rows_v):
            c1 = pltpu.async_copy(rows_v, o_hbm.at[i1_v.at[chunk]], sem_w)
            c2 = pltpu.async_copy(rows_v, o_hbm.at[i2_v.at[chunk]], sem_w)
            c1.wait()
            c2.wait()

        read(0, rows0, sem_r0).start()
        pltpu.sync_copy(s1_hbm.at[pl.ds(wid * (per // SC_ROWS), per // SC_ROWS)], i1_v)
        pltpu.sync_copy(s2_hbm.at[pl.ds(wid * (per // SC_ROWS), per // SC_ROWS)], i2_v)

        @pl.loop(0, n_pairs)
        def _(i):
            read(2 * i, rows0, sem_r0).wait()
            read(2 * i + 1, rows1, sem_r1).start()
            scatter(2 * i, rows0)
            read(2 * i + 1, rows1, sem_r1).wait()

            @pl.when(i + 1 < n_pairs)
            def _():
                read(2 * i + 2, rows0, sem_r0).start()

            scatter(2 * i + 1, rows1)

    return k(xn, slot1, slot2)


def _sc_collect(ys, slot1, slot2):
    m = slot1.size
    per = m // SC_WORKERS
    row = jax.ShapeDtypeStruct((m, HALF), I32)

    @functools.partial(
        pl.kernel, mesh=_sc_mesh(), out_type=[row, row],
        scratch_types=[pltpu.VMEM((per // SC_ROWS, SC_ROWS), I32), pltpu.VMEM((per // SC_ROWS, SC_ROWS), I32),
                       pltpu.VMEM((SC_ROWS, HALF), I32), pltpu.VMEM((SC_ROWS, HALF), I32),
                       pltpu.SemaphoreType.DMA, pltpu.SemaphoreType.DMA],
        name="sc_collect")
    def k(y_hbm, s1_hbm, s2_hbm, g1_hbm, g2_hbm, i1_v, i2_v, rows1, rows2, sem_g, sem_w):
        wid = _sc_worker()
        base = wid * per
        pltpu.sync_copy(s1_hbm.at[pl.ds(wid * (per // SC_ROWS), per // SC_ROWS)], i1_v)
        pltpu.sync_copy(s2_hbm.at[pl.ds(wid * (per // SC_ROWS), per // SC_ROWS)], i2_v)

        @pl.loop(0, per // SC_ROWS)
        def _(ci):
            t0 = base + ci * SC_ROWS
            a1 = pltpu.async_copy(y_hbm.at[i1_v.at[ci]], rows1, sem_g)
            a2 = pltpu.async_copy(y_hbm.at[i2_v.at[ci]], rows2, sem_g)
            a1.wait()
            a2.wait()
            w1 = pltpu.async_copy(rows1, g1_hbm.at[pl.ds(t0, SC_ROWS)], sem_w)
            w2 = pltpu.async_copy(rows2, g2_hbm.at[pl.ds(t0, SC_ROWS)], sem_w)
            w1.wait()
            w2.wait()

    return k(ys, slot1, slot2)


def _expert_kernel(te_ref, used_ref, xs_ref, wg_hbm, wu_hbm, wd_hbm, ys_ref, wg_b, wu_b, wd_b,
                   wg_f, wu_f, wd_f, sem, slot_s):
    j = pl.program_id(0)
    n = pl.num_programs(0)
    e = te_ref[j]
    prev = te_ref[jnp.maximum(j - 1, 0)]
    valid = e < N_EXPERTS

    def weight_copies(expert, slot):
        return [pltpu.make_async_copy(hbm.at[expert], buf.at[slot], sem.at[slot, i])
                for i, (hbm, buf) in enumerate(((wg_hbm, wg_f), (wu_hbm, wu_f), (wd_hbm, wd_f)))]

    @pl.when((j == 0) & valid)
    def _():
        slot_s[0] = 0
        for c in weight_copies(e, 0):
            c.start()

    @pl.when(((j == 0) | (e != prev)) & valid)
    def _():
        slot = slot_s[0]
        for c in weight_copies(e, slot):
            c.wait()
        k = lax.while_loop(lambda k: (k < n) & (te_ref[jnp.minimum(k, n - 1)] == e),
                           lambda k: k + 1, j + 1)
        nxt = te_ref[jnp.minimum(k, n - 1)]

        @pl.when((k < n) & (nxt < N_EXPERTS))
        def _():
            for c in weight_copies(nxt, 1 - slot):
                c.start()

        wg_b[...] = wg_f[slot].astype(BF16)
        wu_b[...] = wu_f[slot].astype(BF16)
        wd_b[...] = wd_f[slot].astype(BF16)
        slot_s[0] = 1 - slot

    @pl.when(valid)
    def _():
        half = SLOT_TILE // 2
        rows = [pl.ds(i * half, half) for i in range(2)]
        row_id = lax.broadcasted_iota(I32, (half, HALF), 0)
        xs = [_unpack_bf16_pairs(jnp.where(row_id + i * half < used_ref[j], xs_ref[r, :], 0))
              .astype(BF16) for i, r in enumerate(rows)]
        gs = [_dot(x, wg_b[...]) for x in xs]
        us = [_dot(x, wu_b[...]) for x in xs]
        hids = [(_silu(g) * u).astype(BF16) for g, u in zip(gs, us)]
        ys = [_dot(hid, wd_b[...]) for hid in hids]
        for r, y in zip(rows, ys):
            ys_ref[r, :] = _pack_bf16_pairs(y)

    @pl.when(e >= N_EXPERTS)
    def _():
        ys_ref[...] = jnp.zeros_like(ys_ref)


def _expert_call(tile_expert, tile_used, xs, wg, wu, wd, n_tiles):
    hbm = pl.BlockSpec(memory_space=pl.ANY)
    return pl.pallas_call(
        _expert_kernel,
        grid_spec=pltpu.PrefetchScalarGridSpec(
            num_scalar_prefetch=2,
            grid=(n_tiles,),
            in_specs=[pl.BlockSpec((SLOT_TILE, HALF), lambda j, te, used: (j, 0)), hbm, hbm, hbm],
            out_specs=pl.BlockSpec((SLOT_TILE, HALF), lambda j, te, used: (j, 0)),
            scratch_shapes=[
                pltpu.VMEM((D_MODEL, D_EXPERT), BF16), pltpu.VMEM((D_MODEL, D_EXPERT), BF16),
                pltpu.VMEM((D_EXPERT, D_MODEL), BF16),
                pltpu.VMEM((2, D_MODEL, D_EXPERT), F32), pltpu.VMEM((2, D_MODEL, D_EXPERT), F32),
                pltpu.VMEM((2, D_EXPERT, D_MODEL), F32),
                pltpu.SemaphoreType.DMA((2, 3)),
                pltpu.SMEM((1,), I32),
            ],
        ),
        out_shape=jax.ShapeDtypeStruct((n_tiles * SLOT_TILE, HALF), I32),
        compiler_params=pltpu.CompilerParams(
            dimension_semantics=("arbitrary",), vmem_limit_bytes=VMEM_LIMIT),
        name="experts",
    )(tile_expert, tile_used, xs, wg, wu, wd)


def _final_kernel(h_ref, g1_ref, g2_ref, rw_ref, nw_ref, o_ref):
    rw = rw_ref[...]
    y = rw[:, 0:1] * _unpack_bf16_pairs(g1_ref[...]) + rw[:, 1:2] * _unpack_bf16_pairs(g2_ref[...])
    h = h_ref[...] + y
    o_ref[...] = h * lax.rsqrt(jnp.mean(h * h, axis=-1, keepdims=True) + EPS) * nw_ref[...]


def _final_call(h1, g1, g2, rw, nw, tm=1024):
    m = h1.shape[0]
    return pl.pallas_call(
        _final_kernel,
        grid=(m // tm,),
        in_specs=[
            pl.BlockSpec((tm, D_MODEL), lambda i: (i, 0)),
            pl.BlockSpec((tm, HALF), lambda i: (i, 0)),
            pl.BlockSpec((tm, HALF), lambda i: (i, 0)),
            pl.BlockSpec((tm, LANES), lambda i: (i, 0)),
            pl.BlockSpec((1, D_MODEL), lambda i: (0, 0)),
        ],
        out_specs=pl.BlockSpec((tm, D_MODEL), lambda i: (i, 0)),
        out_shape=jax.ShapeDtypeStruct((m, D_MODEL), F32),
        compiler_params=pltpu.CompilerParams(dimension_semantics=("arbitrary",)),
        name="combine_final",
    )(h1, g1, g2, rw, nw)


def _pad_lanes(a):
    return jnp.pad(a, ((0, 0), (0, LANES - a.shape[1])))


def kernel(x, norm_mix_w, w_in, conv_w, A_log, dt_bias, gdn_norm_w, w_up_gdn, w_up_ret, w_out,
           norm_ffn_w, w_group, b_group, w_expert, b_expert, w_gate, w_up, w_down, norm_final_w):
    batch, seq, d = x.shape
    m = batch * seq
    h = x.reshape(m, d)
    depth = w_in.shape[0]
    sin, cos, inner, kdec, qdec, cdec = _retention_tables(seq)
    for l in range(depth):
        w_main, w_ab = _repack_call(jnp.transpose(w_in[l]))
        proj, ab = _proj_call(h, norm_mix_w[l][None, :], w_main, w_ab)

        conv8 = jnp.pad(conv_w[l], ((0, 8 - GDN_CONV), (0, 0)))
        ya = _gdn_call(proj, ab, conv8, _pad_lanes(A_log[l][None, :]), _pad_lanes(dt_bias[l][None, :]),
                       gdn_norm_w[l][None, :], batch, seq)
        yb = _ret_call(proj, sin, cos, inner, kdec, qdec, cdec, batch, seq)

        w_router = _pad_lanes(jnp.concatenate([w_expert[l], w_group[l]], axis=1))
        wr_hi = w_router.astype(BF16)
        wr_lo = (w_router - wr_hi.astype(F32)).astype(BF16)
        b_router = _pad_lanes(jnp.concatenate([b_expert[l], b_group[l]])[None, :])
        h1, xn, ridx, rw, cnt = _merge_call(
            h, ya, yb, proj, w_up_gdn[l].astype(BF16), w_up_ret[l].astype(BF16), w_out[l].astype(BF16),
            norm_ffn_w[l][None, :], jnp.concatenate([wr_hi, wr_lo], axis=1), b_router)

        n_slots, n_tiles = _slot_counts(m)
        slots, tiles = _plan_call(ridx, cnt)
        slot1, slot2 = slots[0].reshape(-1, SC_ROWS), slots[1].reshape(-1, SC_ROWS)
        xs = _sc_dispatch(xn, slot1, slot2, n_slots)
        ys = _expert_call(tiles[:n_tiles, 0], tiles[:n_tiles, 1], xs,
                          w_gate[l], w_up[l], w_down[l], n_tiles)
        g1, g2 = _sc_collect(ys, slot1, slot2)

        assert depth == 1
        h = _final_call(h1, g1, g2, rw, norm_final_w[None, :])
    return h.reshape(batch, seq, d)
```

```python
import functools
import math

import jax
import jax.numpy as jnp
from jax import lax
from jax.experimental import pallas as pl
from jax.experimental.pallas import tpu as pltpu
from jax.experimental.pallas import tpu_sc as plsc

F32 = jnp.float32
BF16 = jnp.bfloat16
I32 = jnp.int32
U32 = jnp.uint32

D_MODEL = 1024
EPS = 1e-6
GDN_HEADS = 4
GDN_DK = 128
GDN_DV = 128
GDN_CONV = 4
RET_HEADS = 4
RET_DK = 128
RET_DV = 256
ROPE_BASE = 10000.0
N_GROUPS = 4
EXPERTS_PER_GROUP = 8
N_EXPERTS = N_GROUPS * EXPERTS_PER_GROUP
D_EXPERT = 512

GDN_QK = GDN_HEADS * GDN_DK
GDN_V = GDN_HEADS * GDN_DV
RET_QK = RET_HEADS * RET_DK
RET_V = RET_HEADS * RET_DV

LANES = 128
CHUNK = 128
INV_BLOCK = 16
GDN_PREP_CHUNKS = 2
VMEM_LIMIT = 56 * 1024 * 1024

REPACK_COLS = 512
MERGE_PARTS = 1
TOP_K = 2
SLOT_TILE = 512
PLAN_TILE = 256
PLAN_STEP = 1024
HALF = D_MODEL // 2
SC_WORKERS = 32
SC_ROWS = 64

PROJ_COLS = 3 * GDN_QK + GDN_V + 2 * RET_QK + 2 * RET_V + 2 * D_MODEL


def _silu(x):
    return x / (1.0 + jnp.exp(-x))


def _sigmoid(x):
    return 1.0 / (1.0 + jnp.exp(-x))


def _dot(a, b):
    return jnp.dot(a, b, preferred_element_type=F32)


def _dot_nt(a, b):
    return lax.dot_general(a, b, (((1,), (1,)), ((), ())), preferred_element_type=F32)


def _pack_bf16_pairs(x):
    bits = lax.bitcast_convert_type(x.astype(BF16).astype(F32), U32)
    packed = (bits[:, :HALF] >> 16) | (bits[:, HALF:] & jnp.uint32(0xFFFF0000))
    return lax.bitcast_convert_type(packed, I32)


def _unpack_bf16_pairs(p):
    p = lax.bitcast_convert_type(p, U32)
    lo = lax.bitcast_convert_type(p << 16, F32)
    hi = lax.bitcast_convert_type(p & jnp.uint32(0xFFFF0000), F32)
    return jnp.concatenate([lo, hi], axis=1)


def _proj_kernel(x_ref, nw_ref, w_ref, wab_ref, proj_ref, ab_ref, u_ref):
    j = pl.program_id(1)

    @pl.when(j == 0)
    def _():
        x = x_ref[...]
        u = x * lax.rsqrt(jnp.mean(x * x, axis=-1, keepdims=True) + EPS) * nw_ref[...]
        ub = u.astype(BF16)
        u_ref[...] = ub
        ab_ref[...] = _dot(ub, wab_ref[...])

    proj_ref[...] = _dot(u_ref[...], w_ref[...]).astype(BF16)


def _repack_kernel(wt_ref, abt_ref, main_ref, ab_ref):
    main_ref[...] = wt_ref[...].T.astype(BF16)

    @pl.when(pl.program_id(0) == 0)
    def _():
        ab = abt_ref[...].T
        ab_ref[...] = jnp.concatenate(
            [ab, jnp.zeros((ab.shape[0], LANES - ab.shape[1]), F32)], axis=1).astype(BF16)


def _repack_call(w_in_t):
    d_in, d = w_in_t.shape
    o_ab = 3 * GDN_QK
    n_ab = 2 * GDN_HEADS
    src = lambda r: pl.multiple_of(r * REPACK_COLS + jnp.where(r * REPACK_COLS >= o_ab, n_ab, 0), 8)
    return pl.pallas_call(
        _repack_kernel,
        grid=(PROJ_COLS // REPACK_COLS,),
        in_specs=[pl.BlockSpec((pl.Element(REPACK_COLS), pl.Element(d)), lambda r: (src(r), 0)),
                  pl.BlockSpec((pl.Element(n_ab), pl.Element(d)), lambda r: (o_ab, 0))],
        out_specs=[pl.BlockSpec((d, REPACK_COLS), lambda r: (0, r)),
                   pl.BlockSpec((d, LANES), lambda r: (0, 0))],
        out_shape=[jax.ShapeDtypeStruct((d, PROJ_COLS), BF16), jax.ShapeDtypeStruct((d, LANES), BF16)],
        compiler_params=pltpu.CompilerParams(dimension_semantics=("arbitrary",)),
        name="repack_w_in",
    )(w_in_t, w_in_t)


def _proj_call(x2, norm_w, w_main, w_ab, tm=1024, tn=3584):
    m = x2.shape[0]
    return pl.pallas_call(
        _proj_kernel,
        grid=(m // tm, PROJ_COLS // tn),
        in_specs=[
            pl.BlockSpec((tm, D_MODEL), lambda i, j: (i, 0)),
            pl.BlockSpec((1, D_MODEL), lambda i, j: (0, 0)),
            pl.BlockSpec((D_MODEL, tn), lambda i, j: (0, j)),
            pl.BlockSpec((D_MODEL, LANES), lambda i, j: (0, 0)),
        ],
        out_specs=[
            pl.BlockSpec((tm, tn), lambda i, j: (i, j)),
            pl.BlockSpec((tm, LANES), lambda i, j: (i, 0)),
        ],
        out_shape=[
            jax.ShapeDtypeStruct((m, PROJ_COLS), BF16),
            jax.ShapeDtypeStruct((m, LANES), F32),
        ],
        scratch_shapes=[pltpu.VMEM((tm, D_MODEL), BF16)],
        compiler_params=pltpu.CompilerParams(
            dimension_semantics=("arbitrary", "arbitrary"), vmem_limit_bytes=VMEM_LIMIT),
        name="proj",
    )(x2, norm_w, w_main, w_ab)


def _unit_lower_inverses(lows, ii, jj):
    eye = jnp.where(ii == jj, 1.0, 0.0).astype(F32)
    in_block = (ii // INV_BLOCK) == (jj // INV_BLOCK)
    ps = [jnp.where(in_block, -low, 0.0) for low in lows]
    ts = [eye + p for p in ps]
    ps = [p.astype(BF16) for p in ps]
    lows = [low.astype(BF16) for low in lows]
    span = 2
    while span < INV_BLOCK:
        ps = [_dot(p, p).astype(BF16) for p in ps]
        ts = [t + _dot(t.astype(BF16), p) for t, p in zip(ts, ps)]
        span *= 2
    s = INV_BLOCK
    zero = jnp.zeros((), BF16)
    while s < CHUNK:
        off_diag = ((ii // (2 * s)) == (jj // (2 * s))) & ((ii // s) != (jj // s))
        xs = [_dot(jnp.where(off_diag, low, zero), t.astype(BF16)).astype(BF16)
              for low, t in zip(lows, ts)]
        ts = [t - _dot(t.astype(BF16), x) for t, x in zip(ts, xs)]
        s *= 2
    return ts


def _gdn_kernel(qkv_ref, z_ref, ab_ref, convw_ref, alog_ref, dtb_ref, normw_ref, o_ref,
                b_s, o0_s, m_s, qp_s, gl_s, state_s, q_s, k_s, kb_s, rhs_s, dec_s, cv_s, qg_o, kdt_o,
                *, seq):
    nchunk = seq // CHUNK
    ii = lax.broadcasted_iota(jnp.int32, (CHUNK, CHUNK), 0)
    jj = lax.broadcasted_iota(jnp.int32, (CHUNK, CHUNK), 1)
    causal = ii >= jj
    strict = ii > jj
    tri = jnp.where(causal, 1.0, 0.0).astype(F32)
    neg_a = -jnp.exp(alog_ref[...])
    dtb = dtb_ref[...]

    def conv_cols(c, r0, lo, buf):
        x = qkv_ref[pl.ds(r0, CHUNK), lo:lo + LANES].astype(F32)
        prev0 = pl.multiple_of(jnp.maximum(r0 - 16, 0), 16)
        prev = qkv_ref[pl.ds(prev0, 16), lo:lo + LANES].astype(F32)
        buf[0:8, :] = prev[8:16] * jnp.where(c > 0, 1.0, 0.0)
        buf[8:8 + CHUNK, :] = x
        w = convw_ref[:, lo:lo + LANES]
        y = (w[3:4] * x + w[2:3] * buf[7:7 + CHUNK, :] + w[1:2] * buf[6:6 + CHUNK, :]
             + w[0:1] * buf[5:5 + CHUNK, :])
        return _silu(y)

    def l2n(x):
        return x * lax.rsqrt(jnp.sum(x * x, axis=-1, keepdims=True) + EPS)

    tri_b = tri.astype(BF16)

    def chunk_cumsum(g):
        g1 = g.astype(BF16)
        r1 = g - g1.astype(F32)
        g2 = r1.astype(BF16)
        g3 = (r1 - g2.astype(F32)).astype(BF16)
        return _dot(tri_b, g1) + (_dot(tri_b, g2) + _dot(tri_b, g3))

    def operands(cc, slot):
        for sub in range(GDN_PREP_CHUNKS):
            c = cc * GDN_PREP_CHUNKS + sub
            r0 = pl.multiple_of(c * CHUNK, CHUNK)
            ab = ab_ref[pl.ds(r0, CHUNK), :]
            xg = ab + dtb
            softplus = jnp.maximum(xg, 0.0) + jnp.log(1.0 + jnp.exp(-jnp.abs(xg)))
            g_all = neg_a * softplus
            beta_all = _sigmoid(ab)
            gc_all = chunk_cumsum(g_all)
            gc_t = gc_all.T
            gl_s[c] = jnp.exp(gc_all[CHUNK - 1:CHUNK, :])
            for h in range(GDN_HEADS):
                n = sub * GDN_HEADS + h
                bufs = [cv_s.at[slot, 3 * n + i] for i in range(3)]
                q = l2n(conv_cols(c, r0, h * GDN_DK, bufs[0])) * (GDN_DK ** -0.5)
                k = l2n(conv_cols(c, r0, GDN_QK + h * GDN_DK, bufs[1]))
                v = conv_cols(c, r0, 2 * GDN_QK + h * GDN_DV, bufs[2])
                gcol = gc_all[:, h:h + 1]
                grow = gc_t[h:h + 1, :]
                beta = beta_all[:, GDN_HEADS + h:GDN_HEADS + h + 1]
                dec_s[slot, n] = jnp.where(causal, jnp.exp(gcol - grow), 0.0)
                eg = jnp.exp(gcol)
                kb = k * beta
                q_s[slot, n] = q.astype(BF16)
                k_s[slot, n] = k.astype(BF16)
                kb_s[slot, n] = kb.astype(BF16)
                rhs_s[slot, n] = jnp.concatenate([v * beta, kb * eg], axis=1).astype(BF16)
                qg_o[slot, n] = q * eg
                kd = k * jnp.exp(gc_all[CHUNK - 1:CHUNK, h:h + 1] - gcol)
                kdt_o[slot, n] = kd.T.astype(BF16)

    def solve(cc, slot, between):
        items = [(cc * GDN_PREP_CHUNKS + sub, h, sub * GDN_HEADS + h)
                 for sub in range(GDN_PREP_CHUNKS) for h in range(GDN_HEADS)]
        kks = [_dot_nt(kb_s[slot, n], k_s[slot, n]) for _, _, n in items]
        qks = [_dot_nt(q_s[slot, n], k_s[slot, n]) for _, _, n in items]
        lows = [jnp.where(strict, kk * dec_s[slot, n], 0.0) for kk, (_, _, n) in zip(kks, items)]
        attns = [(qk * dec_s[slot, n]).astype(BF16) for qk, (_, _, n) in zip(qks, items)]
        for step in between[:len(between) // 2]:
            step()
        ts = _unit_lower_inverses(lows, ii, jj)
        for step in between[len(between) // 2:]:
            step()
        uws = [_dot(t.astype(BF16), rhs_s[slot, n]).astype(BF16) for t, (_, _, n) in zip(ts, items)]
        kds = [_dot(kdt_o[slot, n], uw) for uw, (_, _, n) in zip(uws, items)]
        ats = [_dot(attn, uw) for attn, uw in zip(attns, uws)]
        for kd_uw, at_uw, (c, h, n) in zip(kds, ats, items):
            b_s[c, h] = kd_uw[:, :GDN_DV]
            m_s[c, h] = (-kd_uw[:, GDN_DV:]).astype(BF16)
            o0_s[c, h] = at_uw[:, :GDN_DV]
            qp_s[c, h] = (qg_o[slot, n] - at_uw[:, GDN_DV:]).astype(BF16)

    ngroup = nchunk // GDN_PREP_CHUNKS
    per_trip = 2 * GDN_PREP_CHUNKS
    normw = normw_ref[...]
    state_s[...] = jnp.zeros_like(state_s)
    for c0 in range(per_trip):
        gl_s[c0] = jnp.zeros((1, LANES), F32)
        for h in range(GDN_HEADS):
            b_s[c0, h] = jnp.zeros((CHUNK, GDN_DV), F32)
            o0_s[c0, h] = jnp.zeros((CHUNK, GDN_DV), F32)
            m_s[c0, h] = jnp.zeros((CHUNK, GDN_DK), BF16)
            qp_s[c0, h] = jnp.zeros((CHUNK, GDN_DK), BF16)
    operands(0, 0)

    def prep(i, carry):
        first = jnp.maximum(per_trip * (i - 1), 0)
        steps = [functools.partial(scan, first + k, 0) for k in range(per_trip)]
        operands(2 * i + 1, 1)
        solve(2 * i, 0, steps[:per_trip // 2])
        operands(jnp.minimum(2 * i + 2, ngroup - 1), 0)
        solve(2 * i + 1, 1, steps[per_trip // 2:])
        return carry

    def scan(c, carry):
        r0 = pl.multiple_of(c * CHUNK, CHUNK)
        gl = gl_s[c]
        heads = range(GDN_HEADS)
        ss = [state_s[h] for h in heads]
        sbs = [s.astype(BF16) for s in ss]
        mss = [_dot(m_s[c, h], sbs[h]) for h in heads]
        qss = [_dot(qp_s[c, h], sbs[h]) for h in heads]
        for h in heads:
            state_s[h] = ss[h] * gl[:, h:h + 1] + (mss[h] + b_s[c, h])
            o = qss[h] + o0_s[c, h]
            z = z_ref[pl.ds(r0, CHUNK), h * GDN_DV:(h + 1) * GDN_DV].astype(F32)
            on = o * lax.rsqrt(jnp.mean(o * o, axis=-1, keepdims=True) + EPS) * normw
            o_ref[pl.ds(r0, CHUNK), h * GDN_DV:(h + 1) * GDN_DV] = (on * _silu(z)).astype(BF16)
        return carry

    ntrip = ngroup // 2
    lax.fori_loop(0, ntrip, prep, 0)
    lax.fori_loop(per_trip * (ntrip - 1), nchunk, scan, 0)


def _gdn_call(proj, ab, conv_w8, alog_p, dtb_p, normw, batch, seq):
    nchunk = seq // CHUNK
    hs = (nchunk, GDN_HEADS, CHUNK, CHUNK)
    ops = (2, GDN_PREP_CHUNKS * GDN_HEADS, CHUNK, CHUNK)
    return pl.pallas_call(
        functools.partial(_gdn_kernel, seq=seq),
        grid=(batch,),
        in_specs=[
            pl.BlockSpec((seq, 3 * GDN_QK), lambda b: (b, 0)),
            pl.BlockSpec((seq, GDN_V), lambda b: (b, 3)),
            pl.BlockSpec((seq, LANES), lambda b: (b, 0)),
            pl.BlockSpec((8, 3 * GDN_QK), lambda b: (0, 0)),
            pl.BlockSpec((1, LANES), lambda b: (0, 0)),
            pl.BlockSpec((1, LANES), lambda b: (0, 0)),
            pl.BlockSpec((1, GDN_DV), lambda b: (0, 0)),
        ],
        out_specs=pl.BlockSpec((seq, GDN_V), lambda b: (b, 0)),
        out_shape=jax.ShapeDtypeStruct((batch * seq, GDN_V), BF16),
        scratch_shapes=[
            pltpu.VMEM(hs, F32), pltpu.VMEM(hs, F32), pltpu.VMEM(hs, BF16), pltpu.VMEM(hs, BF16),
            pltpu.VMEM((nchunk, 1, LANES), F32),
            pltpu.VMEM((GDN_HEADS, GDN_DK, GDN_DV), F32),
            pltpu.VMEM(ops, BF16), pltpu.VMEM(ops, BF16), pltpu.VMEM(ops, BF16),
            pltpu.VMEM(ops[:3] + (2 * CHUNK,), BF16),
            pltpu.VMEM(ops, F32),
            pltpu.VMEM((2, 3 * ops[1], 8 + CHUNK, LANES), F32),
            pltpu.VMEM(ops, F32), pltpu.VMEM(ops, BF16),
        ],
        compiler_params=pltpu.CompilerParams(
            dimension_semantics=("arbitrary",), vmem_limit_bytes=VMEM_LIMIT),
        name="gdn",
    )(proj, proj, ab, conv_w8, alog_p, dtb_p, normw)


def _ret_kernel(qk_ref, v_ref, g_ref, sin_ref, cos_ref, inner_ref, kdec_ref, qdec_ref, cdec_ref,
                o_ref, state_s, q_s, qd_s, k_s, kt_s, *, seq):
    nchunk = seq // CHUNK
    lane = lax.broadcasted_iota(jnp.int32, (CHUNK, RET_DK), 1)
    even = (lane % 2) == 0

    def rotate(x, sin, cos):
        nxt = pltpu.roll(x, RET_DK - 1, axis=1)
        prv = pltpu.roll(x, 1, axis=1)
        return x * cos + jnp.where(even, -nxt, prv) * sin

    state_s[...] = jnp.zeros_like(state_s)
    kdec = kdec_ref[...]
    qdec = qdec_ref[...]
    cdec = cdec_ref[...]

    heads = range(RET_HEADS)

    def operands(c, slot):
        r0 = pl.multiple_of(c * CHUNK, CHUNK)
        sin = sin_ref[pl.ds(r0, CHUNK), :]
        cos = cos_ref[pl.ds(r0, CHUNK), :]
        for h in heads:
            q = rotate(qk_ref[pl.ds(r0, CHUNK), h * RET_DK:(h + 1) * RET_DK].astype(F32), sin, cos)
            k = rotate(qk_ref[pl.ds(r0, CHUNK), RET_QK + h * RET_DK:RET_QK + (h + 1) * RET_DK]
                       .astype(F32), sin, cos) * (RET_DK ** -0.5)
            q_s[slot, h] = q.astype(BF16)
            qd_s[slot, h] = (q * qdec[:, h:h + 1]).astype(BF16)
            k_s[slot, h] = k.astype(BF16)
            kt_s[slot, h] = (k * kdec[:, h:h + 1]).T.astype(BF16)

    def outputs(c, slot):
        r0 = pl.multiple_of(c * CHUNK, CHUNK)
        vs = [v_ref[pl.ds(r0, CHUNK), h * RET_DV:(h + 1) * RET_DV] for h in heads]
        ss = [state_s[h] for h in heads]
        qks = [_dot_nt(q_s[slot, h], k_s[slot, h]) for h in heads]
        inters = [_dot(qd_s[slot, h], ss[h].astype(BF16)) for h in heads]
        kvs = [_dot(kt_s[slot, h], vs[h]) for h in heads]
        intras = [_dot((qks[h] * inner_ref[h]).astype(BF16), vs[h]) for h in heads]
        for h in heads:
            state_s[h] = ss[h] * cdec[:, h:h + 1] + kvs[h]
            o = intras[h] + inters[h]
            gate = g_ref[pl.ds(r0, CHUNK), h * RET_DV:(h + 1) * RET_DV].astype(F32)
            on = o * lax.rsqrt(jnp.mean(o * o, axis=-1, keepdims=True) + EPS)
            o_ref[pl.ds(r0, CHUNK), h * RET_DV:(h + 1) * RET_DV] = (on * _silu(gate)).astype(BF16)

    operands(0, 0)

    def body(i, carry):
        operands(2 * i + 1, 1)
        outputs(2 * i, 0)
        operands(jnp.minimum(2 * i + 2, nchunk - 1), 0)
        outputs(2 * i + 1, 1)
        return carry

    lax.fori_loop(0, nchunk // 2, body, 0)


def _ret_call(proj, sin, cos, inner, kdec, qdec, cdec, batch, seq):
    return pl.pallas_call(
        functools.partial(_ret_kernel, seq=seq),
        grid=(batch,),
        in_specs=[
            pl.BlockSpec((seq, 2 * RET_QK), lambda b: (b, 2)),
            pl.BlockSpec((seq, RET_V), lambda b: (b, 3)),
            pl.BlockSpec((seq, RET_V), lambda b: (b, 4)),
            pl.BlockSpec((seq, RET_DK), lambda b: (0, 0)),
            pl.BlockSpec((seq, RET_DK), lambda b: (0, 0)),
            pl.BlockSpec((RET_HEADS, CHUNK, CHUNK), lambda b: (0, 0, 0)),
            pl.BlockSpec((CHUNK, LANES), lambda b: (0, 0)),
            pl.BlockSpec((CHUNK, LANES), lambda b: (0, 0)),
            pl.BlockSpec((1, LANES), lambda b: (0, 0)),
        ],
        out_specs=pl.BlockSpec((seq, RET_V), lambda b: (b, 0)),
        out_shape=jax.ShapeDtypeStruct((batch * seq, RET_V), BF16),
        scratch_shapes=[pltpu.VMEM((RET_HEADS, RET_DK, RET_DV), F32)]
        + [pltpu.VMEM((2, RET_HEADS, CHUNK, RET_DK), BF16)] * 4,
        compiler_params=pltpu.CompilerParams(
            dimension_semantics=("arbitrary",), vmem_limit_bytes=VMEM_LIMIT),
        name="retention",
    )(proj, proj, proj, sin, cos, inner, kdec, qdec, cdec)


def _retention_tables(seq):
    inv_freq = 1.0 / (ROPE_BASE ** jnp.linspace(0.0, 1.0, RET_DK // 2, dtype=F32))
    ang = jnp.arange(seq, dtype=F32)[:, None] * inv_freq[None, :]
    sin = jnp.repeat(jnp.sin(ang), 2, axis=-1)
    cos = jnp.repeat(jnp.cos(ang), 2, axis=-1)
    log_gamma = jnp.log(1.0 - 2.0 ** (-5.0 - jnp.arange(RET_HEADS, dtype=F32)))
    idx = jnp.arange(CHUNK, dtype=F32)
    causal = jnp.tril(jnp.ones((CHUNK, CHUNK), dtype=bool))
    rel = jnp.where(causal, idx[:, None] - idx[None, :], 0.0)
    inner = jnp.where(causal, jnp.exp(rel[None] * log_gamma[:, None, None]), 0.0)
    k_decay = jnp.exp(log_gamma[:, None] * (CHUNK - 1.0 - idx)[None, :])
    q_decay = jnp.exp(log_gamma[:, None] * (idx + 1.0)[None, :])
    chunk_decay = jnp.exp(log_gamma * CHUNK)
    pad = LANES - RET_HEADS
    kdec = jnp.pad(k_decay.T, ((0, 0), (0, pad)))
    qdec = jnp.pad(q_decay.T, ((0, 0), (0, pad)))
    cdec = jnp.pad(chunk_decay[None, :], ((0, 0), (0, pad)))
    return sin, cos, inner, kdec, qdec, cdec


def _merge_kernel(x_ref, ya_ref, yb_ref, ma_ref, mb_ref, wa_ref, wr_ref, wo_ref, nw_ref,
                  wrt_ref, br_ref, h_ref, xn_ref, ridx_ref, rw_ref, cnt_ref):
    tm = x_ref.shape[0] // MERGE_PARTS
    rows = [pl.ds(i * tm, tm) for i in range(MERGE_PARTS)]
    a_ = [_dot(ya_ref[r, :], wa_ref[...]) for r in rows]
    r_ = [_dot(yb_ref[r, :], wr_ref[...]) for r in rows]
    merged = [(_sigmoid(ma_ref[r, :].astype(F32)) * a + _sigmoid(mb_ref[r, :].astype(F32)) * rr)
              .astype(BF16) for r, a, rr in zip(rows, a_, r_)]
    hs = [x_ref[r, :] + _dot(mg, wo_ref[...]) for r, mg in zip(rows, merged)]
    xcats = []
    for r, h in zip(rows, hs):
        h_ref[r, :] = h
        xn = h * lax.rsqrt(jnp.mean(h * h, axis=-1, keepdims=True) + EPS) * nw_ref[...]
        xn_ref[r, :] = _pack_bf16_pairs(xn)
        xh = xn.astype(BF16)
        xl = (xn - xh.astype(F32)).astype(BF16)
        xcats.append(jnp.concatenate([xh, xl], axis=0))
    parts_ = [_dot(xc, wrt_ref[...]) for xc in xcats]
    counts = jnp.zeros((1, LANES), F32)
    for r, parts in zip(rows, parts_):
        counts = counts + _route(parts, tm, br_ref[...], ridx_ref.at[r, :], rw_ref.at[r, :])

    @pl.when(pl.program_id(0) == 0)
    def _():
        cnt_ref[...] = jnp.zeros_like(cnt_ref)

    cnt_ref[...] += jnp.broadcast_to(counts, cnt_ref.shape)


def _route(parts, tm, bias, ridx_ref, rw_ref):
    logits = (parts[:tm, :LANES] + (parts[tm:, :LANES] + parts[:tm, LANES:]
                                    + parts[tm:, LANES:])) + bias
    lane = lax.broadcasted_iota(jnp.int32, (tm, LANES), 1)
    neg = -jnp.inf
    gl = jnp.where((lane >= N_EXPERTS) & (lane < N_EXPERTS + N_GROUPS), logits, neg)
    gmax = jnp.max(gl, axis=-1, keepdims=True)
    gidx = jnp.min(jnp.where(gl == gmax, lane, LANES), axis=-1, keepdims=True) - N_EXPERTS
    g_w = 1.0 / jnp.sum(jnp.exp(gl - gmax), axis=-1, keepdims=True)
    el = jnp.where((lane // EXPERTS_PER_GROUP == gidx) & (lane < N_EXPERTS), logits, neg)
    m1 = jnp.max(el, axis=-1, keepdims=True)
    i1 = jnp.min(jnp.where(el == m1, lane, LANES), axis=-1, keepdims=True)
    el2 = jnp.where(lane == i1, neg, el)
    m2 = jnp.max(el2, axis=-1, keepdims=True)
    i2 = jnp.min(jnp.where(el2 == m2, lane, LANES), axis=-1, keepdims=True)
    e2 = jnp.exp(m2 - m1)
    p1 = g_w / (1.0 + e2)
    p2 = g_w * e2 / (1.0 + e2)
    ridx_ref[...] = jnp.where(lane == 0, i1, jnp.where(lane == 1, i2, 0))
    rw_ref[...] = jnp.where(lane == 0, p1, jnp.where(lane == 1, p2, 0.0))
    onehot = jnp.where((lane == i1) | (lane == i2), 1.0, 0.0)
    return jnp.sum(onehot, axis=0, keepdims=True)


def _merge_call(x2, ya, yb, proj, wa, wr, wo, nw, w_router, b_router, tm=1024):
    m = x2.shape[0]
    full = lambda shape: pl.BlockSpec(shape, lambda i: (0, 0))
    return pl.pallas_call(
        _merge_kernel,
        grid=(m // tm,),
        in_specs=[
            pl.BlockSpec((tm, D_MODEL), lambda i: (i, 0)),
            pl.BlockSpec((tm, GDN_V), lambda i: (i, 0)),
            pl.BlockSpec((tm, RET_V), lambda i: (i, 0)),
            pl.BlockSpec((tm, D_MODEL), lambda i: (i, 5)),
            pl.BlockSpec((tm, D_MODEL), lambda i: (i, 6)),
            full((GDN_V, D_MODEL)), full((RET_V, D_MODEL)), full((D_MODEL, D_MODEL)),
            full((1, D_MODEL)),
            full((D_MODEL, 2 * LANES)), full((1, LANES)),
        ],
        out_specs=[
            pl.BlockSpec((tm, D_MODEL), lambda i: (i, 0)),
            pl.BlockSpec((tm, HALF), lambda i: (i, 0)),
            pl.BlockSpec((tm, LANES), lambda i: (i, 0)),
            pl.BlockSpec((tm, LANES), lambda i: (i, 0)),
            pl.BlockSpec((8, LANES), lambda i: (0, 0)),
        ],
        out_shape=[
            jax.ShapeDtypeStruct((m, D_MODEL), F32),
            jax.ShapeDtypeStruct((m, HALF), I32),
            jax.ShapeDtypeStruct((m, LANES), I32),
            jax.ShapeDtypeStruct((m, LANES), F32),
            jax.ShapeDtypeStruct((8, LANES), F32),
        ],
        compiler_params=pltpu.CompilerParams(
            dimension_semantics=("arbitrary",), vmem_limit_bytes=VMEM_LIMIT),
        name="merge_router",
    )(x2, ya, yb, proj, proj, wa, wr, wo, nw, w_router, b_router)


def _slot_counts(m):
    n_slots = TOP_K * m + N_EXPERTS * SLOT_TILE
    return n_slots, n_slots // SLOT_TILE


def _lane_prefix_sum(x, lane):
    s = 1
    while s < LANES:
        x = x + jnp.where(lane >= s, pltpu.roll(x, s, axis=1), 0.0)
        s *= 2
    return x


def _plan_kernel(ridx_ref, cnt_ref, slots_ref, tile_ref, carry_s, off_s):
    i = pl.program_id(0)
    lane = lax.broadcasted_iota(I32, (PLAN_TILE, LANES), 1)
    row = lax.broadcasted_iota(I32, (PLAN_TILE, LANES), 0)
    lane1 = lane[0:1]

    @pl.when(i == 0)
    def _():
        cnt = cnt_ref[0:1, :]
        tile = float(SLOT_TILE)
        padded = jnp.floor((cnt + (tile - 1.0)) / tile) * tile
        incl = _lane_prefix_sum(padded, lane1)
        off = incl - padded
        off_s[...] = off
        carry_s[...] = jnp.zeros_like(carry_s)
        first = (row * SLOT_TILE).astype(F32)
        ended = jnp.where((lane < N_EXPERTS) & (incl <= first), 1.0, 0.0)
        tile_e = jnp.sum(ended, axis=-1, keepdims=True)
        last = jnp.sum(jnp.where(lane.astype(F32) == tile_e, off + cnt, 0.0), axis=-1, keepdims=True)
        used = jnp.clip(last - first[:, 0:1], 0.0, tile)
        tile_ref[...] = jnp.where(lane == 0, tile_e, jnp.where(lane == 1, used, 0.0)).astype(I32)

    strict = jnp.where(row[:, 0:1] > lax.broadcasted_iota(I32, (PLAN_TILE, PLAN_TILE), 1),
                       1.0, 0.0).astype(BF16)
    off = off_s[...]
    carry = carry_s[...]
    for sb in range(PLAN_STEP // PLAN_TILE):
        rows = pl.ds(sb * PLAN_TILE, PLAN_TILE)
        e1 = ridx_ref[rows, 0:1]
        e2 = ridx_ref[rows, 1:2]
        onehot = jnp.where((lane == e1) | (lane == e2), 1.0, 0.0)
        pos = _dot(strict, onehot.astype(BF16)) + (carry + off)
        s1 = jnp.sum(jnp.where(lane == e1, pos, 0.0), axis=-1, keepdims=True)
        s2 = jnp.sum(jnp.where(lane == e2, pos, 0.0), axis=-1, keepdims=True)
        both = jnp.where(lane == 0, s1, jnp.where(lane == 1, s2, 0.0))
        for q in range(PLAN_TILE // LANES):
            t = both[q * LANES:(q + 1) * LANES].T
            c0 = sb * PLAN_TILE + q * LANES
            slots_ref[:, c0:c0 + LANES] = t[0:8].astype(I32)
        carry = carry + jnp.sum(onehot, axis=0, keepdims=True)
    carry_s[...] = carry


def _plan_call(ridx, cnt):
    m = ridx.shape[0]
    _, n_tiles = _slot_counts(m)
    assert n_tiles <= PLAN_TILE
    return pl.pallas_call(
        _plan_kernel,
        grid=(m // PLAN_STEP,),
        in_specs=[pl.BlockSpec((PLAN_STEP, LANES), lambda i: (i, 0)),
                  pl.BlockSpec((8, LANES), lambda i: (0, 0))],
        out_specs=[
            pl.BlockSpec((8, PLAN_STEP), lambda i: (0, i)),
            pl.BlockSpec((PLAN_TILE, LANES), lambda i: (0, 0)),
        ],
        out_shape=[
            jax.ShapeDtypeStruct((8, m), I32),
            jax.ShapeDtypeStruct((PLAN_TILE, LANES), I32),
        ],
        scratch_shapes=[pltpu.VMEM((1, LANES), F32), pltpu.VMEM((1, LANES), F32)],
        compiler_params=pltpu.CompilerParams(dimension_semantics=("arbitrary",)),
        name="dispatch_plan",
    )(ridx, cnt)


def _sc_mesh():
    return plsc.VectorSubcoreMesh(core_axis_name="c", subcore_axis_name="s")


def _sc_worker():
    return lax.axis_index("s") * 2 + lax.axis_index("c")


def _sc_dispatch(xn, slot1, slot2, n_rows):
    m = xn.shape[0]
    per = m // SC_WORKERS
    n_pairs = per // (2 * SC_ROWS)

    @functools.partial(
        pl.kernel, mesh=_sc_mesh(),
        out_type=jax.ShapeDtypeStruct((n_rows, HALF), I32),
        scratch_types=[pltpu.VMEM((per // SC_ROWS, SC_ROWS), I32), pltpu.VMEM((per // SC_ROWS, SC_ROWS), I32),
                       pltpu.VMEM((SC_ROWS, HALF), I32), pltpu.VMEM((SC_ROWS, HALF), I32),
                       pltpu.SemaphoreType.DMA, pltpu.SemaphoreType.DMA, pltpu.SemaphoreType.DMA],
        name="sc_dispatch")
    def k(x_hbm, s1_hbm, s2_hbm, o_hbm, i1_v, i2_v, rows0, rows1, sem_r0, sem_r1, sem_w):
        wid = _sc_worker()
        base = wid * per

        def read(chunk, rows_v, sem):
            return pltpu.make_async_copy(x_hbm.at[pl.ds(base + chunk * SC_ROWS, SC_ROWS)], rows_v, sem)

        def scatter(chunk, rows_v):
            c1 = pltpu.async_copy(rows_v, o_hbm.at[i1_v.at[chunk]], sem_w)
            c2 = pltpu.async_copy(rows_v, o_hbm.at[i2_v.at[chunk]], sem_w)
            c1.wait()
            c2.wait()

        read(0, rows0, sem_r0).start()
        pltpu.sync_copy(s1_hbm.at[pl.ds(wid * (per // SC_ROWS), per // SC_ROWS)], i1_v)
        pltpu.sync_copy(s2_hbm.at[pl.ds(wid * (per // SC_ROWS), per // SC_ROWS)], i2_v)

        @pl.loop(0, n_pairs)
        def _(i):
            read(2 * i, rows0, sem_r0).wait()
            read(2 * i + 1, rows1, sem_r1).start()
            scatter(2 * i, rows0)
            read(2 * i + 1, rows1, sem_r1).wait()

            @pl.when(i + 1 < n_pairs)
            def _():
                read(2 * i + 2, rows0, sem_r0).start()

            scatter(2 * i + 1, rows1)

    return k(xn, slot1, slot2)


def _sc_collect(ys, slot1, slot2):
    m = slot1.size
    per = m // SC_WORKERS
    row = jax.ShapeDtypeStruct((m, HALF), I32)

    @functools.partial(
        pl.kernel, mesh=_sc_mesh(), out_type=[row, row],
        scratch_types=[pltpu.VMEM((per // SC_ROWS, SC_ROWS), I32), pltpu.VMEM((per // SC_ROWS, SC_ROWS), I32),
                       pltpu.VMEM((SC_ROWS, HALF), I32), pltpu.VMEM((SC_ROWS, HALF), I32),
                       pltpu.SemaphoreType.DMA, pltpu.SemaphoreType.DMA],
        name="sc_collect")
    def k(y_hbm, s1_hbm, s2_hbm, g1_hbm, g2_hbm, i1_v, i2_v, rows1, rows2, sem_g, sem_w):
        wid = _sc_worker()
        base = wid * per
        pltpu.sync_copy(s1_hbm.at[pl.ds(wid * (per // SC_ROWS), per // SC_ROWS)], i1_v)
        pltpu.sync_copy(s2_hbm.at[pl.ds(wid * (per // SC_ROWS), per // SC_ROWS)], i2_v)

        @pl.loop(0, per // SC_ROWS)
        def _(ci):
            t0 = base + ci * SC_ROWS
            a1 = pltpu.async_copy(y_hbm.at[i1_v.at[ci]], rows1, sem_g)
            a2 = pltpu.async_copy(y_hbm.at[i2_v.at[ci]], rows2, sem_g)
            a1.wait()
            a2.wait()
            w1 = pltpu.async_copy(rows1, g1_hbm.at[pl.ds(t0, SC_ROWS)], sem_w)
            w2 = pltpu.async_copy(rows2, g2_hbm.at[pl.ds(t0, SC_ROWS)], sem_w)
            w1.wait()
            w2.wait()

    return k(ys, slot1, slot2)


def _expert_kernel(te_ref, used_ref, xs_ref, wg_hbm, wu_hbm, wd_hbm, ys_ref, wg_b, wu_b, wd_b,
                   wg_f, wu_f, wd_f, sem, slot_s):
    j = pl.program_id(0)
    n = pl.num_programs(0)
    e = te_ref[j]
    prev = te_ref[jnp.maximum(j - 1, 0)]
    valid = e < N_EXPERTS

    def weight_copies(expert, slot):
        return [pltpu.make_async_copy(hbm.at[expert], buf.at[slot], sem.at[slot, i])
                for i, (hbm, buf) in enumerate(((wg_hbm, wg_f), (wu_hbm, wu_f), (wd_hbm, wd_f)))]

    @pl.when((j == 0) & valid)
    def _():
        slot_s[0] = 0
        for c in weight_copies(e, 0):
            c.start()

    @pl.when(((j == 0) | (e != prev)) & valid)
    def _():
        slot = slot_s[0]
        for c in weight_copies(e, slot):
            c.wait()
        k = lax.while_loop(lambda k: (k < n) & (te_ref[jnp.minimum(k, n - 1)] == e),
                           lambda k: k + 1, j + 1)
        nxt = te_ref[jnp.minimum(k, n - 1)]

        @pl.when((k < n) & (nxt < N_EXPERTS))
        def _():
            for c in weight_copies(nxt, 1 - slot):
                c.start()

        wg_b[...] = wg_f[slot].astype(BF16)
        wu_b[...] = wu_f[slot].astype(BF16)
        wd_b[...] = wd_f[slot].astype(BF16)
        slot_s[0] = 1 - slot

    @pl.when(valid)
    def _():
        half = SLOT_TILE // 2
        rows = [pl.ds(i * half, half) for i in range(2)]
        row_id = lax.broadcasted_iota(I32, (half, HALF), 0)
        xs = [_unpack_bf16_pairs(jnp.where(row_id + i * half < used_ref[j], xs_ref[r, :], 0))
              .astype(BF16) for i, r in enumerate(rows)]
        gs = [_dot(x, wg_b[...]) for x in xs]
        us = [_dot(x, wu_b[...]) for x in xs]
        hids = [(_silu(g) * u).astype(BF16) for g, u in zip(gs, us)]
        ys = [_dot(hid, wd_b[...]) for hid in hids]
        for r, y in zip(rows, ys):
            ys_ref[r, :] = _pack_bf16_pairs(y)

    @pl.when(e >= N_EXPERTS)
    def _():
        ys_ref[...] = jnp.zeros_like(ys_ref)


def _expert_call(tile_expert, tile_used, xs, wg, wu, wd, n_tiles):
    hbm = pl.BlockSpec(memory_space=pl.ANY)
    return pl.pallas_call(
        _expert_kernel,
        grid_spec=pltpu.PrefetchScalarGridSpec(
            num_scalar_prefetch=2,
            grid=(n_tiles,),
            in_specs=[pl.BlockSpec((SLOT_TILE, HALF), lambda j, te, used: (j, 0)), hbm, hbm, hbm],
            out_specs=pl.BlockSpec((SLOT_TILE, HALF), lambda j, te, used: (j, 0)),
            scratch_shapes=[
                pltpu.VMEM((D_MODEL, D_EXPERT), BF16), pltpu.VMEM((D_MODEL, D_EXPERT), BF16),
                pltpu.VMEM((D_EXPERT, D_MODEL), BF16),
                pltpu.VMEM((2, D_MODEL, D_EXPERT), F32), pltpu.VMEM((2, D_MODEL, D_EXPERT), F32),
                pltpu.VMEM((2, D_EXPERT, D_MODEL), F32),
                pltpu.SemaphoreType.DMA((2, 3)),
                pltpu.SMEM((1,), I32),
            ],
        ),
        out_shape=jax.ShapeDtypeStruct((n_tiles * SLOT_TILE, HALF), I32),
        compiler_params=pltpu.CompilerParams(
            dimension_semantics=("arbitrary",), vmem_limit_bytes=VMEM_LIMIT),
        name="experts",
    )(tile_expert, tile_used, xs, wg, wu, wd)


def _final_kernel(h_ref, g1_ref, g2_ref, rw_ref, nw_ref, o_ref):
    rw = rw_ref[...]
    y = rw[:, 0:1] * _unpack_bf16_pairs(g1_ref[...]) + rw[:, 1:2] * _unpack_bf16_pairs(g2_ref[...])
    h = h_ref[...] + y
    o_ref[...] = h * lax.rsqrt(jnp.mean(h * h, axis=-1, keepdims=True) + EPS) * nw_ref[...]


def _final_call(h1, g1, g2, rw, nw, tm=2048):
    m = h1.shape[0]
    return pl.pallas_call(
        _final_kernel,
        grid=(m // tm,),
        in_specs=[
            pl.BlockSpec((tm, D_MODEL), lambda i: (i, 0)),
            pl.BlockSpec((tm, HALF), lambda i: (i, 0)),
            pl.BlockSpec((tm, HALF), lambda i: (i, 0)),
            pl.BlockSpec((tm, LANES), lambda i: (i, 0)),
            pl.BlockSpec((1, D_MODEL), lambda i: (0, 0)),
        ],
        out_specs=pl.BlockSpec((tm, D_MODEL), lambda i: (i, 0)),
        out_shape=jax.ShapeDtypeStruct((m, D_MODEL), F32),
        compiler_params=pltpu.CompilerParams(dimension_semantics=("arbitrary",)),
        name="combine_final",
    )(h1, g1, g2, rw, nw)


def _pad_lanes(a):
    return jnp.pad(a, ((0, 0), (0, LANES - a.shape[1])))


def kernel(x, norm_mix_w, w_in, conv_w, A_log, dt_bias, gdn_norm_w, w_up_gdn, w_up_ret, w_out,
           norm_ffn_w, w_group, b_group, w_expert, b_expert, w_gate, w_up, w_down, norm_final_w):
    batch, seq, d = x.shape
    m = batch * seq
    h = x.reshape(m, d)
    depth = w_in.shape[0]
    sin, cos, inner, kdec, qdec, cdec = _retention_tables(seq)
    for l in range(depth):
        w_main, w_ab = _repack_call(jnp.transpose(w_in[l]))
        proj, ab = _proj_call(h, norm_mix_w[l][None, :], w_main, w_ab)

        conv8 = jnp.pad(conv_w[l], ((0, 8 - GDN_CONV), (0, 0)))
        ya = _gdn_call(proj, ab, conv8, _pad_lanes(A_log[l][None, :]), _pad_lanes(dt_bias[l][None, :]),
                       gdn_norm_w[l][None, :], batch, seq)
        yb = _ret_call(proj, sin, cos, inner, kdec, qdec, cdec, batch, seq)

        w_router = _pad_lanes(jnp.concatenate([w_expert[l], w_group[l]], axis=1))
        wr_hi = w_router.astype(BF16)
        wr_lo = (w_router - wr_hi.astype(F32)).astype(BF16)
        b_router = _pad_lanes(jnp.concatenate([b_expert[l], b_group[l]])[None, :])
        h1, xn, ridx, rw, cnt = _merge_call(
            h, ya, yb, proj, w_up_gdn[l].astype(BF16), w_up_ret[l].astype(BF16), w_out[l].astype(BF16),
            norm_ffn_w[l][None, :], jnp.concatenate([wr_hi, wr_lo], axis=1), b_router)

        n_slots, n_tiles = _slot_counts(m)
        slots, tiles = _plan_call(ridx, cnt)
        slot1, slot2 = slots[0].reshape(-1, SC_ROWS), slots[1].reshape(-1, SC_ROWS)
        xs = _sc_dispatch(xn, slot1, slot2, n_slots)
        ys = _expert_call(tiles[:n_tiles, 0], tiles[:n_tiles, 1], xs,
                          w_gate[l], w_up[l], w_down[l], n_tiles)
        g1, g2 = _sc_collect(ys, slot1, slot2)

        assert depth == 1
        h = _final_call(h1, g1, g2, rw, norm_final_w[None, :])
    return h.reshape(batch, seq, d)
```

```python
import functools

import jax
import jax.numpy as jnp
from jax import lax
from jax.experimental import pallas as pl
from jax.experimental.pallas import tpu as pltpu
from jax.experimental.pallas import tpu_sc as plsc

F32 = jnp.float32
BF16 = jnp.bfloat16
I32 = jnp.int32
U32 = jnp.uint32

D_MODEL = 1024
EPS = 1e-6
GDN_HEADS = 4
GDN_DK = 128
GDN_DV = 128
GDN_CONV = 4
RET_HEADS = 4
RET_DK = 128
RET_DV = 256
ROPE_BASE = 10000.0
N_GROUPS = 4
EXPERTS_PER_GROUP = 8
N_EXPERTS = N_GROUPS * EXPERTS_PER_GROUP
D_EXPERT = 512

GDN_QK = GDN_HEADS * GDN_DK
GDN_V = GDN_HEADS * GDN_DV
RET_QK = RET_HEADS * RET_DK
RET_V = RET_HEADS * RET_DV

LANES = 128
CHUNK = 128
INV_BLOCK = 16
GDN_PREP_CHUNKS = 2
VMEM_LIMIT = 56 * 1024 * 1024

REPACK_COLS = 512
TOP_K = 2
SLOT_TILE = 512
PLAN_TILE = 256
PLAN_STEP = 1024
HALF = D_MODEL // 2
SC_CORES = 2
SC_WORKERS = SC_CORES * 16
SC_ROWS = 64

PROJ_COLS = 3 * GDN_QK + GDN_V + 2 * RET_QK + 2 * RET_V + 2 * D_MODEL


def _silu(x):
    return x / (1.0 + jnp.exp(-x))


def _sigmoid(x):
    return 1.0 / (1.0 + jnp.exp(-x))


def _dot(a, b):
    return jnp.dot(a, b, preferred_element_type=F32)


def _dot_nt(a, b):
    return lax.dot_general(a, b, (((1,), (1,)), ((), ())), preferred_element_type=F32)


def _pack_bf16_pairs(x):
    bits = lax.bitcast_convert_type(x.astype(BF16).astype(F32), U32)
    packed = (bits[:, :HALF] >> 16) | (bits[:, HALF:] & jnp.uint32(0xFFFF0000))
    return lax.bitcast_convert_type(packed, I32)


def _unpack_bf16_pairs(p):
    p = lax.bitcast_convert_type(p, U32)
    lo = lax.bitcast_convert_type(p << 16, F32)
    hi = lax.bitcast_convert_type(p & jnp.uint32(0xFFFF0000), F32)
    return jnp.concatenate([lo, hi], axis=1)


def _proj_kernel(x_ref, nw_ref, w_ref, wab_ref, proj_ref, ab_ref, u_ref):
    j = pl.program_id(1)

    @pl.when(j == 0)
    def _():
        x = x_ref[...]
        u = x * lax.rsqrt(jnp.mean(x * x, axis=-1, keepdims=True) + EPS) * nw_ref[...]
        ub = u.astype(BF16)
        u_ref[...] = ub
        ab_ref[...] = _dot(ub, wab_ref[...])

    proj_ref[...] = _dot(u_ref[...], w_ref[...]).astype(BF16)


def _repack_kernel(wt_ref, abt_ref, main_ref, ab_ref):
    main_ref[...] = wt_ref[...].T.astype(BF16)

    @pl.when(pl.program_id(0) == 0)
    def _():
        ab = abt_ref[...].T
        ab_ref[...] = jnp.concatenate(
            [ab, jnp.zeros((ab.shape[0], LANES - ab.shape[1]), F32)], axis=1).astype(BF16)


def _repack_call(w_in_t):
    d_in, d = w_in_t.shape
    o_ab = 3 * GDN_QK
    n_ab = 2 * GDN_HEADS
    src = lambda r: pl.multiple_of(r * REPACK_COLS + jnp.where(r * REPACK_COLS >= o_ab, n_ab, 0), 8)
    return pl.pallas_call(
        _repack_kernel,
        grid=(PROJ_COLS // REPACK_COLS,),
        in_specs=[pl.BlockSpec((pl.Element(REPACK_COLS), pl.Element(d)), lambda r: (src(r), 0)),
                  pl.BlockSpec((pl.Element(n_ab), pl.Element(d)), lambda r: (o_ab, 0))],
        out_specs=[pl.BlockSpec((d, REPACK_COLS), lambda r: (0, r)),
                   pl.BlockSpec((d, LANES), lambda r: (0, 0))],
        out_shape=[jax.ShapeDtypeStruct((d, PROJ_COLS), BF16), jax.ShapeDtypeStruct((d, LANES), BF16)],
        compiler_params=pltpu.CompilerParams(dimension_semantics=("arbitrary",)),
        name="repack_w_in",
    )(w_in_t, w_in_t)


def _proj_call(x2, norm_w, w_main, w_ab, tm=1024, tn=3584):
    m = x2.shape[0]
    return pl.pallas_call(
        _proj_kernel,
        grid=(m // tm, PROJ_COLS // tn),
        in_specs=[
            pl.BlockSpec((tm, D_MODEL), lambda i, j: (i, 0)),
            pl.BlockSpec((1, D_MODEL), lambda i, j: (0, 0)),
            pl.BlockSpec((D_MODEL, tn), lambda i, j: (0, j)),
            pl.BlockSpec((D_MODEL, LANES), lambda i, j: (0, 0)),
        ],
        out_specs=[
            pl.BlockSpec((tm, tn), lambda i, j: (i, j)),
            pl.BlockSpec((tm, LANES), lambda i, j: (i, 0)),
        ],
        out_shape=[
            jax.ShapeDtypeStruct((m, PROJ_COLS), BF16),
            jax.ShapeDtypeStruct((m, LANES), F32),
        ],
        scratch_shapes=[pltpu.VMEM((tm, D_MODEL), BF16)],
        compiler_params=pltpu.CompilerParams(
            dimension_semantics=("arbitrary", "arbitrary"), vmem_limit_bytes=VMEM_LIMIT),
        name="proj",
    )(x2, norm_w, w_main, w_ab)


def _unit_lower_inverses(lows, ii, jj):
    eye = jnp.where(ii == jj, 1.0, 0.0).astype(F32)
    in_block = (ii // INV_BLOCK) == (jj // INV_BLOCK)
    ps = [jnp.where(in_block, -low, 0.0) for low in lows]
    ts = [eye + p for p in ps]
    span = 2
    while span < INV_BLOCK:
        ps = [_dot(p, p) for p in ps]
        ts = [t + _dot(t, p) for t, p in zip(ts, ps)]
        span *= 2
    s = INV_BLOCK
    while s < CHUNK:
        off_diag = ((ii // (2 * s)) == (jj // (2 * s))) & ((ii // s) != (jj // s))
        xs = [_dot(jnp.where(off_diag, low, 0.0), t) for low, t in zip(lows, ts)]
        ts = [t - _dot(t, x) for t, x in zip(ts, xs)]
        s *= 2
    return ts


def _gdn_kernel(qkv_ref, z_ref, ab_ref, convw_ref, alog_ref, dtb_ref, normw_ref, o_ref,
                b_s, o0_s, m_s, qp_s, gl_s, state_s, q_s, k_s, kb_s, rhs_s, dec_s, cv_s, qg_o, kdt_o,
                *, seq):
    nchunk = seq // CHUNK
    ii = lax.broadcasted_iota(jnp.int32, (CHUNK, CHUNK), 0)
    jj = lax.broadcasted_iota(jnp.int32, (CHUNK, CHUNK), 1)
    causal = ii >= jj
    strict = ii > jj
    tri = jnp.where(causal, 1.0, 0.0).astype(F32)
    neg_a = -jnp.exp(alog_ref[...])
    dtb = dtb_ref[...]

    def conv_cols(c, r0, lo, buf):
        x = qkv_ref[pl.ds(r0, CHUNK), lo:lo + LANES].astype(F32)
        prev0 = pl.multiple_of(jnp.maximum(r0 - 16, 0), 16)
        prev = qkv_ref[pl.ds(prev0, 16), lo:lo + LANES].astype(F32)
        buf[0:8, :] = prev[8:16] * jnp.where(c > 0, 1.0, 0.0)
        buf[8:8 + CHUNK, :] = x
        w = convw_ref[:, lo:lo + LANES]
        y = (w[3:4] * x + w[2:3] * buf[7:7 + CHUNK, :] + w[1:2] * buf[6:6 + CHUNK, :]
             + w[0:1] * buf[5:5 + CHUNK, :])
        return _silu(y)

    def l2n(x):
        return x * lax.rsqrt(jnp.sum(x * x, axis=-1, keepdims=True) + EPS)

    tri_b = tri.astype(BF16)

    def chunk_cumsum(g):
        g1 = g.astype(BF16)
        r1 = g - g1.astype(F32)
        g2 = r1.astype(BF16)
        g3 = (r1 - g2.astype(F32)).astype(BF16)
        return _dot(tri_b, g1) + (_dot(tri_b, g2) + _dot(tri_b, g3))

    def operands(cc, slot):
        for sub in range(GDN_PREP_CHUNKS):
            c = cc * GDN_PREP_CHUNKS + sub
            r0 = pl.multiple_of(c * CHUNK, CHUNK)
            ab = ab_ref[pl.ds(r0, CHUNK), :]
            xg = ab + dtb
            softplus = jnp.maximum(xg, 0.0) + jnp.log(1.0 + jnp.exp(-jnp.abs(xg)))
            g_all = neg_a * softplus
            beta_all = _sigmoid(ab)
            gc_all = chunk_cumsum(g_all)
            gc_t = gc_all.T
            gl_s[c] = jnp.exp(gc_all[CHUNK - 1:CHUNK, :])
            for h in range(GDN_HEADS):
                n = sub * GDN_HEADS + h
                bufs = [cv_s.at[slot, 3 * n + i] for i in range(3)]
                q = l2n(conv_cols(c, r0, h * GDN_DK, bufs[0])) * (GDN_DK ** -0.5)
                k = l2n(conv_cols(c, r0, GDN_QK + h * GDN_DK, bufs[1]))
                v = conv_cols(c, r0, 2 * GDN_QK + h * GDN_DV, bufs[2])
                gcol = gc_all[:, h:h + 1]
                grow = gc_t[h:h + 1, :]
                beta = beta_all[:, GDN_HEADS + h:GDN_HEADS + h + 1]
                dec_s[slot, n] = jnp.where(causal, jnp.exp(gcol - grow), 0.0)
                eg = jnp.exp(gcol)
                kb = k * beta
                q_s[slot, n] = q.astype(BF16)
                k_s[slot, n] = k.astype(BF16)
                kb_s[slot, n] = kb.astype(BF16)
                rhs_s[slot, n] = jnp.concatenate([v * beta, kb * eg], axis=1).astype(BF16)
                qg_o[slot, n] = q * eg
                kd = k * jnp.exp(gc_all[CHUNK - 1:CHUNK, h:h + 1] - gcol)
                kdt_o[slot, n] = kd.T.astype(BF16)

    def solve(cc, slot, between):
        items = [(cc * GDN_PREP_CHUNKS + sub, h, sub * GDN_HEADS + h)
                 for sub in range(GDN_PREP_CHUNKS) for h in range(GDN_HEADS)]
        kks = [_dot_nt(kb_s[slot, n], k_s[slot, n]) for _, _, n in items]
        qks = [_dot_nt(q_s[slot, n], k_s[slot, n]) for _, _, n in items]
        lows = [jnp.where(strict, kk * dec_s[slot, n], 0.0) for kk, (_, _, n) in zip(kks, items)]
        attns = [(qk * dec_s[slot, n]).astype(BF16) for qk, (_, _, n) in zip(qks, items)]
        for step in between[:len(between) // 2]:
            step()
        ts = _unit_lower_inverses(lows, ii, jj)
        for step in between[len(between) // 2:]:
            step()
        uws = [_dot(t.astype(BF16), rhs_s[slot, n]).astype(BF16) for t, (_, _, n) in zip(ts, items)]
        kds = [_dot(kdt_o[slot, n], uw) for uw, (_, _, n) in zip(uws, items)]
        ats = [_dot(attn, uw) for attn, uw in zip(attns, uws)]
        for kd_uw, at_uw, (c, h, n) in zip(kds, ats, items):
            b_s[c, h] = kd_uw[:, :GDN_DV]
            m_s[c, h] = (-kd_uw[:, GDN_DV:]).astype(BF16)
            o0_s[c, h] = at_uw[:, :GDN_DV]
            qp_s[c, h] = (qg_o[slot, n] - at_uw[:, GDN_DV:]).astype(BF16)

    ngroup = nchunk // GDN_PREP_CHUNKS
    per_trip = 2 * GDN_PREP_CHUNKS
    normw = normw_ref[...]
    state_s[...] = jnp.zeros_like(state_s)
    for c0 in range(per_trip):
        gl_s[c0] = jnp.zeros((1, LANES), F32)
        for h in range(GDN_HEADS):
            b_s[c0, h] = jnp.zeros((CHUNK, GDN_DV), F32)
            o0_s[c0, h] = jnp.zeros((CHUNK, GDN_DV), F32)
            m_s[c0, h] = jnp.zeros((CHUNK, GDN_DK), BF16)
            qp_s[c0, h] = jnp.zeros((CHUNK, GDN_DK), BF16)
    operands(0, 0)

    def prep(i, carry):
        first = jnp.maximum(per_trip * (i - 1), 0)
        steps = [functools.partial(scan, first + k, 0) for k in range(per_trip)]
        operands(2 * i + 1, 1)
        solve(2 * i, 0, steps[:per_trip // 2])
        operands(jnp.minimum(2 * i + 2, ngroup - 1), 0)
        solve(2 * i + 1, 1, steps[per_trip // 2:])
        return carry

    def scan(c, carry):
        r0 = pl.multiple_of(c * CHUNK, CHUNK)
        gl = gl_s[c]
        heads = range(GDN_HEADS)
        ss = [state_s[h] for h in heads]
        sbs = [s.astype(BF16) for s in ss]
        mss = [_dot(m_s[c, h], sbs[h]) for h in heads]
        qss = [_dot(qp_s[c, h], sbs[h]) for h in heads]
        for h in heads:
            state_s[h] = ss[h] * gl[:, h:h + 1] + (mss[h] + b_s[c, h])
            o = qss[h] + o0_s[c, h]
            z = z_ref[pl.ds(r0, CHUNK), h * GDN_DV:(h + 1) * GDN_DV].astype(F32)
            on = o * lax.rsqrt(jnp.mean(o * o, axis=-1, keepdims=True) + EPS) * normw
            o_ref[pl.ds(r0, CHUNK), h * GDN_DV:(h + 1) * GDN_DV] = (on * _silu(z)).astype(BF16)
        return carry

    ntrip = ngroup // 2
    lax.fori_loop(0, ntrip, prep, 0)
    lax.fori_loop(per_trip * (ntrip - 1), nchunk, scan, 0)


def _gdn_call(proj, ab, conv_w8, alog_p, dtb_p, normw, batch, seq):
    nchunk = seq // CHUNK
    hs = (nchunk, GDN_HEADS, CHUNK, CHUNK)
    ops = (2, GDN_PREP_CHUNKS * GDN_HEADS, CHUNK, CHUNK)
    return pl.pallas_call(
        functools.partial(_gdn_kernel, seq=seq),
        grid=(batch,),
        in_specs=[
            pl.BlockSpec((seq, 3 * GDN_QK), lambda b: (b, 0)),
            pl.BlockSpec((seq, GDN_V), lambda b: (b, 3)),
            pl.BlockSpec((seq, LANES), lambda b: (b, 0)),
            pl.BlockSpec((8, 3 * GDN_QK), lambda b: (0, 0)),
            pl.BlockSpec((1, LANES), lambda b: (0, 0)),
            pl.BlockSpec((1, LANES), lambda b: (0, 0)),
            pl.BlockSpec((1, GDN_DV), lambda b: (0, 0)),
        ],
        out_specs=pl.BlockSpec((seq, GDN_V), lambda b: (b, 0)),
        out_shape=jax.ShapeDtypeStruct((batch * seq, GDN_V), BF16),
        scratch_shapes=[
            pltpu.VMEM(hs, F32), pltpu.VMEM(hs, F32), pltpu.VMEM(hs, BF16), pltpu.VMEM(hs, BF16),
            pltpu.VMEM((nchunk, 1, LANES), F32),
            pltpu.VMEM((GDN_HEADS, GDN_DK, GDN_DV), F32),
            pltpu.VMEM(ops, BF16), pltpu.VMEM(ops, BF16), pltpu.VMEM(ops, BF16),
            pltpu.VMEM(ops[:3] + (2 * CHUNK,), BF16),
            pltpu.VMEM(ops, F32),
            pltpu.VMEM((2, 3 * ops[1], 8 + CHUNK, LANES), F32),
            pltpu.VMEM(ops, F32), pltpu.VMEM(ops, BF16),
        ],
        compiler_params=pltpu.CompilerParams(
            dimension_semantics=("arbitrary",), vmem_limit_bytes=VMEM_LIMIT),
        name="gdn",
    )(proj, proj, ab, conv_w8, alog_p, dtb_p, normw)


def _ret_kernel(qk_ref, v_ref, g_ref, sin_ref, cos_ref, inner_ref, kdec_ref, qdec_ref, cdec_ref,
                o_ref, state_s, q_s, qd_s, k_s, kt_s, *, seq):
    nchunk = seq // CHUNK
    lane = lax.broadcasted_iota(jnp.int32, (CHUNK, RET_DK), 1)
    even = (lane % 2) == 0

    def rotate(x, sin, cos):
        nxt = pltpu.roll(x, RET_DK - 1, axis=1)
        prv = pltpu.roll(x, 1, axis=1)
        return x * cos + jnp.where(even, -nxt, prv) * sin

    state_s[...] = jnp.zeros_like(state_s)
    kdec = kdec_ref[...]
    qdec = qdec_ref[...]
    cdec = cdec_ref[...]

    heads = range(RET_HEADS)

    def operands(c, slot):
        r0 = pl.multiple_of(c * CHUNK, CHUNK)
        sin = sin_ref[pl.ds(r0, CHUNK), :]
        cos = cos_ref[pl.ds(r0, CHUNK), :]
        for h in heads:
            q = rotate(qk_ref[pl.ds(r0, CHUNK), h * RET_DK:(h + 1) * RET_DK].astype(F32), sin, cos)
            k = rotate(qk_ref[pl.ds(r0, CHUNK), RET_QK + h * RET_DK:RET_QK + (h + 1) * RET_DK]
                       .astype(F32), sin, cos) * (RET_DK ** -0.5)
            q_s[slot, h] = q.astype(BF16)
            qd_s[slot, h] = (q * qdec[:, h:h + 1]).astype(BF16)
            k_s[slot, h] = k.astype(BF16)
            kt_s[slot, h] = (k * kdec[:, h:h + 1]).T.astype(BF16)

    def outputs(c, slot):
        r0 = pl.multiple_of(c * CHUNK, CHUNK)
        vs = [v_ref[pl.ds(r0, CHUNK), h * RET_DV:(h + 1) * RET_DV] for h in heads]
        ss = [state_s[h] for h in heads]
        qks = [_dot_nt(q_s[slot, h], k_s[slot, h]) for h in heads]
        inters = [_dot(qd_s[slot, h], ss[h].astype(BF16)) for h in heads]
        kvs = [_dot(kt_s[slot, h], vs[h]) for h in heads]
        intras = [_dot((qks[h] * inner_ref[h]).astype(BF16), vs[h]) for h in heads]
        for h in heads:
            state_s[h] = ss[h] * cdec[:, h:h + 1] + kvs[h]
            o = intras[h] + inters[h]
            gate = g_ref[pl.ds(r0, CHUNK), h * RET_DV:(h + 1) * RET_DV].astype(F32)
            on = o * lax.rsqrt(jnp.mean(o * o, axis=-1, keepdims=True) + EPS)
            o_ref[pl.ds(r0, CHUNK), h * RET_DV:(h + 1) * RET_DV] = (on * _silu(gate)).astype(BF16)

    operands(0, 0)

    def body(i, carry):
        operands(2 * i + 1, 1)
        outputs(2 * i, 0)
        operands(jnp.minimum(2 * i + 2, nchunk - 1), 0)
        outputs(2 * i + 1, 1)
        return carry

    lax.fori_loop(0, nchunk // 2, body, 0)


def _ret_call(proj, sin, cos, inner, kdec, qdec, cdec, batch, seq):
    return pl.pallas_call(
        functools.partial(_ret_kernel, seq=seq),
        grid=(batch,),
        in_specs=[
            pl.BlockSpec((seq, 2 * RET_QK), lambda b: (b, 2)),
            pl.BlockSpec((seq, RET_V), lambda b: (b, 3)),
            pl.BlockSpec((seq, RET_V), lambda b: (b, 4)),
            pl.BlockSpec((seq, RET_DK), lambda b: (0, 0)),
            pl.BlockSpec((seq, RET_DK), lambda b: (0, 0)),
            pl.BlockSpec((RET_HEADS, CHUNK, CHUNK), lambda b: (0, 0, 0)),
            pl.BlockSpec((CHUNK, LANES), lambda b: (0, 0)),
            pl.BlockSpec((CHUNK, LANES), lambda b: (0, 0)),
            pl.BlockSpec((1, LANES), lambda b: (0, 0)),
        ],
        out_specs=pl.BlockSpec((seq, RET_V), lambda b: (b, 0)),
        out_shape=jax.ShapeDtypeStruct((batch * seq, RET_V), BF16),
        scratch_shapes=[pltpu.VMEM((RET_HEADS, RET_DK, RET_DV), F32)]
        + [pltpu.VMEM((2, RET_HEADS, CHUNK, RET_DK), BF16)] * 4,
        compiler_params=pltpu.CompilerParams(
            dimension_semantics=("arbitrary",), vmem_limit_bytes=VMEM_LIMIT),
        name="retention",
    )(proj, proj, proj, sin, cos, inner, kdec, qdec, cdec)


def _retention_tables(seq):
    inv_freq = 1.0 / (ROPE_BASE ** jnp.linspace(0.0, 1.0, RET_DK // 2, dtype=F32))
    ang = jnp.arange(seq, dtype=F32)[:, None] * inv_freq[None, :]
    sin = jnp.repeat(jnp.sin(ang), 2, axis=-1)
    cos = jnp.repeat(jnp.cos(ang), 2, axis=-1)
    log_gamma = jnp.log(1.0 - 2.0 ** (-5.0 - jnp.arange(RET_HEADS, dtype=F32)))
    idx = jnp.arange(CHUNK, dtype=F32)
    causal = jnp.tril(jnp.ones((CHUNK, CHUNK), dtype=bool))
    rel = jnp.where(causal, idx[:, None] - idx[None, :], 0.0)
    inner = jnp.where(causal, jnp.exp(rel[None] * log_gamma[:, None, None]), 0.0)
    k_decay = jnp.exp(log_gamma[:, None] * (CHUNK - 1.0 - idx)[None, :])
    q_decay = jnp.exp(log_gamma[:, None] * (idx + 1.0)[None, :])
    chunk_decay = jnp.exp(log_gamma * CHUNK)
    pad = LANES - RET_HEADS
    kdec = jnp.pad(k_decay.T, ((0, 0), (0, pad)))
    qdec = jnp.pad(q_decay.T, ((0, 0), (0, pad)))
    cdec = jnp.pad(chunk_decay[None, :], ((0, 0), (0, pad)))
    return sin, cos, inner, kdec, qdec, cdec


def _merge_kernel(x_ref, ya_ref, yb_ref, ma_ref, mb_ref, wa_ref, wr_ref, wo_ref, nw_ref,
                  wrt_ref, br_ref, h_ref, xn_ref, ridx_ref, rw_ref, cnt_ref):
    tm = x_ref.shape[0]
    a = _dot(ya_ref[...], wa_ref[...])
    r = _dot(yb_ref[...], wr_ref[...])
    merged = _sigmoid(ma_ref[...].astype(F32)) * a + _sigmoid(mb_ref[...].astype(F32)) * r
    h = x_ref[...] + _dot(merged.astype(BF16), wo_ref[...])
    h_ref[...] = h
    xn = h * lax.rsqrt(jnp.mean(h * h, axis=-1, keepdims=True) + EPS) * nw_ref[...]
    xn_ref[...] = _pack_bf16_pairs(xn)
    xh = xn.astype(BF16)
    xl = (xn - xh.astype(F32)).astype(BF16)
    parts = _dot(jnp.concatenate([xh, xl], axis=0), wrt_ref[...])
    counts = _route(parts, tm, br_ref[...], ridx_ref, rw_ref)

    @pl.when(pl.program_id(0) == 0)
    def _():
        cnt_ref[...] = jnp.zeros_like(cnt_ref)

    cnt_ref[...] += jnp.broadcast_to(counts, cnt_ref.shape)


def _route(parts, tm, bias, ridx_ref, rw_ref):
    logits = (parts[:tm, :LANES] + (parts[tm:, :LANES] + parts[:tm, LANES:]
                                    + parts[tm:, LANES:])) + bias
    lane = lax.broadcasted_iota(jnp.int32, (tm, LANES), 1)
    neg = -jnp.inf
    gl = jnp.where((lane >= N_EXPERTS) & (lane < N_EXPERTS + N_GROUPS), logits, neg)
    gmax = jnp.max(gl, axis=-1, keepdims=True)
    gidx = jnp.min(jnp.where(gl == gmax, lane, LANES), axis=-1, keepdims=True) - N_EXPERTS
    g_w = 1.0 / jnp.sum(jnp.exp(gl - gmax), axis=-1, keepdims=True)
    el = jnp.where((lane // EXPERTS_PER_GROUP == gidx) & (lane < N_EXPERTS), logits, neg)
    m1 = jnp.max(el, axis=-1, keepdims=True)
    i1 = jnp.min(jnp.where(el == m1, lane, LANES), axis=-1, keepdims=True)
    el2 = jnp.where(lane == i1, neg, el)
    m2 = jnp.max(el2, axis=-1, keepdims=True)
    i2 = jnp.min(jnp.where(el2 == m2, lane, LANES), axis=-1, keepdims=True)
    e2 = jnp.exp(m2 - m1)
    p1 = g_w / (1.0 + e2)
    p2 = g_w * e2 / (1.0 + e2)
    ridx_ref[...] = jnp.where(lane == 0, i1, jnp.where(lane == 1, i2, 0))
    rw_ref[...] = jnp.where(lane == 0, p1, jnp.where(lane == 1, p2, 0.0))
    onehot = jnp.where((lane == i1) | (lane == i2), 1.0, 0.0)
    return jnp.sum(onehot, axis=0, keepdims=True)


def _merge_call(x2, ya, yb, proj, wa, wr, wo, nw, w_router, b_router, tm=1024):
    m = x2.shape[0]
    full = lambda shape: pl.BlockSpec(shape, lambda i: (0, 0))
    return pl.pallas_call(
        _merge_kernel,
        grid=(m // tm,),
        in_specs=[
            pl.BlockSpec((tm, D_MODEL), lambda i: (i, 0)),
            pl.BlockSpec((tm, GDN_V), lambda i: (i, 0)),
            pl.BlockSpec((tm, RET_V), lambda i: (i, 0)),
            pl.BlockSpec((tm, D_MODEL), lambda i: (i, 5)),
            pl.BlockSpec((tm, D_MODEL), lambda i: (i, 6)),
            full((GDN_V, D_MODEL)), full((RET_V, D_MODEL)), full((D_MODEL, D_MODEL)),
            full((1, D_MODEL)),
            full((D_MODEL, 2 * LANES)), full((1, LANES)),
        ],
        out_specs=[
            pl.BlockSpec((tm, D_MODEL), lambda i: (i, 0)),
            pl.BlockSpec((tm, HALF), lambda i: (i, 0)),
            pl.BlockSpec((tm, LANES), lambda i: (i, 0)),
            pl.BlockSpec((tm, LANES), lambda i: (i, 0)),
            pl.BlockSpec((8, LANES), lambda i: (0, 0)),
        ],
        out_shape=[
            jax.ShapeDtypeStruct((m, D_MODEL), F32),
            jax.ShapeDtypeStruct((m, HALF), I32),
            jax.ShapeDtypeStruct((m, LANES), I32),
            jax.ShapeDtypeStruct((m, LANES), F32),
            jax.ShapeDtypeStruct((8, LANES), F32),
        ],
        compiler_params=pltpu.CompilerParams(
            dimension_semantics=("arbitrary",), vmem_limit_bytes=VMEM_LIMIT),
        name="merge_router",
    )(x2, ya, yb, proj, proj, wa, wr, wo, nw, w_router, b_router)


def _slot_counts(m):
    n_slots = TOP_K * m + N_EXPERTS * SLOT_TILE
    return n_slots, n_slots // SLOT_TILE


def _lane_prefix_sum(x, lane):
    s = 1
    while s < LANES:
        x = x + jnp.where(lane >= s, pltpu.roll(x, s, axis=1), 0.0)
        s *= 2
    return x


def _plan_kernel(ridx_ref, cnt_ref, slots_ref, tile_ref, carry_s, off_s):
    i = pl.program_id(0)
    lane = lax.broadcasted_iota(I32, (PLAN_TILE, LANES), 1)
    row = lax.broadcasted_iota(I32, (PLAN_TILE, LANES), 0)
    lane1 = lane[0:1]

    @pl.when(i == 0)
    def _():
        cnt = cnt_ref[0:1, :]
        tile = float(SLOT_TILE)
        padded = jnp.floor((cnt + (tile - 1.0)) / tile) * tile
        incl = _lane_prefix_sum(padded, lane1)
        off = incl - padded
        off_s[...] = off
        carry_s[...] = jnp.zeros_like(carry_s)
        first = (row * SLOT_TILE).astype(F32)
        ended = jnp.where((lane < N_EXPERTS) & (incl <= first), 1.0, 0.0)
        tile_e = jnp.sum(ended, axis=-1, keepdims=True)
        last = jnp.sum(jnp.where(lane.astype(F32) == tile_e, off + cnt, 0.0), axis=-1, keepdims=True)
        used = jnp.clip(last - first[:, 0:1], 0.0, tile)
        tile_ref[...] = jnp.where(lane == 0, tile_e, jnp.where(lane == 1, used, 0.0)).astype(I32)

    strict = jnp.where(row[:, 0:1] > lax.broadcasted_iota(I32, (PLAN_TILE, PLAN_TILE), 1),
                       1.0, 0.0).astype(BF16)
    off = off_s[...]
    carry = carry_s[...]
    for sb in range(PLAN_STEP // PLAN_TILE):
        rows = pl.ds(sb * PLAN_TILE, PLAN_TILE)
        e1 = ridx_ref[rows, 0:1]
        e2 = ridx_ref[rows, 1:2]
        onehot = jnp.where((lane == e1) | (lane == e2), 1.0, 0.0)
        pos = _dot(strict, onehot.astype(BF16)) + (carry + off)
        s1 = jnp.sum(jnp.where(lane == e1, pos, 0.0), axis=-1, keepdims=True)
        s2 = jnp.sum(jnp.where(lane == e2, pos, 0.0), axis=-1, keepdims=True)
        both = jnp.where(lane == 0, s1, jnp.where(lane == 1, s2, 0.0))
        for q in range(PLAN_TILE // LANES):
            t = both[q * LANES:(q + 1) * LANES].T
            c0 = sb * PLAN_TILE + q * LANES
            slots_ref[:, c0:c0 + LANES] = t[0:8].astype(I32)
        carry = carry + jnp.sum(onehot, axis=0, keepdims=True)
    carry_s[...] = carry


def _plan_call(ridx, cnt):
    m = ridx.shape[0]
    _, n_tiles = _slot_counts(m)
    assert n_tiles <= PLAN_TILE
    return pl.pallas_call(
        _plan_kernel,
        grid=(m // PLAN_STEP,),
        in_specs=[pl.BlockSpec((PLAN_STEP, LANES), lambda i: (i, 0)),
                  pl.BlockSpec((8, LANES), lambda i: (0, 0))],
        out_specs=[
            pl.BlockSpec((8, PLAN_STEP), lambda i: (0, i)),
            pl.BlockSpec((PLAN_TILE, LANES), lambda i: (0, 0)),
        ],
        out_shape=[
            jax.ShapeDtypeStruct((8, m), I32),
            jax.ShapeDtypeStruct((PLAN_TILE, LANES), I32),
        ],
        scratch_shapes=[pltpu.VMEM((1, LANES), F32), pltpu.VMEM((1, LANES), F32)],
        compiler_params=pltpu.CompilerParams(dimension_semantics=("arbitrary",)),
        name="dispatch_plan",
    )(ridx, cnt)


def _sc_mesh():
    return plsc.VectorSubcoreMesh(core_axis_name="c", subcore_axis_name="s")


def _sc_worker():
    return lax.axis_index("s") * SC_CORES + lax.axis_index("c")


def _sc_dispatch(xn, slot1, slot2, n_rows):
    m = xn.shape[0]
    per = m // SC_WORKERS
    n_pairs = per // (2 * SC_ROWS)

    @functools.partial(
        pl.kernel, mesh=_sc_mesh(),
        out_type=jax.ShapeDtypeStruct((n_rows, HALF), I32),
        scratch_types=[pltpu.VMEM((per // SC_ROWS, SC_ROWS), I32), pltpu.VMEM((per // SC_ROWS, SC_ROWS), I32),
                       pltpu.VMEM((SC_ROWS, HALF), I32), pltpu.VMEM((SC_ROWS, HALF), I32),
                       pltpu.SemaphoreType.DMA, pltpu.SemaphoreType.DMA, pltpu.SemaphoreType.DMA],
        name="sc_dispatch")
    def k(x_hbm, s1_hbm, s2_hbm, o_hbm, i1_v, i2_v, rows0, rows1, sem_r0, sem_r1, sem_w):
        wid = _sc_worker()
        base = wid * per

        def read(chunk, rows_v, sem):
            return pltpu.make_async_copy(x_hbm.at[pl.ds(base + chunk * SC_ROWS, SC_ROWS)], rows_v, sem)

        def scatter(chunk, rows_v):
            c1 = pltpu.async_copy(rows_v, o_hbm.at[i1_v.at[chunk]], sem_w)
            c2 = pltpu.async_copy(rows_v, o_hbm.at[i2_v.at[chunk]], sem_w)
            c1.wait()
            c2.wait()

        read(0, rows0, sem_r0).start()
        pltpu.sync_copy(s1_hbm.at[pl.ds(wid * (per // SC_ROWS), per // SC_ROWS)], i1_v)
        pltpu.sync_copy(s2_hbm.at[pl.ds(wid * (per // SC_ROWS), per // SC_ROWS)], i2_v)

        @pl.loop(0, n_pairs)
        def _(i):
            read(2 * i, rows0, sem_r0).wait()
            read(2 * i + 1, rows1, sem_r1).start()
            scatter(2 * i, rows0)
            read(2 * i + 1, rows1, sem_r1).wait()

            @pl.when(i + 1 < n_pairs)
            def _():
                read(2 * i + 2, rows0, sem_r0).start()

            scatter(2 * i + 1, rows1)

    return k(xn, slot1, slot2)


def _sc_collect(ys, slot1, slot2):
    m = slot1.size
    per = m // SC_WORKERS
    row = jax.ShapeDtypeStruct((m, HALF), I32)

    @functools.partial(
        pl.kernel, mesh=_sc_mesh(), out_type=[row, row],
        scratch_types=[pltpu.VMEM((per // SC_ROWS, SC_ROWS), I32), pltpu.VMEM((per // SC_ROWS, SC_ROWS), I32),
                       pltpu.VMEM((SC_ROWS, HALF), I32), pltpu.VMEM((SC_ROWS, HALF), I32),
                       pltpu.SemaphoreType.DMA, pltpu.SemaphoreType.DMA],
        name="sc_collect")
    def k(y_hbm, s1_hbm, s2_hbm, g1_hbm, g2_hbm, i1_v, i2_v, rows1, rows2, sem_g, sem_w):
        wid = _sc_worker()
        base = wid * per
        pltpu.sync_copy(s1_hbm.at[pl.ds(wid * (per // SC_ROWS), per // SC_ROWS)], i1_v)
        pltpu.sync_copy(s2_hbm.at[pl.ds(wid * (per // SC_ROWS), per // SC_ROWS)], i2_v)

        @pl.loop(0, per // SC_ROWS)
        def _(ci):
            t0 = base + ci * SC_ROWS
            a1 = pltpu.async_copy(y_hbm.at[i1_v.at[ci]], rows1, sem_g)
            a2 = pltpu.async_copy(y_hbm.at[i2_v.at[ci]], rows2, sem_g)
            a1.wait()
            a2.wait()
            w1 = pltpu.async_copy(rows1, g1_hbm.at[pl.ds(t0, SC_ROWS)], sem_w)
            w2 = pltpu.async_copy(rows2, g2_hbm.at[pl.ds(t0, SC_ROWS)], sem_w)
            w1.wait()
            w2.wait()

    return k(ys, slot1, slot2)


def _expert_kernel(te_ref, used_ref, xs_ref, wg_hbm, wu_hbm, wd_hbm, ys_ref, wg_b, wu_b, wd_b,
                   wg_f, wu_f, wd_f, sem, slot_s):
    j = pl.program_id(0)
    n = pl.num_programs(0)
    e = te_ref[j]
    prev = te_ref[jnp.maximum(j - 1, 0)]
    valid = e < N_EXPERTS

    def weight_copies(expert, slot):
        return [pltpu.make_async_copy(hbm.at[expert], buf.at[slot], sem.at[slot, i])
                for i, (hbm, buf) in enumerate(((wg_hbm, wg_f), (wu_hbm, wu_f), (wd_hbm, wd_f)))]

    @pl.when((j == 0) & valid)
    def _():
        slot_s[0] = 0
        for c in weight_copies(e, 0):
            c.start()

    @pl.when(((j == 0) | (e != prev)) & valid)
    def _():
        slot = slot_s[0]
        for c in weight_copies(e, slot):
            c.wait()
        k = lax.while_loop(lambda k: (k < n) & (te_ref[jnp.minimum(k, n - 1)] == e),
                           lambda k: k + 1, j + 1)
        nxt = te_ref[jnp.minimum(k, n - 1)]

        @pl.when((k < n) & (nxt < N_EXPERTS))
        def _():
            for c in weight_copies(nxt, 1 - slot):
                c.start()

        wg_b[...] = wg_f[slot].astype(BF16)
        wu_b[...] = wu_f[slot].astype(BF16)
        wd_b[...] = wd_f[slot].astype(BF16)
        slot_s[0] = 1 - slot

    @pl.when(valid)
    def _():
        half = SLOT_TILE // 2
        rows = [pl.ds(i * half, half) for i in range(2)]
        row_id = lax.broadcasted_iota(I32, (half, HALF), 0)
        xs = [_unpack_bf16_pairs(jnp.where(row_id + i * half < used_ref[j], xs_ref[r, :], 0))
              .astype(BF16) for i, r in enumerate(rows)]
        gs = [_dot(x, wg_b[...]) for x in xs]
        us = [_dot(x, wu_b[...]) for x in xs]
        hids = [(_silu(g) * u).astype(BF16) for g, u in zip(gs, us)]
        ys = [_dot(hid, wd_b[...]) for hid in hids]
        for r, y in zip(rows, ys):
            ys_ref[r, :] = _pack_bf16_pairs(y)

    @pl.when(e >= N_EXPERTS)
    def _():
        ys_ref[...] = jnp.zeros_like(ys_ref)


def _expert_call(tile_expert, tile_used, xs, wg, wu, wd, n_tiles):
    hbm = pl.BlockSpec(memory_space=pl.ANY)
    return pl.pallas_call(
        _expert_kernel,
        grid_spec=pltpu.PrefetchScalarGridSpec(
            num_scalar_prefetch=2,
            grid=(n_tiles,),
            in_specs=[pl.BlockSpec((SLOT_TILE, HALF), lambda j, te, used: (j, 0)), hbm, hbm, hbm],
            out_specs=pl.BlockSpec((SLOT_TILE, HALF), lambda j, te, used: (j, 0)),
            scratch_shapes=[
                pltpu.VMEM((D_MODEL, D_EXPERT), BF16), pltpu.VMEM((D_MODEL, D_EXPERT), BF16),
                pltpu.VMEM((D_EXPERT, D_MODEL), BF16),
                pltpu.VMEM((2, D_MODEL, D_EXPERT), F32), pltpu.VMEM((2, D_MODEL, D_EXPERT), F32),
                pltpu.VMEM((2, D_EXPERT, D_MODEL), F32),
                pltpu.SemaphoreType.DMA((2, 3)),
                pltpu.SMEM((1,), I32),
            ],
        ),
        out_shape=jax.ShapeDtypeStruct((n_tiles * SLOT_TILE, HALF), I32),
        compiler_params=pltpu.CompilerParams(
            dimension_semantics=("arbitrary",), vmem_limit_bytes=VMEM_LIMIT),
        name="experts",
    )(tile_expert, tile_used, xs, wg, wu, wd)


def _final_kernel(h_ref, g1_ref, g2_ref, rw_ref, nw_ref, o_ref):
    rw = rw_ref[...]
    y = rw[:, 0:1] * _unpack_bf16_pairs(g1_ref[...]) + rw[:, 1:2] * _unpack_bf16_pairs(g2_ref[...])
    h = h_ref[...] + y
    o_ref[...] = h * lax.rsqrt(jnp.mean(h * h, axis=-1, keepdims=True) + EPS) * nw_ref[...]


def _final_call(h1, g1, g2, rw, nw, tm=1024):
    m = h1.shape[0]
    return pl.pallas_call(
        _final_kernel,
        grid=(m // tm,),
        in_specs=[
            pl.BlockSpec((tm, D_MODEL), lambda i: (i, 0)),
            pl.BlockSpec((tm, HALF), lambda i: (i, 0)),
            pl.BlockSpec((tm, HALF), lambda i: (i, 0)),
            pl.BlockSpec((tm, LANES), lambda i: (i, 0)),
            pl.BlockSpec((1, D_MODEL), lambda i: (0, 0)),
        ],
        out_specs=pl.BlockSpec((tm, D_MODEL), lambda i: (i, 0)),
        out_shape=jax.ShapeDtypeStruct((m, D_MODEL), F32),
        compiler_params=pltpu.CompilerParams(dimension_semantics=("arbitrary",)),
        name="combine_final",
    )(h1, g1, g2, rw, nw)


def _pad_lanes(a):
    return jnp.pad(a, ((0, 0), (0, LANES - a.shape[1])))


def kernel(x, norm_mix_w, w_in, conv_w, A_log, dt_bias, gdn_norm_w, w_up_gdn, w_up_ret, w_out,
           norm_ffn_w, w_group, b_group, w_expert, b_expert, w_gate, w_up, w_down, norm_final_w):
    batch, seq, d = x.shape
    m = batch * seq
    h = x.reshape(m, d)
    depth = w_in.shape[0]
    sin, cos, inner, kdec, qdec, cdec = _retention_tables(seq)
    for l in range(depth):
        w_main, w_ab = _repack_call(jnp.transpose(w_in[l]))
        proj, ab = _proj_call(h, norm_mix_w[l][None, :], w_main, w_ab)

        conv8 = jnp.pad(conv_w[l], ((0, 8 - GDN_CONV), (0, 0)))
        ya = _gdn_call(proj, ab, conv8, _pad_lanes(A_log[l][None, :]), _pad_lanes(dt_bias[l][None, :]),
                       gdn_norm_w[l][None, :], batch, seq)
        yb = _ret_call(proj, sin, cos, inner, kdec, qdec, cdec, batch, seq)

        w_router = _pad_lanes(jnp.concatenate([w_expert[l], w_group[l]], axis=1))
        wr_hi = w_router.astype(BF16)
        wr_lo = (w_router - wr_hi.astype(F32)).astype(BF16)
        b_router = _pad_lanes(jnp.concatenate([b_expert[l], b_group[l]])[None, :])
        h1, xn, ridx, rw, cnt = _merge_call(
            h, ya, yb, proj, w_up_gdn[l].astype(BF16), w_up_ret[l].astype(BF16), w_out[l].astype(BF16),
            norm_ffn_w[l][None, :], jnp.concatenate([wr_hi, wr_lo], axis=1), b_router)

        n_slots, n_tiles = _slot_counts(m)
        slots, tiles = _plan_call(ridx, cnt)
        slot1, slot2 = slots[0].reshape(-1, SC_ROWS), slots[1].reshape(-1, SC_ROWS)
        xs = _sc_dispatch(xn, slot1, slot2, n_slots)
        ys = _expert_call(tiles[:n_tiles, 0], tiles[:n_tiles, 1], xs,
                          w_gate[l], w_up[l], w_down[l], n_tiles)
        g1, g2 = _sc_collect(ys, slot1, slot2)

        assert depth == 1
        h = _final_call(h1, g1, g2, rw, norm_final_w[None, :])
    return h.reshape(batch, seq, d)
```

```python
import functools

import jax
import jax.numpy as jnp
from jax import lax
from jax.experimental import pallas as pl
from jax.experimental.pallas import tpu as pltpu
from jax.experimental.pallas import tpu_sc as plsc

F32 = jnp.float32
BF16 = jnp.bfloat16
I32 = jnp.int32
U32 = jnp.uint32

D_MODEL = 1024
EPS = 1e-6
GDN_HEADS = 4
GDN_DK = 128
GDN_DV = 128
GDN_CONV = 4
RET_HEADS = 4
RET_DK = 128
RET_DV = 256
ROPE_BASE = 10000.0
N_GROUPS = 4
EXPERTS_PER_GROUP = 8
N_EXPERTS = N_GROUPS * EXPERTS_PER_GROUP
D_EXPERT = 512

GDN_QK = GDN_HEADS * GDN_DK
GDN_V = GDN_HEADS * GDN_DV
RET_QK = RET_HEADS * RET_DK
RET_V = RET_HEADS * RET_DV

LANES = 128
CHUNK = 128
INV_BLOCK = 16
GDN_PREP_CHUNKS = 2
VMEM_LIMIT = 56 * 1024 * 1024

REPACK_COLS = 512
TOP_K = 2
SLOT_TILE = 512
PLAN_TILE = 256
PLAN_STEP = 1024
HALF = D_MODEL // 2
SC_CORES = 2
SC_WORKERS = SC_CORES * 16
SC_ROWS = 64

PROJ_COLS = 3 * GDN_QK + GDN_V + 2 * RET_QK + 2 * RET_V + 2 * D_MODEL


def _silu(x):
    return x / (1.0 + jnp.exp(-x))


def _sigmoid(x):
    return 1.0 / (1.0 + jnp.exp(-x))


def _dot(a, b):
    return jnp.dot(a, b, preferred_element_type=F32)


def _dot_nt(a, b):
    return lax.dot_general(a, b, (((1,), (1,)), ((), ())), preferred_element_type=F32)


def _pack_bf16_pairs(x):
    bits = lax.bitcast_convert_type(x.astype(BF16).astype(F32), U32)
    packed = (bits[:, :HALF] >> 16) | (bits[:, HALF:] & jnp.uint32(0xFFFF0000))
    return lax.bitcast_convert_type(packed, I32)


def _unpack_bf16_pairs(p):
    p = lax.bitcast_convert_type(p, U32)
    lo = lax.bitcast_convert_type(p << 16, F32)
    hi = lax.bitcast_convert_type(p & jnp.uint32(0xFFFF0000), F32)
    return jnp.concatenate([lo, hi], axis=1)


def _proj_kernel(x_ref, nw_ref, w_ref, wab_ref, proj_ref, ab_ref, u_ref):
    j = pl.program_id(1)

    @pl.when(j == 0)
    def _():
        x = x_ref[...]
        u = x * lax.rsqrt(jnp.mean(x * x, axis=-1, keepdims=True) + EPS) * nw_ref[...]
        ub = u.astype(BF16)
        u_ref[...] = ub
        ab_ref[...] = _dot(ub, wab_ref[...])

    proj_ref[...] = _dot(u_ref[...], w_ref[...]).astype(BF16)


def _repack_kernel(wt_ref, abt_ref, main_ref, ab_ref):
    main_ref[...] = wt_ref[...].T.astype(BF16)

    @pl.when(pl.program_id(0) == 0)
    def _():
        ab = abt_ref[...].T
        ab_ref[...] = jnp.concatenate(
            [ab, jnp.zeros((ab.shape[0], LANES - ab.shape[1]), F32)], axis=1).astype(BF16)


def _repack_call(w_in_t):
    d_in, d = w_in_t.shape
    o_ab = 3 * GDN_QK
    n_ab = 2 * GDN_HEADS
    src = lambda r: pl.multiple_of(r * REPACK_COLS + jnp.where(r * REPACK_COLS >= o_ab, n_ab, 0), 8)
    return pl.pallas_call(
        _repack_kernel,
        grid=(PROJ_COLS // REPACK_COLS,),
        in_specs=[pl.BlockSpec((pl.Element(REPACK_COLS), pl.Element(d)), lambda r: (src(r), 0)),
                  pl.BlockSpec((pl.Element(n_ab), pl.Element(d)), lambda r: (o_ab, 0))],
        out_specs=[pl.BlockSpec((d, REPACK_COLS), lambda r: (0, r)),
                   pl.BlockSpec((d, LANES), lambda r: (0, 0))],
        out_shape=[jax.ShapeDtypeStruct((d, PROJ_COLS), BF16), jax.ShapeDtypeStruct((d, LANES), BF16)],
        compiler_params=pltpu.CompilerParams(dimension_semantics=("arbitrary",)),
        name="repack_w_in",
    )(w_in_t, w_in_t)


def _proj_call(x2, norm_w, w_main, w_ab, tm=1024, tn=3584):
    m = x2.shape[0]
    return pl.pallas_call(
        _proj_kernel,
        grid=(m // tm, PROJ_COLS // tn),
        in_specs=[
            pl.BlockSpec((tm, D_MODEL), lambda i, j: (i, 0)),
            pl.BlockSpec((1, D_MODEL), lambda i, j: (0, 0)),
            pl.BlockSpec((D_MODEL, tn), lambda i, j: (0, j)),
            pl.BlockSpec((D_MODEL, LANES), lambda i, j: (0, 0)),
        ],
        out_specs=[
            pl.BlockSpec((tm, tn), lambda i, j: (i, j)),
            pl.BlockSpec((tm, LANES), lambda i, j: (i, 0)),
        ],
        out_shape=[
            jax.ShapeDtypeStruct((m, PROJ_COLS), BF16),
            jax.ShapeDtypeStruct((m, LANES), F32),
        ],
        scratch_shapes=[pltpu.VMEM((tm, D_MODEL), BF16)],
        compiler_params=pltpu.CompilerParams(
            dimension_semantics=("arbitrary", "arbitrary"), vmem_limit_bytes=VMEM_LIMIT),
        name="proj",
    )(x2, norm_w, w_main, w_ab)


def _unit_lower_inverses(lows, ii, jj):
    eye = jnp.where(ii == jj, 1.0, 0.0).astype(F32)
    in_block = (ii // INV_BLOCK) == (jj // INV_BLOCK)
    ps = [jnp.where(in_block, -low, 0.0) for low in lows]
    ts = [eye + p for p in ps]
    span = 2
    while span < INV_BLOCK:
        ps = [_dot(p, p) for p in ps]
        ts = [t + _dot(t, p) for t, p in zip(ts, ps)]
        span *= 2
    s = INV_BLOCK
    while s < CHUNK:
        off_diag = ((ii // (2 * s)) == (jj // (2 * s))) & ((ii // s) != (jj // s))
        xs = [_dot(jnp.where(off_diag, low, 0.0), t) for low, t in zip(lows, ts)]
        ts = [t - _dot(t, x) for t, x in zip(ts, xs)]
        s *= 2
    return ts


def _gdn_kernel(qkv_ref, z_ref, ab_ref, convw_ref, alog_ref, dtb_ref, normw_ref, o_ref,
                b_s, o0_s, m_s, qp_s, gl_s, state_s, q_s, k_s, kb_s, rhs_s, dec_s, cv_s, qg_o, kdt_o,
                *, seq):
    nchunk = seq // CHUNK
    ii = lax.broadcasted_iota(jnp.int32, (CHUNK, CHUNK), 0)
    jj = lax.broadcasted_iota(jnp.int32, (CHUNK, CHUNK), 1)
    causal = ii >= jj
    strict = ii > jj
    tri = jnp.where(causal, 1.0, 0.0).astype(F32)
    neg_a = -jnp.exp(alog_ref[...])
    dtb = dtb_ref[...]

    def conv_cols(c, r0, lo, buf):
        x = qkv_ref[pl.ds(r0, CHUNK), lo:lo + LANES].astype(F32)
        prev0 = pl.multiple_of(jnp.maximum(r0 - 16, 0), 16)
        prev = qkv_ref[pl.ds(prev0, 16), lo:lo + LANES].astype(F32)
        buf[0:8, :] = prev[8:16] * jnp.where(c > 0, 1.0, 0.0)
        buf[8:8 + CHUNK, :] = x
        w = convw_ref[:, lo:lo + LANES]
        y = (w[3:4] * x + w[2:3] * buf[7:7 + CHUNK, :] + w[1:2] * buf[6:6 + CHUNK, :]
             + w[0:1] * buf[5:5 + CHUNK, :])
        return _silu(y)

    def l2n(x):
        return x * lax.rsqrt(jnp.sum(x * x, axis=-1, keepdims=True) + EPS)

    tri_b = tri.astype(BF16)

    def chunk_cumsum(g):
        g1 = g.astype(BF16)
        r1 = g - g1.astype(F32)
        g2 = r1.astype(BF16)
        g3 = (r1 - g2.astype(F32)).astype(BF16)
        return _dot(tri_b, g1) + (_dot(tri_b, g2) + _dot(tri_b, g3))

    def operands(cc, slot):
        for sub in range(GDN_PREP_CHUNKS):
            c = cc * GDN_PREP_CHUNKS + sub
            r0 = pl.multiple_of(c * CHUNK, CHUNK)
            ab = ab_ref[pl.ds(r0, CHUNK), :]
            xg = ab + dtb
            softplus = jnp.maximum(xg, 0.0) + jnp.log(1.0 + jnp.exp(-jnp.abs(xg)))
            g_all = neg_a * softplus
            beta_all = _sigmoid(ab)
            gc_all = chunk_cumsum(g_all)
            gc_t = gc_all.T
            gl_s[c] = jnp.exp(gc_all[CHUNK - 1:CHUNK, :])
            for h in range(GDN_HEADS):
                n = sub * GDN_HEADS + h
                bufs = [cv_s.at[slot, 3 * n + i] for i in range(3)]
                q = l2n(conv_cols(c, r0, h * GDN_DK, bufs[0])) * (GDN_DK ** -0.5)
                k = l2n(conv_cols(c, r0, GDN_QK + h * GDN_DK, bufs[1]))
                v = conv_cols(c, r0, 2 * GDN_QK + h * GDN_DV, bufs[2])
                gcol = gc_all[:, h:h + 1]
                grow = gc_t[h:h + 1, :]
                beta = beta_all[:, GDN_HEADS + h:GDN_HEADS + h + 1]
                dec_s[slot, n] = jnp.where(causal, jnp.exp(gcol - grow), 0.0)
                eg = jnp.exp(gcol)
                kb = k * beta
                q_s[slot, n] = q.astype(BF16)
                k_s[slot, n] = k.astype(BF16)
                kb_s[slot, n] = kb.astype(BF16)
                rhs_s[slot, n] = jnp.concatenate([v * beta, kb * eg], axis=1).astype(BF16)
                qg_o[slot, n] = q * eg
                kd = k * jnp.exp(gc_all[CHUNK - 1:CHUNK, h:h + 1] - gcol)
                kdt_o[slot, n] = kd.T.astype(BF16)

    def solve(cc, slot, between):
        items = [(cc * GDN_PREP_CHUNKS + sub, h, sub * GDN_HEADS + h)
                 for sub in range(GDN_PREP_CHUNKS) for h in range(GDN_HEADS)]
        kks = [_dot_nt(kb_s[slot, n], k_s[slot, n]) for _, _, n in items]
        qks = [_dot_nt(q_s[slot, n], k_s[slot, n]) for _, _, n in items]
        lows = [jnp.where(strict, kk * dec_s[slot, n], 0.0) for kk, (_, _, n) in zip(kks, items)]
        attns = [(qk * dec_s[slot, n]).astype(BF16) for qk, (_, _, n) in zip(qks, items)]
        for step in between[:len(between) // 2]:
            step()
        ts = _unit_lower_inverses(lows, ii, jj)
        for step in between[len(between) // 2:]:
            step()
        uws = [_dot(t.astype(BF16), rhs_s[slot, n]).astype(BF16) for t, (_, _, n) in zip(ts, items)]
        kds = [_dot(kdt_o[slot, n], uw) for uw, (_, _, n) in zip(uws, items)]
        ats = [_dot(attn, uw) for attn, uw in zip(attns, uws)]
        for kd_uw, at_uw, (c, h, n) in zip(kds, ats, items):
            b_s[c, h] = kd_uw[:, :GDN_DV]
            m_s[c, h] = (-kd_uw[:, GDN_DV:]).astype(BF16)
            o0_s[c, h] = at_uw[:, :GDN_DV]
            qp_s[c, h] = (qg_o[slot, n] - at_uw[:, GDN_DV:]).astype(BF16)

    ngroup = nchunk // GDN_PREP_CHUNKS
    per_trip = 2 * GDN_PREP_CHUNKS
    normw = normw_ref[...]
    state_s[...] = jnp.zeros_like(state_s)
    for c0 in range(per_trip):
        gl_s[c0] = jnp.zeros((1, LANES), F32)
        for h in range(GDN_HEADS):
            b_s[c0, h] = jnp.zeros((CHUNK, GDN_DV), F32)
            o0_s[c0, h] = jnp.zeros((CHUNK, GDN_DV), F32)
            m_s[c0, h] = jnp.zeros((CHUNK, GDN_DK), BF16)
            qp_s[c0, h] = jnp.zeros((CHUNK, GDN_DK), BF16)
    operands(0, 0)

    def prep(i, carry):
        first = jnp.maximum(per_trip * (i - 1), 0)
        steps = [functools.partial(scan, first + k, 0) for k in range(per_trip)]
        operands(2 * i + 1, 1)
        solve(2 * i, 0, steps[:per_trip // 2])
        operands(jnp.minimum(2 * i + 2, ngroup - 1), 0)
        solve(2 * i + 1, 1, steps[per_trip // 2:])
        return carry

    def scan(c, carry):
        r0 = pl.multiple_of(c * CHUNK, CHUNK)
        gl = gl_s[c]
        heads = range(GDN_HEADS)
        ss = [state_s[h] for h in heads]
        sbs = [s.astype(BF16) for s in ss]
        mss = [_dot(m_s[c, h], sbs[h]) for h in heads]
        qss = [_dot(qp_s[c, h], sbs[h]) for h in heads]
        for h in heads:
            state_s[h] = ss[h] * gl[:, h:h + 1] + (mss[h] + b_s[c, h])
            o = qss[h] + o0_s[c, h]
            z = z_ref[pl.ds(r0, CHUNK), h * GDN_DV:(h + 1) * GDN_DV].astype(F32)
            on = o * lax.rsqrt(jnp.mean(o * o, axis=-1, keepdims=True) + EPS) * normw
            o_ref[pl.ds(r0, CHUNK), h * GDN_DV:(h + 1) * GDN_DV] = (on * _silu(z)).astype(BF16)
        return carry

    ntrip = ngroup // 2
    lax.fori_loop(0, ntrip, prep, 0)
    lax.fori_loop(per_trip * (ntrip - 1), nchunk, scan, 0)


def _gdn_call(proj, ab, conv_w8, alog_p, dtb_p, normw, batch, seq):
    nchunk = seq // CHUNK
    hs = (nchunk, GDN_HEADS, CHUNK, CHUNK)
    ops = (2, GDN_PREP_CHUNKS * GDN_HEADS, CHUNK, CHUNK)
    return pl.pallas_call(
        functools.partial(_gdn_kernel, seq=seq),
        grid=(batch,),
        in_specs=[
            pl.BlockSpec((seq, 3 * GDN_QK), lambda b: (b, 0)),
            pl.BlockSpec((seq, GDN_V), lambda b: (b, 3)),
            pl.BlockSpec((seq, LANES), lambda b: (b, 0)),
            pl.BlockSpec((8, 3 * GDN_QK), lambda b: (0, 0)),
            pl.BlockSpec((1, LANES), lambda b: (0, 0)),
            pl.BlockSpec((1, LANES), lambda b: (0, 0)),
            pl.BlockSpec((1, GDN_DV), lambda b: (0, 0)),
        ],
        out_specs=pl.BlockSpec((seq, GDN_V), lambda b: (b, 0)),
        out_shape=jax.ShapeDtypeStruct((batch * seq, GDN_V), BF16),
        scratch_shapes=[
            pltpu.VMEM(hs, F32), pltpu.VMEM(hs, F32), pltpu.VMEM(hs, BF16), pltpu.VMEM(hs, BF16),
            pltpu.VMEM((nchunk, 1, LANES), F32),
            pltpu.VMEM((GDN_HEADS, GDN_DK, GDN_DV), F32),
            pltpu.VMEM(ops, BF16), pltpu.VMEM(ops, BF16), pltpu.VMEM(ops, BF16),
            pltpu.VMEM(ops[:3] + (2 * CHUNK,), BF16),
            pltpu.VMEM(ops, F32),
            pltpu.VMEM((2, 3 * ops[1], 8 + CHUNK, LANES), F32),
            pltpu.VMEM(ops, F32), pltpu.VMEM(ops, BF16),
        ],
        compiler_params=pltpu.CompilerParams(
            dimension_semantics=("arbitrary",), vmem_limit_bytes=VMEM_LIMIT),
        name="gdn",
    )(proj, proj, ab, conv_w8, alog_p, dtb_p, normw)


def _ret_kernel(qk_ref, v_ref, g_ref, sin_ref, cos_ref, inner_ref, kdec_ref, qdec_ref, cdec_ref,
                o_ref, state_s, q_s, qd_s, k_s, kt_s, *, seq):
    nchunk = seq // CHUNK
    lane = lax.broadcasted_iota(jnp.int32, (CHUNK, RET_DK), 1)
    even = (lane % 2) == 0

    def rotate(x, sin, cos):
        nxt = pltpu.roll(x, RET_DK - 1, axis=1)
        prv = pltpu.roll(x, 1, axis=1)
        return x * cos + jnp.where(even, -nxt, prv) * sin

    state_s[...] = jnp.zeros_like(state_s)
    kdec = kdec_ref[...]
    qdec = qdec_ref[...]
    cdec = cdec_ref[...]

    heads = range(RET_HEADS)

    def operands(c, slot):
        r0 = pl.multiple_of(c * CHUNK, CHUNK)
        sin = sin_ref[pl.ds(r0, CHUNK), :]
        cos = cos_ref[pl.ds(r0, CHUNK), :]
        for h in heads:
            q = rotate(qk_ref[pl.ds(r0, CHUNK), h * RET_DK:(h + 1) * RET_DK].astype(F32), sin, cos)
            k = rotate(qk_ref[pl.ds(r0, CHUNK), RET_QK + h * RET_DK:RET_QK + (h + 1) * RET_DK]
                       .astype(F32), sin, cos) * (RET_DK ** -0.5)
            q_s[slot, h] = q.astype(BF16)
            qd_s[slot, h] = (q * qdec[:, h:h + 1]).astype(BF16)
            k_s[slot, h] = k.astype(BF16)
            kt_s[slot, h] = (k * kdec[:, h:h + 1]).T.astype(BF16)

    def outputs(c, slot):
        r0 = pl.multiple_of(c * CHUNK, CHUNK)
        vs = [v_ref[pl.ds(r0, CHUNK), h * RET_DV:(h + 1) * RET_DV] for h in heads]
        ss = [state_s[h] for h in heads]
        qks = [_dot_nt(q_s[slot, h], k_s[slot, h]) for h in heads]
        inters = [_dot(qd_s[slot, h], ss[h].astype(BF16)) for h in heads]
        kvs = [_dot(kt_s[slot, h], vs[h]) for h in heads]
        intras = [_dot((qks[h] * inner_ref[h]).astype(BF16), vs[h]) for h in heads]
        for h in heads:
            state_s[h] = ss[h] * cdec[:, h:h + 1] + kvs[h]
            o = intras[h] + inters[h]
            gate = g_ref[pl.ds(r0, CHUNK), h * RET_DV:(h + 1) * RET_DV].astype(F32)
            on = o * lax.rsqrt(jnp.mean(o * o, axis=-1, keepdims=True) + EPS)
            o_ref[pl.ds(r0, CHUNK), h * RET_DV:(h + 1) * RET_DV] = (on * _silu(gate)).astype(BF16)

    operands(0, 0)

    def body(i, carry):
        operands(2 * i + 1, 1)
        outputs(2 * i, 0)
        operands(jnp.minimum(2 * i + 2, nchunk - 1), 0)
        outputs(2 * i + 1, 1)
        return carry

    lax.fori_loop(0, nchunk // 2, body, 0)


def _ret_call(proj, sin, cos, inner, kdec, qdec, cdec, batch, seq):
    return pl.pallas_call(
        functools.partial(_ret_kernel, seq=seq),
        grid=(batch,),
        in_specs=[
            pl.BlockSpec((seq, 2 * RET_QK), lambda b: (b, 2)),
            pl.BlockSpec((seq, RET_V), lambda b: (b, 3)),
            pl.BlockSpec((seq, RET_V), lambda b: (b, 4)),
            pl.BlockSpec((seq, RET_DK), lambda b: (0, 0)),
            pl.BlockSpec((seq, RET_DK), lambda b: (0, 0)),
            pl.BlockSpec((RET_HEADS, CHUNK, CHUNK), lambda b: (0, 0, 0)),
            pl.BlockSpec((CHUNK, LANES), lambda b: (0, 0)),
            pl.BlockSpec((CHUNK, LANES), lambda b: (0, 0)),
            pl.BlockSpec((1, LANES), lambda b: (0, 0)),
        ],
        out_specs=pl.BlockSpec((seq, RET_V), lambda b: (b, 0)),
        out_shape=jax.ShapeDtypeStruct((batch * seq, RET_V), BF16),
        scratch_shapes=[pltpu.VMEM((RET_HEADS, RET_DK, RET_DV), F32)]
        + [pltpu.VMEM((2, RET_HEADS, CHUNK, RET_DK), BF16)] * 4,
        compiler_params=pltpu.CompilerParams(
            dimension_semantics=("arbitrary",), vmem_limit_bytes=VMEM_LIMIT),
        name="retention",
    )(proj, proj, proj, sin, cos, inner, kdec, qdec, cdec)


def _retention_tables(seq):
    inv_freq = 1.0 / (ROPE_BASE ** jnp.linspace(0.0, 1.0, RET_DK // 2, dtype=F32))
    ang = jnp.arange(seq, dtype=F32)[:, None] * inv_freq[None, :]
    sin = jnp.repeat(jnp.sin(ang), 2, axis=-1)
    cos = jnp.repeat(jnp.cos(ang), 2, axis=-1)
    log_gamma = jnp.log(1.0 - 2.0 ** (-5.0 - jnp.arange(RET_HEADS, dtype=F32)))
    idx = jnp.arange(CHUNK, dtype=F32)
    causal = jnp.tril(jnp.ones((CHUNK, CHUNK), dtype=bool))
    rel = jnp.where(causal, idx[:, None] - idx[None, :], 0.0)
    inner = jnp.where(causal, jnp.exp(rel[None] * log_gamma[:, None, None]), 0.0)
    k_decay = jnp.exp(log_gamma[:, None] * (CHUNK - 1.0 - idx)[None, :])
    q_decay = jnp.exp(log_gamma[:, None] * (idx + 1.0)[None, :])
    chunk_decay = jnp.exp(log_gamma * CHUNK)
    pad = LANES - RET_HEADS
    kdec = jnp.pad(k_decay.T, ((0, 0), (0, pad)))
    qdec = jnp.pad(q_decay.T, ((0, 0), (0, pad)))
    cdec = jnp.pad(chunk_decay[None, :], ((0, 0), (0, pad)))
    return sin, cos, inner, kdec, qdec, cdec


def _merge_kernel(x_ref, ya_ref, yb_ref, ma_ref, mb_ref, wa_ref, wr_ref, wo_ref, nw_ref,
                  wrt_ref, br_ref, h_ref, xn_ref, ridx_ref, rw_ref, cnt_ref):
    tm = x_ref.shape[0]
    a = _dot(ya_ref[...], wa_ref[...])
    r = _dot(yb_ref[...], wr_ref[...])
    merged = _sigmoid(ma_ref[...].astype(F32)) * a + _sigmoid(mb_ref[...].astype(F32)) * r
    h = x_ref[...] + _dot(merged.astype(BF16), wo_ref[...])
    h_ref[...] = h.astype(BF16)
    xn = h * lax.rsqrt(jnp.mean(h * h, axis=-1, keepdims=True) + EPS) * nw_ref[...]
    xn_ref[...] = _pack_bf16_pairs(xn)
    xh = xn.astype(BF16)
    xl = (xn - xh.astype(F32)).astype(BF16)
    parts = _dot(jnp.concatenate([xh, xl], axis=0), wrt_ref[...])
    counts = _route(parts, tm, br_ref[...], ridx_ref, rw_ref)

    @pl.when(pl.program_id(0) == 0)
    def _():
        cnt_ref[...] = jnp.zeros_like(cnt_ref)

    cnt_ref[...] += jnp.broadcast_to(counts, cnt_ref.shape)


def _route(parts, tm, bias, ridx_ref, rw_ref):
    logits = (parts[:tm, :LANES] + (parts[tm:, :LANES] + parts[:tm, LANES:]
                                    + parts[tm:, LANES:])) + bias
    lane = lax.broadcasted_iota(jnp.int32, (tm, LANES), 1)
    neg = -jnp.inf
    gl = jnp.where((lane >= N_EXPERTS) & (lane < N_EXPERTS + N_GROUPS), logits, neg)
    gmax = jnp.max(gl, axis=-1, keepdims=True)
    gidx = jnp.min(jnp.where(gl == gmax, lane, LANES), axis=-1, keepdims=True) - N_EXPERTS
    g_w = 1.0 / jnp.sum(jnp.exp(gl - gmax), axis=-1, keepdims=True)
    el = jnp.where((lane // EXPERTS_PER_GROUP == gidx) & (lane < N_EXPERTS), logits, neg)
    m1 = jnp.max(el, axis=-1, keepdims=True)
    i1 = jnp.min(jnp.where(el == m1, lane, LANES), axis=-1, keepdims=True)
    el2 = jnp.where(lane == i1, neg, el)
    m2 = jnp.max(el2, axis=-1, keepdims=True)
    i2 = jnp.min(jnp.where(el2 == m2, lane, LANES), axis=-1, keepdims=True)
    e2 = jnp.exp(m2 - m1)
    p1 = g_w / (1.0 + e2)
    p2 = g_w * e2 / (1.0 + e2)
    ridx_ref[...] = jnp.where(lane == 0, i1, jnp.where(lane == 1, i2, 0))
    rw_ref[...] = jnp.where(lane == 0, p1, jnp.where(lane == 1, p2, 0.0))
    onehot = jnp.where((lane == i1) | (lane == i2), 1.0, 0.0)
    return jnp.sum(onehot, axis=0, keepdims=True)


def _merge_call(x2, ya, yb, proj, wa, wr, wo, nw, w_router, b_router, tm=1024):
    m = x2.shape[0]
    full = lambda shape: pl.BlockSpec(shape, lambda i: (0, 0))
    return pl.pallas_call(
        _merge_kernel,
        grid=(m // tm,),
        in_specs=[
            pl.BlockSpec((tm, D_MODEL), lambda i: (i, 0)),
            pl.BlockSpec((tm, GDN_V), lambda i: (i, 0)),
            pl.BlockSpec((tm, RET_V), lambda i: (i, 0)),
            pl.BlockSpec((tm, D_MODEL), lambda i: (i, 5)),
            pl.BlockSpec((tm, D_MODEL), lambda i: (i, 6)),
            full((GDN_V, D_MODEL)), full((RET_V, D_MODEL)), full((D_MODEL, D_MODEL)),
            full((1, D_MODEL)),
            full((D_MODEL, 2 * LANES)), full((1, LANES)),
        ],
        out_specs=[
            pl.BlockSpec((tm, D_MODEL), lambda i: (i, 0)),
            pl.BlockSpec((tm, HALF), lambda i: (i, 0)),
            pl.BlockSpec((tm, LANES), lambda i: (i, 0)),
            pl.BlockSpec((tm, LANES), lambda i: (i, 0)),
            pl.BlockSpec((8, LANES), lambda i: (0, 0)),
        ],
        out_shape=[
            jax.ShapeDtypeStruct((m, D_MODEL), BF16),
            jax.ShapeDtypeStruct((m, HALF), I32),
            jax.ShapeDtypeStruct((m, LANES), I32),
            jax.ShapeDtypeStruct((m, LANES), F32),
            jax.ShapeDtypeStruct((8, LANES), F32),
        ],
        compiler_params=pltpu.CompilerParams(
            dimension_semantics=("arbitrary",), vmem_limit_bytes=VMEM_LIMIT),
        name="merge_router",
    )(x2, ya, yb, proj, proj, wa, wr, wo, nw, w_router, b_router)


def _slot_counts(m):
    n_slots = TOP_K * m + N_EXPERTS * SLOT_TILE
    return n_slots, n_slots // SLOT_TILE


def _lane_prefix_sum(x, lane):
    s = 1
    while s < LANES:
        x = x + jnp.where(lane >= s, pltpu.roll(x, s, axis=1), 0.0)
        s *= 2
    return x


def _plan_kernel(ridx_ref, cnt_ref, slots_ref, tile_ref, carry_s, off_s):
    i = pl.program_id(0)
    lane = lax.broadcasted_iota(I32, (PLAN_TILE, LANES), 1)
    row = lax.broadcasted_iota(I32, (PLAN_TILE, LANES), 0)
    lane1 = lane[0:1]

    @pl.when(i == 0)
    def _():
        cnt = cnt_ref[0:1, :]
        tile = float(SLOT_TILE)
        padded = jnp.floor((cnt + (tile - 1.0)) / tile) * tile
        incl = _lane_prefix_sum(padded, lane1)
        off = incl - padded
        off_s[...] = off
        carry_s[...] = jnp.zeros_like(carry_s)
        first = (row * SLOT_TILE).astype(F32)
        ended = jnp.where((lane < N_EXPERTS) & (incl <= first), 1.0, 0.0)
        tile_e = jnp.sum(ended, axis=-1, keepdims=True)
        last = jnp.sum(jnp.where(lane.astype(F32) == tile_e, off + cnt, 0.0), axis=-1, keepdims=True)
        used = jnp.clip(last - first[:, 0:1], 0.0, tile)
        tile_ref[...] = jnp.where(lane == 0, tile_e, jnp.where(lane == 1, used, 0.0)).astype(I32)

    strict = jnp.where(row[:, 0:1] > lax.broadcasted_iota(I32, (PLAN_TILE, PLAN_TILE), 1),
                       1.0, 0.0).astype(BF16)
    off = off_s[...]
    carry = carry_s[...]
    for sb in range(PLAN_STEP // PLAN_TILE):
        rows = pl.ds(sb * PLAN_TILE, PLAN_TILE)
        e1 = ridx_ref[rows, 0:1]
        e2 = ridx_ref[rows, 1:2]
        onehot = jnp.where((lane == e1) | (lane == e2), 1.0, 0.0)
        pos = _dot(strict, onehot.astype(BF16)) + (carry + off)
        s1 = jnp.sum(jnp.where(lane == e1, pos, 0.0), axis=-1, keepdims=True)
        s2 = jnp.sum(jnp.where(lane == e2, pos, 0.0), axis=-1, keepdims=True)
        both = jnp.where(lane == 0, s1, jnp.where(lane == 1, s2, 0.0))
        for q in range(PLAN_TILE // LANES):
            t = both[q * LANES:(q + 1) * LANES].T
            c0 = sb * PLAN_TILE + q * LANES
            slots_ref[:, c0:c0 + LANES] = t[0:8].astype(I32)
        carry = carry + jnp.sum(onehot, axis=0, keepdims=True)
    carry_s[...] = carry


def _plan_call(ridx, cnt):
    m = ridx.shape[0]
    _, n_tiles = _slot_counts(m)
    assert n_tiles <= PLAN_TILE
    return pl.pallas_call(
        _plan_kernel,
        grid=(m // PLAN_STEP,),
        in_specs=[pl.BlockSpec((PLAN_STEP, LANES), lambda i: (i, 0)),
                  pl.BlockSpec((8, LANES), lambda i: (0, 0))],
        out_specs=[
            pl.BlockSpec((8, PLAN_STEP), lambda i: (0, i)),
            pl.BlockSpec((PLAN_TILE, LANES), lambda i: (0, 0)),
        ],
        out_shape=[
            jax.ShapeDtypeStruct((8, m), I32),
            jax.ShapeDtypeStruct((PLAN_TILE, LANES), I32),
        ],
        scratch_shapes=[pltpu.VMEM((1, LANES), F32), pltpu.VMEM((1, LANES), F32)],
        compiler_params=pltpu.CompilerParams(dimension_semantics=("arbitrary",)),
        name="dispatch_plan",
    )(ridx, cnt)


def _sc_mesh():
    return plsc.VectorSubcoreMesh(core_axis_name="c", subcore_axis_name="s")


def _sc_worker():
    return lax.axis_index("s") * SC_CORES + lax.axis_index("c")


def _sc_dispatch(xn, slot1, slot2, n_rows):
    m = xn.shape[0]
    per = m // SC_WORKERS
    n_pairs = per // (2 * SC_ROWS)

    @functools.partial(
        pl.kernel, mesh=_sc_mesh(),
        out_type=jax.ShapeDtypeStruct((n_rows, HALF), I32),
        scratch_types=[pltpu.VMEM((per // SC_ROWS, SC_ROWS), I32), pltpu.VMEM((per // SC_ROWS, SC_ROWS), I32),
                       pltpu.VMEM((SC_ROWS, HALF), I32), pltpu.VMEM((SC_ROWS, HALF), I32),
                       pltpu.SemaphoreType.DMA, pltpu.SemaphoreType.DMA, pltpu.SemaphoreType.DMA],
        name="sc_dispatch")
    def k(x_hbm, s1_hbm, s2_hbm, o_hbm, i1_v, i2_v, rows0, rows1, sem_r0, sem_r1, sem_w):
        wid = _sc_worker()
        base = wid * per

        def read(chunk, rows_v, sem):
            return pltpu.make_async_copy(x_hbm.at[pl.ds(base + chunk * SC_ROWS, SC_ROWS)], rows_v, sem)

        def scatter(chunk, rows_v):
            c1 = pltpu.async_copy(rows_v, o_hbm.at[i1_v.at[chunk]], sem_w)
            c2 = pltpu.async_copy(rows_v, o_hbm.at[i2_v.at[chunk]], sem_w)
            c1.wait()
            c2.wait()

        read(0, rows0, sem_r0).start()
        pltpu.sync_copy(s1_hbm.at[pl.ds(wid * (per // SC_ROWS), per // SC_ROWS)], i1_v)
        pltpu.sync_copy(s2_hbm.at[pl.ds(wid * (per // SC_ROWS), per // SC_ROWS)], i2_v)

        @pl.loop(0, n_pairs)
        def _(i):
            read(2 * i, rows0, sem_r0).wait()
            read(2 * i + 1, rows1, sem_r1).start()
            scatter(2 * i, rows0)
            read(2 * i + 1, rows1, sem_r1).wait()

            @pl.when(i + 1 < n_pairs)
            def _():
                read(2 * i + 2, rows0, sem_r0).start()

            scatter(2 * i + 1, rows1)

    return k(xn, slot1, slot2)


def _sc_collect(ys, slot1, slot2):
    m = slot1.size
    per = m // SC_WORKERS
    row = jax.ShapeDtypeStruct((m, HALF), I32)

    @functools.partial(
        pl.kernel, mesh=_sc_mesh(), out_type=[row, row],
        scratch_types=[pltpu.VMEM((per // SC_ROWS, SC_ROWS), I32), pltpu.VMEM((per // SC_ROWS, SC_ROWS), I32),
                       pltpu.VMEM((SC_ROWS, HALF), I32), pltpu.VMEM((SC_ROWS, HALF), I32),
                       pltpu.SemaphoreType.DMA, pltpu.SemaphoreType.DMA],
        name="sc_collect")
    def k(y_hbm, s1_hbm, s2_hbm, g1_hbm, g2_hbm, i1_v, i2_v, rows1, rows2, sem_g, sem_w):
        wid = _sc_worker()
        base = wid * per
        pltpu.sync_copy(s1_hbm.at[pl.ds(wid * (per // SC_ROWS), per // SC_ROWS)], i1_v)
        pltpu.sync_copy(s2_hbm.at[pl.ds(wid * (per // SC_ROWS), per // SC_ROWS)], i2_v)

        @pl.loop(0, per // SC_ROWS)
        def _(ci):
            t0 = base + ci * SC_ROWS
            a1 = pltpu.async_copy(y_hbm.at[i1_v.at[ci]], rows1, sem_g)
            a2 = pltpu.async_copy(y_hbm.at[i2_v.at[ci]], rows2, sem_g)
            a1.wait()
            a2.wait()
            w1 = pltpu.async_copy(rows1, g1_hbm.at[pl.ds(t0, SC_ROWS)], sem_w)
            w2 = pltpu.async_copy(rows2, g2_hbm.at[pl.ds(t0, SC_ROWS)], sem_w)
            w1.wait()
            w2.wait()

    return k(ys, slot1, slot2)


def _expert_kernel(te_ref, used_ref, xs_ref, wg_hbm, wu_hbm, wd_hbm, ys_ref, wg_b, wu_b, wd_b,
                   wg_f, wu_f, wd_f, sem, slot_s):
    j = pl.program_id(0)
    n = pl.num_programs(0)
    e = te_ref[j]
    prev = te_ref[jnp.maximum(j - 1, 0)]
    valid = e < N_EXPERTS

    def weight_copies(expert, slot):
        return [pltpu.make_async_copy(hbm.at[expert], buf.at[slot], sem.at[slot, i])
                for i, (hbm, buf) in enumerate(((wg_hbm, wg_f), (wu_hbm, wu_f), (wd_hbm, wd_f)))]

    @pl.when((j == 0) & valid)
    def _():
        slot_s[0] = 0
        for c in weight_copies(e, 0):
            c.start()

    @pl.when(((j == 0) | (e != prev)) & valid)
    def _():
        slot = slot_s[0]
        for c in weight_copies(e, slot):
            c.wait()
        k = lax.while_loop(lambda k: (k < n) & (te_ref[jnp.minimum(k, n - 1)] == e),
                           lambda k: k + 1, j + 1)
        nxt = te_ref[jnp.minimum(k, n - 1)]

        @pl.when((k < n) & (nxt < N_EXPERTS))
        def _():
            for c in weight_copies(nxt, 1 - slot):
                c.start()

        wg_b[...] = wg_f[slot].astype(BF16)
        wu_b[...] = wu_f[slot].astype(BF16)
        wd_b[...] = wd_f[slot].astype(BF16)
        slot_s[0] = 1 - slot

    @pl.when(valid)
    def _():
        half = SLOT_TILE // 2
        rows = [pl.ds(i * half, half) for i in range(2)]
        row_id = lax.broadcasted_iota(I32, (half, HALF), 0)
        xs = [_unpack_bf16_pairs(jnp.where(row_id + i * half < used_ref[j], xs_ref[r, :], 0))
              .astype(BF16) for i, r in enumerate(rows)]
        gs = [_dot(x, wg_b[...]) for x in xs]
        us = [_dot(x, wu_b[...]) for x in xs]
        hids = [(_silu(g) * u).astype(BF16) for g, u in zip(gs, us)]
        ys = [_dot(hid, wd_b[...]) for hid in hids]
        for r, y in zip(rows, ys):
            ys_ref[r, :] = _pack_bf16_pairs(y)

    @pl.when(e >= N_EXPERTS)
    def _():
        ys_ref[...] = jnp.zeros_like(ys_ref)


def _expert_call(tile_expert, tile_used, xs, wg, wu, wd, n_tiles):
    hbm = pl.BlockSpec(memory_space=pl.ANY)
    return pl.pallas_call(
        _expert_kernel,
        grid_spec=pltpu.PrefetchScalarGridSpec(
            num_scalar_prefetch=2,
            grid=(n_tiles,),
            in_specs=[pl.BlockSpec((SLOT_TILE, HALF), lambda j, te, used: (j, 0)), hbm, hbm, hbm],
            out_specs=pl.BlockSpec((SLOT_TILE, HALF), lambda j, te, used: (j, 0)),
            scratch_shapes=[
                pltpu.VMEM((D_MODEL, D_EXPERT), BF16), pltpu.VMEM((D_MODEL, D_EXPERT), BF16),
                pltpu.VMEM((D_EXPERT, D_MODEL), BF16),
                pltpu.VMEM((2, D_MODEL, D_EXPERT), F32), pltpu.VMEM((2, D_MODEL, D_EXPERT), F32),
                pltpu.VMEM((2, D_EXPERT, D_MODEL), F32),
                pltpu.SemaphoreType.DMA((2, 3)),
                pltpu.SMEM((1,), I32),
            ],
        ),
        out_shape=jax.ShapeDtypeStruct((n_tiles * SLOT_TILE, HALF), I32),
        compiler_params=pltpu.CompilerParams(
            dimension_semantics=("arbitrary",), vmem_limit_bytes=VMEM_LIMIT),
        name="experts",
    )(tile_expert, tile_used, xs, wg, wu, wd)


def _final_kernel(h_ref, g1_ref, g2_ref, rw_ref, nw_ref, o_ref):
    rw = rw_ref[...]
    y = rw[:, 0:1] * _unpack_bf16_pairs(g1_ref[...]) + rw[:, 1:2] * _unpack_bf16_pairs(g2_ref[...])
    h = h_ref[...].astype(F32) + y
    o_ref[...] = h * lax.rsqrt(jnp.mean(h * h, axis=-1, keepdims=True) + EPS) * nw_ref[...]


def _final_call(h1, g1, g2, rw, nw, tm=1024):
    m = h1.shape[0]
    return pl.pallas_call(
        _final_kernel,
        grid=(m // tm,),
        in_specs=[
            pl.BlockSpec((tm, D_MODEL), lambda i: (i, 0)),
            pl.BlockSpec((tm, HALF), lambda i: (i, 0)),
            pl.BlockSpec((tm, HALF), lambda i: (i, 0)),
            pl.BlockSpec((tm, LANES), lambda i: (i, 0)),
            pl.BlockSpec((1, D_MODEL), lambda i: (0, 0)),
        ],
        out_specs=pl.BlockSpec((tm, D_MODEL), lambda i: (i, 0)),
        out_shape=jax.ShapeDtypeStruct((m, D_MODEL), F32),
        compiler_params=pltpu.CompilerParams(dimension_semantics=("arbitrary",)),
        name="combine_final",
    )(h1, g1, g2, rw, nw)


def _pad_lanes(a):
    return jnp.pad(a, ((0, 0), (0, LANES - a.shape[1])))


def kernel(x, norm_mix_w, w_in, conv_w, A_log, dt_bias, gdn_norm_w, w_up_gdn, w_up_ret, w_out,
           norm_ffn_w, w_group, b_group, w_expert, b_expert, w_gate, w_up, w_down, norm_final_w):
    batch, seq, d = x.shape
    m = batch * seq
    h = x.reshape(m, d)
    depth = w_in.shape[0]
    sin, cos, inner, kdec, qdec, cdec = _retention_tables(seq)
    for l in range(depth):
        w_main, w_ab = _repack_call(jnp.transpose(w_in[l]))
        proj, ab = _proj_call(h, norm_mix_w[l][None, :], w_main, w_ab)

        conv8 = jnp.pad(conv_w[l], ((0, 8 - GDN_CONV), (0, 0)))
        ya = _gdn_call(proj, ab, conv8, _pad_lanes(A_log[l][None, :]), _pad_lanes(dt_bias[l][None, :]),
                       gdn_norm_w[l][None, :], batch, seq)
        yb = _ret_call(proj, sin, cos, inner, kdec, qdec, cdec, batch, seq)

        w_router = _pad_lanes(jnp.concatenate([w_expert[l], w_group[l]], axis=1))
        wr_hi = w_router.astype(BF16)
        wr_lo = (w_router - wr_hi.astype(F32)).astype(BF16)
        b_router = _pad_lanes(jnp.concatenate([b_expert[l], b_group[l]])[None, :])
        h1, xn, ridx, rw, cnt = _merge_call(
            h, ya, yb, proj, w_up_gdn[l].astype(BF16), w_up_ret[l].astype(BF16), w_out[l].astype(BF16),
            norm_ffn_w[l][None, :], jnp.concatenate([wr_hi, wr_lo], axis=1), b_router)

        n_slots, n_tiles = _slot_counts(m)
        slots, tiles = _plan_call(ridx, cnt)
        slot1, slot2 = slots[0].reshape(-1, SC_ROWS), slots[1].reshape(-1, SC_ROWS)
        xs = _sc_dispatch(xn, slot1, slot2, n_slots)
        ys = _expert_call(tiles[:n_tiles, 0], tiles[:n_tiles, 1], xs,
                          w_gate[l], w_up[l], w_down[l], n_tiles)
        g1, g2 = _sc_collect(ys, slot1, slot2)

        assert depth == 1
        h = _final_call(h1, g1, g2, rw, norm_final_w[None, :])
    return h.reshape(batch, seq, d)
```

```python
import functools

import jax
import jax.numpy as jnp
from jax import lax
from jax.experimental import pallas as pl
from jax.experimental.pallas import tpu as pltpu
from jax.experimental.pallas import tpu_sc as plsc

F32 = jnp.float32
BF16 = jnp.bfloat16
I32 = jnp.int32
U32 = jnp.uint32

D_MODEL = 1024
EPS = 1e-6
GDN_HEADS = 4
GDN_DK = 128
GDN_DV = 128
GDN_CONV = 4
RET_HEADS = 4
RET_DK = 128
RET_DV = 256
ROPE_BASE = 10000.0
N_GROUPS = 4
EXPERTS_PER_GROUP = 8
N_EXPERTS = N_GROUPS * EXPERTS_PER_GROUP
D_EXPERT = 512

GDN_QK = GDN_HEADS * GDN_DK
GDN_V = GDN_HEADS * GDN_DV
RET_QK = RET_HEADS * RET_DK
RET_V = RET_HEADS * RET_DV

LANES = 128
CHUNK = 128
INV_BLOCK = 16
GDN_PREP_CHUNKS = 2
VMEM_LIMIT = 56 * 1024 * 1024

REPACK_COLS = 512
TOP_K = 2
SLOT_TILE = 512
PLAN_TILE = 256
PLAN_STEP = 1024
HALF = D_MODEL // 2
SC_CORES = 2
SC_WORKERS = SC_CORES * 16
SC_ROWS = 64

PROJ_COLS = 3 * GDN_QK + GDN_V + 2 * RET_QK + 2 * RET_V + 2 * D_MODEL


def _silu(x):
    return x / (1.0 + jnp.exp(-x))


def _sigmoid(x):
    return 1.0 / (1.0 + jnp.exp(-x))


def _dot(a, b):
    return jnp.dot(a, b, preferred_element_type=F32)


def _dot_nt(a, b):
    return lax.dot_general(a, b, (((1,), (1,)), ((), ())), preferred_element_type=F32)


def _pack_bf16_pairs(x):
    bits = lax.bitcast_convert_type(x.astype(BF16).astype(F32), U32)
    packed = (bits[:, :HALF] >> 16) | (bits[:, HALF:] & jnp.uint32(0xFFFF0000))
    return lax.bitcast_convert_type(packed, I32)


def _unpack_bf16_pairs(p):
    p = lax.bitcast_convert_type(p, U32)
    lo = lax.bitcast_convert_type(p << 16, F32)
    hi = lax.bitcast_convert_type(p & jnp.uint32(0xFFFF0000), F32)
    return jnp.concatenate([lo, hi], axis=1)


def _proj_kernel(x_ref, nw_ref, w_ref, wab_ref, proj_ref, ab_ref, u_ref):
    j = pl.program_id(1)

    @pl.when(j == 0)
    def _():
        x = x_ref[...]
        u = x * lax.rsqrt(jnp.mean(x * x, axis=-1, keepdims=True) + EPS) * nw_ref[...]
        ub = u.astype(BF16)
        u_ref[...] = ub
        ab_ref[...] = _dot(ub, wab_ref[...])

    proj_ref[...] = _dot(u_ref[...], w_ref[...]).astype(BF16)


def _repack_kernel(wt_ref, abt_ref, main_ref, ab_ref):
    main_ref[...] = wt_ref[...].T.astype(BF16)

    @pl.when(pl.program_id(0) == 0)
    def _():
        ab = abt_ref[...].T
        ab_ref[...] = jnp.concatenate(
            [ab, jnp.zeros((ab.shape[0], LANES - ab.shape[1]), F32)], axis=1).astype(BF16)


def _repack_call(w_in_t):
    d_in, d = w_in_t.shape
    o_ab = 3 * GDN_QK
    n_ab = 2 * GDN_HEADS
    src = lambda r: pl.multiple_of(r * REPACK_COLS + jnp.where(r * REPACK_COLS >= o_ab, n_ab, 0), 8)
    return pl.pallas_call(
        _repack_kernel,
        grid=(PROJ_COLS // REPACK_COLS,),
        in_specs=[pl.BlockSpec((pl.Element(REPACK_COLS), pl.Element(d)), lambda r: (src(r), 0)),
                  pl.BlockSpec((pl.Element(n_ab), pl.Element(d)), lambda r: (o_ab, 0))],
        out_specs=[pl.BlockSpec((d, REPACK_COLS), lambda r: (0, r)),
                   pl.BlockSpec((d, LANES), lambda r: (0, 0))],
        out_shape=[jax.ShapeDtypeStruct((d, PROJ_COLS), BF16), jax.ShapeDtypeStruct((d, LANES), BF16)],
        compiler_params=pltpu.CompilerParams(dimension_semantics=("arbitrary",)),
        name="repack_w_in",
    )(w_in_t, w_in_t)


def _proj_call(x2, norm_w, w_main, w_ab, tm=1024, tn=3584):
    m = x2.shape[0]
    return pl.pallas_call(
        _proj_kernel,
        grid=(m // tm, PROJ_COLS // tn),
        in_specs=[
            pl.BlockSpec((tm, D_MODEL), lambda i, j: (i, 0)),
            pl.BlockSpec((1, D_MODEL), lambda i, j: (0, 0)),
            pl.BlockSpec((D_MODEL, tn), lambda i, j: (0, j)),
            pl.BlockSpec((D_MODEL, LANES), lambda i, j: (0, 0)),
        ],
        out_specs=[
            pl.BlockSpec((tm, tn), lambda i, j: (i, j)),
            pl.BlockSpec((tm, LANES), lambda i, j: (i, 0)),
        ],
        out_shape=[
            jax.ShapeDtypeStruct((m, PROJ_COLS), BF16),
            jax.ShapeDtypeStruct((m, LANES), F32),
        ],
        scratch_shapes=[pltpu.VMEM((tm, D_MODEL), BF16)],
        compiler_params=pltpu.CompilerParams(
            dimension_semantics=("arbitrary", "arbitrary"), vmem_limit_bytes=VMEM_LIMIT),
        name="proj",
    )(x2, norm_w, w_main, w_ab)


def _unit_lower_inverses(lows, ii, jj):
    eye = jnp.where(ii == jj, 1.0, 0.0).astype(F32)
    in_block = (ii // INV_BLOCK) == (jj // INV_BLOCK)
    ps = [jnp.where(in_block, -low, 0.0) for low in lows]
    ts = [eye + p for p in ps]
    span = 2
    while span < INV_BLOCK:
        ps = [_dot(p, p) for p in ps]
        ts = [t + _dot(t, p) for t, p in zip(ts, ps)]
        span *= 2
    s = INV_BLOCK
    while s < CHUNK:
        off_diag = ((ii // (2 * s)) == (jj // (2 * s))) & ((ii // s) != (jj // s))
        xs = [_dot(jnp.where(off_diag, low, 0.0), t) for low, t in zip(lows, ts)]
        ts = [t - _dot(t, x) for t, x in zip(ts, xs)]
        s *= 2
    return ts


def _gdn_kernel(qkv_ref, z_ref, ab_ref, convw_ref, alog_ref, dtb_ref, normw_ref, o_ref,
                b_s, o0_s, m_s, qp_s, gl_s, state_s, q_s, k_s, kb_s, rhs_s, dec_s, cv_s, qg_o, kdt_o,
                *, seq):
    nchunk = seq // CHUNK
    ii = lax.broadcasted_iota(jnp.int32, (CHUNK, CHUNK), 0)
    jj = lax.broadcasted_iota(jnp.int32, (CHUNK, CHUNK), 1)
    causal = ii >= jj
    strict = ii > jj
    tri = jnp.where(causal, 1.0, 0.0).astype(F32)
    neg_a = -jnp.exp(alog_ref[...])
    dtb = dtb_ref[...]

    def conv_cols(c, r0, lo, buf):
        x = qkv_ref[pl.ds(r0, CHUNK), lo:lo + LANES].astype(F32)
        prev0 = pl.multiple_of(jnp.maximum(r0 - 16, 0), 16)
        prev = qkv_ref[pl.ds(prev0, 16), lo:lo + LANES].astype(F32)
        buf[0:8, :] = prev[8:16] * jnp.where(c > 0, 1.0, 0.0)
        buf[8:8 + CHUNK, :] = x
        w = convw_ref[:, lo:lo + LANES]
        y = (w[3:4] * x + w[2:3] * buf[7:7 + CHUNK, :] + w[1:2] * buf[6:6 + CHUNK, :]
             + w[0:1] * buf[5:5 + CHUNK, :])
        return _silu(y)

    def l2n(x):
        return x * lax.rsqrt(jnp.sum(x * x, axis=-1, keepdims=True) + EPS)

    tri_b = tri.astype(BF16)

    def chunk_cumsum(g):
        g1 = g.astype(BF16)
        r1 = g - g1.astype(F32)
        g2 = r1.astype(BF16)
        g3 = (r1 - g2.astype(F32)).astype(BF16)
        return _dot(tri_b, g1) + (_dot(tri_b, g2) + _dot(tri_b, g3))

    def operands(cc, slot):
        for sub in range(GDN_PREP_CHUNKS):
            c = cc * GDN_PREP_CHUNKS + sub
            r0 = pl.multiple_of(c * CHUNK, CHUNK)
            ab = ab_ref[pl.ds(r0, CHUNK), :]
            xg = ab + dtb
            softplus = jnp.maximum(xg, 0.0) + jnp.log(1.0 + jnp.exp(-jnp.abs(xg)))
            g_all = neg_a * softplus
            beta_all = _sigmoid(ab)
            gc_all = chunk_cumsum(g_all)
            gc_t = gc_all.T
            gl_s[c] = jnp.exp(gc_all[CHUNK - 1:CHUNK, :])
            for h in range(GDN_HEADS):
                n = sub * GDN_HEADS + h
                bufs = [cv_s.at[slot, 3 * n + i] for i in range(3)]
                q = l2n(conv_cols(c, r0, h * GDN_DK, bufs[0])) * (GDN_DK ** -0.5)
                k = l2n(conv_cols(c, r0, GDN_QK + h * GDN_DK, bufs[1]))
                v = conv_cols(c, r0, 2 * GDN_QK + h * GDN_DV, bufs[2])
                gcol = gc_all[:, h:h + 1]
                grow = gc_t[h:h + 1, :]
                beta = beta_all[:, GDN_HEADS + h:GDN_HEADS + h + 1]
                dec_s[slot, n] = jnp.where(causal, jnp.exp(gcol - grow), 0.0)
                eg = jnp.exp(gcol)
                kb = k * beta
                q_s[slot, n] = q.astype(BF16)
                k_s[slot, n] = k.astype(BF16)
                kb_s[slot, n] = kb.astype(BF16)
                rhs_s[slot, n] = jnp.concatenate([v * beta, kb * eg], axis=1).astype(BF16)
                qg_o[slot, n] = q * eg
                kd = k * jnp.exp(gc_all[CHUNK - 1:CHUNK, h:h + 1] - gcol)
                kdt_o[slot, n] = kd.T.astype(BF16)

    def solve(cc, slot, between):
        items = [(cc * GDN_PREP_CHUNKS + sub, h, sub * GDN_HEADS + h)
                 for sub in range(GDN_PREP_CHUNKS) for h in range(GDN_HEADS)]
        kks = [_dot_nt(kb_s[slot, n], k_s[slot, n]) for _, _, n in items]
        qks = [_dot_nt(q_s[slot, n], k_s[slot, n]) for _, _, n in items]
        lows = [jnp.where(strict, kk * dec_s[slot, n], 0.0) for kk, (_, _, n) in zip(kks, items)]
        attns = [(qk * dec_s[slot, n]).astype(BF16) for qk, (_, _, n) in zip(qks, items)]
        for step in between[:len(between) // 2]:
            step()
        ts = _unit_lower_inverses(lows, ii, jj)
        for step in between[len(between) // 2:]:
            step()
        uws = [_dot(t.astype(BF16), rhs_s[slot, n]).astype(BF16) for t, (_, _, n) in zip(ts, items)]
        kds = [_dot(kdt_o[slot, n], uw) for uw, (_, _, n) in zip(uws, items)]
        ats = [_dot(attn, uw) for attn, uw in zip(attns, uws)]
        for kd_uw, at_uw, (c, h, n) in zip(kds, ats, items):
            b_s[c, h] = kd_uw[:, :GDN_DV]
            m_s[c, h] = (-kd_uw[:, GDN_DV:]).astype(BF16)
            o0_s[c, h] = at_uw[:, :GDN_DV]
            qp_s[c, h] = (qg_o[slot, n] - at_uw[:, GDN_DV:]).astype(BF16)

    ngroup = nchunk // GDN_PREP_CHUNKS
    per_trip = 2 * GDN_PREP_CHUNKS
    normw = normw_ref[...]
    state_s[...] = jnp.zeros_like(state_s)
    for c0 in range(per_trip):
        gl_s[c0] = jnp.zeros((1, LANES), F32)
        for h in range(GDN_HEADS):
            b_s[c0, h] = jnp.zeros((CHUNK, GDN_DV), F32)
            o0_s[c0, h] = jnp.zeros((CHUNK, GDN_DV), F32)
            m_s[c0, h] = jnp.zeros((CHUNK, GDN_DK), BF16)
            qp_s[c0, h] = jnp.zeros((CHUNK, GDN_DK), BF16)
    operands(0, 0)

    def prep(i, carry):
        first = jnp.maximum(per_trip * (i - 1), 0)
        steps = [functools.partial(scan, first + k, 0) for k in range(per_trip)]
        operands(2 * i + 1, 1)
        solve(2 * i, 0, steps[:per_trip // 2])
        operands(jnp.minimum(2 * i + 2, ngroup - 1), 0)
        solve(2 * i + 1, 1, steps[per_trip // 2:])
        return carry

    def scan(c, carry):
        r0 = pl.multiple_of(c * CHUNK, CHUNK)
        gl = gl_s[c]
        heads = range(GDN_HEADS)
        ss = [state_s[h] for h in heads]
        sbs = [s.astype(BF16) for s in ss]
        mss = [_dot(m_s[c, h], sbs[h]) for h in heads]
        qss = [_dot(qp_s[c, h], sbs[h]) for h in heads]
        for h in heads:
            state_s[h] = ss[h] * gl[:, h:h + 1] + (mss[h] + b_s[c, h])
            o = qss[h] + o0_s[c, h]
            z = z_ref[pl.ds(r0, CHUNK), h * GDN_DV:(h + 1) * GDN_DV].astype(F32)
            on = o * lax.rsqrt(jnp.mean(o * o, axis=-1, keepdims=True) + EPS) * normw
            o_ref[pl.ds(r0, CHUNK), h * GDN_DV:(h + 1) * GDN_DV] = (on * _silu(z)).astype(BF16)
        return carry

    ntrip = ngroup // 2
    lax.fori_loop(0, ntrip, prep, 0)
    lax.fori_loop(per_trip * (ntrip - 1), nchunk, scan, 0)


def _gdn_call(proj, ab, conv_w8, alog_p, dtb_p, normw, batch, seq):
    nchunk = seq // CHUNK
    hs = (nchunk, GDN_HEADS, CHUNK, CHUNK)
    ops = (2, GDN_PREP_CHUNKS * GDN_HEADS, CHUNK, CHUNK)
    return pl.pallas_call(
        functools.partial(_gdn_kernel, seq=seq),
        grid=(batch,),
        in_specs=[
            pl.BlockSpec((seq, 3 * GDN_QK), lambda b: (b, 0)),
            pl.BlockSpec((seq, GDN_V), lambda b: (b, 3)),
            pl.BlockSpec((seq, LANES), lambda b: (b, 0)),
            pl.BlockSpec((8, 3 * GDN_QK), lambda b: (0, 0)),
            pl.BlockSpec((1, LANES), lambda b: (0, 0)),
            pl.BlockSpec((1, LANES), lambda b: (0, 0)),
            pl.BlockSpec((1, GDN_DV), lambda b: (0, 0)),
        ],
        out_specs=pl.BlockSpec((seq, GDN_V), lambda b: (b, 0)),
        out_shape=jax.ShapeDtypeStruct((batch * seq, GDN_V), BF16),
        scratch_shapes=[
            pltpu.VMEM(hs, F32), pltpu.VMEM(hs, F32), pltpu.VMEM(hs, BF16), pltpu.VMEM(hs, BF16),
            pltpu.VMEM((nchunk, 1, LANES), F32),
            pltpu.VMEM((GDN_HEADS, GDN_DK, GDN_DV), F32),
            pltpu.VMEM(ops, BF16), pltpu.VMEM(ops, BF16), pltpu.VMEM(ops, BF16),
            pltpu.VMEM(ops[:3] + (2 * CHUNK,), BF16),
            pltpu.VMEM(ops, F32),
            pltpu.VMEM((2, 3 * ops[1], 8 + CHUNK, LANES), F32),
            pltpu.VMEM(ops, F32), pltpu.VMEM(ops, BF16),
        ],
        compiler_params=pltpu.CompilerParams(
            dimension_semantics=("arbitrary",), vmem_limit_bytes=VMEM_LIMIT),
        name="gdn",
    )(proj, proj, ab, conv_w8, alog_p, dtb_p, normw)


def _ret_kernel(qk_ref, v_ref, g_ref, sin_ref, cos_ref, inner_ref, kdec_ref, qdec_ref, cdec_ref,
                o_ref, state_s, q_s, qd_s, k_s, kt_s, *, seq):
    nchunk = seq // CHUNK
    lane = lax.broadcasted_iota(jnp.int32, (CHUNK, RET_DK), 1)
    even = (lane % 2) == 0

    def rotate(x, sin, cos):
        nxt = pltpu.roll(x, RET_DK - 1, axis=1)
        prv = pltpu.roll(x, 1, axis=1)
        return x * cos + jnp.where(even, -nxt, prv) * sin

    state_s[...] = jnp.zeros_like(state_s)
    kdec = kdec_ref[...]
    qdec = qdec_ref[...]
    cdec = cdec_ref[...]

    heads = range(RET_HEADS)

    def operands(c, slot):
        r0 = pl.multiple_of(c * CHUNK, CHUNK)
        sin = sin_ref[pl.ds(r0, CHUNK), :]
        cos = cos_ref[pl.ds(r0, CHUNK), :]
        for h in heads:
            q = rotate(qk_ref[pl.ds(r0, CHUNK), h * RET_DK:(h + 1) * RET_DK].astype(F32), sin, cos)
            k = rotate(qk_ref[pl.ds(r0, CHUNK), RET_QK + h * RET_DK:RET_QK + (h + 1) * RET_DK]
                       .astype(F32), sin, cos) * (RET_DK ** -0.5)
            q_s[slot, h] = q.astype(BF16)
            qd_s[slot, h] = (q * qdec[:, h:h + 1]).astype(BF16)
            k_s[slot, h] = k.astype(BF16)
            kt_s[slot, h] = (k * kdec[:, h:h + 1]).T.astype(BF16)

    def outputs(c, slot):
        r0 = pl.multiple_of(c * CHUNK, CHUNK)
        vs = [v_ref[pl.ds(r0, CHUNK), h * RET_DV:(h + 1) * RET_DV] for h in heads]
        ss = [state_s[h] for h in heads]
        qks = [_dot_nt(q_s[slot, h], k_s[slot, h]) for h in heads]
        inters = [_dot(qd_s[slot, h], ss[h].astype(BF16)) for h in heads]
        kvs = [_dot(kt_s[slot, h], vs[h]) for h in heads]
        intras = [_dot((qks[h] * inner_ref[h]).astype(BF16), vs[h]) for h in heads]
        for h in heads:
            state_s[h] = ss[h] * cdec[:, h:h + 1] + kvs[h]
            o = intras[h] + inters[h]
            gate = g_ref[pl.ds(r0, CHUNK), h * RET_DV:(h + 1) * RET_DV].astype(F32)
            on = o * lax.rsqrt(jnp.mean(o * o, axis=-1, keepdims=True) + EPS)
            o_ref[pl.ds(r0, CHUNK), h * RET_DV:(h + 1) * RET_DV] = (on * _silu(gate)).astype(BF16)

    operands(0, 0)

    def body(i, carry):
        operands(2 * i + 1, 1)
        outputs(2 * i, 0)
        operands(jnp.minimum(2 * i + 2, nchunk - 1), 0)
        outputs(2 * i + 1, 1)
        return carry

    lax.fori_loop(0, nchunk // 2, body, 0)


def _ret_call(proj, sin, cos, inner, kdec, qdec, cdec, batch, seq):
    return pl.pallas_call(
        functools.partial(_ret_kernel, seq=seq),
        grid=(batch,),
        in_specs=[
            pl.BlockSpec((seq, 2 * RET_QK), lambda b: (b, 2)),
            pl.BlockSpec((seq, RET_V), lambda b: (b, 3)),
            pl.BlockSpec((seq, RET_V), lambda b: (b, 4)),
            pl.BlockSpec((seq, RET_DK), lambda b: (0, 0)),
            pl.BlockSpec((seq, RET_DK), lambda b: (0, 0)),
            pl.BlockSpec((RET_HEADS, CHUNK, CHUNK), lambda b: (0, 0, 0)),
            pl.BlockSpec((CHUNK, LANES), lambda b: (0, 0)),
            pl.BlockSpec((CHUNK, LANES), lambda b: (0, 0)),
            pl.BlockSpec((1, LANES), lambda b: (0, 0)),
        ],
        out_specs=pl.BlockSpec((seq, RET_V), lambda b: (b, 0)),
        out_shape=jax.ShapeDtypeStruct((batch * seq, RET_V), BF16),
        scratch_shapes=[pltpu.VMEM((RET_HEADS, RET_DK, RET_DV), F32)]
        + [pltpu.VMEM((2, RET_HEADS, CHUNK, RET_DK), BF16)] * 4,
        compiler_params=pltpu.CompilerParams(
            dimension_semantics=("arbitrary",), vmem_limit_bytes=VMEM_LIMIT),
        name="retention",
    )(proj, proj, proj, sin, cos, inner, kdec, qdec, cdec)


def _retention_tables(seq):
    inv_freq = 1.0 / (ROPE_BASE ** jnp.linspace(0.0, 1.0, RET_DK // 2, dtype=F32))
    ang = jnp.arange(seq, dtype=F32)[:, None] * inv_freq[None, :]
    sin = jnp.repeat(jnp.sin(ang), 2, axis=-1)
    cos = jnp.repeat(jnp.cos(ang), 2, axis=-1)
    log_gamma = jnp.log(1.0 - 2.0 ** (-5.0 - jnp.arange(RET_HEADS, dtype=F32)))
    idx = jnp.arange(CHUNK, dtype=F32)
    causal = jnp.tril(jnp.ones((CHUNK, CHUNK), dtype=bool))
    rel = jnp.where(causal, idx[:, None] - idx[None, :], 0.0)
    inner = jnp.where(causal, jnp.exp(rel[None] * log_gamma[:, None, None]), 0.0)
    k_decay = jnp.exp(log_gamma[:, None] * (CHUNK - 1.0 - idx)[None, :])
    q_decay = jnp.exp(log_gamma[:, None] * (idx + 1.0)[None, :])
    chunk_decay = jnp.exp(log_gamma * CHUNK)
    pad = LANES - RET_HEADS
    kdec = jnp.pad(k_decay.T, ((0, 0), (0, pad)))
    qdec = jnp.pad(q_decay.T, ((0, 0), (0, pad)))
    cdec = jnp.pad(chunk_decay[None, :], ((0, 0), (0, pad)))
    return sin, cos, inner, kdec, qdec, cdec


def _merge_kernel(x_ref, ya_ref, yb_ref, ma_ref, mb_ref, wa_ref, wr_ref, wo_ref, nw_ref,
                  wrt_ref, br_ref, h_ref, xn_ref, ridx_ref, rw_ref, cnt_ref):
    tm = x_ref.shape[0]
    a = _dot(ya_ref[...], wa_ref[...])
    r = _dot(yb_ref[...], wr_ref[...])
    merged = _sigmoid(ma_ref[...].astype(F32)) * a + _sigmoid(mb_ref[...].astype(F32)) * r
    h = x_ref[...] + _dot(merged.astype(BF16), wo_ref[...])
    h_ref[...] = h
    xn = h * lax.rsqrt(jnp.mean(h * h, axis=-1, keepdims=True) + EPS) * nw_ref[...]
    xn_ref[...] = _pack_bf16_pairs(xn)
    xh = xn.astype(BF16)
    xl = (xn - xh.astype(F32)).astype(BF16)
    parts = _dot(jnp.concatenate([xh, xl], axis=0), wrt_ref[...])
    counts = _route(parts, tm, br_ref[...], ridx_ref, rw_ref)

    @pl.when(pl.program_id(0) == 0)
    def _():
        cnt_ref[...] = jnp.zeros_like(cnt_ref)

    cnt_ref[...] += jnp.broadcast_to(counts, cnt_ref.shape)


def _route(parts, tm, bias, ridx_ref, rw_ref):
    logits = (parts[:tm, :LANES] + (parts[tm:, :LANES] + parts[:tm, LANES:]
                                    + parts[tm:, LANES:])) + bias
    lane = lax.broadcasted_iota(jnp.int32, (tm, LANES), 1)
    neg = -jnp.inf
    gl = jnp.where((lane >= N_EXPERTS) & (lane < N_EXPERTS + N_GROUPS), logits, neg)
    gmax = jnp.max(gl, axis=-1, keepdims=True)
    gidx = jnp.min(jnp.where(gl == gmax, lane, LANES), axis=-1, keepdims=True) - N_EXPERTS
    g_w = 1.0 / jnp.sum(jnp.exp(gl - gmax), axis=-1, keepdims=True)
    el = jnp.where((lane // EXPERTS_PER_GROUP == gidx) & (lane < N_EXPERTS), logits, neg)
    m1 = jnp.max(el, axis=-1, keepdims=True)
    i1 = jnp.min(jnp.where(el == m1, lane, LANES), axis=-1, keepdims=True)
    el2 = jnp.where(lane == i1, neg, el)
    m2 = jnp.max(el2, axis=-1, keepdims=True)
    i2 = jnp.min(jnp.where(el2 == m2, lane, LANES), axis=-1, keepdims=True)
    e2 = jnp.exp(m2 - m1)
    p1 = g_w / (1.0 + e2)
    p2 = g_w * e2 / (1.0 + e2)
    ridx_ref[...] = jnp.where(lane == 0, i1, jnp.where(lane == 1, i2, 0))
    rw_ref[...] = jnp.where(lane == 0, p1, jnp.where(lane == 1, p2, 0.0))
    onehot = jnp.where((lane == i1) | (lane == i2), 1.0, 0.0)
    return jnp.sum(onehot, axis=0, keepdims=True)


def _merge_call(x2, ya, yb, proj, wa, wr, wo, nw, w_router, b_router, tm=1024):
    m = x2.shape[0]
    full = lambda shape: pl.BlockSpec(shape, lambda i: (0, 0))
    return pl.pallas_call(
        _merge_kernel,
        grid=(m // tm,),
        in_specs=[
            pl.BlockSpec((tm, D_MODEL), lambda i: (i, 0)),
            pl.BlockSpec((tm, GDN_V), lambda i: (i, 0)),
            pl.BlockSpec((tm, RET_V), lambda i: (i, 0)),
            pl.BlockSpec((tm, D_MODEL), lambda i: (i, 5)),
            pl.BlockSpec((tm, D_MODEL), lambda i: (i, 6)),
            full((GDN_V, D_MODEL)), full((RET_V, D_MODEL)), full((D_MODEL, D_MODEL)),
            full((1, D_MODEL)),
            full((D_MODEL, 2 * LANES)), full((1, LANES)),
        ],
        out_specs=[
            pl.BlockSpec((tm, D_MODEL), lambda i: (i, 0)),
            pl.BlockSpec((tm, HALF), lambda i: (i, 0)),
            pl.BlockSpec((tm, LANES), lambda i: (i, 0)),
            pl.BlockSpec((tm, LANES), lambda i: (i, 0)),
            pl.BlockSpec((8, LANES), lambda i: (0, 0)),
        ],
        out_shape=[
            jax.ShapeDtypeStruct((m, D_MODEL), F32),
            jax.ShapeDtypeStruct((m, HALF), I32),
            jax.ShapeDtypeStruct((m, LANES), I32),
            jax.ShapeDtypeStruct((m, LANES), F32),
            jax.ShapeDtypeStruct((8, LANES), F32),
        ],
        compiler_params=pltpu.CompilerParams(
            dimension_semantics=("arbitrary",), vmem_limit_bytes=VMEM_LIMIT),
        name="merge_router",
    )(x2, ya, yb, proj, proj, wa, wr, wo, nw, w_router, b_router)


def _slot_counts(m):
    n_slots = TOP_K * m + N_EXPERTS * SLOT_TILE
    return n_slots, n_slots // SLOT_TILE


def _lane_prefix_sum(x, lane):
    s = 1
    while s < LANES:
        x = x + jnp.where(lane >= s, pltpu.roll(x, s, axis=1), 0.0)
        s *= 2
    return x


def _plan_kernel(ridx_ref, cnt_ref, slots_ref, tile_ref, carry_s, off_s):
    i = pl.program_id(0)
    lane = lax.broadcasted_iota(I32, (PLAN_TILE, LANES), 1)
    row = lax.broadcasted_iota(I32, (PLAN_TILE, LANES), 0)
    lane1 = lane[0:1]

    @pl.when(i == 0)
    def _():
        cnt = cnt_ref[0:1, :]
        tile = float(SLOT_TILE)
        padded = jnp.floor((cnt + (tile - 1.0)) / tile) * tile
        incl = _lane_prefix_sum(padded, lane1)
        off = incl - padded
        off_s[...] = off
        carry_s[...] = jnp.zeros_like(carry_s)
        first = (row * SLOT_TILE).astype(F32)
        ended = jnp.where((lane < N_EXPERTS) & (incl <= first), 1.0, 0.0)
        tile_e = jnp.sum(ended, axis=-1, keepdims=True)
        last = jnp.sum(jnp.where(lane.astype(F32) == tile_e, off + cnt, 0.0), axis=-1, keepdims=True)
        used = jnp.clip(last - first[:, 0:1], 0.0, tile)
        tile_ref[...] = jnp.where(lane == 0, tile_e, jnp.where(lane == 1, used, 0.0)).astype(I32)

    strict = jnp.where(row[:, 0:1] > lax.broadcasted_iota(I32, (PLAN_TILE, PLAN_TILE), 1),
                       1.0, 0.0).astype(BF16)
    off = off_s[...]
    carry = carry_s[...]
    for sb in range(PLAN_STEP // PLAN_TILE):
        rows = pl.ds(sb * PLAN_TILE, PLAN_TILE)
        e1 = ridx_ref[rows, 0:1]
        e2 = ridx_ref[rows, 1:2]
        onehot = jnp.where((lane == e1) | (lane == e2), 1.0, 0.0)
        pos = _dot(strict, onehot.astype(BF16)) + (carry + off)
        s1 = jnp.sum(jnp.where(lane == e1, pos, 0.0), axis=-1, keepdims=True)
        s2 = jnp.sum(jnp.where(lane == e2, pos, 0.0), axis=-1, keepdims=True)
        both = jnp.where(lane == 0, s1, jnp.where(lane == 1, s2, 0.0))
        for q in range(PLAN_TILE // LANES):
            t = both[q * LANES:(q + 1) * LANES].T
            c0 = sb * PLAN_TILE + q * LANES
            slots_ref[:, c0:c0 + LANES] = t[0:8].astype(I32)
        carry = carry + jnp.sum(onehot, axis=0, keepdims=True)
    carry_s[...] = carry


def _plan_call(ridx, cnt):
    m = ridx.shape[0]
    _, n_tiles = _slot_counts(m)
    assert n_tiles <= PLAN_TILE
    return pl.pallas_call(
        _plan_kernel,
        grid=(m // PLAN_STEP,),
        in_specs=[pl.BlockSpec((PLAN_STEP, LANES), lambda i: (i, 0)),
                  pl.BlockSpec((8, LANES), lambda i: (0, 0))],
        out_specs=[
            pl.BlockSpec((8, PLAN_STEP), lambda i: (0, i)),
            pl.BlockSpec((PLAN_TILE, LANES), lambda i: (0, 0)),
        ],
        out_shape=[
            jax.ShapeDtypeStruct((8, m), I32),
            jax.ShapeDtypeStruct((PLAN_TILE, LANES), I32),
        ],
        scratch_shapes=[pltpu.VMEM((1, LANES), F32), pltpu.VMEM((1, LANES), F32)],
        compiler_params=pltpu.CompilerParams(dimension_semantics=("arbitrary",)),
        name="dispatch_plan",
    )(ridx, cnt)


def _sc_mesh():
    return plsc.VectorSubcoreMesh(core_axis_name="c", subcore_axis_name="s")


def _sc_worker():
    return lax.axis_index("s") * SC_CORES + lax.axis_index("c")


def _sc_dispatch(xn, slot1, slot2, n_rows):
    m = xn.shape[0]
    per = m // SC_WORKERS
    n_pairs = per // (2 * SC_ROWS)

    @functools.partial(
        pl.kernel, mesh=_sc_mesh(),
        out_type=jax.ShapeDtypeStruct((n_rows, HALF), I32),
        scratch_types=[pltpu.VMEM((per // SC_ROWS, SC_ROWS), I32), pltpu.VMEM((per // SC_ROWS, SC_ROWS), I32),
                       pltpu.VMEM((SC_ROWS, HALF), I32), pltpu.VMEM((SC_ROWS, HALF), I32),
                       pltpu.SemaphoreType.DMA, pltpu.SemaphoreType.DMA, pltpu.SemaphoreType.DMA],
        name="sc_dispatch")
    def k(x_hbm, s1_hbm, s2_hbm, o_hbm, i1_v, i2_v, rows0, rows1, sem_r0, sem_r1, sem_w):
        wid = _sc_worker()
        base = wid * per

        def read(chunk, rows_v, sem):
            return pltpu.make_async_copy(x_hbm.at[pl.ds(base + chunk * SC_ROWS, SC_ROWS)], rows_v, sem)

        def scatter(chunk, rows_v):
            c1 = pltpu.async_copy(rows_v, o_hbm.at[i1_v.at[chunk]], sem_w)
            c2 = pltpu.async_copy(rows_v, o_hbm.at[i2_v.at[chunk]], sem_w)
            c1.wait()
            c2.wait()

        read(0, rows0, sem_r0).start()
        pltpu.sync_copy(s1_hbm.at[pl.ds(wid * (per // SC_ROWS), per // SC_ROWS)], i1_v)
        pltpu.sync_copy(s2_hbm.at[pl.ds(wid * (per // SC_ROWS), per // SC_ROWS)], i2_v)

        @pl.loop(0, n_pairs)
        def _(i):
            read(2 * i, rows0, sem_r0).wait()
            read(2 * i + 1, rows1, sem_r1).start()
            scatter(2 * i, rows0)
            read(2 * i + 1, rows1, sem_r1).wait()

            @pl.when(i + 1 < n_pairs)
            def _():
                read(2 * i + 2, rows0, sem_r0).start()

            scatter(2 * i + 1, rows1)

    return k(xn, slot1, slot2)


def _sc_collect(ys, slot1, slot2):
    m = slot1.size
    per = m // SC_WORKERS
    row = jax.ShapeDtypeStruct((m, HALF), I32)

    @functools.partial(
        pl.kernel, mesh=_sc_mesh(), out_type=[row, row],
        scratch_types=[pltpu.VMEM((per // SC_ROWS, SC_ROWS), I32), pltpu.VMEM((per // SC_ROWS, SC_ROWS), I32),
                       pltpu.VMEM((SC_ROWS, HALF), I32), pltpu.VMEM((SC_ROWS, HALF), I32),
                       pltpu.SemaphoreType.DMA, pltpu.SemaphoreType.DMA],
        name="sc_collect")
    def k(y_hbm, s1_hbm, s2_hbm, g1_hbm, g2_hbm, i1_v, i2_v, rows1, rows2, sem_g, sem_w):
        wid = _sc_worker()
        base = wid * per
        pltpu.sync_copy(s1_hbm.at[pl.ds(wid * (per // SC_ROWS), per // SC_ROWS)], i1_v)
        pltpu.sync_copy(s2_hbm.at[pl.ds(wid * (per // SC_ROWS), per // SC_ROWS)], i2_v)

        @pl.loop(0, per // SC_ROWS)
        def _(ci):
            t0 = base + ci * SC_ROWS
            a1 = pltpu.async_copy(y_hbm.at[i1_v.at[ci]], rows1, sem_g)
            a2 = pltpu.async_copy(y_hbm.at[i2_v.at[ci]], rows2, sem_g)
            a1.wait()
            a2.wait()
            w1 = pltpu.async_copy(rows1, g1_hbm.at[pl.ds(t0, SC_ROWS)], sem_w)
            w2 = pltpu.async_copy(rows2, g2_hbm.at[pl.ds(t0, SC_ROWS)], sem_w)
            w1.wait()
            w2.wait()

    return k(ys, slot1, slot2)


def _expert_kernel(te_ref, used_ref, xs_ref, wg_hbm, wu_hbm, wd_hbm, ys_ref, wg_b, wu_b, wd_b,
                   wg_f, wu_f, wd_f, sem, slot_s):
    j = pl.program_id(0)
    n = pl.num_programs(0)
    e = te_ref[j]
    prev = te_ref[jnp.maximum(j - 1, 0)]
    valid = e < N_EXPERTS

    def weight_copies(expert, slot):
        return [pltpu.make_async_copy(hbm.at[expert], buf.at[slot], sem.at[slot, i])
                for i, (hbm, buf) in enumerate(((wg_hbm, wg_f), (wu_hbm, wu_f), (wd_hbm, wd_f)))]

    def run_tile():
        half = SLOT_TILE // 2
        rows = [pl.ds(i * half, half) for i in range(2)]
        row_id = lax.broadcasted_iota(I32, (half, HALF), 0)
        xs = [_unpack_bf16_pairs(jnp.where(row_id + i * half < used_ref[j], xs_ref[r, :], 0))
              .astype(BF16) for i, r in enumerate(rows)]
        gs = [_dot(x, wg_b[...]) for x in xs]
        us = [_dot(x, wu_b[...]) for x in xs]
        hids = [(_silu(g) * u).astype(BF16) for g, u in zip(gs, us)]
        ys = [_dot(hid, wd_b[...]) for hid in hids]
        for r, y in zip(rows, ys):
            ys_ref[r, :] = _pack_bf16_pairs(y)

    @pl.when((j == 0) & valid)
    def _():
        slot_s[0] = 0
        for c in weight_copies(e, 0):
            c.start()

    first_tile = ((j == 0) | (e != prev)) & valid

    @pl.when(first_tile)
    def _():
        slot = slot_s[0]
        for c in weight_copies(e, slot):
            c.wait()
        k = lax.while_loop(lambda k: (k < n) & (te_ref[jnp.minimum(k, n - 1)] == e),
                           lambda k: k + 1, j + 1)
        nxt = te_ref[jnp.minimum(k, n - 1)]

        @pl.when((k < n) & (nxt < N_EXPERTS))
        def _():
            for c in weight_copies(nxt, 1 - slot):
                c.start()

        wg_b[...] = wg_f[slot].astype(BF16)
        wu_b[...] = wu_f[slot].astype(BF16)
        wd_b[...] = wd_f[slot].astype(BF16)
        slot_s[0] = 1 - slot
        run_tile()

    @pl.when(valid & jnp.logical_not(first_tile))
    def _():
        run_tile()

    @pl.when(e >= N_EXPERTS)
    def _():
        ys_ref[...] = jnp.zeros_like(ys_ref)


def _expert_call(tile_expert, tile_used, xs, wg, wu, wd, n_tiles):
    hbm = pl.BlockSpec(memory_space=pl.ANY)
    return pl.pallas_call(
        _expert_kernel,
        grid_spec=pltpu.PrefetchScalarGridSpec(
            num_scalar_prefetch=2,
            grid=(n_tiles,),
            in_specs=[pl.BlockSpec((SLOT_TILE, HALF), lambda j, te, used: (j, 0)), hbm, hbm, hbm],
            out_specs=pl.BlockSpec((SLOT_TILE, HALF), lambda j, te, used: (j, 0)),
            scratch_shapes=[
                pltpu.VMEM((D_MODEL, D_EXPERT), BF16), pltpu.VMEM((D_MODEL, D_EXPERT), BF16),
                pltpu.VMEM((D_EXPERT, D_MODEL), BF16),
                pltpu.VMEM((2, D_MODEL, D_EXPERT), F32), pltpu.VMEM((2, D_MODEL, D_EXPERT), F32),
                pltpu.VMEM((2, D_EXPERT, D_MODEL), F32),
                pltpu.SemaphoreType.DMA((2, 3)),
                pltpu.SMEM((1,), I32),
            ],
        ),
        out_shape=jax.ShapeDtypeStruct((n_tiles * SLOT_TILE, HALF), I32),
        compiler_params=pltpu.CompilerParams(
            dimension_semantics=("arbitrary",), vmem_limit_bytes=VMEM_LIMIT),
        name="experts",
    )(tile_expert, tile_used, xs, wg, wu, wd)


def _final_kernel(h_ref, g1_ref, g2_ref, rw_ref, nw_ref, o_ref):
    rw = rw_ref[...]
    y = rw[:, 0:1] * _unpack_bf16_pairs(g1_ref[...]) + rw[:, 1:2] * _unpack_bf16_pairs(g2_ref[...])
    h = h_ref[...] + y
    o_ref[...] = h * lax.rsqrt(jnp.mean(h * h, axis=-1, keepdims=True) + EPS) * nw_ref[...]


def _final_call(h1, g1, g2, rw, nw, tm=1024):
    m = h1.shape[0]
    return pl.pallas_call(
        _final_kernel,
        grid=(m // tm,),
        in_specs=[
            pl.BlockSpec((tm, D_MODEL), lambda i: (i, 0)),
            pl.BlockSpec((tm, HALF), lambda i: (i, 0)),
            pl.BlockSpec((tm, HALF), lambda i: (i, 0)),
            pl.BlockSpec((tm, LANES), lambda i: (i, 0)),
            pl.BlockSpec((1, D_MODEL), lambda i: (0, 0)),
        ],
        out_specs=pl.BlockSpec((tm, D_MODEL), lambda i: (i, 0)),
        out_shape=jax.ShapeDtypeStruct((m, D_MODEL), F32),
        compiler_params=pltpu.CompilerParams(dimension_semantics=("arbitrary",)),
        name="combine_final",
    )(h1, g1, g2, rw, nw)


def _pad_lanes(a):
    return jnp.pad(a, ((0, 0), (0, LANES - a.shape[1])))


def kernel(x, norm_mix_w, w_in, conv_w, A_log, dt_bias, gdn_norm_w, w_up_gdn, w_up_ret, w_out,
           norm_ffn_w, w_group, b_group, w_expert, b_expert, w_gate, w_up, w_down, norm_final_w):
    batch, seq, d = x.shape
    m = batch * seq
    h = x.reshape(m, d)
    depth = w_in.shape[0]
    sin, cos, inner, kdec, qdec, cdec = _retention_tables(seq)
    for l in range(depth):
        w_main, w_ab = _repack_call(jnp.transpose(w_in[l]))
        proj, ab = _proj_call(h, norm_mix_w[l][None, :], w_main, w_ab)

        conv8 = jnp.pad(conv_w[l], ((0, 8 - GDN_CONV), (0, 0)))
        ya = _gdn_call(proj, ab, conv8, _pad_lanes(A_log[l][None, :]), _pad_lanes(dt_bias[l][None, :]),
                       gdn_norm_w[l][None, :], batch, seq)
        yb = _ret_call(proj, sin, cos, inner, kdec, qdec, cdec, batch, seq)

        w_router = _pad_lanes(jnp.concatenate([w_expert[l], w_group[l]], axis=1))
        wr_hi = w_router.astype(BF16)
        wr_lo = (w_router - wr_hi.astype(F32)).astype(BF16)
        b_router = _pad_lanes(jnp.concatenate([b_expert[l], b_group[l]])[None, :])
        h1, xn, ridx, rw, cnt = _merge_call(
            h, ya, yb, proj, w_up_gdn[l].astype(BF16), w_up_ret[l].astype(BF16), w_out[l].astype(BF16),
            norm_ffn_w[l][None, :], jnp.concatenate([wr_hi, wr_lo], axis=1), b_router)

        n_slots, n_tiles = _slot_counts(m)
        slots, tiles = _plan_call(ridx, cnt)
        slot1, slot2 = slots[0].reshape(-1, SC_ROWS), slots[1].reshape(-1, SC_ROWS)
        xs = _sc_dispatch(xn, slot1, slot2, n_slots)
        ys = _expert_call(tiles[:n_tiles, 0], tiles[:n_tiles, 1], xs,
                          w_gate[l], w_up[l], w_down[l], n_tiles)
        g1, g2 = _sc_collect(ys, slot1, slot2)

        assert depth == 1
        h = _final_call(h1, g1, g2, rw, norm_final_w[None, :])
    return h.reshape(batch, seq, d)
```

```python
import functools

import jax
import jax.numpy as jnp
from jax import lax
from jax.experimental import pallas as pl
from jax.experimental.pallas import tpu as pltpu
from jax.experimental.pallas import tpu_sc as plsc

F32 = jnp.float32
BF16 = jnp.bfloat16
I32 = jnp.int32
U32 = jnp.uint32

D_MODEL = 1024
EPS = 1e-6
GDN_HEADS = 4
GDN_DK = 128
GDN_DV = 128
GDN_CONV = 4
RET_HEADS = 4
RET_DK = 128
RET_DV = 256
ROPE_BASE = 10000.0
N_GROUPS = 4
EXPERTS_PER_GROUP = 8
N_EXPERTS = N_GROUPS * EXPERTS_PER_GROUP
D_EXPERT = 512

GDN_QK = GDN_HEADS * GDN_DK
GDN_V = GDN_HEADS * GDN_DV
RET_QK = RET_HEADS * RET_DK
RET_V = RET_HEADS * RET_DV

LANES = 128
CHUNK = 128
INV_BLOCK = 16
GDN_PREP_CHUNKS = 2
VMEM_LIMIT = 56 * 1024 * 1024

REPACK_COLS = 512
TOP_K = 2
SLOT_TILE = 512
PLAN_TILE = 256
PLAN_STEP = 1024
HALF = D_MODEL // 2
SC_CORES = 2
SC_WORKERS = SC_CORES * 16
SC_ROWS = 64

PROJ_COLS = 3 * GDN_QK + GDN_V + 2 * RET_QK + 2 * RET_V + 2 * D_MODEL


def _silu(x):
    return x / (1.0 + jnp.exp(-x))


def _sigmoid(x):
    return 1.0 / (1.0 + jnp.exp(-x))


def _dot(a, b):
    return jnp.dot(a, b, preferred_element_type=F32)


def _dot_nt(a, b):
    return lax.dot_general(a, b, (((1,), (1,)), ((), ())), preferred_element_type=F32)


def _pack_bf16_pairs(x):
    bits = lax.bitcast_convert_type(x.astype(BF16).astype(F32), U32)
    packed = (bits[:, :HALF] >> 16) | (bits[:, HALF:] & jnp.uint32(0xFFFF0000))
    return lax.bitcast_convert_type(packed, I32)


def _unpack_bf16_pairs(p):
    p = lax.bitcast_convert_type(p, U32)
    lo = lax.bitcast_convert_type(p << 16, F32)
    hi = lax.bitcast_convert_type(p & jnp.uint32(0xFFFF0000), F32)
    return jnp.concatenate([lo, hi], axis=1)


def _proj_kernel(x_ref, nw_ref, w_ref, wab_ref, proj_ref, ab_ref, u_ref):
    j = pl.program_id(1)

    def project():
        proj_ref[...] = _dot(u_ref[...], w_ref[...]).astype(BF16)

    @pl.when(j == 0)
    def _():
        x = x_ref[...]
        u = x * lax.rsqrt(jnp.mean(x * x, axis=-1, keepdims=True) + EPS) * nw_ref[...]
        ub = u.astype(BF16)
        u_ref[...] = ub
        ab_ref[...] = _dot(ub, wab_ref[...])
        project()

    @pl.when(j != 0)
    def _():
        project()


def _repack_kernel(wt_ref, abt_ref, main_ref, ab_ref):
    main_ref[...] = wt_ref[...].T.astype(BF16)

    @pl.when(pl.program_id(0) == 0)
    def _():
        ab = abt_ref[...].T
        ab_ref[...] = jnp.concatenate(
            [ab, jnp.zeros((ab.shape[0], LANES - ab.shape[1]), F32)], axis=1).astype(BF16)


def _repack_call(w_in_t):
    d_in, d = w_in_t.shape
    o_ab = 3 * GDN_QK
    n_ab = 2 * GDN_HEADS
    src = lambda r: pl.multiple_of(r * REPACK_COLS + jnp.where(r * REPACK_COLS >= o_ab, n_ab, 0), 8)
    return pl.pallas_call(
        _repack_kernel,
        grid=(PROJ_COLS // REPACK_COLS,),
        in_specs=[pl.BlockSpec((pl.Element(REPACK_COLS), pl.Element(d)), lambda r: (src(r), 0)),
                  pl.BlockSpec((pl.Element(n_ab), pl.Element(d)), lambda r: (o_ab, 0))],
        out_specs=[pl.BlockSpec((d, REPACK_COLS), lambda r: (0, r)),
                   pl.BlockSpec((d, LANES), lambda r: (0, 0))],
        out_shape=[jax.ShapeDtypeStruct((d, PROJ_COLS), BF16), jax.ShapeDtypeStruct((d, LANES), BF16)],
        compiler_params=pltpu.CompilerParams(dimension_semantics=("arbitrary",)),
        name="repack_w_in",
    )(w_in_t, w_in_t)


def _proj_call(x2, norm_w, w_main, w_ab, tm=1024, tn=3584):
    m = x2.shape[0]
    return pl.pallas_call(
        _proj_kernel,
        grid=(m // tm, PROJ_COLS // tn),
        in_specs=[
            pl.BlockSpec((tm, D_MODEL), lambda i, j: (i, 0)),
            pl.BlockSpec((1, D_MODEL), lambda i, j: (0, 0)),
            pl.BlockSpec((D_MODEL, tn), lambda i, j: (0, j)),
            pl.BlockSpec((D_MODEL, LANES), lambda i, j: (0, 0)),
        ],
        out_specs=[
            pl.BlockSpec((tm, tn), lambda i, j: (i, j)),
            pl.BlockSpec((tm, LANES), lambda i, j: (i, 0)),
        ],
        out_shape=[
            jax.ShapeDtypeStruct((m, PROJ_COLS), BF16),
            jax.ShapeDtypeStruct((m, LANES), F32),
        ],
        scratch_shapes=[pltpu.VMEM((tm, D_MODEL), BF16)],
        compiler_params=pltpu.CompilerParams(
            dimension_semantics=("arbitrary", "arbitrary"), vmem_limit_bytes=VMEM_LIMIT),
        name="proj",
    )(x2, norm_w, w_main, w_ab)


def _unit_lower_inverses(lows, ii, jj):
    eye = jnp.where(ii == jj, 1.0, 0.0).astype(F32)
    in_block = (ii // INV_BLOCK) == (jj // INV_BLOCK)
    ps = [jnp.where(in_block, -low, 0.0) for low in lows]
    ts = [eye + p for p in ps]
    span = 2
    while span < INV_BLOCK:
        ps = [_dot(p, p) for p in ps]
        ts = [t + _dot(t, p) for t, p in zip(ts, ps)]
        span *= 2
    s = INV_BLOCK
    while s < CHUNK:
        off_diag = ((ii // (2 * s)) == (jj // (2 * s))) & ((ii // s) != (jj // s))
        xs = [_dot(jnp.where(off_diag, low, 0.0), t) for low, t in zip(lows, ts)]
        ts = [t - _dot(t, x) for t, x in zip(ts, xs)]
        s *= 2
    return ts


def _gdn_kernel(qkv_ref, z_ref, ab_ref, convw_ref, alog_ref, dtb_ref, normw_ref, o_ref,
                b_s, o0_s, m_s, qp_s, gl_s, state_s, q_s, k_s, kb_s, rhs_s, dec_s, cv_s, qg_o, kdt_o,
                *, seq):
    nchunk = seq // CHUNK
    ii = lax.broadcasted_iota(jnp.int32, (CHUNK, CHUNK), 0)
    jj = lax.broadcasted_iota(jnp.int32, (CHUNK, CHUNK), 1)
    causal = ii >= jj
    strict = ii > jj
    tri = jnp.where(causal, 1.0, 0.0).astype(F32)
    neg_a = -jnp.exp(alog_ref[...])
    dtb = dtb_ref[...]

    def conv_cols(c, r0, lo, buf):
        x = qkv_ref[pl.ds(r0, CHUNK), lo:lo + LANES].astype(F32)
        prev0 = pl.multiple_of(jnp.maximum(r0 - 16, 0), 16)
        prev = qkv_ref[pl.ds(prev0, 16), lo:lo + LANES].astype(F32)
        buf[0:8, :] = prev[8:16] * jnp.where(c > 0, 1.0, 0.0)
        buf[8:8 + CHUNK, :] = x
        w = convw_ref[:, lo:lo + LANES]
        y = (w[3:4] * x + w[2:3] * buf[7:7 + CHUNK, :] + w[1:2] * buf[6:6 + CHUNK, :]
             + w[0:1] * buf[5:5 + CHUNK, :])
        return _silu(y)

    def l2n(x):
        return x * lax.rsqrt(jnp.sum(x * x, axis=-1, keepdims=True) + EPS)

    tri_b = tri.astype(BF16)

    def chunk_cumsum(g):
        g1 = g.astype(BF16)
        r1 = g - g1.astype(F32)
        g2 = r1.astype(BF16)
        g3 = (r1 - g2.astype(F32)).astype(BF16)
        return _dot(tri_b, g1) + (_dot(tri_b, g2) + _dot(tri_b, g3))

    def operands(cc, slot):
        for sub in range(GDN_PREP_CHUNKS):
            c = cc * GDN_PREP_CHUNKS + sub
            r0 = pl.multiple_of(c * CHUNK, CHUNK)
            ab = ab_ref[pl.ds(r0, CHUNK), :]
            xg = ab + dtb
            softplus = jnp.maximum(xg, 0.0) + jnp.log(1.0 + jnp.exp(-jnp.abs(xg)))
            g_all = neg_a * softplus
            beta_all = _sigmoid(ab)
            gc_all = chunk_cumsum(g_all)
            gc_t = gc_all.T
            gl_s[c] = jnp.exp(gc_all[CHUNK - 1:CHUNK, :])
            for h in range(GDN_HEADS):
                n = sub * GDN_HEADS + h
                bufs = [cv_s.at[slot, 3 * n + i] for i in range(3)]
                q = l2n(conv_cols(c, r0, h * GDN_DK, bufs[0])) * (GDN_DK ** -0.5)
                k = l2n(conv_cols(c, r0, GDN_QK + h * GDN_DK, bufs[1]))
                v = conv_cols(c, r0, 2 * GDN_QK + h * GDN_DV, bufs[2])
                gcol = gc_all[:, h:h + 1]
                grow = gc_t[h:h + 1, :]
                beta = beta_all[:, GDN_HEADS + h:GDN_HEADS + h + 1]
                dec_s[slot, n] = jnp.where(causal, jnp.exp(gcol - grow), 0.0)
                eg = jnp.exp(gcol)
                kb = k * beta
                q_s[slot, n] = q.astype(BF16)
                k_s[slot, n] = k.astype(BF16)
                kb_s[slot, n] = kb.astype(BF16)
                rhs_s[slot, n] = jnp.concatenate([v * beta, kb * eg], axis=1).astype(BF16)
                qg_o[slot, n] = q * eg
                kd = k * jnp.exp(gc_all[CHUNK - 1:CHUNK, h:h + 1] - gcol)
                kdt_o[slot, n] = kd.T.astype(BF16)

    def solve(cc, slot, between):
        items = [(cc * GDN_PREP_CHUNKS + sub, h, sub * GDN_HEADS + h)
                 for sub in range(GDN_PREP_CHUNKS) for h in range(GDN_HEADS)]
        kks = [_dot_nt(kb_s[slot, n], k_s[slot, n]) for _, _, n in items]
        qks = [_dot_nt(q_s[slot, n], k_s[slot, n]) for _, _, n in items]
        lows = [jnp.where(strict, kk * dec_s[slot, n], 0.0) for kk, (_, _, n) in zip(kks, items)]
        attns = [(qk * dec_s[slot, n]).astype(BF16) for qk, (_, _, n) in zip(qks, items)]
        for step in between[:len(between) // 2]:
            step()
        ts = _unit_lower_inverses(lows, ii, jj)
        for step in between[len(between) // 2:]:
            step()
        uws = [_dot(t.astype(BF16), rhs_s[slot, n]).astype(BF16) for t, (_, _, n) in zip(ts, items)]
        kds = [_dot(kdt_o[slot, n], uw) for uw, (_, _, n) in zip(uws, items)]
        ats = [_dot(attn, uw) for attn, uw in zip(attns, uws)]
        for kd_uw, at_uw, (c, h, n) in zip(kds, ats, items):
            b_s[c, h] = kd_uw[:, :GDN_DV]
            m_s[c, h] = (-kd_uw[:, GDN_DV:]).astype(BF16)
            o0_s[c, h] = at_uw[:, :GDN_DV]
            qp_s[c, h] = (qg_o[slot, n] - at_uw[:, GDN_DV:]).astype(BF16)

    ngroup = nchunk // GDN_PREP_CHUNKS
    per_trip = 2 * GDN_PREP_CHUNKS
    normw = normw_ref[...]
    state_s[...] = jnp.zeros_like(state_s)
    for c0 in range(per_trip):
        gl_s[c0] = jnp.zeros((1, LANES), F32)
        for h in range(GDN_HEADS):
            b_s[c0, h] = jnp.zeros((CHUNK, GDN_DV), F32)
            o0_s[c0, h] = jnp.zeros((CHUNK, GDN_DV), F32)
            m_s[c0, h] = jnp.zeros((CHUNK, GDN_DK), BF16)
            qp_s[c0, h] = jnp.zeros((CHUNK, GDN_DK), BF16)
    operands(0, 0)

    def prep(i, carry):
        first = jnp.maximum(per_trip * (i - 1), 0)
        steps = [functools.partial(scan, first + k, 0) for k in range(per_trip)]
        operands(2 * i + 1, 1)
        solve(2 * i, 0, steps[:per_trip // 2])
        operands(jnp.minimum(2 * i + 2, ngroup - 1), 0)
        solve(2 * i + 1, 1, steps[per_trip // 2:])
        return carry

    def scan(c, carry):
        r0 = pl.multiple_of(c * CHUNK, CHUNK)
        gl = gl_s[c]
        heads = range(GDN_HEADS)
        ss = [state_s[h] for h in heads]
        sbs = [s.astype(BF16) for s in ss]
        mss = [_dot(m_s[c, h], sbs[h]) for h in heads]
        qss = [_dot(qp_s[c, h], sbs[h]) for h in heads]
        for h in heads:
            state_s[h] = ss[h] * gl[:, h:h + 1] + (mss[h] + b_s[c, h])
            o = qss[h] + o0_s[c, h]
            z = z_ref[pl.ds(r0, CHUNK), h * GDN_DV:(h + 1) * GDN_DV].astype(F32)
            on = o * lax.rsqrt(jnp.mean(o * o, axis=-1, keepdims=True) + EPS) * normw
            o_ref[pl.ds(r0, CHUNK), h * GDN_DV:(h + 1) * GDN_DV] = (on * _silu(z)).astype(BF16)
        return carry

    ntrip = ngroup // 2
    lax.fori_loop(0, ntrip, prep, 0)
    lax.fori_loop(per_trip * (ntrip - 1), nchunk, scan, 0)


def _gdn_call(proj, ab, conv_w8, alog_p, dtb_p, normw, batch, seq):
    nchunk = seq // CHUNK
    hs = (nchunk, GDN_HEADS, CHUNK, CHUNK)
    ops = (2, GDN_PREP_CHUNKS * GDN_HEADS, CHUNK, CHUNK)
    return pl.pallas_call(
        functools.partial(_gdn_kernel, seq=seq),
        grid=(batch,),
        in_specs=[
            pl.BlockSpec((seq, 3 * GDN_QK), lambda b: (b, 0)),
            pl.BlockSpec((seq, GDN_V), lambda b: (b, 3)),
            pl.BlockSpec((seq, LANES), lambda b: (b, 0)),
            pl.BlockSpec((8, 3 * GDN_QK), lambda b: (0, 0)),
            pl.BlockSpec((1, LANES), lambda b: (0, 0)),
            pl.BlockSpec((1, LANES), lambda b: (0, 0)),
            pl.BlockSpec((1, GDN_DV), lambda b: (0, 0)),
        ],
        out_specs=pl.BlockSpec((seq, GDN_V), lambda b: (b, 0)),
        out_shape=jax.ShapeDtypeStruct((batch * seq, GDN_V), BF16),
        scratch_shapes=[
            pltpu.VMEM(hs, F32), pltpu.VMEM(hs, F32), pltpu.VMEM(hs, BF16), pltpu.VMEM(hs, BF16),
            pltpu.VMEM((nchunk, 1, LANES), F32),
            pltpu.VMEM((GDN_HEADS, GDN_DK, GDN_DV), F32),
            pltpu.VMEM(ops, BF16), pltpu.VMEM(ops, BF16), pltpu.VMEM(ops, BF16),
            pltpu.VMEM(ops[:3] + (2 * CHUNK,), BF16),
            pltpu.VMEM(ops, F32),
            pltpu.VMEM((2, 3 * ops[1], 8 + CHUNK, LANES), F32),
            pltpu.VMEM(ops, F32), pltpu.VMEM(ops, BF16),
        ],
        compiler_params=pltpu.CompilerParams(
            dimension_semantics=("arbitrary",), vmem_limit_bytes=VMEM_LIMIT),
        name="gdn",
    )(proj, proj, ab, conv_w8, alog_p, dtb_p, normw)


def _ret_kernel(qk_ref, v_ref, g_ref, sin_ref, cos_ref, inner_ref, kdec_ref, qdec_ref, cdec_ref,
                o_ref, state_s, q_s, qd_s, k_s, kt_s, *, seq):
    nchunk = seq // CHUNK
    lane = lax.broadcasted_iota(jnp.int32, (CHUNK, RET_DK), 1)
    even = (lane % 2) == 0

    def rotate(x, sin, cos):
        nxt = pltpu.roll(x, RET_DK - 1, axis=1)
        prv = pltpu.roll(x, 1, axis=1)
        return x * cos + jnp.where(even, -nxt, prv) * sin

    state_s[...] = jnp.zeros_like(state_s)
    kdec = kdec_ref[...]
    qdec = qdec_ref[...]
    cdec = cdec_ref[...]

    heads = range(RET_HEADS)

    def operands(c, slot):
        r0 = pl.multiple_of(c * CHUNK, CHUNK)
        sin = sin_ref[pl.ds(r0, CHUNK), :]
        cos = cos_ref[pl.ds(r0, CHUNK), :]
        for h in heads:
            q = rotate(qk_ref[pl.ds(r0, CHUNK), h * RET_DK:(h + 1) * RET_DK].astype(F32), sin, cos)
            k = rotate(qk_ref[pl.ds(r0, CHUNK), RET_QK + h * RET_DK:RET_QK + (h + 1) * RET_DK]
                       .astype(F32), sin, cos) * (RET_DK ** -0.5)
            q_s[slot, h] = q.astype(BF16)
            qd_s[slot, h] = (q * qdec[:, h:h + 1]).astype(BF16)
            k_s[slot, h] = k.astype(BF16)
            kt_s[slot, h] = (k * kdec[:, h:h + 1]).T.astype(BF16)

    def outputs(c, slot):
        r0 = pl.multiple_of(c * CHUNK, CHUNK)
        vs = [v_ref[pl.ds(r0, CHUNK), h * RET_DV:(h + 1) * RET_DV] for h in heads]
        ss = [state_s[h] for h in heads]
        qks = [_dot_nt(q_s[slot, h], k_s[slot, h]) for h in heads]
        inters = [_dot(qd_s[slot, h], ss[h].astype(BF16)) for h in heads]
        kvs = [_dot(kt_s[slot, h], vs[h]) for h in heads]
        intras = [_dot((qks[h] * inner_ref[h]).astype(BF16), vs[h]) for h in heads]
        for h in heads:
            state_s[h] = ss[h] * cdec[:, h:h + 1] + kvs[h]
            o = intras[h] + inters[h]
            gate = g_ref[pl.ds(r0, CHUNK), h * RET_DV:(h + 1) * RET_DV].astype(F32)
            on = o * lax.rsqrt(jnp.mean(o * o, axis=-1, keepdims=True) + EPS)
            o_ref[pl.ds(r0, CHUNK), h * RET_DV:(h + 1) * RET_DV] = (on * _silu(gate)).astype(BF16)

    operands(0, 0)

    def body(i, carry):
        operands(2 * i + 1, 1)
        outputs(2 * i, 0)
        operands(jnp.minimum(2 * i + 2, nchunk - 1), 0)
        outputs(2 * i + 1, 1)
        return carry

    lax.fori_loop(0, nchunk // 2, body, 0)


def _ret_call(proj, sin, cos, inner, kdec, qdec, cdec, batch, seq):
    return pl.pallas_call(
        functools.partial(_ret_kernel, seq=seq),
        grid=(batch,),
        in_specs=[
            pl.BlockSpec((seq, 2 * RET_QK), lambda b: (b, 2)),
            pl.BlockSpec((seq, RET_V), lambda b: (b, 3)),
            pl.BlockSpec((seq, RET_V), lambda b: (b, 4)),
            pl.BlockSpec((seq, RET_DK), lambda b: (0, 0)),
            pl.BlockSpec((seq, RET_DK), lambda b: (0, 0)),
            pl.BlockSpec((RET_HEADS, CHUNK, CHUNK), lambda b: (0, 0, 0)),
            pl.BlockSpec((CHUNK, LANES), lambda b: (0, 0)),
            pl.BlockSpec((CHUNK, LANES), lambda b: (0, 0)),
            pl.BlockSpec((1, LANES), lambda b: (0, 0)),
        ],
        out_specs=pl.BlockSpec((seq, RET_V), lambda b: (b, 0)),
        out_shape=jax.ShapeDtypeStruct((batch * seq, RET_V), BF16),
        scratch_shapes=[pltpu.VMEM((RET_HEADS, RET_DK, RET_DV), F32)]
        + [pltpu.VMEM((2, RET_HEADS, CHUNK, RET_DK), BF16)] * 4,
        compiler_params=pltpu.CompilerParams(
            dimension_semantics=("arbitrary",), vmem_limit_bytes=VMEM_LIMIT),
        name="retention",
    )(proj, proj, proj, sin, cos, inner, kdec, qdec, cdec)


def _retention_tables(seq):
    inv_freq = 1.0 / (ROPE_BASE ** jnp.linspace(0.0, 1.0, RET_DK // 2, dtype=F32))
    ang = jnp.arange(seq, dtype=F32)[:, None] * inv_freq[None, :]
    sin = jnp.repeat(jnp.sin(ang), 2, axis=-1)
    cos = jnp.repeat(jnp.cos(ang), 2, axis=-1)
    log_gamma = jnp.log(1.0 - 2.0 ** (-5.0 - jnp.arange(RET_HEADS, dtype=F32)))
    idx = jnp.arange(CHUNK, dtype=F32)
    causal = jnp.tril(jnp.ones((CHUNK, CHUNK), dtype=bool))
    rel = jnp.where(causal, idx[:, None] - idx[None, :], 0.0)
    inner = jnp.where(causal, jnp.exp(rel[None] * log_gamma[:, None, None]), 0.0)
    k_decay = jnp.exp(log_gamma[:, None] * (CHUNK - 1.0 - idx)[None, :])
    q_decay = jnp.exp(log_gamma[:, None] * (idx + 1.0)[None, :])
    chunk_decay = jnp.exp(log_gamma * CHUNK)
    pad = LANES - RET_HEADS
    kdec = jnp.pad(k_decay.T, ((0, 0), (0, pad)))
    qdec = jnp.pad(q_decay.T, ((0, 0), (0, pad)))
    cdec = jnp.pad(chunk_decay[None, :], ((0, 0), (0, pad)))
    return sin, cos, inner, kdec, qdec, cdec


def _merge_kernel(x_ref, ya_ref, yb_ref, ma_ref, mb_ref, wa_ref, wr_ref, wo_ref, nw_ref,
                  wrt_ref, br_ref, h_ref, xn_ref, ridx_ref, rw_ref, cnt_ref):
    tm = x_ref.shape[0]
    a = _dot(ya_ref[...], wa_ref[...])
    r = _dot(yb_ref[...], wr_ref[...])
    merged = _sigmoid(ma_ref[...].astype(F32)) * a + _sigmoid(mb_ref[...].astype(F32)) * r
    h = x_ref[...] + _dot(merged.astype(BF16), wo_ref[...])
    h_ref[...] = h
    xn = h * lax.rsqrt(jnp.mean(h * h, axis=-1, keepdims=True) + EPS) * nw_ref[...]
    xn_ref[...] = _pack_bf16_pairs(xn)
    xh = xn.astype(BF16)
    xl = (xn - xh.astype(F32)).astype(BF16)
    parts = _dot(jnp.concatenate([xh, xl], axis=0), wrt_ref[...])
    counts = _route(parts, tm, br_ref[...], ridx_ref, rw_ref)

    @pl.when(pl.program_id(0) == 0)
    def _():
        cnt_ref[...] = jnp.zeros_like(cnt_ref)

    cnt_ref[...] += jnp.broadcast_to(counts, cnt_ref.shape)


def _route(parts, tm, bias, ridx_ref, rw_ref):
    logits = (parts[:tm, :LANES] + (parts[tm:, :LANES] + parts[:tm, LANES:]
                                    + parts[tm:, LANES:])) + bias
    lane = lax.broadcasted_iota(jnp.int32, (tm, LANES), 1)
    neg = -jnp.inf
    gl = jnp.where((lane >= N_EXPERTS) & (lane < N_EXPERTS + N_GROUPS), logits, neg)
    gmax = jnp.max(gl, axis=-1, keepdims=True)
    gidx = jnp.min(jnp.where(gl == gmax, lane, LANES), axis=-1, keepdims=True) - N_EXPERTS
    g_w = 1.0 / jnp.sum(jnp.exp(gl - gmax), axis=-1, keepdims=True)
    el = jnp.where((lane // EXPERTS_PER_GROUP == gidx) & (lane < N_EXPERTS), logits, neg)
    m1 = jnp.max(el, axis=-1, keepdims=True)
    i1 = jnp.min(jnp.where(el == m1, lane, LANES), axis=-1, keepdims=True)
    el2 = jnp.where(lane == i1, neg, el)
    m2 = jnp.max(el2, axis=-1, keepdims=True)
    i2 = jnp.min(jnp.where(el2 == m2, lane, LANES), axis=-1, keepdims=True)
    e2 = jnp.exp(m2 - m1)
    p1 = g_w / (1.0 + e2)
    p2 = g_w * e2 / (1.0 + e2)
    ridx_ref[...] = jnp.where(lane == 0, i1, jnp.where(lane == 1, i2, 0))
    rw_ref[...] = jnp.where(lane == 0, p1, jnp.where(lane == 1, p2, 0.0))
    onehot = jnp.where((lane == i1) | (lane == i2), 1.0, 0.0)
    return jnp.sum(onehot, axis=0, keepdims=True)


def _merge_call(x2, ya, yb, proj, wa, wr, wo, nw, w_router, b_router, tm=1024):
    m = x2.shape[0]
    full = lambda shape: pl.BlockSpec(shape, lambda i: (0, 0))
    return pl.pallas_call(
        _merge_kernel,
        grid=(m // tm,),
        in_specs=[
            pl.BlockSpec((tm, D_MODEL), lambda i: (i, 0)),
            pl.BlockSpec((tm, GDN_V), lambda i: (i, 0)),
            pl.BlockSpec((tm, RET_V), lambda i: (i, 0)),
            pl.BlockSpec((tm, D_MODEL), lambda i: (i, 5)),
            pl.BlockSpec((tm, D_MODEL), lambda i: (i, 6)),
            full((GDN_V, D_MODEL)), full((RET_V, D_MODEL)), full((D_MODEL, D_MODEL)),
            full((1, D_MODEL)),
            full((D_MODEL, 2 * LANES)), full((1, LANES)),
        ],
        out_specs=[
            pl.BlockSpec((tm, D_MODEL), lambda i: (i, 0)),
            pl.BlockSpec((tm, HALF), lambda i: (i, 0)),
            pl.BlockSpec((tm, LANES), lambda i: (i, 0)),
            pl.BlockSpec((tm, LANES), lambda i: (i, 0)),
            pl.BlockSpec((8, LANES), lambda i: (0, 0)),
        ],
        out_shape=[
            jax.ShapeDtypeStruct((m, D_MODEL), F32),
            jax.ShapeDtypeStruct((m, HALF), I32),
            jax.ShapeDtypeStruct((m, LANES), I32),
            jax.ShapeDtypeStruct((m, LANES), F32),
            jax.ShapeDtypeStruct((8, LANES), F32),
        ],
        compiler_params=pltpu.CompilerParams(
            dimension_semantics=("arbitrary",), vmem_limit_bytes=VMEM_LIMIT),
        name="merge_router",
    )(x2, ya, yb, proj, proj, wa, wr, wo, nw, w_router, b_router)


def _slot_counts(m):
    n_slots = TOP_K * m + N_EXPERTS * SLOT_TILE
    return n_slots, n_slots // SLOT_TILE


def _lane_prefix_sum(x, lane):
    s = 1
    while s < LANES:
        x = x + jnp.where(lane >= s, pltpu.roll(x, s, axis=1), 0.0)
        s *= 2
    return x


def _plan_kernel(ridx_ref, cnt_ref, slots_ref, tile_ref, carry_s, off_s):
    i = pl.program_id(0)
    lane = lax.broadcasted_iota(I32, (PLAN_TILE, LANES), 1)
    row = lax.broadcasted_iota(I32, (PLAN_TILE, LANES), 0)
    lane1 = lane[0:1]

    @pl.when(i == 0)
    def _():
        cnt = cnt_ref[0:1, :]
        tile = float(SLOT_TILE)
        padded = jnp.floor((cnt + (tile - 1.0)) / tile) * tile
        incl = _lane_prefix_sum(padded, lane1)
        off = incl - padded
        off_s[...] = off
        carry_s[...] = jnp.zeros_like(carry_s)
        first = (row * SLOT_TILE).astype(F32)
        ended = jnp.where((lane < N_EXPERTS) & (incl <= first), 1.0, 0.0)
        tile_e = jnp.sum(ended, axis=-1, keepdims=True)
        last = jnp.sum(jnp.where(lane.astype(F32) == tile_e, off + cnt, 0.0), axis=-1, keepdims=True)
        used = jnp.clip(last - first[:, 0:1], 0.0, tile)
        tile_ref[...] = jnp.where(lane == 0, tile_e, jnp.where(lane == 1, used, 0.0)).astype(I32)

    strict = jnp.where(row[:, 0:1] > lax.broadcasted_iota(I32, (PLAN_TILE, PLAN_TILE), 1),
                       1.0, 0.0).astype(BF16)
    off = off_s[...]
    carry = carry_s[...]
    for sb in range(PLAN_STEP // PLAN_TILE):
        rows = pl.ds(sb * PLAN_TILE, PLAN_TILE)
        e1 = ridx_ref[rows, 0:1]
        e2 = ridx_ref[rows, 1:2]
        onehot = jnp.where((lane == e1) | (lane == e2), 1.0, 0.0)
        pos = _dot(strict, onehot.astype(BF16)) + (carry + off)
        s1 = jnp.sum(jnp.where(lane == e1, pos, 0.0), axis=-1, keepdims=True)
        s2 = jnp.sum(jnp.where(lane == e2, pos, 0.0), axis=-1, keepdims=True)
        both = jnp.where(lane == 0, s1, jnp.where(lane == 1, s2, 0.0))
        for q in range(PLAN_TILE // LANES):
            t = both[q * LANES:(q + 1) * LANES].T
            c0 = sb * PLAN_TILE + q * LANES
            slots_ref[:, c0:c0 + LANES] = t[0:8].astype(I32)
        carry = carry + jnp.sum(onehot, axis=0, keepdims=True)
    carry_s[...] = carry


def _plan_call(ridx, cnt):
    m = ridx.shape[0]
    _, n_tiles = _slot_counts(m)
    assert n_tiles <= PLAN_TILE
    return pl.pallas_call(
        _plan_kernel,
        grid=(m // PLAN_STEP,),
        in_specs=[pl.BlockSpec((PLAN_STEP, LANES), lambda i: (i, 0)),
                  pl.BlockSpec((8, LANES), lambda i: (0, 0))],
        out_specs=[
            pl.BlockSpec((8, PLAN_STEP), lambda i: (0, i)),
            pl.BlockSpec((PLAN_TILE, LANES), lambda i: (0, 0)),
        ],
        out_shape=[
            jax.ShapeDtypeStruct((8, m), I32),
            jax.ShapeDtypeStruct((PLAN_TILE, LANES), I32),
        ],
        scratch_shapes=[pltpu.VMEM((1, LANES), F32), pltpu.VMEM((1, LANES), F32)],
        compiler_params=pltpu.CompilerParams(dimension_semantics=("arbitrary",)),
        name="dispatch_plan",
    )(ridx, cnt)


def _sc_mesh():
    return plsc.VectorSubcoreMesh(core_axis_name="c", subcore_axis_name="s")


def _sc_worker():
    return lax.axis_index("s") * SC_CORES + lax.axis_index("c")


def _sc_dispatch(xn, slot1, slot2, n_rows):
    m = xn.shape[0]
    per = m // SC_WORKERS
    n_pairs = per // (2 * SC_ROWS)

    @functools.partial(
        pl.kernel, mesh=_sc_mesh(),
        out_type=jax.ShapeDtypeStruct((n_rows, HALF), I32),
        scratch_types=[pltpu.VMEM((per // SC_ROWS, SC_ROWS), I32), pltpu.VMEM((per // SC_ROWS, SC_ROWS), I32),
                       pltpu.VMEM((SC_ROWS, HALF), I32), pltpu.VMEM((SC_ROWS, HALF), I32),
                       pltpu.SemaphoreType.DMA, pltpu.SemaphoreType.DMA, pltpu.SemaphoreType.DMA],
        name="sc_dispatch")
    def k(x_hbm, s1_hbm, s2_hbm, o_hbm, i1_v, i2_v, rows0, rows1, sem_r0, sem_r1, sem_w):
        wid = _sc_worker()
        base = wid * per

        def read(chunk, rows_v, sem):
            return pltpu.make_async_copy(x_hbm.at[pl.ds(base + chunk * SC_ROWS, SC_ROWS)], rows_v, sem)

        def scatter(chunk, rows_v):
            c1 = pltpu.async_copy(rows_v, o_hbm.at[i1_v.at[chunk]], sem_w)
            c2 = pltpu.async_copy(rows_v, o_hbm.at[i2_v.at[chunk]], sem_w)
            c1.wait()
            c2.wait()

        read(0, rows0, sem_r0).start()
        pltpu.sync_copy(s1_hbm.at[pl.ds(wid * (per // SC_ROWS), per // SC_ROWS)], i1_v)
        pltpu.sync_copy(s2_hbm.at[pl.ds(wid * (per // SC_ROWS), per // SC_ROWS)], i2_v)

        @pl.loop(0, n_pairs)
        def _(i):
            read(2 * i, rows0, sem_r0).wait()
            read(2 * i + 1, rows1, sem_r1).start()
            scatter(2 * i, rows0)
            read(2 * i + 1, rows1, sem_r1).wait()

            @pl.when(i + 1 < n_pairs)
            def _():
                read(2 * i + 2, rows0, sem_r0).start()

            scatter(2 * i + 1, rows1)

    return k(xn, slot1, slot2)


def _sc_collect(ys, slot1, slot2):
    m = slot1.size
    per = m // SC_WORKERS
    row = jax.ShapeDtypeStruct((m, HALF), I32)

    @functools.partial(
        pl.kernel, mesh=_sc_mesh(), out_type=[row, row],
        scratch_types=[pltpu.VMEM((per // SC_ROWS, SC_ROWS), I32), pltpu.VMEM((per // SC_ROWS, SC_ROWS), I32),
                       pltpu.VMEM((SC_ROWS, HALF), I32), pltpu.VMEM((SC_ROWS, HALF), I32),
                       pltpu.SemaphoreType.DMA, pltpu.SemaphoreType.DMA],
        name="sc_collect")
    def k(y_hbm, s1_hbm, s2_hbm, g1_hbm, g2_hbm, i1_v, i2_v, rows1, rows2, sem_g, sem_w):
        wid = _sc_worker()
        base = wid * per
        pltpu.sync_copy(s1_hbm.at[pl.ds(wid * (per // SC_ROWS), per // SC_ROWS)], i1_v)
        pltpu.sync_copy(s2_hbm.at[pl.ds(wid * (per // SC_ROWS), per // SC_ROWS)], i2_v)

        @pl.loop(0, per // SC_ROWS)
        def _(ci):
            t0 = base + ci * SC_ROWS
            a1 = pltpu.async_copy(y_hbm.at[i1_v.at[ci]], rows1, sem_g)
            a2 = pltpu.async_copy(y_hbm.at[i2_v.at[ci]], rows2, sem_g)
            a1.wait()
            a2.wait()
            w1 = pltpu.async_copy(rows1, g1_hbm.at[pl.ds(t0, SC_ROWS)], sem_w)
            w2 = pltpu.async_copy(rows2, g2_hbm.at[pl.ds(t0, SC_ROWS)], sem_w)
            w1.wait()
            w2.wait()

    return k(ys, slot1, slot2)


def _expert_kernel(te_ref, used_ref, xs_ref, wg_hbm, wu_hbm, wd_hbm, ys_ref, wg_b, wu_b, wd_b,
                   wg_f, wu_f, wd_f, sem, slot_s):
    j = pl.program_id(0)
    n = pl.num_programs(0)
    e = te_ref[j]
    prev = te_ref[jnp.maximum(j - 1, 0)]
    valid = e < N_EXPERTS

    def weight_copies(expert, slot):
        return [pltpu.make_async_copy(hbm.at[expert], buf.at[slot], sem.at[slot, i])
                for i, (hbm, buf) in enumerate(((wg_hbm, wg_f), (wu_hbm, wu_f), (wd_hbm, wd_f)))]

    def run_tile():
        half = SLOT_TILE // 2
        rows = [pl.ds(i * half, half) for i in range(2)]
        row_id = lax.broadcasted_iota(I32, (half, HALF), 0)
        xs = [_unpack_bf16_pairs(jnp.where(row_id + i * half < used_ref[j], xs_ref[r, :], 0))
              .astype(BF16) for i, r in enumerate(rows)]
        gs = [_dot(x, wg_b[...]) for x in xs]
        us = [_dot(x, wu_b[...]) for x in xs]
        hids = [(_silu(g) * u).astype(BF16) for g, u in zip(gs, us)]
        ys = [_dot(hid, wd_b[...]) for hid in hids]
        for r, y in zip(rows, ys):
            ys_ref[r, :] = _pack_bf16_pairs(y)

    @pl.when((j == 0) & valid)
    def _():
        slot_s[0] = 0
        for c in weight_copies(e, 0):
            c.start()

    first_tile = ((j == 0) | (e != prev)) & valid

    @pl.when(first_tile)
    def _():
        slot = slot_s[0]
        for c in weight_copies(e, slot):
            c.wait()
        k = lax.while_loop(lambda k: (k < n) & (te_ref[jnp.minimum(k, n - 1)] == e),
                           lambda k: k + 1, j + 1)
        nxt = te_ref[jnp.minimum(k, n - 1)]

        @pl.when((k < n) & (nxt < N_EXPERTS))
        def _():
            for c in weight_copies(nxt, 1 - slot):
                c.start()

        wg_b[...] = wg_f[slot].astype(BF16)
        wu_b[...] = wu_f[slot].astype(BF16)
        wd_b[...] = wd_f[slot].astype(BF16)
        slot_s[0] = 1 - slot
        run_tile()

    @pl.when(valid & jnp.logical_not(first_tile))
    def _():
        run_tile()

    @pl.when(e >= N_EXPERTS)
    def _():
        ys_ref[...] = jnp.zeros_like(ys_ref)


def _expert_call(tile_expert, tile_used, xs, wg, wu, wd, n_tiles):
    hbm = pl.BlockSpec(memory_space=pl.ANY)
    return pl.pallas_call(
        _expert_kernel,
        grid_spec=pltpu.PrefetchScalarGridSpec(
            num_scalar_prefetch=2,
            grid=(n_tiles,),
            in_specs=[pl.BlockSpec((SLOT_TILE, HALF), lambda j, te, used: (j, 0)), hbm, hbm, hbm],
            out_specs=pl.BlockSpec((SLOT_TILE, HALF), lambda j, te, used: (j, 0)),
            scratch_shapes=[
                pltpu.VMEM((D_MODEL, D_EXPERT), BF16), pltpu.VMEM((D_MODEL, D_EXPERT), BF16),
                pltpu.VMEM((D_EXPERT, D_MODEL), BF16),
                pltpu.VMEM((2, D_MODEL, D_EXPERT), F32), pltpu.VMEM((2, D_MODEL, D_EXPERT), F32),
                pltpu.VMEM((2, D_EXPERT, D_MODEL), F32),
                pltpu.SemaphoreType.DMA((2, 3)),
                pltpu.SMEM((1,), I32),
            ],
        ),
        out_shape=jax.ShapeDtypeStruct((n_tiles * SLOT_TILE, HALF), I32),
        compiler_params=pltpu.CompilerParams(
            dimension_semantics=("arbitrary",), vmem_limit_bytes=VMEM_LIMIT),
        name="experts",
    )(tile_expert, tile_used, xs, wg, wu, wd)


def _final_kernel(h_ref, g1_ref, g2_ref, rw_ref, nw_ref, o_ref):
    rw = rw_ref[...]
    y = rw[:, 0:1] * _unpack_bf16_pairs(g1_ref[...]) + rw[:, 1:2] * _unpack_bf16_pairs(g2_ref[...])
    h = h_ref[...] + y
    o_ref[...] = h * lax.rsqrt(jnp.mean(h * h, axis=-1, keepdims=True) + EPS) * nw_ref[...]


def _final_call(h1, g1, g2, rw, nw, tm=1024):
    m = h1.shape[0]
    return pl.pallas_call(
        _final_kernel,
        grid=(m // tm,),
        in_specs=[
            pl.BlockSpec((tm, D_MODEL), lambda i: (i, 0)),
            pl.BlockSpec((tm, HALF), lambda i: (i, 0)),
            pl.BlockSpec((tm, HALF), lambda i: (i, 0)),
            pl.BlockSpec((tm, LANES), lambda i: (i, 0)),
            pl.BlockSpec((1, D_MODEL), lambda i: (0, 0)),
        ],
        out_specs=pl.BlockSpec((tm, D_MODEL), lambda i: (i, 0)),
        out_shape=jax.ShapeDtypeStruct((m, D_MODEL), F32),
        compiler_params=pltpu.CompilerParams(dimension_semantics=("arbitrary",)),
        name="combine_final",
    )(h1, g1, g2, rw, nw)


def _pad_lanes(a):
    return jnp.pad(a, ((0, 0), (0, LANES - a.shape[1])))


def kernel(x, norm_mix_w, w_in, conv_w, A_log, dt_bias, gdn_norm_w, w_up_gdn, w_up_ret, w_out,
           norm_ffn_w, w_group, b_group, w_expert, b_expert, w_gate, w_up, w_down, norm_final_w):
    batch, seq, d = x.shape
    m = batch * seq
    h = x.reshape(m, d)
    depth = w_in.shape[0]
    sin, cos, inner, kdec, qdec, cdec = _retention_tables(seq)
    for l in range(depth):
        w_main, w_ab = _repack_call(jnp.transpose(w_in[l]))
        proj, ab = _proj_call(h, norm_mix_w[l][None, :], w_main, w_ab)

        conv8 = jnp.pad(conv_w[l], ((0, 8 - GDN_CONV), (0, 0)))
        ya = _gdn_call(proj, ab, conv8, _pad_lanes(A_log[l][None, :]), _pad_lanes(dt_bias[l][None, :]),
                       gdn_norm_w[l][None, :], batch, seq)
        yb = _ret_call(proj, sin, cos, inner, kdec, qdec, cdec, batch, seq)

        w_router = _pad_lanes(jnp.concatenate([w_expert[l], w_group[l]], axis=1))
        wr_hi = w_router.astype(BF16)
        wr_lo = (w_router - wr_hi.astype(F32)).astype(BF16)
        b_router = _pad_lanes(jnp.concatenate([b_expert[l], b_group[l]])[None, :])
        h1, xn, ridx, rw, cnt = _merge_call(
            h, ya, yb, proj, w_up_gdn[l].astype(BF16), w_up_ret[l].astype(BF16), w_out[l].astype(BF16),
            norm_ffn_w[l][None, :], jnp.concatenate([wr_hi, wr_lo], axis=1), b_router)

        n_slots, n_tiles = _slot_counts(m)
        slots, tiles = _plan_call(ridx, cnt)
        slot1, slot2 = slots[0].reshape(-1, SC_ROWS), slots[1].reshape(-1, SC_ROWS)
        xs = _sc_dispatch(xn, slot1, slot2, n_slots)
        ys = _expert_call(tiles[:n_tiles, 0], tiles[:n_tiles, 1], xs,
                          w_gate[l], w_up[l], w_down[l], n_tiles)
        g1, g2 = _sc_collect(ys, slot1, slot2)

        assert depth == 1
        h = _final_call(h1, g1, g2, rw, norm_final_w[None, :])
    return h.reshape(batch, seq, d)
```

```python
import functools

import jax
import jax.numpy as jnp
from jax import lax
from jax.experimental import pallas as pl
from jax.experimental.pallas import tpu as pltpu
from jax.experimental.pallas import tpu_sc as plsc

F32 = jnp.float32
BF16 = jnp.bfloat16
I32 = jnp.int32
U32 = jnp.uint32

D_MODEL = 1024
EPS = 1e-6
GDN_HEADS = 4
GDN_DK = 128
GDN_DV = 128
GDN_CONV = 4
RET_HEADS = 4
RET_DK = 128
RET_DV = 256
ROPE_BASE = 10000.0
N_GROUPS = 4
EXPERTS_PER_GROUP = 8
N_EXPERTS = N_GROUPS * EXPERTS_PER_GROUP
D_EXPERT = 512

GDN_QK = GDN_HEADS * GDN_DK
GDN_V = GDN_HEADS * GDN_DV
RET_QK = RET_HEADS * RET_DK
RET_V = RET_HEADS * RET_DV

LANES = 128
CHUNK = 128
INV_BLOCK = 16
GDN_PREP_CHUNKS = 2
VMEM_LIMIT = 56 * 1024 * 1024

REPACK_COLS = 512
TOP_K = 2
SLOT_TILE = 512
PLAN_TILE = 256
PLAN_STEP = 1024
HALF = D_MODEL // 2
SC_CORES = 2
SC_WORKERS = SC_CORES * 16
SC_ROWS = 64

PROJ_COLS = 3 * GDN_QK + GDN_V + 2 * RET_QK + 2 * RET_V + 2 * D_MODEL


def _silu(x):
    return x / (1.0 + jnp.exp(-x))


def _sigmoid(x):
    return 1.0 / (1.0 + jnp.exp(-x))


def _dot(a, b):
    return jnp.dot(a, b, preferred_element_type=F32)


def _dot_nt(a, b):
    return lax.dot_general(a, b, (((1,), (1,)), ((), ())), preferred_element_type=F32)


def _pack_bf16_pairs(x):
    bits = lax.bitcast_convert_type(x.astype(BF16).astype(F32), U32)
    packed = (bits[:, :HALF] >> 16) | (bits[:, HALF:] & jnp.uint32(0xFFFF0000))
    return lax.bitcast_convert_type(packed, I32)


def _unpack_bf16_pairs(p):
    p = lax.bitcast_convert_type(p, U32)
    lo = lax.bitcast_convert_type(p << 16, F32)
    hi = lax.bitcast_convert_type(p & jnp.uint32(0xFFFF0000), F32)
    return jnp.concatenate([lo, hi], axis=1)


def _proj_kernel(x_ref, nw_ref, w_ref, wab_ref, proj_ref, ab_ref, u_ref):
    j = pl.program_id(1)

    def project():
        proj_ref[...] = _dot(u_ref[...], w_ref[...]).astype(BF16)

    @pl.when(j == 0)
    def _():
        x = x_ref[...]
        u = x * lax.rsqrt(jnp.mean(x * x, axis=-1, keepdims=True) + EPS) * nw_ref[...]
        ub = u.astype(BF16)
        u_ref[...] = ub
        ab_ref[...] = _dot(ub, wab_ref[...])
        project()

    @pl.when(j != 0)
    def _():
        project()


def _repack_kernel(wt_ref, abt_ref, main_ref, ab_ref):
    main_ref[...] = wt_ref[...].T.astype(BF16)

    @pl.when(pl.program_id(0) == 0)
    def _():
        ab = abt_ref[...].T
        ab_ref[...] = jnp.concatenate(
            [ab, jnp.zeros((ab.shape[0], LANES - ab.shape[1]), F32)], axis=1).astype(BF16)


def _repack_call(w_in_t):
    d_in, d = w_in_t.shape
    o_ab = 3 * GDN_QK
    n_ab = 2 * GDN_HEADS
    src = lambda r: pl.multiple_of(r * REPACK_COLS + jnp.where(r * REPACK_COLS >= o_ab, n_ab, 0), 8)
    return pl.pallas_call(
        _repack_kernel,
        grid=(PROJ_COLS // REPACK_COLS,),
        in_specs=[pl.BlockSpec((pl.Element(REPACK_COLS), pl.Element(d)), lambda r: (src(r), 0)),
                  pl.BlockSpec((pl.Element(n_ab), pl.Element(d)), lambda r: (o_ab, 0))],
        out_specs=[pl.BlockSpec((d, REPACK_COLS), lambda r: (0, r)),
                   pl.BlockSpec((d, LANES), lambda r: (0, 0))],
        out_shape=[jax.ShapeDtypeStruct((d, PROJ_COLS), BF16), jax.ShapeDtypeStruct((d, LANES), BF16)],
        compiler_params=pltpu.CompilerParams(dimension_semantics=("arbitrary",)),
        name="repack_w_in",
    )(w_in_t, w_in_t)


def _proj_call(x2, norm_w, w_main, w_ab, tm=1024, tn=3584):
    m = x2.shape[0]
    return pl.pallas_call(
        _proj_kernel,
        grid=(m // tm, PROJ_COLS // tn),
        in_specs=[
            pl.BlockSpec((tm, D_MODEL), lambda i, j: (i, 0)),
            pl.BlockSpec((1, D_MODEL), lambda i, j: (0, 0)),
            pl.BlockSpec((D_MODEL, tn), lambda i, j: (0, j)),
            pl.BlockSpec((D_MODEL, LANES), lambda i, j: (0, 0)),
        ],
        out_specs=[
            pl.BlockSpec((tm, tn), lambda i, j: (i, j)),
            pl.BlockSpec((tm, LANES), lambda i, j: (i, 0)),
        ],
        out_shape=[
            jax.ShapeDtypeStruct((m, PROJ_COLS), BF16),
            jax.ShapeDtypeStruct((m, LANES), F32),
        ],
        scratch_shapes=[pltpu.VMEM((tm, D_MODEL), BF16)],
        compiler_params=pltpu.CompilerParams(
            dimension_semantics=("arbitrary", "arbitrary"), vmem_limit_bytes=VMEM_LIMIT),
        name="proj",
    )(x2, norm_w, w_main, w_ab)


def _unit_lower_inverses(lows, ii, jj):
    eye = jnp.where(ii == jj, 1.0, 0.0).astype(F32)
    in_block = (ii // INV_BLOCK) == (jj // INV_BLOCK)
    ps = [jnp.where(in_block, -low, 0.0) for low in lows]
    ts = [eye + p for p in ps]
    span = 2
    while span < INV_BLOCK:
        ps = [_dot(p, p) for p in ps]
        ts = [t + _dot(t, p) for t, p in zip(ts, ps)]
        span *= 2
    s = INV_BLOCK
    while s < CHUNK:
        off_diag = ((ii // (2 * s)) == (jj // (2 * s))) & ((ii // s) != (jj // s))
        xs = [_dot(jnp.where(off_diag, low, 0.0), t) for low, t in zip(lows, ts)]
        ts = [t - _dot(t, x) for t, x in zip(ts, xs)]
        s *= 2
    return ts


def _gdn_kernel(qkv_ref, z_ref, ab_ref, convw_ref, alog_ref, dtb_ref, normw_ref, o_ref,
                b_s, o0_s, m_s, qp_s, gl_s, state_s, q_s, k_s, kb_s, rhs_s, dec_s, cv_s, qg_o, kdt_o,
                *, seq):
    nchunk = seq // CHUNK
    ii = lax.broadcasted_iota(jnp.int32, (CHUNK, CHUNK), 0)
    jj = lax.broadcasted_iota(jnp.int32, (CHUNK, CHUNK), 1)
    causal = ii >= jj
    strict = ii > jj
    tri = jnp.where(causal, 1.0, 0.0).astype(F32)
    neg_a = -jnp.exp(alog_ref[...])
    dtb = dtb_ref[...]

    def conv_cols(c, r0, lo, buf):
        x = qkv_ref[pl.ds(r0, CHUNK), lo:lo + LANES].astype(F32)
        prev0 = pl.multiple_of(jnp.maximum(r0 - 16, 0), 16)
        prev = qkv_ref[pl.ds(prev0, 16), lo:lo + LANES].astype(F32)
        buf[0:8, :] = prev[8:16] * jnp.where(c > 0, 1.0, 0.0)
        buf[8:8 + CHUNK, :] = x
        w = convw_ref[:, lo:lo + LANES]
        y = (w[3:4] * x + w[2:3] * buf[7:7 + CHUNK, :] + w[1:2] * buf[6:6 + CHUNK, :]
             + w[0:1] * buf[5:5 + CHUNK, :])
        return _silu(y)

    def l2n(x):
        return x * lax.rsqrt(jnp.sum(x * x, axis=-1, keepdims=True) + EPS)

    tri_b = tri.astype(BF16)

    def chunk_cumsum(g):
        g1 = g.astype(BF16)
        r1 = g - g1.astype(F32)
        g2 = r1.astype(BF16)
        g3 = (r1 - g2.astype(F32)).astype(BF16)
        return _dot(tri_b, g1) + (_dot(tri_b, g2) + _dot(tri_b, g3))

    def operands(cc, slot):
        for sub in range(GDN_PREP_CHUNKS):
            c = cc * GDN_PREP_CHUNKS + sub
            r0 = pl.multiple_of(c * CHUNK, CHUNK)
            ab = ab_ref[pl.ds(r0, CHUNK), :]
            xg = ab + dtb
            softplus = jnp.maximum(xg, 0.0) + jnp.log(1.0 + jnp.exp(-jnp.abs(xg)))
            g_all = neg_a * softplus
            beta_all = _sigmoid(ab)
            gc_all = chunk_cumsum(g_all)
            gc_t = gc_all.T
            gl_s[c] = jnp.exp(gc_all[CHUNK - 1:CHUNK, :])
            for h in range(GDN_HEADS):
                n = sub * GDN_HEADS + h
                bufs = [cv_s.at[slot, 3 * n + i] for i in range(3)]
                q = l2n(conv_cols(c, r0, h * GDN_DK, bufs[0])) * (GDN_DK ** -0.5)
                k = l2n(conv_cols(c, r0, GDN_QK + h * GDN_DK, bufs[1]))
                v = conv_cols(c, r0, 2 * GDN_QK + h * GDN_DV, bufs[2])
                gcol = gc_all[:, h:h + 1]
                grow = gc_t[h:h + 1, :]
                beta = beta_all[:, GDN_HEADS + h:GDN_HEADS + h + 1]
                dec_s[slot, n] = jnp.where(causal, jnp.exp(gcol - grow), 0.0)
                eg = jnp.exp(gcol)
                kb = k * beta
                q_s[slot, n] = q.astype(BF16)
                k_s[slot, n] = k.astype(BF16)
                kb_s[slot, n] = kb.astype(BF16)
                rhs_s[slot, n] = jnp.concatenate([v * beta, kb * eg], axis=1).astype(BF16)
                qg_o[slot, n] = q * eg
                kd = k * jnp.exp(gc_all[CHUNK - 1:CHUNK, h:h + 1] - gcol)
                kdt_o[slot, n] = kd.T.astype(BF16)

    def solve(cc, slot, between):
        items = [(cc * GDN_PREP_CHUNKS + sub, h, sub * GDN_HEADS + h)
                 for sub in range(GDN_PREP_CHUNKS) for h in range(GDN_HEADS)]
        kks = [_dot_nt(kb_s[slot, n], k_s[slot, n]) for _, _, n in items]
        qks = [_dot_nt(q_s[slot, n], k_s[slot, n]) for _, _, n in items]
        lows = [jnp.where(strict, kk * dec_s[slot, n], 0.0) for kk, (_, _, n) in zip(kks, items)]
        attns = [(qk * dec_s[slot, n]).astype(BF16) for qk, (_, _, n) in zip(qks, items)]
        for step in between[:len(between) // 2]:
            step()
        ts = _unit_lower_inverses(lows, ii, jj)
        for step in between[len(between) // 2:]:
            step()
        uws = [_dot(t.astype(BF16), rhs_s[slot, n]).astype(BF16) for t, (_, _, n) in zip(ts, items)]
        kds = [_dot(kdt_o[slot, n], uw) for uw, (_, _, n) in zip(uws, items)]
        ats = [_dot(attn, uw) for attn, uw in zip(attns, uws)]
        for kd_uw, at_uw, (c, h, n) in zip(kds, ats, items):
            b_s[c, h] = kd_uw[:, :GDN_DV]
            m_s[c, h] = (-kd_uw[:, GDN_DV:]).astype(BF16)
            o0_s[c, h] = at_uw[:, :GDN_DV]
            qp_s[c, h] = (qg_o[slot, n] - at_uw[:, GDN_DV:]).astype(BF16)

    ngroup = nchunk // GDN_PREP_CHUNKS
    per_trip = 2 * GDN_PREP_CHUNKS
    normw = normw_ref[...]
    state_s[...] = jnp.zeros_like(state_s)
    for c0 in range(per_trip):
        gl_s[c0] = jnp.zeros((1, LANES), F32)
        for h in range(GDN_HEADS):
            b_s[c0, h] = jnp.zeros((CHUNK, GDN_DV), F32)
            o0_s[c0, h] = jnp.zeros((CHUNK, GDN_DV), F32)
            m_s[c0, h] = jnp.zeros((CHUNK, GDN_DK), BF16)
            qp_s[c0, h] = jnp.zeros((CHUNK, GDN_DK), BF16)
    operands(0, 0)

    def prep(i, carry):
        first = jnp.maximum(per_trip * (i - 1), 0)
        steps = [functools.partial(scan, first + k, 0) for k in range(per_trip)]
        operands(2 * i + 1, 1)
        solve(2 * i, 0, steps[:per_trip // 2])
        operands(jnp.minimum(2 * i + 2, ngroup - 1), 0)
        solve(2 * i + 1, 1, steps[per_trip // 2:])
        return carry

    def scan(c, carry):
        r0 = pl.multiple_of(c * CHUNK, CHUNK)
        gl = gl_s[c]
        heads = range(GDN_HEADS)
        ss = [state_s[h] for h in heads]
        sbs = [s.astype(BF16) for s in ss]
        mss = [_dot(m_s[c, h], sbs[h]) for h in heads]
        qss = [_dot(qp_s[c, h], sbs[h]) for h in heads]
        for h in heads:
            state_s[h] = ss[h] * gl[:, h:h + 1] + (mss[h] + b_s[c, h])
            o = qss[h] + o0_s[c, h]
            z = z_ref[pl.ds(r0, CHUNK), h * GDN_DV:(h + 1) * GDN_DV].astype(F32)
            on = o * lax.rsqrt(jnp.mean(o * o, axis=-1, keepdims=True) + EPS) * normw
            o_ref[pl.ds(r0, CHUNK), h * GDN_DV:(h + 1) * GDN_DV] = (on * _silu(z)).astype(BF16)
        return carry

    ntrip = ngroup // 2
    lax.fori_loop(0, ntrip, prep, 0)
    lax.fori_loop(per_trip * (ntrip - 1), nchunk, scan, 0)


def _gdn_call(proj, ab, conv_w8, alog_p, dtb_p, normw, batch, seq):
    nchunk = seq // CHUNK
    hs = (nchunk, GDN_HEADS, CHUNK, CHUNK)
    ops = (2, GDN_PREP_CHUNKS * GDN_HEADS, CHUNK, CHUNK)
    return pl.pallas_call(
        functools.partial(_gdn_kernel, seq=seq),
        grid=(batch,),
        in_specs=[
            pl.BlockSpec((seq, 3 * GDN_QK), lambda b: (b, 0)),
            pl.BlockSpec((seq, GDN_V), lambda b: (b, 3)),
            pl.BlockSpec((seq, LANES), lambda b: (b, 0)),
            pl.BlockSpec((8, 3 * GDN_QK), lambda b: (0, 0)),
            pl.BlockSpec((1, LANES), lambda b: (0, 0)),
            pl.BlockSpec((1, LANES), lambda b: (0, 0)),
            pl.BlockSpec((1, GDN_DV), lambda b: (0, 0)),
        ],
        out_specs=pl.BlockSpec((seq, GDN_V), lambda b: (b, 0)),
        out_shape=jax.ShapeDtypeStruct((batch * seq, GDN_V), BF16),
        scratch_shapes=[
            pltpu.VMEM(hs, F32), pltpu.VMEM(hs, F32), pltpu.VMEM(hs, BF16), pltpu.VMEM(hs, BF16),
            pltpu.VMEM((nchunk, 1, LANES), F32),
            pltpu.VMEM((GDN_HEADS, GDN_DK, GDN_DV), F32),
            pltpu.VMEM(ops, BF16), pltpu.VMEM(ops, BF16), pltpu.VMEM(ops, BF16),
            pltpu.VMEM(ops[:3] + (2 * CHUNK,), BF16),
            pltpu.VMEM(ops, F32),
            pltpu.VMEM((2, 3 * ops[1], 8 + CHUNK, LANES), F32),
            pltpu.VMEM(ops, F32), pltpu.VMEM(ops, BF16),
        ],
        compiler_params=pltpu.CompilerParams(
            dimension_semantics=("arbitrary",), vmem_limit_bytes=VMEM_LIMIT),
        name="gdn",
    )(proj, proj, ab, conv_w8, alog_p, dtb_p, normw)


def _ret_kernel(qk_ref, v_ref, g_ref, sin_ref, cos_ref, inner_ref, kdec_ref, qdec_ref, cdec_ref,
                o_ref, state_s, q_s, qd_s, k_s, kt_s, *, seq):
    nchunk = seq // CHUNK
    lane = lax.broadcasted_iota(jnp.int32, (CHUNK, RET_DK), 1)
    even = (lane % 2) == 0

    def rotate(x, sin, cos):
        nxt = pltpu.roll(x, RET_DK - 1, axis=1)
        prv = pltpu.roll(x, 1, axis=1)
        return x * cos + jnp.where(even, -nxt, prv) * sin

    state_s[...] = jnp.zeros_like(state_s)
    kdec = kdec_ref[...]
    qdec = qdec_ref[...]
    cdec = cdec_ref[...]

    heads = range(RET_HEADS)

    def operands(c, slot):
        r0 = pl.multiple_of(c * CHUNK, CHUNK)
        sin = sin_ref[pl.ds(r0, CHUNK), :]
        cos = cos_ref[pl.ds(r0, CHUNK), :]
        for h in heads:
            q = rotate(qk_ref[pl.ds(r0, CHUNK), h * RET_DK:(h + 1) * RET_DK].astype(F32), sin, cos)
            k = rotate(qk_ref[pl.ds(r0, CHUNK), RET_QK + h * RET_DK:RET_QK + (h + 1) * RET_DK]
                       .astype(F32), sin, cos) * (RET_DK ** -0.5)
            q_s[slot, h] = q.astype(BF16)
            qd_s[slot, h] = (q * qdec[:, h:h + 1]).astype(BF16)
            k_s[slot, h] = k.astype(BF16)
            kt_s[slot, h] = (k * kdec[:, h:h + 1]).T.astype(BF16)

    def outputs(c, slot):
        r0 = pl.multiple_of(c * CHUNK, CHUNK)
        vs = [v_ref[pl.ds(r0, CHUNK), h * RET_DV:(h + 1) * RET_DV] for h in heads]
        ss = [state_s[h] for h in heads]
        qks = [_dot_nt(q_s[slot, h], k_s[slot, h]) for h in heads]
        inters = [_dot(qd_s[slot, h], ss[h].astype(BF16)) for h in heads]
        kvs = [_dot(kt_s[slot, h], vs[h]) for h in heads]
        intras = [_dot((qks[h] * inner_ref[h]).astype(BF16), vs[h]) for h in heads]
        for h in heads:
            state_s[h] = ss[h] * cdec[:, h:h + 1] + kvs[h]
            o = intras[h] + inters[h]
            gate = g_ref[pl.ds(r0, CHUNK), h * RET_DV:(h + 1) * RET_DV].astype(F32)
            on = o * lax.rsqrt(jnp.mean(o * o, axis=-1, keepdims=True) + EPS)
            o_ref[pl.ds(r0, CHUNK), h * RET_DV:(h + 1) * RET_DV] = (on * _silu(gate)).astype(BF16)

    operands(0, 0)

    def body(i, carry):
        operands(2 * i + 1, 1)
        outputs(2 * i, 0)
        operands(jnp.minimum(2 * i + 2, nchunk - 1), 0)
        outputs(2 * i + 1, 1)
        return carry

    lax.fori_loop(0, nchunk // 2, body, 0)


def _ret_call(proj, sin, cos, inner, kdec, qdec, cdec, batch, seq):
    return pl.pallas_call(
        functools.partial(_ret_kernel, seq=seq),
        grid=(batch,),
        in_specs=[
            pl.BlockSpec((seq, 2 * RET_QK), lambda b: (b, 2)),
            pl.BlockSpec((seq, RET_V), lambda b: (b, 3)),
            pl.BlockSpec((seq, RET_V), lambda b: (b, 4)),
            pl.BlockSpec((seq, RET_DK), lambda b: (0, 0)),
            pl.BlockSpec((seq, RET_DK), lambda b: (0, 0)),
            pl.BlockSpec((RET_HEADS, CHUNK, CHUNK), lambda b: (0, 0, 0)),
            pl.BlockSpec((CHUNK, LANES), lambda b: (0, 0)),
            pl.BlockSpec((CHUNK, LANES), lambda b: (0, 0)),
            pl.BlockSpec((1, LANES), lambda b: (0, 0)),
        ],
        out_specs=pl.BlockSpec((seq, RET_V), lambda b: (b, 0)),
        out_shape=jax.ShapeDtypeStruct((batch * seq, RET_V), BF16),
        scratch_shapes=[pltpu.VMEM((RET_HEADS, RET_DK, RET_DV), F32)]
        + [pltpu.VMEM((2, RET_HEADS, CHUNK, RET_DK), BF16)] * 4,
        compiler_params=pltpu.CompilerParams(
            dimension_semantics=("arbitrary",), vmem_limit_bytes=VMEM_LIMIT),
        name="retention",
    )(proj, proj, proj, sin, cos, inner, kdec, qdec, cdec)


def _retention_tables(seq):
    inv_freq = 1.0 / (ROPE_BASE ** jnp.linspace(0.0, 1.0, RET_DK // 2, dtype=F32))
    ang = jnp.arange(seq, dtype=F32)[:, None] * inv_freq[None, :]
    sin = jnp.repeat(jnp.sin(ang), 2, axis=-1)
    cos = jnp.repeat(jnp.cos(ang), 2, axis=-1)
    log_gamma = jnp.log(1.0 - 2.0 ** (-5.0 - jnp.arange(RET_HEADS, dtype=F32)))
    idx = jnp.arange(CHUNK, dtype=F32)
    causal = jnp.tril(jnp.ones((CHUNK, CHUNK), dtype=bool))
    rel = jnp.where(causal, idx[:, None] - idx[None, :], 0.0)
    inner = jnp.where(causal, jnp.exp(rel[None] * log_gamma[:, None, None]), 0.0)
    k_decay = jnp.exp(log_gamma[:, None] * (CHUNK - 1.0 - idx)[None, :])
    q_decay = jnp.exp(log_gamma[:, None] * (idx + 1.0)[None, :])
    chunk_decay = jnp.exp(log_gamma * CHUNK)
    pad = LANES - RET_HEADS
    kdec = jnp.pad(k_decay.T, ((0, 0), (0, pad)))
    qdec = jnp.pad(q_decay.T, ((0, 0), (0, pad)))
    cdec = jnp.pad(chunk_decay[None, :], ((0, 0), (0, pad)))
    return sin, cos, inner, kdec, qdec, cdec


def _merge_kernel(x_ref, ya_ref, yb_ref, ma_ref, mb_ref, wa_ref, wr_ref, wo_ref, nw_ref,
                  wrt_ref, br_ref, h_ref, xn_ref, ridx_ref, rw_ref, cnt_ref, logit_s):
    i = pl.program_id(0)
    n = pl.num_programs(0) - 1
    tm = x_ref.shape[0]

    def route_previous():
        counts = _route(logit_s[...], ridx_ref, rw_ref)
        cnt_ref[...] += jnp.broadcast_to(counts * jnp.where(i > 0, 1.0, 0.0), cnt_ref.shape)

    @pl.when(i == 0)
    def _():
        cnt_ref[...] = jnp.zeros_like(cnt_ref)
        logit_s[...] = jnp.zeros_like(logit_s)

    @pl.when(i < n)
    def _():
        route_previous()
        a = _dot(ya_ref[...], wa_ref[...])
        r = _dot(yb_ref[...], wr_ref[...])
        merged = _sigmoid(ma_ref[...].astype(F32)) * a + _sigmoid(mb_ref[...].astype(F32)) * r
        h = x_ref[...] + _dot(merged.astype(BF16), wo_ref[...])
        h_ref[...] = h
        xn = h * lax.rsqrt(jnp.mean(h * h, axis=-1, keepdims=True) + EPS) * nw_ref[...]
        xn_ref[...] = _pack_bf16_pairs(xn)
        xh = xn.astype(BF16)
        xl = (xn - xh.astype(F32)).astype(BF16)
        parts = _dot(jnp.concatenate([xh, xl], axis=0), wrt_ref[...])
        logit_s[...] = (parts[:tm, :LANES] + (parts[tm:, :LANES] + parts[:tm, LANES:]
                                              + parts[tm:, LANES:])) + br_ref[...]

    @pl.when(i == n)
    def _():
        route_previous()


def _route(logits, ridx_ref, rw_ref):
    tm = logits.shape[0]
    lane = lax.broadcasted_iota(jnp.int32, (tm, LANES), 1)
    neg = -jnp.inf
    gl = jnp.where((lane >= N_EXPERTS) & (lane < N_EXPERTS + N_GROUPS), logits, neg)
    gmax = jnp.max(gl, axis=-1, keepdims=True)
    gidx = jnp.min(jnp.where(gl == gmax, lane, LANES), axis=-1, keepdims=True) - N_EXPERTS
    g_w = 1.0 / jnp.sum(jnp.exp(gl - gmax), axis=-1, keepdims=True)
    el = jnp.where((lane // EXPERTS_PER_GROUP == gidx) & (lane < N_EXPERTS), logits, neg)
    m1 = jnp.max(el, axis=-1, keepdims=True)
    i1 = jnp.min(jnp.where(el == m1, lane, LANES), axis=-1, keepdims=True)
    el2 = jnp.where(lane == i1, neg, el)
    m2 = jnp.max(el2, axis=-1, keepdims=True)
    i2 = jnp.min(jnp.where(el2 == m2, lane, LANES), axis=-1, keepdims=True)
    e2 = jnp.exp(m2 - m1)
    p1 = g_w / (1.0 + e2)
    p2 = g_w * e2 / (1.0 + e2)
    ridx_ref[...] = jnp.where(lane == 0, i1, jnp.where(lane == 1, i2, 0))
    rw_ref[...] = jnp.where(lane == 0, p1, jnp.where(lane == 1, p2, 0.0))
    onehot = jnp.where((lane == i1) | (lane == i2), 1.0, 0.0)
    return jnp.sum(onehot, axis=0, keepdims=True)


def _merge_call(x2, ya, yb, proj, wa, wr, wo, nw, w_router, b_router, tm=1024):
    m = x2.shape[0]
    n = m // tm
    full = lambda shape: pl.BlockSpec(shape, lambda i: (0, 0))
    cur = lambda i: jnp.minimum(i, n - 1)
    prev = lambda i: jnp.maximum(i - 1, 0)
    return pl.pallas_call(
        _merge_kernel,
        grid=(n + 1,),
        in_specs=[
            pl.BlockSpec((tm, D_MODEL), lambda i: (cur(i), 0)),
            pl.BlockSpec((tm, GDN_V), lambda i: (cur(i), 0)),
            pl.BlockSpec((tm, RET_V), lambda i: (cur(i), 0)),
            pl.BlockSpec((tm, D_MODEL), lambda i: (cur(i), 5)),
            pl.BlockSpec((tm, D_MODEL), lambda i: (cur(i), 6)),
            full((GDN_V, D_MODEL)), full((RET_V, D_MODEL)), full((D_MODEL, D_MODEL)),
            full((1, D_MODEL)),
            full((D_MODEL, 2 * LANES)), full((1, LANES)),
        ],
        out_specs=[
            pl.BlockSpec((tm, D_MODEL), lambda i: (cur(i), 0)),
            pl.BlockSpec((tm, HALF), lambda i: (cur(i), 0)),
            pl.BlockSpec((tm, LANES), lambda i: (prev(i), 0)),
            pl.BlockSpec((tm, LANES), lambda i: (prev(i), 0)),
            pl.BlockSpec((8, LANES), lambda i: (0, 0)),
        ],
        out_shape=[
            jax.ShapeDtypeStruct((m, D_MODEL), F32),
            jax.ShapeDtypeStruct((m, HALF), I32),
            jax.ShapeDtypeStruct((m, LANES), I32),
            jax.ShapeDtypeStruct((m, LANES), F32),
            jax.ShapeDtypeStruct((8, LANES), F32),
        ],
        scratch_shapes=[pltpu.VMEM((tm, LANES), F32)],
        compiler_params=pltpu.CompilerParams(
            dimension_semantics=("arbitrary",), vmem_limit_bytes=VMEM_LIMIT),
        name="merge_router",
    )(x2, ya, yb, proj, proj, wa, wr, wo, nw, w_router, b_router)


def _slot_counts(m):
    n_slots = TOP_K * m + N_EXPERTS * SLOT_TILE
    return n_slots, n_slots // SLOT_TILE


def _lane_prefix_sum(x, lane):
    s = 1
    while s < LANES:
        x = x + jnp.where(lane >= s, pltpu.roll(x, s, axis=1), 0.0)
        s *= 2
    return x


def _plan_kernel(ridx_ref, cnt_ref, slots_ref, tile_ref, carry_s, off_s):
    i = pl.program_id(0)
    lane = lax.broadcasted_iota(I32, (PLAN_TILE, LANES), 1)
    row = lax.broadcasted_iota(I32, (PLAN_TILE, LANES), 0)
    lane1 = lane[0:1]

    @pl.when(i == 0)
    def _():
        cnt = cnt_ref[0:1, :]
        tile = float(SLOT_TILE)
        padded = jnp.floor((cnt + (tile - 1.0)) / tile) * tile
        incl = _lane_prefix_sum(padded, lane1)
        off = incl - padded
        off_s[...] = off
        carry_s[...] = jnp.zeros_like(carry_s)
        first = (row * SLOT_TILE).astype(F32)
        ended = jnp.where((lane < N_EXPERTS) & (incl <= first), 1.0, 0.0)
        tile_e = jnp.sum(ended, axis=-1, keepdims=True)
        last = jnp.sum(jnp.where(lane.astype(F32) == tile_e, off + cnt, 0.0), axis=-1, keepdims=True)
        used = jnp.clip(last - first[:, 0:1], 0.0, tile)
        tile_ref[...] = jnp.where(lane == 0, tile_e, jnp.where(lane == 1, used, 0.0)).astype(I32)

    strict = jnp.where(row[:, 0:1] > lax.broadcasted_iota(I32, (PLAN_TILE, PLAN_TILE), 1),
                       1.0, 0.0).astype(BF16)
    off = off_s[...]
    carry = carry_s[...]
    for sb in range(PLAN_STEP // PLAN_TILE):
        rows = pl.ds(sb * PLAN_TILE, PLAN_TILE)
        e1 = ridx_ref[rows, 0:1]
        e2 = ridx_ref[rows, 1:2]
        onehot = jnp.where((lane == e1) | (lane == e2), 1.0, 0.0)
        pos = _dot(strict, onehot.astype(BF16)) + (carry + off)
        s1 = jnp.sum(jnp.where(lane == e1, pos, 0.0), axis=-1, keepdims=True)
        s2 = jnp.sum(jnp.where(lane == e2, pos, 0.0), axis=-1, keepdims=True)
        both = jnp.where(lane == 0, s1, jnp.where(lane == 1, s2, 0.0))
        for q in range(PLAN_TILE // LANES):
            t = both[q * LANES:(q + 1) * LANES].T
            c0 = sb * PLAN_TILE + q * LANES
            slots_ref[:, c0:c0 + LANES] = t[0:8].astype(I32)
        carry = carry + jnp.sum(onehot, axis=0, keepdims=True)
    carry_s[...] = carry


def _plan_call(ridx, cnt):
    m = ridx.shape[0]
    _, n_tiles = _slot_counts(m)
    assert n_tiles <= PLAN_TILE
    return pl.pallas_call(
        _plan_kernel,
        grid=(m // PLAN_STEP,),
        in_specs=[pl.BlockSpec((PLAN_STEP, LANES), lambda i: (i, 0)),
                  pl.BlockSpec((8, LANES), lambda i: (0, 0))],
        out_specs=[
            pl.BlockSpec((8, PLAN_STEP), lambda i: (0, i)),
            pl.BlockSpec((PLAN_TILE, LANES), lambda i: (0, 0)),
        ],
        out_shape=[
            jax.ShapeDtypeStruct((8, m), I32),
            jax.ShapeDtypeStruct((PLAN_TILE, LANES), I32),
        ],
        scratch_shapes=[pltpu.VMEM((1, LANES), F32), pltpu.VMEM((1, LANES), F32)],
        compiler_params=pltpu.CompilerParams(dimension_semantics=("arbitrary",)),
        name="dispatch_plan",
    )(ridx, cnt)


def _sc_mesh():
    return plsc.VectorSubcoreMesh(core_axis_name="c", subcore_axis_name="s")


def _sc_worker():
    return lax.axis_index("s") * SC_CORES + lax.axis_index("c")


def _sc_dispatch(xn, slot1, slot2, n_rows):
    m = xn.shape[0]
    per = m // SC_WORKERS
    n_pairs = per // (2 * SC_ROWS)

    @functools.partial(
        pl.kernel, mesh=_sc_mesh(),
        out_type=jax.ShapeDtypeStruct((n_rows, HALF), I32),
        scratch_types=[pltpu.VMEM((per // SC_ROWS, SC_ROWS), I32), pltpu.VMEM((per // SC_ROWS, SC_ROWS), I32),
                       pltpu.VMEM((SC_ROWS, HALF), I32), pltpu.VMEM((SC_ROWS, HALF), I32),
                       pltpu.SemaphoreType.DMA, pltpu.SemaphoreType.DMA, pltpu.SemaphoreType.DMA],
        name="sc_dispatch")
    def k(x_hbm, s1_hbm, s2_hbm, o_hbm, i1_v, i2_v, rows0, rows1, sem_r0, sem_r1, sem_w):
        wid = _sc_worker()
        base = wid * per

        def read(chunk, rows_v, sem):
            return pltpu.make_async_copy(x_hbm.at[pl.ds(base + chunk * SC_ROWS, SC_ROWS)], rows_v, sem)

        def scatter(chunk, rows_v):
            c1 = pltpu.async_copy(rows_v, o_hbm.at[i1_v.at[chunk]], sem_w)
            c2 = pltpu.async_copy(rows_v, o_hbm.at[i2_v.at[chunk]], sem_w)
            c1.wait()
            c2.wait()

        read(0, rows0, sem_r0).start()
        pltpu.sync_copy(s1_hbm.at[pl.ds(wid * (per // SC_ROWS), per // SC_ROWS)], i1_v)
        pltpu.sync_copy(s2_hbm.at[pl.ds(wid * (per // SC_ROWS), per // SC_ROWS)], i2_v)

        @pl.loop(0, n_pairs)
        def _(i):
            read(2 * i, rows0, sem_r0).wait()
            read(2 * i + 1, rows1, sem_r1).start()
            scatter(2 * i, rows0)
            read(2 * i + 1, rows1, sem_r1).wait()

            @pl.when(i + 1 < n_pairs)
            def _():
                read(2 * i + 2, rows0, sem_r0).start()

            scatter(2 * i + 1, rows1)

    return k(xn, slot1, slot2)


def _sc_collect(ys, slot1, slot2):
    m = slot1.size
    per = m // SC_WORKERS
    row = jax.ShapeDtypeStruct((m, HALF), I32)

    @functools.partial(
        pl.kernel, mesh=_sc_mesh(), out_type=[row, row],
        scratch_types=[pltpu.VMEM((per // SC_ROWS, SC_ROWS), I32), pltpu.VMEM((per // SC_ROWS, SC_ROWS), I32),
                       pltpu.VMEM((SC_ROWS, HALF), I32), pltpu.VMEM((SC_ROWS, HALF), I32),
                       pltpu.SemaphoreType.DMA, pltpu.SemaphoreType.DMA],
        name="sc_collect")
    def k(y_hbm, s1_hbm, s2_hbm, g1_hbm, g2_hbm, i1_v, i2_v, rows1, rows2, sem_g, sem_w):
        wid = _sc_worker()
        base = wid * per
        pltpu.sync_copy(s1_hbm.at[pl.ds(wid * (per // SC_ROWS), per // SC_ROWS)], i1_v)
        pltpu.sync_copy(s2_hbm.at[pl.ds(wid * (per // SC_ROWS), per // SC_ROWS)], i2_v)

        @pl.loop(0, per // SC_ROWS)
        def _(ci):
            t0 = base + ci * SC_ROWS
            a1 = pltpu.async_copy(y_hbm.at[i1_v.at[ci]], rows1, sem_g)
            a2 = pltpu.async_copy(y_hbm.at[i2_v.at[ci]], rows2, sem_g)
            a1.wait()
            a2.wait()
            w1 = pltpu.async_copy(rows1, g1_hbm.at[pl.ds(t0, SC_ROWS)], sem_w)
            w2 = pltpu.async_copy(rows2, g2_hbm.at[pl.ds(t0, SC_ROWS)], sem_w)
            w1.wait()
            w2.wait()

    return k(ys, slot1, slot2)


def _expert_kernel(te_ref, used_ref, xs_ref, wg_hbm, wu_hbm, wd_hbm, ys_ref, wg_b, wu_b, wd_b,
                   wg_f, wu_f, wd_f, sem, slot_s):
    j = pl.program_id(0)
    n = pl.num_programs(0)
    e = te_ref[j]
    prev = te_ref[jnp.maximum(j - 1, 0)]
    valid = e < N_EXPERTS

    def weight_copies(expert, slot):
        return [pltpu.make_async_copy(hbm.at[expert], buf.at[slot], sem.at[slot, i])
                for i, (hbm, buf) in enumerate(((wg_hbm, wg_f), (wu_hbm, wu_f), (wd_hbm, wd_f)))]

    def run_tile():
        half = SLOT_TILE // 2
        rows = [pl.ds(i * half, half) for i in range(2)]
        row_id = lax.broadcasted_iota(I32, (half, HALF), 0)
        xs = [_unpack_bf16_pairs(jnp.where(row_id + i * half < used_ref[j], xs_ref[r, :], 0))
              .astype(BF16) for i, r in enumerate(rows)]
        gs = [_dot(x, wg_b[...]) for x in xs]
        us = [_dot(x, wu_b[...]) for x in xs]
        hids = [(_silu(g) * u).astype(BF16) for g, u in zip(gs, us)]
        ys = [_dot(hid, wd_b[...]) for hid in hids]
        for r, y in zip(rows, ys):
            ys_ref[r, :] = _pack_bf16_pairs(y)

    @pl.when((j == 0) & valid)
    def _():
        slot_s[0] = 0
        for c in weight_copies(e, 0):
            c.start()

    first_tile = ((j == 0) | (e != prev)) & valid

    @pl.when(first_tile)
    def _():
        slot = slot_s[0]
        for c in weight_copies(e, slot):
            c.wait()
        k = lax.while_loop(lambda k: (k < n) & (te_ref[jnp.minimum(k, n - 1)] == e),
                           lambda k: k + 1, j + 1)
        nxt = te_ref[jnp.minimum(k, n - 1)]

        @pl.when((k < n) & (nxt < N_EXPERTS))
        def _():
            for c in weight_copies(nxt, 1 - slot):
                c.start()

        wg_b[...] = wg_f[slot].astype(BF16)
        wu_b[...] = wu_f[slot].astype(BF16)
        wd_b[...] = wd_f[slot].astype(BF16)
        slot_s[0] = 1 - slot
        run_tile()

    @pl.when(valid & jnp.logical_not(first_tile))
    def _():
        run_tile()

    @pl.when(e >= N_EXPERTS)
    def _():
        ys_ref[...] = jnp.zeros_like(ys_ref)


def _expert_call(tile_expert, tile_used, xs, wg, wu, wd, n_tiles):
    hbm = pl.BlockSpec(memory_space=pl.ANY)
    return pl.pallas_call(
        _expert_kernel,
        grid_spec=pltpu.PrefetchScalarGridSpec(
            num_scalar_prefetch=2,
            grid=(n_tiles,),
            in_specs=[pl.BlockSpec((SLOT_TILE, HALF), lambda j, te, used: (j, 0)), hbm, hbm, hbm],
            out_specs=pl.BlockSpec((SLOT_TILE, HALF), lambda j, te, used: (j, 0)),
            scratch_shapes=[
                pltpu.VMEM((D_MODEL, D_EXPERT), BF16), pltpu.VMEM((D_MODEL, D_EXPERT), BF16),
                pltpu.VMEM((D_EXPERT, D_MODEL), BF16),
                pltpu.VMEM((2, D_MODEL, D_EXPERT), F32), pltpu.VMEM((2, D_MODEL, D_EXPERT), F32),
                pltpu.VMEM((2, D_EXPERT, D_MODEL), F32),
                pltpu.SemaphoreType.DMA((2, 3)),
                pltpu.SMEM((1,), I32),
            ],
        ),
        out_shape=jax.ShapeDtypeStruct((n_tiles * SLOT_TILE, HALF), I32),
        compiler_params=pltpu.CompilerParams(
            dimension_semantics=("arbitrary",), vmem_limit_bytes=VMEM_LIMIT),
        name="experts",
    )(tile_expert, tile_used, xs, wg, wu, wd)


def _final_kernel(h_ref, g1_ref, g2_ref, rw_ref, nw_ref, o_ref):
    rw = rw_ref[...]
    y = rw[:, 0:1] * _unpack_bf16_pairs(g1_ref[...]) + rw[:, 1:2] * _unpack_bf16_pairs(g2_ref[...])
    h = h_ref[...] + y
    o_ref[...] = h * lax.rsqrt(jnp.mean(h * h, axis=-1, keepdims=True) + EPS) * nw_ref[...]


def _final_call(h1, g1, g2, rw, nw, tm=1024):
    m = h1.shape[0]
    return pl.pallas_call(
        _final_kernel,
        grid=(m // tm,),
        in_specs=[
            pl.BlockSpec((tm, D_MODEL), lambda i: (i, 0)),
            pl.BlockSpec((tm, HALF), lambda i: (i, 0)),
            pl.BlockSpec((tm, HALF), lambda i: (i, 0)),
            pl.BlockSpec((tm, LANES), lambda i: (i, 0)),
            pl.BlockSpec((1, D_MODEL), lambda i: (0, 0)),
        ],
        out_specs=pl.BlockSpec((tm, D_MODEL), lambda i: (i, 0)),
        out_shape=jax.ShapeDtypeStruct((m, D_MODEL), F32),
        compiler_params=pltpu.CompilerParams(dimension_semantics=("arbitrary",)),
        name="combine_final",
    )(h1, g1, g2, rw, nw)


def _pad_lanes(a):
    return jnp.pad(a, ((0, 0), (0, LANES - a.shape[1])))


def kernel(x, norm_mix_w, w_in, conv_w, A_log, dt_bias, gdn_norm_w, w_up_gdn, w_up_ret, w_out,
           norm_ffn_w, w_group, b_group, w_expert, b_expert, w_gate, w_up, w_down, norm_final_w):
    batch, seq, d = x.shape
    m = batch * seq
    h = x.reshape(m, d)
    depth = w_in.shape[0]
    sin, cos, inner, kdec, qdec, cdec = _retention_tables(seq)
    for l in range(depth):
        w_main, w_ab = _repack_call(jnp.transpose(w_in[l]))
        proj, ab = _proj_call(h, norm_mix_w[l][None, :], w_main, w_ab)

        conv8 = jnp.pad(conv_w[l], ((0, 8 - GDN_CONV), (0, 0)))
        ya = _gdn_call(proj, ab, conv8, _pad_lanes(A_log[l][None, :]), _pad_lanes(dt_bias[l][None, :]),
                       gdn_norm_w[l][None, :], batch, seq)
        yb = _ret_call(proj, sin, cos, inner, kdec, qdec, cdec, batch, seq)

        w_router = _pad_lanes(jnp.concatenate([w_expert[l], w_group[l]], axis=1))
        wr_hi = w_router.astype(BF16)
        wr_lo = (w_router - wr_hi.astype(F32)).astype(BF16)
        b_router = _pad_lanes(jnp.concatenate([b_expert[l], b_group[l]])[None, :])
        h1, xn, ridx, rw, cnt = _merge_call(
            h, ya, yb, proj, w_up_gdn[l].astype(BF16), w_up_ret[l].astype(BF16), w_out[l].astype(BF16),
            norm_ffn_w[l][None, :], jnp.concatenate([wr_hi, wr_lo], axis=1), b_router)

        n_slots, n_tiles = _slot_counts(m)
        slots, tiles = _plan_call(ridx, cnt)
        slot1, slot2 = slots[0].reshape(-1, SC_ROWS), slots[1].reshape(-1, SC_ROWS)
        xs = _sc_dispatch(xn, slot1, slot2, n_slots)
        ys = _expert_call(tiles[:n_tiles, 0], tiles[:n_tiles, 1], xs,
                          w_gate[l], w_up[l], w_down[l], n_tiles)
        g1, g2 = _sc_collect(ys, slot1, slot2)

        assert depth == 1
        h = _final_call(h1, g1, g2, rw, norm_final_w[None, :])
    return h.reshape(batch, seq, d)
```

```python
import functools

import jax
import jax.numpy as jnp
from jax import lax
from jax.experimental import pallas as pl
from jax.experimental.pallas import tpu as pltpu
from jax.experimental.pallas import tpu_sc as plsc

F32 = jnp.float32
BF16 = jnp.bfloat16
I32 = jnp.int32
U32 = jnp.uint32

D_MODEL = 1024
EPS = 1e-6
GDN_HEADS = 4
GDN_DK = 128
GDN_DV = 128
GDN_CONV = 4
RET_HEADS = 4
RET_DK = 128
RET_DV = 256
ROPE_BASE = 10000.0
N_GROUPS = 4
EXPERTS_PER_GROUP = 8
N_EXPERTS = N_GROUPS * EXPERTS_PER_GROUP
D_EXPERT = 512

GDN_QK = GDN_HEADS * GDN_DK
GDN_V = GDN_HEADS * GDN_DV
RET_QK = RET_HEADS * RET_DK
RET_V = RET_HEADS * RET_DV

LANES = 128
CHUNK = 128
INV_BLOCK = 16
GDN_PREP_CHUNKS = 2
VMEM_LIMIT = 56 * 1024 * 1024

REPACK_COLS = 512
TOP_K = 2
SLOT_TILE = 512
PLAN_TILE = 256
PLAN_STEP = 1024
HALF = D_MODEL // 2
SC_CORES = 2
SC_WORKERS = SC_CORES * 16
SC_ROWS = 64

PROJ_COLS = 3 * GDN_QK + GDN_V + 2 * RET_QK + 2 * RET_V + 2 * D_MODEL


def _silu(x):
    return x / (1.0 + jnp.exp(-x))


def _sigmoid(x):
    return 1.0 / (1.0 + jnp.exp(-x))


def _dot(a, b):
    return jnp.dot(a, b, preferred_element_type=F32)


def _dot_nt(a, b):
    return lax.dot_general(a, b, (((1,), (1,)), ((), ())), preferred_element_type=F32)


def _pack_bf16_pairs(x):
    bits = lax.bitcast_convert_type(x.astype(BF16).astype(F32), U32)
    packed = (bits[:, :HALF] >> 16) | (bits[:, HALF:] & jnp.uint32(0xFFFF0000))
    return lax.bitcast_convert_type(packed, I32)


def _unpack_bf16_pairs(p):
    p = lax.bitcast_convert_type(p, U32)
    lo = lax.bitcast_convert_type(p << 16, F32)
    hi = lax.bitcast_convert_type(p & jnp.uint32(0xFFFF0000), F32)
    return jnp.concatenate([lo, hi], axis=1)


def _proj_kernel(x_ref, nw_ref, w_ref, wab_ref, proj_ref, ab_ref, u_ref):
    j = pl.program_id(1)

    def project():
        proj_ref[...] = _dot(u_ref[...], w_ref[...]).astype(BF16)

    @pl.when(j == 0)
    def _():
        x = x_ref[...]
        u = x * lax.rsqrt(jnp.mean(x * x, axis=-1, keepdims=True) + EPS) * nw_ref[...]
        ub = u.astype(BF16)
        u_ref[...] = ub
        ab_ref[...] = _dot(ub, wab_ref[...])
        project()

    @pl.when(j != 0)
    def _():
        project()


def _repack_kernel(wt_ref, abt_ref, main_ref, ab_ref):
    main_ref[...] = wt_ref[...].T.astype(BF16)

    @pl.when(pl.program_id(0) == 0)
    def _():
        ab = abt_ref[...].T
        ab_ref[...] = jnp.concatenate(
            [ab, jnp.zeros((ab.shape[0], LANES - ab.shape[1]), F32)], axis=1).astype(BF16)


def _repack_call(w_in_t):
    d_in, d = w_in_t.shape
    o_ab = 3 * GDN_QK
    n_ab = 2 * GDN_HEADS
    src = lambda r: pl.multiple_of(r * REPACK_COLS + jnp.where(r * REPACK_COLS >= o_ab, n_ab, 0), 8)
    return pl.pallas_call(
        _repack_kernel,
        grid=(PROJ_COLS // REPACK_COLS,),
        in_specs=[pl.BlockSpec((pl.Element(REPACK_COLS), pl.Element(d)), lambda r: (src(r), 0)),
                  pl.BlockSpec((pl.Element(n_ab), pl.Element(d)), lambda r: (o_ab, 0))],
        out_specs=[pl.BlockSpec((d, REPACK_COLS), lambda r: (0, r)),
                   pl.BlockSpec((d, LANES), lambda r: (0, 0))],
        out_shape=[jax.ShapeDtypeStruct((d, PROJ_COLS), BF16), jax.ShapeDtypeStruct((d, LANES), BF16)],
        compiler_params=pltpu.CompilerParams(dimension_semantics=("arbitrary",)),
        name="repack_w_in",
    )(w_in_t, w_in_t)


def _proj_call(x2, norm_w, w_main, w_ab, tm=1024, tn=3584):
    m = x2.shape[0]
    return pl.pallas_call(
        _proj_kernel,
        grid=(m // tm, PROJ_COLS // tn),
        in_specs=[
            pl.BlockSpec((tm, D_MODEL), lambda i, j: (i, 0)),
            pl.BlockSpec((1, D_MODEL), lambda i, j: (0, 0)),
            pl.BlockSpec((D_MODEL, tn), lambda i, j: (0, j)),
            pl.BlockSpec((D_MODEL, LANES), lambda i, j: (0, 0)),
        ],
        out_specs=[
            pl.BlockSpec((tm, tn), lambda i, j: (i, j)),
            pl.BlockSpec((tm, LANES), lambda i, j: (i, 0)),
        ],
        out_shape=[
            jax.ShapeDtypeStruct((m, PROJ_COLS), BF16),
            jax.ShapeDtypeStruct((m, LANES), F32),
        ],
        scratch_shapes=[pltpu.VMEM((tm, D_MODEL), BF16)],
        compiler_params=pltpu.CompilerParams(
            dimension_semantics=("arbitrary", "arbitrary"), vmem_limit_bytes=VMEM_LIMIT),
        name="proj",
    )(x2, norm_w, w_main, w_ab)


def _unit_lower_inverses(lows, ii, jj):
    eye = jnp.where(ii == jj, 1.0, 0.0).astype(F32)
    in_block = (ii // INV_BLOCK) == (jj // INV_BLOCK)
    ps = [jnp.where(in_block, -low, 0.0) for low in lows]
    ts = [eye + p for p in ps]
    span = 2
    while span < INV_BLOCK:
        ps = [_dot(p, p) for p in ps]
        ts = [t + _dot(t, p) for t, p in zip(ts, ps)]
        span *= 2
    s = INV_BLOCK
    while s < CHUNK:
        off_diag = ((ii // (2 * s)) == (jj // (2 * s))) & ((ii // s) != (jj // s))
        xs = [_dot(jnp.where(off_diag, low, 0.0), t) for low, t in zip(lows, ts)]
        ts = [t - _dot(t, x) for t, x in zip(ts, xs)]
        s *= 2
    return ts


def _gdn_kernel(qkv_ref, z_ref, ab_ref, qkv_nx_ref, ab_nx_ref, convw_ref, alog_ref, dtb_ref, normw_ref, o_ref,
                b_s, o0_s, m_s, qp_s, gl_s, state_s, q_s, k_s, kb_s, rhs_s, dec_s, cv_s, qg_o, kdt_o,
                *, seq):
    nchunk = seq // CHUNK
    ii = lax.broadcasted_iota(jnp.int32, (CHUNK, CHUNK), 0)
    jj = lax.broadcasted_iota(jnp.int32, (CHUNK, CHUNK), 1)
    causal = ii >= jj
    strict = ii > jj
    tri = jnp.where(causal, 1.0, 0.0).astype(F32)
    neg_a = -jnp.exp(alog_ref[...])
    dtb = dtb_ref[...]

    def conv_cols(c, r0, lo, buf, qkv):
        x = qkv[pl.ds(r0, CHUNK), lo:lo + LANES].astype(F32)
        prev0 = pl.multiple_of(jnp.maximum(r0 - 16, 0), 16)
        prev = qkv[pl.ds(prev0, 16), lo:lo + LANES].astype(F32)
        buf[0:8, :] = prev[8:16] * jnp.where(c > 0, 1.0, 0.0)
        buf[8:8 + CHUNK, :] = x
        w = convw_ref[:, lo:lo + LANES]
        y = (w[3:4] * x + w[2:3] * buf[7:7 + CHUNK, :] + w[1:2] * buf[6:6 + CHUNK, :]
             + w[0:1] * buf[5:5 + CHUNK, :])
        return _silu(y)

    def l2n(x):
        return x * lax.rsqrt(jnp.sum(x * x, axis=-1, keepdims=True) + EPS)

    tri_b = tri.astype(BF16)

    def chunk_cumsum(g):
        g1 = g.astype(BF16)
        r1 = g - g1.astype(F32)
        g2 = r1.astype(BF16)
        g3 = (r1 - g2.astype(F32)).astype(BF16)
        return _dot(tri_b, g1) + (_dot(tri_b, g2) + _dot(tri_b, g3))

    def operands(cc, slot, qkv=qkv_ref, ab_src=ab_ref):
        for sub in range(GDN_PREP_CHUNKS):
            c = cc * GDN_PREP_CHUNKS + sub
            r0 = pl.multiple_of(c * CHUNK, CHUNK)
            ab = ab_src[pl.ds(r0, CHUNK), :]
            xg = ab + dtb
            softplus = jnp.maximum(xg, 0.0) + jnp.log(1.0 + jnp.exp(-jnp.abs(xg)))
            g_all = neg_a * softplus
            beta_all = _sigmoid(ab)
            gc_all = chunk_cumsum(g_all)
            gc_t = gc_all.T
            gl_s[c] = jnp.exp(gc_all[CHUNK - 1:CHUNK, :])
            for h in range(GDN_HEADS):
                n = sub * GDN_HEADS + h
                bufs = [cv_s.at[slot, 3 * n + i] for i in range(3)]
                q = l2n(conv_cols(c, r0, h * GDN_DK, bufs[0], qkv)) * (GDN_DK ** -0.5)
                k = l2n(conv_cols(c, r0, GDN_QK + h * GDN_DK, bufs[1], qkv))
                v = conv_cols(c, r0, 2 * GDN_QK + h * GDN_DV, bufs[2], qkv)
                gcol = gc_all[:, h:h + 1]
                grow = gc_t[h:h + 1, :]
                beta = beta_all[:, GDN_HEADS + h:GDN_HEADS + h + 1]
                dec_s[slot, n] = jnp.where(causal, jnp.exp(gcol - grow), 0.0)
                eg = jnp.exp(gcol)
                kb = k * beta
                q_s[slot, n] = q.astype(BF16)
                k_s[slot, n] = k.astype(BF16)
                kb_s[slot, n] = kb.astype(BF16)
                rhs_s[slot, n] = jnp.concatenate([v * beta, kb * eg], axis=1).astype(BF16)
                qg_o[slot, n] = q * eg
                kd = k * jnp.exp(gc_all[CHUNK - 1:CHUNK, h:h + 1] - gcol)
                kdt_o[slot, n] = kd.T.astype(BF16)

    def solve(cc, slot, between):
        items = [(cc * GDN_PREP_CHUNKS + sub, h, sub * GDN_HEADS + h)
                 for sub in range(GDN_PREP_CHUNKS) for h in range(GDN_HEADS)]
        kks = [_dot_nt(kb_s[slot, n], k_s[slot, n]) for _, _, n in items]
        qks = [_dot_nt(q_s[slot, n], k_s[slot, n]) for _, _, n in items]
        lows = [jnp.where(strict, kk * dec_s[slot, n], 0.0) for kk, (_, _, n) in zip(kks, items)]
        attns = [(qk * dec_s[slot, n]).astype(BF16) for qk, (_, _, n) in zip(qks, items)]
        for step in between[:len(between) // 2]:
            step()
        ts = _unit_lower_inverses(lows, ii, jj)
        for step in between[len(between) // 2:]:
            step()
        uws = [_dot(t.astype(BF16), rhs_s[slot, n]).astype(BF16) for t, (_, _, n) in zip(ts, items)]
        kds = [_dot(kdt_o[slot, n], uw) for uw, (_, _, n) in zip(uws, items)]
        ats = [_dot(attn, uw) for attn, uw in zip(attns, uws)]
        for kd_uw, at_uw, (c, h, n) in zip(kds, ats, items):
            b_s[c, h] = kd_uw[:, :GDN_DV]
            m_s[c, h] = (-kd_uw[:, GDN_DV:]).astype(BF16)
            o0_s[c, h] = at_uw[:, :GDN_DV]
            qp_s[c, h] = (qg_o[slot, n] - at_uw[:, GDN_DV:]).astype(BF16)

    ngroup = nchunk // GDN_PREP_CHUNKS
    per_trip = 2 * GDN_PREP_CHUNKS
    normw = normw_ref[...]
    state_s[...] = jnp.zeros_like(state_s)
    for c0 in range(per_trip):
        for h in range(GDN_HEADS):
            b_s[c0, h] = jnp.zeros((CHUNK, GDN_DV), F32)
            o0_s[c0, h] = jnp.zeros((CHUNK, GDN_DV), F32)
            m_s[c0, h] = jnp.zeros((CHUNK, GDN_DK), BF16)
            qp_s[c0, h] = jnp.zeros((CHUNK, GDN_DK), BF16)

    @pl.when(pl.program_id(0) == 0)
    def _():
        operands(0, 0)

    def prep(i, carry, last=False):
        first = jnp.maximum(per_trip * (i - 1), 0)
        steps = [functools.partial(scan, first + k, 0) for k in range(per_trip)]
        operands(2 * i + 1, 1)
        solve(2 * i, 0, steps[:per_trip // 2])
        if last:
            operands(0, 0, qkv_nx_ref, ab_nx_ref)
        else:
            operands(2 * i + 2, 0)
        solve(2 * i + 1, 1, steps[per_trip // 2:])
        return carry

    def scan(c, carry):
        r0 = pl.multiple_of(c * CHUNK, CHUNK)
        gl = gl_s[c]
        heads = range(GDN_HEADS)
        ss = [state_s[h] for h in heads]
        sbs = [s.astype(BF16) for s in ss]
        mss = [_dot(m_s[c, h], sbs[h]) for h in heads]
        qss = [_dot(qp_s[c, h], sbs[h]) for h in heads]
        for h in heads:
            state_s[h] = ss[h] * gl[:, h:h + 1] + (mss[h] + b_s[c, h])
            o = qss[h] + o0_s[c, h]
            z = z_ref[pl.ds(r0, CHUNK), h * GDN_DV:(h + 1) * GDN_DV].astype(F32)
            on = o * lax.rsqrt(jnp.mean(o * o, axis=-1, keepdims=True) + EPS) * normw
            o_ref[pl.ds(r0, CHUNK), h * GDN_DV:(h + 1) * GDN_DV] = (on * _silu(z)).astype(BF16)
        return carry

    ntrip = ngroup // 2
    lax.fori_loop(0, ntrip - 1, prep, 0)
    prep(ntrip - 1, 0, last=True)
    lax.fori_loop(per_trip * (ntrip - 1), nchunk, scan, 0)


def _gdn_call(proj, ab, conv_w8, alog_p, dtb_p, normw, batch, seq):
    nchunk = seq // CHUNK
    hs = (nchunk, GDN_HEADS, CHUNK, CHUNK)
    ops = (2, GDN_PREP_CHUNKS * GDN_HEADS, CHUNK, CHUNK)
    group = GDN_PREP_CHUNKS * CHUNK
    return pl.pallas_call(
        functools.partial(_gdn_kernel, seq=seq),
        grid=(batch,),
        in_specs=[
            pl.BlockSpec((seq, 3 * GDN_QK), lambda b: (b, 0)),
            pl.BlockSpec((seq, GDN_V), lambda b: (b, 3)),
            pl.BlockSpec((seq, LANES), lambda b: (b, 0)),
            pl.BlockSpec((group, 3 * GDN_QK), lambda b: (jnp.minimum(b + 1, batch - 1) * (seq // group), 0)),
            pl.BlockSpec((group, LANES), lambda b: (jnp.minimum(b + 1, batch - 1) * (seq // group), 0)),
            pl.BlockSpec((8, 3 * GDN_QK), lambda b: (0, 0)),
            pl.BlockSpec((1, LANES), lambda b: (0, 0)),
            pl.BlockSpec((1, LANES), lambda b: (0, 0)),
            pl.BlockSpec((1, GDN_DV), lambda b: (0, 0)),
        ],
        out_specs=pl.BlockSpec((seq, GDN_V), lambda b: (b, 0)),
        out_shape=jax.ShapeDtypeStruct((batch * seq, GDN_V), BF16),
        scratch_shapes=[
            pltpu.VMEM(hs, F32), pltpu.VMEM(hs, F32), pltpu.VMEM(hs, BF16), pltpu.VMEM(hs, BF16),
            pltpu.VMEM((nchunk, 1, LANES), F32),
            pltpu.VMEM((GDN_HEADS, GDN_DK, GDN_DV), F32),
            pltpu.VMEM(ops, BF16), pltpu.VMEM(ops, BF16), pltpu.VMEM(ops, BF16),
            pltpu.VMEM(ops[:3] + (2 * CHUNK,), BF16),
            pltpu.VMEM(ops, F32),
            pltpu.VMEM((2, 3 * ops[1], 8 + CHUNK, LANES), F32),
            pltpu.VMEM(ops, F32), pltpu.VMEM(ops, BF16),
        ],
        compiler_params=pltpu.CompilerParams(
            dimension_semantics=("arbitrary",), vmem_limit_bytes=VMEM_LIMIT),
        name="gdn",
    )(proj, proj, ab, proj, ab, conv_w8, alog_p, dtb_p, normw)


def _ret_kernel(qk_ref, v_ref, g_ref, sin_ref, cos_ref, inner_ref, kdec_ref, qdec_ref, cdec_ref,
                o_ref, state_s, q_s, qd_s, k_s, kt_s, *, seq):
    nchunk = seq // CHUNK
    lane = lax.broadcasted_iota(jnp.int32, (CHUNK, RET_DK), 1)
    even = (lane % 2) == 0

    def rotate(x, sin, cos):
        nxt = pltpu.roll(x, RET_DK - 1, axis=1)
        prv = pltpu.roll(x, 1, axis=1)
        return x * cos + jnp.where(even, -nxt, prv) * sin

    state_s[...] = jnp.zeros_like(state_s)
    kdec = kdec_ref[...]
    qdec = qdec_ref[...]
    cdec = cdec_ref[...]

    heads = range(RET_HEADS)

    def operands(c, slot):
        r0 = pl.multiple_of(c * CHUNK, CHUNK)
        sin = sin_ref[pl.ds(r0, CHUNK), :]
        cos = cos_ref[pl.ds(r0, CHUNK), :]
        for h in heads:
            q = rotate(qk_ref[pl.ds(r0, CHUNK), h * RET_DK:(h + 1) * RET_DK].astype(F32), sin, cos)
            k = rotate(qk_ref[pl.ds(r0, CHUNK), RET_QK + h * RET_DK:RET_QK + (h + 1) * RET_DK]
                       .astype(F32), sin, cos) * (RET_DK ** -0.5)
            q_s[slot, h] = q.astype(BF16)
            qd_s[slot, h] = (q * qdec[:, h:h + 1]).astype(BF16)
            k_s[slot, h] = k.astype(BF16)
            kt_s[slot, h] = (k * kdec[:, h:h + 1]).T.astype(BF16)

    def outputs(c, slot):
        r0 = pl.multiple_of(c * CHUNK, CHUNK)
        vs = [v_ref[pl.ds(r0, CHUNK), h * RET_DV:(h + 1) * RET_DV] for h in heads]
        ss = [state_s[h] for h in heads]
        qks = [_dot_nt(q_s[slot, h], k_s[slot, h]) for h in heads]
        inters = [_dot(qd_s[slot, h], ss[h].astype(BF16)) for h in heads]
        kvs = [_dot(kt_s[slot, h], vs[h]) for h in heads]
        intras = [_dot((qks[h] * inner_ref[h]).astype(BF16), vs[h]) for h in heads]
        for h in heads:
            state_s[h] = ss[h] * cdec[:, h:h + 1] + kvs[h]
            o = intras[h] + inters[h]
            gate = g_ref[pl.ds(r0, CHUNK), h * RET_DV:(h + 1) * RET_DV].astype(F32)
            on = o * lax.rsqrt(jnp.mean(o * o, axis=-1, keepdims=True) + EPS)
            o_ref[pl.ds(r0, CHUNK), h * RET_DV:(h + 1) * RET_DV] = (on * _silu(gate)).astype(BF16)

    operands(0, 0)

    def body(i, carry):
        operands(2 * i + 1, 1)
        outputs(2 * i, 0)
        operands(jnp.minimum(2 * i + 2, nchunk - 1), 0)
        outputs(2 * i + 1, 1)
        return carry

    lax.fori_loop(0, nchunk // 2, body, 0)


def _ret_call(proj, sin, cos, inner, kdec, qdec, cdec, batch, seq):
    return pl.pallas_call(
        functools.partial(_ret_kernel, seq=seq),
        grid=(batch,),
        in_specs=[
            pl.BlockSpec((seq, 2 * RET_QK), lambda b: (b, 2)),
            pl.BlockSpec((seq, RET_V), lambda b: (b, 3)),
            pl.BlockSpec((seq, RET_V), lambda b: (b, 4)),
            pl.BlockSpec((seq, RET_DK), lambda b: (0, 0)),
            pl.BlockSpec((seq, RET_DK), lambda b: (0, 0)),
            pl.BlockSpec((RET_HEADS, CHUNK, CHUNK), lambda b: (0, 0, 0)),
            pl.BlockSpec((CHUNK, LANES), lambda b: (0, 0)),
            pl.BlockSpec((CHUNK, LANES), lambda b: (0, 0)),
            pl.BlockSpec((1, LANES), lambda b: (0, 0)),
        ],
        out_specs=pl.BlockSpec((seq, RET_V), lambda b: (b, 0)),
        out_shape=jax.ShapeDtypeStruct((batch * seq, RET_V), BF16),
        scratch_shapes=[pltpu.VMEM((RET_HEADS, RET_DK, RET_DV), F32)]
        + [pltpu.VMEM((2, RET_HEADS, CHUNK, RET_DK), BF16)] * 4,
        compiler_params=pltpu.CompilerParams(
            dimension_semantics=("arbitrary",), vmem_limit_bytes=VMEM_LIMIT),
        name="retention",
    )(proj, proj, proj, sin, cos, inner, kdec, qdec, cdec)


def _retention_tables(seq):
    inv_freq = 1.0 / (ROPE_BASE ** jnp.linspace(0.0, 1.0, RET_DK // 2, dtype=F32))
    ang = jnp.arange(seq, dtype=F32)[:, None] * inv_freq[None, :]
    sin = jnp.repeat(jnp.sin(ang), 2, axis=-1)
    cos = jnp.repeat(jnp.cos(ang), 2, axis=-1)
    log_gamma = jnp.log(1.0 - 2.0 ** (-5.0 - jnp.arange(RET_HEADS, dtype=F32)))
    idx = jnp.arange(CHUNK, dtype=F32)
    causal = jnp.tril(jnp.ones((CHUNK, CHUNK), dtype=bool))
    rel = jnp.where(causal, idx[:, None] - idx[None, :], 0.0)
    inner = jnp.where(causal, jnp.exp(rel[None] * log_gamma[:, None, None]), 0.0)
    k_decay = jnp.exp(log_gamma[:, None] * (CHUNK - 1.0 - idx)[None, :])
    q_decay = jnp.exp(log_gamma[:, None] * (idx + 1.0)[None, :])
    chunk_decay = jnp.exp(log_gamma * CHUNK)
    pad = LANES - RET_HEADS
    kdec = jnp.pad(k_decay.T, ((0, 0), (0, pad)))
    qdec = jnp.pad(q_decay.T, ((0, 0), (0, pad)))
    cdec = jnp.pad(chunk_decay[None, :], ((0, 0), (0, pad)))
    return sin, cos, inner, kdec, qdec, cdec


def _merge_kernel(x_ref, ya_ref, yb_ref, ma_ref, mb_ref, wa_ref, wr_ref, wo_ref, nw_ref,
                  wrt_ref, br_ref, h_ref, xn_ref, ridx_ref, rw_ref, cnt_ref):
    tm = x_ref.shape[0]
    a = _dot(ya_ref[...], wa_ref[...])
    r = _dot(yb_ref[...], wr_ref[...])
    merged = _sigmoid(ma_ref[...].astype(F32)) * a + _sigmoid(mb_ref[...].astype(F32)) * r
    h = x_ref[...] + _dot(merged.astype(BF16), wo_ref[...])
    h_ref[...] = h
    xn = h * lax.rsqrt(jnp.mean(h * h, axis=-1, keepdims=True) + EPS) * nw_ref[...]
    xn_ref[...] = _pack_bf16_pairs(xn)
    xh = xn.astype(BF16)
    xl = (xn - xh.astype(F32)).astype(BF16)
    parts = _dot(jnp.concatenate([xh, xl], axis=0), wrt_ref[...])
    counts = _route(parts, tm, br_ref[...], ridx_ref, rw_ref)

    @pl.when(pl.program_id(0) == 0)
    def _():
        cnt_ref[...] = jnp.zeros_like(cnt_ref)

    cnt_ref[...] += jnp.broadcast_to(counts, cnt_ref.shape)


def _route(parts, tm, bias, ridx_ref, rw_ref):
    logits = (parts[:tm, :LANES] + (parts[tm:, :LANES] + parts[:tm, LANES:]
                                    + parts[tm:, LANES:])) + bias
    lane = lax.broadcasted_iota(jnp.int32, (tm, LANES), 1)
    neg = -jnp.inf
    gl = jnp.where((lane >= N_EXPERTS) & (lane < N_EXPERTS + N_GROUPS), logits, neg)
    gmax = jnp.max(gl, axis=-1, keepdims=True)
    gidx = jnp.min(jnp.where(gl == gmax, lane, LANES), axis=-1, keepdims=True) - N_EXPERTS
    g_w = 1.0 / jnp.sum(jnp.exp(gl - gmax), axis=-1, keepdims=True)
    el = jnp.where((lane // EXPERTS_PER_GROUP == gidx) & (lane < N_EXPERTS), logits, neg)
    m1 = jnp.max(el, axis=-1, keepdims=True)
    i1 = jnp.min(jnp.where(el == m1, lane, LANES), axis=-1, keepdims=True)
    el2 = jnp.where(lane == i1, neg, el)
    m2 = jnp.max(el2, axis=-1, keepdims=True)
    i2 = jnp.min(jnp.where(el2 == m2, lane, LANES), axis=-1, keepdims=True)
    e2 = jnp.exp(m2 - m1)
    p1 = g_w / (1.0 + e2)
    p2 = g_w * e2 / (1.0 + e2)
    ridx_ref[...] = jnp.where(lane == 0, i1, jnp.where(lane == 1, i2, 0))
    rw_ref[...] = jnp.where(lane == 0, p1, jnp.where(lane == 1, p2, 0.0))
    onehot = jnp.where((lane == i1) | (lane == i2), 1.0, 0.0)
    return jnp.sum(onehot, axis=0, keepdims=True)


def _merge_call(x2, ya, yb, proj, wa, wr, wo, nw, w_router, b_router, tm=1024):
    m = x2.shape[0]
    full = lambda shape: pl.BlockSpec(shape, lambda i: (0, 0))
    return pl.pallas_call(
        _merge_kernel,
        grid=(m // tm,),
        in_specs=[
            pl.BlockSpec((tm, D_MODEL), lambda i: (i, 0)),
            pl.BlockSpec((tm, GDN_V), lambda i: (i, 0)),
            pl.BlockSpec((tm, RET_V), lambda i: (i, 0)),
            pl.BlockSpec((tm, D_MODEL), lambda i: (i, 5)),
            pl.BlockSpec((tm, D_MODEL), lambda i: (i, 6)),
            full((GDN_V, D_MODEL)), full((RET_V, D_MODEL)), full((D_MODEL, D_MODEL)),
            full((1, D_MODEL)),
            full((D_MODEL, 2 * LANES)), full((1, LANES)),
        ],
        out_specs=[
            pl.BlockSpec((tm, D_MODEL), lambda i: (i, 0)),
            pl.BlockSpec((tm, HALF), lambda i: (i, 0)),
            pl.BlockSpec((tm, LANES), lambda i: (i, 0)),
            pl.BlockSpec((tm, LANES), lambda i: (i, 0)),
            pl.BlockSpec((8, LANES), lambda i: (0, 0)),
        ],
        out_shape=[
            jax.ShapeDtypeStruct((m, D_MODEL), F32),
            jax.ShapeDtypeStruct((m, HALF), I32),
            jax.ShapeDtypeStruct((m, LANES), I32),
            jax.ShapeDtypeStruct((m, LANES), F32),
            jax.ShapeDtypeStruct((8, LANES), F32),
        ],
        compiler_params=pltpu.CompilerParams(
            dimension_semantics=("arbitrary",), vmem_limit_bytes=VMEM_LIMIT),
        name="merge_router",
    )(x2, ya, yb, proj, proj, wa, wr, wo, nw, w_router, b_router)


def _slot_counts(m):
    n_slots = TOP_K * m + N_EXPERTS * SLOT_TILE
    return n_slots, n_slots // SLOT_TILE


def _lane_prefix_sum(x, lane):
    s = 1
    while s < LANES:
        x = x + jnp.where(lane >= s, pltpu.roll(x, s, axis=1), 0.0)
        s *= 2
    return x


def _plan_kernel(ridx_ref, cnt_ref, slots_ref, tile_ref, carry_s, off_s):
    i = pl.program_id(0)
    lane = lax.broadcasted_iota(I32, (PLAN_TILE, LANES), 1)
    row = lax.broadcasted_iota(I32, (PLAN_TILE, LANES), 0)
    lane1 = lane[0:1]

    @pl.when(i == 0)
    def _():
        cnt = cnt_ref[0:1, :]
        tile = float(SLOT_TILE)
        padded = jnp.floor((cnt + (tile - 1.0)) / tile) * tile
        incl = _lane_prefix_sum(padded, lane1)
        off = incl - padded
        off_s[...] = off
        carry_s[...] = jnp.zeros_like(carry_s)
        first = (row * SLOT_TILE).astype(F32)
        ended = jnp.where((lane < N_EXPERTS) & (incl <= first), 1.0, 0.0)
        tile_e = jnp.sum(ended, axis=-1, keepdims=True)
        last = jnp.sum(jnp.where(lane.astype(F32) == tile_e, off + cnt, 0.0), axis=-1, keepdims=True)
        used = jnp.clip(last - first[:, 0:1], 0.0, tile)
        tile_ref[...] = jnp.where(lane == 0, tile_e, jnp.where(lane == 1, used, 0.0)).astype(I32)

    strict = jnp.where(row[:, 0:1] > lax.broadcasted_iota(I32, (PLAN_TILE, PLAN_TILE), 1),
                       1.0, 0.0).astype(BF16)
    off = off_s[...]
    carry = carry_s[...]
    for sb in range(PLAN_STEP // PLAN_TILE):
        rows = pl.ds(sb * PLAN_TILE, PLAN_TILE)
        e1 = ridx_ref[rows, 0:1]
        e2 = ridx_ref[rows, 1:2]
        onehot = jnp.where((lane == e1) | (lane == e2), 1.0, 0.0)
        pos = _dot(strict, onehot.astype(BF16)) + (carry + off)
        s1 = jnp.sum(jnp.where(lane == e1, pos, 0.0), axis=-1, keepdims=True)
        s2 = jnp.sum(jnp.where(lane == e2, pos, 0.0), axis=-1, keepdims=True)
        both = jnp.where(lane == 0, s1, jnp.where(lane == 1, s2, 0.0))
        for q in range(PLAN_TILE // LANES):
            t = both[q * LANES:(q + 1) * LANES].T
            c0 = sb * PLAN_TILE + q * LANES
            slots_ref[:, c0:c0 + LANES] = t[0:8].astype(I32)
        carry = carry + jnp.sum(onehot, axis=0, keepdims=True)
    carry_s[...] = carry


def _plan_call(ridx, cnt):
    m = ridx.shape[0]
    _, n_tiles = _slot_counts(m)
    assert n_tiles <= PLAN_TILE
    return pl.pallas_call(
        _plan_kernel,
        grid=(m // PLAN_STEP,),
        in_specs=[pl.BlockSpec((PLAN_STEP, LANES), lambda i: (i, 0)),
                  pl.BlockSpec((8, LANES), lambda i: (0, 0))],
        out_specs=[
            pl.BlockSpec((8, PLAN_STEP), lambda i: (0, i)),
            pl.BlockSpec((PLAN_TILE, LANES), lambda i: (0, 0)),
        ],
        out_shape=[
            jax.ShapeDtypeStruct((8, m), I32),
            jax.ShapeDtypeStruct((PLAN_TILE, LANES), I32),
        ],
        scratch_shapes=[pltpu.VMEM((1, LANES), F32), pltpu.VMEM((1, LANES), F32)],
        compiler_params=pltpu.CompilerParams(dimension_semantics=("arbitrary",)),
        name="dispatch_plan",
    )(ridx, cnt)


def _sc_mesh():
    return plsc.VectorSubcoreMesh(core_axis_name="c", subcore_axis_name="s")


def _sc_worker():
    return lax.axis_index("s") * SC_CORES + lax.axis_index("c")


def _sc_dispatch(xn, slot1, slot2, n_rows):
    m = xn.shape[0]
    per = m // SC_WORKERS
    n_pairs = per // (2 * SC_ROWS)

    @functools.partial(
        pl.kernel, mesh=_sc_mesh(),
        out_type=jax.ShapeDtypeStruct((n_rows, HALF), I32),
        scratch_types=[pltpu.VMEM((per // SC_ROWS, SC_ROWS), I32), pltpu.VMEM((per // SC_ROWS, SC_ROWS), I32),
                       pltpu.VMEM((SC_ROWS, HALF), I32), pltpu.VMEM((SC_ROWS, HALF), I32),
                       pltpu.SemaphoreType.DMA, pltpu.SemaphoreType.DMA, pltpu.SemaphoreType.DMA],
        name="sc_dispatch")
    def k(x_hbm, s1_hbm, s2_hbm, o_hbm, i1_v, i2_v, rows0, rows1, sem_r0, sem_r1, sem_w):
        wid = _sc_worker()
        base = wid * per

        def read(chunk, rows_v, sem):
            return pltpu.make_async_copy(x_hbm.at[pl.ds(base + chunk * SC_ROWS, SC_ROWS)], rows_v, sem)

        def scatter(chunk, rows_v):
            c1 = pltpu.async_copy(rows_v, o_hbm.at[i1_v.at[chunk]], sem_w)
            c2 = pltpu.async_copy(rows_v, o_hbm.at[i2_v.at[chunk]], sem_w)
            c1.wait()
            c2.wait()

        read(0, rows0, sem_r0).start()
        pltpu.sync_copy(s1_hbm.at[pl.ds(wid * (per // SC_ROWS), per // SC_ROWS)], i1_v)
        pltpu.sync_copy(s2_hbm.at[pl.ds(wid * (per // SC_ROWS), per // SC_ROWS)], i2_v)

        @pl.loop(0, n_pairs)
        def _(i):
            read(2 * i, rows0, sem_r0).wait()
            read(2 * i + 1, rows1, sem_r1).start()
            scatter(2 * i, rows0)
            read(2 * i + 1, rows1, sem_r1).wait()

            @pl.when(i + 1 < n_pairs)
            def _():
                read(2 * i + 2, rows0, sem_r0).start()

            scatter(2 * i + 1, rows1)

    return k(xn, slot1, slot2)


def _sc_collect(ys, slot1, slot2):
    m = slot1.size
    per = m // SC_WORKERS
    row = jax.ShapeDtypeStruct((m, HALF), I32)

    @functools.partial(
        pl.kernel, mesh=_sc_mesh(), out_type=[row, row],
        scratch_types=[pltpu.VMEM((per // SC_ROWS, SC_ROWS), I32), pltpu.VMEM((per // SC_ROWS, SC_ROWS), I32),
                       pltpu.VMEM((SC_ROWS, HALF), I32), pltpu.VMEM((SC_ROWS, HALF), I32),
                       pltpu.SemaphoreType.DMA, pltpu.SemaphoreType.DMA],
        name="sc_collect")
    def k(y_hbm, s1_hbm, s2_hbm, g1_hbm, g2_hbm, i1_v, i2_v, rows1, rows2, sem_g, sem_w):
        wid = _sc_worker()
        base = wid * per
        pltpu.sync_copy(s1_hbm.at[pl.ds(wid * (per // SC_ROWS), per // SC_ROWS)], i1_v)
        pltpu.sync_copy(s2_hbm.at[pl.ds(wid * (per // SC_ROWS), per // SC_ROWS)], i2_v)

        @pl.loop(0, per // SC_ROWS)
        def _(ci):
            t0 = base + ci * SC_ROWS
            a1 = pltpu.async_copy(y_hbm.at[i1_v.at[ci]], rows1, sem_g)
            a2 = pltpu.async_copy(y_hbm.at[i2_v.at[ci]], rows2, sem_g)
            a1.wait()
            a2.wait()
            w1 = pltpu.async_copy(rows1, g1_hbm.at[pl.ds(t0, SC_ROWS)], sem_w)
            w2 = pltpu.async_copy(rows2, g2_hbm.at[pl.ds(t0, SC_ROWS)], sem_w)
            w1.wait()
            w2.wait()

    return k(ys, slot1, slot2)


def _expert_kernel(te_ref, used_ref, xs_ref, wg_hbm, wu_hbm, wd_hbm, ys_ref, wg_b, wu_b, wd_b,
                   wg_f, wu_f, wd_f, sem, slot_s):
    j = pl.program_id(0)
    n = pl.num_programs(0)
    e = te_ref[j]
    prev = te_ref[jnp.maximum(j - 1, 0)]
    valid = e < N_EXPERTS

    def weight_copies(expert, slot):
        return [pltpu.make_async_copy(hbm.at[expert], buf.at[slot], sem.at[slot, i])
                for i, (hbm, buf) in enumerate(((wg_hbm, wg_f), (wu_hbm, wu_f), (wd_hbm, wd_f)))]

    def run_tile():
        half = SLOT_TILE // 2
        rows = [pl.ds(i * half, half) for i in range(2)]
        row_id = lax.broadcasted_iota(I32, (half, HALF), 0)
        xs = [_unpack_bf16_pairs(jnp.where(row_id + i * half < used_ref[j], xs_ref[r, :], 0))
              .astype(BF16) for i, r in enumerate(rows)]
        gs = [_dot(x, wg_b[...]) for x in xs]
        us = [_dot(x, wu_b[...]) for x in xs]
        hids = [(_silu(g) * u).astype(BF16) for g, u in zip(gs, us)]
        ys = [_dot(hid, wd_b[...]) for hid in hids]
        for r, y in zip(rows, ys):
            ys_ref[r, :] = _pack_bf16_pairs(y)

    @pl.when((j == 0) & valid)
    def _():
        slot_s[0] = 0
        for c in weight_copies(e, 0):
            c.start()

    first_tile = ((j == 0) | (e != prev)) & valid

    @pl.when(first_tile)
    def _():
        slot = slot_s[0]
        for c in weight_copies(e, slot):
            c.wait()
        k = lax.while_loop(lambda k: (k < n) & (te_ref[jnp.minimum(k, n - 1)] == e),
                           lambda k: k + 1, j + 1)
        nxt = te_ref[jnp.minimum(k, n - 1)]

        @pl.when((k < n) & (nxt < N_EXPERTS))
        def _():
            for c in weight_copies(nxt, 1 - slot):
                c.start()

        wg_b[...] = wg_f[slot].astype(BF16)
        wu_b[...] = wu_f[slot].astype(BF16)
        wd_b[...] = wd_f[slot].astype(BF16)
        slot_s[0] = 1 - slot
        run_tile()

    @pl.when(valid & jnp.logical_not(first_tile))
    def _():
        run_tile()

    @pl.when(e >= N_EXPERTS)
    def _():
        ys_ref[...] = jnp.zeros_like(ys_ref)


def _expert_call(tile_expert, tile_used, xs, wg, wu, wd, n_tiles):
    hbm = pl.BlockSpec(memory_space=pl.ANY)
    return pl.pallas_call(
        _expert_kernel,
        grid_spec=pltpu.PrefetchScalarGridSpec(
            num_scalar_prefetch=2,
            grid=(n_tiles,),
            in_specs=[pl.BlockSpec((SLOT_TILE, HALF), lambda j, te, used: (j, 0)), hbm, hbm, hbm],
            out_specs=pl.BlockSpec((SLOT_TILE, HALF), lambda j, te, used: (j, 0)),
            scratch_shapes=[
                pltpu.VMEM((D_MODEL, D_EXPERT), BF16), pltpu.VMEM((D_MODEL, D_EXPERT), BF16),
                pltpu.VMEM((D_EXPERT, D_MODEL), BF16),
                pltpu.VMEM((2, D_MODEL, D_EXPERT), F32), pltpu.VMEM((2, D_MODEL, D_EXPERT), F32),
                pltpu.VMEM((2, D_EXPERT, D_MODEL), F32),
                pltpu.SemaphoreType.DMA((2, 3)),
                pltpu.SMEM((1,), I32),
            ],
        ),
        out_shape=jax.ShapeDtypeStruct((n_tiles * SLOT_TILE, HALF), I32),
        compiler_params=pltpu.CompilerParams(
            dimension_semantics=("arbitrary",), vmem_limit_bytes=VMEM_LIMIT),
        name="experts",
    )(tile_expert, tile_used, xs, wg, wu, wd)


def _final_kernel(h_ref, g1_ref, g2_ref, rw_ref, nw_ref, o_ref):
    rw = rw_ref[...]
    y = rw[:, 0:1] * _unpack_bf16_pairs(g1_ref[...]) + rw[:, 1:2] * _unpack_bf16_pairs(g2_ref[...])
    h = h_ref[...] + y
    o_ref[...] = h * lax.rsqrt(jnp.mean(h * h, axis=-1, keepdims=True) + EPS) * nw_ref[...]


def _final_call(h1, g1, g2, rw, nw, tm=1024):
    m = h1.shape[0]
    return pl.pallas_call(
        _final_kernel,
        grid=(m // tm,),
        in_specs=[
            pl.BlockSpec((tm, D_MODEL), lambda i: (i, 0)),
            pl.BlockSpec((tm, HALF), lambda i: (i, 0)),
            pl.BlockSpec((tm, HALF), lambda i: (i, 0)),
            pl.BlockSpec((tm, LANES), lambda i: (i, 0)),
            pl.BlockSpec((1, D_MODEL), lambda i: (0, 0)),
        ],
        out_specs=pl.BlockSpec((tm, D_MODEL), lambda i: (i, 0)),
        out_shape=jax.ShapeDtypeStruct((m, D_MODEL), F32),
        compiler_params=pltpu.CompilerParams(dimension_semantics=("arbitrary",)),
        name="combine_final",
    )(h1, g1, g2, rw, nw)


def _pad_lanes(a):
    return jnp.pad(a, ((0, 0), (0, LANES - a.shape[1])))


def kernel(x, norm_mix_w, w_in, conv_w, A_log, dt_bias, gdn_norm_w, w_up_gdn, w_up_ret, w_out,
           norm_ffn_w, w_group, b_group, w_expert, b_expert, w_gate, w_up, w_down, norm_final_w):
    batch, seq, d = x.shape
    m = batch * seq
    h = x.reshape(m, d)
    depth = w_in.shape[0]
    sin, cos, inner, kdec, qdec, cdec = _retention_tables(seq)
    for l in range(depth):
        w_main, w_ab = _repack_call(jnp.transpose(w_in[l]))
        proj, ab = _proj_call(h, norm_mix_w[l][None, :], w_main, w_ab)

        conv8 = jnp.pad(conv_w[l], ((0, 8 - GDN_CONV), (0, 0)))
        ya = _gdn_call(proj, ab, conv8, _pad_lanes(A_log[l][None, :]), _pad_lanes(dt_bias[l][None, :]),
                       gdn_norm_w[l][None, :], batch, seq)
        yb = _ret_call(proj, sin, cos, inner, kdec, qdec, cdec, batch, seq)

        w_router = _pad_lanes(jnp.concatenate([w_expert[l], w_group[l]], axis=1))
        wr_hi = w_router.astype(BF16)
        wr_lo = (w_router - wr_hi.astype(F32)).astype(BF16)
        b_router = _pad_lanes(jnp.concatenate([b_expert[l], b_group[l]])[None, :])
        h1, xn, ridx, rw, cnt = _merge_call(
            h, ya, yb, proj, w_up_gdn[l].astype(BF16), w_up_ret[l].astype(BF16), w_out[l].astype(BF16),
            norm_ffn_w[l][None, :], jnp.concatenate([wr_hi, wr_lo], axis=1), b_router)

        n_slots, n_tiles = _slot_counts(m)
        slots, tiles = _plan_call(ridx, cnt)
        slot1, slot2 = slots[0].reshape(-1, SC_ROWS), slots[1].reshape(-1, SC_ROWS)
        xs = _sc_dispatch(xn, slot1, slot2, n_slots)
        ys = _expert_call(tiles[:n_tiles, 0], tiles[:n_tiles, 1], xs,
                          w_gate[l], w_up[l], w_down[l], n_tiles)
        g1, g2 = _sc_collect(ys, slot1, slot2)

        assert depth == 1
        h = _final_call(h1, g1, g2, rw, norm_final_w[None, :])
    return h.reshape(batch, seq, d)
```

```python
import functools

import jax
import jax.numpy as jnp
from jax import lax
from jax.experimental import pallas as pl
from jax.experimental.pallas import tpu as pltpu
from jax.experimental.pallas import tpu_sc as plsc

F32 = jnp.float32
BF16 = jnp.bfloat16
I32 = jnp.int32
U32 = jnp.uint32

D_MODEL = 1024
EPS = 1e-6
GDN_HEADS = 4
GDN_DK = 128
GDN_DV = 128
GDN_CONV = 4
RET_HEADS = 4
RET_DK = 128
RET_DV = 256
ROPE_BASE = 10000.0
N_GROUPS = 4
EXPERTS_PER_GROUP = 8
N_EXPERTS = N_GROUPS * EXPERTS_PER_GROUP
D_EXPERT = 512

GDN_QK = GDN_HEADS * GDN_DK
GDN_V = GDN_HEADS * GDN_DV
RET_QK = RET_HEADS * RET_DK
RET_V = RET_HEADS * RET_DV

LANES = 128
CHUNK = 128
INV_BLOCK = 16
GDN_PREP_CHUNKS = 2
VMEM_LIMIT = 56 * 1024 * 1024

REPACK_COLS = 512
TOP_K = 2
SLOT_TILE = 512
PLAN_TILE = 256
PLAN_STEP = 1024
HALF = D_MODEL // 2
SC_CORES = 2
SC_WORKERS = SC_CORES * 16
SC_ROWS = 64

PROJ_COLS = 3 * GDN_QK + GDN_V + 2 * RET_QK + 2 * RET_V + 2 * D_MODEL


def _silu(x):
    return x / (1.0 + jnp.exp(-x))


def _sigmoid(x):
    return 1.0 / (1.0 + jnp.exp(-x))


def _dot(a, b):
    return jnp.dot(a, b, preferred_element_type=F32)


def _dot_nt(a, b):
    return lax.dot_general(a, b, (((1,), (1,)), ((), ())), preferred_element_type=F32)


def _pack_bf16_pairs(x):
    bits = lax.bitcast_convert_type(x.astype(BF16).astype(F32), U32)
    packed = (bits[:, :HALF] >> 16) | (bits[:, HALF:] & jnp.uint32(0xFFFF0000))
    return lax.bitcast_convert_type(packed, I32)


def _unpack_bf16_pairs(p):
    p = lax.bitcast_convert_type(p, U32)
    lo = lax.bitcast_convert_type(p << 16, F32)
    hi = lax.bitcast_convert_type(p & jnp.uint32(0xFFFF0000), F32)
    return jnp.concatenate([lo, hi], axis=1)


def _proj_kernel(x_ref, nw_ref, w_ref, wab_ref, proj_ref, ab_ref, u_ref):
    j = pl.program_id(1)

    def project():
        proj_ref[...] = _dot(u_ref[...], w_ref[...]).astype(BF16)

    @pl.when(j == 0)
    def _():
        x = x_ref[...]
        u = x * lax.rsqrt(jnp.mean(x * x, axis=-1, keepdims=True) + EPS) * nw_ref[...]
        ub = u.astype(BF16)
        u_ref[...] = ub
        ab_ref[...] = _dot(ub, wab_ref[...])
        project()

    @pl.when(j != 0)
    def _():
        project()


def _repack_kernel(wt_ref, abt_ref, main_ref, ab_ref):
    main_ref[...] = wt_ref[...].T.astype(BF16)

    @pl.when(pl.program_id(0) == 0)
    def _():
        ab = abt_ref[...].T
        ab_ref[...] = jnp.concatenate(
            [ab, jnp.zeros((ab.shape[0], LANES - ab.shape[1]), F32)], axis=1).astype(BF16)


def _repack_call(w_in_t):
    d_in, d = w_in_t.shape
    o_ab = 3 * GDN_QK
    n_ab = 2 * GDN_HEADS
    src = lambda r: pl.multiple_of(r * REPACK_COLS + jnp.where(r * REPACK_COLS >= o_ab, n_ab, 0), 8)
    return pl.pallas_call(
        _repack_kernel,
        grid=(PROJ_COLS // REPACK_COLS,),
        in_specs=[pl.BlockSpec((pl.Element(REPACK_COLS), pl.Element(d)), lambda r: (src(r), 0)),
                  pl.BlockSpec((pl.Element(n_ab), pl.Element(d)), lambda r: (o_ab, 0))],
        out_specs=[pl.BlockSpec((d, REPACK_COLS), lambda r: (0, r)),
                   pl.BlockSpec((d, LANES), lambda r: (0, 0))],
        out_shape=[jax.ShapeDtypeStruct((d, PROJ_COLS), BF16), jax.ShapeDtypeStruct((d, LANES), BF16)],
        compiler_params=pltpu.CompilerParams(dimension_semantics=("arbitrary",)),
        name="repack_w_in",
    )(w_in_t, w_in_t)


def _proj_call(x2, norm_w, w_main, w_ab, tm=1024, tn=3584):
    m = x2.shape[0]
    return pl.pallas_call(
        _proj_kernel,
        grid=(m // tm, PROJ_COLS // tn),
        in_specs=[
            pl.BlockSpec((tm, D_MODEL), lambda i, j: (i, 0)),
            pl.BlockSpec((1, D_MODEL), lambda i, j: (0, 0)),
            pl.BlockSpec((D_MODEL, tn), lambda i, j: (0, j)),
            pl.BlockSpec((D_MODEL, LANES), lambda i, j: (0, 0)),
        ],
        out_specs=[
            pl.BlockSpec((tm, tn), lambda i, j: (i, j)),
            pl.BlockSpec((tm, LANES), lambda i, j: (i, 0)),
        ],
        out_shape=[
            jax.ShapeDtypeStruct((m, PROJ_COLS), BF16),
            jax.ShapeDtypeStruct((m, LANES), F32),
        ],
        scratch_shapes=[pltpu.VMEM((tm, D_MODEL), BF16)],
        compiler_params=pltpu.CompilerParams(
            dimension_semantics=("arbitrary", "arbitrary"), vmem_limit_bytes=VMEM_LIMIT),
        name="proj",
    )(x2, norm_w, w_main, w_ab)


def _unit_lower_inverses(lows, ii, jj):
    eye = jnp.where(ii == jj, 1.0, 0.0).astype(F32)
    in_block = (ii // INV_BLOCK) == (jj // INV_BLOCK)
    ps = [jnp.where(in_block, -low, 0.0) for low in lows]
    ts = [eye + p for p in ps]
    span = 2
    while span < INV_BLOCK:
        ps = [_dot(p, p) for p in ps]
        ts = [t + _dot(t, p) for t, p in zip(ts, ps)]
        span *= 2
    s = INV_BLOCK
    while s < CHUNK:
        off_diag = ((ii // (2 * s)) == (jj // (2 * s))) & ((ii // s) != (jj // s))
        xs = [_dot(jnp.where(off_diag, low, 0.0), t) for low, t in zip(lows, ts)]
        ts = [t - _dot(t, x) for t, x in zip(ts, xs)]
        s *= 2
    return ts


def _gdn_kernel(qkv_ref, z_ref, ab_ref, qkv_nx_ref, ab_nx_ref, convw_ref, alog_ref, dtb_ref, normw_ref, o_ref,
                b_s, o0_s, m_s, qp_s, gl_s, state_s, q_s, k_s, kb_s, rhs_s, dec_s, cv_s, qg_o, kdt_o,
                *, seq):
    nchunk = seq // CHUNK
    ii = lax.broadcasted_iota(jnp.int32, (CHUNK, CHUNK), 0)
    jj = lax.broadcasted_iota(jnp.int32, (CHUNK, CHUNK), 1)
    causal = ii >= jj
    strict = ii > jj
    tri = jnp.where(causal, 1.0, 0.0).astype(F32)
    neg_a = -jnp.exp(alog_ref[...])
    dtb = dtb_ref[...]

    def conv_cols(c, r0, lo, buf, qkv):
        x = qkv[pl.ds(r0, CHUNK), lo:lo + LANES].astype(F32)
        prev0 = pl.multiple_of(jnp.maximum(r0 - 16, 0), 16)
        prev = qkv[pl.ds(prev0, 16), lo:lo + LANES].astype(F32)
        buf[0:8, :] = prev[8:16] * jnp.where(c > 0, 1.0, 0.0)
        buf[8:8 + CHUNK, :] = x
        w = convw_ref[:, lo:lo + LANES]
        y = (w[3:4] * x + w[2:3] * buf[7:7 + CHUNK, :] + w[1:2] * buf[6:6 + CHUNK, :]
             + w[0:1] * buf[5:5 + CHUNK, :])
        return _silu(y)

    def l2n(x):
        return x * lax.rsqrt(jnp.sum(x * x, axis=-1, keepdims=True) + EPS)

    tri_b = tri.astype(BF16)

    def chunk_cumsum(g):
        g1 = g.astype(BF16)
        r1 = g - g1.astype(F32)
        g2 = r1.astype(BF16)
        g3 = (r1 - g2.astype(F32)).astype(BF16)
        return _dot(tri_b, g1) + (_dot(tri_b, g2) + _dot(tri_b, g3))

    def operands(cc, slot, qkv=qkv_ref, ab_src=ab_ref):
        for sub in range(GDN_PREP_CHUNKS):
            c = cc * GDN_PREP_CHUNKS + sub
            r0 = pl.multiple_of(c * CHUNK, CHUNK)
            ab = ab_src[pl.ds(r0, CHUNK), :]
            xg = ab + dtb
            softplus = jnp.maximum(xg, 0.0) + jnp.log(1.0 + jnp.exp(-jnp.abs(xg)))
            g_all = neg_a * softplus
            beta_all = _sigmoid(ab)
            gc_all = chunk_cumsum(g_all)
            gc_t = gc_all.T
            gl_s[c] = jnp.exp(gc_all[CHUNK - 1:CHUNK, :])
            for h in range(GDN_HEADS):
                n = sub * GDN_HEADS + h
                bufs = [cv_s.at[slot, 3 * n + i] for i in range(3)]
                q = l2n(conv_cols(c, r0, h * GDN_DK, bufs[0], qkv)) * (GDN_DK ** -0.5)
                k = l2n(conv_cols(c, r0, GDN_QK + h * GDN_DK, bufs[1], qkv))
                v = conv_cols(c, r0, 2 * GDN_QK + h * GDN_DV, bufs[2], qkv)
                gcol = gc_all[:, h:h + 1]
                grow = gc_t[h:h + 1, :]
                beta = beta_all[:, GDN_HEADS + h:GDN_HEADS + h + 1]
                dec_s[slot, n] = jnp.where(causal, jnp.exp(gcol - grow), 0.0)
                eg = jnp.exp(gcol)
                kb = k * beta
                q_s[slot, n] = q.astype(BF16)
                k_s[slot, n] = k.astype(BF16)
                kb_s[slot, n] = kb.astype(BF16)
                rhs_s[slot, n] = jnp.concatenate([v * beta, kb * eg], axis=1).astype(BF16)
                qg_o[slot, n] = q * eg
                kd = k * jnp.exp(gc_all[CHUNK - 1:CHUNK, h:h + 1] - gcol)
                kdt_o[slot, n] = kd.T.astype(BF16)

    def solve(cc, slot, between):
        items = [(cc * GDN_PREP_CHUNKS + sub, h, sub * GDN_HEADS + h)
                 for sub in range(GDN_PREP_CHUNKS) for h in range(GDN_HEADS)]
        kks = [_dot_nt(kb_s[slot, n], k_s[slot, n]) for _, _, n in items]
        qks = [_dot_nt(q_s[slot, n], k_s[slot, n]) for _, _, n in items]
        lows = [jnp.where(strict, kk * dec_s[slot, n], 0.0) for kk, (_, _, n) in zip(kks, items)]
        attns = [(qk * dec_s[slot, n]).astype(BF16) for qk, (_, _, n) in zip(qks, items)]
        for step in between[:len(between) // 2]:
            step()
        ts = _unit_lower_inverses(lows, ii, jj)
        for step in between[len(between) // 2:]:
            step()
        uws = [_dot(t.astype(BF16), rhs_s[slot, n]).astype(BF16) for t, (_, _, n) in zip(ts, items)]
        kds = [_dot(kdt_o[slot, n], uw) for uw, (_, _, n) in zip(uws, items)]
        ats = [_dot(attn, uw) for attn, uw in zip(attns, uws)]
        for kd_uw, at_uw, (c, h, n) in zip(kds, ats, items):
            b_s[c, h] = kd_uw[:, :GDN_DV]
            m_s[c, h] = (-kd_uw[:, GDN_DV:]).astype(BF16)
            o0_s[c, h] = at_uw[:, :GDN_DV]
            qp_s[c, h] = (qg_o[slot, n] - at_uw[:, GDN_DV:]).astype(BF16)

    ngroup = nchunk // GDN_PREP_CHUNKS
    per_trip = 2 * GDN_PREP_CHUNKS
    normw = normw_ref[...]
    state_s[...] = jnp.zeros_like(state_s)
    for c0 in range(per_trip):
        for h in range(GDN_HEADS):
            b_s[c0, h] = jnp.zeros((CHUNK, GDN_DV), F32)
            o0_s[c0, h] = jnp.zeros((CHUNK, GDN_DV), F32)
            m_s[c0, h] = jnp.zeros((CHUNK, GDN_DK), BF16)
            qp_s[c0, h] = jnp.zeros((CHUNK, GDN_DK), BF16)

    @pl.when(pl.program_id(0) == 0)
    def _():
        operands(0, 0)

    def prep(i, carry, last=False):
        first = jnp.maximum(per_trip * (i - 1), 0)
        steps = [functools.partial(scan, first + k, 0) for k in range(per_trip)]
        operands(2 * i + 1, 1)
        solve(2 * i, 0, steps[:per_trip // 2])
        if last:
            operands(0, 0, qkv_nx_ref, ab_nx_ref)
        else:
            operands(2 * i + 2, 0)
        solve(2 * i + 1, 1, steps[per_trip // 2:])
        return carry

    def scan(c, carry):
        r0 = pl.multiple_of(c * CHUNK, CHUNK)
        gl = gl_s[c]
        heads = range(GDN_HEADS)
        ss = [state_s[h] for h in heads]
        sbs = [s.astype(BF16) for s in ss]
        mss = [_dot(m_s[c, h], sbs[h]) for h in heads]
        qss = [_dot(qp_s[c, h], sbs[h]) for h in heads]
        for h in heads:
            state_s[h] = ss[h] * gl[:, h:h + 1] + (mss[h] + b_s[c, h])
            o = qss[h] + o0_s[c, h]
            z = z_ref[pl.ds(r0, CHUNK), h * GDN_DV:(h + 1) * GDN_DV].astype(F32)
            on = o * lax.rsqrt(jnp.mean(o * o, axis=-1, keepdims=True) + EPS) * normw
            o_ref[pl.ds(r0, CHUNK), h * GDN_DV:(h + 1) * GDN_DV] = (on * _silu(z)).astype(BF16)
        return carry

    ntrip = ngroup // 2
    lax.fori_loop(0, ntrip - 1, prep, 0)
    prep(ntrip - 1, 0, last=True)
    lax.fori_loop(per_trip * (ntrip - 1), nchunk, scan, 0)


def _gdn_call(proj, ab, conv_w8, alog_p, dtb_p, normw, batch, seq):
    nchunk = seq // CHUNK
    hs = (nchunk, GDN_HEADS, CHUNK, CHUNK)
    ops = (2, GDN_PREP_CHUNKS * GDN_HEADS, CHUNK, CHUNK)
    group = GDN_PREP_CHUNKS * CHUNK
    return pl.pallas_call(
        functools.partial(_gdn_kernel, seq=seq),
        grid=(batch,),
        in_specs=[
            pl.BlockSpec((seq, 3 * GDN_QK), lambda b: (b, 0)),
            pl.BlockSpec((seq, GDN_V), lambda b: (b, 3)),
            pl.BlockSpec((seq, LANES), lambda b: (b, 0)),
            pl.BlockSpec((group, 3 * GDN_QK), lambda b: (jnp.minimum(b + 1, batch - 1) * (seq // group), 0)),
            pl.BlockSpec((group, LANES), lambda b: (jnp.minimum(b + 1, batch - 1) * (seq // group), 0)),
            pl.BlockSpec((8, 3 * GDN_QK), lambda b: (0, 0)),
            pl.BlockSpec((1, LANES), lambda b: (0, 0)),
            pl.BlockSpec((1, LANES), lambda b: (0, 0)),
            pl.BlockSpec((1, GDN_DV), lambda b: (0, 0)),
        ],
        out_specs=pl.BlockSpec((seq, GDN_V), lambda b: (b, 0)),
        out_shape=jax.ShapeDtypeStruct((batch * seq, GDN_V), BF16),
        scratch_shapes=[
            pltpu.VMEM(hs, F32), pltpu.VMEM(hs, F32), pltpu.VMEM(hs, BF16), pltpu.VMEM(hs, BF16),
            pltpu.VMEM((nchunk, 1, LANES), F32),
            pltpu.VMEM((GDN_HEADS, GDN_DK, GDN_DV), F32),
            pltpu.VMEM(ops, BF16), pltpu.VMEM(ops, BF16), pltpu.VMEM(ops, BF16),
            pltpu.VMEM(ops[:3] + (2 * CHUNK,), BF16),
            pltpu.VMEM(ops, F32),
            pltpu.VMEM((2, 3 * ops[1], 8 + CHUNK, LANES), F32),
            pltpu.VMEM(ops, F32), pltpu.VMEM(ops, BF16),
        ],
        compiler_params=pltpu.CompilerParams(
            dimension_semantics=("arbitrary",), vmem_limit_bytes=VMEM_LIMIT),
        name="gdn",
    )(proj, proj, ab, proj, ab, conv_w8, alog_p, dtb_p, normw)


def _ret_kernel(qk_ref, v_ref, g_ref, qk_nx_ref, sin_ref, cos_ref, inner_ref, kdec_ref, qdec_ref, cdec_ref,
                o_ref, state_s, q_s, qd_s, k_s, kt_s, *, seq):
    nchunk = seq // CHUNK
    lane = lax.broadcasted_iota(jnp.int32, (CHUNK, RET_DK), 1)
    even = (lane % 2) == 0

    def rotate(x, sin, cos):
        nxt = pltpu.roll(x, RET_DK - 1, axis=1)
        prv = pltpu.roll(x, 1, axis=1)
        return x * cos + jnp.where(even, -nxt, prv) * sin

    state_s[...] = jnp.zeros_like(state_s)
    kdec = kdec_ref[...]
    qdec = qdec_ref[...]
    cdec = cdec_ref[...]

    heads = range(RET_HEADS)

    def operands(c, slot, qk=qk_ref):
        r0 = pl.multiple_of(c * CHUNK, CHUNK)
        sin = sin_ref[pl.ds(r0, CHUNK), :]
        cos = cos_ref[pl.ds(r0, CHUNK), :]
        for h in heads:
            q = rotate(qk[pl.ds(r0, CHUNK), h * RET_DK:(h + 1) * RET_DK].astype(F32), sin, cos)
            k = rotate(qk[pl.ds(r0, CHUNK), RET_QK + h * RET_DK:RET_QK + (h + 1) * RET_DK]
                       .astype(F32), sin, cos) * (RET_DK ** -0.5)
            q_s[slot, h] = q.astype(BF16)
            qd_s[slot, h] = (q * qdec[:, h:h + 1]).astype(BF16)
            k_s[slot, h] = k.astype(BF16)
            kt_s[slot, h] = (k * kdec[:, h:h + 1]).T.astype(BF16)

    def outputs(c, slot):
        r0 = pl.multiple_of(c * CHUNK, CHUNK)
        vs = [v_ref[pl.ds(r0, CHUNK), h * RET_DV:(h + 1) * RET_DV] for h in heads]
        ss = [state_s[h] for h in heads]
        qks = [_dot_nt(q_s[slot, h], k_s[slot, h]) for h in heads]
        inters = [_dot(qd_s[slot, h], ss[h].astype(BF16)) for h in heads]
        kvs = [_dot(kt_s[slot, h], vs[h]) for h in heads]
        intras = [_dot((qks[h] * inner_ref[h]).astype(BF16), vs[h]) for h in heads]
        for h in heads:
            state_s[h] = ss[h] * cdec[:, h:h + 1] + kvs[h]
            o = intras[h] + inters[h]
            gate = g_ref[pl.ds(r0, CHUNK), h * RET_DV:(h + 1) * RET_DV].astype(F32)
            on = o * lax.rsqrt(jnp.mean(o * o, axis=-1, keepdims=True) + EPS)
            o_ref[pl.ds(r0, CHUNK), h * RET_DV:(h + 1) * RET_DV] = (on * _silu(gate)).astype(BF16)

    @pl.when(pl.program_id(0) == 0)
    def _():
        operands(0, 0)

    def body(i, carry, last=False):
        operands(2 * i + 1, 1)
        outputs(2 * i, 0)
        if last:
            operands(0, 0, qk_nx_ref)
        else:
            operands(2 * i + 2, 0)
        outputs(2 * i + 1, 1)
        return carry

    lax.fori_loop(0, nchunk // 2 - 1, body, 0)
    body(nchunk // 2 - 1, 0, last=True)


def _ret_call(proj, sin, cos, inner, kdec, qdec, cdec, batch, seq):
    return pl.pallas_call(
        functools.partial(_ret_kernel, seq=seq),
        grid=(batch,),
        in_specs=[
            pl.BlockSpec((seq, 2 * RET_QK), lambda b: (b, 2)),
            pl.BlockSpec((seq, RET_V), lambda b: (b, 3)),
            pl.BlockSpec((seq, RET_V), lambda b: (b, 4)),
            pl.BlockSpec((CHUNK, 2 * RET_QK), lambda b: (jnp.minimum(b + 1, batch - 1) * (seq // CHUNK), 2)),
            pl.BlockSpec((seq, RET_DK), lambda b: (0, 0)),
            pl.BlockSpec((seq, RET_DK), lambda b: (0, 0)),
            pl.BlockSpec((RET_HEADS, CHUNK, CHUNK), lambda b: (0, 0, 0)),
            pl.BlockSpec((CHUNK, LANES), lambda b: (0, 0)),
            pl.BlockSpec((CHUNK, LANES), lambda b: (0, 0)),
            pl.BlockSpec((1, LANES), lambda b: (0, 0)),
        ],
        out_specs=pl.BlockSpec((seq, RET_V), lambda b: (b, 0)),
        out_shape=jax.ShapeDtypeStruct((batch * seq, RET_V), BF16),
        scratch_shapes=[pltpu.VMEM((RET_HEADS, RET_DK, RET_DV), F32)]
        + [pltpu.VMEM((2, RET_HEADS, CHUNK, RET_DK), BF16)] * 4,
        compiler_params=pltpu.CompilerParams(
            dimension_semantics=("arbitrary",), vmem_limit_bytes=VMEM_LIMIT),
        name="retention",
    )(proj, proj, proj, proj, sin, cos, inner, kdec, qdec, cdec)


def _retention_tables(seq):
    inv_freq = 1.0 / (ROPE_BASE ** jnp.linspace(0.0, 1.0, RET_DK // 2, dtype=F32))
    ang = jnp.arange(seq, dtype=F32)[:, None] * inv_freq[None, :]
    sin = jnp.repeat(jnp.sin(ang), 2, axis=-1)
    cos = jnp.repeat(jnp.cos(ang), 2, axis=-1)
    log_gamma = jnp.log(1.0 - 2.0 ** (-5.0 - jnp.arange(RET_HEADS, dtype=F32)))
    idx = jnp.arange(CHUNK, dtype=F32)
    causal = jnp.tril(jnp.ones((CHUNK, CHUNK), dtype=bool))
    rel = jnp.where(causal, idx[:, None] - idx[None, :], 0.0)
    inner = jnp.where(causal, jnp.exp(rel[None] * log_gamma[:, None, None]), 0.0)
    k_decay = jnp.exp(log_gamma[:, None] * (CHUNK - 1.0 - idx)[None, :])
    q_decay = jnp.exp(log_gamma[:, None] * (idx + 1.0)[None, :])
    chunk_decay = jnp.exp(log_gamma * CHUNK)
    pad = LANES - RET_HEADS
    kdec = jnp.pad(k_decay.T, ((0, 0), (0, pad)))
    qdec = jnp.pad(q_decay.T, ((0, 0), (0, pad)))
    cdec = jnp.pad(chunk_decay[None, :], ((0, 0), (0, pad)))
    return sin, cos, inner, kdec, qdec, cdec


def _merge_kernel(x_ref, ya_ref, yb_ref, ma_ref, mb_ref, wa_ref, wr_ref, wo_ref, nw_ref,
                  wrt_ref, br_ref, h_ref, xn_ref, ridx_ref, rw_ref, cnt_ref):
    tm = x_ref.shape[0]
    a = _dot(ya_ref[...], wa_ref[...])
    r = _dot(yb_ref[...], wr_ref[...])
    merged = _sigmoid(ma_ref[...].astype(F32)) * a + _sigmoid(mb_ref[...].astype(F32)) * r
    h = x_ref[...] + _dot(merged.astype(BF16), wo_ref[...])
    h_ref[...] = h
    xn = h * lax.rsqrt(jnp.mean(h * h, axis=-1, keepdims=True) + EPS) * nw_ref[...]
    xn_ref[...] = _pack_bf16_pairs(xn)
    xh = xn.astype(BF16)
    xl = (xn - xh.astype(F32)).astype(BF16)
    parts = _dot(jnp.concatenate([xh, xl], axis=0), wrt_ref[...])
    counts = _route(parts, tm, br_ref[...], ridx_ref, rw_ref)

    @pl.when(pl.program_id(0) == 0)
    def _():
        cnt_ref[...] = jnp.zeros_like(cnt_ref)

    cnt_ref[...] += jnp.broadcast_to(counts, cnt_ref.shape)


def _route(parts, tm, bias, ridx_ref, rw_ref):
    logits = (parts[:tm, :LANES] + (parts[tm:, :LANES] + parts[:tm, LANES:]
                                    + parts[tm:, LANES:])) + bias
    lane = lax.broadcasted_iota(jnp.int32, (tm, LANES), 1)
    neg = -jnp.inf
    gl = jnp.where((lane >= N_EXPERTS) & (lane < N_EXPERTS + N_GROUPS), logits, neg)
    gmax = jnp.max(gl, axis=-1, keepdims=True)
    gidx = jnp.min(jnp.where(gl == gmax, lane, LANES), axis=-1, keepdims=True) - N_EXPERTS
    g_w = 1.0 / jnp.sum(jnp.exp(gl - gmax), axis=-1, keepdims=True)
    el = jnp.where((lane // EXPERTS_PER_GROUP == gidx) & (lane < N_EXPERTS), logits, neg)
    m1 = jnp.max(el, axis=-1, keepdims=True)
    i1 = jnp.min(jnp.where(el == m1, lane, LANES), axis=-1, keepdims=True)
    el2 = jnp.where(lane == i1, neg, el)
    m2 = jnp.max(el2, axis=-1, keepdims=True)
    i2 = jnp.min(jnp.where(el2 == m2, lane, LANES), axis=-1, keepdims=True)
    e2 = jnp.exp(m2 - m1)
    p1 = g_w / (1.0 + e2)
    p2 = g_w * e2 / (1.0 + e2)
    ridx_ref[...] = jnp.where(lane == 0, i1, jnp.where(lane == 1, i2, 0))
    rw_ref[...] = jnp.where(lane == 0, p1, jnp.where(lane == 1, p2, 0.0))
    onehot = jnp.where((lane == i1) | (lane == i2), 1.0, 0.0)
    return jnp.sum(onehot, axis=0, keepdims=True)


def _merge_call(x2, ya, yb, proj, wa, wr, wo, nw, w_router, b_router, tm=1024):
    m = x2.shape[0]
    full = lambda shape: pl.BlockSpec(shape, lambda i: (0, 0))
    return pl.pallas_call(
        _merge_kernel,
        grid=(m // tm,),
        in_specs=[
            pl.BlockSpec((tm, D_MODEL), lambda i: (i, 0)),
            pl.BlockSpec((tm, GDN_V), lambda i: (i, 0)),
            pl.BlockSpec((tm, RET_V), lambda i: (i, 0)),
            pl.BlockSpec((tm, D_MODEL), lambda i: (i, 5)),
            pl.BlockSpec((tm, D_MODEL), lambda i: (i, 6)),
            full((GDN_V, D_MODEL)), full((RET_V, D_MODEL)), full((D_MODEL, D_MODEL)),
            full((1, D_MODEL)),
            full((D_MODEL, 2 * LANES)), full((1, LANES)),
        ],
        out_specs=[
            pl.BlockSpec((tm, D_MODEL), lambda i: (i, 0)),
            pl.BlockSpec((tm, HALF), lambda i: (i, 0)),
            pl.BlockSpec((tm, LANES), lambda i: (i, 0)),
            pl.BlockSpec((tm, LANES), lambda i: (i, 0)),
            pl.BlockSpec((8, LANES), lambda i: (0, 0)),
        ],
        out_shape=[
            jax.ShapeDtypeStruct((m, D_MODEL), F32),
            jax.ShapeDtypeStruct((m, HALF), I32),
            jax.ShapeDtypeStruct((m, LANES), I32),
            jax.ShapeDtypeStruct((m, LANES), F32),
            jax.ShapeDtypeStruct((8, LANES), F32),
        ],
        compiler_params=pltpu.CompilerParams(
            dimension_semantics=("arbitrary",), vmem_limit_bytes=VMEM_LIMIT),
        name="merge_router",
    )(x2, ya, yb, proj, proj, wa, wr, wo, nw, w_router, b_router)


def _slot_counts(m):
    n_slots = TOP_K * m + N_EXPERTS * SLOT_TILE
    return n_slots, n_slots // SLOT_TILE


def _lane_prefix_sum(x, lane):
    s = 1
    while s < LANES:
        x = x + jnp.where(lane >= s, pltpu.roll(x, s, axis=1), 0.0)
        s *= 2
    return x


def _plan_kernel(ridx_ref, cnt_ref, slots_ref, tile_ref, carry_s, off_s):
    i = pl.program_id(0)
    lane = lax.broadcasted_iota(I32, (PLAN_TILE, LANES), 1)
    row = lax.broadcasted_iota(I32, (PLAN_TILE, LANES), 0)
    lane1 = lane[0:1]

    @pl.when(i == 0)
    def _():
        cnt = cnt_ref[0:1, :]
        tile = float(SLOT_TILE)
        padded = jnp.floor((cnt + (tile - 1.0)) / tile) * tile
        incl = _lane_prefix_sum(padded, lane1)
        off = incl - padded
        off_s[...] = off
        carry_s[...] = jnp.zeros_like(carry_s)
        first = (row * SLOT_TILE).astype(F32)
        ended = jnp.where((lane < N_EXPERTS) & (incl <= first), 1.0, 0.0)
        tile_e = jnp.sum(ended, axis=-1, keepdims=True)
        last = jnp.sum(jnp.where(lane.astype(F32) == tile_e, off + cnt, 0.0), axis=-1, keepdims=True)
        used = jnp.clip(last - first[:, 0:1], 0.0, tile)
        tile_ref[...] = jnp.where(lane == 0, tile_e, jnp.where(lane == 1, used, 0.0)).astype(I32)

    strict = jnp.where(row[:, 0:1] > lax.broadcasted_iota(I32, (PLAN_TILE, PLAN_TILE), 1),
                       1.0, 0.0).astype(BF16)
    off = off_s[...]
    carry = carry_s[...]
    for sb in range(PLAN_STEP // PLAN_TILE):
        rows = pl.ds(sb * PLAN_TILE, PLAN_TILE)
        e1 = ridx_ref[rows, 0:1]
        e2 = ridx_ref[rows, 1:2]
        onehot = jnp.where((lane == e1) | (lane == e2), 1.0, 0.0)
        pos = _dot(strict, onehot.astype(BF16)) + (carry + off)
        s1 = jnp.sum(jnp.where(lane == e1, pos, 0.0), axis=-1, keepdims=True)
        s2 = jnp.sum(jnp.where(lane == e2, pos, 0.0), axis=-1, keepdims=True)
        both = jnp.where(lane == 0, s1, jnp.where(lane == 1, s2, 0.0))
        for q in range(PLAN_TILE // LANES):
            t = both[q * LANES:(q + 1) * LANES].T
            c0 = sb * PLAN_TILE + q * LANES
            slots_ref[:, c0:c0 + LANES] = t[0:8].astype(I32)
        carry = carry + jnp.sum(onehot, axis=0, keepdims=True)
    carry_s[...] = carry


def _plan_call(ridx, cnt):
    m = ridx.shape[0]
    _, n_tiles = _slot_counts(m)
    assert n_tiles <= PLAN_TILE
    return pl.pallas_call(
        _plan_kernel,
        grid=(m // PLAN_STEP,),
        in_specs=[pl.BlockSpec((PLAN_STEP, LANES), lambda i: (i, 0)),
                  pl.BlockSpec((8, LANES), lambda i: (0, 0))],
        out_specs=[
            pl.BlockSpec((8, PLAN_STEP), lambda i: (0, i)),
            pl.BlockSpec((PLAN_TILE, LANES), lambda i: (0, 0)),
        ],
        out_shape=[
            jax.ShapeDtypeStruct((8, m), I32),
            jax.ShapeDtypeStruct((PLAN_TILE, LANES), I32),
        ],
        scratch_shapes=[pltpu.VMEM((1, LANES), F32), pltpu.VMEM((1, LANES), F32)],
        compiler_params=pltpu.CompilerParams(dimension_semantics=("arbitrary",)),
        name="dispatch_plan",
    )(ridx, cnt)


def _sc_mesh():
    return plsc.VectorSubcoreMesh(core_axis_name="c", subcore_axis_name="s")


def _sc_worker():
    return lax.axis_index("s") * SC_CORES + lax.axis_index("c")


def _sc_dispatch(xn, slot1, slot2, n_rows):
    m = xn.shape[0]
    per = m // SC_WORKERS
    n_pairs = per // (2 * SC_ROWS)

    @functools.partial(
        pl.kernel, mesh=_sc_mesh(),
        out_type=jax.ShapeDtypeStruct((n_rows, HALF), I32),
        scratch_types=[pltpu.VMEM((per // SC_ROWS, SC_ROWS), I32), pltpu.VMEM((per // SC_ROWS, SC_ROWS), I32),
                       pltpu.VMEM((SC_ROWS, HALF), I32), pltpu.VMEM((SC_ROWS, HALF), I32),
                       pltpu.SemaphoreType.DMA, pltpu.SemaphoreType.DMA, pltpu.SemaphoreType.DMA],
        name="sc_dispatch")
    def k(x_hbm, s1_hbm, s2_hbm, o_hbm, i1_v, i2_v, rows0, rows1, sem_r0, sem_r1, sem_w):
        wid = _sc_worker()
        base = wid * per

        def read(chunk, rows_v, sem):
            return pltpu.make_async_copy(x_hbm.at[pl.ds(base + chunk * SC_ROWS, SC_ROWS)], rows_v, sem)

        def scatter(chunk, rows_v):
            c1 = pltpu.async_copy(rows_v, o_hbm.at[i1_v.at[chunk]], sem_w)
            c2 = pltpu.async_copy(rows_v, o_hbm.at[i2_v.at[chunk]], sem_w)
            c1.wait()
            c2.wait()

        read(0, rows0, sem_r0).start()
        pltpu.sync_copy(s1_hbm.at[pl.ds(wid * (per // SC_ROWS), per // SC_ROWS)], i1_v)
        pltpu.sync_copy(s2_hbm.at[pl.ds(wid * (per // SC_ROWS), per // SC_ROWS)], i2_v)

        @pl.loop(0, n_pairs)
        def _(i):
            read(2 * i, rows0, sem_r0).wait()
            read(2 * i + 1, rows1, sem_r1).start()
            scatter(2 * i, rows0)
            read(2 * i + 1, rows1, sem_r1).wait()

            @pl.when(i + 1 < n_pairs)
            def _():
                read(2 * i + 2, rows0, sem_r0).start()

            scatter(2 * i + 1, rows1)

    return k(xn, slot1, slot2)


def _sc_collect(ys, slot1, slot2):
    m = slot1.size
    per = m // SC_WORKERS
    row = jax.ShapeDtypeStruct((m, HALF), I32)

    @functools.partial(
        pl.kernel, mesh=_sc_mesh(), out_type=[row, row],
        scratch_types=[pltpu.VMEM((per // SC_ROWS, SC_ROWS), I32), pltpu.VMEM((per // SC_ROWS, SC_ROWS), I32),
                       pltpu.VMEM((SC_ROWS, HALF), I32), pltpu.VMEM((SC_ROWS, HALF), I32),
                       pltpu.SemaphoreType.DMA, pltpu.SemaphoreType.DMA],
        name="sc_collect")
    def k(y_hbm, s1_hbm, s2_hbm, g1_hbm, g2_hbm, i1_v, i2_v, rows1, rows2, sem_g, sem_w):
        wid = _sc_worker()
        base = wid * per
        pltpu.sync_copy(s1_hbm.at[pl.ds(wid * (per // SC_ROWS), per // SC_ROWS)], i1_v)
        pltpu.sync_copy(s2_hbm.at[pl.ds(wid * (per // SC_ROWS), per // SC_ROWS)], i2_v)

        @pl.loop(0, per // SC_ROWS)
        def _(ci):
            t0 = base + ci * SC_ROWS
            a1 = pltpu.async_copy(y_hbm.at[i1_v.at[ci]], rows1, sem_g)
            a2 = pltpu.async_copy(y_hbm.at[i2_v.at[ci]], rows2, sem_g)
            a1.wait()
            a2.wait()
            w1 = pltpu.async_copy(rows1, g1_hbm.at[pl.ds(t0, SC_ROWS)], sem_w)
            w2 = pltpu.async_copy(rows2, g2_hbm.at[pl.ds(t0, SC_ROWS)], sem_w)
            w1.wait()
            w2.wait()

    return k(ys, slot1, slot2)


def _expert_kernel(te_ref, used_ref, xs_ref, wg_hbm, wu_hbm, wd_hbm, ys_ref, wg_b, wu_b, wd_b,
                   wg_f, wu_f, wd_f, sem, slot_s):
    j = pl.program_id(0)
    n = pl.num_programs(0)
    e = te_ref[j]
    prev = te_ref[jnp.maximum(j - 1, 0)]
    valid = e < N_EXPERTS

    def weight_copies(expert, slot):
        return [pltpu.make_async_copy(hbm.at[expert], buf.at[slot], sem.at[slot, i])
                for i, (hbm, buf) in enumerate(((wg_hbm, wg_f), (wu_hbm, wu_f), (wd_hbm, wd_f)))]

    def run_tile():
        half = SLOT_TILE // 2
        rows = [pl.ds(i * half, half) for i in range(2)]
        row_id = lax.broadcasted_iota(I32, (half, HALF), 0)
        xs = [_unpack_bf16_pairs(jnp.where(row_id + i * half < used_ref[j], xs_ref[r, :], 0))
              .astype(BF16) for i, r in enumerate(rows)]
        gs = [_dot(x, wg_b[...]) for x in xs]
        us = [_dot(x, wu_b[...]) for x in xs]
        hids = [(_silu(g) * u).astype(BF16) for g, u in zip(gs, us)]
        ys = [_dot(hid, wd_b[...]) for hid in hids]
        for r, y in zip(rows, ys):
            ys_ref[r, :] = _pack_bf16_pairs(y)

    @pl.when((j == 0) & valid)
    def _():
        slot_s[0] = 0
        for c in weight_copies(e, 0):
            c.start()

    first_tile = ((j == 0) | (e != prev)) & valid

    @pl.when(first_tile)
    def _():
        slot = slot_s[0]
        for c in weight_copies(e, slot):
            c.wait()
        k = lax.while_loop(lambda k: (k < n) & (te_ref[jnp.minimum(k, n - 1)] == e),
                           lambda k: k + 1, j + 1)
        nxt = te_ref[jnp.minimum(k, n - 1)]

        @pl.when((k < n) & (nxt < N_EXPERTS))
        def _():
            for c in weight_copies(nxt, 1 - slot):
                c.start()

        wg_b[...] = wg_f[slot].astype(BF16)
        wu_b[...] = wu_f[slot].astype(BF16)
        wd_b[...] = wd_f[slot].astype(BF16)
        slot_s[0] = 1 - slot
        run_tile()

    @pl.when(valid & jnp.logical_not(first_tile))
    def _():
        run_tile()

    @pl.when(e >= N_EXPERTS)
    def _():
        ys_ref[...] = jnp.zeros_like(ys_ref)


def _expert_call(tile_expert, tile_used, xs, wg, wu, wd, n_tiles):
    hbm = pl.BlockSpec(memory_space=pl.ANY)
    return pl.pallas_call(
        _expert_kernel,
        grid_spec=pltpu.PrefetchScalarGridSpec(
            num_scalar_prefetch=2,
            grid=(n_tiles,),
            in_specs=[pl.BlockSpec((SLOT_TILE, HALF), lambda j, te, used: (j, 0)), hbm, hbm, hbm],
            out_specs=pl.BlockSpec((SLOT_TILE, HALF), lambda j, te, used: (j, 0)),
            scratch_shapes=[
                pltpu.VMEM((D_MODEL, D_EXPERT), BF16), pltpu.VMEM((D_MODEL, D_EXPERT), BF16),
                pltpu.VMEM((D_EXPERT, D_MODEL), BF16),
                pltpu.VMEM((2, D_MODEL, D_EXPERT), F32), pltpu.VMEM((2, D_MODEL, D_EXPERT), F32),
                pltpu.VMEM((2, D_EXPERT, D_MODEL), F32),
                pltpu.SemaphoreType.DMA((2, 3)),
                pltpu.SMEM((1,), I32),
            ],
        ),
        out_shape=jax.ShapeDtypeStruct((n_tiles * SLOT_TILE, HALF), I32),
        compiler_params=pltpu.CompilerParams(
            dimension_semantics=("arbitrary",), vmem_limit_bytes=VMEM_LIMIT),
        name="experts",
    )(tile_expert, tile_used, xs, wg, wu, wd)


def _final_kernel(h_ref, g1_ref, g2_ref, rw_ref, nw_ref, o_ref):
    rw = rw_ref[...]
    y = rw[:, 0:1] * _unpack_bf16_pairs(g1_ref[...]) + rw[:, 1:2] * _unpack_bf16_pairs(g2_ref[...])
    h = h_ref[...] + y
    o_ref[...] = h * lax.rsqrt(jnp.mean(h * h, axis=-1, keepdims=True) + EPS) * nw_ref[...]


def _final_call(h1, g1, g2, rw, nw, tm=1024):
    m = h1.shape[0]
    return pl.pallas_call(
        _final_kernel,
        grid=(m // tm,),
        in_specs=[
            pl.BlockSpec((tm, D_MODEL), lambda i: (i, 0)),
            pl.BlockSpec((tm, HALF), lambda i: (i, 0)),
            pl.BlockSpec((tm, HALF), lambda i: (i, 0)),
            pl.BlockSpec((tm, LANES), lambda i: (i, 0)),
            pl.BlockSpec((1, D_MODEL), lambda i: (0, 0)),
        ],
        out_specs=pl.BlockSpec((tm, D_MODEL), lambda i: (i, 0)),
        out_shape=jax.ShapeDtypeStruct((m, D_MODEL), F32),
        compiler_params=pltpu.CompilerParams(dimension_semantics=("arbitrary",)),
        name="combine_final",
    )(h1, g1, g2, rw, nw)


def _pad_lanes(a):
    return jnp.pad(a, ((0, 0), (0, LANES - a.shape[1])))


def kernel(x, norm_mix_w, w_in, conv_w, A_log, dt_bias, gdn_norm_w, w_up_gdn, w_up_ret, w_out,
           norm_ffn_w, w_group, b_group, w_expert, b_expert, w_gate, w_up, w_down, norm_final_w):
    batch, seq, d = x.shape
    m = batch * seq
    h = x.reshape(m, d)
    depth = w_in.shape[0]
    sin, cos, inner, kdec, qdec, cdec = _retention_tables(seq)
    for l in range(depth):
        w_main, w_ab = _repack_call(jnp.transpose(w_in[l]))
        proj, ab = _proj_call(h, norm_mix_w[l][None, :], w_main, w_ab)

        conv8 = jnp.pad(conv_w[l], ((0, 8 - GDN_CONV), (0, 0)))
        ya = _gdn_call(proj, ab, conv8, _pad_lanes(A_log[l][None, :]), _pad_lanes(dt_bias[l][None, :]),
                       gdn_norm_w[l][None, :], batch, seq)
        yb = _ret_call(proj, sin, cos, inner, kdec, qdec, cdec, batch, seq)

        w_router = _pad_lanes(jnp.concatenate([w_expert[l], w_group[l]], axis=1))
        wr_hi = w_router.astype(BF16)
        wr_lo = (w_router - wr_hi.astype(F32)).astype(BF16)
        b_router = _pad_lanes(jnp.concatenate([b_expert[l], b_group[l]])[None, :])
        h1, xn, ridx, rw, cnt = _merge_call(
            h, ya, yb, proj, w_up_gdn[l].astype(BF16), w_up_ret[l].astype(BF16), w_out[l].astype(BF16),
            norm_ffn_w[l][None, :], jnp.concatenate([wr_hi, wr_lo], axis=1), b_router)

        n_slots, n_tiles = _slot_counts(m)
        slots, tiles = _plan_call(ridx, cnt)
        slot1, slot2 = slots[0].reshape(-1, SC_ROWS), slots[1].reshape(-1, SC_ROWS)
        xs = _sc_dispatch(xn, slot1, slot2, n_slots)
        ys = _expert_call(tiles[:n_tiles, 0], tiles[:n_tiles, 1], xs,
                          w_gate[l], w_up[l], w_down[l], n_tiles)
        g1, g2 = _sc_collect(ys, slot1, slot2)

        assert depth == 1
        h = _final_call(h1, g1, g2, rw, norm_final_w[None, :])
    return h.reshape(batch, seq, d)
```
